```python
import jax, jax.numpy as jnp
from jax import lax
import numpy as np

D_MODEL = 1024
BATCH = 8
SEQ = 4096
DEPTH = 1

D_A = D_MODEL
H_A = 8
GROUP_A = D_A // H_A
CHUNK_A = 128
H_B = 4
KEY_B = D_MODEL // 2
VAL_B = D_MODEL
DK_B = KEY_B // H_B
DV_B = VAL_B // H_B
GATE_RANK = 16
GATE_NORM = 16.0
CHUNK_B = 64
EPS = 1e-6
LN_EPS = 1e-5

N_IN = 3 * D_A + 2 * KEY_B + 2 * VAL_B + GATE_RANK + 2 * D_MODEL
SPLIT_POINTS = (
    D_A,
    2 * D_A,
    3 * D_A,
    3 * D_A + KEY_B,
    3 * D_A + 2 * KEY_B,
    3 * D_A + 2 * KEY_B + VAL_B,
    3 * D_A + 2 * KEY_B + 2 * VAL_B,
    3 * D_A + 2 * KEY_B + 2 * VAL_B + GATE_RANK,
)

kernel_name = 'hybrid_gmlp_gla_gated_parallel'


def _rmsnorm(x, g):
    xf = x.astype(jnp.float32)
    y = xf * lax.rsqrt(jnp.mean(xf * xf, axis=-1, keepdims=True) + EPS)
    return (y * g.astype(jnp.float32)).astype(x.dtype)


def _layernorm(x, g, b):
    xf = x.astype(jnp.float32)
    mu = jnp.mean(xf, axis=-1, keepdims=True)
    xc = xf - mu
    y = xc * lax.rsqrt(jnp.mean(xc * xc, axis=-1, keepdims=True) + LN_EPS)
    return (y * g.astype(jnp.float32) + b.astype(jnp.float32)).astype(x.dtype)


def _spatial_gating(u, v, ln_g, ln_b, w_s, b_s):
    bsz, seq, _ = u.shape
    n_chunks = seq // CHUNK_A
    v = _layernorm(v, ln_g, ln_b)
    vc = v.reshape(bsz, n_chunks, CHUNK_A, H_A, GROUP_A)
    causal = jnp.tril(jnp.ones((CHUNK_A, CHUNK_A), dtype=bool))
    w = jnp.where(causal, w_s, jnp.zeros_like(w_s)).astype(v.dtype)
    mixed = jnp.einsum('hts,bnshc->bnthc', w, vc) + b_s.T.astype(v.dtype)[None, None, :, :, None]
    return u * mixed.reshape(bsz, seq, D_A)


def _gla_chunk_step(state, inp):
    q, k, v, g = inp
    b = jnp.cumsum(g, axis=1)
    b_last = b[:, -1]
    b_mid = b[:, CHUNK_B // 2 - 1][:, None]
    q_i = q * jnp.exp(b - b_mid)
    k_i = k * jnp.exp(b_mid - b)
    scores = jnp.einsum('bthk,bshk->bhts', q_i, k_i)
    causal = jnp.tril(jnp.ones((CHUNK_B, CHUNK_B), dtype=bool))
    scores = jnp.where(causal, scores, 0.0)
    o = (jnp.einsum('bhts,bshv->bthv', scores, v)
         + jnp.einsum('bthk,bhkv->bthv', q * jnp.exp(b), state))
    k_s = k * jnp.exp(b_last[:, None] - b)
    state = jnp.exp(b_last)[..., None] * state + jnp.einsum('bshk,bshv->bhkv', k_s, v)
    return state, o


def _gla(q, k, v, g):
    bsz, seq, _ = q.shape
    n_chunks = seq // CHUNK_B

    def to_chunks(t, d):
        t = t.astype(jnp.float32).reshape(bsz, n_chunks, CHUNK_B, H_B, d)
        return jnp.transpose(t, (1, 0, 2, 3, 4))

    qc = to_chunks(q, DK_B) * (DK_B ** -0.5)
    kc = to_chunks(k, DK_B)
    vc = to_chunks(v, DV_B)
    gc = to_chunks(g, DK_B)
    state0 = jnp.zeros((bsz, H_B, DK_B, DV_B), jnp.float32)
    _, o = lax.scan(_gla_chunk_step, state0, (qc, kc, vc, gc))
    o = jnp.transpose(o, (1, 0, 2, 3, 4)).reshape(bsz, seq, H_B, DV_B)
    return o


def _fwd_setup_inputs(seed: int = 0) -> dict:
    key = jax.random.key(seed)
    ks = jax.random.split(key, 16)
    f32 = jnp.float32
    nrm = lambda k, shape, s: jax.random.normal(k, shape, f32) * s
    return {
        'x': nrm(ks[0], (BATCH, SEQ, D_MODEL), 1.0),
        'norm_g': 1.0 + nrm(ks[1], (DEPTH, D_MODEL), 0.02),
        'w_in': nrm(ks[2], (DEPTH, D_MODEL, N_IN), D_MODEL ** -0.5),
        'ln_v_g': 1.0 + nrm(ks[3], (DEPTH, D_A), 0.02),
        'ln_v_b': nrm(ks[4], (DEPTH, D_A), 0.02),
        'w_spatial': nrm(ks[5], (DEPTH, H_A, CHUNK_A, CHUNK_A), 0.5 * CHUNK_A ** -0.5),
        'b_spatial': 1.0 + nrm(ks[6], (DEPTH, H_A, CHUNK_A), 0.02),
        'w_gate_up': nrm(ks[7], (DEPTH, GATE_RANK, KEY_B), GATE_RANK ** -0.5),
        'b_gate_up': nrm(ks[8], (DEPTH, KEY_B), 0.01),
        'gla_norm_g': 1.0 + nrm(ks[9], (DEPTH, DV_B), 0.02),
        'w_branch_a': nrm(ks[10], (DEPTH, D_A, D_MODEL), D_A ** -0.5),
        'w_branch_b': nrm(ks[11], (DEPTH, VAL_B, D_MODEL), VAL_B ** -0.5),
        'w_out': nrm(ks[12], (DEPTH, D_MODEL, D_MODEL), D_MODEL ** -0.5),
        'final_norm_g': 1.0 + nrm(ks[13], (D_MODEL,), 0.02),
    }


def _fwd_reference(x, norm_g, w_in, ln_v_g, ln_v_b, w_spatial, b_spatial, w_gate_up, b_gate_up,
              gla_norm_g, w_branch_a, w_branch_b, w_out, final_norm_g):
    bsz, seq, _ = x.shape
    for l in range(DEPTH):
        h = _rmsnorm(x, norm_g[l])
        proj = h @ w_in[l]
        u, v, z_a, q, k, v_b, z_b, lr, gates = jnp.split(proj, SPLIT_POINTS, axis=-1)
        a = _spatial_gating(jax.nn.gelu(u, approximate=False), jax.nn.gelu(v, approximate=False),
                            ln_v_g[l], ln_v_b[l], w_spatial[l], b_spatial[l])
        a = a * jax.nn.silu(z_a)
        logit = (lr @ w_gate_up[l] + b_gate_up[l]).astype(jnp.float32)
        log_alpha = jax.nn.log_sigmoid(logit) / GATE_NORM
        o = _gla(q, k, v_b, log_alpha)
        o = _rmsnorm(o, gla_norm_g[l]).astype(x.dtype).reshape(bsz, seq, VAL_B)
        o = o * jax.nn.silu(z_b)
        g_a, g_b = jnp.split(jax.nn.sigmoid(gates), 2, axis=-1)
        merged = g_a * (a @ w_branch_a[l]) + g_b * (o @ w_branch_b[l])
        x = x + merged @ w_out[l]
    return _rmsnorm(x, final_norm_g)


import jax as _jax
import jax.numpy as _jnp

TWIN_FORMAT = 'train_step'
FWD_PARAMS = ['x', 'norm_g', 'w_in', 'ln_v_g', 'ln_v_b', 'w_spatial', 'b_spatial', 'w_gate_up', 'b_gate_up', 'gla_norm_g', 'w_branch_a', 'w_branch_b', 'w_out', 'final_norm_g']
TWIN_WEIGHTS = ['norm_g', 'w_in', 'ln_v_g', 'ln_v_b', 'w_spatial', 'b_spatial', 'w_gate_up', 'b_gate_up', 'gla_norm_g', 'w_branch_a', 'w_branch_b', 'w_out', 'final_norm_g']
TWIN_DIFF_INPUT = 'x'
TWIN_INPUTS = ['x', 'norm_g', 'w_in', 'ln_v_g', 'ln_v_b', 'w_spatial', 'b_spatial', 'w_gate_up', 'b_gate_up', 'gla_norm_g', 'w_branch_a', 'w_branch_b', 'w_out', 'final_norm_g', 'loss_target', 'm_norm_g', 'm_w_in', 'm_ln_v_g', 'm_ln_v_b', 'm_w_spatial', 'm_b_spatial', 'm_w_gate_up', 'm_b_gate_up', 'm_gla_norm_g', 'm_w_branch_a', 'm_w_branch_b', 'm_w_out', 'm_final_norm_g', 'v_norm_g', 'v_w_in', 'v_ln_v_g', 'v_ln_v_b', 'v_w_spatial', 'v_b_spatial', 'v_w_gate_up', 'v_b_gate_up', 'v_gla_norm_g', 'v_w_branch_a', 'v_w_branch_b', 'v_w_out', 'v_final_norm_g']
TWIN_OUTPUTS = ['loss', 'grad_x', 'grad_norm_g', 'grad_w_in', 'grad_ln_v_g', 'grad_ln_v_b', 'grad_w_spatial', 'grad_b_spatial', 'grad_w_gate_up', 'grad_b_gate_up', 'grad_gla_norm_g', 'grad_w_branch_a', 'grad_w_branch_b', 'grad_w_out', 'grad_final_norm_g', 'delta_norm_g', 'delta_w_in', 'delta_ln_v_g', 'delta_ln_v_b', 'delta_w_spatial', 'delta_b_spatial', 'delta_w_gate_up', 'delta_b_gate_up', 'delta_gla_norm_g', 'delta_w_branch_a', 'delta_w_branch_b', 'delta_w_out', 'delta_final_norm_g', 'new_m_norm_g', 'new_m_w_in', 'new_m_ln_v_g', 'new_m_ln_v_b', 'new_m_w_spatial', 'new_m_b_spatial', 'new_m_w_gate_up', 'new_m_b_gate_up', 'new_m_gla_norm_g', 'new_m_w_branch_a', 'new_m_w_branch_b', 'new_m_w_out', 'new_m_final_norm_g', 'new_v_norm_g', 'new_v_w_in', 'new_v_ln_v_g', 'new_v_ln_v_b', 'new_v_w_spatial', 'new_v_b_spatial', 'new_v_w_gate_up', 'new_v_b_gate_up', 'new_v_gla_norm_g', 'new_v_w_branch_a', 'new_v_w_branch_b', 'new_v_w_out', 'new_v_final_norm_g']
TWIN_LEAF_KINDS = {'loss': 'loss', 'grad_x': 'grad_x', 'grad_norm_g': 'grad_w', 'grad_w_in': 'grad_w', 'grad_ln_v_g': 'grad_w', 'grad_ln_v_b': 'grad_w', 'grad_w_spatial': 'grad_w', 'grad_b_spatial': 'grad_w', 'grad_w_gate_up': 'grad_w', 'grad_b_gate_up': 'grad_w', 'grad_gla_norm_g': 'grad_w', 'grad_w_branch_a': 'grad_w', 'grad_w_branch_b': 'grad_w', 'grad_w_out': 'grad_w', 'grad_final_norm_g': 'grad_w', 'delta_norm_g': 'delta_w', 'delta_w_in': 'delta_w', 'delta_ln_v_g': 'delta_w', 'delta_ln_v_b': 'delta_w', 'delta_w_spatial': 'delta_w', 'delta_b_spatial': 'delta_w', 'delta_w_gate_up': 'delta_w', 'delta_b_gate_up': 'delta_w', 'delta_gla_norm_g': 'delta_w', 'delta_w_branch_a': 'delta_w', 'delta_w_branch_b': 'delta_w', 'delta_w_out': 'delta_w', 'delta_final_norm_g': 'delta_w', 'new_m_norm_g': 'new_m', 'new_m_w_in': 'new_m', 'new_m_ln_v_g': 'new_m', 'new_m_ln_v_b': 'new_m', 'new_m_w_spatial': 'new_m', 'new_m_b_spatial': 'new_m', 'new_m_w_gate_up': 'new_m', 'new_m_b_gate_up': 'new_m', 'new_m_gla_norm_g': 'new_m', 'new_m_w_branch_a': 'new_m', 'new_m_w_branch_b': 'new_m', 'new_m_w_out': 'new_m', 'new_m_final_norm_g': 'new_m', 'new_v_norm_g': 'new_v', 'new_v_w_in': 'new_v', 'new_v_ln_v_g': 'new_v', 'new_v_ln_v_b': 'new_v', 'new_v_w_spatial': 'new_v', 'new_v_b_spatial': 'new_v', 'new_v_w_gate_up': 'new_v', 'new_v_b_gate_up': 'new_v', 'new_v_gla_norm_g': 'new_v', 'new_v_w_branch_a': 'new_v', 'new_v_w_branch_b': 'new_v', 'new_v_w_out': 'new_v', 'new_v_final_norm_g': 'new_v'}


def _forward(args):
    return _fwd_reference(*[args[k] for k in FWD_PARAMS])


def _output_shape():
    out = _jax.eval_shape(lambda: _forward(_fwd_setup_inputs(0)))
    return out.shape, out.dtype

N_MICROBATCH = 1
ADAM_LR = 0.001
ADAM_B1 = 0.9
ADAM_B2 = 0.999
ADAM_EPS = 1e-08
ADAM_WD = 0.01
ADAM_STEP = 10
PER_EXAMPLE_BATCH_AXIS = {'x': 0, 'loss_target': 0}
SHARED_INPUTS = []
_WEIGHT_DTYPES = {'norm_g': _jnp.float32, 'w_in': _jnp.float32, 'ln_v_g': _jnp.float32, 'ln_v_b': _jnp.float32, 'w_spatial': _jnp.float32, 'b_spatial': _jnp.float32, 'w_gate_up': _jnp.float32, 'b_gate_up': _jnp.float32, 'gla_norm_g': _jnp.float32, 'w_branch_a': _jnp.float32, 'w_branch_b': _jnp.float32, 'w_out': _jnp.float32, 'final_norm_g': _jnp.float32}
MOMENT_SCALE = {'norm_g': 1.252625e-01, 'w_in': 4.489586e-02, 'ln_v_g': 1.240722e-02, 'ln_v_b': 1.146345e-02, 'w_spatial': 2.432138e-02, 'b_spatial': 3.521859e-02, 'w_gate_up': 9.178837e-03, 'b_gate_up': 3.908128e-02, 'gla_norm_g': 1.125551e-01, 'w_branch_a': 3.688520e-02, 'w_branch_b': 5.328761e-02, 'w_out': 6.494548e-02, 'final_norm_g': 3.199657e+01}


def _to_microbatches(a, axis):
    t = _jnp.moveaxis(a, axis, 0)
    t = t.reshape((N_MICROBATCH, t.shape[0] // N_MICROBATCH) + t.shape[1:])
    return _jnp.moveaxis(t, 1, axis + 1)


def setup_inputs(seed: int = 0) -> dict:
    inp = _fwd_setup_inputs(seed)
    key = _jax.random.fold_in(_jax.random.key(seed), 7919)
    shape, _ = _output_shape()
    out = dict(inp)
    out["loss_target"] = _jax.random.normal(_jax.random.fold_in(key, 0), shape, _jnp.float32)
    for i, name in enumerate(TWIN_WEIGHTS):
        w = inp[name].astype(_jnp.float32)
        if MOMENT_SCALE is None:
            s = _jnp.sqrt(_jnp.mean(_jnp.square(w)) + 1e-30)
        else:
            s = MOMENT_SCALE[name]
        km, kv = _jax.random.split(_jax.random.fold_in(key, i + 1))
        out[name] = w
        out["m_" + name] = s * _jax.random.normal(km, w.shape, _jnp.float32)
        out["v_" + name] = (s * s) * _jax.random.uniform(kv, w.shape, _jnp.float32, 0.5, 1.5)
    if N_MICROBATCH > 1:
        for name, axis in PER_EXAMPLE_BATCH_AXIS.items():
            out[name] = _to_microbatches(out[name], axis)
    return {'x': out['x'], 'norm_g': out['norm_g'], 'w_in': out['w_in'], 'ln_v_g': out['ln_v_g'], 'ln_v_b': out['ln_v_b'], 'w_spatial': out['w_spatial'], 'b_spatial': out['b_spatial'], 'w_gate_up': out['w_gate_up'], 'b_gate_up': out['b_gate_up'], 'gla_norm_g': out['gla_norm_g'], 'w_branch_a': out['w_branch_a'], 'w_branch_b': out['w_branch_b'], 'w_out': out['w_out'], 'final_norm_g': out['final_norm_g'], 'loss_target': out['loss_target'], 'm_norm_g': out['m_norm_g'], 'm_w_in': out['m_w_in'], 'm_ln_v_g': out['m_ln_v_g'], 'm_ln_v_b': out['m_ln_v_b'], 'm_w_spatial': out['m_w_spatial'], 'm_b_spatial': out['m_b_spatial'], 'm_w_gate_up': out['m_w_gate_up'], 'm_b_gate_up': out['m_b_gate_up'], 'm_gla_norm_g': out['m_gla_norm_g'], 'm_w_branch_a': out['m_w_branch_a'], 'm_w_branch_b': out['m_w_branch_b'], 'm_w_out': out['m_w_out'], 'm_final_norm_g': out['m_final_norm_g'], 'v_norm_g': out['v_norm_g'], 'v_w_in': out['v_w_in'], 'v_ln_v_g': out['v_ln_v_g'], 'v_ln_v_b': out['v_ln_v_b'], 'v_w_spatial': out['v_w_spatial'], 'v_b_spatial': out['v_b_spatial'], 'v_w_gate_up': out['v_w_gate_up'], 'v_b_gate_up': out['v_b_gate_up'], 'v_gla_norm_g': out['v_gla_norm_g'], 'v_w_branch_a': out['v_w_branch_a'], 'v_w_branch_b': out['v_w_branch_b'], 'v_w_out': out['v_w_out'], 'v_final_norm_g': out['v_final_norm_g']}


def _loss(weights, diff, rest, loss_target):
    with _jax.named_scope("forward"):
        args = {**rest, TWIN_DIFF_INPUT: diff, **{k: w.astype(_WEIGHT_DTYPES[k]) for k, w in weights.items()}}
        y = _forward(args)
    with _jax.named_scope("loss_head"):
        err = _jnp.square(y.astype(_jnp.float32) - loss_target)
        return 0.5 * _jnp.sum(_jnp.mean(err, axis=-1)) if err.ndim else 0.5 * err


def _adamw(w, g, m, v):
    m = ADAM_B1 * m + (1.0 - ADAM_B1) * g
    v = ADAM_B2 * v + (1.0 - ADAM_B2) * _jnp.square(g)
    m_hat = m / (1.0 - ADAM_B1 ** ADAM_STEP)
    v_hat = v / (1.0 - ADAM_B2 ** ADAM_STEP)
    delta = -ADAM_LR * (m_hat / (_jnp.sqrt(v_hat) + ADAM_EPS) + ADAM_WD * w)
    return delta, m, v


def reference(x, norm_g, w_in, ln_v_g, ln_v_b, w_spatial, b_spatial, w_gate_up, b_gate_up, gla_norm_g, w_branch_a, w_branch_b, w_out, final_norm_g, loss_target, m_norm_g, m_w_in, m_ln_v_g, m_ln_v_b, m_w_spatial, m_b_spatial, m_w_gate_up, m_b_gate_up, m_gla_norm_g, m_w_branch_a, m_w_branch_b, m_w_out, m_final_norm_g, v_norm_g, v_w_in, v_ln_v_g, v_ln_v_b, v_w_spatial, v_b_spatial, v_w_gate_up, v_b_gate_up, v_gla_norm_g, v_w_branch_a, v_w_branch_b, v_w_out, v_final_norm_g):
    given = dict(x=x, norm_g=norm_g, w_in=w_in, ln_v_g=ln_v_g, ln_v_b=ln_v_b, w_spatial=w_spatial, b_spatial=b_spatial, w_gate_up=w_gate_up, b_gate_up=b_gate_up, gla_norm_g=gla_norm_g, w_branch_a=w_branch_a, w_branch_b=w_branch_b, w_out=w_out, final_norm_g=final_norm_g, loss_target=loss_target, m_norm_g=m_norm_g, m_w_in=m_w_in, m_ln_v_g=m_ln_v_g, m_ln_v_b=m_ln_v_b, m_w_spatial=m_w_spatial, m_b_spatial=m_b_spatial, m_w_gate_up=m_w_gate_up, m_b_gate_up=m_b_gate_up, m_gla_norm_g=m_gla_norm_g, m_w_branch_a=m_w_branch_a, m_w_branch_b=m_w_branch_b, m_w_out=m_w_out, m_final_norm_g=m_final_norm_g, v_norm_g=v_norm_g, v_w_in=v_w_in, v_ln_v_g=v_ln_v_g, v_ln_v_b=v_ln_v_b, v_w_spatial=v_w_spatial, v_b_spatial=v_b_spatial, v_w_gate_up=v_w_gate_up, v_b_gate_up=v_b_gate_up, v_gla_norm_g=v_gla_norm_g, v_w_branch_a=v_w_branch_a, v_w_branch_b=v_w_branch_b, v_w_out=v_w_out, v_final_norm_g=v_final_norm_g)
    weights = {n: given[n] for n in TWIN_WEIGHTS}
    shared = {n: given[n] for n in SHARED_INPUTS}
    per_example = {n: given[n] for n in ['x']}
    grad_fn = _jax.value_and_grad(_loss, argnums=(0, 1))

    def one_microbatch(ex, loss_target):
        ex = dict(ex)
        diff = ex.pop(TWIN_DIFF_INPUT)
        return grad_fn(weights, diff, {**shared, **ex}, loss_target)

    if N_MICROBATCH == 1:
        loss, (grad_w, grad_x) = one_microbatch(per_example, given["loss_target"])
    else:
        def body(carry, xs):
            loss_sum, grad_sum = carry
            l_k, (gw_k, gx_k) = one_microbatch(xs[0], xs[1])
            with _jax.named_scope("update"):
                return (loss_sum + l_k, _jax.tree.map(_jnp.add, grad_sum, gw_k)), gx_k

        init = (_jnp.zeros((), _jnp.float32), _jax.tree.map(_jnp.zeros_like, weights))
        (loss, grad_w), grad_x = _jax.lax.scan(body, init, (per_example, given["loss_target"]))
    with _jax.named_scope("update"):
        delta_w, new_m, new_v = {}, {}, {}
        for n in TWIN_WEIGHTS:
            delta_w[n], new_m[n], new_v[n] = _adamw(weights[n], grad_w[n], given["m_" + n], given["v_" + n])
    return (loss, grad_x, *[grad_w[n] for n in TWIN_WEIGHTS], *[delta_w[n] for n in TWIN_WEIGHTS],
            *[new_m[n] for n in TWIN_WEIGHTS], *[new_v[n] for n in TWIN_WEIGHTS])
```

```python
import functools
import math

import jax
import jax.numpy as jnp
from jax import lax
from jax.experimental import pallas as pl
from jax.experimental.pallas import tpu as pltpu

f32 = jnp.float32
bf16 = jnp.bfloat16

D = 1024
NMAIN = 8192
LRP = 128
RANK = 16
HA, GA, CA = 8, 128, 128
HB, DK, DV, CB = 4, 128, 256, 64
KEYB = HB * DK
EPS = 1e-6
LN_EPS = 1e-5
GATE_NORM = 16.0
QSCALE = DK ** -0.5
COL_U, COL_V, COL_ZA = 0, 1, 2
COL_Q, COL_K = 6, 7
COL_VB, COL_ZB = 4, 5
COL_GATES = 3
VMEM_LIMIT = 56 * 1024 * 1024

ADAM_LR, ADAM_B1, ADAM_B2, ADAM_EPS, ADAM_WD, ADAM_STEP = 0.001, 0.9, 0.999, 1e-08, 0.01, 10

_SQRT_HALF = 0.7071067811865476
_INV_SQRT_2PI = 0.3989422804014327


def _dot(a, b):
    return jnp.dot(a, b, preferred_element_type=f32)


def _dot_nt(a, b):
    return lax.dot_general(a, b, (((1,), (1,)), ((), ())), preferred_element_type=f32)


def _dot_tn(a, b):
    return lax.dot_general(a, b, (((0,), (0,)), ((), ())), preferred_element_type=f32)


def _dot_exact(a, b):
    return jnp.dot(a, b, preferred_element_type=f32, precision=lax.Precision.HIGHEST)


def _gelu(x):
    return 0.5 * x * (1.0 + lax.erf(x * _SQRT_HALF))


def _gelu_grad(x):
    return 0.5 * (1.0 + lax.erf(x * _SQRT_HALF)) + x * (jnp.exp(-0.5 * x * x) * _INV_SQRT_2PI)


def _sigmoid(x):
    return 1.0 / (1.0 + jnp.exp(-x))


def _params(sem):
    return pltpu.CompilerParams(dimension_semantics=sem, vmem_limit_bytes=VMEM_LIMIT)


def _resident(shape):
    nd = len(shape)
    return pl.BlockSpec(shape, lambda *_: (0,) * nd, pipeline_mode=pl.Buffered(1))


def _proj_fwd(x, g0, w_main, w_lr, tm=1024, tn=512):
    T = x.shape[0]
    tm = min(tm, T)

    def body(x_ref, g_ref, w_ref, wl_ref, proj_ref, lr_ref, h_ref):
        @pl.when(pl.program_id(1) == 0)
        def _():
            xv = x_ref[...]
            r = lax.rsqrt(jnp.mean(xv * xv, axis=-1, keepdims=True) + EPS)
            h = (xv * r * g_ref[...]).astype(bf16)
            h_ref[...] = h
            lr_ref[...] = _dot(h, wl_ref[...])

        proj_ref[...] = _dot(h_ref[...], w_ref[...])

    return pl.pallas_call(
        body,
        grid=(T // tm, NMAIN // tn),
        in_specs=[
            pl.BlockSpec((tm, D), lambda i, j: (i, 0)),
            pl.BlockSpec((1, D), lambda i, j: (0, 0)),
            pl.BlockSpec((D, tn), lambda i, j: (0, j)),
            pl.BlockSpec((D, LRP), lambda i, j: (0, 0)),
        ],
        out_specs=[
            pl.BlockSpec((tm, tn), lambda i, j: (i, j)),
            pl.BlockSpec((tm, LRP), lambda i, j: (i, 0)),
            pl.BlockSpec((tm, D), lambda i, j: (i, 0)),
        ],
        out_shape=[
            jax.ShapeDtypeStruct((T, NMAIN), f32),
            jax.ShapeDtypeStruct((T, LRP), f32),
            jax.ShapeDtypeStruct((T, D), bf16),
        ],
        compiler_params=_params(("parallel", "arbitrary")),
        name="proj_fwd",
    )(x, g0, w_main, w_lr)


def _causal_mask():
    t = lax.broadcasted_iota(jnp.int32, (CA, CA), 0)
    s = lax.broadcasted_iota(jnp.int32, (CA, CA), 1)
    return s <= t


def _layernorm_parts(gv):
    mu = jnp.mean(gv, axis=-1, keepdims=True)
    xc = gv - mu
    rs = lax.rsqrt(jnp.mean(xc * xc, axis=-1, keepdims=True) + LN_EPS)
    return xc * rs, rs


def _mixer_a_fwd(proj, ln_g, ln_b, w_s, b_sb, tm=256):
    T = proj.shape[0]

    def body(u_ref, v_ref, za_ref, lg_ref, lb_ref, ws_ref, bs_ref, a_ref, vln_s):
        vhat, _ = _layernorm_parts(_gelu(v_ref[...]))
        vln_s[...] = (vhat * lg_ref[...] + lb_ref[...]).astype(bf16)
        mask = _causal_mask()
        for g in range(HA):
            wg = jnp.where(mask, ws_ref[g], 0.0).astype(bf16)
            cols = slice(g * GA, (g + 1) * GA)
            for c in range(tm // CA):
                rows = slice(c * CA, (c + 1) * CA)
                mixed = _dot(wg, vln_s[rows, cols]) + bs_ref[g]
                za = za_ref[rows, cols]
                a = _gelu(u_ref[rows, cols]) * mixed * (za * _sigmoid(za))
                a_ref[rows, cols] = a.astype(bf16)

    def col(cidx):
        return pl.BlockSpec((tm, D), lambda i, c=cidx: (i, c))

    return pl.pallas_call(
        body,
        grid=(T // tm,),
        in_specs=[col(COL_U), col(COL_V), col(COL_ZA), _resident((1, D)), _resident((1, D)),
                  _resident((HA, CA, CA)), _resident((HA, CA, GA))],
        out_specs=pl.BlockSpec((tm, D), lambda i: (i, 0)),
        out_shape=jax.ShapeDtypeStruct((T, D), bf16),
        scratch_shapes=[pltpu.VMEM((tm, D), bf16)],
        compiler_params=_params(("parallel",)),
        name="mixer_a_fwd",
    )(proj, proj, proj, ln_g, ln_b, w_s, b_sb)


def _mixer_a_bwd(proj, da, dproj, ln_g, ln_b, w_s, b_sb, tm=256):
    T = proj.shape[0]
    nsteps = T // tm

    def body(u_ref, v_ref, za_ref, da_ref, dp_in, lg_ref, lb_ref, ws_ref, bs_ref,
             dp_ref, dws_ref, dbs_ref, dlg_ref, dlb_ref, vln_s, dvln_s):
        del dp_in
        i = pl.program_id(0)

        @pl.when(i == 0)
        def _():
            dws_ref[...] = jnp.zeros_like(dws_ref)
            dbs_ref[...] = jnp.zeros_like(dbs_ref)
            dlg_ref[...] = jnp.zeros_like(dlg_ref)
            dlb_ref[...] = jnp.zeros_like(dlb_ref)

        v = v_ref[...]
        vhat, rs = _layernorm_parts(_gelu(v))
        vln_s[...] = (vhat * lg_ref[...] + lb_ref[...]).astype(bf16)
        mask = _causal_mask()
        for g in range(HA):
            wg = jnp.where(mask, ws_ref[g], 0.0).astype(bf16)
            cols = slice(g * GA, (g + 1) * GA)
            dw_acc = jnp.zeros((CA, CA), f32)
            db_acc = jnp.zeros((CA, 1), f32)
            for c in range(tm // CA):
                rows = slice(c * CA, (c + 1) * CA)
                vln = vln_s[rows, cols]
                mixed = _dot(wg, vln) + bs_ref[g]
                u = u_ref[rows, cols]
                za = za_ref[rows, cols]
                da_blk = da_ref[rows, cols]
                sg = _sigmoid(za)
                sz = za * sg
                gu = _gelu(u)
                dp_ref[rows, cols] = (da_blk * mixed * sz * _gelu_grad(u)).astype(bf16)
                dp_ref[rows, 2 * D + g * GA:2 * D + (g + 1) * GA] = (
                    da_blk * gu * mixed * (sg * (1.0 + za * (1.0 - sg)))).astype(bf16)
                dmixed = da_blk * gu * sz
                dmb = dmixed.astype(bf16)
                dvln_s[rows, cols] = _dot_tn(wg, dmb)
                dw_acc = dw_acc + _dot_nt(dmb, vln)
                db_acc = db_acc + jnp.sum(dmixed, axis=-1, keepdims=True)
            dws_ref[g] += dw_acc
            dbs_ref[g] += jnp.broadcast_to(db_acc, (CA, GA))

        dvln = dvln_s[...]
        dlg_ref[...] += jnp.sum(dvln * vhat, axis=0, keepdims=True)
        dlb_ref[...] += jnp.sum(dvln, axis=0, keepdims=True)
        dvhat = dvln * lg_ref[...]
        dgv = rs * (dvhat - jnp.mean(dvhat, axis=-1, keepdims=True)
                    - vhat * jnp.mean(dvhat * vhat, axis=-1, keepdims=True))
        dp_ref[:, D:2 * D] = (dgv * _gelu_grad(v)).astype(bf16)

        @pl.when(i == nsteps - 1)
        def _():
            for g in range(HA):
                dws_ref[g] = jnp.where(mask, dws_ref[g], 0.0)

    def col(cidx):
        return pl.BlockSpec((tm, D), lambda i, c=cidx: (i, c))

    return pl.pallas_call(
        body,
        grid=(nsteps,),
        in_specs=[col(COL_U), col(COL_V), col(COL_ZA), pl.BlockSpec((tm, D), lambda i: (i, 0)),
                  pl.BlockSpec(memory_space=pl.ANY),
                  _resident((1, D)), _resident((1, D)), _resident((HA, CA, CA)), _resident((HA, CA, GA))],
        out_specs=[pl.BlockSpec((tm, 3 * D), lambda i: (i, 0)),
                   _resident((HA, CA, CA)), _resident((HA, CA, GA)), _resident((1, D)), _resident((1, D))],
        out_shape=[jax.ShapeDtypeStruct(dproj.shape, dproj.dtype),
                   jax.ShapeDtypeStruct((HA, CA, CA), f32), jax.ShapeDtypeStruct((HA, CA, GA), f32),
                   jax.ShapeDtypeStruct((1, D), f32), jax.ShapeDtypeStruct((1, D), f32)],
        scratch_shapes=[pltpu.VMEM((tm, D), bf16), pltpu.VMEM((tm, D), f32)],
        input_output_aliases={4: 0},
        compiler_params=_params(("arbitrary",)),
        name="mixer_a_bwd",
    )(proj, proj, proj, da, dproj, ln_g, ln_b, w_s, b_sb)


def _tri(n, upper):
    r = lax.broadcasted_iota(jnp.int32, (n, n), 0)
    c = lax.broadcasted_iota(jnp.int32, (n, n), 1)
    return jnp.where((c >= r) if upper else (c <= r), 1.0, 0.0).astype(f32)


def _log_alpha(lr, wg, bg):
    logit = _dot(lr.astype(bf16), wg.astype(bf16)) + bg
    la = (jnp.minimum(logit, 0.0) - jnp.log1p(jnp.exp(-jnp.abs(logit)))) * (1.0 / GATE_NORM)
    return logit, la


def _gla_fwd(proj, lr, w_gate, b_gate, gla_g, tm=256):
    T = proj.shape[0]
    nchunk = T // CB
    cpb = tm // CB

    def body(q_ref, k_ref, v_ref, zb_ref, lr_ref, wg_ref, bg_ref, gg_ref,
             o_ref, ob_ref, st_ref, state, la_s):
        @pl.when(pl.program_id(0) == 0)
        def _():
            state[...] = jnp.zeros_like(state)

        _, la = _log_alpha(lr_ref[...], wg_ref[...], bg_ref[...])
        la_s[...] = la
        ltri = _tri(CB, upper=False)
        causal = ltri > 0.5
        for c in range(cpb):
            rows = slice(c * CB, (c + 1) * CB)
            for hd in range(HB):
                kc = slice(hd * DK, (hd + 1) * DK)
                vc = slice(hd * DV, (hd + 1) * DV)
                b = _dot_exact(ltri, la_s[rows, kc])
                bl = b[CB - 1:CB, :]
                bm = b[CB // 2 - 1:CB // 2, :]
                q = q_ref[rows, kc] * QSCALE
                k = k_ref[rows, kc]
                v = v_ref[rows, vc].astype(bf16)
                qi = (q * jnp.exp(b - bm)).astype(bf16)
                ki = (k * jnp.exp(bm - b)).astype(bf16)
                qe = (q * jnp.exp(b)).astype(bf16)
                ks = (k * jnp.exp(bl - b)).astype(bf16)
                p = jnp.where(causal, _dot_nt(qi, ki), 0.0).astype(bf16)
                s0 = state[hd]
                st_ref[c, hd] = s0
                o = _dot(p, v) + _dot_nt(qe, s0.astype(bf16))
                state[hd] = s0 * jnp.exp(bl) + _dot_tn(v, ks)
                o_ref[rows, vc] = o
                ro = lax.rsqrt(jnp.mean(o * o, axis=-1, keepdims=True) + EPS)
                zb = zb_ref[rows, vc]
                ob_ref[rows, vc] = (o * ro * gg_ref[...] * (zb * _sigmoid(zb))).astype(bf16)

    return pl.pallas_call(
        body,
        grid=(T // tm,),
        in_specs=[pl.BlockSpec((tm, KEYB), lambda i: (i, COL_Q)),
                  pl.BlockSpec((tm, KEYB), lambda i: (i, COL_K)),
                  pl.BlockSpec((tm, D), lambda i: (i, COL_VB)),
                  pl.BlockSpec((tm, D), lambda i: (i, COL_ZB)),
                  pl.BlockSpec((tm, LRP), lambda i: (i, 0)),
                  _resident((LRP, KEYB)), _resident((1, KEYB)), _resident((1, DV))],
        out_specs=[pl.BlockSpec((tm, D), lambda i: (i, 0)),
                   pl.BlockSpec((tm, D), lambda i: (i, 0)),
                   pl.BlockSpec((cpb, HB, DV, DK), lambda i: (i, 0, 0, 0))],
        out_shape=[jax.ShapeDtypeStruct((T, D), f32), jax.ShapeDtypeStruct((T, D), bf16),
                   jax.ShapeDtypeStruct((nchunk, HB, DV, DK), f32)],
        scratch_shapes=[pltpu.VMEM((HB, DV, DK), f32), pltpu.VMEM((tm, KEYB), f32)],
        compiler_params=_params(("arbitrary",)),
        name="gla_fwd",
    )(proj, proj, proj, proj, lr, w_gate, b_gate, gla_g)


def _gla_bwd(proj, lr, o, states, dob, dproj, w_gate, b_gate, gla_g, tm=256):
    T = proj.shape[0]
    cpb = tm // CB
    nb = T // tm

    def body(q_ref, k_ref, v_ref, zb_ref, lr_ref, o_ref, st_ref, dob_ref, dp_in, wg_ref, bg_ref, gg_ref,
             dp_ref, dlr_ref, dwg_ref, dbg_ref, dgg_ref, dstate, la_s, dlogit_s):
        del dp_in
        step = pl.program_id(0)

        @pl.when(step == 0)
        def _():
            dstate[...] = jnp.zeros_like(dstate)
            dwg_ref[...] = jnp.zeros_like(dwg_ref)
            dbg_ref[...] = jnp.zeros_like(dbg_ref)
            dgg_ref[...] = jnp.zeros_like(dgg_ref)

        lr_v = lr_ref[...]
        logit, la = _log_alpha(lr_v, wg_ref[...], bg_ref[...])
        la_s[...] = la
        ltri = _tri(CB, upper=False)
        utri = _tri(CB, upper=True)
        causal = ltri > 0.5
        gg = gg_ref[...]
        dgg_acc = jnp.zeros((1, DV), f32)
        for c in reversed(range(cpb)):
            rows = slice(c * CB, (c + 1) * CB)
            for hd in range(HB):
                kc = slice(hd * DK, (hd + 1) * DK)
                vc = slice(hd * DV, (hd + 1) * DV)
                o_h = o_ref[rows, vc]
                ro = lax.rsqrt(jnp.mean(o_h * o_h, axis=-1, keepdims=True) + EPS)
                ohat = o_h * ro
                zb = zb_ref[rows, vc]
                sg = _sigmoid(zb)
                dob_h = dob_ref[rows, vc]
                don = dob_h * (zb * sg)
                dp_ref[rows, 2 * D + hd * DV:2 * D + (hd + 1) * DV] = (
                    dob_h * ohat * gg * (sg * (1.0 + zb * (1.0 - sg)))).astype(bf16)
                dgg_acc = dgg_acc + jnp.sum(don * ohat, axis=0, keepdims=True)
                dohat = don * gg
                do = (ro * (dohat - ohat * jnp.mean(dohat * ohat, axis=-1, keepdims=True))).astype(bf16)
                b = _dot_exact(ltri, la_s[rows, kc])
                bl = b[CB - 1:CB, :]
                bm = b[CB // 2 - 1:CB // 2, :]
                e_b, e_qm, e_km, e_ks, e_l = jnp.exp(b), jnp.exp(b - bm), jnp.exp(bm - b), jnp.exp(bl - b), jnp.exp(bl)
                q = q_ref[rows, kc] * QSCALE
                k = k_ref[rows, kc]
                v = v_ref[rows, vc].astype(bf16)
                qi = (q * e_qm).astype(bf16)
                ki = (k * e_km).astype(bf16)
                qe = (q * e_b).astype(bf16)
                ks_f = k * e_ks
                ks = ks_f.astype(bf16)
                p = jnp.where(causal, _dot_nt(qi, ki), 0.0).astype(bf16)
                s0 = st_ref[c, hd]
                ds = dstate[hd]
                ds_b = ds.astype(bf16)
                dv = _dot_tn(p, do) + _dot_nt(ks, ds_b)
                dpm = jnp.where(causal, _dot_nt(do, v), 0.0).astype(bf16)
                dqi = _dot(dpm, ki)
                dki = _dot_tn(dpm, qi)
                dqe = _dot(do, s0.astype(bf16))
                dks = _dot(v, ds_b)
                dq_s = dqi * e_qm + dqe * e_b
                dk = dki * e_km + dks * e_ks
                tail = (jnp.sum(dks * ks_f, axis=0, keepdims=True)
                        + e_l * jnp.sum(ds * s0, axis=0, keepdims=True))
                dg = _dot_exact(utri, dq_s * q - dk * k) + tail
                dstate[hd] = _dot_tn(do, qe) + ds * e_l
                dp_ref[rows, kc] = (dq_s * QSCALE).astype(bf16)
                dp_ref[rows, KEYB + hd * DK:KEYB + (hd + 1) * DK] = dk.astype(bf16)
                dp_ref[rows, D + hd * DV:D + (hd + 1) * DV] = dv.astype(bf16)
                dlogit_s[rows, kc] = dg * (1.0 / GATE_NORM)
        dgg_ref[...] += dgg_acc
        dlogit = dlogit_s[...] * _sigmoid(-logit)
        dbg_ref[...] += jnp.sum(dlogit, axis=0, keepdims=True)
        dlb = dlogit.astype(bf16)
        dlr_ref[...] = _dot_nt(dlb, wg_ref[...].astype(bf16)).astype(bf16)
        dwg_ref[...] += _dot_tn(lr_v.astype(bf16), dlb)

    def rev(cidx):
        return lambda i, c=cidx: (nb - 1 - i, c)

    return pl.pallas_call(
        body,
        grid=(nb,),
        in_specs=[pl.BlockSpec((tm, KEYB), rev(COL_Q)),
                  pl.BlockSpec((tm, KEYB), rev(COL_K)),
                  pl.BlockSpec((tm, D), rev(COL_VB)),
                  pl.BlockSpec((tm, D), rev(COL_ZB)),
                  pl.BlockSpec((tm, LRP), rev(0)),
                  pl.BlockSpec((tm, D), rev(0)),
                  pl.BlockSpec((cpb, HB, DV, DK), lambda i: (nb - 1 - i, 0, 0, 0)),
                  pl.BlockSpec((tm, D), rev(0)),
                  pl.BlockSpec(memory_space=pl.ANY),
                  _resident((LRP, KEYB)), _resident((1, KEYB)), _resident((1, DV))],
        out_specs=[pl.BlockSpec((tm, 3 * D), rev(1)),
                   pl.BlockSpec((tm, LRP), rev(0)),
                   _resident((LRP, KEYB)), _resident((1, KEYB)), _resident((1, DV))],
        out_shape=[jax.ShapeDtypeStruct(dproj.shape, dproj.dtype),
                   jax.ShapeDtypeStruct((T, LRP), bf16),
                   jax.ShapeDtypeStruct((LRP, KEYB), f32), jax.ShapeDtypeStruct((1, KEYB), f32),
                   jax.ShapeDtypeStruct((1, DV), f32)],
        scratch_shapes=[pltpu.VMEM((HB, DV, DK), f32), pltpu.VMEM((tm, KEYB), f32), pltpu.VMEM((tm, KEYB), f32)],
        input_output_aliases={8: 0},
        compiler_params=_params(("arbitrary",)),
        name="gla_bwd",
    )(proj, proj, proj, proj, lr, o, states, dob, dproj, w_gate, b_gate, gla_g)


def _merge_fwd_bwd(x, tgt, proj, a, ob, w_a, w_b, w_o, g_f, tm=256):
    T = x.shape[0]

    def body(x_ref, t_ref, gt_ref, a_ref, ob_ref, wa_ref, wb_ref, wo_ref, gf_ref,
             dp_ref, dy_ref, da_ref, dob_ref, dwa_ref, dwb_ref, dwo_ref, dgf_ref, loss_ref):
        @pl.when(pl.program_id(0) == 0)
        def _():
            dwa_ref[...] = jnp.zeros_like(dwa_ref)
            dwb_ref[...] = jnp.zeros_like(dwb_ref)
            dwo_ref[...] = jnp.zeros_like(dwo_ref)
            dgf_ref[...] = jnp.zeros_like(dgf_ref)
            loss_ref[...] = jnp.zeros_like(loss_ref)

        ga = _sigmoid(gt_ref[:, :D])
        gb = _sigmoid(gt_ref[:, D:])
        a_v = a_ref[...]
        ob_v = ob_ref[...]
        pa = _dot(a_v, wa_ref[...])
        pb = _dot(ob_v, wb_ref[...])
        mb = (ga * pa + gb * pb).astype(bf16)
        y = x_ref[...] + _dot(mb, wo_ref[...])
        r1 = lax.rsqrt(jnp.mean(y * y, axis=-1, keepdims=True) + EPS)
        yhat = y * r1
        gf = gf_ref[...]
        err = yhat * gf - t_ref[...]
        loss_ref[...] += jnp.sum(err * err, axis=0, keepdims=True) * (0.5 / D)
        dout = err * (1.0 / D)
        dgf_ref[...] += jnp.sum(dout * yhat, axis=0, keepdims=True)
        dyn = dout * gf
        dy = r1 * (dyn - yhat * jnp.mean(dyn * yhat, axis=-1, keepdims=True))
        dy_ref[...] = dy
        dyb = dy.astype(bf16)
        dwo_ref[...] += _dot_tn(mb, dyb)
        dm = _dot_nt(dyb, wo_ref[...])
        dpa = (dm * ga).astype(bf16)
        dpb = (dm * gb).astype(bf16)
        dp_ref[:, :D] = (dm * pa * ga * (1.0 - ga)).astype(bf16)
        dp_ref[:, D:] = (dm * pb * gb * (1.0 - gb)).astype(bf16)
        dwa_ref[...] += _dot_tn(a_v, dpa)
        dwb_ref[...] += _dot_tn(ob_v, dpb)
        da_ref[...] = _dot_nt(dpa, wa_ref[...])
        dob_ref[...] = _dot_nt(dpb, wb_ref[...])

    row = lambda: pl.BlockSpec((tm, D), lambda i: (i, 0))
    return pl.pallas_call(
        body,
        grid=(T // tm,),
        in_specs=[row(), row(), pl.BlockSpec((tm, 2 * D), lambda i: (i, COL_GATES)), row(), row(),
                  _resident((D, D)), _resident((D, D)), _resident((D, D)), _resident((1, D))],
        out_specs=[pl.BlockSpec((tm, 2 * D), lambda i: (i, COL_GATES)), row(), row(), row(),
                   _resident((D, D)), _resident((D, D)), _resident((D, D)), _resident((1, D)), _resident((1, D))],
        out_shape=[jax.ShapeDtypeStruct((T, NMAIN), bf16),
                   jax.ShapeDtypeStruct((T, D), f32), jax.ShapeDtypeStruct((T, D), f32),
                   jax.ShapeDtypeStruct((T, D), f32),
                   jax.ShapeDtypeStruct((D, D), f32), jax.ShapeDtypeStruct((D, D), f32),
                   jax.ShapeDtypeStruct((D, D), f32),
                   jax.ShapeDtypeStruct((1, D), f32), jax.ShapeDtypeStruct((1, D), f32)],
        compiler_params=_params(("arbitrary",)),
        name="merge_fwd_bwd",
    )(x, tgt, proj, a, ob, w_a, w_b, w_o, g_f)


def _dx_bwd(x, dy, dproj, dlr, g0, w_main, w_lr, tm=256):
    T = x.shape[0]

    def body(x_ref, dy_ref, dp_ref, dl_ref, g_ref, w_ref, wl_ref, dx_ref, dg_ref, dwl_ref):
        @pl.when(pl.program_id(0) == 0)
        def _():
            dg_ref[...] = jnp.zeros_like(dg_ref)
            dwl_ref[...] = jnp.zeros_like(dwl_ref)

        xv = x_ref[...]
        r = lax.rsqrt(jnp.mean(xv * xv, axis=-1, keepdims=True) + EPS)
        xhat = xv * r
        g = g_ref[...]
        dl = dl_ref[...]
        dh = _dot_nt(dp_ref[...], w_ref[...]) + _dot_nt(dl, wl_ref[...])
        dg_ref[...] += jnp.sum(dh * xhat, axis=0, keepdims=True)
        t = dh * g
        dx_ref[...] = dy_ref[...] + r * (t - xhat * jnp.mean(t * xhat, axis=-1, keepdims=True))
        dwl_ref[...] += _dot_tn((xhat * g).astype(bf16), dl)

    row = lambda: pl.BlockSpec((tm, D), lambda i: (i, 0))
    return pl.pallas_call(
        body,
        grid=(T // tm,),
        in_specs=[row(), row(), pl.BlockSpec((tm, NMAIN), lambda i: (i, 0)),
                  pl.BlockSpec((tm, LRP), lambda i: (i, 0)),
                  _resident((1, D)), _resident((D, NMAIN)), _resident((D, LRP))],
        out_specs=[row(), _resident((1, D)), _resident((D, LRP))],
        out_shape=[jax.ShapeDtypeStruct((T, D), f32), jax.ShapeDtypeStruct((1, D), f32),
                   jax.ShapeDtypeStruct((D, LRP), f32)],
        compiler_params=_params(("arbitrary",)),
        name="dx_bwd",
    )(x, dy, dproj, dlr, g0, w_main, w_lr)


def _dw_main(h, dproj, tm=512, tn=1024):
    T = h.shape[0]
    tm = min(tm, T)

    def body(h_ref, dp_ref, dw_ref):
        @pl.when(pl.program_id(1) == 0)
        def _():
            dw_ref[...] = jnp.zeros_like(dw_ref)

        dw_ref[...] += _dot_tn(h_ref[...], dp_ref[...])

    return pl.pallas_call(
        body,
        grid=(NMAIN // tn, T // tm),
        in_specs=[pl.BlockSpec((tm, D), lambda j, k: (k, 0)), pl.BlockSpec((tm, tn), lambda j, k: (k, j))],
        out_specs=pl.BlockSpec((D, tn), lambda j, k: (0, j)),
        out_shape=jax.ShapeDtypeStruct((D, NMAIN), f32),
        compiler_params=_params(("parallel", "arbitrary")),
        name="dw_main",
    )(h, dproj)


def _local_step(x, tgt, g0, w_main, w_lr, ln_g, ln_b, w_s, b_sb, w_gate, b_gate, gla_g, w_a, w_b, w_o, g_f):
    proj, lr, h = _proj_fwd(x, g0, w_main, w_lr)
    a = _mixer_a_fwd(proj, ln_g, ln_b, w_s, b_sb)
    o, ob, states = _gla_fwd(proj, lr, w_gate, b_gate, gla_g)
    dproj, dy, da, dob, dwa, dwb, dwo, dgf, loss_cols = _merge_fwd_bwd(x, tgt, proj, a, ob, w_a, w_b, w_o, g_f)
    dproj, dws, dbs, dlg, dlb = _mixer_a_bwd(proj, da, dproj, ln_g, ln_b, w_s, b_sb)
    dproj, dlr, dwg, dbg, dgg = _gla_bwd(proj, lr, o, states, dob, dproj, w_gate, b_gate, gla_g)
    dx, dg0, dwl = _dx_bwd(x, dy, dproj, dlr, g0, w_main, w_lr)
    dwm = _dw_main(h, dproj)
    return dict(loss_cols=loss_cols, dx=dx, dg0=dg0, dwm=dwm, dwl=dwl, dlg=dlg, dlb=dlb, dws=dws, dbs=dbs,
                dwg=dwg, dbg=dbg, dgg=dgg, dwa=dwa, dwb=dwb, dwo=dwo, dgf=dgf)


MESH = pl.DeviceIdType.MESH
_ANY = pl.BlockSpec(memory_space=pl.ANY)
NCHIP = 4


def _place():
    x, y, c = lax.axis_index("x"), lax.axis_index("y"), lax.axis_index("c")
    others = [(1 - x, y), (x, 1 - y), (1 - x, 1 - y)]
    return x, y, c, 2 * x + y, others


def _remote(src, dst, send_sem, recv_sem, to):
    return pltpu.make_async_remote_copy(src_ref=src, dst_ref=dst, send_sem=send_sem, recv_sem=recv_sem,
                                        device_id=to, device_id_type=MESH)


def _gather_weights(shards):
    n = len(shards)

    def body(*refs):
        srcs, dsts = refs[:n], refs[n:2 * n]
        send_sems, recv_sems, pass_send, pass_recv, local_sems = refs[2 * n:]
        x, y, c, me, others = _place()
        sibling = (x, y, 1 - c)
        own = [pltpu.make_async_copy(srcs[a], dsts[a].at[me], local_sems.at[a]) for a in range(n)]
        for cp in own:
            cp.start()
        sends = [_remote(srcs[a].at[c], dsts[a].at[me, c], send_sems.at[k, a], recv_sems.at[k, a], (cx, cy, c))
                 for k, (cx, cy) in enumerate(others) for a in range(n)]
        for cp in sends:
            cp.start()
        passes = []
        for k, (cx, cy) in enumerate(others):
            j = 2 * cx + cy
            for a in range(n):
                _remote(srcs[a].at[c], dsts[a].at[j, c], send_sems.at[k, a], recv_sems.at[k, a], (cx, cy, c)).wait_recv()
                cp = _remote(dsts[a].at[j, c], dsts[a].at[j, c], pass_send.at[k, a], pass_recv.at[k, a], sibling)
                cp.start()
                passes.append(cp)
        for k, (cx, cy) in enumerate(others):
            j = 2 * cx + cy
            for a in range(n):
                _remote(srcs[a].at[c], dsts[a].at[j, 1 - c], pass_send.at[k, a], pass_recv.at[k, a], sibling).wait_recv()
        for cp in sends + passes:
            cp.wait_send()
        for cp in own:
            cp.wait()

    return pl.pallas_call(
        body,
        in_specs=[_ANY] * n,
        out_specs=[_ANY] * n,
        out_shape=[jax.ShapeDtypeStruct((NCHIP,) + s.shape, s.dtype) for s in shards],
        scratch_shapes=[pltpu.SemaphoreType.DMA((3, n))] * 4 + [pltpu.SemaphoreType.DMA((n,))],
        name="gather_weights",
    )(*shards)


def _sibling_halves(bufs):
    n = len(bufs)

    def body(*refs):
        srcs, dsts = refs[:n], refs[n:2 * n]
        send_sems, recv_sems = refs[2 * n:]
        x, y, c, _, _ = _place()
        cps = [_remote(srcs[a].at[1 - c], dsts[a], send_sems.at[a], recv_sems.at[a], (x, y, 1 - c)) for a in range(n)]
        for cp in cps:
            cp.start()
        for cp in cps:
            cp.wait()

    return pl.pallas_call(
        body,
        in_specs=[_ANY] * n,
        out_specs=[_ANY] * n,
        out_shape=[jax.ShapeDtypeStruct(b.shape[1:], b.dtype) for b in bufs],
        scratch_shapes=[pltpu.SemaphoreType.DMA((n,))] * 2,
        name="sibling_halves",
    )(*bufs)


def _chip_exchange(parts):
    n = len(parts)

    def body(*refs):
        srcs, dsts = refs[:n], refs[n:2 * n]
        send_sems, recv_sems, local_sems = refs[2 * n:]
        x, y, c, me, others = _place()

        def part(a, j):
            return srcs[a].at[j if parts[a].shape[0] == NCHIP else 0]

        own = [pltpu.make_async_copy(part(a, me), dsts[a].at[me], local_sems.at[a]) for a in range(n)]
        for cp in own:
            cp.start()
        sends = [_remote(part(a, 2 * cx + cy), dsts[a].at[me], send_sems.at[k, a], recv_sems.at[k, a], (cx, cy, c))
                 for k, (cx, cy) in enumerate(others) for a in range(n)]
        for cp in sends:
            cp.start()
        for k, (cx, cy) in enumerate(others):
            for a in range(n):
                _remote(part(a, me), dsts[a].at[2 * cx + cy], send_sems.at[k, a], recv_sems.at[k, a], (cx, cy, c)).wait_recv()
        for cp in sends:
            cp.wait_send()
        for cp in own:
            cp.wait()

    return pl.pallas_call(
        body,
        in_specs=[_ANY] * n,
        out_specs=[_ANY] * n,
        out_shape=[jax.ShapeDtypeStruct((NCHIP,) + p.shape[1:], p.dtype) for p in parts],
        scratch_shapes=[pltpu.SemaphoreType.DMA((3, n))] * 2 + [pltpu.SemaphoreType.DMA((n,))],
        name="chip_exchange",
    )(*parts)


def _sibling_join(halves):
    n = len(halves)

    def body(*refs):
        srcs, dsts = refs[:n], refs[n:2 * n]
        send_sems, recv_sems, local_sems = refs[2 * n:]
        x, y, c, _, _ = _place()
        own = [pltpu.make_async_copy(srcs[a], dsts[a].at[c], local_sems.at[a]) for a in range(n)]
        cps = [_remote(srcs[a], dsts[a].at[c], send_sems.at[a], recv_sems.at[a], (x, y, 1 - c)) for a in range(n)]
        for cp in own + cps:
            cp.start()
        for a in range(n):
            _remote(srcs[a], dsts[a].at[1 - c], send_sems.at[a], recv_sems.at[a], (x, y, 1 - c)).wait_recv()
        for cp in cps:
            cp.wait_send()
        for cp in own:
            cp.wait()

    return pl.pallas_call(
        body,
        in_specs=[_ANY] * n,
        out_specs=[_ANY] * n,
        out_shape=[jax.ShapeDtypeStruct((2,) + s.shape, s.dtype) for s in halves],
        scratch_shapes=[pltpu.SemaphoreType.DMA((n,))] * 3,
        name="sibling_join",
    )(*halves)


def _row_tile(h):
    return 128 if h % 128 == 0 else h


def _pair_sum(core, buf, got, out_dtype):
    _, nj, h, w = buf.shape
    th = _row_tile(h)

    def body(c_ref, a_ref, b_ref, o_ref):
        del c_ref
        o_ref[...] = (a_ref[...] + b_ref[...]).astype(out_dtype)

    return pl.pallas_call(
        body,
        grid_spec=pltpu.PrefetchScalarGridSpec(
            num_scalar_prefetch=1,
            grid=(nj, h // th),
            in_specs=[pl.BlockSpec((None, None, th, w), lambda j, r, c_ref: (c_ref[0], j, r, 0)),
                      pl.BlockSpec((None, th, w), lambda j, r, c_ref: (j, r, 0))],
            out_specs=pl.BlockSpec((None, th, w), lambda j, r, c_ref: (j, r, 0)),
        ),
        out_shape=jax.ShapeDtypeStruct((nj, h, w), out_dtype),
        compiler_params=_params(("parallel", "parallel")),
        name="pair_sum",
    )(core, buf, got)


def _chip_sum(slots):
    _, h, w = slots.shape
    th = _row_tile(h)

    def body(s_ref, o_ref):
        acc = s_ref[0].astype(f32)
        for j in range(1, NCHIP):
            acc = acc + s_ref[j].astype(f32)
        o_ref[...] = acc

    return pl.pallas_call(
        body,
        grid=(h // th,),
        in_specs=[pl.BlockSpec((NCHIP, th, w), lambda r: (0, r, 0))],
        out_specs=pl.BlockSpec((th, w), lambda r: (r, 0)),
        out_shape=jax.ShapeDtypeStruct((h, w), f32),
        compiler_params=_params(("parallel",)),
        name="chip_sum",
    )(slots)


def _adamw(w, g, m, v):
    rows, width = w.shape
    tr = _row_tile(rows)
    m_corr = 1.0 - ADAM_B1 ** ADAM_STEP
    v_corr = 1.0 - ADAM_B2 ** ADAM_STEP

    def body(w_ref, g_ref, m_ref, v_ref, d_ref, nm_ref, nv_ref):
        g_v = g_ref[...]
        nm = ADAM_B1 * m_ref[...] + (1.0 - ADAM_B1) * g_v
        nv = ADAM_B2 * v_ref[...] + (1.0 - ADAM_B2) * (g_v * g_v)
        nm_ref[...] = nm
        nv_ref[...] = nv
        d_ref[...] = -ADAM_LR * ((nm / m_corr) / (jnp.sqrt(nv / v_corr) + ADAM_EPS) + ADAM_WD * w_ref[...])

    spec = pl.BlockSpec((tr, width), lambda r: (r, 0))
    return pl.pallas_call(
        body,
        grid=(rows // tr,),
        in_specs=[spec] * 4,
        out_specs=[spec] * 3,
        out_shape=[jax.ShapeDtypeStruct((rows, width), f32)] * 3,
        compiler_params=_params(("parallel",)),
        name="adamw",
    )(w, g, m, v)


def _reduce_gradients(core, bufs):
    got = _sibling_halves(bufs)
    parts = [_pair_sum(core, b, r, bf16 if b.shape[1] == NCHIP else f32) for b, r in zip(bufs, got)]
    slots = _chip_exchange(parts)
    halves = [_chip_sum(s) for s in slots]
    joined = _sibling_join(halves)
    return [j.reshape((2 * j.shape[1],) + j.shape[2:]) for j in joined]


_SMALL = (("norm_g", 8), ("ln_v_g", 8), ("ln_v_b", 8), ("w_spatial", 1024), ("b_spatial", 8), ("b_gate_up", 4),
          ("gla_norm_g", 2), ("final_norm_g", 8), ("w_gate_up", 64))
_SMALL_ROWS = 1136
WIN_SHARD = 2052
LR_COL = 6144


def _pack_rows(arrays, rows):
    flat = jnp.concatenate([a.reshape(-1, 128) for a in arrays], axis=0)
    return jnp.pad(flat, ((0, rows - flat.shape[0]), (0, 0)))


def kernel(x, norm_g, w_in, ln_v_g, ln_v_b, w_spatial, b_spatial, w_gate_up, b_gate_up, gla_norm_g, w_branch_a, w_branch_b, w_out, final_norm_g, loss_target, m_norm_g, m_w_in, m_ln_v_g, m_ln_v_b, m_w_spatial, m_b_spatial, m_w_gate_up, m_b_gate_up, m_gla_norm_g, m_w_branch_a, m_w_branch_b, m_w_out, m_final_norm_g, v_norm_g, v_w_in, v_ln_v_g, v_ln_v_b, v_w_spatial, v_b_spatial, v_w_gate_up, v_b_gate_up, v_gla_norm_g, v_w_branch_a, v_w_branch_b, v_w_out, v_final_norm_g):
    chip = 2 * lax.axis_index("x") + lax.axis_index("y")
    core = lax.axis_index("c").astype(jnp.int32).reshape(1)

    abo = jnp.concatenate([w_branch_a[0], w_branch_b[0], w_out[0]], axis=0).astype(bf16)
    g_win, g_abo, g_gate = _gather_weights([
        w_in[0].astype(bf16).reshape(2, D // 2, WIN_SHARD),
        abo.reshape(2, 3 * 128, D),
        w_gate_up[0].reshape(2, RANK // 2, 128)])
    w_full = jnp.transpose(g_win.reshape(NCHIP, D, WIN_SHARD), (1, 0, 2)).reshape(D, NCHIP * WIN_SHARD)
    w_main = jnp.concatenate([w_full[:, :LR_COL], w_full[:, LR_COL + RANK:]], axis=1)
    w_lr = jnp.pad(w_full[:, LR_COL:LR_COL + RANK], ((0, 0), (0, LRP - RANK)))
    g_abo = g_abo.reshape(NCHIP, 3, D // NCHIP, D)
    w_a, w_b, w_o = (g_abo[:, i].reshape(D, D) for i in range(3))
    w_gate = jnp.transpose(g_gate.reshape(NCHIP, RANK, 128), (1, 0, 2)).reshape(RANK, KEYB)
    w_gate = jnp.pad(w_gate, ((0, LRP - RANK), (0, 0)))
    b_sb = jnp.broadcast_to(b_spatial[0][:, :, None], (HA, CA, GA))

    r = _local_step(x[0], loss_target[0], norm_g, w_main, w_lr, ln_v_g, ln_v_b, w_spatial[0], b_sb,
                    w_gate, b_gate_up, gla_norm_g, w_a, w_b, w_o, final_norm_g.reshape(1, D))
    loss = lax.psum(jnp.sum(r["loss_cols"]), ("x", "y", "c"))

    dwm, dwl = r["dwm"], r["dwl"]
    dw_in = jnp.concatenate([dwm[:, :LR_COL], dwl[:, :RANK], dwm[:, LR_COL:]], axis=1)
    b_win = jnp.transpose(dw_in.reshape(2, D // 2, NCHIP, WIN_SHARD), (0, 2, 1, 3))
    dabo = jnp.stack([r["dwa"].reshape(NCHIP, D // NCHIP, D), r["dwb"].reshape(NCHIP, D // NCHIP, D),
                      r["dwo"].reshape(NCHIP, D // NCHIP, D)], axis=1)
    b_abo = jnp.transpose(dabo.reshape(NCHIP, 2, 3 * 128, D), (1, 0, 2, 3))
    small = _pack_rows([r["dg0"], r["dlg"], r["dlb"], r["dws"], r["dbs"][:, :, 0], r["dbg"], r["dgg"], r["dgf"],
                        r["dwg"][:RANK]], _SMALL_ROWS)
    b_small = small.reshape(2, 1, _SMALL_ROWS // 2, 128)
    g_win_s, g_abo_s, g_small = _reduce_gradients(core, [b_win, b_abo, b_small])

    grads = {}
    row = 0
    for name, rows in _SMALL:
        grads[name] = g_small[row:row + rows]
        row += rows
    dwg_full = grads["w_gate_up"].reshape(RANK, KEYB)
    grads["w_gate_up"] = lax.dynamic_slice_in_dim(dwg_full, chip * 128, 128, axis=1)
    grads["w_in"] = g_win_s
    g_abo_s = g_abo_s.reshape(3, D // NCHIP, D)
    grads["w_branch_a"], grads["w_branch_b"], grads["w_out"] = g_abo_s[0], g_abo_s[1], g_abo_s[2]

    weights = dict(norm_g=norm_g, w_in=w_in, ln_v_g=ln_v_g, ln_v_b=ln_v_b, w_spatial=w_spatial, b_spatial=b_spatial,
                   w_gate_up=w_gate_up, b_gate_up=b_gate_up, gla_norm_g=gla_norm_g, w_branch_a=w_branch_a,
                   w_branch_b=w_branch_b, w_out=w_out, final_norm_g=final_norm_g)
    m_in = dict(norm_g=m_norm_g, w_in=m_w_in, ln_v_g=m_ln_v_g, ln_v_b=m_ln_v_b, w_spatial=m_w_spatial,
                b_spatial=m_b_spatial, w_gate_up=m_w_gate_up, b_gate_up=m_b_gate_up, gla_norm_g=m_gla_norm_g,
                w_branch_a=m_w_branch_a, w_branch_b=m_w_branch_b, w_out=m_w_out, final_norm_g=m_final_norm_g)
    v_in = dict(norm_g=v_norm_g, w_in=v_w_in, ln_v_g=v_ln_v_g, ln_v_b=v_ln_v_b, w_spatial=v_w_spatial,
                b_spatial=v_b_spatial, w_gate_up=v_w_gate_up, b_gate_up=v_b_gate_up, gla_norm_g=v_gla_norm_g,
                w_branch_a=v_w_branch_a, w_branch_b=v_w_branch_b, w_out=v_w_out, final_norm_g=v_final_norm_g)
    names = list(weights)
    big = ("w_in", "w_branch_a", "w_branch_b", "w_out")
    small_names = [n for n in names if n not in big]
    out_g, out_d, out_m, out_v = {}, {}, {}, {}
    for n in big:
        shape = weights[n].shape
        two_d = shape[1:]
        g2 = grads[n].reshape(two_d)
        d, nm, nv = _adamw(weights[n].reshape(two_d), g2, m_in[n].reshape(two_d), v_in[n].reshape(two_d))
        out_g[n], out_d[n], out_m[n], out_v[n] = (t.reshape(shape) for t in (g2, d, nm, nv))
    upd_rows = sum(weights[n].size for n in small_names) // 128
    pad_rows = -(-upd_rows // 8) * 8
    packed = [_pack_rows([t[n] for n in small_names], pad_rows) for t in (weights, grads, m_in, v_in)]
    d_s, m_s, v_s = _adamw(*packed)
    row = 0
    for n in small_names:
        shape = weights[n].shape
        rows = weights[n].size // 128
        out_g[n] = grads[n].reshape(shape)
        out_d[n], out_m[n], out_v[n] = (t[row:row + rows].reshape(shape) for t in (d_s, m_s, v_s))
        row += rows
    return (loss, r["dx"][None], *[out_g[n] for n in names], *[out_d[n] for n in names],
            *[out_m[n] for n in names], *[out_v[n] for n in names])
```

```python
import functools
import math

import jax
import jax.numpy as jnp
from jax import lax
from jax.experimental import pallas as pl
from jax.experimental.pallas import tpu as pltpu

f32 = jnp.float32
bf16 = jnp.bfloat16

D = 1024
NMAIN = 8192
LRP = 128
RANK = 16
HA, GA, CA = 8, 128, 128
HB, DK, DV, CB = 4, 128, 256, 64
KEYB = HB * DK
EPS = 1e-6
LN_EPS = 1e-5
GATE_NORM = 16.0
QSCALE = DK ** -0.5
COL_U, COL_V, COL_ZA = 0, 1, 2
COL_Q, COL_K = 6, 7
COL_VB, COL_ZB = 4, 5
COL_GATES = 3
VMEM_LIMIT = 56 * 1024 * 1024

ADAM_LR, ADAM_B1, ADAM_B2, ADAM_EPS, ADAM_WD, ADAM_STEP = 0.001, 0.9, 0.999, 1e-08, 0.01, 10

_SQRT_HALF = 0.7071067811865476
_INV_SQRT_2PI = 0.3989422804014327


def _dot(a, b):
    return jnp.dot(a, b, preferred_element_type=f32)


def _dot_nt(a, b):
    return lax.dot_general(a, b, (((1,), (1,)), ((), ())), preferred_element_type=f32)


def _dot_tn(a, b):
    return lax.dot_general(a, b, (((0,), (0,)), ((), ())), preferred_element_type=f32)


def _dot_exact(a, b):
    return jnp.dot(a, b, preferred_element_type=f32, precision=lax.Precision.HIGHEST)


def _gelu(x):
    return 0.5 * x * (1.0 + lax.erf(x * _SQRT_HALF))


def _gelu_grad(x):
    return 0.5 * (1.0 + lax.erf(x * _SQRT_HALF)) + x * (jnp.exp(-0.5 * x * x) * _INV_SQRT_2PI)


def _sigmoid(x):
    return 1.0 / (1.0 + jnp.exp(-x))


def _params(sem):
    return pltpu.CompilerParams(dimension_semantics=sem, vmem_limit_bytes=VMEM_LIMIT)


def _resident(shape):
    nd = len(shape)
    return pl.BlockSpec(shape, lambda *_: (0,) * nd, pipeline_mode=pl.Buffered(1))


def _proj_fwd(x, g0, w_main, w_lr, tm=1024, tn=512):
    T = x.shape[0]
    tm = min(tm, T)

    def body(x_ref, g_ref, w_ref, wl_ref, proj_ref, lr_ref, h_ref):
        @pl.when(pl.program_id(1) == 0)
        def _():
            xv = x_ref[...]
            r = lax.rsqrt(jnp.mean(xv * xv, axis=-1, keepdims=True) + EPS)
            h = (xv * r * g_ref[...]).astype(bf16)
            h_ref[...] = h
            lr_ref[...] = _dot_nt(h, wl_ref[...])

        proj_ref[...] = _dot_nt(h_ref[...], w_ref[...])

    return pl.pallas_call(
        body,
        grid=(T // tm, NMAIN // tn),
        in_specs=[
            pl.BlockSpec((tm, D), lambda i, j: (i, 0)),
            pl.BlockSpec((1, D), lambda i, j: (0, 0)),
            pl.BlockSpec((tn, D), lambda i, j: (j, 0)),
            pl.BlockSpec((LRP, D), lambda i, j: (0, 0)),
        ],
        out_specs=[
            pl.BlockSpec((tm, tn), lambda i, j: (i, j)),
            pl.BlockSpec((tm, LRP), lambda i, j: (i, 0)),
            pl.BlockSpec((tm, D), lambda i, j: (i, 0)),
        ],
        out_shape=[
            jax.ShapeDtypeStruct((T, NMAIN), f32),
            jax.ShapeDtypeStruct((T, LRP), f32),
            jax.ShapeDtypeStruct((T, D), bf16),
        ],
        compiler_params=_params(("parallel", "arbitrary")),
        name="proj_fwd",
    )(x, g0, w_main, w_lr)


def _causal_mask():
    t = lax.broadcasted_iota(jnp.int32, (CA, CA), 0)
    s = lax.broadcasted_iota(jnp.int32, (CA, CA), 1)
    return s <= t


def _layernorm_parts(gv):
    mu = jnp.mean(gv, axis=-1, keepdims=True)
    xc = gv - mu
    rs = lax.rsqrt(jnp.mean(xc * xc, axis=-1, keepdims=True) + LN_EPS)
    return xc * rs, rs


def _mixer_a_fwd(proj, ln_g, ln_b, w_s, b_sb, tm=256):
    T = proj.shape[0]

    def body(u_ref, v_ref, za_ref, lg_ref, lb_ref, ws_ref, bs_ref, a_ref, vln_s):
        vhat, _ = _layernorm_parts(_gelu(v_ref[...]))
        vln_s[...] = (vhat * lg_ref[...] + lb_ref[...]).astype(bf16)
        mask = _causal_mask()
        for g in range(HA):
            wg = jnp.where(mask, ws_ref[g], 0.0).astype(bf16)
            cols = slice(g * GA, (g + 1) * GA)
            for c in range(tm // CA):
                rows = slice(c * CA, (c + 1) * CA)
                mixed = _dot(wg, vln_s[rows, cols]) + bs_ref[g]
                za = za_ref[rows, cols]
                a = _gelu(u_ref[rows, cols]) * mixed * (za * _sigmoid(za))
                a_ref[rows, cols] = a.astype(bf16)

    def col(cidx):
        return pl.BlockSpec((tm, D), lambda i, c=cidx: (i, c))

    return pl.pallas_call(
        body,
        grid=(T // tm,),
        in_specs=[col(COL_U), col(COL_V), col(COL_ZA), _resident((1, D)), _resident((1, D)),
                  _resident((HA, CA, CA)), _resident((HA, CA, GA))],
        out_specs=pl.BlockSpec((tm, D), lambda i: (i, 0)),
        out_shape=jax.ShapeDtypeStruct((T, D), bf16),
        scratch_shapes=[pltpu.VMEM((tm, D), bf16)],
        compiler_params=_params(("parallel",)),
        name="mixer_a_fwd",
    )(proj, proj, proj, ln_g, ln_b, w_s, b_sb)


def _mixer_a_bwd(proj, da, dproj, ln_g, ln_b, w_s, b_sb, tm=256):
    T = proj.shape[0]
    nsteps = T // tm

    def body(u_ref, v_ref, za_ref, da_ref, dp_in, lg_ref, lb_ref, ws_ref, bs_ref,
             dp_ref, dws_ref, dbs_ref, dlg_ref, dlb_ref, vln_s, dvln_s):
        del dp_in
        i = pl.program_id(0)

        @pl.when(i == 0)
        def _():
            dws_ref[...] = jnp.zeros_like(dws_ref)
            dbs_ref[...] = jnp.zeros_like(dbs_ref)
            dlg_ref[...] = jnp.zeros_like(dlg_ref)
            dlb_ref[...] = jnp.zeros_like(dlb_ref)

        v = v_ref[...]
        vhat, rs = _layernorm_parts(_gelu(v))
        vln_s[...] = (vhat * lg_ref[...] + lb_ref[...]).astype(bf16)
        mask = _causal_mask()
        for g in range(HA):
            wg = jnp.where(mask, ws_ref[g], 0.0).astype(bf16)
            cols = slice(g * GA, (g + 1) * GA)
            dw_acc = jnp.zeros((CA, CA), f32)
            db_acc = jnp.zeros((CA, 1), f32)
            for c in range(tm // CA):
                rows = slice(c * CA, (c + 1) * CA)
                vln = vln_s[rows, cols]
                mixed = _dot(wg, vln) + bs_ref[g]
                u = u_ref[rows, cols]
                za = za_ref[rows, cols]
                da_blk = da_ref[rows, cols]
                sg = _sigmoid(za)
                sz = za * sg
                gu = _gelu(u)
                dp_ref[rows, cols] = (da_blk * mixed * sz * _gelu_grad(u)).astype(bf16)
                dp_ref[rows, 2 * D + g * GA:2 * D + (g + 1) * GA] = (
                    da_blk * gu * mixed * (sg * (1.0 + za * (1.0 - sg)))).astype(bf16)
                dmixed = da_blk * gu * sz
                dmb = dmixed.astype(bf16)
                dvln_s[rows, cols] = _dot_tn(wg, dmb)
                dw_acc = dw_acc + _dot_nt(dmb, vln)
                db_acc = db_acc + jnp.sum(dmixed, axis=-1, keepdims=True)
            dws_ref[g] += dw_acc
            dbs_ref[g] += jnp.broadcast_to(db_acc, (CA, GA))

        dvln = dvln_s[...]
        dlg_ref[...] += jnp.sum(dvln * vhat, axis=0, keepdims=True)
        dlb_ref[...] += jnp.sum(dvln, axis=0, keepdims=True)
        dvhat = dvln * lg_ref[...]
        dgv = rs * (dvhat - jnp.mean(dvhat, axis=-1, keepdims=True)
                    - vhat * jnp.mean(dvhat * vhat, axis=-1, keepdims=True))
        dp_ref[:, D:2 * D] = (dgv * _gelu_grad(v)).astype(bf16)

        @pl.when(i == nsteps - 1)
        def _():
            for g in range(HA):
                dws_ref[g] = jnp.where(mask, dws_ref[g], 0.0)

    def col(cidx):
        return pl.BlockSpec((tm, D), lambda i, c=cidx: (i, c))

    return pl.pallas_call(
        body,
        grid=(nsteps,),
        in_specs=[col(COL_U), col(COL_V), col(COL_ZA), pl.BlockSpec((tm, D), lambda i: (i, 0)),
                  pl.BlockSpec(memory_space=pl.ANY),
                  _resident((1, D)), _resident((1, D)), _resident((HA, CA, CA)), _resident((HA, CA, GA))],
        out_specs=[pl.BlockSpec((tm, 3 * D), lambda i: (i, 0)),
                   _resident((HA, CA, CA)), _resident((HA, CA, GA)), _resident((1, D)), _resident((1, D))],
        out_shape=[jax.ShapeDtypeStruct(dproj.shape, dproj.dtype),
                   jax.ShapeDtypeStruct((HA, CA, CA), f32), jax.ShapeDtypeStruct((HA, CA, GA), f32),
                   jax.ShapeDtypeStruct((1, D), f32), jax.ShapeDtypeStruct((1, D), f32)],
        scratch_shapes=[pltpu.VMEM((tm, D), bf16), pltpu.VMEM((tm, D), f32)],
        input_output_aliases={4: 0},
        compiler_params=_params(("arbitrary",)),
        name="mixer_a_bwd",
    )(proj, proj, proj, da, dproj, ln_g, ln_b, w_s, b_sb)


def _tri(n, upper):
    r = lax.broadcasted_iota(jnp.int32, (n, n), 0)
    c = lax.broadcasted_iota(jnp.int32, (n, n), 1)
    return jnp.where((c >= r) if upper else (c <= r), 1.0, 0.0).astype(f32)


def _log_alpha(lr, wg, bg):
    logit = _dot(lr.astype(bf16), wg.astype(bf16)) + bg
    la = (jnp.minimum(logit, 0.0) - jnp.log1p(jnp.exp(-jnp.abs(logit)))) * (1.0 / GATE_NORM)
    return logit, la


def _gla_fwd(proj, lr, w_gate, b_gate, gla_g, tm=256):
    T = proj.shape[0]
    nchunk = T // CB
    cpb = tm // CB

    def body(q_ref, k_ref, v_ref, zb_ref, lr_ref, wg_ref, bg_ref, gg_ref,
             o_ref, ob_ref, st_ref, state, la_s):
        @pl.when(pl.program_id(0) == 0)
        def _():
            state[...] = jnp.zeros_like(state)

        _, la = _log_alpha(lr_ref[...], wg_ref[...], bg_ref[...])
        la_s[...] = la
        ltri = _tri(CB, upper=False)
        causal = ltri > 0.5
        for c in range(cpb):
            rows = slice(c * CB, (c + 1) * CB)
            for hd in range(HB):
                kc = slice(hd * DK, (hd + 1) * DK)
                vc = slice(hd * DV, (hd + 1) * DV)
                b = _dot_exact(ltri, la_s[rows, kc])
                bl = b[CB - 1:CB, :]
                bm = b[CB // 2 - 1:CB // 2, :]
                q = q_ref[rows, kc] * QSCALE
                k = k_ref[rows, kc]
                v = v_ref[rows, vc].astype(bf16)
                qi = (q * jnp.exp(b - bm)).astype(bf16)
                ki = (k * jnp.exp(bm - b)).astype(bf16)
                qe = (q * jnp.exp(b)).astype(bf16)
                ks = (k * jnp.exp(bl - b)).astype(bf16)
                p = jnp.where(causal, _dot_nt(qi, ki), 0.0).astype(bf16)
                s0 = state[hd]
                st_ref[c, hd] = s0
                o = _dot(p, v) + _dot_nt(qe, s0.astype(bf16))
                state[hd] = s0 * jnp.exp(bl) + _dot_tn(v, ks)
                o_ref[rows, vc] = o
                ro = lax.rsqrt(jnp.mean(o * o, axis=-1, keepdims=True) + EPS)
                zb = zb_ref[rows, vc]
                ob_ref[rows, vc] = (o * ro * gg_ref[...] * (zb * _sigmoid(zb))).astype(bf16)

    return pl.pallas_call(
        body,
        grid=(T // tm,),
        in_specs=[pl.BlockSpec((tm, KEYB), lambda i: (i, COL_Q)),
                  pl.BlockSpec((tm, KEYB), lambda i: (i, COL_K)),
                  pl.BlockSpec((tm, D), lambda i: (i, COL_VB)),
                  pl.BlockSpec((tm, D), lambda i: (i, COL_ZB)),
                  pl.BlockSpec((tm, LRP), lambda i: (i, 0)),
                  _resident((LRP, KEYB)), _resident((1, KEYB)), _resident((1, DV))],
        out_specs=[pl.BlockSpec((tm, D), lambda i: (i, 0)),
                   pl.BlockSpec((tm, D), lambda i: (i, 0)),
                   pl.BlockSpec((cpb, HB, DV, DK), lambda i: (i, 0, 0, 0))],
        out_shape=[jax.ShapeDtypeStruct((T, D), f32), jax.ShapeDtypeStruct((T, D), bf16),
                   jax.ShapeDtypeStruct((nchunk, HB, DV, DK), f32)],
        scratch_shapes=[pltpu.VMEM((HB, DV, DK), f32), pltpu.VMEM((tm, KEYB), f32)],
        compiler_params=_params(("arbitrary",)),
        name="gla_fwd",
    )(proj, proj, proj, proj, lr, w_gate, b_gate, gla_g)


def _gla_bwd(proj, lr, o, states, dob, dproj, w_gate, b_gate, gla_g, tm=256):
    T = proj.shape[0]
    cpb = tm // CB
    nb = T // tm

    def body(q_ref, k_ref, v_ref, zb_ref, lr_ref, o_ref, st_ref, dob_ref, dp_in, wg_ref, bg_ref, gg_ref,
             dp_ref, dlr_ref, dwg_ref, dbg_ref, dgg_ref, dstate, la_s, dlogit_s):
        del dp_in
        step = pl.program_id(0)

        @pl.when(step == 0)
        def _():
            dstate[...] = jnp.zeros_like(dstate)
            dwg_ref[...] = jnp.zeros_like(dwg_ref)
            dbg_ref[...] = jnp.zeros_like(dbg_ref)
            dgg_ref[...] = jnp.zeros_like(dgg_ref)

        lr_v = lr_ref[...]
        logit, la = _log_alpha(lr_v, wg_ref[...], bg_ref[...])
        la_s[...] = la
        ltri = _tri(CB, upper=False)
        utri = _tri(CB, upper=True)
        causal = ltri > 0.5
        gg = gg_ref[...]
        dgg_acc = jnp.zeros((1, DV), f32)
        for c in reversed(range(cpb)):
            rows = slice(c * CB, (c + 1) * CB)
            for hd in range(HB):
                kc = slice(hd * DK, (hd + 1) * DK)
                vc = slice(hd * DV, (hd + 1) * DV)
                o_h = o_ref[rows, vc]
                ro = lax.rsqrt(jnp.mean(o_h * o_h, axis=-1, keepdims=True) + EPS)
                ohat = o_h * ro
                zb = zb_ref[rows, vc]
                sg = _sigmoid(zb)
                dob_h = dob_ref[rows, vc]
                don = dob_h * (zb * sg)
                dp_ref[rows, 2 * D + hd * DV:2 * D + (hd + 1) * DV] = (
                    dob_h * ohat * gg * (sg * (1.0 + zb * (1.0 - sg)))).astype(bf16)
                dgg_acc = dgg_acc + jnp.sum(don * ohat, axis=0, keepdims=True)
                dohat = don * gg
                do = (ro * (dohat - ohat * jnp.mean(dohat * ohat, axis=-1, keepdims=True))).astype(bf16)
                b = _dot_exact(ltri, la_s[rows, kc])
                bl = b[CB - 1:CB, :]
                bm = b[CB // 2 - 1:CB // 2, :]
                e_b, e_qm, e_km, e_ks, e_l = jnp.exp(b), jnp.exp(b - bm), jnp.exp(bm - b), jnp.exp(bl - b), jnp.exp(bl)
                q = q_ref[rows, kc] * QSCALE
                k = k_ref[rows, kc]
                v = v_ref[rows, vc].astype(bf16)
                qi = (q * e_qm).astype(bf16)
                ki = (k * e_km).astype(bf16)
                qe = (q * e_b).astype(bf16)
                ks_f = k * e_ks
                ks = ks_f.astype(bf16)
                p = jnp.where(causal, _dot_nt(qi, ki), 0.0).astype(bf16)
                s0 = st_ref[c, hd]
                ds = dstate[hd]
                ds_b = ds.astype(bf16)
                dv = _dot_tn(p, do) + _dot_nt(ks, ds_b)
                dpm = jnp.where(causal, _dot_nt(do, v), 0.0).astype(bf16)
                dqi = _dot(dpm, ki)
                dki = _dot_tn(dpm, qi)
                dqe = _dot(do, s0.astype(bf16))
                dks = _dot(v, ds_b)
                dq_s = dqi * e_qm + dqe * e_b
                dk = dki * e_km + dks * e_ks
                tail = (jnp.sum(dks * ks_f, axis=0, keepdims=True)
                        + e_l * jnp.sum(ds * s0, axis=0, keepdims=True))
                dg = _dot_exact(utri, dq_s * q - dk * k) + tail
                dstate[hd] = _dot_tn(do, qe) + ds * e_l
                dp_ref[rows, kc] = (dq_s * QSCALE).astype(bf16)
                dp_ref[rows, KEYB + hd * DK:KEYB + (hd + 1) * DK] = dk.astype(bf16)
                dp_ref[rows, D + hd * DV:D + (hd + 1) * DV] = dv.astype(bf16)
                dlogit_s[rows, kc] = dg * (1.0 / GATE_NORM)
        dgg_ref[...] += dgg_acc
        dlogit = dlogit_s[...] * _sigmoid(-logit)
        dbg_ref[...] += jnp.sum(dlogit, axis=0, keepdims=True)
        dlb = dlogit.astype(bf16)
        dlr_ref[...] = _dot_nt(dlb, wg_ref[...].astype(bf16)).astype(bf16)
        dwg_ref[...] += _dot_tn(lr_v.astype(bf16), dlb)

    def rev(cidx):
        return lambda i, c=cidx: (nb - 1 - i, c)

    return pl.pallas_call(
        body,
        grid=(nb,),
        in_specs=[pl.BlockSpec((tm, KEYB), rev(COL_Q)),
                  pl.BlockSpec((tm, KEYB), rev(COL_K)),
                  pl.BlockSpec((tm, D), rev(COL_VB)),
                  pl.BlockSpec((tm, D), rev(COL_ZB)),
                  pl.BlockSpec((tm, LRP), rev(0)),
                  pl.BlockSpec((tm, D), rev(0)),
                  pl.BlockSpec((cpb, HB, DV, DK), lambda i: (nb - 1 - i, 0, 0, 0)),
                  pl.BlockSpec((tm, D), rev(0)),
                  pl.BlockSpec(memory_space=pl.ANY),
                  _resident((LRP, KEYB)), _resident((1, KEYB)), _resident((1, DV))],
        out_specs=[pl.BlockSpec((tm, 3 * D), rev(1)),
                   pl.BlockSpec((tm, LRP), rev(0)),
                   _resident((LRP, KEYB)), _resident((1, KEYB)), _resident((1, DV))],
        out_shape=[jax.ShapeDtypeStruct(dproj.shape, dproj.dtype),
                   jax.ShapeDtypeStruct((T, LRP), bf16),
                   jax.ShapeDtypeStruct((LRP, KEYB), f32), jax.ShapeDtypeStruct((1, KEYB), f32),
                   jax.ShapeDtypeStruct((1, DV), f32)],
        scratch_shapes=[pltpu.VMEM((HB, DV, DK), f32), pltpu.VMEM((tm, KEYB), f32), pltpu.VMEM((tm, KEYB), f32)],
        input_output_aliases={8: 0},
        compiler_params=_params(("arbitrary",)),
        name="gla_bwd",
    )(proj, proj, proj, proj, lr, o, states, dob, dproj, w_gate, b_gate, gla_g)


def _merge_fwd_bwd(x, tgt, proj, a, ob, w_a, w_b, w_o, g_f, tm=256):
    T = x.shape[0]

    def body(x_ref, t_ref, gt_ref, a_ref, ob_ref, wa_ref, wb_ref, wo_ref, gf_ref,
             dp_ref, dy_ref, da_ref, dob_ref, dwa_ref, dwb_ref, dwo_ref, dgf_ref, loss_ref):
        @pl.when(pl.program_id(0) == 0)
        def _():
            dwa_ref[...] = jnp.zeros_like(dwa_ref)
            dwb_ref[...] = jnp.zeros_like(dwb_ref)
            dwo_ref[...] = jnp.zeros_like(dwo_ref)
            dgf_ref[...] = jnp.zeros_like(dgf_ref)
            loss_ref[...] = jnp.zeros_like(loss_ref)

        ga = _sigmoid(gt_ref[:, :D])
        gb = _sigmoid(gt_ref[:, D:])
        a_v = a_ref[...]
        ob_v = ob_ref[...]
        pa = _dot(a_v, wa_ref[...])
        pb = _dot(ob_v, wb_ref[...])
        mb = (ga * pa + gb * pb).astype(bf16)
        y = x_ref[...] + _dot(mb, wo_ref[...])
        r1 = lax.rsqrt(jnp.mean(y * y, axis=-1, keepdims=True) + EPS)
        yhat = y * r1
        gf = gf_ref[...]
        err = yhat * gf - t_ref[...]
        loss_ref[...] += jnp.sum(err * err, axis=0, keepdims=True) * (0.5 / D)
        dout = err * (1.0 / D)
        dgf_ref[...] += jnp.sum(dout * yhat, axis=0, keepdims=True)
        dyn = dout * gf
        dy = r1 * (dyn - yhat * jnp.mean(dyn * yhat, axis=-1, keepdims=True))
        dy_ref[...] = dy
        dyb = dy.astype(bf16)
        dwo_ref[...] += _dot_tn(mb, dyb)
        dm = _dot_nt(dyb, wo_ref[...])
        dpa = (dm * ga).astype(bf16)
        dpb = (dm * gb).astype(bf16)
        dp_ref[:, :D] = (dm * pa * ga * (1.0 - ga)).astype(bf16)
        dp_ref[:, D:] = (dm * pb * gb * (1.0 - gb)).astype(bf16)
        dwa_ref[...] += _dot_tn(a_v, dpa)
        dwb_ref[...] += _dot_tn(ob_v, dpb)
        da_ref[...] = _dot_nt(dpa, wa_ref[...])
        dob_ref[...] = _dot_nt(dpb, wb_ref[...])

    row = lambda: pl.BlockSpec((tm, D), lambda i: (i, 0))
    return pl.pallas_call(
        body,
        grid=(T // tm,),
        in_specs=[row(), row(), pl.BlockSpec((tm, 2 * D), lambda i: (i, COL_GATES)), row(), row(),
                  _resident((D, D)), _resident((D, D)), _resident((D, D)), _resident((1, D))],
        out_specs=[pl.BlockSpec((tm, 2 * D), lambda i: (i, COL_GATES)), row(), row(), row(),
                   _resident((D, D)), _resident((D, D)), _resident((D, D)), _resident((1, D)), _resident((1, D))],
        out_shape=[jax.ShapeDtypeStruct((T, NMAIN), bf16),
                   jax.ShapeDtypeStruct((T, D), f32), jax.ShapeDtypeStruct((T, D), f32),
                   jax.ShapeDtypeStruct((T, D), f32),
                   jax.ShapeDtypeStruct((D, D), f32), jax.ShapeDtypeStruct((D, D), f32),
                   jax.ShapeDtypeStruct((D, D), f32),
                   jax.ShapeDtypeStruct((1, D), f32), jax.ShapeDtypeStruct((1, D), f32)],
        compiler_params=_params(("arbitrary",)),
        name="merge_fwd_bwd",
    )(x, tgt, proj, a, ob, w_a, w_b, w_o, g_f)


def _dx_bwd(x, dy, dproj, dlr, g0, w_main, w_lr, tm=256):
    T = x.shape[0]

    def body(x_ref, dy_ref, dp_ref, dl_ref, g_ref, w_ref, wl_ref, dx_ref, dg_ref, dwl_ref):
        @pl.when(pl.program_id(0) == 0)
        def _():
            dg_ref[...] = jnp.zeros_like(dg_ref)
            dwl_ref[...] = jnp.zeros_like(dwl_ref)

        xv = x_ref[...]
        r = lax.rsqrt(jnp.mean(xv * xv, axis=-1, keepdims=True) + EPS)
        xhat = xv * r
        g = g_ref[...]
        dl = dl_ref[...]
        dh = _dot(dp_ref[...], w_ref[...]) + _dot(dl, wl_ref[...])
        dg_ref[...] += jnp.sum(dh * xhat, axis=0, keepdims=True)
        t = dh * g
        dx_ref[...] = dy_ref[...] + r * (t - xhat * jnp.mean(t * xhat, axis=-1, keepdims=True))
        dwl_ref[...] += _dot_tn(dl, (xhat * g).astype(bf16))

    row = lambda: pl.BlockSpec((tm, D), lambda i: (i, 0))
    return pl.pallas_call(
        body,
        grid=(T // tm,),
        in_specs=[row(), row(), pl.BlockSpec((tm, NMAIN), lambda i: (i, 0)),
                  pl.BlockSpec((tm, LRP), lambda i: (i, 0)),
                  _resident((1, D)), _resident((NMAIN, D)), _resident((LRP, D))],
        out_specs=[row(), _resident((1, D)), _resident((LRP, D))],
        out_shape=[jax.ShapeDtypeStruct((T, D), f32), jax.ShapeDtypeStruct((1, D), f32),
                   jax.ShapeDtypeStruct((LRP, D), f32)],
        compiler_params=_params(("arbitrary",)),
        name="dx_bwd",
    )(x, dy, dproj, dlr, g0, w_main, w_lr)


def _dw_main(h, dproj, tm=512, tn=1024):
    T = h.shape[0]
    tm = min(tm, T)

    def body(h_ref, dp_ref, dw_ref):
        @pl.when(pl.program_id(1) == 0)
        def _():
            dw_ref[...] = jnp.zeros_like(dw_ref)

        dw_ref[...] += _dot_tn(dp_ref[...], h_ref[...])

    return pl.pallas_call(
        body,
        grid=(NMAIN // tn, T // tm),
        in_specs=[pl.BlockSpec((tm, D), lambda j, k: (k, 0)), pl.BlockSpec((tm, tn), lambda j, k: (k, j))],
        out_specs=pl.BlockSpec((tn, D), lambda j, k: (j, 0)),
        out_shape=jax.ShapeDtypeStruct((NMAIN, D), f32),
        compiler_params=_params(("parallel", "arbitrary")),
        name="dw_main",
    )(h, dproj)


def _local_step(x, tgt, g0, w_main, w_lr, ln_g, ln_b, w_s, b_sb, w_gate, b_gate, gla_g, w_a, w_b, w_o, g_f):
    proj, lr, h = _proj_fwd(x, g0, w_main, w_lr)
    a = _mixer_a_fwd(proj, ln_g, ln_b, w_s, b_sb)
    o, ob, states = _gla_fwd(proj, lr, w_gate, b_gate, gla_g)
    dproj, dy, da, dob, dwa, dwb, dwo, dgf, loss_cols = _merge_fwd_bwd(x, tgt, proj, a, ob, w_a, w_b, w_o, g_f)
    dproj, dws, dbs, dlg, dlb = _mixer_a_bwd(proj, da, dproj, ln_g, ln_b, w_s, b_sb)
    dproj, dlr, dwg, dbg, dgg = _gla_bwd(proj, lr, o, states, dob, dproj, w_gate, b_gate, gla_g)
    dx, dg0, dwl = _dx_bwd(x, dy, dproj, dlr, g0, w_main, w_lr)
    dwm = _dw_main(h, dproj)
    return dict(loss_cols=loss_cols, dx=dx, dg0=dg0, dwm=dwm, dwl=dwl, dlg=dlg, dlb=dlb, dws=dws, dbs=dbs,
                dwg=dwg, dbg=dbg, dgg=dgg, dwa=dwa, dwb=dwb, dwo=dwo, dgf=dgf)


MESH = pl.DeviceIdType.MESH
_ANY = pl.BlockSpec(memory_space=pl.ANY)
NCHIP = 4


def _place():
    x, y, c = lax.axis_index("x"), lax.axis_index("y"), lax.axis_index("c")
    others = [(1 - x, y), (x, 1 - y), (1 - x, 1 - y)]
    return x, y, c, 2 * x + y, others


def _remote(src, dst, send_sem, recv_sem, to):
    return pltpu.make_async_remote_copy(src_ref=src, dst_ref=dst, send_sem=send_sem, recv_sem=recv_sem,
                                        device_id=to, device_id_type=MESH)


def _gather_weights(shards):
    n = len(shards)

    def body(*refs):
        srcs, dsts = refs[:n], refs[n:2 * n]
        send_sems, recv_sems, pass_send, pass_recv = refs[2 * n:]
        x, y, c, me, others = _place()
        sibling = (x, y, 1 - c)
        sends = [_remote(srcs[a].at[c], dsts[a].at[me, c], send_sems.at[k, a], recv_sems.at[k, a], (cx, cy, c))
                 for k, (cx, cy) in enumerate(others) for a in range(n)]
        for cp in sends:
            cp.start()
        passes = []
        for k, (cx, cy) in enumerate(others):
            j = 2 * cx + cy
            for a in range(n):
                _remote(srcs[a].at[c], dsts[a].at[j, c], send_sems.at[k, a], recv_sems.at[k, a], (cx, cy, c)).wait_recv()
                cp = _remote(dsts[a].at[j, c], dsts[a].at[j, c], pass_send.at[k, a], pass_recv.at[k, a], sibling)
                cp.start()
                passes.append(cp)
        for k, (cx, cy) in enumerate(others):
            j = 2 * cx + cy
            for a in range(n):
                _remote(srcs[a].at[c], dsts[a].at[j, 1 - c], pass_send.at[k, a], pass_recv.at[k, a], sibling).wait_recv()
        for cp in sends + passes:
            cp.wait_send()

    return pl.pallas_call(
        body,
        in_specs=[_ANY] * n,
        out_specs=[_ANY] * n,
        out_shape=[jax.ShapeDtypeStruct((NCHIP,) + s.shape, s.dtype) for s in shards],
        scratch_shapes=[pltpu.SemaphoreType.DMA((3, n))] * 4,
        name="gather_weights",
    )(*shards)


def _sibling_halves(bufs, half_major):
    n = len(bufs)
    pieces = [1 if hm else b.shape[0] for b, hm in zip(bufs, half_major)]

    def body(*refs):
        srcs, dsts = refs[:n], refs[n:2 * n]
        send_sems, recv_sems = refs[2 * n:]
        x, y, c, _, _ = _place()
        cps = []
        for a in range(n):
            if half_major[a]:
                cps.append(_remote(srcs[a].at[1 - c], dsts[a], send_sems.at[a, 0], recv_sems.at[a, 0], (x, y, 1 - c)))
            else:
                cps += [_remote(srcs[a].at[j, 1 - c], dsts[a].at[j], send_sems.at[a, j], recv_sems.at[a, j],
                                (x, y, 1 - c)) for j in range(pieces[a])]
        for cp in cps:
            cp.start()
        for cp in cps:
            cp.wait()

    def landed(b, hm):
        return b.shape[1:] if hm else (b.shape[0],) + b.shape[2:]

    return pl.pallas_call(
        body,
        in_specs=[_ANY] * n,
        out_specs=[_ANY] * n,
        out_shape=[jax.ShapeDtypeStruct(landed(b, hm), b.dtype) for b, hm in zip(bufs, half_major)],
        scratch_shapes=[pltpu.SemaphoreType.DMA((n, max(pieces)))] * 2,
        name="sibling_halves",
    )(*bufs)


def _chip_exchange(parts):
    n = len(parts)

    def body(*refs):
        srcs, dsts = refs[:n], refs[n:2 * n]
        send_sems, recv_sems = refs[2 * n:]
        x, y, c, me, others = _place()

        def part(a, j):
            return srcs[a].at[j if parts[a].shape[0] == NCHIP else 0]

        sends = [_remote(part(a, 2 * cx + cy), dsts[a].at[k], send_sems.at[k, a], recv_sems.at[k, a], (cx, cy, c))
                 for k, (cx, cy) in enumerate(others) for a in range(n)]
        for cp in sends:
            cp.start()
        for cp in sends:
            cp.wait()

    return pl.pallas_call(
        body,
        in_specs=[_ANY] * n,
        out_specs=[_ANY] * n,
        out_shape=[jax.ShapeDtypeStruct((3,) + p.shape[1:], p.dtype) for p in parts],
        scratch_shapes=[pltpu.SemaphoreType.DMA((3, n))] * 2,
        name="chip_exchange",
    )(*parts)


def _sibling_swap(halves):
    n = len(halves)

    def body(*refs):
        srcs, dsts = refs[:n], refs[n:2 * n]
        send_sems, recv_sems = refs[2 * n:]
        x, y, c, _, _ = _place()
        cps = [_remote(srcs[a], dsts[a], send_sems.at[a], recv_sems.at[a], (x, y, 1 - c)) for a in range(n)]
        for cp in cps:
            cp.start()
        for cp in cps:
            cp.wait()

    return pl.pallas_call(
        body,
        in_specs=[_ANY] * n,
        out_specs=[_ANY] * n,
        out_shape=[jax.ShapeDtypeStruct(s.shape, s.dtype) for s in halves],
        scratch_shapes=[pltpu.SemaphoreType.DMA((n,))] * 2,
        name="sibling_swap",
    )(*halves)


TILE_BYTES = 1 << 20


def _tile(h, w):
    if h % 128 == 0:
        return 128, w
    if h * w * 4 <= TILE_BYTES:
        return h, w
    return h, 128


def _pair_sum(place, buf, half_major, got, out_dtype):
    nj, h, w = got.shape
    th, tw = _tile(h, w)

    def body(p_ref, a_ref, b_ref, o_ref):
        del p_ref
        o_ref[...] = (a_ref[...] + b_ref[...]).astype(out_dtype)

    if half_major:
        mine = pl.BlockSpec((None, None, th, tw), lambda j, r, q, p: (p[0], j, r, q))
    else:
        mine = pl.BlockSpec((None, None, th, tw), lambda j, r, q, p: (j, p[0], r, q))
    return pl.pallas_call(
        body,
        grid_spec=pltpu.PrefetchScalarGridSpec(
            num_scalar_prefetch=1,
            grid=(nj, h // th, w // tw),
            in_specs=[mine, pl.BlockSpec((None, th, tw), lambda j, r, q, p: (j, r, q))],
            out_specs=pl.BlockSpec((None, th, tw), lambda j, r, q, p: (j, r, q)),
        ),
        out_shape=jax.ShapeDtypeStruct((nj, h, w), out_dtype),
        compiler_params=_params(("parallel", "parallel", "parallel")),
        name="pair_sum",
    )(place, buf, got)


def _chip_sum(place, part, slots):
    nj, h, w = part.shape
    th, tw = _tile(h, w)

    def body(p_ref, own_ref, s_ref, o_ref):
        me = p_ref[1]
        own = own_ref[...].astype(f32)
        by_flip = {2: s_ref[0].astype(f32), 1: s_ref[1].astype(f32), 3: s_ref[2].astype(f32)}
        acc = None
        for j in range(NCHIP):
            flip = me ^ j
            term = jnp.where(flip == 0, own, jnp.where(flip == 2, by_flip[2], jnp.where(flip == 1, by_flip[1], by_flip[3])))
            acc = term if acc is None else acc + term
        o_ref[...] = acc

    return pl.pallas_call(
        body,
        grid_spec=pltpu.PrefetchScalarGridSpec(
            num_scalar_prefetch=1,
            grid=(h // th, w // tw),
            in_specs=[pl.BlockSpec((None, th, tw), lambda r, q, p: (p[1] if nj == NCHIP else 0, r, q)),
                      pl.BlockSpec((3, th, tw), lambda r, q, p: (0, r, q))],
            out_specs=pl.BlockSpec((th, tw), lambda r, q, p: (r, q)),
        ),
        out_shape=jax.ShapeDtypeStruct((h, w), f32),
        compiler_params=_params(("parallel", "parallel")),
        name="chip_sum",
    )(place, part, slots)


def _adamw_math(w, g, m, v):
    nm = ADAM_B1 * m + (1.0 - ADAM_B1) * g
    nv = ADAM_B2 * v + (1.0 - ADAM_B2) * (g * g)
    m_hat = nm / (1.0 - ADAM_B1 ** ADAM_STEP)
    v_hat = nv / (1.0 - ADAM_B2 ** ADAM_STEP)
    return -ADAM_LR * (m_hat / (jnp.sqrt(v_hat) + ADAM_EPS) + ADAM_WD * w), nm, nv


def _adamw(w, g, m, v):
    rows, width = w.shape
    th, tw = _tile(rows, width)

    def body(w_ref, g_ref, m_ref, v_ref, d_ref, nm_ref, nv_ref):
        d_ref[...], nm_ref[...], nv_ref[...] = _adamw_math(w_ref[...], g_ref[...], m_ref[...], v_ref[...])

    spec = pl.BlockSpec((th, tw), lambda r, q: (r, q))
    return pl.pallas_call(
        body,
        grid=(rows // th, width // tw),
        in_specs=[spec] * 4,
        out_specs=[spec] * 3,
        out_shape=[jax.ShapeDtypeStruct((rows, width), f32)] * 3,
        compiler_params=_params(("parallel", "parallel")),
        name="adamw",
    )(w, g, m, v)


def _adamw_halves(place, w, mine, got, m, v, axis):
    rows, width = w.shape
    h, hw = mine.shape
    th, tw = _tile(h, hw)
    nr, nq = h // th, hw // tw

    def body(p_ref, w_ref, a_ref, b_ref, m_ref, v_ref, g_ref, d_ref, nm_ref, nv_ref):
        g = jnp.where(pl.program_id(0) == p_ref[0], a_ref[...], b_ref[...])
        g_ref[...] = g
        d_ref[...], nm_ref[...], nv_ref[...] = _adamw_math(w_ref[...], g, m_ref[...], v_ref[...])

    if axis == 0:
        full = pl.BlockSpec((th, tw), lambda e, r, q, p: (e * nr + r, q))
    else:
        full = pl.BlockSpec((th, tw), lambda e, r, q, p: (r, e * nq + q))
    half = pl.BlockSpec((th, tw), lambda e, r, q, p: (r, q))
    return pl.pallas_call(
        body,
        grid_spec=pltpu.PrefetchScalarGridSpec(
            num_scalar_prefetch=1,
            grid=(2, nr, nq),
            in_specs=[full, half, half, full, full],
            out_specs=[full] * 4,
        ),
        out_shape=[jax.ShapeDtypeStruct((rows, width), f32)] * 4,
        compiler_params=_params(("parallel", "parallel", "parallel")),
        name="adamw_halves",
    )(place, w, mine, got, m, v)


def _reduce_gradients(place, bufs, half_major):
    got = _sibling_halves(bufs, half_major)
    parts = [_pair_sum(place, b, hm, r, bf16 if r.shape[0] == NCHIP else f32) for b, hm, r in zip(bufs, half_major, got)]
    slots = _chip_exchange(parts)
    mine = [_chip_sum(place, p, s) for p, s in zip(parts, slots)]
    theirs = _sibling_swap(mine)
    return mine, theirs


_SMALL = (("norm_g", 8), ("ln_v_g", 8), ("ln_v_b", 8), ("w_spatial", 1024), ("b_spatial", 8), ("b_gate_up", 4),
          ("gla_norm_g", 2), ("final_norm_g", 8), ("w_gate_up", 64))
_SMALL_ROWS = 1136
WIN_SHARD = 2052
LR_COL = 6144


def _pack_rows(arrays, rows):
    flat = jnp.concatenate([a.reshape(-1, 128) for a in arrays], axis=0)
    return jnp.pad(flat, ((0, rows - flat.shape[0]), (0, 0)))


def kernel(x, norm_g, w_in, ln_v_g, ln_v_b, w_spatial, b_spatial, w_gate_up, b_gate_up, gla_norm_g, w_branch_a, w_branch_b, w_out, final_norm_g, loss_target, m_norm_g, m_w_in, m_ln_v_g, m_ln_v_b, m_w_spatial, m_b_spatial, m_w_gate_up, m_b_gate_up, m_gla_norm_g, m_w_branch_a, m_w_branch_b, m_w_out, m_final_norm_g, v_norm_g, v_w_in, v_ln_v_g, v_ln_v_b, v_w_spatial, v_b_spatial, v_w_gate_up, v_b_gate_up, v_gla_norm_g, v_w_branch_a, v_w_branch_b, v_w_out, v_final_norm_g):
    chip = 2 * lax.axis_index("x") + lax.axis_index("y")
    core = lax.axis_index("c")
    place = jnp.stack([core, chip]).astype(jnp.int32)
    mat_names = ("w_branch_a", "w_branch_b", "w_out")

    wt_halves = jnp.transpose(jnp.transpose(w_in[0]).astype(bf16).reshape(WIN_SHARD, 2, D // 2), (1, 0, 2))
    mats = [w[0].astype(bf16).reshape(2, D // NCHIP // 2, D) for w in (w_branch_a, w_branch_b, w_out)]
    gate_sh = w_gate_up[0].reshape(2, RANK // 2, 128)
    g_win, g_a, g_b, g_o, g_gate = _gather_weights([wt_halves] + mats + [gate_sh])

    def with_own(gathered, own):
        mine = (jnp.arange(NCHIP) == chip).reshape((NCHIP,) + (1,) * own.ndim)
        return jnp.where(mine, own[None], gathered)

    w_full_t = jnp.transpose(with_own(g_win, wt_halves), (0, 2, 1, 3)).reshape(NCHIP * WIN_SHARD, D)
    w_main_t = jnp.concatenate([w_full_t[:LR_COL], w_full_t[LR_COL + RANK:]], axis=0)
    w_lr_t = jnp.pad(w_full_t[LR_COL:LR_COL + RANK], ((0, LRP - RANK), (0, 0)))
    w_a, w_b, w_o = (with_own(g, own).reshape(D, D) for g, own in zip((g_a, g_b, g_o), mats))
    w_gate = jnp.transpose(with_own(g_gate, gate_sh).reshape(NCHIP, RANK, 128), (1, 0, 2)).reshape(RANK, KEYB)
    w_gate = jnp.pad(w_gate, ((0, LRP - RANK), (0, 0)))
    b_sb = jnp.broadcast_to(b_spatial[0][:, :, None], (HA, CA, GA))

    r = _local_step(x[0], loss_target[0], norm_g, w_main_t, w_lr_t, ln_v_g, ln_v_b, w_spatial[0], b_sb,
                    w_gate, b_gate_up, gla_norm_g, w_a, w_b, w_o, final_norm_g.reshape(1, D))
    loss = lax.psum(jnp.sum(r["loss_cols"]), ("x", "y", "c"))

    dwm_t, dwl_t = r["dwm"], r["dwl"]
    dw_full_t = jnp.concatenate([dwm_t[:LR_COL], dwl_t[:RANK], dwm_t[LR_COL:]], axis=0)
    b_win = jnp.transpose(dw_full_t.reshape(NCHIP, WIN_SHARD, 2, D // 2), (2, 0, 1, 3))
    b_mats = [r[k].reshape(NCHIP, 2, D // NCHIP // 2, D) for k in ("dwa", "dwb", "dwo")]
    small = _pack_rows([r["dg0"], r["dlg"], r["dlb"], r["dws"], r["dbs"][:, :, 0], r["dbg"], r["dgg"], r["dgf"],
                        r["dwg"][:RANK]], _SMALL_ROWS)
    b_small = small.reshape(2, 1, _SMALL_ROWS // 2, 128)
    mine, theirs = _reduce_gradients(place, [b_win] + b_mats + [b_small], [True, False, False, False, True])

    g_small = jnp.where(core == 0, jnp.concatenate([mine[4], theirs[4]], axis=0),
                        jnp.concatenate([theirs[4], mine[4]], axis=0))
    grads = {}
    row = 0
    for name, rows in _SMALL:
        grads[name] = g_small[row:row + rows]
        row += rows
    dwg_full = grads["w_gate_up"].reshape(RANK, KEYB)
    grads["w_gate_up"] = lax.dynamic_slice_in_dim(dwg_full, chip * 128, 128, axis=1)

    weights = dict(norm_g=norm_g, w_in=w_in, ln_v_g=ln_v_g, ln_v_b=ln_v_b, w_spatial=w_spatial, b_spatial=b_spatial,
                   w_gate_up=w_gate_up, b_gate_up=b_gate_up, gla_norm_g=gla_norm_g, w_branch_a=w_branch_a,
                   w_branch_b=w_branch_b, w_out=w_out, final_norm_g=final_norm_g)
    m_in = dict(norm_g=m_norm_g, w_in=m_w_in, ln_v_g=m_ln_v_g, ln_v_b=m_ln_v_b, w_spatial=m_w_spatial,
                b_spatial=m_b_spatial, w_gate_up=m_w_gate_up, b_gate_up=m_b_gate_up, gla_norm_g=m_gla_norm_g,
                w_branch_a=m_w_branch_a, w_branch_b=m_w_branch_b, w_out=m_w_out, final_norm_g=m_final_norm_g)
    v_in = dict(norm_g=v_norm_g, w_in=v_w_in, ln_v_g=v_ln_v_g, ln_v_b=v_ln_v_b, w_spatial=v_w_spatial,
                b_spatial=v_b_spatial, w_gate_up=v_w_gate_up, b_gate_up=v_b_gate_up, gla_norm_g=v_gla_norm_g,
                w_branch_a=v_w_branch_a, w_branch_b=v_w_branch_b, w_out=v_w_out, final_norm_g=v_final_norm_g)
    names = list(weights)
    small_names = [n for n in names if n != "w_in" and n not in mat_names]
    out_g, out_d, out_m, out_v = {}, {}, {}, {}
    res = _adamw_halves(place, jnp.transpose(w_in[0]), mine[0], theirs[0], jnp.transpose(m_w_in[0]),
                        jnp.transpose(v_w_in[0]), axis=1)
    out_g["w_in"], out_d["w_in"], out_m["w_in"], out_v["w_in"] = (jnp.transpose(t)[None] for t in res)
    for i, n in enumerate(mat_names):
        res = _adamw_halves(place, weights[n][0], mine[1 + i], theirs[1 + i], m_in[n][0], v_in[n][0], axis=0)
        out_g[n], out_d[n], out_m[n], out_v[n] = (t[None] for t in res)
    upd_rows = sum(weights[n].size for n in small_names) // 128
    pad_rows = -(-upd_rows // 8) * 8
    packed = [_pack_rows([t[n] for n in small_names], pad_rows) for t in (weights, grads, m_in, v_in)]
    d_s, m_s, v_s = _adamw(*packed)
    row = 0
    for n in small_names:
        shape = weights[n].shape
        rows = weights[n].size // 128
        out_g[n] = grads[n].reshape(shape)
        out_d[n], out_m[n], out_v[n] = (t[row:row + rows].reshape(shape) for t in (d_s, m_s, v_s))
        row += rows
    return (loss, r["dx"][None], *[out_g[n] for n in names], *[out_d[n] for n in names],
            *[out_m[n] for n in names], *[out_v[n] for n in names])
```

```python
import functools
import math

import jax
import jax.numpy as jnp
from jax import lax
from jax.experimental import pallas as pl
from jax.experimental.pallas import tpu as pltpu

f32 = jnp.float32
bf16 = jnp.bfloat16

D = 1024
NMAIN = 8192
LRP = 128
RANK = 16
HA, GA, CA = 8, 128, 128
HB, DK, DV, CB = 4, 128, 256, 64
KEYB = HB * DK
EPS = 1e-6
LN_EPS = 1e-5
GATE_NORM = 16.0
QSCALE = DK ** -0.5
COL_U, COL_V, COL_ZA = 0, 1, 2
COL_Q, COL_K = 6, 7
COL_VB, COL_ZB = 4, 5
COL_GATES = 3
VMEM_LIMIT = 56 * 1024 * 1024
NCHIP = 4
WIN_SHARD = 2052
LR_COL = 6144
_ANY = pl.BlockSpec(memory_space=pl.ANY)

ADAM_LR, ADAM_B1, ADAM_B2, ADAM_EPS, ADAM_WD, ADAM_STEP = 0.001, 0.9, 0.999, 1e-08, 0.01, 10

_SQRT_HALF = 0.7071067811865476
_INV_SQRT_2PI = 0.3989422804014327


def _dot(a, b):
    return jnp.dot(a, b, preferred_element_type=f32)


def _dot_nt(a, b):
    return lax.dot_general(a, b, (((1,), (1,)), ((), ())), preferred_element_type=f32)


def _dot_tn(a, b):
    return lax.dot_general(a, b, (((0,), (0,)), ((), ())), preferred_element_type=f32)


def _dot_exact(a, b):
    return jnp.dot(a, b, preferred_element_type=f32, precision=lax.Precision.HIGHEST)


def _gelu(x):
    return 0.5 * x * (1.0 + lax.erf(x * _SQRT_HALF))


def _gelu_grad(x):
    return 0.5 * (1.0 + lax.erf(x * _SQRT_HALF)) + x * (jnp.exp(-0.5 * x * x) * _INV_SQRT_2PI)


def _sigmoid(x):
    return 1.0 / (1.0 + jnp.exp(-x))


def _params(sem):
    return pltpu.CompilerParams(dimension_semantics=sem, vmem_limit_bytes=VMEM_LIMIT)


def _resident(shape):
    nd = len(shape)
    return pl.BlockSpec(shape, lambda *_: (0,) * nd, pipeline_mode=pl.Buffered(1))


def _proj_fwd(x, g0, w_main, w_lr, tm=1024, tn=512):
    T = x.shape[0]
    tm = min(tm, T)

    def body(x_ref, g_ref, w_ref, wl_ref, proj_ref, lr_ref, h_ref):
        @pl.when(pl.program_id(1) == 0)
        def _():
            xv = x_ref[...]
            r = lax.rsqrt(jnp.mean(xv * xv, axis=-1, keepdims=True) + EPS)
            h = (xv * r * g_ref[...]).astype(bf16)
            h_ref[...] = h
            lr_ref[...] = _dot_nt(h, wl_ref[...])

        proj_ref[...] = _dot_nt(h_ref[...], w_ref[...])

    return pl.pallas_call(
        body,
        grid=(T // tm, NMAIN // tn),
        in_specs=[
            pl.BlockSpec((tm, D), lambda i, j: (i, 0)),
            pl.BlockSpec((1, D), lambda i, j: (0, 0)),
            pl.BlockSpec((tn, D), lambda i, j: (j, 0)),
            pl.BlockSpec((LRP, D), lambda i, j: (0, 0)),
        ],
        out_specs=[
            pl.BlockSpec((tm, tn), lambda i, j: (i, j)),
            pl.BlockSpec((tm, LRP), lambda i, j: (i, 0)),
            pl.BlockSpec((tm, D), lambda i, j: (i, 0)),
        ],
        out_shape=[
            jax.ShapeDtypeStruct((T, NMAIN), f32),
            jax.ShapeDtypeStruct((T, LRP), f32),
            jax.ShapeDtypeStruct((T, D), bf16),
        ],
        compiler_params=_params(("parallel", "arbitrary")),
        name="proj_fwd",
    )(x, g0, w_main, w_lr)


def _causal_mask():
    t = lax.broadcasted_iota(jnp.int32, (CA, CA), 0)
    s = lax.broadcasted_iota(jnp.int32, (CA, CA), 1)
    return s <= t


def _layernorm_parts(gv):
    mu = jnp.mean(gv, axis=-1, keepdims=True)
    xc = gv - mu
    rs = lax.rsqrt(jnp.mean(xc * xc, axis=-1, keepdims=True) + LN_EPS)
    return xc * rs, rs


def _mixer_a_fwd(proj, ln_g, ln_b, w_s, b_sb, tm=256):
    T = proj.shape[0]

    def body(u_ref, v_ref, za_ref, lg_ref, lb_ref, ws_ref, bs_ref, a_ref, vln_s):
        vhat, _ = _layernorm_parts(_gelu(v_ref[...]))
        vln_s[...] = (vhat * lg_ref[...] + lb_ref[...]).astype(bf16)
        mask = _causal_mask()
        for g in range(HA):
            wg = jnp.where(mask, ws_ref[g], 0.0).astype(bf16)
            cols = slice(g * GA, (g + 1) * GA)
            for c in range(tm // CA):
                rows = slice(c * CA, (c + 1) * CA)
                mixed = _dot(wg, vln_s[rows, cols]) + bs_ref[g]
                za = za_ref[rows, cols]
                a = _gelu(u_ref[rows, cols]) * mixed * (za * _sigmoid(za))
                a_ref[rows, cols] = a.astype(bf16)

    def col(cidx):
        return pl.BlockSpec((tm, D), lambda i, c=cidx: (i, c))

    return pl.pallas_call(
        body,
        grid=(T // tm,),
        in_specs=[col(COL_U), col(COL_V), col(COL_ZA), _resident((1, D)), _resident((1, D)),
                  _resident((HA, CA, CA)), _resident((HA, CA, GA))],
        out_specs=pl.BlockSpec((tm, D), lambda i: (i, 0)),
        out_shape=jax.ShapeDtypeStruct((T, D), bf16),
        scratch_shapes=[pltpu.VMEM((tm, D), bf16)],
        compiler_params=_params(("parallel",)),
        name="mixer_a_fwd",
    )(proj, proj, proj, ln_g, ln_b, w_s, b_sb)


def _mixer_a_bwd(proj, da, dproj, ln_g, ln_b, w_s, b_sb, tm=256):
    T = proj.shape[0]
    nsteps = T // tm

    def body(u_ref, v_ref, za_ref, da_ref, dp_in, lg_ref, lb_ref, ws_ref, bs_ref,
             dp_ref, dws_ref, dbs_ref, dlg_ref, dlb_ref, vln_s, dvln_s):
        del dp_in
        i = pl.program_id(0)

        @pl.when(i == 0)
        def _():
            dws_ref[...] = jnp.zeros_like(dws_ref)
            dbs_ref[...] = jnp.zeros_like(dbs_ref)
            dlg_ref[...] = jnp.zeros_like(dlg_ref)
            dlb_ref[...] = jnp.zeros_like(dlb_ref)

        v = v_ref[...]
        vhat, rs = _layernorm_parts(_gelu(v))
        vln_s[...] = (vhat * lg_ref[...] + lb_ref[...]).astype(bf16)
        mask = _causal_mask()
        for g in range(HA):
            wg = jnp.where(mask, ws_ref[g], 0.0).astype(bf16)
            cols = slice(g * GA, (g + 1) * GA)
            dw_acc = jnp.zeros((CA, CA), f32)
            db_acc = jnp.zeros((CA, 1), f32)
            for c in range(tm // CA):
                rows = slice(c * CA, (c + 1) * CA)
                vln = vln_s[rows, cols]
                mixed = _dot(wg, vln) + bs_ref[g]
                u = u_ref[rows, cols]
                za = za_ref[rows, cols]
                da_blk = da_ref[rows, cols]
                sg = _sigmoid(za)
                sz = za * sg
                gu = _gelu(u)
                dp_ref[rows, cols] = (da_blk * mixed * sz * _gelu_grad(u)).astype(bf16)
                dp_ref[rows, 2 * D + g * GA:2 * D + (g + 1) * GA] = (
                    da_blk * gu * mixed * (sg * (1.0 + za * (1.0 - sg)))).astype(bf16)
                dmixed = da_blk * gu * sz
                dmb = dmixed.astype(bf16)
                dvln_s[rows, cols] = _dot_tn(wg, dmb)
                dw_acc = dw_acc + _dot_nt(dmb, vln)
                db_acc = db_acc + jnp.sum(dmixed, axis=-1, keepdims=True)
            dws_ref[g] += dw_acc
            dbs_ref[g] += jnp.broadcast_to(db_acc, (CA, GA))

        dvln = dvln_s[...]
        dlg_ref[...] += jnp.sum(dvln * vhat, axis=0, keepdims=True)
        dlb_ref[...] += jnp.sum(dvln, axis=0, keepdims=True)
        dvhat = dvln * lg_ref[...]
        dgv = rs * (dvhat - jnp.mean(dvhat, axis=-1, keepdims=True)
                    - vhat * jnp.mean(dvhat * vhat, axis=-1, keepdims=True))
        dp_ref[:, D:2 * D] = (dgv * _gelu_grad(v)).astype(bf16)

        @pl.when(i == nsteps - 1)
        def _():
            for g in range(HA):
                dws_ref[g] = jnp.where(mask, dws_ref[g], 0.0)

    def col(cidx):
        return pl.BlockSpec((tm, D), lambda i, c=cidx: (i, c))

    return pl.pallas_call(
        body,
        grid=(nsteps,),
        in_specs=[col(COL_U), col(COL_V), col(COL_ZA), pl.BlockSpec((tm, D), lambda i: (i, 0)),
                  pl.BlockSpec(memory_space=pl.ANY),
                  _resident((1, D)), _resident((1, D)), _resident((HA, CA, CA)), _resident((HA, CA, GA))],
        out_specs=[pl.BlockSpec((tm, 3 * D), lambda i: (i, 0)),
                   _resident((HA, CA, CA)), _resident((HA, CA, GA)), _resident((1, D)), _resident((1, D))],
        out_shape=[jax.ShapeDtypeStruct(dproj.shape, dproj.dtype),
                   jax.ShapeDtypeStruct((HA, CA, CA), f32), jax.ShapeDtypeStruct((HA, CA, GA), f32),
                   jax.ShapeDtypeStruct((1, D), f32), jax.ShapeDtypeStruct((1, D), f32)],
        scratch_shapes=[pltpu.VMEM((tm, D), bf16), pltpu.VMEM((tm, D), f32)],
        input_output_aliases={4: 0},
        compiler_params=_params(("arbitrary",)),
        name="mixer_a_bwd",
    )(proj, proj, proj, da, dproj, ln_g, ln_b, w_s, b_sb)


def _tri(n, upper):
    r = lax.broadcasted_iota(jnp.int32, (n, n), 0)
    c = lax.broadcasted_iota(jnp.int32, (n, n), 1)
    return jnp.where((c >= r) if upper else (c <= r), 1.0, 0.0).astype(f32)


def _log_alpha(lr, wg, bg):
    logit = _dot(lr.astype(bf16), wg.astype(bf16)) + bg
    la = (jnp.minimum(logit, 0.0) - jnp.log1p(jnp.exp(-jnp.abs(logit)))) * (1.0 / GATE_NORM)
    return logit, la


def _gla_fwd(proj, lr, w_gate, b_gate, gla_g, tm=256):
    T = proj.shape[0]
    nchunk = T // CB
    cpb = tm // CB

    def body(q_ref, k_ref, v_ref, zb_ref, lr_ref, wg_ref, bg_ref, gg_ref,
             o_ref, ob_ref, st_ref, state, la_s):
        @pl.when(pl.program_id(0) == 0)
        def _():
            state[...] = jnp.zeros_like(state)

        _, la = _log_alpha(lr_ref[...], wg_ref[...], bg_ref[...])
        la_s[...] = la
        ltri = _tri(CB, upper=False)
        causal = ltri > 0.5
        for c in range(cpb):
            rows = slice(c * CB, (c + 1) * CB)
            for hd in range(HB):
                kc = slice(hd * DK, (hd + 1) * DK)
                vc = slice(hd * DV, (hd + 1) * DV)
                b = _dot_exact(ltri, la_s[rows, kc])
                bl = b[CB - 1:CB, :]
                bm = b[CB // 2 - 1:CB // 2, :]
                q = q_ref[rows, kc] * QSCALE
                k = k_ref[rows, kc]
                v = v_ref[rows, vc].astype(bf16)
                qi = (q * jnp.exp(b - bm)).astype(bf16)
                ki = (k * jnp.exp(bm - b)).astype(bf16)
                qe = (q * jnp.exp(b)).astype(bf16)
                ks = (k * jnp.exp(bl - b)).astype(bf16)
                p = jnp.where(causal, _dot_nt(qi, ki), 0.0).astype(bf16)
                s0 = state[hd]
                st_ref[c, hd] = s0
                o = _dot(p, v) + _dot_nt(qe, s0.astype(bf16))
                state[hd] = s0 * jnp.exp(bl) + _dot_tn(v, ks)
                o_ref[rows, vc] = o
                ro = lax.rsqrt(jnp.mean(o * o, axis=-1, keepdims=True) + EPS)
                zb = zb_ref[rows, vc]
                ob_ref[rows, vc] = (o * ro * gg_ref[...] * (zb * _sigmoid(zb))).astype(bf16)

    return pl.pallas_call(
        body,
        grid=(T // tm,),
        in_specs=[pl.BlockSpec((tm, KEYB), lambda i: (i, COL_Q)),
                  pl.BlockSpec((tm, KEYB), lambda i: (i, COL_K)),
                  pl.BlockSpec((tm, D), lambda i: (i, COL_VB)),
                  pl.BlockSpec((tm, D), lambda i: (i, COL_ZB)),
                  pl.BlockSpec((tm, LRP), lambda i: (i, 0)),
                  _resident((LRP, KEYB)), _resident((1, KEYB)), _resident((1, DV))],
        out_specs=[pl.BlockSpec((tm, D), lambda i: (i, 0)),
                   pl.BlockSpec((tm, D), lambda i: (i, 0)),
                   pl.BlockSpec((cpb, HB, DV, DK), lambda i: (i, 0, 0, 0))],
        out_shape=[jax.ShapeDtypeStruct((T, D), f32), jax.ShapeDtypeStruct((T, D), bf16),
                   jax.ShapeDtypeStruct((nchunk, HB, DV, DK), f32)],
        scratch_shapes=[pltpu.VMEM((HB, DV, DK), f32), pltpu.VMEM((tm, KEYB), f32)],
        compiler_params=_params(("arbitrary",)),
        name="gla_fwd",
    )(proj, proj, proj, proj, lr, w_gate, b_gate, gla_g)


def _gla_bwd(proj, lr, o, states, dob, dproj, w_gate, b_gate, gla_g, tm=256):
    T = proj.shape[0]
    cpb = tm // CB
    nb = T // tm

    def body(q_ref, k_ref, v_ref, zb_ref, lr_ref, o_ref, st_ref, dob_ref, dp_in, wg_ref, bg_ref, gg_ref,
             dp_ref, dlr_ref, dwg_ref, dbg_ref, dgg_ref, dstate, la_s, dlogit_s):
        del dp_in
        step = pl.program_id(0)

        @pl.when(step == 0)
        def _():
            dstate[...] = jnp.zeros_like(dstate)
            dwg_ref[...] = jnp.zeros_like(dwg_ref)
            dbg_ref[...] = jnp.zeros_like(dbg_ref)
            dgg_ref[...] = jnp.zeros_like(dgg_ref)

        lr_v = lr_ref[...]
        logit, la = _log_alpha(lr_v, wg_ref[...], bg_ref[...])
        la_s[...] = la
        ltri = _tri(CB, upper=False)
        utri = _tri(CB, upper=True)
        causal = ltri > 0.5
        gg = gg_ref[...]
        dgg_acc = jnp.zeros((1, DV), f32)
        for c in reversed(range(cpb)):
            rows = slice(c * CB, (c + 1) * CB)
            for hd in range(HB):
                kc = slice(hd * DK, (hd + 1) * DK)
                vc = slice(hd * DV, (hd + 1) * DV)
                o_h = o_ref[rows, vc]
                ro = lax.rsqrt(jnp.mean(o_h * o_h, axis=-1, keepdims=True) + EPS)
                ohat = o_h * ro
                zb = zb_ref[rows, vc]
                sg = _sigmoid(zb)
                dob_h = dob_ref[rows, vc]
                don = dob_h * (zb * sg)
                dp_ref[rows, 2 * D + hd * DV:2 * D + (hd + 1) * DV] = (
                    dob_h * ohat * gg * (sg * (1.0 + zb * (1.0 - sg)))).astype(bf16)
                dgg_acc = dgg_acc + jnp.sum(don * ohat, axis=0, keepdims=True)
                dohat = don * gg
                do = (ro * (dohat - ohat * jnp.mean(dohat * ohat, axis=-1, keepdims=True))).astype(bf16)
                b = _dot_exact(ltri, la_s[rows, kc])
                bl = b[CB - 1:CB, :]
                bm = b[CB // 2 - 1:CB // 2, :]
                e_b, e_qm, e_km, e_ks, e_l = jnp.exp(b), jnp.exp(b - bm), jnp.exp(bm - b), jnp.exp(bl - b), jnp.exp(bl)
                q = q_ref[rows, kc] * QSCALE
                k = k_ref[rows, kc]
                v = v_ref[rows, vc].astype(bf16)
                qi = (q * e_qm).astype(bf16)
                ki = (k * e_km).astype(bf16)
                qe = (q * e_b).astype(bf16)
                ks_f = k * e_ks
                ks = ks_f.astype(bf16)
                p = jnp.where(causal, _dot_nt(qi, ki), 0.0).astype(bf16)
                s0 = st_ref[c, hd]
                ds = dstate[hd]
                ds_b = ds.astype(bf16)
                dv = _dot_tn(p, do) + _dot_nt(ks, ds_b)
                dpm = jnp.where(causal, _dot_nt(do, v), 0.0).astype(bf16)
                dqi = _dot(dpm, ki)
                dki = _dot_tn(dpm, qi)
                dqe = _dot(do, s0.astype(bf16))
                dks = _dot(v, ds_b)
                dq_s = dqi * e_qm + dqe * e_b
                dk = dki * e_km + dks * e_ks
                tail = (jnp.sum(dks * ks_f, axis=0, keepdims=True)
                        + e_l * jnp.sum(ds * s0, axis=0, keepdims=True))
                dg = _dot_exact(utri, dq_s * q - dk * k) + tail
                dstate[hd] = _dot_tn(do, qe) + ds * e_l
                dp_ref[rows, kc] = (dq_s * QSCALE).astype(bf16)
                dp_ref[rows, KEYB + hd * DK:KEYB + (hd + 1) * DK] = dk.astype(bf16)
                dp_ref[rows, D + hd * DV:D + (hd + 1) * DV] = dv.astype(bf16)
                dlogit_s[rows, kc] = dg * (1.0 / GATE_NORM)
        dgg_ref[...] += dgg_acc
        dlogit = dlogit_s[...] * _sigmoid(-logit)
        dbg_ref[...] += jnp.sum(dlogit, axis=0, keepdims=True)
        dlb = dlogit.astype(bf16)
        dlr_ref[...] = _dot_nt(dlb, wg_ref[...].astype(bf16)).astype(bf16)
        dwg_ref[...] += _dot_tn(lr_v.astype(bf16), dlb)

    def rev(cidx):
        return lambda i, c=cidx: (nb - 1 - i, c)

    return pl.pallas_call(
        body,
        grid=(nb,),
        in_specs=[pl.BlockSpec((tm, KEYB), rev(COL_Q)),
                  pl.BlockSpec((tm, KEYB), rev(COL_K)),
                  pl.BlockSpec((tm, D), rev(COL_VB)),
                  pl.BlockSpec((tm, D), rev(COL_ZB)),
                  pl.BlockSpec((tm, LRP), rev(0)),
                  pl.BlockSpec((tm, D), rev(0)),
                  pl.BlockSpec((cpb, HB, DV, DK), lambda i: (nb - 1 - i, 0, 0, 0)),
                  pl.BlockSpec((tm, D), rev(0)),
                  pl.BlockSpec(memory_space=pl.ANY),
                  _resident((LRP, KEYB)), _resident((1, KEYB)), _resident((1, DV))],
        out_specs=[pl.BlockSpec((tm, 3 * D), rev(1)),
                   pl.BlockSpec((tm, LRP), rev(0)),
                   _resident((LRP, KEYB)), _resident((1, KEYB)), _resident((1, DV))],
        out_shape=[jax.ShapeDtypeStruct(dproj.shape, dproj.dtype),
                   jax.ShapeDtypeStruct((T, LRP), bf16),
                   jax.ShapeDtypeStruct((LRP, KEYB), f32), jax.ShapeDtypeStruct((1, KEYB), f32),
                   jax.ShapeDtypeStruct((1, DV), f32)],
        scratch_shapes=[pltpu.VMEM((HB, DV, DK), f32), pltpu.VMEM((tm, KEYB), f32), pltpu.VMEM((tm, KEYB), f32)],
        input_output_aliases={8: 0},
        compiler_params=_params(("arbitrary",)),
        name="gla_bwd",
    )(proj, proj, proj, proj, lr, o, states, dob, dproj, w_gate, b_gate, gla_g)


def _merge_fwd_bwd(x, tgt, proj, a, ob, w_a, w_b, w_o, g_f, tm=256):
    T = x.shape[0]

    def body(x_ref, t_ref, gt_ref, a_ref, ob_ref, wa_ref, wb_ref, wo_ref, gf_ref,
             dp_ref, dy_ref, da_ref, dob_ref, dwa_ref, dwb_ref, dwo_ref, dgf_ref, loss_ref):
        @pl.when(pl.program_id(0) == 0)
        def _():
            dwa_ref[...] = jnp.zeros_like(dwa_ref)
            dwb_ref[...] = jnp.zeros_like(dwb_ref)
            dwo_ref[...] = jnp.zeros_like(dwo_ref)
            dgf_ref[...] = jnp.zeros_like(dgf_ref)
            loss_ref[...] = jnp.zeros_like(loss_ref)

        ga = _sigmoid(gt_ref[:, :D])
        gb = _sigmoid(gt_ref[:, D:])
        a_v = a_ref[...]
        ob_v = ob_ref[...]
        pa = _dot(a_v, wa_ref[...])
        pb = _dot(ob_v, wb_ref[...])
        mb = (ga * pa + gb * pb).astype(bf16)
        y = x_ref[...] + _dot(mb, wo_ref[...])
        r1 = lax.rsqrt(jnp.mean(y * y, axis=-1, keepdims=True) + EPS)
        yhat = y * r1
        gf = gf_ref[...]
        err = yhat * gf - t_ref[...]
        loss_ref[...] += jnp.sum(err * err, axis=0, keepdims=True) * (0.5 / D)
        dout = err * (1.0 / D)
        dgf_ref[...] += jnp.sum(dout * yhat, axis=0, keepdims=True)
        dyn = dout * gf
        dy = r1 * (dyn - yhat * jnp.mean(dyn * yhat, axis=-1, keepdims=True))
        dy_ref[...] = dy
        dyb = dy.astype(bf16)
        dwo_ref[...] += _dot_tn(mb, dyb)
        dm = _dot_nt(dyb, wo_ref[...])
        dpa = (dm * ga).astype(bf16)
        dpb = (dm * gb).astype(bf16)
        dp_ref[:, :D] = (dm * pa * ga * (1.0 - ga)).astype(bf16)
        dp_ref[:, D:] = (dm * pb * gb * (1.0 - gb)).astype(bf16)
        dwa_ref[...] += _dot_tn(a_v, dpa)
        dwb_ref[...] += _dot_tn(ob_v, dpb)
        da_ref[...] = _dot_nt(dpa, wa_ref[...])
        dob_ref[...] = _dot_nt(dpb, wb_ref[...])

    row = lambda: pl.BlockSpec((tm, D), lambda i: (i, 0))
    return pl.pallas_call(
        body,
        grid=(T // tm,),
        in_specs=[row(), row(), pl.BlockSpec((tm, 2 * D), lambda i: (i, COL_GATES)), row(), row(),
                  _resident((D, D)), _resident((D, D)), _resident((D, D)), _resident((1, D))],
        out_specs=[pl.BlockSpec((tm, 2 * D), lambda i: (i, COL_GATES)), row(), row(), row(),
                   _resident((D, D)), _resident((D, D)), _resident((D, D)), _resident((1, D)), _resident((1, D))],
        out_shape=[jax.ShapeDtypeStruct((T, NMAIN), bf16),
                   jax.ShapeDtypeStruct((T, D), f32), jax.ShapeDtypeStruct((T, D), f32),
                   jax.ShapeDtypeStruct((T, D), f32),
                   jax.ShapeDtypeStruct((D, D), f32), jax.ShapeDtypeStruct((D, D), f32),
                   jax.ShapeDtypeStruct((D, D), f32),
                   jax.ShapeDtypeStruct((1, D), f32), jax.ShapeDtypeStruct((1, D), f32)],
        compiler_params=_params(("arbitrary",)),
        name="merge_fwd_bwd",
    )(x, tgt, proj, a, ob, w_a, w_b, w_o, g_f)


def _dx_bwd(x, dy, dproj, dlr, g0, w_main, w_lr, tm=256):
    T = x.shape[0]

    def body(x_ref, dy_ref, dp_ref, dl_ref, g_ref, w_ref, wl_ref, dx_ref, dg_ref, dwl_ref):
        @pl.when(pl.program_id(0) == 0)
        def _():
            dg_ref[...] = jnp.zeros_like(dg_ref)
            dwl_ref[...] = jnp.zeros_like(dwl_ref)

        xv = x_ref[...]
        r = lax.rsqrt(jnp.mean(xv * xv, axis=-1, keepdims=True) + EPS)
        xhat = xv * r
        g = g_ref[...]
        dl = dl_ref[...]
        dh = _dot(dp_ref[...], w_ref[...]) + _dot(dl, wl_ref[...])
        dg_ref[...] += jnp.sum(dh * xhat, axis=0, keepdims=True)
        t = dh * g
        dx_ref[...] = dy_ref[...] + r * (t - xhat * jnp.mean(t * xhat, axis=-1, keepdims=True))
        dwl_ref[...] += _dot_tn(dl, (xhat * g).astype(bf16))

    row = lambda: pl.BlockSpec((tm, D), lambda i: (i, 0))
    return pl.pallas_call(
        body,
        grid=(T // tm,),
        in_specs=[row(), row(), pl.BlockSpec((tm, NMAIN), lambda i: (i, 0)),
                  pl.BlockSpec((tm, LRP), lambda i: (i, 0)),
                  _resident((1, D)), _resident((NMAIN, D)), _resident((LRP, D))],
        out_specs=[row(), _resident((1, D)), _resident((LRP, D))],
        out_shape=[jax.ShapeDtypeStruct((T, D), f32), jax.ShapeDtypeStruct((1, D), f32),
                   jax.ShapeDtypeStruct((LRP, D), f32)],
        compiler_params=_params(("arbitrary",)),
        name="dx_bwd",
    )(x, dy, dproj, dlr, g0, w_main, w_lr)


def _dw_in(h, dproj, dwl, tm=512, tn=512):
    T = h.shape[0]
    tm = min(tm, T)
    nj, nk = NMAIN // tn, T // tm
    lr_tile = LR_COL // tn

    def body(h_ref, dp_ref, dwl_ref, out_ref, acc, sems, lr_sem):
        j, k = pl.program_id(0), pl.program_id(1)
        slot = j % 2

        def tile_out(jj, s):
            row = pl.multiple_of(jj * tn + jnp.where(jj >= lr_tile, RANK, 0), 8)
            return pltpu.make_async_copy(acc.at[s], out_ref.at[pl.ds(row, tn)], sems.at[s])

        lr_rows = pltpu.make_async_copy(dwl_ref.at[pl.ds(0, RANK)], out_ref.at[pl.ds(LR_COL, RANK)], lr_sem)

        @pl.when((j == 0) & (k == 0))
        def _():
            lr_rows.start()

        @pl.when(k == 0)
        def _():
            acc[slot] = jnp.zeros((tn, D), f32)

        acc[slot] += _dot_tn(dp_ref[...], h_ref[...])

        @pl.when(k == nk - 1)
        def _():
            tile_out(j, slot).start()

            @pl.when(j > 0)
            def _():
                tile_out(j - 1, 1 - slot).wait()

            @pl.when(j == nj - 1)
            def _():
                tile_out(j, slot).wait()
                lr_rows.wait()

    return pl.pallas_call(
        body,
        grid=(nj, nk),
        in_specs=[pl.BlockSpec((tm, D), lambda j, k: (k, 0)), pl.BlockSpec((tm, tn), lambda j, k: (k, j)), _ANY],
        out_specs=_ANY,
        out_shape=jax.ShapeDtypeStruct((NMAIN + RANK, D), f32),
        scratch_shapes=[pltpu.VMEM((2, tn, D), f32), pltpu.SemaphoreType.DMA((2,)), pltpu.SemaphoreType.DMA],
        compiler_params=_params(("arbitrary", "arbitrary")),
        name="dw_in",
    )(h, dproj, dwl)


def _local_step(x, tgt, g0, w_main, w_lr, ln_g, ln_b, w_s, b_sb, w_gate, b_gate, gla_g, w_a, w_b, w_o, g_f):
    proj, lr, h = _proj_fwd(x, g0, w_main, w_lr)
    a = _mixer_a_fwd(proj, ln_g, ln_b, w_s, b_sb)
    o, ob, states = _gla_fwd(proj, lr, w_gate, b_gate, gla_g)
    dproj, dy, da, dob, dwa, dwb, dwo, dgf, loss_cols = _merge_fwd_bwd(x, tgt, proj, a, ob, w_a, w_b, w_o, g_f)
    dproj, dws, dbs, dlg, dlb = _mixer_a_bwd(proj, da, dproj, ln_g, ln_b, w_s, b_sb)
    dproj, dlr, dwg, dbg, dgg = _gla_bwd(proj, lr, o, states, dob, dproj, w_gate, b_gate, gla_g)
    dx, dg0, dwl = _dx_bwd(x, dy, dproj, dlr, g0, w_main, w_lr)
    dwt = _dw_in(h, dproj, dwl)
    return dict(loss_cols=loss_cols, dx=dx, dg0=dg0, dwt=dwt, dlg=dlg, dlb=dlb, dws=dws, dbs=dbs,
                dwg=dwg, dbg=dbg, dgg=dgg, dwa=dwa, dwb=dwb, dwo=dwo, dgf=dgf)


MESH = pl.DeviceIdType.MESH


def _place():
    x, y, c = lax.axis_index("x"), lax.axis_index("y"), lax.axis_index("c")
    others = [(1 - x, y), (x, 1 - y), (1 - x, 1 - y)]
    return x, y, c, 2 * x + y, others


def _remote(src, dst, send_sem, recv_sem, to):
    return pltpu.make_async_remote_copy(src_ref=src, dst_ref=dst, send_sem=send_sem, recv_sem=recv_sem,
                                        device_id=to, device_id_type=MESH)


def _half(ref, e, by_columns):
    if not by_columns:
        return ref.at[e]
    hw = ref.shape[-1] // 2
    return ref.at[:, pl.ds(pl.multiple_of(e * hw, 128), hw)]


def _gather_weights(shards, by_columns):
    n = len(shards)

    def body(*refs):
        srcs, dsts = refs[:n], refs[n:2 * n]
        send_sems, recv_sems, pass_send, pass_recv = refs[2 * n:]
        x, y, c, me, others = _place()
        sibling = (x, y, 1 - c)

        def src(a, e):
            return _half(srcs[a], e, by_columns[a])

        def dst(a, j, e):
            return _half(dsts[a].at[j], e, by_columns[a])

        sends = [_remote(src(a, c), dst(a, me, c), send_sems.at[k, a], recv_sems.at[k, a], (cx, cy, c))
                 for k, (cx, cy) in enumerate(others) for a in range(n)]
        for cp in sends:
            cp.start()
        passes = []
        for k, (cx, cy) in enumerate(others):
            j = 2 * cx + cy
            for a in range(n):
                _remote(src(a, c), dst(a, j, c), send_sems.at[k, a], recv_sems.at[k, a], (cx, cy, c)).wait_recv()
                cp = _remote(dst(a, j, c), dst(a, j, c), pass_send.at[k, a], pass_recv.at[k, a], sibling)
                cp.start()
                passes.append(cp)
        for k, (cx, cy) in enumerate(others):
            j = 2 * cx + cy
            for a in range(n):
                _remote(src(a, c), dst(a, j, 1 - c), pass_send.at[k, a], pass_recv.at[k, a], sibling).wait_recv()
        for cp in sends + passes:
            cp.wait_send()

    return pl.pallas_call(
        body,
        in_specs=[_ANY] * n,
        out_specs=[_ANY] * n,
        out_shape=[jax.ShapeDtypeStruct((NCHIP,) + s.shape, s.dtype) for s in shards],
        scratch_shapes=[pltpu.SemaphoreType.DMA((3, n))] * 4,
        name="gather_weights",
    )(*shards)


HALF_FIRST, CHIP_FIRST, BY_COLUMNS = "half_first", "chip_first", "by_columns"


def _sibling_halves(bufs, kinds):
    n = len(bufs)
    pieces = [1 if kind == HALF_FIRST else b.shape[0] for b, kind in zip(bufs, kinds)]

    def body(*refs):
        srcs, dsts = refs[:n], refs[n:2 * n]
        send_sems, recv_sems = refs[2 * n:]
        x, y, c, _, _ = _place()
        cps = []
        for a in range(n):
            if kinds[a] == HALF_FIRST:
                cps.append(_remote(srcs[a].at[1 - c], dsts[a], send_sems.at[a, 0], recv_sems.at[a, 0], (x, y, 1 - c)))
            else:
                cps += [_remote(_half(srcs[a].at[j], 1 - c, kinds[a] == BY_COLUMNS), dsts[a].at[j],
                                send_sems.at[a, j], recv_sems.at[a, j], (x, y, 1 - c)) for j in range(pieces[a])]
        for cp in cps:
            cp.start()
        for cp in cps:
            cp.wait()

    def landed(b, kind):
        if kind == HALF_FIRST:
            return b.shape[1:]
        if kind == CHIP_FIRST:
            return (b.shape[0],) + b.shape[2:]
        return b.shape[:2] + (b.shape[2] // 2,)

    return pl.pallas_call(
        body,
        in_specs=[_ANY] * n,
        out_specs=[_ANY] * n,
        out_shape=[jax.ShapeDtypeStruct(landed(b, kind), b.dtype) for b, kind in zip(bufs, kinds)],
        scratch_shapes=[pltpu.SemaphoreType.DMA((n, max(pieces)))] * 2,
        name="sibling_halves",
    )(*bufs)


def _chip_exchange(parts):
    n = len(parts)

    def body(*refs):
        srcs, dsts = refs[:n], refs[n:2 * n]
        send_sems, recv_sems = refs[2 * n:]
        x, y, c, me, others = _place()

        def part(a, j):
            return srcs[a].at[j if parts[a].shape[0] == NCHIP else 0]

        sends = [_remote(part(a, 2 * cx + cy), dsts[a].at[k], send_sems.at[k, a], recv_sems.at[k, a], (cx, cy, c))
                 for k, (cx, cy) in enumerate(others) for a in range(n)]
        for cp in sends:
            cp.start()
        for cp in sends:
            cp.wait()

    return pl.pallas_call(
        body,
        in_specs=[_ANY] * n,
        out_specs=[_ANY] * n,
        out_shape=[jax.ShapeDtypeStruct((3,) + p.shape[1:], p.dtype) for p in parts],
        scratch_shapes=[pltpu.SemaphoreType.DMA((3, n))] * 2,
        name="chip_exchange",
    )(*parts)


def _sibling_swap(halves):
    n = len(halves)

    def body(*refs):
        srcs, dsts = refs[:n], refs[n:2 * n]
        send_sems, recv_sems = refs[2 * n:]
        x, y, c, _, _ = _place()
        cps = [_remote(srcs[a], dsts[a], send_sems.at[a], recv_sems.at[a], (x, y, 1 - c)) for a in range(n)]
        for cp in cps:
            cp.start()
        for cp in cps:
            cp.wait()

    return pl.pallas_call(
        body,
        in_specs=[_ANY] * n,
        out_specs=[_ANY] * n,
        out_shape=[jax.ShapeDtypeStruct(s.shape, s.dtype) for s in halves],
        scratch_shapes=[pltpu.SemaphoreType.DMA((n,))] * 2,
        name="sibling_swap",
    )(*halves)


TILE_BYTES = 1 << 20


def _tile(h, w):
    if h % 128 == 0:
        return 128, w
    if h * w * 4 <= TILE_BYTES:
        return h, w
    return h, 128


def _pair_sum(place, buf, kind, got, out_dtype):
    nj, h, w = got.shape
    th, tw = _tile(h, w)
    nq = w // tw

    def body(p_ref, a_ref, b_ref, o_ref):
        del p_ref
        o_ref[...] = (a_ref[...] + b_ref[...]).astype(out_dtype)

    if kind == HALF_FIRST:
        mine = pl.BlockSpec((None, None, th, tw), lambda j, r, q, p: (p[0], j, r, q))
    elif kind == CHIP_FIRST:
        mine = pl.BlockSpec((None, None, th, tw), lambda j, r, q, p: (j, p[0], r, q))
    else:
        mine = pl.BlockSpec((None, th, tw), lambda j, r, q, p: (j, r, p[0] * nq + q))
    return pl.pallas_call(
        body,
        grid_spec=pltpu.PrefetchScalarGridSpec(
            num_scalar_prefetch=1,
            grid=(nj, h // th, w // tw),
            in_specs=[mine, pl.BlockSpec((None, th, tw), lambda j, r, q, p: (j, r, q))],
            out_specs=pl.BlockSpec((None, th, tw), lambda j, r, q, p: (j, r, q)),
        ),
        out_shape=jax.ShapeDtypeStruct((nj, h, w), out_dtype),
        compiler_params=_params(("parallel", "parallel", "parallel")),
        name="pair_sum",
    )(place, buf, got)


def _chip_sum(place, part, slots):
    nj, h, w = part.shape
    th, tw = _tile(h, w)

    def body(p_ref, own_ref, s_ref, o_ref):
        me = p_ref[1]
        own = own_ref[...].astype(f32)
        by_flip = {2: s_ref[0].astype(f32), 1: s_ref[1].astype(f32), 3: s_ref[2].astype(f32)}
        acc = None
        for j in range(NCHIP):
            flip = me ^ j
            term = jnp.where(flip == 0, own, jnp.where(flip == 2, by_flip[2], jnp.where(flip == 1, by_flip[1], by_flip[3])))
            acc = term if acc is None else acc + term
        o_ref[...] = acc

    return pl.pallas_call(
        body,
        grid_spec=pltpu.PrefetchScalarGridSpec(
            num_scalar_prefetch=1,
            grid=(h // th, w // tw),
            in_specs=[pl.BlockSpec((None, th, tw), lambda r, q, p: (p[1] if nj == NCHIP else 0, r, q)),
                      pl.BlockSpec((3, th, tw), lambda r, q, p: (0, r, q))],
            out_specs=pl.BlockSpec((th, tw), lambda r, q, p: (r, q)),
        ),
        out_shape=jax.ShapeDtypeStruct((h, w), f32),
        compiler_params=_params(("parallel", "parallel")),
        name="chip_sum",
    )(place, part, slots)


def _adamw_math(w, g, m, v):
    nm = ADAM_B1 * m + (1.0 - ADAM_B1) * g
    nv = ADAM_B2 * v + (1.0 - ADAM_B2) * (g * g)
    m_hat = nm / (1.0 - ADAM_B1 ** ADAM_STEP)
    v_hat = nv / (1.0 - ADAM_B2 ** ADAM_STEP)
    return -ADAM_LR * (m_hat / (jnp.sqrt(v_hat) + ADAM_EPS) + ADAM_WD * w), nm, nv


def _adamw(w, g, m, v):
    rows, width = w.shape
    th, tw = _tile(rows, width)

    def body(w_ref, g_ref, m_ref, v_ref, d_ref, nm_ref, nv_ref):
        d_ref[...], nm_ref[...], nv_ref[...] = _adamw_math(w_ref[...], g_ref[...], m_ref[...], v_ref[...])

    spec = pl.BlockSpec((th, tw), lambda r, q: (r, q))
    return pl.pallas_call(
        body,
        grid=(rows // th, width // tw),
        in_specs=[spec] * 4,
        out_specs=[spec] * 3,
        out_shape=[jax.ShapeDtypeStruct((rows, width), f32)] * 3,
        compiler_params=_params(("parallel", "parallel")),
        name="adamw",
    )(w, g, m, v)


def _adamw_halves(place, w, mine, got, m, v, axis):
    rows, width = w.shape
    h, hw = mine.shape
    th, tw = _tile(h, hw)
    nr, nq = h // th, hw // tw

    def body(p_ref, w_ref, a_ref, b_ref, m_ref, v_ref, g_ref, d_ref, nm_ref, nv_ref):
        g = jnp.where(pl.program_id(0) == p_ref[0], a_ref[...], b_ref[...])
        g_ref[...] = g
        d_ref[...], nm_ref[...], nv_ref[...] = _adamw_math(w_ref[...], g, m_ref[...], v_ref[...])

    if axis == 0:
        full = pl.BlockSpec((th, tw), lambda e, r, q, p: (e * nr + r, q))
    else:
        full = pl.BlockSpec((th, tw), lambda e, r, q, p: (r, e * nq + q))
    half = pl.BlockSpec((th, tw), lambda e, r, q, p: (r, q))
    return pl.pallas_call(
        body,
        grid_spec=pltpu.PrefetchScalarGridSpec(
            num_scalar_prefetch=1,
            grid=(2, nr, nq),
            in_specs=[full, half, half, full, full],
            out_specs=[full] * 4,
        ),
        out_shape=[jax.ShapeDtypeStruct((rows, width), f32)] * 4,
        compiler_params=_params(("parallel", "parallel", "parallel")),
        name="adamw_halves",
    )(place, w, mine, got, m, v)


def _reduce_gradients(place, bufs, kinds):
    got = _sibling_halves(bufs, kinds)
    parts = [_pair_sum(place, b, kind, r, bf16 if r.shape[0] == NCHIP else f32) for b, kind, r in zip(bufs, kinds, got)]
    slots = _chip_exchange(parts)
    mine = [_chip_sum(place, p, s) for p, s in zip(parts, slots)]
    theirs = _sibling_swap(mine)
    return mine, theirs


_SMALL = (("norm_g", 8), ("ln_v_g", 8), ("ln_v_b", 8), ("w_spatial", 1024), ("b_spatial", 8), ("b_gate_up", 4),
          ("gla_norm_g", 2), ("final_norm_g", 8), ("w_gate_up", 64))
_SMALL_ROWS = 1136


def _pack_rows(arrays, rows):
    flat = jnp.concatenate([a.reshape(-1, 128) for a in arrays], axis=0)
    return jnp.pad(flat, ((0, rows - flat.shape[0]), (0, 0)))


def kernel(x, norm_g, w_in, ln_v_g, ln_v_b, w_spatial, b_spatial, w_gate_up, b_gate_up, gla_norm_g, w_branch_a, w_branch_b, w_out, final_norm_g, loss_target, m_norm_g, m_w_in, m_ln_v_g, m_ln_v_b, m_w_spatial, m_b_spatial, m_w_gate_up, m_b_gate_up, m_gla_norm_g, m_w_branch_a, m_w_branch_b, m_w_out, m_final_norm_g, v_norm_g, v_w_in, v_ln_v_g, v_ln_v_b, v_w_spatial, v_b_spatial, v_w_gate_up, v_b_gate_up, v_gla_norm_g, v_w_branch_a, v_w_branch_b, v_w_out, v_final_norm_g):
    chip = 2 * lax.axis_index("x") + lax.axis_index("y")
    core = lax.axis_index("c")
    place = jnp.stack([core, chip]).astype(jnp.int32)
    mat_names = ("w_branch_a", "w_branch_b", "w_out")

    wt_shard = jnp.transpose(w_in[0]).astype(bf16)
    mats = [w[0].astype(bf16).reshape(2, D // NCHIP // 2, D) for w in (w_branch_a, w_branch_b, w_out)]
    gate_sh = w_gate_up[0].reshape(2, RANK // 2, 128)
    g_win, g_a, g_b, g_o, g_gate = _gather_weights([wt_shard] + mats + [gate_sh], [True, False, False, False, False])

    def with_own(gathered, own):
        mine = (jnp.arange(NCHIP) == chip).reshape((NCHIP,) + (1,) * own.ndim)
        return jnp.where(mine, own[None], gathered)

    w_full_t = with_own(g_win, wt_shard).reshape(NCHIP * WIN_SHARD, D)
    w_main_t = jnp.concatenate([w_full_t[:LR_COL], w_full_t[LR_COL + RANK:]], axis=0)
    w_lr_t = jnp.pad(w_full_t[LR_COL:LR_COL + RANK], ((0, LRP - RANK), (0, 0)))
    w_a, w_b, w_o = (with_own(g, own).reshape(D, D) for g, own in zip((g_a, g_b, g_o), mats))
    w_gate = jnp.transpose(with_own(g_gate, gate_sh).reshape(NCHIP, RANK, 128), (1, 0, 2)).reshape(RANK, KEYB)
    w_gate = jnp.pad(w_gate, ((0, LRP - RANK), (0, 0)))
    b_sb = jnp.broadcast_to(b_spatial[0][:, :, None], (HA, CA, GA))

    r = _local_step(x[0], loss_target[0], norm_g, w_main_t, w_lr_t, ln_v_g, ln_v_b, w_spatial[0], b_sb,
                    w_gate, b_gate_up, gla_norm_g, w_a, w_b, w_o, final_norm_g.reshape(1, D))
    loss = lax.psum(jnp.sum(r["loss_cols"]), ("x", "y", "c"))

    b_win = r["dwt"].reshape(NCHIP, WIN_SHARD, D)
    b_mats = [r[k].reshape(NCHIP, 2, D // NCHIP // 2, D) for k in ("dwa", "dwb", "dwo")]
    small = _pack_rows([r["dg0"], r["dlg"], r["dlb"], r["dws"], r["dbs"][:, :, 0], r["dbg"], r["dgg"], r["dgf"],
                        r["dwg"][:RANK]], _SMALL_ROWS)
    b_small = small.reshape(2, 1, _SMALL_ROWS // 2, 128)
    mine, theirs = _reduce_gradients(place, [b_win] + b_mats + [b_small],
                                     [BY_COLUMNS, CHIP_FIRST, CHIP_FIRST, CHIP_FIRST, HALF_FIRST])

    g_small = jnp.where(core == 0, jnp.concatenate([mine[4], theirs[4]], axis=0),
                        jnp.concatenate([theirs[4], mine[4]], axis=0))
    grads = {}
    row = 0
    for name, rows in _SMALL:
        grads[name] = g_small[row:row + rows]
        row += rows
    dwg_full = grads["w_gate_up"].reshape(RANK, KEYB)
    grads["w_gate_up"] = lax.dynamic_slice_in_dim(dwg_full, chip * 128, 128, axis=1)

    weights = dict(norm_g=norm_g, w_in=w_in, ln_v_g=ln_v_g, ln_v_b=ln_v_b, w_spatial=w_spatial, b_spatial=b_spatial,
                   w_gate_up=w_gate_up, b_gate_up=b_gate_up, gla_norm_g=gla_norm_g, w_branch_a=w_branch_a,
                   w_branch_b=w_branch_b, w_out=w_out, final_norm_g=final_norm_g)
    m_in = dict(norm_g=m_norm_g, w_in=m_w_in, ln_v_g=m_ln_v_g, ln_v_b=m_ln_v_b, w_spatial=m_w_spatial,
                b_spatial=m_b_spatial, w_gate_up=m_w_gate_up, b_gate_up=m_b_gate_up, gla_norm_g=m_gla_norm_g,
                w_branch_a=m_w_branch_a, w_branch_b=m_w_branch_b, w_out=m_w_out, final_norm_g=m_final_norm_g)
    v_in = dict(norm_g=v_norm_g, w_in=v_w_in, ln_v_g=v_ln_v_g, ln_v_b=v_ln_v_b, w_spatial=v_w_spatial,
                b_spatial=v_b_spatial, w_gate_up=v_w_gate_up, b_gate_up=v_b_gate_up, gla_norm_g=v_gla_norm_g,
                w_branch_a=v_w_branch_a, w_branch_b=v_w_branch_b, w_out=v_w_out, final_norm_g=v_final_norm_g)
    names = list(weights)
    small_names = [n for n in names if n != "w_in" and n not in mat_names]
    out_g, out_d, out_m, out_v = {}, {}, {}, {}
    res = _adamw_halves(place, jnp.transpose(w_in[0]), mine[0], theirs[0], jnp.transpose(m_w_in[0]),
                        jnp.transpose(v_w_in[0]), axis=1)
    out_g["w_in"], out_d["w_in"], out_m["w_in"], out_v["w_in"] = (jnp.transpose(t)[None] for t in res)
    for i, n in enumerate(mat_names):
        res = _adamw_halves(place, weights[n][0], mine[1 + i], theirs[1 + i], m_in[n][0], v_in[n][0], axis=0)
        out_g[n], out_d[n], out_m[n], out_v[n] = (t[None] for t in res)
    upd_rows = sum(weights[n].size for n in small_names) // 128
    pad_rows = -(-upd_rows // 8) * 8
    packed = [_pack_rows([t[n] for n in small_names], pad_rows) for t in (weights, grads, m_in, v_in)]
    d_s, m_s, v_s = _adamw(*packed)
    row = 0
    for n in small_names:
        shape = weights[n].shape
        rows = weights[n].size // 128
        out_g[n] = grads[n].reshape(shape)
        out_d[n], out_m[n], out_v[n] = (t[row:row + rows].reshape(shape) for t in (d_s, m_s, v_s))
        row += rows
    return (loss, r["dx"][None], *[out_g[n] for n in names], *[out_d[n] for n in names],
            *[out_m[n] for n in names], *[out_v[n] for n in names])
```

```python
import functools
import math

import jax
import jax.numpy as jnp
from jax import lax
from jax.experimental import pallas as pl
from jax.experimental.pallas import tpu as pltpu

f32 = jnp.float32
bf16 = jnp.bfloat16

D = 1024
NMAIN = 8192
LRP = 128
RANK = 16
HA, GA, CA = 8, 128, 128
HB, DK, DV, CB = 4, 128, 256, 64
KEYB = HB * DK
EPS = 1e-6
LN_EPS = 1e-5
GATE_NORM = 16.0
QSCALE = DK ** -0.5
COL_U, COL_V, COL_ZA = 0, 1, 2
COL_Q, COL_K = 6, 7
COL_VB, COL_ZB = 4, 5
COL_GATES = 3
VMEM_LIMIT = 56 * 1024 * 1024
NCHIP = 4
WIN_SHARD = 2052
LR_COL = 6144
_ANY = pl.BlockSpec(memory_space=pl.ANY)

ADAM_LR, ADAM_B1, ADAM_B2, ADAM_EPS, ADAM_WD, ADAM_STEP = 0.001, 0.9, 0.999, 1e-08, 0.01, 10

_SQRT_HALF = 0.7071067811865476
_INV_SQRT_2PI = 0.3989422804014327


def _dot(a, b):
    return jnp.dot(a, b, preferred_element_type=f32)


def _dot_nt(a, b):
    return lax.dot_general(a, b, (((1,), (1,)), ((), ())), preferred_element_type=f32)


def _dot_tn(a, b):
    return lax.dot_general(a, b, (((0,), (0,)), ((), ())), preferred_element_type=f32)


def _dot_exact(a, b):
    return jnp.dot(a, b, preferred_element_type=f32, precision=lax.Precision.HIGHEST)


def _gelu(x):
    return 0.5 * x * (1.0 + lax.erf(x * _SQRT_HALF))


def _gelu_grad(x):
    return 0.5 * (1.0 + lax.erf(x * _SQRT_HALF)) + x * (jnp.exp(-0.5 * x * x) * _INV_SQRT_2PI)


def _sigmoid(x):
    return 1.0 / (1.0 + jnp.exp(-x))


def _params(sem):
    return pltpu.CompilerParams(dimension_semantics=sem, vmem_limit_bytes=VMEM_LIMIT)


def _resident(shape):
    nd = len(shape)
    return pl.BlockSpec(shape, lambda *_: (0,) * nd, pipeline_mode=pl.Buffered(1))


def _proj_fwd(x, g0, w_main, w_lr, tm=1024, tn=512):
    T = x.shape[0]
    tm = min(tm, T)

    def body(x_ref, g_ref, w_ref, wl_ref, proj_ref, lr_ref, h_ref):
        @pl.when(pl.program_id(1) == 0)
        def _():
            xv = x_ref[...]
            r = lax.rsqrt(jnp.mean(xv * xv, axis=-1, keepdims=True) + EPS)
            h = (xv * r * g_ref[...]).astype(bf16)
            h_ref[...] = h
            lr_ref[...] = _dot_nt(h, wl_ref[...])

        proj_ref[...] = _dot_nt(h_ref[...], w_ref[...])

    return pl.pallas_call(
        body,
        grid=(T // tm, NMAIN // tn),
        in_specs=[
            pl.BlockSpec((tm, D), lambda i, j: (i, 0)),
            pl.BlockSpec((1, D), lambda i, j: (0, 0)),
            pl.BlockSpec((tn, D), lambda i, j: (j, 0)),
            pl.BlockSpec((LRP, D), lambda i, j: (0, 0)),
        ],
        out_specs=[
            pl.BlockSpec((tm, tn), lambda i, j: (i, j)),
            pl.BlockSpec((tm, LRP), lambda i, j: (i, 0)),
            pl.BlockSpec((tm, D), lambda i, j: (i, 0)),
        ],
        out_shape=[
            jax.ShapeDtypeStruct((T, NMAIN), f32),
            jax.ShapeDtypeStruct((T, LRP), f32),
            jax.ShapeDtypeStruct((T, D), bf16),
        ],
        compiler_params=_params(("parallel", "arbitrary")),
        name="proj_fwd",
    )(x, g0, w_main, w_lr)


def _causal_mask():
    t = lax.broadcasted_iota(jnp.int32, (CA, CA), 0)
    s = lax.broadcasted_iota(jnp.int32, (CA, CA), 1)
    return s <= t


def _layernorm_parts(gv):
    mu = jnp.mean(gv, axis=-1, keepdims=True)
    xc = gv - mu
    rs = lax.rsqrt(jnp.mean(xc * xc, axis=-1, keepdims=True) + LN_EPS)
    return xc * rs, rs


def _mixer_a_fwd(proj, ln_g, ln_b, w_s, b_sb, tm=256):
    T = proj.shape[0]

    def body(u_ref, v_ref, za_ref, lg_ref, lb_ref, ws_ref, bs_ref, a_ref, vln_s):
        vhat, _ = _layernorm_parts(_gelu(v_ref[...]))
        vln_s[...] = (vhat * lg_ref[...] + lb_ref[...]).astype(bf16)
        mask = _causal_mask()
        for g in range(HA):
            wg = jnp.where(mask, ws_ref[g], 0.0).astype(bf16)
            cols = slice(g * GA, (g + 1) * GA)
            for c in range(tm // CA):
                rows = slice(c * CA, (c + 1) * CA)
                mixed = _dot(wg, vln_s[rows, cols]) + bs_ref[g]
                za = za_ref[rows, cols]
                a = _gelu(u_ref[rows, cols]) * mixed * (za * _sigmoid(za))
                a_ref[rows, cols] = a.astype(bf16)

    def col(cidx):
        return pl.BlockSpec((tm, D), lambda i, c=cidx: (i, c))

    return pl.pallas_call(
        body,
        grid=(T // tm,),
        in_specs=[col(COL_U), col(COL_V), col(COL_ZA), _resident((1, D)), _resident((1, D)),
                  _resident((HA, CA, CA)), _resident((HA, CA, GA))],
        out_specs=pl.BlockSpec((tm, D), lambda i: (i, 0)),
        out_shape=jax.ShapeDtypeStruct((T, D), bf16),
        scratch_shapes=[pltpu.VMEM((tm, D), bf16)],
        compiler_params=_params(("parallel",)),
        name="mixer_a_fwd",
    )(proj, proj, proj, ln_g, ln_b, w_s, b_sb)


def _mixer_a_bwd(proj, da, dproj, ln_g, ln_b, w_s, b_sb, tm=256):
    T = proj.shape[0]
    nsteps = T // tm

    def body(u_ref, v_ref, za_ref, da_ref, dp_in, lg_ref, lb_ref, ws_ref, bs_ref,
             dp_ref, dws_ref, dbs_ref, dlg_ref, dlb_ref, vln_s, dvln_s):
        del dp_in
        i = pl.program_id(0)

        @pl.when(i == 0)
        def _():
            dws_ref[...] = jnp.zeros_like(dws_ref)
            dbs_ref[...] = jnp.zeros_like(dbs_ref)
            dlg_ref[...] = jnp.zeros_like(dlg_ref)
            dlb_ref[...] = jnp.zeros_like(dlb_ref)

        v = v_ref[...]
        vhat, rs = _layernorm_parts(_gelu(v))
        vln_s[...] = (vhat * lg_ref[...] + lb_ref[...]).astype(bf16)
        mask = _causal_mask()
        for g in range(HA):
            wg = jnp.where(mask, ws_ref[g], 0.0).astype(bf16)
            cols = slice(g * GA, (g + 1) * GA)
            dw_acc = jnp.zeros((CA, CA), f32)
            db_acc = jnp.zeros((CA, 1), f32)
            for c in range(tm // CA):
                rows = slice(c * CA, (c + 1) * CA)
                vln = vln_s[rows, cols]
                mixed = _dot(wg, vln) + bs_ref[g]
                u = u_ref[rows, cols]
                za = za_ref[rows, cols]
                da_blk = da_ref[rows, cols]
                sg = _sigmoid(za)
                sz = za * sg
                gu = _gelu(u)
                dp_ref[rows, cols] = (da_blk * mixed * sz * _gelu_grad(u)).astype(bf16)
                dp_ref[rows, 2 * D + g * GA:2 * D + (g + 1) * GA] = (
                    da_blk * gu * mixed * (sg * (1.0 + za * (1.0 - sg)))).astype(bf16)
                dmixed = da_blk * gu * sz
                dmb = dmixed.astype(bf16)
                dvln_s[rows, cols] = _dot_tn(wg, dmb)
                dw_acc = dw_acc + _dot_nt(dmb, vln)
                db_acc = db_acc + jnp.sum(dmixed, axis=-1, keepdims=True)
            dws_ref[g] += dw_acc
            dbs_ref[g] += jnp.broadcast_to(db_acc, (CA, GA))

        dvln = dvln_s[...]
        dlg_ref[...] += jnp.sum(dvln * vhat, axis=0, keepdims=True)
        dlb_ref[...] += jnp.sum(dvln, axis=0, keepdims=True)
        dvhat = dvln * lg_ref[...]
        dgv = rs * (dvhat - jnp.mean(dvhat, axis=-1, keepdims=True)
                    - vhat * jnp.mean(dvhat * vhat, axis=-1, keepdims=True))
        dp_ref[:, D:2 * D] = (dgv * _gelu_grad(v)).astype(bf16)

        @pl.when(i == nsteps - 1)
        def _():
            for g in range(HA):
                dws_ref[g] = jnp.where(mask, dws_ref[g], 0.0)

    def col(cidx):
        return pl.BlockSpec((tm, D), lambda i, c=cidx: (i, c))

    return pl.pallas_call(
        body,
        grid=(nsteps,),
        in_specs=[col(COL_U), col(COL_V), col(COL_ZA), pl.BlockSpec((tm, D), lambda i: (i, 0)),
                  pl.BlockSpec(memory_space=pl.ANY),
                  _resident((1, D)), _resident((1, D)), _resident((HA, CA, CA)), _resident((HA, CA, GA))],
        out_specs=[pl.BlockSpec((tm, 3 * D), lambda i: (i, 0)),
                   _resident((HA, CA, CA)), _resident((HA, CA, GA)), _resident((1, D)), _resident((1, D))],
        out_shape=[jax.ShapeDtypeStruct(dproj.shape, dproj.dtype),
                   jax.ShapeDtypeStruct((HA, CA, CA), f32), jax.ShapeDtypeStruct((HA, CA, GA), f32),
                   jax.ShapeDtypeStruct((1, D), f32), jax.ShapeDtypeStruct((1, D), f32)],
        scratch_shapes=[pltpu.VMEM((tm, D), bf16), pltpu.VMEM((tm, D), f32)],
        input_output_aliases={4: 0},
        compiler_params=_params(("arbitrary",)),
        name="mixer_a_bwd",
    )(proj, proj, proj, da, dproj, ln_g, ln_b, w_s, b_sb)


def _tri(n, upper):
    r = lax.broadcasted_iota(jnp.int32, (n, n), 0)
    c = lax.broadcasted_iota(jnp.int32, (n, n), 1)
    return jnp.where((c >= r) if upper else (c <= r), 1.0, 0.0).astype(f32)


def _chunk_tri(n, upper):
    r = lax.broadcasted_iota(jnp.int32, (n, n), 0)
    c = lax.broadcasted_iota(jnp.int32, (n, n), 1)
    shift = CB.bit_length() - 1
    same_chunk = jnp.right_shift(r, shift) == jnp.right_shift(c, shift)
    return jnp.where(same_chunk & ((c >= r) if upper else (c <= r)), 1.0, 0.0).astype(f32)


def _log_alpha(lr, wg, bg):
    logit = _dot(lr.astype(bf16), wg.astype(bf16)) + bg
    la = (jnp.minimum(logit, 0.0) - jnp.log1p(jnp.exp(-jnp.abs(logit)))) * (1.0 / GATE_NORM)
    return logit, la


def _gla_fwd(proj, lr, w_gate, b_gate, gla_g, tm=256):
    T = proj.shape[0]
    nchunk = T // CB
    cpb = tm // CB

    def body(q_ref, k_ref, v_ref, zb_ref, lr_ref, wg_ref, bg_ref, gg_ref,
             o_ref, ob_ref, st_ref, state, la_s):
        @pl.when(pl.program_id(0) == 0)
        def _():
            state[...] = jnp.zeros_like(state)

        _, la = _log_alpha(lr_ref[...], wg_ref[...], bg_ref[...])
        la_s[...] = _dot_exact(_chunk_tri(tm, upper=False), la)
        causal = _tri(CB, upper=False) > 0.5
        states = [state[hd] for hd in range(HB)]
        for c in range(cpb):
            rows = slice(c * CB, (c + 1) * CB)
            b = la_s[rows, :]
            bl = b[CB - 1:CB, :]
            bm = b[CB // 2 - 1:CB // 2, :]
            q = q_ref[rows, :] * QSCALE
            k = k_ref[rows, :]
            qi_all = (q * jnp.exp(b - bm)).astype(bf16)
            ki_all = (k * jnp.exp(bm - b)).astype(bf16)
            qe_all = (q * jnp.exp(b)).astype(bf16)
            ks_all = (k * jnp.exp(bl - b)).astype(bf16)
            e_l = jnp.exp(bl)
            for hd in range(HB):
                kc = slice(hd * DK, (hd + 1) * DK)
                vc = slice(hd * DV, (hd + 1) * DV)
                v = v_ref[rows, vc].astype(bf16)
                p = jnp.where(causal, _dot_nt(qi_all[:, kc], ki_all[:, kc]), 0.0).astype(bf16)
                s0 = states[hd]
                st_ref[c, hd] = s0
                o = _dot(p, v) + _dot_nt(qe_all[:, kc], s0.astype(bf16))
                states[hd] = s0 * e_l[:, kc] + _dot_tn(v, ks_all[:, kc])
                o_ref[rows, vc] = o
                ro = lax.rsqrt(jnp.mean(o * o, axis=-1, keepdims=True) + EPS)
                zb = zb_ref[rows, vc]
                ob_ref[rows, vc] = (o * ro * gg_ref[...] * (zb * _sigmoid(zb))).astype(bf16)
        for hd in range(HB):
            state[hd] = states[hd]

    return pl.pallas_call(
        body,
        grid=(T // tm,),
        in_specs=[pl.BlockSpec((tm, KEYB), lambda i: (i, COL_Q)),
                  pl.BlockSpec((tm, KEYB), lambda i: (i, COL_K)),
                  pl.BlockSpec((tm, D), lambda i: (i, COL_VB)),
                  pl.BlockSpec((tm, D), lambda i: (i, COL_ZB)),
                  pl.BlockSpec((tm, LRP), lambda i: (i, 0)),
                  _resident((LRP, KEYB)), _resident((1, KEYB)), _resident((1, DV))],
        out_specs=[pl.BlockSpec((tm, D), lambda i: (i, 0)),
                   pl.BlockSpec((tm, D), lambda i: (i, 0)),
                   pl.BlockSpec((cpb, HB, DV, DK), lambda i: (i, 0, 0, 0))],
        out_shape=[jax.ShapeDtypeStruct((T, D), f32), jax.ShapeDtypeStruct((T, D), bf16),
                   jax.ShapeDtypeStruct((nchunk, HB, DV, DK), f32)],
        scratch_shapes=[pltpu.VMEM((HB, DV, DK), f32), pltpu.VMEM((tm, KEYB), f32)],
        compiler_params=_params(("arbitrary",)),
        name="gla_fwd",
    )(proj, proj, proj, proj, lr, w_gate, b_gate, gla_g)


def _gla_bwd(proj, lr, o, states, dob, dproj, w_gate, b_gate, gla_g, tm=256):
    T = proj.shape[0]
    cpb = tm // CB
    nb = T // tm

    def body(q_ref, k_ref, v_ref, zb_ref, lr_ref, o_ref, st_ref, dob_ref, dp_in, wg_ref, bg_ref, gg_ref,
             dp_ref, dlr_ref, dwg_ref, dbg_ref, dgg_ref, dstate, la_s, dlogit_s, tail_s):
        del dp_in
        step = pl.program_id(0)

        @pl.when(step == 0)
        def _():
            dstate[...] = jnp.zeros_like(dstate)
            dwg_ref[...] = jnp.zeros_like(dwg_ref)
            dbg_ref[...] = jnp.zeros_like(dbg_ref)
            dgg_ref[...] = jnp.zeros_like(dgg_ref)

        lr_v = lr_ref[...]
        logit, la = _log_alpha(lr_v, wg_ref[...], bg_ref[...])
        la_s[...] = _dot_exact(_chunk_tri(tm, upper=False), la)
        causal = _tri(CB, upper=False) > 0.5
        gg = gg_ref[...]
        dgg_acc = jnp.zeros((1, DV), f32)
        dstates = [dstate[hd] for hd in range(HB)]
        for c in reversed(range(cpb)):
            rows = slice(c * CB, (c + 1) * CB)
            b = la_s[rows, :]
            bl = b[CB - 1:CB, :]
            bm = b[CB // 2 - 1:CB // 2, :]
            eb_all, eqm_all, ekm_all = jnp.exp(b), jnp.exp(b - bm), jnp.exp(bm - b)
            eks_all, el_all = jnp.exp(bl - b), jnp.exp(bl)
            q_all = q_ref[rows, :] * QSCALE
            k_all = k_ref[rows, :]
            qi_all = (q_all * eqm_all).astype(bf16)
            ki_all = (k_all * ekm_all).astype(bf16)
            qe_all = (q_all * eb_all).astype(bf16)
            ksf_all = k_all * eks_all
            ks_all = ksf_all.astype(bf16)
            for hd in range(HB):
                kc = slice(hd * DK, (hd + 1) * DK)
                vc = slice(hd * DV, (hd + 1) * DV)
                o_h = o_ref[rows, vc]
                ro = lax.rsqrt(jnp.mean(o_h * o_h, axis=-1, keepdims=True) + EPS)
                ohat = o_h * ro
                zb = zb_ref[rows, vc]
                sg = _sigmoid(zb)
                dob_h = dob_ref[rows, vc]
                don = dob_h * (zb * sg)
                dp_ref[rows, 2 * D + hd * DV:2 * D + (hd + 1) * DV] = (
                    dob_h * ohat * gg * (sg * (1.0 + zb * (1.0 - sg)))).astype(bf16)
                dgg_acc = dgg_acc + jnp.sum(don * ohat, axis=0, keepdims=True)
                dohat = don * gg
                do = (ro * (dohat - ohat * jnp.mean(dohat * ohat, axis=-1, keepdims=True))).astype(bf16)
                e_b, e_qm, e_km, e_ks, e_l = eb_all[:, kc], eqm_all[:, kc], ekm_all[:, kc], eks_all[:, kc], el_all[:, kc]
                q, k, ks_f = q_all[:, kc], k_all[:, kc], ksf_all[:, kc]
                qi, ki, qe, ks = qi_all[:, kc], ki_all[:, kc], qe_all[:, kc], ks_all[:, kc]
                v = v_ref[rows, vc].astype(bf16)
                p = jnp.where(causal, _dot_nt(qi, ki), 0.0).astype(bf16)
                s0 = st_ref[c, hd]
                ds = dstates[hd]
                ds_b = ds.astype(bf16)
                dv = _dot_tn(p, do) + _dot_nt(ks, ds_b)
                dpm = jnp.where(causal, _dot_nt(do, v), 0.0).astype(bf16)
                dqi = _dot(dpm, ki)
                dki = _dot_tn(dpm, qi)
                dqe = _dot(do, s0.astype(bf16))
                dks = _dot(v, ds_b)
                dq_s = dqi * e_qm + dqe * e_b
                dk = dki * e_km + dks * e_ks
                tail = (jnp.sum(dks * ks_f, axis=0, keepdims=True)
                        + e_l * jnp.sum(ds * s0, axis=0, keepdims=True))
                dstates[hd] = _dot_tn(do, qe) + ds * e_l
                dp_ref[rows, kc] = (dq_s * QSCALE).astype(bf16)
                dp_ref[rows, KEYB + hd * DK:KEYB + (hd + 1) * DK] = dk.astype(bf16)
                dp_ref[rows, D + hd * DV:D + (hd + 1) * DV] = dv.astype(bf16)
                dlogit_s[rows, kc] = dq_s * q - dk * k
                tail_s[rows, kc] = jnp.broadcast_to(tail, (CB, DK))
        for hd in range(HB):
            dstate[hd] = dstates[hd]
        dgg_ref[...] += dgg_acc
        dg = _dot_exact(_chunk_tri(tm, upper=True), dlogit_s[...]) + tail_s[...]
        dlogit = dg * (1.0 / GATE_NORM) * _sigmoid(-logit)
        dbg_ref[...] += jnp.sum(dlogit, axis=0, keepdims=True)
        dlb = dlogit.astype(bf16)
        dlr_ref[...] = _dot_nt(dlb, wg_ref[...].astype(bf16)).astype(bf16)
        dwg_ref[...] += _dot_tn(lr_v.astype(bf16), dlb)

    def rev(cidx):
        return lambda i, c=cidx: (nb - 1 - i, c)

    return pl.pallas_call(
        body,
        grid=(nb,),
        in_specs=[pl.BlockSpec((tm, KEYB), rev(COL_Q)),
                  pl.BlockSpec((tm, KEYB), rev(COL_K)),
                  pl.BlockSpec((tm, D), rev(COL_VB)),
                  pl.BlockSpec((tm, D), rev(COL_ZB)),
                  pl.BlockSpec((tm, LRP), rev(0)),
                  pl.BlockSpec((tm, D), rev(0)),
                  pl.BlockSpec((cpb, HB, DV, DK), lambda i: (nb - 1 - i, 0, 0, 0)),
                  pl.BlockSpec((tm, D), rev(0)),
                  pl.BlockSpec(memory_space=pl.ANY),
                  _resident((LRP, KEYB)), _resident((1, KEYB)), _resident((1, DV))],
        out_specs=[pl.BlockSpec((tm, 3 * D), rev(1)),
                   pl.BlockSpec((tm, LRP), rev(0)),
                   _resident((LRP, KEYB)), _resident((1, KEYB)), _resident((1, DV))],
        out_shape=[jax.ShapeDtypeStruct(dproj.shape, dproj.dtype),
                   jax.ShapeDtypeStruct((T, LRP), bf16),
                   jax.ShapeDtypeStruct((LRP, KEYB), f32), jax.ShapeDtypeStruct((1, KEYB), f32),
                   jax.ShapeDtypeStruct((1, DV), f32)],
        scratch_shapes=[pltpu.VMEM((HB, DV, DK), f32)] + [pltpu.VMEM((tm, KEYB), f32)] * 3,
        input_output_aliases={8: 0},
        compiler_params=_params(("arbitrary",)),
        name="gla_bwd",
    )(proj, proj, proj, proj, lr, o, states, dob, dproj, w_gate, b_gate, gla_g)


def _merge_fwd_bwd(x, tgt, proj, a, ob, w_a, w_b, w_o, g_f, tm=256):
    T = x.shape[0]

    def body(x_ref, t_ref, gt_ref, a_ref, ob_ref, wa_ref, wb_ref, wo_ref, gf_ref,
             dp_ref, dy_ref, da_ref, dob_ref, dwa_ref, dwb_ref, dwo_ref, dgf_ref, loss_ref):
        @pl.when(pl.program_id(0) == 0)
        def _():
            dwa_ref[...] = jnp.zeros_like(dwa_ref)
            dwb_ref[...] = jnp.zeros_like(dwb_ref)
            dwo_ref[...] = jnp.zeros_like(dwo_ref)
            dgf_ref[...] = jnp.zeros_like(dgf_ref)
            loss_ref[...] = jnp.zeros_like(loss_ref)

        ga = _sigmoid(gt_ref[:, :D])
        gb = _sigmoid(gt_ref[:, D:])
        a_v = a_ref[...]
        ob_v = ob_ref[...]
        pa = _dot(a_v, wa_ref[...])
        pb = _dot(ob_v, wb_ref[...])
        mb = (ga * pa + gb * pb).astype(bf16)
        y = x_ref[...] + _dot(mb, wo_ref[...])
        r1 = lax.rsqrt(jnp.mean(y * y, axis=-1, keepdims=True) + EPS)
        yhat = y * r1
        gf = gf_ref[...]
        err = yhat * gf - t_ref[...]
        loss_ref[...] += jnp.sum(err * err, axis=0, keepdims=True) * (0.5 / D)
        dout = err * (1.0 / D)
        dgf_ref[...] += jnp.sum(dout * yhat, axis=0, keepdims=True)
        dyn = dout * gf
        dy = r1 * (dyn - yhat * jnp.mean(dyn * yhat, axis=-1, keepdims=True))
        dy_ref[...] = dy
        dyb = dy.astype(bf16)
        dwo_ref[...] += _dot_tn(mb, dyb)
        dm = _dot_nt(dyb, wo_ref[...])
        dpa = (dm * ga).astype(bf16)
        dpb = (dm * gb).astype(bf16)
        dp_ref[:, :D] = (dm * pa * ga * (1.0 - ga)).astype(bf16)
        dp_ref[:, D:] = (dm * pb * gb * (1.0 - gb)).astype(bf16)
        dwa_ref[...] += _dot_tn(a_v, dpa)
        dwb_ref[...] += _dot_tn(ob_v, dpb)
        da_ref[...] = _dot_nt(dpa, wa_ref[...])
        dob_ref[...] = _dot_nt(dpb, wb_ref[...])

    row = lambda: pl.BlockSpec((tm, D), lambda i: (i, 0))
    return pl.pallas_call(
        body,
        grid=(T // tm,),
        in_specs=[row(), row(), pl.BlockSpec((tm, 2 * D), lambda i: (i, COL_GATES)), row(), row(),
                  _resident((D, D)), _resident((D, D)), _resident((D, D)), _resident((1, D))],
        out_specs=[pl.BlockSpec((tm, 2 * D), lambda i: (i, COL_GATES)), row(), row(), row(),
                   _resident((D, D)), _resident((D, D)), _resident((D, D)), _resident((1, D)), _resident((1, D))],
        out_shape=[jax.ShapeDtypeStruct((T, NMAIN), bf16),
                   jax.ShapeDtypeStruct((T, D), f32), jax.ShapeDtypeStruct((T, D), f32),
                   jax.ShapeDtypeStruct((T, D), f32),
                   jax.ShapeDtypeStruct((D, D), f32), jax.ShapeDtypeStruct((D, D), f32),
                   jax.ShapeDtypeStruct((D, D), f32),
                   jax.ShapeDtypeStruct((1, D), f32), jax.ShapeDtypeStruct((1, D), f32)],
        compiler_params=_params(("arbitrary",)),
        name="merge_fwd_bwd",
    )(x, tgt, proj, a, ob, w_a, w_b, w_o, g_f)


def _dx_bwd(x, dy, dproj, dlr, g0, w_main, w_lr, tm=256):
    T = x.shape[0]

    def body(x_ref, dy_ref, dp_ref, dl_ref, g_ref, w_ref, wl_ref, dx_ref, dg_ref, dwl_ref):
        @pl.when(pl.program_id(0) == 0)
        def _():
            dg_ref[...] = jnp.zeros_like(dg_ref)
            dwl_ref[...] = jnp.zeros_like(dwl_ref)

        xv = x_ref[...]
        r = lax.rsqrt(jnp.mean(xv * xv, axis=-1, keepdims=True) + EPS)
        xhat = xv * r
        g = g_ref[...]
        dl = dl_ref[...]
        dh = _dot(dp_ref[...], w_ref[...]) + _dot(dl, wl_ref[...])
        dg_ref[...] += jnp.sum(dh * xhat, axis=0, keepdims=True)
        t = dh * g
        dx_ref[...] = dy_ref[...] + r * (t - xhat * jnp.mean(t * xhat, axis=-1, keepdims=True))
        dwl_ref[...] += _dot_tn(dl, (xhat * g).astype(bf16))

    row = lambda: pl.BlockSpec((tm, D), lambda i: (i, 0))
    return pl.pallas_call(
        body,
        grid=(T // tm,),
        in_specs=[row(), row(), pl.BlockSpec((tm, NMAIN), lambda i: (i, 0)),
                  pl.BlockSpec((tm, LRP), lambda i: (i, 0)),
                  _resident((1, D)), _resident((NMAIN, D)), _resident((LRP, D))],
        out_specs=[row(), _resident((1, D)), _resident((LRP, D))],
        out_shape=[jax.ShapeDtypeStruct((T, D), f32), jax.ShapeDtypeStruct((1, D), f32),
                   jax.ShapeDtypeStruct((LRP, D), f32)],
        compiler_params=_params(("arbitrary",)),
        name="dx_bwd",
    )(x, dy, dproj, dlr, g0, w_main, w_lr)


def _dw_in(h, dproj, dwl, tm=1024, tn=1024):
    T = h.shape[0]
    tm = min(tm, T)
    nj, nk = NMAIN // tn, T // tm
    lr_tile = LR_COL // tn

    def body(h_ref, dp_ref, dwl_ref, out_ref, acc, sems, lr_sem):
        j, k = pl.program_id(0), pl.program_id(1)
        slot = j % 2

        def tile_out(jj, s):
            row = pl.multiple_of(jj * tn + jnp.where(jj >= lr_tile, RANK, 0), 8)
            return pltpu.make_async_copy(acc.at[s], out_ref.at[pl.ds(row, tn)], sems.at[s])

        lr_rows = pltpu.make_async_copy(dwl_ref.at[pl.ds(0, RANK)], out_ref.at[pl.ds(LR_COL, RANK)], lr_sem)

        @pl.when((j == 0) & (k == 0))
        def _():
            lr_rows.start()

        @pl.when(k == 0)
        def _():
            acc[slot] = jnp.zeros((tn, D), f32)

        acc[slot] += _dot_tn(dp_ref[...], h_ref[...])

        @pl.when(k == nk - 1)
        def _():
            tile_out(j, slot).start()

            @pl.when(j > 0)
            def _():
                tile_out(j - 1, 1 - slot).wait()

            @pl.when(j == nj - 1)
            def _():
                tile_out(j, slot).wait()
                lr_rows.wait()

    return pl.pallas_call(
        body,
        grid=(nj, nk),
        in_specs=[pl.BlockSpec((tm, D), lambda j, k: (k, 0)), pl.BlockSpec((tm, tn), lambda j, k: (k, j)), _ANY],
        out_specs=_ANY,
        out_shape=jax.ShapeDtypeStruct((NMAIN + RANK, D), f32),
        scratch_shapes=[pltpu.VMEM((2, tn, D), f32), pltpu.SemaphoreType.DMA((2,)), pltpu.SemaphoreType.DMA],
        compiler_params=_params(("arbitrary", "arbitrary")),
        name="dw_in",
    )(h, dproj, dwl)


def _local_step(x, tgt, g0, w_main, w_lr, ln_g, ln_b, w_s, b_sb, w_gate, b_gate, gla_g, w_a, w_b, w_o, g_f):
    proj, lr, h = _proj_fwd(x, g0, w_main, w_lr)
    a = _mixer_a_fwd(proj, ln_g, ln_b, w_s, b_sb)
    o, ob, states = _gla_fwd(proj, lr, w_gate, b_gate, gla_g)
    dproj, dy, da, dob, dwa, dwb, dwo, dgf, loss_cols = _merge_fwd_bwd(x, tgt, proj, a, ob, w_a, w_b, w_o, g_f)
    dproj, dws, dbs, dlg, dlb = _mixer_a_bwd(proj, da, dproj, ln_g, ln_b, w_s, b_sb)
    dproj, dlr, dwg, dbg, dgg = _gla_bwd(proj, lr, o, states, dob, dproj, w_gate, b_gate, gla_g)
    dx, dg0, dwl = _dx_bwd(x, dy, dproj, dlr, g0, w_main, w_lr)
    dwt = _dw_in(h, dproj, dwl)
    return dict(loss_cols=loss_cols, dx=dx, dg0=dg0, dwt=dwt, dlg=dlg, dlb=dlb, dws=dws, dbs=dbs,
                dwg=dwg, dbg=dbg, dgg=dgg, dwa=dwa, dwb=dwb, dwo=dwo, dgf=dgf)


MESH = pl.DeviceIdType.MESH


def _place():
    x, y, c = lax.axis_index("x"), lax.axis_index("y"), lax.axis_index("c")
    others = [(1 - x, y), (x, 1 - y), (1 - x, 1 - y)]
    return x, y, c, 2 * x + y, others


def _remote(src, dst, send_sem, recv_sem, to):
    return pltpu.make_async_remote_copy(src_ref=src, dst_ref=dst, send_sem=send_sem, recv_sem=recv_sem,
                                        device_id=to, device_id_type=MESH)


def _half(ref, e, by_columns):
    if not by_columns:
        return ref.at[e]
    hw = ref.shape[-1] // 2
    return ref.at[:, pl.ds(pl.multiple_of(e * hw, 128), hw)]


def _gather_weights(shards, by_columns):
    n = len(shards)

    def body(*refs):
        srcs, dsts = refs[:n], refs[n:2 * n]
        send_sems, recv_sems, pass_send, pass_recv = refs[2 * n:]
        x, y, c, me, others = _place()
        sibling = (x, y, 1 - c)

        def src(a, e):
            return _half(srcs[a], e, by_columns[a])

        def dst(a, j, e):
            return _half(dsts[a].at[j], e, by_columns[a])

        sends = [_remote(src(a, c), dst(a, me, c), send_sems.at[k, a], recv_sems.at[k, a], (cx, cy, c))
                 for k, (cx, cy) in enumerate(others) for a in range(n)]
        for cp in sends:
            cp.start()
        passes = []
        for k, (cx, cy) in enumerate(others):
            j = 2 * cx + cy
            for a in range(n):
                _remote(src(a, c), dst(a, j, c), send_sems.at[k, a], recv_sems.at[k, a], (cx, cy, c)).wait_recv()
                cp = _remote(dst(a, j, c), dst(a, j, c), pass_send.at[k, a], pass_recv.at[k, a], sibling)
                cp.start()
                passes.append(cp)
        for k, (cx, cy) in enumerate(others):
            j = 2 * cx + cy
            for a in range(n):
                _remote(src(a, c), dst(a, j, 1 - c), pass_send.at[k, a], pass_recv.at[k, a], sibling).wait_recv()
        for cp in sends + passes:
            cp.wait_send()

    return pl.pallas_call(
        body,
        in_specs=[_ANY] * n,
        out_specs=[_ANY] * n,
        out_shape=[jax.ShapeDtypeStruct((NCHIP,) + s.shape, s.dtype) for s in shards],
        scratch_shapes=[pltpu.SemaphoreType.DMA((3, n))] * 4,
        name="gather_weights",
    )(*shards)


HALF_FIRST, CHIP_FIRST, BY_COLUMNS = "half_first", "chip_first", "by_columns"


def _sibling_halves(bufs, kinds):
    n = len(bufs)
    pieces = [1 if kind == HALF_FIRST else b.shape[0] for b, kind in zip(bufs, kinds)]

    def body(*refs):
        srcs, dsts = refs[:n], refs[n:2 * n]
        send_sems, recv_sems = refs[2 * n:]
        x, y, c, _, _ = _place()
        cps = []
        for a in range(n):
            if kinds[a] == HALF_FIRST:
                cps.append(_remote(srcs[a].at[1 - c], dsts[a], send_sems.at[a, 0], recv_sems.at[a, 0], (x, y, 1 - c)))
            else:
                cps += [_remote(_half(srcs[a].at[j], 1 - c, kinds[a] == BY_COLUMNS), dsts[a].at[j],
                                send_sems.at[a, j], recv_sems.at[a, j], (x, y, 1 - c)) for j in range(pieces[a])]
        for cp in cps:
            cp.start()
        for cp in cps:
            cp.wait()

    def landed(b, kind):
        if kind == HALF_FIRST:
            return b.shape[1:]
        if kind == CHIP_FIRST:
            return (b.shape[0],) + b.shape[2:]
        return b.shape[:2] + (b.shape[2] // 2,)

    return pl.pallas_call(
        body,
        in_specs=[_ANY] * n,
        out_specs=[_ANY] * n,
        out_shape=[jax.ShapeDtypeStruct(landed(b, kind), b.dtype) for b, kind in zip(bufs, kinds)],
        scratch_shapes=[pltpu.SemaphoreType.DMA((n, max(pieces)))] * 2,
        name="sibling_halves",
    )(*bufs)


def _chip_exchange(parts):
    n = len(parts)

    def body(*refs):
        srcs, dsts = refs[:n], refs[n:2 * n]
        send_sems, recv_sems = refs[2 * n:]
        x, y, c, me, others = _place()

        def part(a, j):
            return srcs[a].at[j if parts[a].shape[0] == NCHIP else 0]

        sends = [_remote(part(a, 2 * cx + cy), dsts[a].at[k], send_sems.at[k, a], recv_sems.at[k, a], (cx, cy, c))
                 for k, (cx, cy) in enumerate(others) for a in range(n)]
        for cp in sends:
            cp.start()
        for cp in sends:
            cp.wait()

    return pl.pallas_call(
        body,
        in_specs=[_ANY] * n,
        out_specs=[_ANY] * n,
        out_shape=[jax.ShapeDtypeStruct((3,) + p.shape[1:], p.dtype) for p in parts],
        scratch_shapes=[pltpu.SemaphoreType.DMA((3, n))] * 2,
        name="chip_exchange",
    )(*parts)


def _sibling_swap(halves):
    n = len(halves)

    def body(*refs):
        srcs, dsts = refs[:n], refs[n:2 * n]
        send_sems, recv_sems = refs[2 * n:]
        x, y, c, _, _ = _place()
        cps = [_remote(srcs[a], dsts[a], send_sems.at[a], recv_sems.at[a], (x, y, 1 - c)) for a in range(n)]
        for cp in cps:
            cp.start()
        for cp in cps:
            cp.wait()

    return pl.pallas_call(
        body,
        in_specs=[_ANY] * n,
        out_specs=[_ANY] * n,
        out_shape=[jax.ShapeDtypeStruct(s.shape, s.dtype) for s in halves],
        scratch_shapes=[pltpu.SemaphoreType.DMA((n,))] * 2,
        name="sibling_swap",
    )(*halves)


TILE_BYTES = 1 << 20


def _tile(h, w):
    if h % 128 == 0:
        return 128, w
    if h * w * 4 <= TILE_BYTES:
        return h, w
    return h, 128


def _pair_sum(place, buf, kind, got, out_dtype):
    nj, h, w = got.shape
    th, tw = _tile(h, w)
    nq = w // tw

    def body(p_ref, a_ref, b_ref, o_ref):
        del p_ref
        o_ref[...] = (a_ref[...] + b_ref[...]).astype(out_dtype)

    if kind == HALF_FIRST:
        mine = pl.BlockSpec((None, None, th, tw), lambda j, r, q, p: (p[0], j, r, q))
    elif kind == CHIP_FIRST:
        mine = pl.BlockSpec((None, None, th, tw), lambda j, r, q, p: (j, p[0], r, q))
    else:
        mine = pl.BlockSpec((None, th, tw), lambda j, r, q, p: (j, r, p[0] * nq + q))
    return pl.pallas_call(
        body,
        grid_spec=pltpu.PrefetchScalarGridSpec(
            num_scalar_prefetch=1,
            grid=(nj, h // th, w // tw),
            in_specs=[mine, pl.BlockSpec((None, th, tw), lambda j, r, q, p: (j, r, q))],
            out_specs=pl.BlockSpec((None, th, tw), lambda j, r, q, p: (j, r, q)),
        ),
        out_shape=jax.ShapeDtypeStruct((nj, h, w), out_dtype),
        compiler_params=_params(("parallel", "parallel", "parallel")),
        name="pair_sum",
    )(place, buf, got)


def _chip_sum(place, part, slots):
    nj, h, w = part.shape
    th, tw = _tile(h, w)

    def body(p_ref, own_ref, s_ref, o_ref):
        me = p_ref[1]
        own = own_ref[...].astype(f32)
        by_flip = {2: s_ref[0].astype(f32), 1: s_ref[1].astype(f32), 3: s_ref[2].astype(f32)}
        acc = None
        for j in range(NCHIP):
            flip = me ^ j
            term = jnp.where(flip == 0, own, jnp.where(flip == 2, by_flip[2], jnp.where(flip == 1, by_flip[1], by_flip[3])))
            acc = term if acc is None else acc + term
        o_ref[...] = acc

    return pl.pallas_call(
        body,
        grid_spec=pltpu.PrefetchScalarGridSpec(
            num_scalar_prefetch=1,
            grid=(h // th, w // tw),
            in_specs=[pl.BlockSpec((None, th, tw), lambda r, q, p: (p[1] if nj == NCHIP else 0, r, q)),
                      pl.BlockSpec((3, th, tw), lambda r, q, p: (0, r, q))],
            out_specs=pl.BlockSpec((th, tw), lambda r, q, p: (r, q)),
        ),
        out_shape=jax.ShapeDtypeStruct((h, w), f32),
        compiler_params=_params(("parallel", "parallel")),
        name="chip_sum",
    )(place, part, slots)


def _adamw_math(w, g, m, v):
    nm = ADAM_B1 * m + (1.0 - ADAM_B1) * g
    nv = ADAM_B2 * v + (1.0 - ADAM_B2) * (g * g)
    m_hat = nm / (1.0 - ADAM_B1 ** ADAM_STEP)
    v_hat = nv / (1.0 - ADAM_B2 ** ADAM_STEP)
    return -ADAM_LR * (m_hat / (jnp.sqrt(v_hat) + ADAM_EPS) + ADAM_WD * w), nm, nv


def _adamw(w, g, m, v):
    rows, width = w.shape
    th, tw = _tile(rows, width)

    def body(w_ref, g_ref, m_ref, v_ref, d_ref, nm_ref, nv_ref):
        d_ref[...], nm_ref[...], nv_ref[...] = _adamw_math(w_ref[...], g_ref[...], m_ref[...], v_ref[...])

    spec = pl.BlockSpec((th, tw), lambda r, q: (r, q))
    return pl.pallas_call(
        body,
        grid=(rows // th, width // tw),
        in_specs=[spec] * 4,
        out_specs=[spec] * 3,
        out_shape=[jax.ShapeDtypeStruct((rows, width), f32)] * 3,
        compiler_params=_params(("parallel", "parallel")),
        name="adamw",
    )(w, g, m, v)


def _adamw_halves(place, w, mine, got, m, v, axis):
    rows, width = w.shape
    h, hw = mine.shape
    th, tw = _tile(h, hw)
    nr, nq = h // th, hw // tw

    def body(p_ref, w_ref, a_ref, b_ref, m_ref, v_ref, g_ref, d_ref, nm_ref, nv_ref):
        g = jnp.where(pl.program_id(0) == p_ref[0], a_ref[...], b_ref[...])
        g_ref[...] = g
        d_ref[...], nm_ref[...], nv_ref[...] = _adamw_math(w_ref[...], g, m_ref[...], v_ref[...])

    if axis == 0:
        full = pl.BlockSpec((th, tw), lambda e, r, q, p: (e * nr + r, q))
    else:
        full = pl.BlockSpec((th, tw), lambda e, r, q, p: (r, e * nq + q))
    half = pl.BlockSpec((th, tw), lambda e, r, q, p: (r, q))
    return pl.pallas_call(
        body,
        grid_spec=pltpu.PrefetchScalarGridSpec(
            num_scalar_prefetch=1,
            grid=(2, nr, nq),
            in_specs=[full, half, half, full, full],
            out_specs=[full] * 4,
        ),
        out_shape=[jax.ShapeDtypeStruct((rows, width), f32)] * 4,
        compiler_params=_params(("parallel", "parallel", "parallel")),
        name="adamw_halves",
    )(place, w, mine, got, m, v)


def _reduce_gradients(place, bufs, kinds):
    got = _sibling_halves(bufs, kinds)
    parts = [_pair_sum(place, b, kind, r, bf16 if r.shape[0] == NCHIP else f32) for b, kind, r in zip(bufs, kinds, got)]
    slots = _chip_exchange(parts)
    mine = [_chip_sum(place, p, s) for p, s in zip(parts, slots)]
    theirs = _sibling_swap(mine)
    return mine, theirs


_SMALL = (("norm_g", 8), ("ln_v_g", 8), ("ln_v_b", 8), ("w_spatial", 1024), ("b_spatial", 8), ("b_gate_up", 4),
          ("gla_norm_g", 2), ("final_norm_g", 8), ("w_gate_up", 64), ("loss", 8))
_SMALL_ROWS = 1152


def _pack_rows(arrays, rows):
    flat = jnp.concatenate([a.reshape(-1, 128) for a in arrays], axis=0)
    return jnp.pad(flat, ((0, rows - flat.shape[0]), (0, 0)))


def kernel(x, norm_g, w_in, ln_v_g, ln_v_b, w_spatial, b_spatial, w_gate_up, b_gate_up, gla_norm_g, w_branch_a, w_branch_b, w_out, final_norm_g, loss_target, m_norm_g, m_w_in, m_ln_v_g, m_ln_v_b, m_w_spatial, m_b_spatial, m_w_gate_up, m_b_gate_up, m_gla_norm_g, m_w_branch_a, m_w_branch_b, m_w_out, m_final_norm_g, v_norm_g, v_w_in, v_ln_v_g, v_ln_v_b, v_w_spatial, v_b_spatial, v_w_gate_up, v_b_gate_up, v_gla_norm_g, v_w_branch_a, v_w_branch_b, v_w_out, v_final_norm_g):
    chip = 2 * lax.axis_index("x") + lax.axis_index("y")
    core = lax.axis_index("c")
    place = jnp.stack([core, chip]).astype(jnp.int32)
    mat_names = ("w_branch_a", "w_branch_b", "w_out")

    wt_shard = jnp.transpose(w_in[0]).astype(bf16)
    mats = [w[0].astype(bf16).reshape(2, D // NCHIP // 2, D) for w in (w_branch_a, w_branch_b, w_out)]
    gate_sh = w_gate_up[0].reshape(2, RANK // 2, 128)
    g_win, g_a, g_b, g_o, g_gate = _gather_weights([wt_shard] + mats + [gate_sh], [True, False, False, False, False])

    def with_own(gathered, own):
        mine = (jnp.arange(NCHIP) == chip).reshape((NCHIP,) + (1,) * own.ndim)
        return jnp.where(mine, own[None], gathered)

    w_full_t = with_own(g_win, wt_shard).reshape(NCHIP * WIN_SHARD, D)
    w_main_t = jnp.concatenate([w_full_t[:LR_COL], w_full_t[LR_COL + RANK:]], axis=0)
    w_lr_t = jnp.pad(w_full_t[LR_COL:LR_COL + RANK], ((0, LRP - RANK), (0, 0)))
    w_a, w_b, w_o = (with_own(g, own).reshape(D, D) for g, own in zip((g_a, g_b, g_o), mats))
    w_gate = jnp.transpose(with_own(g_gate, gate_sh).reshape(NCHIP, RANK, 128), (1, 0, 2)).reshape(RANK, KEYB)
    w_gate = jnp.pad(w_gate, ((0, LRP - RANK), (0, 0)))
    b_sb = jnp.broadcast_to(b_spatial[0][:, :, None], (HA, CA, GA))

    r = _local_step(x[0], loss_target[0], norm_g, w_main_t, w_lr_t, ln_v_g, ln_v_b, w_spatial[0], b_sb,
                    w_gate, b_gate_up, gla_norm_g, w_a, w_b, w_o, final_norm_g.reshape(1, D))

    b_win = r["dwt"].reshape(NCHIP, WIN_SHARD, D)
    b_mats = [r[k].reshape(NCHIP, 2, D // NCHIP // 2, D) for k in ("dwa", "dwb", "dwo")]
    small = _pack_rows([r["dg0"], r["dlg"], r["dlb"], r["dws"], r["dbs"][:, :, 0], r["dbg"], r["dgg"], r["dgf"],
                        r["dwg"][:RANK], r["loss_cols"]], _SMALL_ROWS)
    b_small = small.reshape(2, 1, _SMALL_ROWS // 2, 128)
    mine, theirs = _reduce_gradients(place, [b_win] + b_mats + [b_small],
                                     [BY_COLUMNS, CHIP_FIRST, CHIP_FIRST, CHIP_FIRST, HALF_FIRST])

    g_small = jnp.where(core == 0, jnp.concatenate([mine[4], theirs[4]], axis=0),
                        jnp.concatenate([theirs[4], mine[4]], axis=0))
    grads = {}
    row = 0
    for name, rows in _SMALL:
        grads[name] = g_small[row:row + rows]
        row += rows
    loss = jnp.sum(grads["loss"])
    dwg_full = grads["w_gate_up"].reshape(RANK, KEYB)
    grads["w_gate_up"] = lax.dynamic_slice_in_dim(dwg_full, chip * 128, 128, axis=1)

    weights = dict(norm_g=norm_g, w_in=w_in, ln_v_g=ln_v_g, ln_v_b=ln_v_b, w_spatial=w_spatial, b_spatial=b_spatial,
                   w_gate_up=w_gate_up, b_gate_up=b_gate_up, gla_norm_g=gla_norm_g, w_branch_a=w_branch_a,
                   w_branch_b=w_branch_b, w_out=w_out, final_norm_g=final_norm_g)
    m_in = dict(norm_g=m_norm_g, w_in=m_w_in, ln_v_g=m_ln_v_g, ln_v_b=m_ln_v_b, w_spatial=m_w_spatial,
                b_spatial=m_b_spatial, w_gate_up=m_w_gate_up, b_gate_up=m_b_gate_up, gla_norm_g=m_gla_norm_g,
                w_branch_a=m_w_branch_a, w_branch_b=m_w_branch_b, w_out=m_w_out, final_norm_g=m_final_norm_g)
    v_in = dict(norm_g=v_norm_g, w_in=v_w_in, ln_v_g=v_ln_v_g, ln_v_b=v_ln_v_b, w_spatial=v_w_spatial,
                b_spatial=v_b_spatial, w_gate_up=v_w_gate_up, b_gate_up=v_b_gate_up, gla_norm_g=v_gla_norm_g,
                w_branch_a=v_w_branch_a, w_branch_b=v_w_branch_b, w_out=v_w_out, final_norm_g=v_final_norm_g)
    names = list(weights)
    small_names = [n for n in names if n != "w_in" and n not in mat_names]
    out_g, out_d, out_m, out_v = {}, {}, {}, {}
    res = _adamw_halves(place, jnp.transpose(w_in[0]), mine[0], theirs[0], jnp.transpose(m_w_in[0]),
                        jnp.transpose(v_w_in[0]), axis=1)
    out_g["w_in"], out_d["w_in"], out_m["w_in"], out_v["w_in"] = (jnp.transpose(t)[None] for t in res)
    for i, n in enumerate(mat_names):
        res = _adamw_halves(place, weights[n][0], mine[1 + i], theirs[1 + i], m_in[n][0], v_in[n][0], axis=0)
        out_g[n], out_d[n], out_m[n], out_v[n] = (t[None] for t in res)
    upd_rows = sum(weights[n].size for n in small_names) // 128
    pad_rows = -(-upd_rows // 8) * 8
    packed = [_pack_rows([t[n] for n in small_names], pad_rows) for t in (weights, grads, m_in, v_in)]
    d_s, m_s, v_s = _adamw(*packed)
    row = 0
    for n in small_names:
        shape = weights[n].shape
        rows = weights[n].size // 128
        out_g[n] = grads[n].reshape(shape)
        out_d[n], out_m[n], out_v[n] = (t[row:row + rows].reshape(shape) for t in (d_s, m_s, v_s))
        row += rows
    return (loss, r["dx"][None], *[out_g[n] for n in names], *[out_d[n] for n in names],
            *[out_m[n] for n in names], *[out_v[n] for n in names])
```

```python
import functools
import math

import jax
import jax.numpy as jnp
from jax import lax
from jax.experimental import pallas as pl
from jax.experimental.pallas import tpu as pltpu

f32 = jnp.float32
bf16 = jnp.bfloat16

D = 1024
NMAIN = 8192
LRP = 128
RANK = 16
HA, GA, CA = 8, 128, 128
HB, DK, DV, CB = 4, 128, 256, 64
KEYB = HB * DK
EPS = 1e-6
LN_EPS = 1e-5
GATE_NORM = 16.0
QSCALE = DK ** -0.5
COL_U, COL_V, COL_ZA = 0, 1, 2
COL_Q, COL_K = 6, 7
COL_VB, COL_ZB = 4, 5
COL_GATES = 3
VMEM_LIMIT = 56 * 1024 * 1024
NCHIP = 4
WIN_SHARD = 2052
LR_COL = 6144
_ANY = pl.BlockSpec(memory_space=pl.ANY)

ADAM_LR, ADAM_B1, ADAM_B2, ADAM_EPS, ADAM_WD, ADAM_STEP = 0.001, 0.9, 0.999, 1e-08, 0.01, 10

_SQRT_HALF = 0.7071067811865476
_INV_SQRT_2PI = 0.3989422804014327


def _dot(a, b):
    return jnp.dot(a, b, preferred_element_type=f32)


def _dot_nt(a, b):
    return lax.dot_general(a, b, (((1,), (1,)), ((), ())), preferred_element_type=f32)


def _dot_tn(a, b):
    return lax.dot_general(a, b, (((0,), (0,)), ((), ())), preferred_element_type=f32)


def _dot_exact(a, b):
    return jnp.dot(a, b, preferred_element_type=f32, precision=lax.Precision.HIGHEST)


def _gelu(x):
    return 0.5 * x * (1.0 + lax.erf(x * _SQRT_HALF))


def _gelu_grad(x):
    return 0.5 * (1.0 + lax.erf(x * _SQRT_HALF)) + x * (jnp.exp(-0.5 * x * x) * _INV_SQRT_2PI)


def _sigmoid(x):
    return 1.0 / (1.0 + jnp.exp(-x))


def _params(sem):
    return pltpu.CompilerParams(dimension_semantics=sem, vmem_limit_bytes=VMEM_LIMIT)


def _resident(shape):
    nd = len(shape)
    return pl.BlockSpec(shape, lambda *_: (0,) * nd, pipeline_mode=pl.Buffered(1))


def _proj_fwd(x, g0, w_main, w_lr, riders=(), tm=256, tn=1024):
    T = x.shape[0]
    nsteps = T // tm
    n = len(riders)

    def body(x_ref, g_ref, w_ref, wl_ref, *rest):
        srcs, (proj_ref, lr_ref, h_ref), dsts, sems = rest[:n], rest[n:n + 3], rest[n + 3:2 * n + 3], rest[2 * n + 3:]
        if n:
            @pl.when(pl.program_id(0) == 0)
            def _():
                _gather_start(srcs, dsts, [False] * n, sems)

        xv = x_ref[...]
        r = lax.rsqrt(jnp.mean(xv * xv, axis=-1, keepdims=True) + EPS)
        h = (xv * r * g_ref[...]).astype(bf16)
        h_ref[...] = h
        lr_ref[...] = _dot_nt(h, wl_ref[...])
        for c in range(NMAIN // tn):
            proj_ref[:, c * tn:(c + 1) * tn] = _dot_nt(h, w_ref[c * tn:(c + 1) * tn, :])

        if n:
            @pl.when(pl.program_id(0) == nsteps - 1)
            def _():
                _gather_finish(srcs, dsts, [False] * n, sems)

    return pl.pallas_call(
        body,
        grid=(nsteps,),
        in_specs=[
            pl.BlockSpec((tm, D), lambda i: (i, 0)),
            _resident((1, D)), _resident((NMAIN, D)), _resident((LRP, D)),
        ] + [_ANY] * n,
        out_specs=[
            pl.BlockSpec((tm, NMAIN), lambda i: (i, 0)),
            pl.BlockSpec((tm, LRP), lambda i: (i, 0)),
            pl.BlockSpec((tm, D), lambda i: (i, 0)),
        ] + [_ANY] * n,
        out_shape=[
            jax.ShapeDtypeStruct((T, NMAIN), f32),
            jax.ShapeDtypeStruct((T, LRP), f32),
            jax.ShapeDtypeStruct((T, D), bf16),
        ] + _gathered_shapes(riders),
        scratch_shapes=_gather_sems(n) if n else [],
        compiler_params=_params(("arbitrary",)),
        name="proj_fwd",
    )(x, g0, w_main, w_lr, *riders)


def _causal_mask():
    t = lax.broadcasted_iota(jnp.int32, (CA, CA), 0)
    s = lax.broadcasted_iota(jnp.int32, (CA, CA), 1)
    return s <= t


def _layernorm_parts(gv):
    mu = jnp.mean(gv, axis=-1, keepdims=True)
    xc = gv - mu
    rs = lax.rsqrt(jnp.mean(xc * xc, axis=-1, keepdims=True) + LN_EPS)
    return xc * rs, rs


def _mixer_a_fwd(proj, ln_g, ln_b, w_s, b_sb, tm=256):
    T = proj.shape[0]

    def body(u_ref, v_ref, za_ref, lg_ref, lb_ref, ws_ref, bs_ref, a_ref, vln_s):
        vhat, _ = _layernorm_parts(_gelu(v_ref[...]))
        vln_s[...] = (vhat * lg_ref[...] + lb_ref[...]).astype(bf16)
        mask = _causal_mask()
        for g in range(HA):
            wg = jnp.where(mask, ws_ref[g], 0.0).astype(bf16)
            cols = slice(g * GA, (g + 1) * GA)
            for c in range(tm // CA):
                rows = slice(c * CA, (c + 1) * CA)
                mixed = _dot(wg, vln_s[rows, cols]) + bs_ref[g]
                za = za_ref[rows, cols]
                a = _gelu(u_ref[rows, cols]) * mixed * (za * _sigmoid(za))
                a_ref[rows, cols] = a.astype(bf16)

    def col(cidx):
        return pl.BlockSpec((tm, D), lambda i, c=cidx: (i, c))

    return pl.pallas_call(
        body,
        grid=(T // tm,),
        in_specs=[col(COL_U), col(COL_V), col(COL_ZA), _resident((1, D)), _resident((1, D)),
                  _resident((HA, CA, CA)), _resident((HA, CA, GA))],
        out_specs=pl.BlockSpec((tm, D), lambda i: (i, 0)),
        out_shape=jax.ShapeDtypeStruct((T, D), bf16),
        scratch_shapes=[pltpu.VMEM((tm, D), bf16)],
        compiler_params=_params(("parallel",)),
        name="mixer_a_fwd",
    )(proj, proj, proj, ln_g, ln_b, w_s, b_sb)


def _mixer_a_bwd(proj, da, dproj, ln_g, ln_b, w_s, b_sb, tm=256):
    T = proj.shape[0]
    nsteps = T // tm

    def body(u_ref, v_ref, za_ref, da_ref, dp_in, lg_ref, lb_ref, ws_ref, bs_ref,
             dp_ref, dws_ref, dbs_ref, dlg_ref, dlb_ref, vln_s, dvln_s):
        del dp_in
        i = pl.program_id(0)

        @pl.when(i == 0)
        def _():
            dws_ref[...] = jnp.zeros_like(dws_ref)
            dbs_ref[...] = jnp.zeros_like(dbs_ref)
            dlg_ref[...] = jnp.zeros_like(dlg_ref)
            dlb_ref[...] = jnp.zeros_like(dlb_ref)

        v = v_ref[...]
        vhat, rs = _layernorm_parts(_gelu(v))
        vln_s[...] = (vhat * lg_ref[...] + lb_ref[...]).astype(bf16)
        mask = _causal_mask()
        for g in range(HA):
            wg = jnp.where(mask, ws_ref[g], 0.0).astype(bf16)
            cols = slice(g * GA, (g + 1) * GA)
            dw_acc = jnp.zeros((CA, CA), f32)
            db_acc = jnp.zeros((CA, 1), f32)
            for c in range(tm // CA):
                rows = slice(c * CA, (c + 1) * CA)
                vln = vln_s[rows, cols]
                mixed = _dot(wg, vln) + bs_ref[g]
                u = u_ref[rows, cols]
                za = za_ref[rows, cols]
                da_blk = da_ref[rows, cols]
                sg = _sigmoid(za)
                sz = za * sg
                gu = _gelu(u)
                dp_ref[rows, cols] = (da_blk * mixed * sz * _gelu_grad(u)).astype(bf16)
                dp_ref[rows, 2 * D + g * GA:2 * D + (g + 1) * GA] = (
                    da_blk * gu * mixed * (sg * (1.0 + za * (1.0 - sg)))).astype(bf16)
                dmixed = da_blk * gu * sz
                dmb = dmixed.astype(bf16)
                dvln_s[rows, cols] = _dot_tn(wg, dmb)
                dw_acc = dw_acc + _dot_nt(dmb, vln)
                db_acc = db_acc + jnp.sum(dmixed, axis=-1, keepdims=True)
            dws_ref[g] += dw_acc
            dbs_ref[g] += jnp.broadcast_to(db_acc, (CA, GA))

        dvln = dvln_s[...]
        dlg_ref[...] += jnp.sum(dvln * vhat, axis=0, keepdims=True)
        dlb_ref[...] += jnp.sum(dvln, axis=0, keepdims=True)
        dvhat = dvln * lg_ref[...]
        dgv = rs * (dvhat - jnp.mean(dvhat, axis=-1, keepdims=True)
                    - vhat * jnp.mean(dvhat * vhat, axis=-1, keepdims=True))
        dp_ref[:, D:2 * D] = (dgv * _gelu_grad(v)).astype(bf16)

        @pl.when(i == nsteps - 1)
        def _():
            for g in range(HA):
                dws_ref[g] = jnp.where(mask, dws_ref[g], 0.0)

    def col(cidx):
        return pl.BlockSpec((tm, D), lambda i, c=cidx: (i, c))

    return pl.pallas_call(
        body,
        grid=(nsteps,),
        in_specs=[col(COL_U), col(COL_V), col(COL_ZA), pl.BlockSpec((tm, D), lambda i: (i, 0)),
                  pl.BlockSpec(memory_space=pl.ANY),
                  _resident((1, D)), _resident((1, D)), _resident((HA, CA, CA)), _resident((HA, CA, GA))],
        out_specs=[pl.BlockSpec((tm, 3 * D), lambda i: (i, 0)),
                   _resident((HA, CA, CA)), _resident((HA, CA, GA)), _resident((1, D)), _resident((1, D))],
        out_shape=[jax.ShapeDtypeStruct(dproj.shape, dproj.dtype),
                   jax.ShapeDtypeStruct((HA, CA, CA), f32), jax.ShapeDtypeStruct((HA, CA, GA), f32),
                   jax.ShapeDtypeStruct((1, D), f32), jax.ShapeDtypeStruct((1, D), f32)],
        scratch_shapes=[pltpu.VMEM((tm, D), bf16), pltpu.VMEM((tm, D), f32)],
        input_output_aliases={4: 0},
        compiler_params=_params(("arbitrary",)),
        name="mixer_a_bwd",
    )(proj, proj, proj, da, dproj, ln_g, ln_b, w_s, b_sb)


def _tri(n, upper):
    r = lax.broadcasted_iota(jnp.int32, (n, n), 0)
    c = lax.broadcasted_iota(jnp.int32, (n, n), 1)
    return jnp.where((c >= r) if upper else (c <= r), 1.0, 0.0).astype(f32)


def _chunk_tri(n, upper):
    r = lax.broadcasted_iota(jnp.int32, (n, n), 0)
    c = lax.broadcasted_iota(jnp.int32, (n, n), 1)
    shift = CB.bit_length() - 1
    same_chunk = jnp.right_shift(r, shift) == jnp.right_shift(c, shift)
    return jnp.where(same_chunk & ((c >= r) if upper else (c <= r)), 1.0, 0.0).astype(f32)


def _log_alpha(lr, wg, bg):
    logit = _dot(lr.astype(bf16), wg.astype(bf16)) + bg
    la = (jnp.minimum(logit, 0.0) - jnp.log1p(jnp.exp(-jnp.abs(logit)))) * (1.0 / GATE_NORM)
    return logit, la


def _gla_fwd(proj, lr, w_gate, b_gate, gla_g, tm=256):
    T = proj.shape[0]
    nchunk = T // CB
    cpb = tm // CB

    def body(q_ref, k_ref, v_ref, zb_ref, lr_ref, wg_ref, bg_ref, gg_ref,
             o_ref, ob_ref, st_ref, state, la_s):
        @pl.when(pl.program_id(0) == 0)
        def _():
            state[...] = jnp.zeros_like(state)

        _, la = _log_alpha(lr_ref[...], wg_ref[...], bg_ref[...])
        la_s[...] = _dot_exact(_chunk_tri(tm, upper=False), la)
        causal = _tri(CB, upper=False) > 0.5
        states = [state[hd] for hd in range(HB)]
        for c in range(cpb):
            rows = slice(c * CB, (c + 1) * CB)
            b = la_s[rows, :]
            bl = b[CB - 1:CB, :]
            bm = b[CB // 2 - 1:CB // 2, :]
            q = q_ref[rows, :] * QSCALE
            k = k_ref[rows, :]
            qi_all = (q * jnp.exp(b - bm)).astype(bf16)
            ki_all = (k * jnp.exp(bm - b)).astype(bf16)
            qe_all = (q * jnp.exp(b)).astype(bf16)
            ks_all = (k * jnp.exp(bl - b)).astype(bf16)
            e_l = jnp.exp(bl)
            for hd in range(HB):
                kc = slice(hd * DK, (hd + 1) * DK)
                vc = slice(hd * DV, (hd + 1) * DV)
                v = v_ref[rows, vc].astype(bf16)
                p = jnp.where(causal, _dot_nt(qi_all[:, kc], ki_all[:, kc]), 0.0).astype(bf16)
                s0 = states[hd]
                st_ref[c, hd] = s0
                o = _dot(p, v) + _dot_nt(qe_all[:, kc], s0.astype(bf16))
                states[hd] = s0 * e_l[:, kc] + _dot_tn(v, ks_all[:, kc])
                o_ref[rows, vc] = o
                ro = lax.rsqrt(jnp.mean(o * o, axis=-1, keepdims=True) + EPS)
                zb = zb_ref[rows, vc]
                ob_ref[rows, vc] = (o * ro * gg_ref[...] * (zb * _sigmoid(zb))).astype(bf16)
        for hd in range(HB):
            state[hd] = states[hd]

    return pl.pallas_call(
        body,
        grid=(T // tm,),
        in_specs=[pl.BlockSpec((tm, KEYB), lambda i: (i, COL_Q)),
                  pl.BlockSpec((tm, KEYB), lambda i: (i, COL_K)),
                  pl.BlockSpec((tm, D), lambda i: (i, COL_VB)),
                  pl.BlockSpec((tm, D), lambda i: (i, COL_ZB)),
                  pl.BlockSpec((tm, LRP), lambda i: (i, 0)),
                  _resident((LRP, KEYB)), _resident((1, KEYB)), _resident((1, DV))],
        out_specs=[pl.BlockSpec((tm, D), lambda i: (i, 0)),
                   pl.BlockSpec((tm, D), lambda i: (i, 0)),
                   pl.BlockSpec((cpb, HB, DV, DK), lambda i: (i, 0, 0, 0))],
        out_shape=[jax.ShapeDtypeStruct((T, D), f32), jax.ShapeDtypeStruct((T, D), bf16),
                   jax.ShapeDtypeStruct((nchunk, HB, DV, DK), f32)],
        scratch_shapes=[pltpu.VMEM((HB, DV, DK), f32), pltpu.VMEM((tm, KEYB), f32)],
        compiler_params=_params(("arbitrary",)),
        name="gla_fwd",
    )(proj, proj, proj, proj, lr, w_gate, b_gate, gla_g)


def _gla_bwd(proj, lr, o, states, dob, dproj, w_gate, b_gate, gla_g, tm=256):
    T = proj.shape[0]
    cpb = tm // CB
    nb = T // tm

    def body(q_ref, k_ref, v_ref, zb_ref, lr_ref, o_ref, st_ref, dob_ref, dp_in, wg_ref, bg_ref, gg_ref,
             dp_ref, dlr_ref, dwg_ref, dbg_ref, dgg_ref, dstate, la_s, dlogit_s, tail_s):
        del dp_in
        step = pl.program_id(0)

        @pl.when(step == 0)
        def _():
            dstate[...] = jnp.zeros_like(dstate)
            dwg_ref[...] = jnp.zeros_like(dwg_ref)
            dbg_ref[...] = jnp.zeros_like(dbg_ref)
            dgg_ref[...] = jnp.zeros_like(dgg_ref)

        lr_v = lr_ref[...]
        logit, la = _log_alpha(lr_v, wg_ref[...], bg_ref[...])
        la_s[...] = _dot_exact(_chunk_tri(tm, upper=False), la)
        causal = _tri(CB, upper=False) > 0.5
        gg = gg_ref[...]
        dgg_acc = jnp.zeros((1, DV), f32)
        dstates = [dstate[hd] for hd in range(HB)]
        for c in reversed(range(cpb)):
            rows = slice(c * CB, (c + 1) * CB)
            b = la_s[rows, :]
            bl = b[CB - 1:CB, :]
            bm = b[CB // 2 - 1:CB // 2, :]
            eb_all, eqm_all, ekm_all = jnp.exp(b), jnp.exp(b - bm), jnp.exp(bm - b)
            eks_all, el_all = jnp.exp(bl - b), jnp.exp(bl)
            q_all = q_ref[rows, :] * QSCALE
            k_all = k_ref[rows, :]
            qi_all = (q_all * eqm_all).astype(bf16)
            ki_all = (k_all * ekm_all).astype(bf16)
            qe_all = (q_all * eb_all).astype(bf16)
            ksf_all = k_all * eks_all
            ks_all = ksf_all.astype(bf16)
            for hd in range(HB):
                kc = slice(hd * DK, (hd + 1) * DK)
                vc = slice(hd * DV, (hd + 1) * DV)
                o_h = o_ref[rows, vc]
                ro = lax.rsqrt(jnp.mean(o_h * o_h, axis=-1, keepdims=True) + EPS)
                ohat = o_h * ro
                zb = zb_ref[rows, vc]
                sg = _sigmoid(zb)
                dob_h = dob_ref[rows, vc]
                don = dob_h * (zb * sg)
                dp_ref[rows, 2 * D + hd * DV:2 * D + (hd + 1) * DV] = (
                    dob_h * ohat * gg * (sg * (1.0 + zb * (1.0 - sg)))).astype(bf16)
                dgg_acc = dgg_acc + jnp.sum(don * ohat, axis=0, keepdims=True)
                dohat = don * gg
                do = (ro * (dohat - ohat * jnp.mean(dohat * ohat, axis=-1, keepdims=True))).astype(bf16)
                e_b, e_qm, e_km, e_ks, e_l = eb_all[:, kc], eqm_all[:, kc], ekm_all[:, kc], eks_all[:, kc], el_all[:, kc]
                q, k, ks_f = q_all[:, kc], k_all[:, kc], ksf_all[:, kc]
                qi, ki, qe, ks = qi_all[:, kc], ki_all[:, kc], qe_all[:, kc], ks_all[:, kc]
                v = v_ref[rows, vc].astype(bf16)
                p = jnp.where(causal, _dot_nt(qi, ki), 0.0).astype(bf16)
                s0 = st_ref[c, hd]
                ds = dstates[hd]
                ds_b = ds.astype(bf16)
                dv = _dot_tn(p, do) + _dot_nt(ks, ds_b)
                dpm = jnp.where(causal, _dot_nt(do, v), 0.0).astype(bf16)
                dqi = _dot(dpm, ki)
                dki = _dot_tn(dpm, qi)
                dqe = _dot(do, s0.astype(bf16))
                dks = _dot(v, ds_b)
                dq_s = dqi * e_qm + dqe * e_b
                dk = dki * e_km + dks * e_ks
                tail = (jnp.sum(dks * ks_f, axis=0, keepdims=True)
                        + e_l * jnp.sum(ds * s0, axis=0, keepdims=True))
                dstates[hd] = _dot_tn(do, qe) + ds * e_l
                dp_ref[rows, kc] = (dq_s * QSCALE).astype(bf16)
                dp_ref[rows, KEYB + hd * DK:KEYB + (hd + 1) * DK] = dk.astype(bf16)
                dp_ref[rows, D + hd * DV:D + (hd + 1) * DV] = dv.astype(bf16)
                dlogit_s[rows, kc] = dq_s * q - dk * k
                tail_s[rows, kc] = jnp.broadcast_to(tail, (CB, DK))
        for hd in range(HB):
            dstate[hd] = dstates[hd]
        dgg_ref[...] += dgg_acc
        dg = _dot_exact(_chunk_tri(tm, upper=True), dlogit_s[...]) + tail_s[...]
        dlogit = dg * (1.0 / GATE_NORM) * _sigmoid(-logit)
        dbg_ref[...] += jnp.sum(dlogit, axis=0, keepdims=True)
        dlb = dlogit.astype(bf16)
        dlr_ref[...] = _dot_nt(dlb, wg_ref[...].astype(bf16)).astype(bf16)
        dwg_ref[...] += _dot_tn(lr_v.astype(bf16), dlb)

    def rev(cidx):
        return lambda i, c=cidx: (nb - 1 - i, c)

    return pl.pallas_call(
        body,
        grid=(nb,),
        in_specs=[pl.BlockSpec((tm, KEYB), rev(COL_Q)),
                  pl.BlockSpec((tm, KEYB), rev(COL_K)),
                  pl.BlockSpec((tm, D), rev(COL_VB)),
                  pl.BlockSpec((tm, D), rev(COL_ZB)),
                  pl.BlockSpec((tm, LRP), rev(0)),
                  pl.BlockSpec((tm, D), rev(0)),
                  pl.BlockSpec((cpb, HB, DV, DK), lambda i: (nb - 1 - i, 0, 0, 0)),
                  pl.BlockSpec((tm, D), rev(0)),
                  pl.BlockSpec(memory_space=pl.ANY),
                  _resident((LRP, KEYB)), _resident((1, KEYB)), _resident((1, DV))],
        out_specs=[pl.BlockSpec((tm, 3 * D), rev(1)),
                   pl.BlockSpec((tm, LRP), rev(0)),
                   _resident((LRP, KEYB)), _resident((1, KEYB)), _resident((1, DV))],
        out_shape=[jax.ShapeDtypeStruct(dproj.shape, dproj.dtype),
                   jax.ShapeDtypeStruct((T, LRP), bf16),
                   jax.ShapeDtypeStruct((LRP, KEYB), f32), jax.ShapeDtypeStruct((1, KEYB), f32),
                   jax.ShapeDtypeStruct((1, DV), f32)],
        scratch_shapes=[pltpu.VMEM((HB, DV, DK), f32)] + [pltpu.VMEM((tm, KEYB), f32)] * 3,
        input_output_aliases={8: 0},
        compiler_params=_params(("arbitrary",)),
        name="gla_bwd",
    )(proj, proj, proj, proj, lr, o, states, dob, dproj, w_gate, b_gate, gla_g)


def _merge_fwd_bwd(x, tgt, proj, a, ob, w_a, w_b, w_o, g_f, tm=256):
    T = x.shape[0]

    def body(x_ref, t_ref, gt_ref, a_ref, ob_ref, wa_ref, wb_ref, wo_ref, gf_ref,
             dp_ref, dy_ref, da_ref, dob_ref, dwa_ref, dwb_ref, dwo_ref, dgf_ref, loss_ref):
        @pl.when(pl.program_id(0) == 0)
        def _():
            dwa_ref[...] = jnp.zeros_like(dwa_ref)
            dwb_ref[...] = jnp.zeros_like(dwb_ref)
            dwo_ref[...] = jnp.zeros_like(dwo_ref)
            dgf_ref[...] = jnp.zeros_like(dgf_ref)
            loss_ref[...] = jnp.zeros_like(loss_ref)

        ga = _sigmoid(gt_ref[:, :D])
        gb = _sigmoid(gt_ref[:, D:])
        a_v = a_ref[...]
        ob_v = ob_ref[...]
        pa = _dot(a_v, wa_ref[...])
        pb = _dot(ob_v, wb_ref[...])
        mb = (ga * pa + gb * pb).astype(bf16)
        y = x_ref[...] + _dot(mb, wo_ref[...])
        r1 = lax.rsqrt(jnp.mean(y * y, axis=-1, keepdims=True) + EPS)
        yhat = y * r1
        gf = gf_ref[...]
        err = yhat * gf - t_ref[...]
        loss_ref[...] += jnp.sum(err * err, axis=0, keepdims=True) * (0.5 / D)
        dout = err * (1.0 / D)
        dgf_ref[...] += jnp.sum(dout * yhat, axis=0, keepdims=True)
        dyn = dout * gf
        dy = r1 * (dyn - yhat * jnp.mean(dyn * yhat, axis=-1, keepdims=True))
        dy_ref[...] = dy
        dyb = dy.astype(bf16)
        dwo_ref[...] += _dot_tn(mb, dyb)
        dm = _dot_nt(dyb, wo_ref[...])
        dpa = (dm * ga).astype(bf16)
        dpb = (dm * gb).astype(bf16)
        dp_ref[:, :D] = (dm * pa * ga * (1.0 - ga)).astype(bf16)
        dp_ref[:, D:] = (dm * pb * gb * (1.0 - gb)).astype(bf16)
        dwa_ref[...] += _dot_tn(a_v, dpa)
        dwb_ref[...] += _dot_tn(ob_v, dpb)
        da_ref[...] = _dot_nt(dpa, wa_ref[...])
        dob_ref[...] = _dot_nt(dpb, wb_ref[...])

    row = lambda: pl.BlockSpec((tm, D), lambda i: (i, 0))
    return pl.pallas_call(
        body,
        grid=(T // tm,),
        in_specs=[row(), row(), pl.BlockSpec((tm, 2 * D), lambda i: (i, COL_GATES)), row(), row(),
                  _resident((D, D)), _resident((D, D)), _resident((D, D)), _resident((1, D))],
        out_specs=[pl.BlockSpec((tm, 2 * D), lambda i: (i, COL_GATES)), row(), row(), row(),
                   _resident((D, D)), _resident((D, D)), _resident((D, D)), _resident((1, D)), _resident((1, D))],
        out_shape=[jax.ShapeDtypeStruct((T, NMAIN), bf16),
                   jax.ShapeDtypeStruct((T, D), f32), jax.ShapeDtypeStruct((T, D), f32),
                   jax.ShapeDtypeStruct((T, D), f32),
                   jax.ShapeDtypeStruct((D, D), f32), jax.ShapeDtypeStruct((D, D), f32),
                   jax.ShapeDtypeStruct((D, D), f32),
                   jax.ShapeDtypeStruct((1, D), f32), jax.ShapeDtypeStruct((1, D), f32)],
        compiler_params=_params(("arbitrary",)),
        name="merge_fwd_bwd",
    )(x, tgt, proj, a, ob, w_a, w_b, w_o, g_f)


def _dx_bwd(x, dy, dproj, dlr, g0, w_main, w_lr, riders=(), kinds=(), tm=256):
    T = x.shape[0]
    nsteps = T // tm
    n = len(riders)

    def body(x_ref, dy_ref, dp_ref, dl_ref, g_ref, w_ref, wl_ref, *rest):
        srcs, (dx_ref, dg_ref), dsts, sems = rest[:n], rest[n:n + 2], rest[n + 2:2 * n + 2], rest[2 * n + 2:]

        @pl.when(pl.program_id(0) == 0)
        def _():
            dg_ref[...] = jnp.zeros_like(dg_ref)
            for cp in _halves_copies(srcs, dsts, kinds, sems) if n else []:
                cp.start()

        xv = x_ref[...]
        r = lax.rsqrt(jnp.mean(xv * xv, axis=-1, keepdims=True) + EPS)
        xhat = xv * r
        dh = _dot(dp_ref[...], w_ref[...]) + _dot(dl_ref[...], wl_ref[...])
        dg_ref[...] += jnp.sum(dh * xhat, axis=0, keepdims=True)
        t = dh * g_ref[...]
        dx_ref[...] = dy_ref[...] + r * (t - xhat * jnp.mean(t * xhat, axis=-1, keepdims=True))

        if n:
            @pl.when(pl.program_id(0) == nsteps - 1)
            def _():
                for cp in _halves_copies(srcs, dsts, kinds, sems):
                    cp.wait()

    row = lambda: pl.BlockSpec((tm, D), lambda i: (i, 0))
    return pl.pallas_call(
        body,
        grid=(nsteps,),
        in_specs=[row(), row(), pl.BlockSpec((tm, NMAIN), lambda i: (i, 0)),
                  pl.BlockSpec((tm, LRP), lambda i: (i, 0)),
                  _resident((1, D)), _resident((NMAIN, D)), _resident((LRP, D))] + [_ANY] * n,
        out_specs=[row(), _resident((1, D))] + [_ANY] * n,
        out_shape=[jax.ShapeDtypeStruct((T, D), f32), jax.ShapeDtypeStruct((1, D), f32)] + _halves_shapes(riders, kinds),
        scratch_shapes=_halves_sems(riders) if n else [],
        compiler_params=_params(("arbitrary",)),
        name="dx_bwd",
    )(x, dy, dproj, dlr, g0, w_main, w_lr, *riders)


def _dw_in(h, dproj, dlr, tm=1024, tn=1024):
    T = h.shape[0]
    tm = min(tm, T)
    nj, nk = NMAIN // tn, T // tm
    lr_tile = LR_COL // tn

    def body(h_ref, dp_ref, dl_ref, out_ref, acc, lr_acc, sems, lr_sem):
        j, k = pl.program_id(0), pl.program_id(1)
        slot = j % 2

        def tile_out(jj, s):
            row = pl.multiple_of(jj * tn + jnp.where(jj >= lr_tile, RANK, 0), 8)
            return pltpu.make_async_copy(acc.at[s], out_ref.at[pl.ds(row, tn)], sems.at[s])

        lr_rows = pltpu.make_async_copy(lr_acc.at[pl.ds(0, RANK)], out_ref.at[pl.ds(LR_COL, RANK)], lr_sem)

        @pl.when(j == 0)
        def _():
            @pl.when(k == 0)
            def _():
                lr_acc[...] = jnp.zeros_like(lr_acc)

            lr_acc[...] += _dot_tn(dl_ref[...], h_ref[...])

            @pl.when(k == nk - 1)
            def _():
                lr_rows.start()

        @pl.when(k == 0)
        def _():
            acc[slot] = jnp.zeros((tn, D), f32)

        acc[slot] += _dot_tn(dp_ref[...], h_ref[...])

        @pl.when(k == nk - 1)
        def _():
            tile_out(j, slot).start()

            @pl.when(j > 0)
            def _():
                tile_out(j - 1, 1 - slot).wait()

            @pl.when(j == nj - 1)
            def _():
                tile_out(j, slot).wait()
                lr_rows.wait()

    return pl.pallas_call(
        body,
        grid=(nj, nk),
        in_specs=[pl.BlockSpec((tm, D), lambda j, k: (k, 0)), pl.BlockSpec((tm, tn), lambda j, k: (k, j)),
                  pl.BlockSpec((tm, LRP), lambda j, k: (k, 0))],
        out_specs=_ANY,
        out_shape=jax.ShapeDtypeStruct((NMAIN + RANK, D), f32),
        scratch_shapes=[pltpu.VMEM((2, tn, D), f32), pltpu.VMEM((LRP, D), f32),
                        pltpu.SemaphoreType.DMA((2,)), pltpu.SemaphoreType.DMA],
        compiler_params=_params(("arbitrary", "arbitrary")),
        name="dw_in",
    )(h, dproj, dlr)


MESH = pl.DeviceIdType.MESH


def _place():
    x, y, c = lax.axis_index("x"), lax.axis_index("y"), lax.axis_index("c")
    others = [(1 - x, y), (x, 1 - y), (1 - x, 1 - y)]
    return x, y, c, 2 * x + y, others


def _remote(src, dst, send_sem, recv_sem, to):
    return pltpu.make_async_remote_copy(src_ref=src, dst_ref=dst, send_sem=send_sem, recv_sem=recv_sem,
                                        device_id=to, device_id_type=MESH)


def _half(ref, e, by_columns):
    if not by_columns:
        return ref.at[e]
    hw = ref.shape[-1] // 2
    return ref.at[:, pl.ds(pl.multiple_of(e * hw, 128), hw)]


def _gather_weights(shards, by_columns):
    n = len(shards)

    def body(*refs):
        _gather_start(refs[:n], refs[n:2 * n], by_columns, refs[2 * n:])
        _gather_finish(refs[:n], refs[n:2 * n], by_columns, refs[2 * n:])

    return pl.pallas_call(
        body,
        in_specs=[_ANY] * n,
        out_specs=[_ANY] * n,
        out_shape=_gathered_shapes(shards),
        scratch_shapes=_gather_sems(n),
        name="gather_weights",
    )(*shards)


def _gathered_shapes(shards):
    return [jax.ShapeDtypeStruct((NCHIP,) + s.shape, s.dtype) for s in shards]


def _gather_sems(n):
    return [pltpu.SemaphoreType.DMA((3, n))] * 4


def _gather_copies(srcs, dsts, by_columns, sems, sends_only):
    n = len(srcs)
    send_sems, recv_sems, pass_send, pass_recv = sems
    x, y, c, me, others = _place()
    sibling = (x, y, 1 - c)

    def src(a, e):
        return _half(srcs[a], e, by_columns[a])

    def dst(a, j, e):
        return _half(dsts[a].at[j], e, by_columns[a])

    sends, arrivals, passes, passed = [], [], [], []
    for k, (cx, cy) in enumerate(others):
        j = 2 * cx + cy
        for a in range(n):
            sends.append(_remote(src(a, c), dst(a, me, c), send_sems.at[k, a], recv_sems.at[k, a], (cx, cy, c)))
            if not sends_only:
                arrivals.append(_remote(src(a, c), dst(a, j, c), send_sems.at[k, a], recv_sems.at[k, a], (cx, cy, c)))
                passes.append(_remote(dst(a, j, c), dst(a, j, c), pass_send.at[k, a], pass_recv.at[k, a], sibling))
                passed.append(_remote(src(a, c), dst(a, j, 1 - c), pass_send.at[k, a], pass_recv.at[k, a], sibling))
    return sends, arrivals, passes, passed


def _gather_start(srcs, dsts, by_columns, sems):
    for cp in _gather_copies(srcs, dsts, by_columns, sems, sends_only=True)[0]:
        cp.start()


def _gather_finish(srcs, dsts, by_columns, sems):
    sends, arrivals, passes, passed = _gather_copies(srcs, dsts, by_columns, sems, sends_only=False)
    for arrival, cp in zip(arrivals, passes):
        arrival.wait_recv()
        cp.start()
    for arrival in passed:
        arrival.wait_recv()
    for cp in sends + passes:
        cp.wait_send()


HALF_FIRST, CHIP_FIRST, BY_COLUMNS = "half_first", "chip_first", "by_columns"


def _sibling_halves(bufs, kinds):
    n = len(bufs)

    def body(*refs):
        cps = _halves_copies(refs[:n], refs[n:2 * n], kinds, refs[2 * n:])
        for cp in cps:
            cp.start()
        for cp in cps:
            cp.wait()

    return pl.pallas_call(
        body,
        in_specs=[_ANY] * n,
        out_specs=[_ANY] * n,
        out_shape=_halves_shapes(bufs, kinds),
        scratch_shapes=_halves_sems(bufs),
        name="sibling_halves",
    )(*bufs)


def _halves_shapes(bufs, kinds):
    def landed(b, kind):
        if kind == HALF_FIRST:
            return b.shape[1:]
        if kind == CHIP_FIRST:
            return (b.shape[0],) + b.shape[2:]
        return b.shape[:2] + (b.shape[2] // 2,)

    return [jax.ShapeDtypeStruct(landed(b, kind), b.dtype) for b, kind in zip(bufs, kinds)]


def _halves_sems(bufs):
    return [pltpu.SemaphoreType.DMA((len(bufs), NCHIP))] * 2


def _halves_copies(srcs, dsts, kinds, sems):
    send_sems, recv_sems = sems
    x, y, c, _, _ = _place()
    cps = []
    for a, kind in enumerate(kinds):
        if kind == HALF_FIRST:
            cps.append(_remote(srcs[a].at[1 - c], dsts[a], send_sems.at[a, 0], recv_sems.at[a, 0], (x, y, 1 - c)))
        else:
            cps += [_remote(_half(srcs[a].at[j], 1 - c, kind == BY_COLUMNS), dsts[a].at[j],
                            send_sems.at[a, j], recv_sems.at[a, j], (x, y, 1 - c)) for j in range(srcs[a].shape[0])]
    return cps


def _chip_exchange(parts):
    n = len(parts)

    def body(*refs):
        srcs, dsts = refs[:n], refs[n:2 * n]
        send_sems, recv_sems = refs[2 * n:]
        x, y, c, me, others = _place()

        def part(a, j):
            return srcs[a].at[j if parts[a].shape[0] == NCHIP else 0]

        sends = [_remote(part(a, 2 * cx + cy), dsts[a].at[k], send_sems.at[k, a], recv_sems.at[k, a], (cx, cy, c))
                 for k, (cx, cy) in enumerate(others) for a in range(n)]
        for cp in sends:
            cp.start()
        for cp in sends:
            cp.wait()

    return pl.pallas_call(
        body,
        in_specs=[_ANY] * n,
        out_specs=[_ANY] * n,
        out_shape=[jax.ShapeDtypeStruct((3,) + p.shape[1:], p.dtype) for p in parts],
        scratch_shapes=[pltpu.SemaphoreType.DMA((3, n))] * 2,
        name="chip_exchange",
    )(*parts)


def _sibling_swap(halves):
    n = len(halves)

    def body(*refs):
        srcs, dsts = refs[:n], refs[n:2 * n]
        send_sems, recv_sems = refs[2 * n:]
        x, y, c, _, _ = _place()
        cps = [_remote(srcs[a], dsts[a], send_sems.at[a], recv_sems.at[a], (x, y, 1 - c)) for a in range(n)]
        for cp in cps:
            cp.start()
        for cp in cps:
            cp.wait()

    return pl.pallas_call(
        body,
        in_specs=[_ANY] * n,
        out_specs=[_ANY] * n,
        out_shape=[jax.ShapeDtypeStruct(s.shape, s.dtype) for s in halves],
        scratch_shapes=[pltpu.SemaphoreType.DMA((n,))] * 2,
        name="sibling_swap",
    )(*halves)


def _tile(h, w, operands=5):
    if h % 128 == 0:
        return 128, w
    budget = VMEM_LIMIT * 3 // 4 // (2 * operands * 4)
    tw = w
    while h * tw > budget and tw % 256 == 0:
        tw //= 2
    return h, tw


def _pair_sum(place, buf, kind, got, out_dtype):
    nj, h, w = got.shape
    th, tw = _tile(h, w)
    nq = w // tw

    def body(p_ref, a_ref, b_ref, o_ref):
        del p_ref
        o_ref[...] = (a_ref[...] + b_ref[...]).astype(out_dtype)

    if kind == HALF_FIRST:
        mine = pl.BlockSpec((None, None, th, tw), lambda j, r, q, p: (p[0], j, r, q))
    elif kind == CHIP_FIRST:
        mine = pl.BlockSpec((None, None, th, tw), lambda j, r, q, p: (j, p[0], r, q))
    else:
        mine = pl.BlockSpec((None, th, tw), lambda j, r, q, p: (j, r, p[0] * nq + q))
    return pl.pallas_call(
        body,
        grid_spec=pltpu.PrefetchScalarGridSpec(
            num_scalar_prefetch=1,
            grid=(nj, h // th, w // tw),
            in_specs=[mine, pl.BlockSpec((None, th, tw), lambda j, r, q, p: (j, r, q))],
            out_specs=pl.BlockSpec((None, th, tw), lambda j, r, q, p: (j, r, q)),
        ),
        out_shape=jax.ShapeDtypeStruct((nj, h, w), out_dtype),
        compiler_params=_params(("parallel", "parallel", "parallel")),
        name="pair_sum",
    )(place, buf, got)


def _chip_sum(place, part, slots):
    nj, h, w = part.shape
    th, tw = _tile(h, w)

    def body(p_ref, own_ref, s_ref, o_ref):
        me = p_ref[1]
        own = own_ref[...].astype(f32)
        by_flip = {2: s_ref[0].astype(f32), 1: s_ref[1].astype(f32), 3: s_ref[2].astype(f32)}
        acc = None
        for j in range(NCHIP):
            flip = me ^ j
            term = jnp.where(flip == 0, own, jnp.where(flip == 2, by_flip[2], jnp.where(flip == 1, by_flip[1], by_flip[3])))
            acc = term if acc is None else acc + term
        o_ref[...] = acc

    return pl.pallas_call(
        body,
        grid_spec=pltpu.PrefetchScalarGridSpec(
            num_scalar_prefetch=1,
            grid=(h // th, w // tw),
            in_specs=[pl.BlockSpec((None, th, tw), lambda r, q, p: (p[1] if nj == NCHIP else 0, r, q)),
                      pl.BlockSpec((3, th, tw), lambda r, q, p: (0, r, q))],
            out_specs=pl.BlockSpec((th, tw), lambda r, q, p: (r, q)),
        ),
        out_shape=jax.ShapeDtypeStruct((h, w), f32),
        compiler_params=_params(("parallel", "parallel")),
        name="chip_sum",
    )(place, part, slots)


def _adamw_math(w, g, m, v):
    nm = ADAM_B1 * m + (1.0 - ADAM_B1) * g
    nv = ADAM_B2 * v + (1.0 - ADAM_B2) * (g * g)
    m_hat = nm / (1.0 - ADAM_B1 ** ADAM_STEP)
    v_hat = nv / (1.0 - ADAM_B2 ** ADAM_STEP)
    return -ADAM_LR * (m_hat / (jnp.sqrt(v_hat) + ADAM_EPS) + ADAM_WD * w), nm, nv


def _adamw(w, g, m, v):
    rows, width = w.shape
    th, tw = _tile(rows, width, operands=7)

    def body(w_ref, g_ref, m_ref, v_ref, d_ref, nm_ref, nv_ref):
        d_ref[...], nm_ref[...], nv_ref[...] = _adamw_math(w_ref[...], g_ref[...], m_ref[...], v_ref[...])

    spec = pl.BlockSpec((th, tw), lambda r, q: (r, q))
    return pl.pallas_call(
        body,
        grid=(rows // th, width // tw),
        in_specs=[spec] * 4,
        out_specs=[spec] * 3,
        out_shape=[jax.ShapeDtypeStruct((rows, width), f32)] * 3,
        compiler_params=_params(("parallel", "parallel")),
        name="adamw",
    )(w, g, m, v)


def _adamw_halves(place, w, mine, got, m, v, axis):
    rows, width = w.shape
    h, hw = mine.shape
    th, tw = _tile(h, hw, operands=10)
    nr, nq = h // th, hw // tw

    def body(p_ref, w_ref, a_ref, b_ref, m_ref, v_ref, g_ref, d_ref, nm_ref, nv_ref):
        g = jnp.where(pl.program_id(0) == p_ref[0], a_ref[...], b_ref[...])
        g_ref[...] = g
        d_ref[...], nm_ref[...], nv_ref[...] = _adamw_math(w_ref[...], g, m_ref[...], v_ref[...])

    if axis == 0:
        full = pl.BlockSpec((th, tw), lambda e, r, q, p: (e * nr + r, q))
    else:
        full = pl.BlockSpec((th, tw), lambda e, r, q, p: (r, e * nq + q))
    half = pl.BlockSpec((th, tw), lambda e, r, q, p: (r, q))
    return pl.pallas_call(
        body,
        grid_spec=pltpu.PrefetchScalarGridSpec(
            num_scalar_prefetch=1,
            grid=(2, nr, nq),
            in_specs=[full, half, half, full, full],
            out_specs=[full] * 4,
        ),
        out_shape=[jax.ShapeDtypeStruct((rows, width), f32)] * 4,
        compiler_params=_params(("parallel", "parallel", "parallel")),
        name="adamw_halves",
    )(place, w, mine, got, m, v)


def _reduce_gradients(place, bufs, kinds, got):
    parts = [_pair_sum(place, b, kind, r, bf16 if r.shape[0] == NCHIP else f32) for b, kind, r in zip(bufs, kinds, got)]
    slots = _chip_exchange(parts)
    mine = [_chip_sum(place, p, s) for p, s in zip(parts, slots)]
    theirs = _sibling_swap(mine)
    return mine, theirs


_SMALL = (("norm_g", 8), ("ln_v_g", 8), ("ln_v_b", 8), ("w_spatial", 1024), ("b_spatial", 8), ("b_gate_up", 4),
          ("gla_norm_g", 2), ("final_norm_g", 8), ("w_gate_up", 64), ("loss", 8))
_SMALL_ROWS = 1152


def _pack_rows(arrays, rows):
    flat = jnp.concatenate([a.reshape(-1, 128) for a in arrays], axis=0)
    return jnp.pad(flat, ((0, rows - flat.shape[0]), (0, 0)))


def kernel(x, norm_g, w_in, ln_v_g, ln_v_b, w_spatial, b_spatial, w_gate_up, b_gate_up, gla_norm_g, w_branch_a, w_branch_b, w_out, final_norm_g, loss_target, m_norm_g, m_w_in, m_ln_v_g, m_ln_v_b, m_w_spatial, m_b_spatial, m_w_gate_up, m_b_gate_up, m_gla_norm_g, m_w_branch_a, m_w_branch_b, m_w_out, m_final_norm_g, v_norm_g, v_w_in, v_ln_v_g, v_ln_v_b, v_w_spatial, v_b_spatial, v_w_gate_up, v_b_gate_up, v_gla_norm_g, v_w_branch_a, v_w_branch_b, v_w_out, v_final_norm_g):
    chip = 2 * lax.axis_index("x") + lax.axis_index("y")
    core = lax.axis_index("c")
    place = jnp.stack([core, chip]).astype(jnp.int32)
    mat_names = ("w_branch_a", "w_branch_b", "w_out")

    wt_shard = jnp.transpose(w_in[0]).astype(bf16)
    mats = [w[0].astype(bf16).reshape(2, D // NCHIP // 2, D) for w in (w_branch_a, w_branch_b, w_out)]
    gate_sh = w_gate_up[0].reshape(2, RANK // 2, 128)
    (g_win,) = _gather_weights([wt_shard], [True])

    def with_own(gathered, own):
        mine = (jnp.arange(NCHIP) == chip).reshape((NCHIP,) + (1,) * own.ndim)
        return jnp.where(mine, own[None], gathered)

    w_full_t = with_own(g_win, wt_shard).reshape(NCHIP * WIN_SHARD, D)
    w_main_t = jnp.concatenate([w_full_t[:LR_COL], w_full_t[LR_COL + RANK:]], axis=0)
    w_lr_t = jnp.pad(w_full_t[LR_COL:LR_COL + RANK], ((0, LRP - RANK), (0, 0)))
    b_sb = jnp.broadcast_to(b_spatial[0][:, :, None], (HA, CA, GA))
    xs, tgt = x[0], loss_target[0]

    proj, lr, h, g_a, g_b, g_o, g_gate = _proj_fwd(xs, norm_g, w_main_t, w_lr_t, riders=mats + [gate_sh])
    w_a, w_b, w_o = (with_own(g, own).reshape(D, D) for g, own in zip((g_a, g_b, g_o), mats))
    w_gate = jnp.transpose(with_own(g_gate, gate_sh).reshape(NCHIP, RANK, 128), (1, 0, 2)).reshape(RANK, KEYB)
    w_gate = jnp.pad(w_gate, ((0, LRP - RANK), (0, 0)))
    a = _mixer_a_fwd(proj, ln_v_g, ln_v_b, w_spatial[0], b_sb)
    o, ob, states = _gla_fwd(proj, lr, w_gate, b_gate_up, gla_norm_g)
    dproj, dy, da, dob, dwa, dwb, dwo, dgf, loss_cols = _merge_fwd_bwd(xs, tgt, proj, a, ob, w_a, w_b, w_o,
                                                                       final_norm_g.reshape(1, D))
    dproj, dws, dbs, dlg, dlb = _mixer_a_bwd(proj, da, dproj, ln_v_g, ln_v_b, w_spatial[0], b_sb)
    dproj, dlr, dwg, dbg, dgg = _gla_bwd(proj, lr, o, states, dob, dproj, w_gate, b_gate_up, gla_norm_g)
    b_win = _dw_in(h, dproj, dlr).reshape(NCHIP, WIN_SHARD, D)
    b_mats = [t.reshape(NCHIP, 2, D // NCHIP // 2, D) for t in (dwa, dwb, dwo)]
    big, big_kinds = [b_win] + b_mats, [BY_COLUMNS, CHIP_FIRST, CHIP_FIRST, CHIP_FIRST]
    dx, dg0, *got_big = _dx_bwd(xs, dy, dproj, dlr, norm_g, w_main_t, w_lr_t, riders=big, kinds=big_kinds)
    small = _pack_rows([dg0, dlg, dlb, dws, dbs[:, :, 0], dbg, dgg, dgf, dwg[:RANK], loss_cols], _SMALL_ROWS)
    b_small = small.reshape(2, 1, _SMALL_ROWS // 2, 128)
    got_small = _sibling_halves([b_small], [HALF_FIRST])
    mine, theirs = _reduce_gradients(place, big + [b_small], big_kinds + [HALF_FIRST], got_big + list(got_small))

    g_small = jnp.where(core == 0, jnp.concatenate([mine[4], theirs[4]], axis=0),
                        jnp.concatenate([theirs[4], mine[4]], axis=0))
    grads = {}
    row = 0
    for name, rows in _SMALL:
        grads[name] = g_small[row:row + rows]
        row += rows
    loss = jnp.sum(grads["loss"])
    dwg_full = grads["w_gate_up"].reshape(RANK, KEYB)
    grads["w_gate_up"] = lax.dynamic_slice_in_dim(dwg_full, chip * 128, 128, axis=1)

    weights = dict(norm_g=norm_g, w_in=w_in, ln_v_g=ln_v_g, ln_v_b=ln_v_b, w_spatial=w_spatial, b_spatial=b_spatial,
                   w_gate_up=w_gate_up, b_gate_up=b_gate_up, gla_norm_g=gla_norm_g, w_branch_a=w_branch_a,
                   w_branch_b=w_branch_b, w_out=w_out, final_norm_g=final_norm_g)
    m_in = dict(norm_g=m_norm_g, w_in=m_w_in, ln_v_g=m_ln_v_g, ln_v_b=m_ln_v_b, w_spatial=m_w_spatial,
                b_spatial=m_b_spatial, w_gate_up=m_w_gate_up, b_gate_up=m_b_gate_up, gla_norm_g=m_gla_norm_g,
                w_branch_a=m_w_branch_a, w_branch_b=m_w_branch_b, w_out=m_w_out, final_norm_g=m_final_norm_g)
    v_in = dict(norm_g=v_norm_g, w_in=v_w_in, ln_v_g=v_ln_v_g, ln_v_b=v_ln_v_b, w_spatial=v_w_spatial,
                b_spatial=v_b_spatial, w_gate_up=v_w_gate_up, b_gate_up=v_b_gate_up, gla_norm_g=v_gla_norm_g,
                w_branch_a=v_w_branch_a, w_branch_b=v_w_branch_b, w_out=v_w_out, final_norm_g=v_final_norm_g)
    names = list(weights)
    small_names = [n for n in names if n != "w_in" and n not in mat_names]
    out_g, out_d, out_m, out_v = {}, {}, {}, {}
    res = _adamw_halves(place, jnp.transpose(w_in[0]), mine[0], theirs[0], jnp.transpose(m_w_in[0]),
                        jnp.transpose(v_w_in[0]), axis=1)
    out_g["w_in"], out_d["w_in"], out_m["w_in"], out_v["w_in"] = (jnp.transpose(t)[None] for t in res)
    for i, n in enumerate(mat_names):
        res = _adamw_halves(place, weights[n][0], mine[1 + i], theirs[1 + i], m_in[n][0], v_in[n][0], axis=0)
        out_g[n], out_d[n], out_m[n], out_v[n] = (t[None] for t in res)
    upd_rows = sum(weights[n].size for n in small_names) // 128
    pad_rows = -(-upd_rows // 8) * 8
    packed = [_pack_rows([t[n] for n in small_names], pad_rows) for t in (weights, grads, m_in, v_in)]
    d_s, m_s, v_s = _adamw(*packed)
    row = 0
    for n in small_names:
        shape = weights[n].shape
        rows = weights[n].size // 128
        out_g[n] = grads[n].reshape(shape)
        out_d[n], out_m[n], out_v[n] = (t[row:row + rows].reshape(shape) for t in (d_s, m_s, v_s))
        row += rows
    return (loss, dx[None], *[out_g[n] for n in names], *[out_d[n] for n in names],
            *[out_m[n] for n in names], *[out_v[n] for n in names])
```

```python
import functools
import math

import jax
import jax.numpy as jnp
from jax import lax
from jax.experimental import pallas as pl
from jax.experimental.pallas import tpu as pltpu

f32 = jnp.float32
bf16 = jnp.bfloat16

D = 1024
NMAIN = 8192
LRP = 128
RANK = 16
HA, GA, CA = 8, 128, 128
HB, DK, DV, CB = 4, 128, 256, 64
KEYB = HB * DK
EPS = 1e-6
LN_EPS = 1e-5
GATE_NORM = 16.0
QSCALE = DK ** -0.5
COL_U, COL_V, COL_ZA = 0, 1, 2
COL_Q, COL_K = 6, 7
COL_VB, COL_ZB = 4, 5
COL_GATES = 3
VMEM_LIMIT = 56 * 1024 * 1024
NCHIP = 4
WIN_SHARD = 2052
LR_COL = 6144
_ANY = pl.BlockSpec(memory_space=pl.ANY)

ADAM_LR, ADAM_B1, ADAM_B2, ADAM_EPS, ADAM_WD, ADAM_STEP = 0.001, 0.9, 0.999, 1e-08, 0.01, 10

_SQRT_HALF = 0.7071067811865476
_INV_SQRT_2PI = 0.3989422804014327


def _dot(a, b):
    return jnp.dot(a, b, preferred_element_type=f32)


def _dot_nt(a, b):
    return lax.dot_general(a, b, (((1,), (1,)), ((), ())), preferred_element_type=f32)


def _dot_tn(a, b):
    return lax.dot_general(a, b, (((0,), (0,)), ((), ())), preferred_element_type=f32)


def _dot_exact(a, b):
    return jnp.dot(a, b, preferred_element_type=f32, precision=lax.Precision.HIGHEST)


def _gelu(x):
    return 0.5 * x * (1.0 + lax.erf(x * _SQRT_HALF))


def _gelu_grad(x):
    return 0.5 * (1.0 + lax.erf(x * _SQRT_HALF)) + x * (jnp.exp(-0.5 * x * x) * _INV_SQRT_2PI)


def _sigmoid(x):
    return 1.0 / (1.0 + jnp.exp(-x))


def _params(sem):
    return pltpu.CompilerParams(dimension_semantics=sem, vmem_limit_bytes=VMEM_LIMIT)


def _resident(shape):
    nd = len(shape)
    return pl.BlockSpec(shape, lambda *_: (0,) * nd, pipeline_mode=pl.Buffered(1))


def _proj_fwd(x, g0, w_main, w_lr, riders=(), tm=256, tn=1024):
    T = x.shape[0]
    nsteps = T // tm
    n = len(riders)

    def body(x_ref, g_ref, w_ref, wl_ref, *rest):
        srcs, (proj_ref, lr_ref, h_ref), dsts, sems = rest[:n], rest[n:n + 3], rest[n + 3:2 * n + 3], rest[2 * n + 3:]
        if n:
            @pl.when(pl.program_id(0) == 0)
            def _():
                _gather_start(srcs, dsts, [False] * n, sems)

        xv = x_ref[...]
        r = lax.rsqrt(jnp.mean(xv * xv, axis=-1, keepdims=True) + EPS)
        h = (xv * r * g_ref[...]).astype(bf16)
        h_ref[...] = h
        lr_ref[...] = _dot_nt(h, wl_ref[...])
        for c in range(NMAIN // tn):
            proj_ref[:, c * tn:(c + 1) * tn] = _dot_nt(h, w_ref[c * tn:(c + 1) * tn, :])

        if n:
            @pl.when(pl.program_id(0) == nsteps - 1)
            def _():
                _gather_finish(srcs, dsts, [False] * n, sems)

    return pl.pallas_call(
        body,
        grid=(nsteps,),
        in_specs=[
            pl.BlockSpec((tm, D), lambda i: (i, 0)),
            _resident((1, D)), _resident((NMAIN, D)), _resident((LRP, D)),
        ] + [_ANY] * n,
        out_specs=[
            pl.BlockSpec((tm, NMAIN), lambda i: (i, 0)),
            pl.BlockSpec((tm, LRP), lambda i: (i, 0)),
            pl.BlockSpec((tm, D), lambda i: (i, 0)),
        ] + [_ANY] * n,
        out_shape=[
            jax.ShapeDtypeStruct((T, NMAIN), f32),
            jax.ShapeDtypeStruct((T, LRP), f32),
            jax.ShapeDtypeStruct((T, D), bf16),
        ] + _gathered_shapes(riders),
        scratch_shapes=_gather_sems(n) if n else [],
        compiler_params=_params(("arbitrary",)),
        name="proj_fwd",
    )(x, g0, w_main, w_lr, *riders)


def _causal_mask():
    t = lax.broadcasted_iota(jnp.int32, (CA, CA), 0)
    s = lax.broadcasted_iota(jnp.int32, (CA, CA), 1)
    return s <= t


def _layernorm_parts(gv):
    mu = jnp.mean(gv, axis=-1, keepdims=True)
    xc = gv - mu
    rs = lax.rsqrt(jnp.mean(xc * xc, axis=-1, keepdims=True) + LN_EPS)
    return xc * rs, rs


def _mixer_a_fwd(proj, ln_g, ln_b, w_s, b_sb, tm=256):
    T = proj.shape[0]

    def body(u_ref, v_ref, za_ref, lg_ref, lb_ref, ws_ref, bs_ref, a_ref, vln_s):
        vhat, _ = _layernorm_parts(_gelu(v_ref[...]))
        vln_s[...] = (vhat * lg_ref[...] + lb_ref[...]).astype(bf16)
        mask = _causal_mask()
        for g in range(HA):
            wg = jnp.where(mask, ws_ref[g], 0.0).astype(bf16)
            cols = slice(g * GA, (g + 1) * GA)
            for c in range(tm // CA):
                rows = slice(c * CA, (c + 1) * CA)
                mixed = _dot(wg, vln_s[rows, cols]) + bs_ref[g]
                za = za_ref[rows, cols]
                a = _gelu(u_ref[rows, cols]) * mixed * (za * _sigmoid(za))
                a_ref[rows, cols] = a.astype(bf16)

    def col(cidx):
        return pl.BlockSpec((tm, D), lambda i, c=cidx: (i, c))

    return pl.pallas_call(
        body,
        grid=(T // tm,),
        in_specs=[col(COL_U), col(COL_V), col(COL_ZA), _resident((1, D)), _resident((1, D)),
                  _resident((HA, CA, CA)), _resident((HA, CA, GA))],
        out_specs=pl.BlockSpec((tm, D), lambda i: (i, 0)),
        out_shape=jax.ShapeDtypeStruct((T, D), bf16),
        scratch_shapes=[pltpu.VMEM((tm, D), bf16)],
        compiler_params=_params(("parallel",)),
        name="mixer_a_fwd",
    )(proj, proj, proj, ln_g, ln_b, w_s, b_sb)


def _mixer_a_bwd(proj, da, dproj, ln_g, ln_b, w_s, b_sb, tm=256):
    T = proj.shape[0]
    nsteps = T // tm

    def body(u_ref, v_ref, za_ref, da_ref, dp_in, lg_ref, lb_ref, ws_ref, bs_ref,
             dp_ref, dws_ref, dbs_ref, dlg_ref, dlb_ref, vln_s, dvln_s):
        del dp_in
        i = pl.program_id(0)

        @pl.when(i == 0)
        def _():
            dws_ref[...] = jnp.zeros_like(dws_ref)
            dbs_ref[...] = jnp.zeros_like(dbs_ref)
            dlg_ref[...] = jnp.zeros_like(dlg_ref)
            dlb_ref[...] = jnp.zeros_like(dlb_ref)

        v = v_ref[...]
        vhat, rs = _layernorm_parts(_gelu(v))
        vln_s[...] = (vhat * lg_ref[...] + lb_ref[...]).astype(bf16)
        mask = _causal_mask()
        for g in range(HA):
            wg = jnp.where(mask, ws_ref[g], 0.0).astype(bf16)
            cols = slice(g * GA, (g + 1) * GA)
            dw_acc = jnp.zeros((CA, CA), f32)
            db_acc = jnp.zeros((CA, 1), f32)
            for c in range(tm // CA):
                rows = slice(c * CA, (c + 1) * CA)
                vln = vln_s[rows, cols]
                mixed = _dot(wg, vln) + bs_ref[g]
                u = u_ref[rows, cols]
                za = za_ref[rows, cols]
                da_blk = da_ref[rows, cols]
                sg = _sigmoid(za)
                sz = za * sg
                gu = _gelu(u)
                dp_ref[rows, cols] = (da_blk * mixed * sz * _gelu_grad(u)).astype(bf16)
                dp_ref[rows, 2 * D + g * GA:2 * D + (g + 1) * GA] = (
                    da_blk * gu * mixed * (sg * (1.0 + za * (1.0 - sg)))).astype(bf16)
                dmixed = da_blk * gu * sz
                dmb = dmixed.astype(bf16)
                dvln_s[rows, cols] = _dot_tn(wg, dmb)
                dw_acc = dw_acc + _dot_nt(dmb, vln)
                db_acc = db_acc + jnp.sum(dmixed, axis=-1, keepdims=True)
            dws_ref[g] += dw_acc
            dbs_ref[g] += jnp.broadcast_to(db_acc, (CA, GA))

        dvln = dvln_s[...]
        dlg_ref[...] += jnp.sum(dvln * vhat, axis=0, keepdims=True)
        dlb_ref[...] += jnp.sum(dvln, axis=0, keepdims=True)
        dvhat = dvln * lg_ref[...]
        dgv = rs * (dvhat - jnp.mean(dvhat, axis=-1, keepdims=True)
                    - vhat * jnp.mean(dvhat * vhat, axis=-1, keepdims=True))
        dp_ref[:, D:2 * D] = (dgv * _gelu_grad(v)).astype(bf16)

        @pl.when(i == nsteps - 1)
        def _():
            for g in range(HA):
                dws_ref[g] = jnp.where(mask, dws_ref[g], 0.0)

    def col(cidx):
        return pl.BlockSpec((tm, D), lambda i, c=cidx: (i, c))

    return pl.pallas_call(
        body,
        grid=(nsteps,),
        in_specs=[col(COL_U), col(COL_V), col(COL_ZA), pl.BlockSpec((tm, D), lambda i: (i, 0)),
                  pl.BlockSpec(memory_space=pl.ANY),
                  _resident((1, D)), _resident((1, D)), _resident((HA, CA, CA)), _resident((HA, CA, GA))],
        out_specs=[pl.BlockSpec((tm, 3 * D), lambda i: (i, 0)),
                   _resident((HA, CA, CA)), _resident((HA, CA, GA)), _resident((1, D)), _resident((1, D))],
        out_shape=[jax.ShapeDtypeStruct(dproj.shape, dproj.dtype),
                   jax.ShapeDtypeStruct((HA, CA, CA), f32), jax.ShapeDtypeStruct((HA, CA, GA), f32),
                   jax.ShapeDtypeStruct((1, D), f32), jax.ShapeDtypeStruct((1, D), f32)],
        scratch_shapes=[pltpu.VMEM((tm, D), bf16), pltpu.VMEM((tm, D), f32)],
        input_output_aliases={4: 0},
        compiler_params=_params(("arbitrary",)),
        name="mixer_a_bwd",
    )(proj, proj, proj, da, dproj, ln_g, ln_b, w_s, b_sb)


def _tri(n, upper):
    r = lax.broadcasted_iota(jnp.int32, (n, n), 0)
    c = lax.broadcasted_iota(jnp.int32, (n, n), 1)
    return jnp.where((c >= r) if upper else (c <= r), 1.0, 0.0).astype(f32)


def _chunk_tri(n, upper):
    r = lax.broadcasted_iota(jnp.int32, (n, n), 0)
    c = lax.broadcasted_iota(jnp.int32, (n, n), 1)
    shift = CB.bit_length() - 1
    same_chunk = jnp.right_shift(r, shift) == jnp.right_shift(c, shift)
    return jnp.where(same_chunk & ((c >= r) if upper else (c <= r)), 1.0, 0.0).astype(f32)


def _log_alpha(lr, wg, bg):
    logit = _dot(lr.astype(bf16), wg.astype(bf16)) + bg
    la = (jnp.minimum(logit, 0.0) - jnp.log1p(jnp.exp(-jnp.abs(logit)))) * (1.0 / GATE_NORM)
    return logit, la


def _gla_fwd(proj, lr, w_gate, b_gate, gla_g, tm=256):
    T = proj.shape[0]
    nchunk = T // CB
    cpb = tm // CB

    def body(q_ref, k_ref, v_ref, zb_ref, lr_ref, wg_ref, bg_ref, gg_ref,
             o_ref, ob_ref, st_ref, state, la_s):
        @pl.when(pl.program_id(0) == 0)
        def _():
            state[...] = jnp.zeros_like(state)

        _, la = _log_alpha(lr_ref[...], wg_ref[...], bg_ref[...])
        la_s[...] = _dot_exact(_chunk_tri(tm, upper=False), la)
        causal = _tri(CB, upper=False) > 0.5
        states = [state[hd] for hd in range(HB)]
        for c in range(cpb):
            rows = slice(c * CB, (c + 1) * CB)
            b = la_s[rows, :]
            bl = b[CB - 1:CB, :]
            bm = b[CB // 2 - 1:CB // 2, :]
            q = q_ref[rows, :] * QSCALE
            k = k_ref[rows, :]
            qi_all = (q * jnp.exp(b - bm)).astype(bf16)
            ki_all = (k * jnp.exp(bm - b)).astype(bf16)
            qe_all = (q * jnp.exp(b)).astype(bf16)
            ks_all = (k * jnp.exp(bl - b)).astype(bf16)
            e_l = jnp.exp(bl)
            for hd in range(HB):
                kc = slice(hd * DK, (hd + 1) * DK)
                vc = slice(hd * DV, (hd + 1) * DV)
                v = v_ref[rows, vc].astype(bf16)
                p = jnp.where(causal, _dot_nt(qi_all[:, kc], ki_all[:, kc]), 0.0).astype(bf16)
                s0 = states[hd]
                st_ref[c, hd] = s0
                o = _dot(p, v) + _dot_nt(qe_all[:, kc], s0.astype(bf16))
                states[hd] = s0 * e_l[:, kc] + _dot_tn(v, ks_all[:, kc])
                o_ref[rows, vc] = o
                ro = lax.rsqrt(jnp.mean(o * o, axis=-1, keepdims=True) + EPS)
                zb = zb_ref[rows, vc]
                ob_ref[rows, vc] = (o * ro * gg_ref[...] * (zb * _sigmoid(zb))).astype(bf16)
        for hd in range(HB):
            state[hd] = states[hd]

    return pl.pallas_call(
        body,
        grid=(T // tm,),
        in_specs=[pl.BlockSpec((tm, KEYB), lambda i: (i, COL_Q)),
                  pl.BlockSpec((tm, KEYB), lambda i: (i, COL_K)),
                  pl.BlockSpec((tm, D), lambda i: (i, COL_VB)),
                  pl.BlockSpec((tm, D), lambda i: (i, COL_ZB)),
                  pl.BlockSpec((tm, LRP), lambda i: (i, 0)),
                  _resident((LRP, KEYB)), _resident((1, KEYB)), _resident((1, DV))],
        out_specs=[pl.BlockSpec((tm, D), lambda i: (i, 0)),
                   pl.BlockSpec((tm, D), lambda i: (i, 0)),
                   pl.BlockSpec((cpb, HB, DV, DK), lambda i: (i, 0, 0, 0))],
        out_shape=[jax.ShapeDtypeStruct((T, D), f32), jax.ShapeDtypeStruct((T, D), bf16),
                   jax.ShapeDtypeStruct((nchunk, HB, DV, DK), f32)],
        scratch_shapes=[pltpu.VMEM((HB, DV, DK), f32), pltpu.VMEM((tm, KEYB), f32)],
        compiler_params=_params(("arbitrary",)),
        name="gla_fwd",
    )(proj, proj, proj, proj, lr, w_gate, b_gate, gla_g)


def _gla_bwd(proj, lr, o, states, dob, dproj, w_gate, b_gate, gla_g, tm=256):
    T = proj.shape[0]
    cpb = tm // CB
    nb = T // tm

    def body(q_ref, k_ref, v_ref, zb_ref, lr_ref, o_ref, st_ref, dob_ref, dp_in, wg_ref, bg_ref, gg_ref,
             dp_ref, dlr_ref, dwg_ref, dbg_ref, dgg_ref, dstate, la_s, dlogit_s, tail_s):
        del dp_in
        step = pl.program_id(0)

        @pl.when(step == 0)
        def _():
            dstate[...] = jnp.zeros_like(dstate)
            dwg_ref[...] = jnp.zeros_like(dwg_ref)
            dbg_ref[...] = jnp.zeros_like(dbg_ref)
            dgg_ref[...] = jnp.zeros_like(dgg_ref)

        lr_v = lr_ref[...]
        logit, la = _log_alpha(lr_v, wg_ref[...], bg_ref[...])
        la_s[...] = _dot_exact(_chunk_tri(tm, upper=False), la)
        causal = _tri(CB, upper=False) > 0.5
        gg = gg_ref[...]
        dgg_acc = jnp.zeros((1, DV), f32)
        dstates = [dstate[hd] for hd in range(HB)]
        for c in reversed(range(cpb)):
            rows = slice(c * CB, (c + 1) * CB)
            b = la_s[rows, :]
            bl = b[CB - 1:CB, :]
            bm = b[CB // 2 - 1:CB // 2, :]
            eb_all, eqm_all, ekm_all = jnp.exp(b), jnp.exp(b - bm), jnp.exp(bm - b)
            eks_all, el_all = jnp.exp(bl - b), jnp.exp(bl)
            q_all = q_ref[rows, :] * QSCALE
            k_all = k_ref[rows, :]
            qi_all = (q_all * eqm_all).astype(bf16)
            ki_all = (k_all * ekm_all).astype(bf16)
            qe_all = (q_all * eb_all).astype(bf16)
            ksf_all = k_all * eks_all
            ks_all = ksf_all.astype(bf16)
            for hd in range(HB):
                kc = slice(hd * DK, (hd + 1) * DK)
                vc = slice(hd * DV, (hd + 1) * DV)
                o_h = o_ref[rows, vc]
                ro = lax.rsqrt(jnp.mean(o_h * o_h, axis=-1, keepdims=True) + EPS)
                ohat = o_h * ro
                zb = zb_ref[rows, vc]
                sg = _sigmoid(zb)
                dob_h = dob_ref[rows, vc]
                don = dob_h * (zb * sg)
                dp_ref[rows, 2 * D + hd * DV:2 * D + (hd + 1) * DV] = (
                    dob_h * ohat * gg * (sg * (1.0 + zb * (1.0 - sg)))).astype(bf16)
                dgg_acc = dgg_acc + jnp.sum(don * ohat, axis=0, keepdims=True)
                dohat = don * gg
                do = (ro * (dohat - ohat * jnp.mean(dohat * ohat, axis=-1, keepdims=True))).astype(bf16)
                e_b, e_qm, e_km, e_ks, e_l = eb_all[:, kc], eqm_all[:, kc], ekm_all[:, kc], eks_all[:, kc], el_all[:, kc]
                q, k, ks_f = q_all[:, kc], k_all[:, kc], ksf_all[:, kc]
                qi, ki, qe, ks = qi_all[:, kc], ki_all[:, kc], qe_all[:, kc], ks_all[:, kc]
                v = v_ref[rows, vc].astype(bf16)
                p = jnp.where(causal, _dot_nt(qi, ki), 0.0).astype(bf16)
                s0 = st_ref[c, hd]
                ds = dstates[hd]
                ds_b = ds.astype(bf16)
                dv = _dot_tn(p, do) + _dot_nt(ks, ds_b)
                dpm = jnp.where(causal, _dot_nt(do, v), 0.0).astype(bf16)
                dqi = _dot(dpm, ki)
                dki = _dot_tn(dpm, qi)
                dqe = _dot(do, s0.astype(bf16))
                dks = _dot(v, ds_b)
                dq_s = dqi * e_qm + dqe * e_b
                dk = dki * e_km + dks * e_ks
                tail = (jnp.sum(dks * ks_f, axis=0, keepdims=True)
                        + e_l * jnp.sum(ds * s0, axis=0, keepdims=True))
                dstates[hd] = _dot_tn(do, qe) + ds * e_l
                dp_ref[rows, kc] = (dq_s * QSCALE).astype(bf16)
                dp_ref[rows, KEYB + hd * DK:KEYB + (hd + 1) * DK] = dk.astype(bf16)
                dp_ref[rows, D + hd * DV:D + (hd + 1) * DV] = dv.astype(bf16)
                dlogit_s[rows, kc] = dq_s * q - dk * k
                tail_s[rows, kc] = jnp.broadcast_to(tail, (CB, DK))
        for hd in range(HB):
            dstate[hd] = dstates[hd]
        dgg_ref[...] += dgg_acc
        dg = _dot_exact(_chunk_tri(tm, upper=True), dlogit_s[...]) + tail_s[...]
        dlogit = dg * (1.0 / GATE_NORM) * _sigmoid(-logit)
        dbg_ref[...] += jnp.sum(dlogit, axis=0, keepdims=True)
        dlb = dlogit.astype(bf16)
        dlr_ref[...] = _dot_nt(dlb, wg_ref[...].astype(bf16)).astype(bf16)
        dwg_ref[...] += _dot_tn(lr_v.astype(bf16), dlb)

    def rev(cidx):
        return lambda i, c=cidx: (nb - 1 - i, c)

    return pl.pallas_call(
        body,
        grid=(nb,),
        in_specs=[pl.BlockSpec((tm, KEYB), rev(COL_Q)),
                  pl.BlockSpec((tm, KEYB), rev(COL_K)),
                  pl.BlockSpec((tm, D), rev(COL_VB)),
                  pl.BlockSpec((tm, D), rev(COL_ZB)),
                  pl.BlockSpec((tm, LRP), rev(0)),
                  pl.BlockSpec((tm, D), rev(0)),
                  pl.BlockSpec((cpb, HB, DV, DK), lambda i: (nb - 1 - i, 0, 0, 0)),
                  pl.BlockSpec((tm, D), rev(0)),
                  pl.BlockSpec(memory_space=pl.ANY),
                  _resident((LRP, KEYB)), _resident((1, KEYB)), _resident((1, DV))],
        out_specs=[pl.BlockSpec((tm, 3 * D), rev(1)),
                   pl.BlockSpec((tm, LRP), rev(0)),
                   _resident((LRP, KEYB)), _resident((1, KEYB)), _resident((1, DV))],
        out_shape=[jax.ShapeDtypeStruct(dproj.shape, dproj.dtype),
                   jax.ShapeDtypeStruct((T, LRP), bf16),
                   jax.ShapeDtypeStruct((LRP, KEYB), f32), jax.ShapeDtypeStruct((1, KEYB), f32),
                   jax.ShapeDtypeStruct((1, DV), f32)],
        scratch_shapes=[pltpu.VMEM((HB, DV, DK), f32)] + [pltpu.VMEM((tm, KEYB), f32)] * 3,
        input_output_aliases={8: 0},
        compiler_params=_params(("arbitrary",)),
        name="gla_bwd",
    )(proj, proj, proj, proj, lr, o, states, dob, dproj, w_gate, b_gate, gla_g)


def _merge_fwd_bwd(x, tgt, proj, a, ob, w_a, w_b, w_o, g_f, tm=256):
    T = x.shape[0]

    def body(x_ref, t_ref, gt_ref, a_ref, ob_ref, wa_ref, wb_ref, wo_ref, gf_ref,
             dp_ref, dy_ref, da_ref, dob_ref, dwa_ref, dwb_ref, dwo_ref, dgf_ref, loss_ref):
        @pl.when(pl.program_id(0) == 0)
        def _():
            dwa_ref[...] = jnp.zeros_like(dwa_ref)
            dwb_ref[...] = jnp.zeros_like(dwb_ref)
            dwo_ref[...] = jnp.zeros_like(dwo_ref)
            dgf_ref[...] = jnp.zeros_like(dgf_ref)
            loss_ref[...] = jnp.zeros_like(loss_ref)

        ga = _sigmoid(gt_ref[:, :D])
        gb = _sigmoid(gt_ref[:, D:])
        a_v = a_ref[...]
        ob_v = ob_ref[...]
        pa = _dot(a_v, wa_ref[...])
        pb = _dot(ob_v, wb_ref[...])
        mb = (ga * pa + gb * pb).astype(bf16)
        y = x_ref[...] + _dot(mb, wo_ref[...])
        r1 = lax.rsqrt(jnp.mean(y * y, axis=-1, keepdims=True) + EPS)
        yhat = y * r1
        gf = gf_ref[...]
        err = yhat * gf - t_ref[...]
        loss_ref[...] += jnp.sum(err * err, axis=0, keepdims=True) * (0.5 / D)
        dout = err * (1.0 / D)
        dgf_ref[...] += jnp.sum(dout * yhat, axis=0, keepdims=True)
        dyn = dout * gf
        dy = r1 * (dyn - yhat * jnp.mean(dyn * yhat, axis=-1, keepdims=True))
        dy_ref[...] = dy
        dyb = dy.astype(bf16)
        dwo_ref[...] += _dot_tn(mb, dyb)
        dm = _dot_nt(dyb, wo_ref[...])
        dpa = (dm * ga).astype(bf16)
        dpb = (dm * gb).astype(bf16)
        dp_ref[:, :D] = (dm * pa * ga * (1.0 - ga)).astype(bf16)
        dp_ref[:, D:] = (dm * pb * gb * (1.0 - gb)).astype(bf16)
        dwa_ref[...] += _dot_tn(a_v, dpa)
        dwb_ref[...] += _dot_tn(ob_v, dpb)
        da_ref[...] = _dot_nt(dpa, wa_ref[...])
        dob_ref[...] = _dot_nt(dpb, wb_ref[...])

    row = lambda: pl.BlockSpec((tm, D), lambda i: (i, 0))
    return pl.pallas_call(
        body,
        grid=(T // tm,),
        in_specs=[row(), row(), pl.BlockSpec((tm, 2 * D), lambda i: (i, COL_GATES)), row(), row(),
                  _resident((D, D)), _resident((D, D)), _resident((D, D)), _resident((1, D))],
        out_specs=[pl.BlockSpec((tm, 2 * D), lambda i: (i, COL_GATES)), row(), row(), row(),
                   _resident((D, D)), _resident((D, D)), _resident((D, D)), _resident((1, D)), _resident((1, D))],
        out_shape=[jax.ShapeDtypeStruct((T, NMAIN), bf16),
                   jax.ShapeDtypeStruct((T, D), f32), jax.ShapeDtypeStruct((T, D), f32),
                   jax.ShapeDtypeStruct((T, D), f32),
                   jax.ShapeDtypeStruct((D, D), f32), jax.ShapeDtypeStruct((D, D), f32),
                   jax.ShapeDtypeStruct((D, D), f32),
                   jax.ShapeDtypeStruct((1, D), f32), jax.ShapeDtypeStruct((1, D), f32)],
        compiler_params=_params(("arbitrary",)),
        name="merge_fwd_bwd",
    )(x, tgt, proj, a, ob, w_a, w_b, w_o, g_f)


def _dx_bwd(x, dy, dproj, dlr, g0, w_main, w_lr, riders=(), sends_what=(), tm=256):
    T = x.shape[0]
    nsteps = T // tm
    n = len(riders)

    def body(x_ref, dy_ref, dp_ref, dl_ref, g_ref, w_ref, wl_ref, *rest):
        srcs, (dx_ref, dg_ref), dsts, sems = rest[:n], rest[n:n + 2], rest[n + 2:2 * n + 2], rest[2 * n + 2:]

        @pl.when(pl.program_id(0) == 0)
        def _():
            dg_ref[...] = jnp.zeros_like(dg_ref)
            for cp in _exchange_copies(srcs, dsts, sends_what, sems) if n else []:
                cp.start()

        xv = x_ref[...]
        r = lax.rsqrt(jnp.mean(xv * xv, axis=-1, keepdims=True) + EPS)
        xhat = xv * r
        dh = _dot(dp_ref[...], w_ref[...]) + _dot(dl_ref[...], wl_ref[...])
        dg_ref[...] += jnp.sum(dh * xhat, axis=0, keepdims=True)
        t = dh * g_ref[...]
        dx_ref[...] = dy_ref[...] + r * (t - xhat * jnp.mean(t * xhat, axis=-1, keepdims=True))

        if n:
            @pl.when(pl.program_id(0) == nsteps - 1)
            def _():
                for cp in _exchange_copies(srcs, dsts, sends_what, sems):
                    cp.wait()

    row = lambda: pl.BlockSpec((tm, D), lambda i: (i, 0))
    return pl.pallas_call(
        body,
        grid=(nsteps,),
        in_specs=[row(), row(), pl.BlockSpec((tm, NMAIN), lambda i: (i, 0)),
                  pl.BlockSpec((tm, LRP), lambda i: (i, 0)),
                  _resident((1, D)), _resident((NMAIN, D)), _resident((LRP, D))] + [_ANY] * n,
        out_specs=[row(), _resident((1, D))] + [_ANY] * n,
        out_shape=[jax.ShapeDtypeStruct((T, D), f32), jax.ShapeDtypeStruct((1, D), f32)]
        + _exchange_shapes(riders, sends_what),
        scratch_shapes=_exchange_sems(n) if n else [],
        compiler_params=_params(("arbitrary",)),
        name="dx_bwd",
    )(x, dy, dproj, dlr, g0, w_main, w_lr, *riders)


def _dw_in(h, dproj, dlr, swap=False, riders=(), kinds=(), tm=1024, tn=1024):
    T = h.shape[0]
    tm = min(tm, T)
    nj, nk = NMAIN // tn, T // tm
    lr_tile = LR_COL // tn
    n = len(riders)
    hd = D // 2

    def body(h_ref, dp_ref, dl_ref, *rest):
        srcs, rest = rest[:n], rest[n:]
        out_ref, rest = rest[0], rest[1:]
        if swap:
            got_ref, dsts, rest = rest[0], rest[1:1 + n], rest[1 + n:]
            acc, lr_acc, sems, lr_sem, swap_send, swap_recv = rest[:6]
            rider_sems = rest[6:]
        else:
            acc, lr_acc, sems, lr_sem = rest
        j, k = pl.program_id(0), pl.program_id(1)
        slot = j % 2

        def tile_row(jj):
            return pl.multiple_of(jj * tn + jnp.where(jj >= lr_tile, RANK, 0), 8)

        def tile_out(jj, s):
            return pltpu.make_async_copy(acc.at[s], out_ref.at[pl.ds(tile_row(jj), tn)], sems.at[s])

        def other_half(ref):
            c = lax.axis_index("c")
            return ref.at[:, pl.ds(pl.multiple_of((1 - c) * hd, 128), hd)]

        def tile_swap(jj, s):
            return _remote(other_half(acc.at[s]), got_ref.at[pl.ds(tile_row(jj), tn)], swap_send.at[jj], swap_recv.at[jj],
                           _sibling())

        def lr_swap():
            return _remote(other_half(lr_acc.at[pl.ds(0, RANK)]), got_ref.at[pl.ds(LR_COL, RANK)], swap_send.at[nj],
                           swap_recv.at[nj], _sibling())

        lr_rows = pltpu.make_async_copy(lr_acc.at[pl.ds(0, RANK)], out_ref.at[pl.ds(LR_COL, RANK)], lr_sem)

        @pl.when(j == 0)
        def _():
            @pl.when(k == 0)
            def _():
                lr_acc[...] = jnp.zeros_like(lr_acc)
                for cp in _halves_copies(srcs, dsts, kinds, rider_sems) if n else []:
                    cp.start()

            lr_acc[...] += _dot_tn(dl_ref[...], h_ref[...])

            @pl.when(k == nk - 1)
            def _():
                lr_rows.start()
                if swap:
                    lr_swap().start()

        @pl.when(k == 0)
        def _():
            acc[slot] = jnp.zeros((tn, D), f32)

        acc[slot] += _dot_tn(dp_ref[...], h_ref[...])

        @pl.when(k == nk - 1)
        def _():
            tile_out(j, slot).start()
            if swap:
                tile_swap(j, slot).start()

            @pl.when(j > 0)
            def _():
                tile_out(j - 1, 1 - slot).wait()
                if swap:
                    tile_swap(j - 1, 1 - slot).wait_send()

            @pl.when(j == nj - 1)
            def _():
                tile_out(j, slot).wait()
                lr_rows.wait()
                if swap:
                    tile_swap(j, slot).wait_send()
                    lr_swap().wait()
                    for jj in range(nj):
                        tile_swap(jj, 0).wait_recv()
                    for cp in _halves_copies(srcs, dsts, kinds, rider_sems) if n else []:
                        cp.wait()

    swap_outs = [jax.ShapeDtypeStruct((NMAIN + RANK, hd), f32)] + _halves_shapes(riders, kinds) if swap else []
    swap_sems = [pltpu.SemaphoreType.DMA((nj + 1,))] * 2 + (_halves_sems(riders) if n else []) if swap else []
    return pl.pallas_call(
        body,
        grid=(nj, nk),
        in_specs=[pl.BlockSpec((tm, D), lambda j, k: (k, 0)), pl.BlockSpec((tm, tn), lambda j, k: (k, j)),
                  pl.BlockSpec((tm, LRP), lambda j, k: (k, 0))] + [_ANY] * n,
        out_specs=[_ANY] * (1 + len(swap_outs)),
        out_shape=[jax.ShapeDtypeStruct((NMAIN + RANK, D), f32)] + swap_outs,
        scratch_shapes=[pltpu.VMEM((2, tn, D), f32), pltpu.VMEM((LRP, D), f32),
                        pltpu.SemaphoreType.DMA((2,)), pltpu.SemaphoreType.DMA] + swap_sems,
        compiler_params=_params(("arbitrary", "arbitrary")),
        name="dw_in",
    )(h, dproj, dlr, *riders)


MESH = pl.DeviceIdType.MESH


def _place():
    x, y, c = lax.axis_index("x"), lax.axis_index("y"), lax.axis_index("c")
    others = [(1 - x, y), (x, 1 - y), (1 - x, 1 - y)]
    return x, y, c, 2 * x + y, others


def _sibling():
    return lax.axis_index("x"), lax.axis_index("y"), 1 - lax.axis_index("c")


def _remote(src, dst, send_sem, recv_sem, to):
    return pltpu.make_async_remote_copy(src_ref=src, dst_ref=dst, send_sem=send_sem, recv_sem=recv_sem,
                                        device_id=to, device_id_type=MESH)


def _half(ref, e, by_columns):
    if not by_columns:
        return ref.at[e]
    hw = ref.shape[-1] // 2
    return ref.at[:, pl.ds(pl.multiple_of(e * hw, 128), hw)]


def _gather_weights(shards, by_columns):
    n = len(shards)

    def body(*refs):
        _gather_start(refs[:n], refs[n:2 * n], by_columns, refs[2 * n:])
        _gather_finish(refs[:n], refs[n:2 * n], by_columns, refs[2 * n:])

    return pl.pallas_call(
        body,
        in_specs=[_ANY] * n,
        out_specs=[_ANY] * n,
        out_shape=_gathered_shapes(shards),
        scratch_shapes=_gather_sems(n),
        name="gather_weights",
    )(*shards)


def _gathered_shapes(shards):
    return [jax.ShapeDtypeStruct((NCHIP,) + s.shape, s.dtype) for s in shards]


def _gather_sems(n):
    return [pltpu.SemaphoreType.DMA((3, n))] * 4


def _gather_copies(srcs, dsts, by_columns, sems, sends_only):
    n = len(srcs)
    send_sems, recv_sems, pass_send, pass_recv = sems
    x, y, c, me, others = _place()
    sibling = (x, y, 1 - c)

    def src(a, e):
        return _half(srcs[a], e, by_columns[a])

    def dst(a, j, e):
        return _half(dsts[a].at[j], e, by_columns[a])

    sends, arrivals, passes, passed = [], [], [], []
    for k, (cx, cy) in enumerate(others):
        j = 2 * cx + cy
        for a in range(n):
            sends.append(_remote(src(a, c), dst(a, me, c), send_sems.at[k, a], recv_sems.at[k, a], (cx, cy, c)))
            if not sends_only:
                arrivals.append(_remote(src(a, c), dst(a, j, c), send_sems.at[k, a], recv_sems.at[k, a], (cx, cy, c)))
                passes.append(_remote(dst(a, j, c), dst(a, j, c), pass_send.at[k, a], pass_recv.at[k, a], sibling))
                passed.append(_remote(src(a, c), dst(a, j, 1 - c), pass_send.at[k, a], pass_recv.at[k, a], sibling))
    return sends, arrivals, passes, passed


def _gather_start(srcs, dsts, by_columns, sems):
    for cp in _gather_copies(srcs, dsts, by_columns, sems, sends_only=True)[0]:
        cp.start()


def _gather_finish(srcs, dsts, by_columns, sems):
    sends, arrivals, passes, passed = _gather_copies(srcs, dsts, by_columns, sems, sends_only=False)
    for arrival, cp in zip(arrivals, passes):
        arrival.wait_recv()
        cp.start()
    for arrival in passed:
        arrival.wait_recv()
    for cp in sends + passes:
        cp.wait_send()


HALF_FIRST, CHIP_FIRST, BY_COLUMNS = "half_first", "chip_first", "by_columns"


def _sibling_halves(bufs, kinds):
    n = len(bufs)

    def body(*refs):
        cps = _halves_copies(refs[:n], refs[n:2 * n], kinds, refs[2 * n:])
        for cp in cps:
            cp.start()
        for cp in cps:
            cp.wait()

    return pl.pallas_call(
        body,
        in_specs=[_ANY] * n,
        out_specs=[_ANY] * n,
        out_shape=_halves_shapes(bufs, kinds),
        scratch_shapes=_halves_sems(bufs),
        name="sibling_halves",
    )(*bufs)


def _halves_shapes(bufs, kinds):
    def landed(b, kind):
        if kind == HALF_FIRST:
            return b.shape[1:]
        if kind == CHIP_FIRST:
            return (b.shape[0],) + b.shape[2:]
        return b.shape[:2] + (b.shape[2] // 2,)

    return [jax.ShapeDtypeStruct(landed(b, kind), b.dtype) for b, kind in zip(bufs, kinds)]


def _halves_sems(bufs):
    return [pltpu.SemaphoreType.DMA((len(bufs), NCHIP))] * 2


def _halves_copies(srcs, dsts, kinds, sems):
    send_sems, recv_sems = sems
    x, y, c, _, _ = _place()
    cps = []
    for a, kind in enumerate(kinds):
        if kind == HALF_FIRST:
            cps.append(_remote(srcs[a].at[1 - c], dsts[a], send_sems.at[a, 0], recv_sems.at[a, 0], (x, y, 1 - c)))
        else:
            cps += [_remote(_half(srcs[a].at[j], 1 - c, kind == BY_COLUMNS), dsts[a].at[j],
                            send_sems.at[a, j], recv_sems.at[a, j], (x, y, 1 - c)) for j in range(srcs[a].shape[0])]
    return cps


TO_ITS_CHIP, TO_EVERY_CHIP, ROWS_TO_ITS_CHIP = "to_its_chip", "to_every_chip", "rows_to_its_chip"
PIECE_STEP = 2048
PIECE_ROWS = 2064


def _chip_exchange(parts, sends_what):
    n = len(parts)

    def body(*refs):
        cps = _exchange_copies(refs[:n], refs[n:2 * n], sends_what, refs[2 * n:])
        for cp in cps:
            cp.start()
        for cp in cps:
            cp.wait()

    return pl.pallas_call(
        body,
        in_specs=[_ANY] * n,
        out_specs=[_ANY] * n,
        out_shape=_exchange_shapes(parts, sends_what),
        scratch_shapes=_exchange_sems(n),
        name="chip_exchange",
    )(*parts)


def _exchange_shapes(parts, sends_what):
    def landed(p, what):
        return (3, PIECE_ROWS, p.shape[1]) if what == ROWS_TO_ITS_CHIP else (3,) + p.shape[1:]

    return [jax.ShapeDtypeStruct(landed(p, what), p.dtype) for p, what in zip(parts, sends_what)]


def _exchange_sems(n):
    return [pltpu.SemaphoreType.DMA((3, n))] * 2


def _exchange_copies(srcs, dsts, sends_what, sems):
    send_sems, recv_sems = sems
    x, y, c, me, others = _place()

    def part(a, j):
        if sends_what[a] == ROWS_TO_ITS_CHIP:
            return srcs[a].at[pl.ds(pl.multiple_of(j * PIECE_STEP, PIECE_STEP), PIECE_ROWS)]
        return srcs[a].at[j if sends_what[a] == TO_ITS_CHIP else 0]

    return [_remote(part(a, 2 * cx + cy), dsts[a].at[k], send_sems.at[k, a], recv_sems.at[k, a], (cx, cy, c))
            for k, (cx, cy) in enumerate(others) for a in range(len(srcs))]


def _sibling_swap(halves):
    n = len(halves)

    def body(*refs):
        srcs, dsts = refs[:n], refs[n:2 * n]
        send_sems, recv_sems = refs[2 * n:]
        x, y, c, _, _ = _place()
        cps = [_remote(srcs[a], dsts[a], send_sems.at[a], recv_sems.at[a], (x, y, 1 - c)) for a in range(n)]
        for cp in cps:
            cp.start()
        for cp in cps:
            cp.wait()

    return pl.pallas_call(
        body,
        in_specs=[_ANY] * n,
        out_specs=[_ANY] * n,
        out_shape=[jax.ShapeDtypeStruct(s.shape, s.dtype) for s in halves],
        scratch_shapes=[pltpu.SemaphoreType.DMA((n,))] * 2,
        name="sibling_swap",
    )(*halves)


def _tile(h, w, operands=5):
    if h % 128 == 0:
        return 128, w
    budget = VMEM_LIMIT * 3 // 4 // (2 * operands * 4)
    tw = w
    while h * tw > budget and tw % 256 == 0:
        tw //= 2
    return h, tw


def _pair_sum(place, buf, kind, got, out_dtype):
    nj, h, w = got.shape
    th, tw = _tile(h, w)
    nq = w // tw

    def body(p_ref, a_ref, b_ref, o_ref):
        del p_ref
        o_ref[...] = (a_ref[...] + b_ref[...]).astype(out_dtype)

    if kind == HALF_FIRST:
        mine = pl.BlockSpec((None, None, th, tw), lambda j, r, q, p: (p[0], j, r, q))
    elif kind == CHIP_FIRST:
        mine = pl.BlockSpec((None, None, th, tw), lambda j, r, q, p: (j, p[0], r, q))
    else:
        mine = pl.BlockSpec((None, th, tw), lambda j, r, q, p: (j, r, p[0] * nq + q))
    return pl.pallas_call(
        body,
        grid_spec=pltpu.PrefetchScalarGridSpec(
            num_scalar_prefetch=1,
            grid=(nj, h // th, w // tw),
            in_specs=[mine, pl.BlockSpec((None, th, tw), lambda j, r, q, p: (j, r, q))],
            out_specs=pl.BlockSpec((None, th, tw), lambda j, r, q, p: (j, r, q)),
        ),
        out_shape=jax.ShapeDtypeStruct((nj, h, w), out_dtype),
        compiler_params=_params(("parallel", "parallel", "parallel")),
        name="pair_sum",
    )(place, buf, got)


def _chip_sum(place, part, slots):
    nj, h, w = part.shape
    th, tw = _tile(h, w)

    def body(p_ref, own_ref, s_ref, o_ref):
        me = p_ref[1]
        own = own_ref[...].astype(f32)
        by_flip = {2: s_ref[0].astype(f32), 1: s_ref[1].astype(f32), 3: s_ref[2].astype(f32)}
        acc = None
        for j in range(NCHIP):
            flip = me ^ j
            term = jnp.where(flip == 0, own, jnp.where(flip == 2, by_flip[2], jnp.where(flip == 1, by_flip[1], by_flip[3])))
            acc = term if acc is None else acc + term
        o_ref[...] = acc

    return pl.pallas_call(
        body,
        grid_spec=pltpu.PrefetchScalarGridSpec(
            num_scalar_prefetch=1,
            grid=(h // th, w // tw),
            in_specs=[pl.BlockSpec((None, th, tw), lambda r, q, p: (p[1] if nj == NCHIP else 0, r, q)),
                      pl.BlockSpec((3, th, tw), lambda r, q, p: (0, r, q))],
            out_specs=pl.BlockSpec((th, tw), lambda r, q, p: (r, q)),
        ),
        out_shape=jax.ShapeDtypeStruct((h, w), f32),
        compiler_params=_params(("parallel", "parallel")),
        name="chip_sum",
    )(place, part, slots)


def _adamw_math(w, g, m, v):
    nm = ADAM_B1 * m + (1.0 - ADAM_B1) * g
    nv = ADAM_B2 * v + (1.0 - ADAM_B2) * (g * g)
    m_hat = nm / (1.0 - ADAM_B1 ** ADAM_STEP)
    v_hat = nv / (1.0 - ADAM_B2 ** ADAM_STEP)
    return -ADAM_LR * (m_hat / (jnp.sqrt(v_hat) + ADAM_EPS) + ADAM_WD * w), nm, nv


def _adamw(w, g, m, v):
    rows, width = w.shape
    th, tw = _tile(rows, width, operands=7)

    def body(w_ref, g_ref, m_ref, v_ref, d_ref, nm_ref, nv_ref):
        d_ref[...], nm_ref[...], nv_ref[...] = _adamw_math(w_ref[...], g_ref[...], m_ref[...], v_ref[...])

    spec = pl.BlockSpec((th, tw), lambda r, q: (r, q))
    return pl.pallas_call(
        body,
        grid=(rows // th, width // tw),
        in_specs=[spec] * 4,
        out_specs=[spec] * 3,
        out_shape=[jax.ShapeDtypeStruct((rows, width), f32)] * 3,
        compiler_params=_params(("parallel", "parallel")),
        name="adamw",
    )(w, g, m, v)


def _adamw_halves(place, w, mine, got, m, v, axis):
    rows, width = w.shape
    h, hw = mine.shape
    th, tw = _tile(h, hw, operands=10)
    nr, nq = h // th, hw // tw

    def body(p_ref, w_ref, a_ref, b_ref, m_ref, v_ref, g_ref, d_ref, nm_ref, nv_ref):
        g = jnp.where(pl.program_id(0) == p_ref[0], a_ref[...], b_ref[...])
        g_ref[...] = g
        d_ref[...], nm_ref[...], nv_ref[...] = _adamw_math(w_ref[...], g, m_ref[...], v_ref[...])

    if axis == 0:
        full = pl.BlockSpec((th, tw), lambda e, r, q, p: (e * nr + r, q))
    else:
        full = pl.BlockSpec((th, tw), lambda e, r, q, p: (r, e * nq + q))
    half = pl.BlockSpec((th, tw), lambda e, r, q, p: (r, q))
    return pl.pallas_call(
        body,
        grid_spec=pltpu.PrefetchScalarGridSpec(
            num_scalar_prefetch=1,
            grid=(2, nr, nq),
            in_specs=[full, half, half, full, full],
            out_specs=[full] * 4,
        ),
        out_shape=[jax.ShapeDtypeStruct((rows, width), f32)] * 4,
        compiler_params=_params(("parallel", "parallel", "parallel")),
        name="adamw_halves",
    )(place, w, mine, got, m, v)


def _pair_sum_columns(place, buf, got, out_dtype, th=432):
    rows, hw = got.shape

    def body(p_ref, a_ref, b_ref, o_ref):
        del p_ref
        o_ref[...] = (a_ref[...] + b_ref[...]).astype(out_dtype)

    return pl.pallas_call(
        body,
        grid_spec=pltpu.PrefetchScalarGridSpec(
            num_scalar_prefetch=1,
            grid=(rows // th,),
            in_specs=[pl.BlockSpec((th, hw), lambda r, p: (r, p[0])), pl.BlockSpec((th, hw), lambda r, p: (r, 0))],
            out_specs=pl.BlockSpec((th, hw), lambda r, p: (r, 0)),
        ),
        out_shape=jax.ShapeDtypeStruct((rows, hw), out_dtype),
        compiler_params=_params(("parallel",)),
        name="pair_sum_columns",
    )(place, buf, got)


_SMALL = (("norm_g", 8), ("ln_v_g", 8), ("ln_v_b", 8), ("w_spatial", 1024), ("b_spatial", 8), ("b_gate_up", 4),
          ("gla_norm_g", 2), ("final_norm_g", 8), ("w_gate_up", 64), ("loss", 8))
_SMALL_ROWS = 1152


def _pack_rows(arrays, rows):
    flat = jnp.concatenate([a.reshape(-1, 128) for a in arrays], axis=0)
    return jnp.pad(flat, ((0, rows - flat.shape[0]), (0, 0)))


def kernel(x, norm_g, w_in, ln_v_g, ln_v_b, w_spatial, b_spatial, w_gate_up, b_gate_up, gla_norm_g, w_branch_a, w_branch_b, w_out, final_norm_g, loss_target, m_norm_g, m_w_in, m_ln_v_g, m_ln_v_b, m_w_spatial, m_b_spatial, m_w_gate_up, m_b_gate_up, m_gla_norm_g, m_w_branch_a, m_w_branch_b, m_w_out, m_final_norm_g, v_norm_g, v_w_in, v_ln_v_g, v_ln_v_b, v_w_spatial, v_b_spatial, v_w_gate_up, v_b_gate_up, v_gla_norm_g, v_w_branch_a, v_w_branch_b, v_w_out, v_final_norm_g):
    chip = 2 * lax.axis_index("x") + lax.axis_index("y")
    core = lax.axis_index("c")
    place = jnp.stack([core, chip]).astype(jnp.int32)
    mat_names = ("w_branch_a", "w_branch_b", "w_out")

    wt_shard = jnp.transpose(w_in[0]).astype(bf16)
    mats = [w[0].astype(bf16).reshape(2, D // NCHIP // 2, D) for w in (w_branch_a, w_branch_b, w_out)]
    gate_sh = w_gate_up[0].reshape(2, RANK // 2, 128)
    (g_win,) = _gather_weights([wt_shard], [True])

    def with_own(gathered, own):
        mine = (jnp.arange(NCHIP) == chip).reshape((NCHIP,) + (1,) * own.ndim)
        return jnp.where(mine, own[None], gathered)

    w_full_t = with_own(g_win, wt_shard).reshape(NCHIP * WIN_SHARD, D)
    w_main_t = jnp.concatenate([w_full_t[:LR_COL], w_full_t[LR_COL + RANK:]], axis=0)
    w_lr_t = jnp.pad(w_full_t[LR_COL:LR_COL + RANK], ((0, LRP - RANK), (0, 0)))
    b_sb = jnp.broadcast_to(b_spatial[0][:, :, None], (HA, CA, GA))
    xs, tgt = x[0], loss_target[0]

    proj, lr, h, g_a, g_b, g_o, g_gate = _proj_fwd(xs, norm_g, w_main_t, w_lr_t, riders=mats + [gate_sh])
    w_a, w_b, w_o = (with_own(g, own).reshape(D, D) for g, own in zip((g_a, g_b, g_o), mats))
    w_gate = jnp.transpose(with_own(g_gate, gate_sh).reshape(NCHIP, RANK, 128), (1, 0, 2)).reshape(RANK, KEYB)
    w_gate = jnp.pad(w_gate, ((0, LRP - RANK), (0, 0)))
    a = _mixer_a_fwd(proj, ln_v_g, ln_v_b, w_spatial[0], b_sb)
    o, ob, states = _gla_fwd(proj, lr, w_gate, b_gate_up, gla_norm_g)
    dproj, dy, da, dob, dwa, dwb, dwo, dgf, loss_cols = _merge_fwd_bwd(xs, tgt, proj, a, ob, w_a, w_b, w_o,
                                                                       final_norm_g.reshape(1, D))
    dproj, dws, dbs, dlg, dlb = _mixer_a_bwd(proj, da, dproj, ln_v_g, ln_v_b, w_spatial[0], b_sb)
    dproj, dlr, dwg, dbg, dgg = _gla_bwd(proj, lr, o, states, dob, dproj, w_gate, b_gate_up, gla_norm_g)
    b_mats = [t.reshape(NCHIP, 2, D // NCHIP // 2, D) for t in (dwa, dwb, dwo)]
    dwt, got_win, *got_mats = _dw_in(h, dproj, dlr, swap=True, riders=b_mats, kinds=[CHIP_FIRST] * 3)
    part_win = _pair_sum_columns(place, dwt, got_win, bf16)
    part_mats = [_pair_sum(place, b, CHIP_FIRST, g, bf16) for b, g in zip(b_mats, got_mats)]
    dx, dg0, *slots_big = _dx_bwd(xs, dy, dproj, dlr, norm_g, w_main_t, w_lr_t, riders=[part_win] + part_mats,
                                  sends_what=[ROWS_TO_ITS_CHIP] + [TO_ITS_CHIP] * 3)
    small = _pack_rows([dg0, dlg, dlb, dws, dbs[:, :, 0], dbg, dgg, dgf, dwg[:RANK], loss_cols], _SMALL_ROWS)
    b_small = small.reshape(2, 1, _SMALL_ROWS // 2, 128)
    (got_small,) = _sibling_halves([b_small], [HALF_FIRST])
    part_small = _pair_sum(place, b_small, HALF_FIRST, got_small, f32)
    (slots_small,) = _chip_exchange([part_small], [TO_EVERY_CHIP])
    own_win = lax.dynamic_slice_in_dim(part_win, chip * PIECE_STEP, PIECE_ROWS, axis=0)[None]
    mine = [_chip_sum(place, p, s) for p, s in zip([own_win] + part_mats + [part_small], slots_big + [slots_small])]
    theirs = list(_sibling_swap(mine))
    mine[0], theirs[0] = (lax.dynamic_slice_in_dim(t, (WIN_SHARD - PIECE_STEP) * chip, WIN_SHARD, axis=0)
                          for t in (mine[0], theirs[0]))

    g_small = jnp.where(core == 0, jnp.concatenate([mine[4], theirs[4]], axis=0),
                        jnp.concatenate([theirs[4], mine[4]], axis=0))
    grads = {}
    row = 0
    for name, rows in _SMALL:
        grads[name] = g_small[row:row + rows]
        row += rows
    loss = jnp.sum(grads["loss"])
    dwg_full = grads["w_gate_up"].reshape(RANK, KEYB)
    grads["w_gate_up"] = lax.dynamic_slice_in_dim(dwg_full, chip * 128, 128, axis=1)

    weights = dict(norm_g=norm_g, w_in=w_in, ln_v_g=ln_v_g, ln_v_b=ln_v_b, w_spatial=w_spatial, b_spatial=b_spatial,
                   w_gate_up=w_gate_up, b_gate_up=b_gate_up, gla_norm_g=gla_norm_g, w_branch_a=w_branch_a,
                   w_branch_b=w_branch_b, w_out=w_out, final_norm_g=final_norm_g)
    m_in = dict(norm_g=m_norm_g, w_in=m_w_in, ln_v_g=m_ln_v_g, ln_v_b=m_ln_v_b, w_spatial=m_w_spatial,
                b_spatial=m_b_spatial, w_gate_up=m_w_gate_up, b_gate_up=m_b_gate_up, gla_norm_g=m_gla_norm_g,
                w_branch_a=m_w_branch_a, w_branch_b=m_w_branch_b, w_out=m_w_out, final_norm_g=m_final_norm_g)
    v_in = dict(norm_g=v_norm_g, w_in=v_w_in, ln_v_g=v_ln_v_g, ln_v_b=v_ln_v_b, w_spatial=v_w_spatial,
                b_spatial=v_b_spatial, w_gate_up=v_w_gate_up, b_gate_up=v_b_gate_up, gla_norm_g=v_gla_norm_g,
                w_branch_a=v_w_branch_a, w_branch_b=v_w_branch_b, w_out=v_w_out, final_norm_g=v_final_norm_g)
    names = list(weights)
    small_names = [n for n in names if n != "w_in" and n not in mat_names]
    out_g, out_d, out_m, out_v = {}, {}, {}, {}
    res = _adamw_halves(place, jnp.transpose(w_in[0]), mine[0], theirs[0], jnp.transpose(m_w_in[0]),
                        jnp.transpose(v_w_in[0]), axis=1)
    out_g["w_in"], out_d["w_in"], out_m["w_in"], out_v["w_in"] = (jnp.transpose(t)[None] for t in res)
    for i, n in enumerate(mat_names):
        res = _adamw_halves(place, weights[n][0], mine[1 + i], theirs[1 + i], m_in[n][0], v_in[n][0], axis=0)
        out_g[n], out_d[n], out_m[n], out_v[n] = (t[None] for t in res)
    upd_rows = sum(weights[n].size for n in small_names) // 128
    pad_rows = -(-upd_rows // 8) * 8
    packed = [_pack_rows([t[n] for n in small_names], pad_rows) for t in (weights, grads, m_in, v_in)]
    d_s, m_s, v_s = _adamw(*packed)
    row = 0
    for n in small_names:
        shape = weights[n].shape
        rows = weights[n].size // 128
        out_g[n] = grads[n].reshape(shape)
        out_d[n], out_m[n], out_v[n] = (t[row:row + rows].reshape(shape) for t in (d_s, m_s, v_s))
        row += rows
    return (loss, dx[None], *[out_g[n] for n in names], *[out_d[n] for n in names],
            *[out_m[n] for n in names], *[out_v[n] for n in names])
```

```python
import functools
import math

import jax
import jax.numpy as jnp
from jax import lax
from jax.experimental import pallas as pl
from jax.experimental.pallas import tpu as pltpu

f32 = jnp.float32
bf16 = jnp.bfloat16

D = 1024
NMAIN = 8192
LRP = 128
RANK = 16
HA, GA, CA = 8, 128, 128
HB, DK, DV, CB = 4, 128, 256, 64
KEYB = HB * DK
EPS = 1e-6
LN_EPS = 1e-5
GATE_NORM = 16.0
QSCALE = DK ** -0.5
COL_U, COL_V, COL_ZA = 0, 1, 2
COL_Q, COL_K = 6, 7
COL_VB, COL_ZB = 4, 5
COL_GATES = 3
VMEM_LIMIT = 56 * 1024 * 1024
NCHIP = 4
WIN_SHARD = 2052
LR_COL = 6144
_ANY = pl.BlockSpec(memory_space=pl.ANY)

ADAM_LR, ADAM_B1, ADAM_B2, ADAM_EPS, ADAM_WD, ADAM_STEP = 0.001, 0.9, 0.999, 1e-08, 0.01, 10

_SQRT_HALF = 0.7071067811865476
_INV_SQRT_2PI = 0.3989422804014327


def _dot(a, b):
    return jnp.dot(a, b, preferred_element_type=f32)


def _dot_nt(a, b):
    return lax.dot_general(a, b, (((1,), (1,)), ((), ())), preferred_element_type=f32)


def _dot_tn(a, b):
    return lax.dot_general(a, b, (((0,), (0,)), ((), ())), preferred_element_type=f32)


def _dot_exact(a, b):
    return jnp.dot(a, b, preferred_element_type=f32, precision=lax.Precision.HIGHEST)


def _gelu(x):
    return 0.5 * x * (1.0 + lax.erf(x * _SQRT_HALF))


def _gelu_grad(x):
    return 0.5 * (1.0 + lax.erf(x * _SQRT_HALF)) + x * (jnp.exp(-0.5 * x * x) * _INV_SQRT_2PI)


def _sigmoid(x):
    return 1.0 / (1.0 + jnp.exp(-x))


def _params(sem):
    return pltpu.CompilerParams(dimension_semantics=sem, vmem_limit_bytes=VMEM_LIMIT)


def _resident(shape):
    nd = len(shape)
    return pl.BlockSpec(shape, lambda *_: (0,) * nd, pipeline_mode=pl.Buffered(1))


def _w_rows(c, tn):
    start = c * tn + (RANK if c * tn >= LR_COL else 0)
    return slice(start, start + tn)


LR_ROWS = slice(LR_COL, LR_COL + LRP)


def _proj_fwd(x, g0, w_t, riders=(), tm=256, tn=1024):
    T = x.shape[0]
    nsteps = T // tm
    n = len(riders)

    def body(x_ref, g_ref, w_ref, *rest):
        srcs, (proj_ref, lr_ref, h_ref), dsts, sems = rest[:n], rest[n:n + 3], rest[n + 3:2 * n + 3], rest[2 * n + 3:]
        if n:
            @pl.when(pl.program_id(0) == 0)
            def _():
                _gather_start(srcs, dsts, [False] * n, sems)

        xv = x_ref[...]
        r = lax.rsqrt(jnp.mean(xv * xv, axis=-1, keepdims=True) + EPS)
        h = (xv * r * g_ref[...]).astype(bf16)
        h_ref[...] = h
        lr_ref[...] = _dot_nt(h, w_ref[LR_ROWS, :])
        for c in range(NMAIN // tn):
            proj_ref[:, c * tn:(c + 1) * tn] = _dot_nt(h, w_ref[_w_rows(c, tn), :])

        if n:
            @pl.when(pl.program_id(0) == nsteps - 1)
            def _():
                _gather_finish(srcs, dsts, [False] * n, sems)

    return pl.pallas_call(
        body,
        grid=(nsteps,),
        in_specs=[
            pl.BlockSpec((tm, D), lambda i: (i, 0)),
            _resident((1, D)), _resident((NMAIN + RANK, D)),
        ] + [_ANY] * n,
        out_specs=[
            pl.BlockSpec((tm, NMAIN), lambda i: (i, 0)),
            pl.BlockSpec((tm, LRP), lambda i: (i, 0)),
            pl.BlockSpec((tm, D), lambda i: (i, 0)),
        ] + [_ANY] * n,
        out_shape=[
            jax.ShapeDtypeStruct((T, NMAIN), f32),
            jax.ShapeDtypeStruct((T, LRP), f32),
            jax.ShapeDtypeStruct((T, D), bf16),
        ] + _gathered_shapes(riders),
        scratch_shapes=_gather_scratch(riders) if n else [],
        compiler_params=_params(("arbitrary",)),
        name="proj_fwd",
    )(x, g0, w_t, *riders)


def _causal_mask():
    t = lax.broadcasted_iota(jnp.int32, (CA, CA), 0)
    s = lax.broadcasted_iota(jnp.int32, (CA, CA), 1)
    return s <= t


def _layernorm_parts(gv):
    mu = jnp.mean(gv, axis=-1, keepdims=True)
    xc = gv - mu
    rs = lax.rsqrt(jnp.mean(xc * xc, axis=-1, keepdims=True) + LN_EPS)
    return xc * rs, rs


def _mixer_a_fwd(proj, ln_g, ln_b, w_s, b_sb, tm=256):
    T = proj.shape[0]

    def body(u_ref, v_ref, za_ref, lg_ref, lb_ref, ws_ref, bs_ref, a_ref, vln_s):
        vhat, _ = _layernorm_parts(_gelu(v_ref[...]))
        vln_s[...] = (vhat * lg_ref[...] + lb_ref[...]).astype(bf16)
        mask = _causal_mask()
        for g in range(HA):
            wg = jnp.where(mask, ws_ref[g], 0.0).astype(bf16)
            cols = slice(g * GA, (g + 1) * GA)
            for c in range(tm // CA):
                rows = slice(c * CA, (c + 1) * CA)
                mixed = _dot(wg, vln_s[rows, cols]) + bs_ref[g]
                za = za_ref[rows, cols]
                a = _gelu(u_ref[rows, cols]) * mixed * (za * _sigmoid(za))
                a_ref[rows, cols] = a.astype(bf16)

    def col(cidx):
        return pl.BlockSpec((tm, D), lambda i, c=cidx: (i, c))

    return pl.pallas_call(
        body,
        grid=(T // tm,),
        in_specs=[col(COL_U), col(COL_V), col(COL_ZA), _resident((1, D)), _resident((1, D)),
                  _resident((HA, CA, CA)), _resident((HA, CA, GA))],
        out_specs=pl.BlockSpec((tm, D), lambda i: (i, 0)),
        out_shape=jax.ShapeDtypeStruct((T, D), bf16),
        scratch_shapes=[pltpu.VMEM((tm, D), bf16)],
        compiler_params=_params(("parallel",)),
        name="mixer_a_fwd",
    )(proj, proj, proj, ln_g, ln_b, w_s, b_sb)


def _mixer_a_bwd(proj, da, dproj, ln_g, ln_b, w_s, b_sb, riders=(), kinds=(), tm=256):
    T = proj.shape[0]
    nsteps = T // tm
    n = len(riders)

    def body(u_ref, v_ref, za_ref, da_ref, dp_in, lg_ref, lb_ref, ws_ref, bs_ref, *rest):
        srcs, (dp_ref, dws_ref, dbs_ref, dlg_ref, dlb_ref), dsts = rest[:n], rest[n:n + 5], rest[n + 5:2 * n + 5]
        vln_s, dvln_s, *sems = rest[2 * n + 5:]
        del dp_in
        i = pl.program_id(0)

        @pl.when(i == 0)
        def _():
            dws_ref[...] = jnp.zeros_like(dws_ref)
            dbs_ref[...] = jnp.zeros_like(dbs_ref)
            dlg_ref[...] = jnp.zeros_like(dlg_ref)
            dlb_ref[...] = jnp.zeros_like(dlb_ref)
            for cp in _halves_copies(srcs, dsts, kinds, sems) if n else []:
                cp.start()

        v = v_ref[...]
        vhat, rs = _layernorm_parts(_gelu(v))
        vln_s[...] = (vhat * lg_ref[...] + lb_ref[...]).astype(bf16)
        mask = _causal_mask()
        for g in range(HA):
            wg = jnp.where(mask, ws_ref[g], 0.0).astype(bf16)
            cols = slice(g * GA, (g + 1) * GA)
            dw_acc = jnp.zeros((CA, CA), f32)
            db_acc = jnp.zeros((CA, 1), f32)
            for c in range(tm // CA):
                rows = slice(c * CA, (c + 1) * CA)
                vln = vln_s[rows, cols]
                mixed = _dot(wg, vln) + bs_ref[g]
                u = u_ref[rows, cols]
                za = za_ref[rows, cols]
                da_blk = da_ref[rows, cols]
                sg = _sigmoid(za)
                sz = za * sg
                gu = _gelu(u)
                dp_ref[rows, cols] = (da_blk * mixed * sz * _gelu_grad(u)).astype(bf16)
                dp_ref[rows, 2 * D + g * GA:2 * D + (g + 1) * GA] = (
                    da_blk * gu * mixed * (sg * (1.0 + za * (1.0 - sg)))).astype(bf16)
                dmixed = da_blk * gu * sz
                dmb = dmixed.astype(bf16)
                dvln_s[rows, cols] = _dot_tn(wg, dmb)
                dw_acc = dw_acc + _dot_nt(dmb, vln)
                db_acc = db_acc + jnp.sum(dmixed, axis=-1, keepdims=True)
            dws_ref[g] += dw_acc
            dbs_ref[g] += jnp.broadcast_to(db_acc, (CA, GA))

        dvln = dvln_s[...]
        dlg_ref[...] += jnp.sum(dvln * vhat, axis=0, keepdims=True)
        dlb_ref[...] += jnp.sum(dvln, axis=0, keepdims=True)
        dvhat = dvln * lg_ref[...]
        dgv = rs * (dvhat - jnp.mean(dvhat, axis=-1, keepdims=True)
                    - vhat * jnp.mean(dvhat * vhat, axis=-1, keepdims=True))
        dp_ref[:, D:2 * D] = (dgv * _gelu_grad(v)).astype(bf16)

        @pl.when(i == nsteps - 1)
        def _():
            for g in range(HA):
                dws_ref[g] = jnp.where(mask, dws_ref[g], 0.0)
            for cp in _halves_copies(srcs, dsts, kinds, sems) if n else []:
                cp.wait()

    def col(cidx):
        return pl.BlockSpec((tm, D), lambda i, c=cidx: (i, c))

    return pl.pallas_call(
        body,
        grid=(nsteps,),
        in_specs=[col(COL_U), col(COL_V), col(COL_ZA), pl.BlockSpec((tm, D), lambda i: (i, 0)),
                  pl.BlockSpec(memory_space=pl.ANY),
                  _resident((1, D)), _resident((1, D)), _resident((HA, CA, CA)), _resident((HA, CA, GA))] + [_ANY] * n,
        out_specs=[pl.BlockSpec((tm, 3 * D), lambda i: (i, 0)),
                   _resident((HA, CA, CA)), _resident((HA, CA, GA)), _resident((1, D)), _resident((1, D))] + [_ANY] * n,
        out_shape=[jax.ShapeDtypeStruct(dproj.shape, dproj.dtype),
                   jax.ShapeDtypeStruct((HA, CA, CA), f32), jax.ShapeDtypeStruct((HA, CA, GA), f32),
                   jax.ShapeDtypeStruct((1, D), f32), jax.ShapeDtypeStruct((1, D), f32)] + _halves_shapes(riders, kinds),
        scratch_shapes=[pltpu.VMEM((tm, D), bf16), pltpu.VMEM((tm, D), f32)] + (_halves_sems(riders) if n else []),
        input_output_aliases={4: 0},
        compiler_params=_params(("arbitrary",)),
        name="mixer_a_bwd",
    )(proj, proj, proj, da, dproj, ln_g, ln_b, w_s, b_sb, *riders)


def _tri(n, upper):
    r = lax.broadcasted_iota(jnp.int32, (n, n), 0)
    c = lax.broadcasted_iota(jnp.int32, (n, n), 1)
    return jnp.where((c >= r) if upper else (c <= r), 1.0, 0.0).astype(f32)


def _chunk_tri(n, upper):
    r = lax.broadcasted_iota(jnp.int32, (n, n), 0)
    c = lax.broadcasted_iota(jnp.int32, (n, n), 1)
    shift = CB.bit_length() - 1
    same_chunk = jnp.right_shift(r, shift) == jnp.right_shift(c, shift)
    return jnp.where(same_chunk & ((c >= r) if upper else (c <= r)), 1.0, 0.0).astype(f32)


def _log_alpha(lr, wg, bg):
    logit = _dot(lr.astype(bf16), wg.astype(bf16)) + bg
    la = (jnp.minimum(logit, 0.0) - jnp.log1p(jnp.exp(-jnp.abs(logit)))) * (1.0 / GATE_NORM)
    return logit, la


def _gla_fwd(proj, lr, w_gate, b_gate, gla_g, tm=256):
    T = proj.shape[0]
    nchunk = T // CB
    cpb = tm // CB

    def body(q_ref, k_ref, v_ref, zb_ref, lr_ref, wg_ref, bg_ref, gg_ref,
             o_ref, ob_ref, st_ref, state, la_s):
        @pl.when(pl.program_id(0) == 0)
        def _():
            state[...] = jnp.zeros_like(state)

        _, la = _log_alpha(lr_ref[...], wg_ref[...], bg_ref[...])
        la_s[...] = _dot_exact(_chunk_tri(tm, upper=False), la)
        causal = _tri(CB, upper=False) > 0.5
        states = [state[hd] for hd in range(HB)]
        for c in range(cpb):
            rows = slice(c * CB, (c + 1) * CB)
            b = la_s[rows, :]
            bl = b[CB - 1:CB, :]
            bm = b[CB // 2 - 1:CB // 2, :]
            q = q_ref[rows, :] * QSCALE
            k = k_ref[rows, :]
            qi_all = (q * jnp.exp(b - bm)).astype(bf16)
            ki_all = (k * jnp.exp(bm - b)).astype(bf16)
            qe_all = (q * jnp.exp(b)).astype(bf16)
            ks_all = (k * jnp.exp(bl - b)).astype(bf16)
            e_l = jnp.exp(bl)
            for hd in range(HB):
                kc = slice(hd * DK, (hd + 1) * DK)
                vc = slice(hd * DV, (hd + 1) * DV)
                v = v_ref[rows, vc].astype(bf16)
                p = jnp.where(causal, _dot_nt(qi_all[:, kc], ki_all[:, kc]), 0.0).astype(bf16)
                s0 = states[hd]
                st_ref[c, hd] = s0
                o = _dot(p, v) + _dot_nt(qe_all[:, kc], s0.astype(bf16))
                states[hd] = s0 * e_l[:, kc] + _dot_tn(v, ks_all[:, kc])
                o_ref[rows, vc] = o
                ro = lax.rsqrt(jnp.mean(o * o, axis=-1, keepdims=True) + EPS)
                zb = zb_ref[rows, vc]
                ob_ref[rows, vc] = (o * ro * gg_ref[...] * (zb * _sigmoid(zb))).astype(bf16)
        for hd in range(HB):
            state[hd] = states[hd]

    return pl.pallas_call(
        body,
        grid=(T // tm,),
        in_specs=[pl.BlockSpec((tm, KEYB), lambda i: (i, COL_Q)),
                  pl.BlockSpec((tm, KEYB), lambda i: (i, COL_K)),
                  pl.BlockSpec((tm, D), lambda i: (i, COL_VB)),
                  pl.BlockSpec((tm, D), lambda i: (i, COL_ZB)),
                  pl.BlockSpec((tm, LRP), lambda i: (i, 0)),
                  _resident((LRP, KEYB)), _resident((1, KEYB)), _resident((1, DV))],
        out_specs=[pl.BlockSpec((tm, D), lambda i: (i, 0)),
                   pl.BlockSpec((tm, D), lambda i: (i, 0)),
                   pl.BlockSpec((cpb, HB, DV, DK), lambda i: (i, 0, 0, 0))],
        out_shape=[jax.ShapeDtypeStruct((T, D), f32), jax.ShapeDtypeStruct((T, D), bf16),
                   jax.ShapeDtypeStruct((nchunk, HB, DV, DK), f32)],
        scratch_shapes=[pltpu.VMEM((HB, DV, DK), f32), pltpu.VMEM((tm, KEYB), f32)],
        compiler_params=_params(("arbitrary",)),
        name="gla_fwd",
    )(proj, proj, proj, proj, lr, w_gate, b_gate, gla_g)


def _gla_bwd(proj, lr, o, states, dob, dproj, w_gate, b_gate, gla_g, riders=(), sends_what=(), tm=256):
    T = proj.shape[0]
    cpb = tm // CB
    nb = T // tm
    n = len(riders)

    def body(q_ref, k_ref, v_ref, zb_ref, lr_ref, o_ref, st_ref, dob_ref, dp_in, wg_ref, bg_ref, gg_ref, *rest):
        srcs, (dp_ref, dlr_ref, dwg_ref, dbg_ref, dgg_ref), dsts = rest[:n], rest[n:n + 5], rest[n + 5:2 * n + 5]
        dstate, la_s, dlogit_s, tail_s, *sems = rest[2 * n + 5:]
        del dp_in
        step = pl.program_id(0)

        @pl.when(step == 0)
        def _():
            dstate[...] = jnp.zeros_like(dstate)
            dwg_ref[...] = jnp.zeros_like(dwg_ref)
            dbg_ref[...] = jnp.zeros_like(dbg_ref)
            dgg_ref[...] = jnp.zeros_like(dgg_ref)
            for cp in _exchange_copies(srcs, dsts, sends_what, sems) if n else []:
                cp.start()

        lr_v = lr_ref[...]
        logit, la = _log_alpha(lr_v, wg_ref[...], bg_ref[...])
        la_s[...] = _dot_exact(_chunk_tri(tm, upper=False), la)
        causal = _tri(CB, upper=False) > 0.5
        gg = gg_ref[...]
        dgg_acc = jnp.zeros((1, DV), f32)
        dstates = [dstate[hd] for hd in range(HB)]
        for c in reversed(range(cpb)):
            rows = slice(c * CB, (c + 1) * CB)
            b = la_s[rows, :]
            bl = b[CB - 1:CB, :]
            bm = b[CB // 2 - 1:CB // 2, :]
            eb_all, eqm_all, ekm_all = jnp.exp(b), jnp.exp(b - bm), jnp.exp(bm - b)
            eks_all, el_all = jnp.exp(bl - b), jnp.exp(bl)
            q_all = q_ref[rows, :] * QSCALE
            k_all = k_ref[rows, :]
            qi_all = (q_all * eqm_all).astype(bf16)
            ki_all = (k_all * ekm_all).astype(bf16)
            qe_all = (q_all * eb_all).astype(bf16)
            ksf_all = k_all * eks_all
            ks_all = ksf_all.astype(bf16)
            for hd in range(HB):
                kc = slice(hd * DK, (hd + 1) * DK)
                vc = slice(hd * DV, (hd + 1) * DV)
                o_h = o_ref[rows, vc]
                ro = lax.rsqrt(jnp.mean(o_h * o_h, axis=-1, keepdims=True) + EPS)
                ohat = o_h * ro
                zb = zb_ref[rows, vc]
                sg = _sigmoid(zb)
                dob_h = dob_ref[rows, vc]
                don = dob_h * (zb * sg)
                dp_ref[rows, 2 * D + hd * DV:2 * D + (hd + 1) * DV] = (
                    dob_h * ohat * gg * (sg * (1.0 + zb * (1.0 - sg)))).astype(bf16)
                dgg_acc = dgg_acc + jnp.sum(don * ohat, axis=0, keepdims=True)
                dohat = don * gg
                do = (ro * (dohat - ohat * jnp.mean(dohat * ohat, axis=-1, keepdims=True))).astype(bf16)
                e_b, e_qm, e_km, e_ks, e_l = eb_all[:, kc], eqm_all[:, kc], ekm_all[:, kc], eks_all[:, kc], el_all[:, kc]
                q, k, ks_f = q_all[:, kc], k_all[:, kc], ksf_all[:, kc]
                qi, ki, qe, ks = qi_all[:, kc], ki_all[:, kc], qe_all[:, kc], ks_all[:, kc]
                v = v_ref[rows, vc].astype(bf16)
                p = jnp.where(causal, _dot_nt(qi, ki), 0.0).astype(bf16)
                s0 = st_ref[c, hd]
                ds = dstates[hd]
                ds_b = ds.astype(bf16)
                dv = _dot_tn(p, do) + _dot_nt(ks, ds_b)
                dpm = jnp.where(causal, _dot_nt(do, v), 0.0).astype(bf16)
                dqi = _dot(dpm, ki)
                dki = _dot_tn(dpm, qi)
                dqe = _dot(do, s0.astype(bf16))
                dks = _dot(v, ds_b)
                dq_s = dqi * e_qm + dqe * e_b
                dk = dki * e_km + dks * e_ks
                tail = (jnp.sum(dks * ks_f, axis=0, keepdims=True)
                        + e_l * jnp.sum(ds * s0, axis=0, keepdims=True))
                dstates[hd] = _dot_tn(do, qe) + ds * e_l
                dp_ref[rows, kc] = (dq_s * QSCALE).astype(bf16)
                dp_ref[rows, KEYB + hd * DK:KEYB + (hd + 1) * DK] = dk.astype(bf16)
                dp_ref[rows, D + hd * DV:D + (hd + 1) * DV] = dv.astype(bf16)
                dlogit_s[rows, kc] = dq_s * q - dk * k
                tail_s[rows, kc] = jnp.broadcast_to(tail, (CB, DK))
        for hd in range(HB):
            dstate[hd] = dstates[hd]
        dgg_ref[...] += dgg_acc
        dg = _dot_exact(_chunk_tri(tm, upper=True), dlogit_s[...]) + tail_s[...]
        dlogit = dg * (1.0 / GATE_NORM) * _sigmoid(-logit)
        dbg_ref[...] += jnp.sum(dlogit, axis=0, keepdims=True)
        dlb = dlogit.astype(bf16)
        dlr_ref[...] = _dot_nt(dlb, wg_ref[...].astype(bf16)).astype(bf16)
        dwg_ref[...] += _dot_tn(lr_v.astype(bf16), dlb)

        if n:
            @pl.when(step == nb - 1)
            def _():
                for cp in _exchange_copies(srcs, dsts, sends_what, sems):
                    cp.wait()

    def rev(cidx):
        return lambda i, c=cidx: (nb - 1 - i, c)

    return pl.pallas_call(
        body,
        grid=(nb,),
        in_specs=[pl.BlockSpec((tm, KEYB), rev(COL_Q)),
                  pl.BlockSpec((tm, KEYB), rev(COL_K)),
                  pl.BlockSpec((tm, D), rev(COL_VB)),
                  pl.BlockSpec((tm, D), rev(COL_ZB)),
                  pl.BlockSpec((tm, LRP), rev(0)),
                  pl.BlockSpec((tm, D), rev(0)),
                  pl.BlockSpec((cpb, HB, DV, DK), lambda i: (nb - 1 - i, 0, 0, 0)),
                  pl.BlockSpec((tm, D), rev(0)),
                  pl.BlockSpec(memory_space=pl.ANY),
                  _resident((LRP, KEYB)), _resident((1, KEYB)), _resident((1, DV))] + [_ANY] * n,
        out_specs=[pl.BlockSpec((tm, 3 * D), rev(1)),
                   pl.BlockSpec((tm, LRP), rev(0)),
                   _resident((LRP, KEYB)), _resident((1, KEYB)), _resident((1, DV))] + [_ANY] * n,
        out_shape=[jax.ShapeDtypeStruct(dproj.shape, dproj.dtype),
                   jax.ShapeDtypeStruct((T, LRP), bf16),
                   jax.ShapeDtypeStruct((LRP, KEYB), f32), jax.ShapeDtypeStruct((1, KEYB), f32),
                   jax.ShapeDtypeStruct((1, DV), f32)] + _exchange_shapes(riders, sends_what),
        scratch_shapes=[pltpu.VMEM((HB, DV, DK), f32)] + [pltpu.VMEM((tm, KEYB), f32)] * 3
        + (_exchange_sems(n) if n else []),
        input_output_aliases={8: 0},
        compiler_params=_params(("arbitrary",)),
        name="gla_bwd",
    )(proj, proj, proj, proj, lr, o, states, dob, dproj, w_gate, b_gate, gla_g, *riders)


def _merge_fwd_bwd(x, tgt, proj, a, ob, w_a, w_b, w_o, g_f, tm=256):
    T = x.shape[0]

    def body(x_ref, t_ref, gt_ref, a_ref, ob_ref, wa_ref, wb_ref, wo_ref, gf_ref,
             dp_ref, dy_ref, da_ref, dob_ref, dwa_ref, dwb_ref, dwo_ref, dgf_ref, loss_ref):
        @pl.when(pl.program_id(0) == 0)
        def _():
            dwa_ref[...] = jnp.zeros_like(dwa_ref)
            dwb_ref[...] = jnp.zeros_like(dwb_ref)
            dwo_ref[...] = jnp.zeros_like(dwo_ref)
            dgf_ref[...] = jnp.zeros_like(dgf_ref)
            loss_ref[...] = jnp.zeros_like(loss_ref)

        ga = _sigmoid(gt_ref[:, :D])
        gb = _sigmoid(gt_ref[:, D:])
        a_v = a_ref[...]
        ob_v = ob_ref[...]
        pa = _dot(a_v, wa_ref[...])
        pb = _dot(ob_v, wb_ref[...])
        mb = (ga * pa + gb * pb).astype(bf16)
        y = x_ref[...] + _dot(mb, wo_ref[...])
        r1 = lax.rsqrt(jnp.mean(y * y, axis=-1, keepdims=True) + EPS)
        yhat = y * r1
        gf = gf_ref[...]
        err = yhat * gf - t_ref[...]
        loss_ref[...] += jnp.sum(err * err, axis=0, keepdims=True) * (0.5 / D)
        dout = err * (1.0 / D)
        dgf_ref[...] += jnp.sum(dout * yhat, axis=0, keepdims=True)
        dyn = dout * gf
        dy = r1 * (dyn - yhat * jnp.mean(dyn * yhat, axis=-1, keepdims=True))
        dy_ref[...] = dy
        dyb = dy.astype(bf16)
        dwo_ref[...] += _dot_tn(mb, dyb)
        dm = _dot_nt(dyb, wo_ref[...])
        dpa = (dm * ga).astype(bf16)
        dpb = (dm * gb).astype(bf16)
        dp_ref[:, :D] = (dm * pa * ga * (1.0 - ga)).astype(bf16)
        dp_ref[:, D:] = (dm * pb * gb * (1.0 - gb)).astype(bf16)
        dwa_ref[...] += _dot_tn(a_v, dpa)
        dwb_ref[...] += _dot_tn(ob_v, dpb)
        da_ref[...] = _dot_nt(dpa, wa_ref[...])
        dob_ref[...] = _dot_nt(dpb, wb_ref[...])

    row = lambda: pl.BlockSpec((tm, D), lambda i: (i, 0))
    return pl.pallas_call(
        body,
        grid=(T // tm,),
        in_specs=[row(), row(), pl.BlockSpec((tm, 2 * D), lambda i: (i, COL_GATES)), row(), row(),
                  _resident((D, D)), _resident((D, D)), _resident((D, D)), _resident((1, D))],
        out_specs=[pl.BlockSpec((tm, 2 * D), lambda i: (i, COL_GATES)), row(), row(), row(),
                   _resident((D, D)), _resident((D, D)), _resident((D, D)), _resident((1, D)), _resident((1, D))],
        out_shape=[jax.ShapeDtypeStruct((T, NMAIN), bf16),
                   jax.ShapeDtypeStruct((T, D), f32), jax.ShapeDtypeStruct((T, D), f32),
                   jax.ShapeDtypeStruct((T, D), f32),
                   jax.ShapeDtypeStruct((D, D), f32), jax.ShapeDtypeStruct((D, D), f32),
                   jax.ShapeDtypeStruct((D, D), f32),
                   jax.ShapeDtypeStruct((1, D), f32), jax.ShapeDtypeStruct((1, D), f32)],
        compiler_params=_params(("arbitrary",)),
        name="merge_fwd_bwd",
    )(x, tgt, proj, a, ob, w_a, w_b, w_o, g_f)


def _dx_bwd(x, dy, dproj, dlr, g0, w_t, riders=(), sends_what=(), tm=256):
    T = x.shape[0]
    nsteps = T // tm
    n = len(riders)

    def body(x_ref, dy_ref, dp_ref, dl_ref, g_ref, w_ref, *rest):
        srcs, (dx_ref, dg_ref), dsts, sems = rest[:n], rest[n:n + 2], rest[n + 2:2 * n + 2], rest[2 * n + 2:]

        @pl.when(pl.program_id(0) == 0)
        def _():
            dg_ref[...] = jnp.zeros_like(dg_ref)
            for cp in _exchange_copies(srcs, dsts, sends_what, sems) if n else []:
                cp.start()

        xv = x_ref[...]
        r = lax.rsqrt(jnp.mean(xv * xv, axis=-1, keepdims=True) + EPS)
        xhat = xv * r
        dh = (_dot(dp_ref[:, :LR_COL], w_ref[:LR_COL, :]) + _dot(dp_ref[:, LR_COL:], w_ref[LR_COL + RANK:, :])
              + _dot(dl_ref[...], w_ref[LR_ROWS, :]))
        dg_ref[...] += jnp.sum(dh * xhat, axis=0, keepdims=True)
        t = dh * g_ref[...]
        dx_ref[...] = dy_ref[...] + r * (t - xhat * jnp.mean(t * xhat, axis=-1, keepdims=True))

        if n:
            @pl.when(pl.program_id(0) == nsteps - 1)
            def _():
                for cp in _exchange_copies(srcs, dsts, sends_what, sems):
                    cp.wait()

    row = lambda: pl.BlockSpec((tm, D), lambda i: (i, 0))
    return pl.pallas_call(
        body,
        grid=(nsteps,),
        in_specs=[row(), row(), pl.BlockSpec((tm, NMAIN), lambda i: (i, 0)),
                  pl.BlockSpec((tm, LRP), lambda i: (i, 0)),
                  _resident((1, D)), _resident((NMAIN + RANK, D))] + [_ANY] * n,
        out_specs=[row(), _resident((1, D))] + [_ANY] * n,
        out_shape=[jax.ShapeDtypeStruct((T, D), f32), jax.ShapeDtypeStruct((1, D), f32)]
        + _exchange_shapes(riders, sends_what),
        scratch_shapes=_exchange_sems(n) if n else [],
        compiler_params=_params(("arbitrary",)),
        name="dx_bwd",
    )(x, dy, dproj, dlr, g0, w_t, *riders)


def _dw_in(h, dproj, dlr, swap=False, riders=(), kinds=(), tm=1024, tn=1024):
    T = h.shape[0]
    tm = min(tm, T)
    nj, nk = NMAIN // tn, T // tm
    lr_tile = LR_COL // tn
    n = len(riders)
    hd = D // 2

    def body(h_ref, dp_ref, dl_ref, *rest):
        srcs, rest = rest[:n], rest[n:]
        out_ref, rest = rest[0], rest[1:]
        if swap:
            got_ref, dsts, rest = rest[0], rest[1:1 + n], rest[1 + n:]
            acc, lr_acc, sems, lr_sem, swap_send, swap_recv = rest[:6]
            rider_sems = rest[6:]
        else:
            acc, lr_acc, sems, lr_sem = rest
        j, k = pl.program_id(0), pl.program_id(1)
        slot = j % 2

        def tile_row(jj):
            return pl.multiple_of(jj * tn + jnp.where(jj >= lr_tile, RANK, 0), 8)

        def tile_out(jj, s):
            return pltpu.make_async_copy(acc.at[s], out_ref.at[pl.ds(tile_row(jj), tn)], sems.at[s])

        def other_half(ref):
            c = lax.axis_index("c")
            return ref.at[:, pl.ds(pl.multiple_of((1 - c) * hd, 128), hd)]

        def tile_swap(jj, s):
            return _remote(other_half(acc.at[s]), got_ref.at[pl.ds(tile_row(jj), tn)], swap_send.at[jj], swap_recv.at[jj],
                           _sibling())

        def lr_swap():
            return _remote(other_half(lr_acc.at[pl.ds(0, RANK)]), got_ref.at[pl.ds(LR_COL, RANK)], swap_send.at[nj],
                           swap_recv.at[nj], _sibling())

        lr_rows = pltpu.make_async_copy(lr_acc.at[pl.ds(0, RANK)], out_ref.at[pl.ds(LR_COL, RANK)], lr_sem)

        @pl.when(j == 0)
        def _():
            @pl.when(k == 0)
            def _():
                lr_acc[...] = jnp.zeros_like(lr_acc)
                for cp in _halves_copies(srcs, dsts, kinds, rider_sems) if n else []:
                    cp.start()

            lr_acc[...] += _dot_tn(dl_ref[...], h_ref[...])

            @pl.when(k == nk - 1)
            def _():
                lr_rows.start()
                if swap:
                    lr_swap().start()

        @pl.when(k == 0)
        def _():
            acc[slot] = jnp.zeros((tn, D), f32)

        acc[slot] += _dot_tn(dp_ref[...], h_ref[...])

        @pl.when(k == nk - 1)
        def _():
            tile_out(j, slot).start()
            if swap:
                tile_swap(j, slot).start()

            @pl.when(j > 0)
            def _():
                tile_out(j - 1, 1 - slot).wait()
                if swap:
                    tile_swap(j - 1, 1 - slot).wait_send()

            @pl.when(j == nj - 1)
            def _():
                tile_out(j, slot).wait()
                lr_rows.wait()
                if swap:
                    tile_swap(j, slot).wait_send()
                    lr_swap().wait()
                    for jj in range(nj):
                        tile_swap(jj, 0).wait_recv()
                    for cp in _halves_copies(srcs, dsts, kinds, rider_sems) if n else []:
                        cp.wait()

    swap_outs = [jax.ShapeDtypeStruct((NMAIN + RANK, hd), f32)] + _halves_shapes(riders, kinds) if swap else []
    swap_sems = [pltpu.SemaphoreType.DMA((nj + 1,))] * 2 + (_halves_sems(riders) if n else []) if swap else []
    return pl.pallas_call(
        body,
        grid=(nj, nk),
        in_specs=[pl.BlockSpec((tm, D), lambda j, k: (k, 0)), pl.BlockSpec((tm, tn), lambda j, k: (k, j)),
                  pl.BlockSpec((tm, LRP), lambda j, k: (k, 0))] + [_ANY] * n,
        out_specs=[_ANY] * (1 + len(swap_outs)),
        out_shape=[jax.ShapeDtypeStruct((NMAIN + RANK, D), f32)] + swap_outs,
        scratch_shapes=[pltpu.VMEM((2, tn, D), f32), pltpu.VMEM((LRP, D), f32),
                        pltpu.SemaphoreType.DMA((2,)), pltpu.SemaphoreType.DMA] + swap_sems,
        compiler_params=_params(("arbitrary", "arbitrary")),
        name="dw_in",
    )(h, dproj, dlr, *riders)


MESH = pl.DeviceIdType.MESH


def _place():
    x, y, c = lax.axis_index("x"), lax.axis_index("y"), lax.axis_index("c")
    others = [(1 - x, y), (x, 1 - y), (1 - x, 1 - y)]
    return x, y, c, 2 * x + y, others


def _sibling():
    return lax.axis_index("x"), lax.axis_index("y"), 1 - lax.axis_index("c")


def _remote(src, dst, send_sem, recv_sem, to):
    return pltpu.make_async_remote_copy(src_ref=src, dst_ref=dst, send_sem=send_sem, recv_sem=recv_sem,
                                        device_id=to, device_id_type=MESH)


def _half(ref, e, by_columns):
    if not by_columns:
        return ref.at[e]
    hw = ref.shape[-1] // 2
    return ref.at[:, pl.ds(pl.multiple_of(e * hw, 128), hw)]


def _gather_weights(shards, by_columns):
    n = len(shards)

    def body(*refs):
        _gather_start(refs[:n], refs[n:2 * n], by_columns, refs[2 * n:])
        _gather_finish(refs[:n], refs[n:2 * n], by_columns, refs[2 * n:])

    return pl.pallas_call(
        body,
        in_specs=[_ANY] * n,
        out_specs=[_ANY] * n,
        out_shape=_gathered_shapes(shards),
        scratch_shapes=_gather_scratch(shards),
        compiler_params=pltpu.CompilerParams(vmem_limit_bytes=VMEM_LIMIT),
        name="gather_weights",
    )(*shards)


def _gathered_shapes(shards):
    return [jax.ShapeDtypeStruct((NCHIP,) + s.shape, s.dtype) for s in shards]


def _gather_scratch(shards):
    n = len(shards)
    return ([pltpu.SemaphoreType.DMA((3, n))] * 4 + [pltpu.SemaphoreType.DMA((2, n))]
            + [pltpu.VMEM(s.shape, s.dtype) for s in shards])


def _own_to_stage(srcs, scratch):
    own_sems, stages = scratch[4], scratch[5:]
    return [pltpu.make_async_copy(srcs[a], stages[a], own_sems.at[0, a]) for a in range(len(srcs))]


def _own_to_slot(dsts, scratch):
    own_sems, stages = scratch[4], scratch[5:]
    me = _place()[3]
    return [pltpu.make_async_copy(stages[a], dsts[a].at[me], own_sems.at[1, a]) for a in range(len(dsts))]


def _gather_copies(srcs, dsts, by_columns, sems, sends_only):
    n = len(srcs)
    send_sems, recv_sems, pass_send, pass_recv = sems[:4]
    x, y, c, me, others = _place()
    sibling = (x, y, 1 - c)

    def src(a, e):
        return _half(srcs[a], e, by_columns[a])

    def dst(a, j, e):
        return _half(dsts[a].at[j], e, by_columns[a])

    sends, arrivals, passes, passed = [], [], [], []
    for k, (cx, cy) in enumerate(others):
        j = 2 * cx + cy
        for a in range(n):
            sends.append(_remote(src(a, c), dst(a, me, c), send_sems.at[k, a], recv_sems.at[k, a], (cx, cy, c)))
            if not sends_only:
                arrivals.append(_remote(src(a, c), dst(a, j, c), send_sems.at[k, a], recv_sems.at[k, a], (cx, cy, c)))
                passes.append(_remote(dst(a, j, c), dst(a, j, c), pass_send.at[k, a], pass_recv.at[k, a], sibling))
                passed.append(_remote(src(a, c), dst(a, j, 1 - c), pass_send.at[k, a], pass_recv.at[k, a], sibling))
    return sends, arrivals, passes, passed


def _gather_start(srcs, dsts, by_columns, scratch):
    for cp in _gather_copies(srcs, dsts, by_columns, scratch, sends_only=True)[0]:
        cp.start()
    for cp in _own_to_stage(srcs, scratch):
        cp.start()


def _gather_finish(srcs, dsts, by_columns, scratch):
    for cp in _own_to_stage(srcs, scratch):
        cp.wait()
    own = _own_to_slot(dsts, scratch)
    for cp in own:
        cp.start()
    sends, arrivals, passes, passed = _gather_copies(srcs, dsts, by_columns, scratch, sends_only=False)
    for arrival, cp in zip(arrivals, passes):
        arrival.wait_recv()
        cp.start()
    for arrival in passed:
        arrival.wait_recv()
    for cp in sends + passes:
        cp.wait_send()
    for cp in own:
        cp.wait()


HALF_FIRST, CHIP_FIRST, BY_COLUMNS = "half_first", "chip_first", "by_columns"


def _sibling_halves(bufs, kinds):
    n = len(bufs)

    def body(*refs):
        cps = _halves_copies(refs[:n], refs[n:2 * n], kinds, refs[2 * n:])
        for cp in cps:
            cp.start()
        for cp in cps:
            cp.wait()

    return pl.pallas_call(
        body,
        in_specs=[_ANY] * n,
        out_specs=[_ANY] * n,
        out_shape=_halves_shapes(bufs, kinds),
        scratch_shapes=_halves_sems(bufs),
        name="sibling_halves",
    )(*bufs)


def _halves_shapes(bufs, kinds):
    def landed(b, kind):
        if kind == HALF_FIRST:
            return b.shape[1:]
        if kind == CHIP_FIRST:
            return (b.shape[0],) + b.shape[2:]
        return b.shape[:2] + (b.shape[2] // 2,)

    return [jax.ShapeDtypeStruct(landed(b, kind), b.dtype) for b, kind in zip(bufs, kinds)]


def _halves_sems(bufs):
    return [pltpu.SemaphoreType.DMA((len(bufs), NCHIP))] * 2


def _halves_copies(srcs, dsts, kinds, sems):
    send_sems, recv_sems = sems
    x, y, c, _, _ = _place()
    cps = []
    for a, kind in enumerate(kinds):
        if kind == HALF_FIRST:
            cps.append(_remote(srcs[a].at[1 - c], dsts[a], send_sems.at[a, 0], recv_sems.at[a, 0], (x, y, 1 - c)))
        else:
            cps += [_remote(_half(srcs[a].at[j], 1 - c, kind == BY_COLUMNS), dsts[a].at[j],
                            send_sems.at[a, j], recv_sems.at[a, j], (x, y, 1 - c)) for j in range(srcs[a].shape[0])]
    return cps


TO_ITS_CHIP, TO_EVERY_CHIP, ROWS_TO_ITS_CHIP = "to_its_chip", "to_every_chip", "rows_to_its_chip"
PIECE_STEP = 2048
PIECE_ROWS = 2064


def _chip_exchange(parts, sends_what):
    n = len(parts)

    def body(*refs):
        cps = _exchange_copies(refs[:n], refs[n:2 * n], sends_what, refs[2 * n:])
        for cp in cps:
            cp.start()
        for cp in cps:
            cp.wait()

    return pl.pallas_call(
        body,
        in_specs=[_ANY] * n,
        out_specs=[_ANY] * n,
        out_shape=_exchange_shapes(parts, sends_what),
        scratch_shapes=_exchange_sems(n),
        name="chip_exchange",
    )(*parts)


def _exchange_shapes(parts, sends_what):
    def landed(p, what):
        return (3, PIECE_ROWS, p.shape[1]) if what == ROWS_TO_ITS_CHIP else (3,) + p.shape[1:]

    return [jax.ShapeDtypeStruct(landed(p, what), p.dtype) for p, what in zip(parts, sends_what)]


def _exchange_sems(n):
    return [pltpu.SemaphoreType.DMA((3, n))] * 2


def _exchange_copies(srcs, dsts, sends_what, sems):
    send_sems, recv_sems = sems
    x, y, c, me, others = _place()

    def part(a, j):
        if sends_what[a] == ROWS_TO_ITS_CHIP:
            return srcs[a].at[pl.ds(pl.multiple_of(j * PIECE_STEP, PIECE_STEP), PIECE_ROWS)]
        return srcs[a].at[j if sends_what[a] == TO_ITS_CHIP else 0]

    return [_remote(part(a, 2 * cx + cy), dsts[a].at[k], send_sems.at[k, a], recv_sems.at[k, a], (cx, cy, c))
            for k, (cx, cy) in enumerate(others) for a in range(len(srcs))]


def _sibling_swap(halves):
    n = len(halves)

    def body(*refs):
        srcs, dsts = refs[:n], refs[n:2 * n]
        send_sems, recv_sems = refs[2 * n:]
        x, y, c, _, _ = _place()
        cps = [_remote(srcs[a], dsts[a], send_sems.at[a], recv_sems.at[a], (x, y, 1 - c)) for a in range(n)]
        for cp in cps:
            cp.start()
        for cp in cps:
            cp.wait()

    return pl.pallas_call(
        body,
        in_specs=[_ANY] * n,
        out_specs=[_ANY] * n,
        out_shape=[jax.ShapeDtypeStruct(s.shape, s.dtype) for s in halves],
        scratch_shapes=[pltpu.SemaphoreType.DMA((n,))] * 2,
        name="sibling_swap",
    )(*halves)


def _tile(h, w, operands=5):
    if h % 128 == 0:
        return 128, w
    budget = VMEM_LIMIT * 3 // 4 // (2 * operands * 4)
    tw = w
    while h * tw > budget and tw % 256 == 0:
        tw //= 2
    return h, tw


def _pair_sum(place, buf, kind, got, out_dtype):
    nj, h, w = got.shape
    th, tw = _tile(h, w)
    nq = w // tw

    def body(p_ref, a_ref, b_ref, o_ref):
        del p_ref
        o_ref[...] = (a_ref[...] + b_ref[...]).astype(out_dtype)

    if kind == HALF_FIRST:
        mine = pl.BlockSpec((None, None, th, tw), lambda j, r, q, p: (p[0], j, r, q))
    elif kind == CHIP_FIRST:
        mine = pl.BlockSpec((None, None, th, tw), lambda j, r, q, p: (j, p[0], r, q))
    else:
        mine = pl.BlockSpec((None, th, tw), lambda j, r, q, p: (j, r, p[0] * nq + q))
    return pl.pallas_call(
        body,
        grid_spec=pltpu.PrefetchScalarGridSpec(
            num_scalar_prefetch=1,
            grid=(nj, h // th, w // tw),
            in_specs=[mine, pl.BlockSpec((None, th, tw), lambda j, r, q, p: (j, r, q))],
            out_specs=pl.BlockSpec((None, th, tw), lambda j, r, q, p: (j, r, q)),
        ),
        out_shape=jax.ShapeDtypeStruct((nj, h, w), out_dtype),
        compiler_params=_params(("parallel", "parallel", "parallel")),
        name="pair_sum",
    )(place, buf, got)


def _chip_sum(place, part, slots):
    nj, h, w = part.shape
    th, tw = _tile(h, w)

    def body(p_ref, own_ref, s_ref, o_ref):
        me = p_ref[1]
        own = own_ref[...].astype(f32)
        by_flip = {2: s_ref[0].astype(f32), 1: s_ref[1].astype(f32), 3: s_ref[2].astype(f32)}
        acc = None
        for j in range(NCHIP):
            flip = me ^ j
            term = jnp.where(flip == 0, own, jnp.where(flip == 2, by_flip[2], jnp.where(flip == 1, by_flip[1], by_flip[3])))
            acc = term if acc is None else acc + term
        o_ref[...] = acc

    return pl.pallas_call(
        body,
        grid_spec=pltpu.PrefetchScalarGridSpec(
            num_scalar_prefetch=1,
            grid=(h // th, w // tw),
            in_specs=[pl.BlockSpec((None, th, tw), lambda r, q, p: (p[1] if nj == NCHIP else 0, r, q)),
                      pl.BlockSpec((3, th, tw), lambda r, q, p: (0, r, q))],
            out_specs=pl.BlockSpec((th, tw), lambda r, q, p: (r, q)),
        ),
        out_shape=jax.ShapeDtypeStruct((h, w), f32),
        compiler_params=_params(("parallel", "parallel")),
        name="chip_sum",
    )(place, part, slots)


def _adamw_math(w, g, m, v):
    nm = ADAM_B1 * m + (1.0 - ADAM_B1) * g
    nv = ADAM_B2 * v + (1.0 - ADAM_B2) * (g * g)
    m_hat = nm / (1.0 - ADAM_B1 ** ADAM_STEP)
    v_hat = nv / (1.0 - ADAM_B2 ** ADAM_STEP)
    return -ADAM_LR * (m_hat / (jnp.sqrt(v_hat) + ADAM_EPS) + ADAM_WD * w), nm, nv


def _adamw(w, g, m, v):
    rows, width = w.shape
    th, tw = _tile(rows, width, operands=7)

    def body(w_ref, g_ref, m_ref, v_ref, d_ref, nm_ref, nv_ref):
        d_ref[...], nm_ref[...], nv_ref[...] = _adamw_math(w_ref[...], g_ref[...], m_ref[...], v_ref[...])

    spec = pl.BlockSpec((th, tw), lambda r, q: (r, q))
    return pl.pallas_call(
        body,
        grid=(rows // th, width // tw),
        in_specs=[spec] * 4,
        out_specs=[spec] * 3,
        out_shape=[jax.ShapeDtypeStruct((rows, width), f32)] * 3,
        compiler_params=_params(("parallel", "parallel")),
        name="adamw",
    )(w, g, m, v)


def _adamw_halves(place, w, mine, got, m, v, axis):
    rows, width = w.shape
    h, hw = mine.shape
    th, tw = _tile(h, hw, operands=10)
    nr, nq = h // th, hw // tw

    def body(p_ref, w_ref, a_ref, b_ref, m_ref, v_ref, g_ref, d_ref, nm_ref, nv_ref):
        g = jnp.where(pl.program_id(0) == p_ref[0], a_ref[...], b_ref[...])
        g_ref[...] = g
        d_ref[...], nm_ref[...], nv_ref[...] = _adamw_math(w_ref[...], g, m_ref[...], v_ref[...])

    if axis == 0:
        full = pl.BlockSpec((th, tw), lambda e, r, q, p: (e * nr + r, q))
    else:
        full = pl.BlockSpec((th, tw), lambda e, r, q, p: (r, e * nq + q))
    half = pl.BlockSpec((th, tw), lambda e, r, q, p: (r, q))
    return pl.pallas_call(
        body,
        grid_spec=pltpu.PrefetchScalarGridSpec(
            num_scalar_prefetch=1,
            grid=(2, nr, nq),
            in_specs=[full, half, half, full, full],
            out_specs=[full] * 4,
        ),
        out_shape=[jax.ShapeDtypeStruct((rows, width), f32)] * 4,
        compiler_params=_params(("parallel", "parallel", "parallel")),
        name="adamw_halves",
    )(place, w, mine, got, m, v)


def _pair_sum_columns(place, buf, got, out_dtype, th=432):
    rows, hw = got.shape

    def body(p_ref, a_ref, b_ref, o_ref):
        del p_ref
        o_ref[...] = (a_ref[...] + b_ref[...]).astype(out_dtype)

    return pl.pallas_call(
        body,
        grid_spec=pltpu.PrefetchScalarGridSpec(
            num_scalar_prefetch=1,
            grid=(rows // th,),
            in_specs=[pl.BlockSpec((th, hw), lambda r, p: (r, p[0])), pl.BlockSpec((th, hw), lambda r, p: (r, 0))],
            out_specs=pl.BlockSpec((th, hw), lambda r, p: (r, 0)),
        ),
        out_shape=jax.ShapeDtypeStruct((rows, hw), out_dtype),
        compiler_params=_params(("parallel",)),
        name="pair_sum_columns",
    )(place, buf, got)


_SMALL = (("norm_g", 8), ("ln_v_g", 8), ("ln_v_b", 8), ("w_spatial", 1024), ("b_spatial", 8), ("b_gate_up", 4),
          ("gla_norm_g", 2), ("final_norm_g", 8), ("w_gate_up", 64), ("loss", 8))
_SMALL_ROWS = 1152


def _pack_rows(arrays, rows):
    flat = jnp.concatenate([a.reshape(-1, 128) for a in arrays], axis=0)
    return jnp.pad(flat, ((0, rows - flat.shape[0]), (0, 0)))


def kernel(x, norm_g, w_in, ln_v_g, ln_v_b, w_spatial, b_spatial, w_gate_up, b_gate_up, gla_norm_g, w_branch_a, w_branch_b, w_out, final_norm_g, loss_target, m_norm_g, m_w_in, m_ln_v_g, m_ln_v_b, m_w_spatial, m_b_spatial, m_w_gate_up, m_b_gate_up, m_gla_norm_g, m_w_branch_a, m_w_branch_b, m_w_out, m_final_norm_g, v_norm_g, v_w_in, v_ln_v_g, v_ln_v_b, v_w_spatial, v_b_spatial, v_w_gate_up, v_b_gate_up, v_gla_norm_g, v_w_branch_a, v_w_branch_b, v_w_out, v_final_norm_g):
    chip = 2 * lax.axis_index("x") + lax.axis_index("y")
    core = lax.axis_index("c")
    place = jnp.stack([core, chip]).astype(jnp.int32)
    mat_names = ("w_branch_a", "w_branch_b", "w_out")

    wt_shard = jnp.transpose(w_in[0]).astype(bf16)
    mats = [w[0].astype(bf16).reshape(2, D // NCHIP // 2, D) for w in (w_branch_a, w_branch_b, w_out)]
    gate_sh = w_gate_up[0].reshape(2, RANK // 2, 128)
    (g_win,) = _gather_weights([wt_shard], [True])
    w_t = g_win.reshape(NCHIP * WIN_SHARD, D)
    b_sb = jnp.broadcast_to(b_spatial[0][:, :, None], (HA, CA, GA))
    xs, tgt = x[0], loss_target[0]

    proj, lr, h, g_a, g_b, g_o, g_gate = _proj_fwd(xs, norm_g, w_t, riders=mats + [gate_sh])
    w_a, w_b, w_o = (g.reshape(D, D) for g in (g_a, g_b, g_o))
    w_gate = jnp.transpose(g_gate.reshape(NCHIP, RANK, 128), (1, 0, 2)).reshape(RANK, KEYB)
    w_gate = jnp.pad(w_gate, ((0, LRP - RANK), (0, 0)))
    a = _mixer_a_fwd(proj, ln_v_g, ln_v_b, w_spatial[0], b_sb)
    o, ob, states = _gla_fwd(proj, lr, w_gate, b_gate_up, gla_norm_g)
    dproj, dy, da, dob, dwa, dwb, dwo, dgf, loss_cols = _merge_fwd_bwd(xs, tgt, proj, a, ob, w_a, w_b, w_o,
                                                                       final_norm_g.reshape(1, D))
    b_mats = [t.reshape(NCHIP, 2, D // NCHIP // 2, D) for t in (dwa, dwb, dwo)]
    dproj, dws, dbs, dlg, dlb, *got_mats = _mixer_a_bwd(proj, da, dproj, ln_v_g, ln_v_b, w_spatial[0], b_sb,
                                                        riders=b_mats, kinds=[CHIP_FIRST] * 3)
    part_mats = [_pair_sum(place, b, CHIP_FIRST, g, bf16) for b, g in zip(b_mats, got_mats)]
    dproj, dlr, dwg, dbg, dgg, *slots_mats = _gla_bwd(proj, lr, o, states, dob, dproj, w_gate, b_gate_up, gla_norm_g,
                                                      riders=part_mats, sends_what=[TO_ITS_CHIP] * 3)
    dwt, got_win = _dw_in(h, dproj, dlr, swap=True)
    part_win = _pair_sum_columns(place, dwt, got_win, bf16)
    dx, dg0, slots_win = _dx_bwd(xs, dy, dproj, dlr, norm_g, w_t, riders=[part_win], sends_what=[ROWS_TO_ITS_CHIP])
    slots_big = [slots_win] + slots_mats
    small = _pack_rows([dg0, dlg, dlb, dws, dbs[:, :, 0], dbg, dgg, dgf, dwg[:RANK], loss_cols], _SMALL_ROWS)
    b_small = small.reshape(2, 1, _SMALL_ROWS // 2, 128)
    (got_small,) = _sibling_halves([b_small], [HALF_FIRST])
    part_small = _pair_sum(place, b_small, HALF_FIRST, got_small, f32)
    (slots_small,) = _chip_exchange([part_small], [TO_EVERY_CHIP])
    own_win = lax.dynamic_slice_in_dim(part_win, chip * PIECE_STEP, PIECE_ROWS, axis=0)[None]
    mine = [_chip_sum(place, p, s) for p, s in zip([own_win] + part_mats + [part_small], slots_big + [slots_small])]
    theirs = list(_sibling_swap(mine))
    mine[0], theirs[0] = (lax.dynamic_slice_in_dim(t, (WIN_SHARD - PIECE_STEP) * chip, WIN_SHARD, axis=0)
                          for t in (mine[0], theirs[0]))

    g_small = jnp.where(core == 0, jnp.concatenate([mine[4], theirs[4]], axis=0),
                        jnp.concatenate([theirs[4], mine[4]], axis=0))
    grads = {}
    row = 0
    for name, rows in _SMALL:
        grads[name] = g_small[row:row + rows]
        row += rows
    loss = jnp.sum(grads["loss"])
    dwg_full = grads["w_gate_up"].reshape(RANK, KEYB)
    grads["w_gate_up"] = lax.dynamic_slice_in_dim(dwg_full, chip * 128, 128, axis=1)

    weights = dict(norm_g=norm_g, w_in=w_in, ln_v_g=ln_v_g, ln_v_b=ln_v_b, w_spatial=w_spatial, b_spatial=b_spatial,
                   w_gate_up=w_gate_up, b_gate_up=b_gate_up, gla_norm_g=gla_norm_g, w_branch_a=w_branch_a,
                   w_branch_b=w_branch_b, w_out=w_out, final_norm_g=final_norm_g)
    m_in = dict(norm_g=m_norm_g, w_in=m_w_in, ln_v_g=m_ln_v_g, ln_v_b=m_ln_v_b, w_spatial=m_w_spatial,
                b_spatial=m_b_spatial, w_gate_up=m_w_gate_up, b_gate_up=m_b_gate_up, gla_norm_g=m_gla_norm_g,
                w_branch_a=m_w_branch_a, w_branch_b=m_w_branch_b, w_out=m_w_out, final_norm_g=m_final_norm_g)
    v_in = dict(norm_g=v_norm_g, w_in=v_w_in, ln_v_g=v_ln_v_g, ln_v_b=v_ln_v_b, w_spatial=v_w_spatial,
                b_spatial=v_b_spatial, w_gate_up=v_w_gate_up, b_gate_up=v_b_gate_up, gla_norm_g=v_gla_norm_g,
                w_branch_a=v_w_branch_a, w_branch_b=v_w_branch_b, w_out=v_w_out, final_norm_g=v_final_norm_g)
    names = list(weights)
    small_names = [n for n in names if n != "w_in" and n not in mat_names]
    out_g, out_d, out_m, out_v = {}, {}, {}, {}
    res = _adamw_halves(place, jnp.transpose(w_in[0]), mine[0], theirs[0], jnp.transpose(m_w_in[0]),
                        jnp.transpose(v_w_in[0]), axis=1)
    out_g["w_in"], out_d["w_in"], out_m["w_in"], out_v["w_in"] = (jnp.transpose(t)[None] for t in res)
    for i, n in enumerate(mat_names):
        res = _adamw_halves(place, weights[n][0], mine[1 + i], theirs[1 + i], m_in[n][0], v_in[n][0], axis=0)
        out_g[n], out_d[n], out_m[n], out_v[n] = (t[None] for t in res)
    upd_rows = sum(weights[n].size for n in small_names) // 128
    pad_rows = -(-upd_rows // 8) * 8
    packed = [_pack_rows([t[n] for n in small_names], pad_rows) for t in (weights, grads, m_in, v_in)]
    d_s, m_s, v_s = _adamw(*packed)
    row = 0
    for n in small_names:
        shape = weights[n].shape
        rows = weights[n].size // 128
        out_g[n] = grads[n].reshape(shape)
        out_d[n], out_m[n], out_v[n] = (t[row:row + rows].reshape(shape) for t in (d_s, m_s, v_s))
        row += rows
    return (loss, dx[None], *[out_g[n] for n in names], *[out_d[n] for n in names],
            *[out_m[n] for n in names], *[out_v[n] for n in names])
```

```python
import functools
import math

import jax
import jax.numpy as jnp
from jax import lax
from jax.experimental import pallas as pl
from jax.experimental.pallas import tpu as pltpu

f32 = jnp.float32
bf16 = jnp.bfloat16

D = 1024
NMAIN = 8192
LRP = 128
RANK = 16
HA, GA, CA = 8, 128, 128
HB, DK, DV, CB = 4, 128, 256, 64
KEYB = HB * DK
EPS = 1e-6
LN_EPS = 1e-5
GATE_NORM = 16.0
QSCALE = DK ** -0.5
COL_U, COL_V, COL_ZA = 0, 1, 2
COL_Q, COL_K = 6, 7
COL_VB, COL_ZB = 4, 5
COL_GATES = 3
VMEM_LIMIT = 56 * 1024 * 1024
NCHIP = 4
WIN_SHARD = 2052
LR_COL = 6144
_ANY = pl.BlockSpec(memory_space=pl.ANY)

ADAM_LR, ADAM_B1, ADAM_B2, ADAM_EPS, ADAM_WD, ADAM_STEP = 0.001, 0.9, 0.999, 1e-08, 0.01, 10

_SQRT_HALF = 0.7071067811865476
_INV_SQRT_2PI = 0.3989422804014327


def _dot(a, b):
    return jnp.dot(a, b, preferred_element_type=f32)


def _dot_nt(a, b):
    return lax.dot_general(a, b, (((1,), (1,)), ((), ())), preferred_element_type=f32)


def _dot_tn(a, b):
    return lax.dot_general(a, b, (((0,), (0,)), ((), ())), preferred_element_type=f32)


def _dot_exact(a, b):
    return jnp.dot(a, b, preferred_element_type=f32, precision=lax.Precision.HIGHEST)


def _gelu(x):
    return 0.5 * x * (1.0 + lax.erf(x * _SQRT_HALF))


def _gelu_grad(x):
    return 0.5 * (1.0 + lax.erf(x * _SQRT_HALF)) + x * (jnp.exp(-0.5 * x * x) * _INV_SQRT_2PI)


def _sigmoid(x):
    return 1.0 / (1.0 + jnp.exp(-x))


def _params(sem):
    return pltpu.CompilerParams(dimension_semantics=sem, vmem_limit_bytes=VMEM_LIMIT)


def _resident(shape):
    nd = len(shape)
    return pl.BlockSpec(shape, lambda *_: (0,) * nd, pipeline_mode=pl.Buffered(1))


def _w_rows(c, tn):
    start = c * tn + (RANK if c * tn >= LR_COL else 0)
    return slice(start, start + tn)


LR_ROWS = slice(LR_COL, LR_COL + LRP)


def _proj_fwd(x, g0, w_t, riders=(), tm=256, tn=1024):
    T = x.shape[0]
    nsteps = T // tm
    n = len(riders)

    def body(x_ref, g_ref, w_ref, *rest):
        srcs, (proj_ref, lr_ref, h_ref), dsts, sems = rest[:n], rest[n:n + 3], rest[n + 3:2 * n + 3], rest[2 * n + 3:]
        if n:
            @pl.when(pl.program_id(0) == 0)
            def _():
                _gather_start(srcs, dsts, [False] * n, sems)

        xv = x_ref[...]
        r = lax.rsqrt(jnp.mean(xv * xv, axis=-1, keepdims=True) + EPS)
        h = (xv * r * g_ref[...]).astype(bf16)
        h_ref[...] = h
        lr_ref[...] = _dot_nt(h, w_ref[LR_ROWS, :])
        for c in range(NMAIN // tn):
            proj_ref[:, c * tn:(c + 1) * tn] = _dot_nt(h, w_ref[_w_rows(c, tn), :])

        if n:
            @pl.when(pl.program_id(0) == nsteps - 1)
            def _():
                _gather_finish(srcs, dsts, [False] * n, sems)

    return pl.pallas_call(
        body,
        grid=(nsteps,),
        in_specs=[
            pl.BlockSpec((tm, D), lambda i: (i, 0)),
            _resident((1, D)), _resident((NMAIN + RANK, D)),
        ] + [_ANY] * n,
        out_specs=[
            pl.BlockSpec((tm, NMAIN), lambda i: (i, 0)),
            pl.BlockSpec((tm, LRP), lambda i: (i, 0)),
            pl.BlockSpec((tm, D), lambda i: (i, 0)),
        ] + [_ANY] * n,
        out_shape=[
            jax.ShapeDtypeStruct((T, NMAIN), f32),
            jax.ShapeDtypeStruct((T, LRP), f32),
            jax.ShapeDtypeStruct((T, D), bf16),
        ] + _gathered_shapes(riders),
        scratch_shapes=_gather_scratch(riders) if n else [],
        compiler_params=_params(("arbitrary",)),
        name="proj_fwd",
    )(x, g0, w_t, *riders)


def _causal_mask():
    t = lax.broadcasted_iota(jnp.int32, (CA, CA), 0)
    s = lax.broadcasted_iota(jnp.int32, (CA, CA), 1)
    return s <= t


def _layernorm_parts(gv):
    mu = jnp.mean(gv, axis=-1, keepdims=True)
    xc = gv - mu
    rs = lax.rsqrt(jnp.mean(xc * xc, axis=-1, keepdims=True) + LN_EPS)
    return xc * rs, rs


def _mixer_a_fwd(proj, ln_g, ln_b, w_s, b_sb, tm=256):
    T = proj.shape[0]

    def body(u_ref, v_ref, za_ref, lg_ref, lb_ref, ws_ref, bs_ref, a_ref, vln_s):
        vhat, _ = _layernorm_parts(_gelu(v_ref[...]))
        vln_s[...] = (vhat * lg_ref[...] + lb_ref[...]).astype(bf16)
        mask = _causal_mask()
        for g in range(HA):
            wg = jnp.where(mask, ws_ref[g], 0.0).astype(bf16)
            cols = slice(g * GA, (g + 1) * GA)
            for c in range(tm // CA):
                rows = slice(c * CA, (c + 1) * CA)
                mixed = _dot(wg, vln_s[rows, cols]) + bs_ref[g]
                za = za_ref[rows, cols]
                a = _gelu(u_ref[rows, cols]) * mixed * (za * _sigmoid(za))
                a_ref[rows, cols] = a.astype(bf16)

    def col(cidx):
        return pl.BlockSpec((tm, D), lambda i, c=cidx: (i, c))

    return pl.pallas_call(
        body,
        grid=(T // tm,),
        in_specs=[col(COL_U), col(COL_V), col(COL_ZA), _resident((1, D)), _resident((1, D)),
                  _resident((HA, CA, CA)), _resident((HA, CA, GA))],
        out_specs=pl.BlockSpec((tm, D), lambda i: (i, 0)),
        out_shape=jax.ShapeDtypeStruct((T, D), bf16),
        scratch_shapes=[pltpu.VMEM((tm, D), bf16)],
        compiler_params=_params(("parallel",)),
        name="mixer_a_fwd",
    )(proj, proj, proj, ln_g, ln_b, w_s, b_sb)


def _mixer_a_bwd(proj, da, dproj, ln_g, ln_b, w_s, b_sb, riders=(), kinds=(), tm=256):
    T = proj.shape[0]
    nsteps = T // tm
    n = len(riders)

    def body(u_ref, v_ref, za_ref, da_ref, dp_in, lg_ref, lb_ref, ws_ref, bs_ref, *rest):
        srcs, (dp_ref, dws_ref, dbs_ref, dlg_ref, dlb_ref), dsts = rest[:n], rest[n:n + 5], rest[n + 5:2 * n + 5]
        vln_s, dvln_s, *sems = rest[2 * n + 5:]
        del dp_in
        i = pl.program_id(0)

        @pl.when(i == 0)
        def _():
            dws_ref[...] = jnp.zeros_like(dws_ref)
            dbs_ref[...] = jnp.zeros_like(dbs_ref)
            dlg_ref[...] = jnp.zeros_like(dlg_ref)
            dlb_ref[...] = jnp.zeros_like(dlb_ref)
            for cp in _halves_copies(srcs, dsts, kinds, sems) if n else []:
                cp.start()

        v = v_ref[...]
        vhat, rs = _layernorm_parts(_gelu(v))
        vln_s[...] = (vhat * lg_ref[...] + lb_ref[...]).astype(bf16)
        mask = _causal_mask()
        for g in range(HA):
            wg = jnp.where(mask, ws_ref[g], 0.0).astype(bf16)
            cols = slice(g * GA, (g + 1) * GA)
            dw_acc = jnp.zeros((CA, CA), f32)
            db_acc = jnp.zeros((CA, 1), f32)
            for c in range(tm // CA):
                rows = slice(c * CA, (c + 1) * CA)
                vln = vln_s[rows, cols]
                mixed = _dot(wg, vln) + bs_ref[g]
                u = u_ref[rows, cols]
                za = za_ref[rows, cols]
                da_blk = da_ref[rows, cols]
                sg = _sigmoid(za)
                sz = za * sg
                gu = _gelu(u)
                dp_ref[rows, cols] = (da_blk * mixed * sz * _gelu_grad(u)).astype(bf16)
                dp_ref[rows, 2 * D + g * GA:2 * D + (g + 1) * GA] = (
                    da_blk * gu * mixed * (sg * (1.0 + za * (1.0 - sg)))).astype(bf16)
                dmixed = da_blk * gu * sz
                dmb = dmixed.astype(bf16)
                dvln_s[rows, cols] = _dot_tn(wg, dmb)
                dw_acc = dw_acc + _dot_nt(dmb, vln)
                db_acc = db_acc + jnp.sum(dmixed, axis=-1, keepdims=True)
            dws_ref[g] += dw_acc
            dbs_ref[g] += jnp.broadcast_to(db_acc, (CA, GA))

        dvln = dvln_s[...]
        dlg_ref[...] += jnp.sum(dvln * vhat, axis=0, keepdims=True)
        dlb_ref[...] += jnp.sum(dvln, axis=0, keepdims=True)
        dvhat = dvln * lg_ref[...]
        dgv = rs * (dvhat - jnp.mean(dvhat, axis=-1, keepdims=True)
                    - vhat * jnp.mean(dvhat * vhat, axis=-1, keepdims=True))
        dp_ref[:, D:2 * D] = (dgv * _gelu_grad(v)).astype(bf16)

        @pl.when(i == nsteps - 1)
        def _():
            for g in range(HA):
                dws_ref[g] = jnp.where(mask, dws_ref[g], 0.0)
            for cp in _halves_copies(srcs, dsts, kinds, sems) if n else []:
                cp.wait()

    def col(cidx):
        return pl.BlockSpec((tm, D), lambda i, c=cidx: (i, c))

    return pl.pallas_call(
        body,
        grid=(nsteps,),
        in_specs=[col(COL_U), col(COL_V), col(COL_ZA), pl.BlockSpec((tm, D), lambda i: (i, 0)),
                  pl.BlockSpec(memory_space=pl.ANY),
                  _resident((1, D)), _resident((1, D)), _resident((HA, CA, CA)), _resident((HA, CA, GA))] + [_ANY] * n,
        out_specs=[pl.BlockSpec((tm, 3 * D), lambda i: (i, 0)),
                   _resident((HA, CA, CA)), _resident((HA, CA, GA)), _resident((1, D)), _resident((1, D))] + [_ANY] * n,
        out_shape=[jax.ShapeDtypeStruct(dproj.shape, dproj.dtype),
                   jax.ShapeDtypeStruct((HA, CA, CA), f32), jax.ShapeDtypeStruct((HA, CA, GA), f32),
                   jax.ShapeDtypeStruct((1, D), f32), jax.ShapeDtypeStruct((1, D), f32)] + _halves_shapes(riders, kinds),
        scratch_shapes=[pltpu.VMEM((tm, D), bf16), pltpu.VMEM((tm, D), f32)] + (_halves_sems(riders) if n else []),
        input_output_aliases={4: 0},
        compiler_params=_params(("arbitrary",)),
        name="mixer_a_bwd",
    )(proj, proj, proj, da, dproj, ln_g, ln_b, w_s, b_sb, *riders)


def _tri(n, upper):
    r = lax.broadcasted_iota(jnp.int32, (n, n), 0)
    c = lax.broadcasted_iota(jnp.int32, (n, n), 1)
    return jnp.where((c >= r) if upper else (c <= r), 1.0, 0.0).astype(f32)


def _chunk_tri(n, upper):
    r = lax.broadcasted_iota(jnp.int32, (n, n), 0)
    c = lax.broadcasted_iota(jnp.int32, (n, n), 1)
    shift = CB.bit_length() - 1
    same_chunk = jnp.right_shift(r, shift) == jnp.right_shift(c, shift)
    return jnp.where(same_chunk & ((c >= r) if upper else (c <= r)), 1.0, 0.0).astype(f32)


def _log_alpha(lr, wg, bg):
    logit = _dot(lr.astype(bf16), wg.astype(bf16)) + bg
    la = (jnp.minimum(logit, 0.0) - jnp.log1p(jnp.exp(-jnp.abs(logit)))) * (1.0 / GATE_NORM)
    return logit, la


def _gla_fwd(proj, lr, w_gate, b_gate, gla_g, tm=256):
    T = proj.shape[0]
    nchunk = T // CB
    cpb = tm // CB

    def body(q_ref, k_ref, v_ref, zb_ref, lr_ref, wg_ref, bg_ref, gg_ref,
             o_ref, ob_ref, st_ref, state, la_s):
        @pl.when(pl.program_id(0) == 0)
        def _():
            state[...] = jnp.zeros_like(state)

        _, la = _log_alpha(lr_ref[...], wg_ref[...], bg_ref[...])
        la_s[...] = _dot_exact(_chunk_tri(tm, upper=False), la)
        causal = _tri(CB, upper=False) > 0.5
        states = [state[hd] for hd in range(HB)]
        for c in range(cpb):
            rows = slice(c * CB, (c + 1) * CB)
            b = la_s[rows, :]
            bl = b[CB - 1:CB, :]
            bm = b[CB // 2 - 1:CB // 2, :]
            q = q_ref[rows, :] * QSCALE
            k = k_ref[rows, :]
            qi_all = (q * jnp.exp(b - bm)).astype(bf16)
            ki_all = (k * jnp.exp(bm - b)).astype(bf16)
            qe_all = (q * jnp.exp(b)).astype(bf16)
            ks_all = (k * jnp.exp(bl - b)).astype(bf16)
            e_l = jnp.exp(bl)
            for hd in range(HB):
                kc = slice(hd * DK, (hd + 1) * DK)
                vc = slice(hd * DV, (hd + 1) * DV)
                v = v_ref[rows, vc].astype(bf16)
                p = jnp.where(causal, _dot_nt(qi_all[:, kc], ki_all[:, kc]), 0.0).astype(bf16)
                s0 = states[hd]
                st_ref[c, hd] = s0
                o = _dot(p, v) + _dot_nt(qe_all[:, kc], s0.astype(bf16))
                states[hd] = s0 * e_l[:, kc] + _dot_tn(v, ks_all[:, kc])
                o_ref[rows, vc] = o
                ro = lax.rsqrt(jnp.mean(o * o, axis=-1, keepdims=True) + EPS)
                zb = zb_ref[rows, vc]
                ob_ref[rows, vc] = (o * ro * gg_ref[...] * (zb * _sigmoid(zb))).astype(bf16)
        for hd in range(HB):
            state[hd] = states[hd]

    return pl.pallas_call(
        body,
        grid=(T // tm,),
        in_specs=[pl.BlockSpec((tm, KEYB), lambda i: (i, COL_Q)),
                  pl.BlockSpec((tm, KEYB), lambda i: (i, COL_K)),
                  pl.BlockSpec((tm, D), lambda i: (i, COL_VB)),
                  pl.BlockSpec((tm, D), lambda i: (i, COL_ZB)),
                  pl.BlockSpec((tm, LRP), lambda i: (i, 0)),
                  _resident((LRP, KEYB)), _resident((1, KEYB)), _resident((1, DV))],
        out_specs=[pl.BlockSpec((tm, D), lambda i: (i, 0)),
                   pl.BlockSpec((tm, D), lambda i: (i, 0)),
                   pl.BlockSpec((cpb, HB, DV, DK), lambda i: (i, 0, 0, 0))],
        out_shape=[jax.ShapeDtypeStruct((T, D), f32), jax.ShapeDtypeStruct((T, D), bf16),
                   jax.ShapeDtypeStruct((nchunk, HB, DV, DK), f32)],
        scratch_shapes=[pltpu.VMEM((HB, DV, DK), f32), pltpu.VMEM((tm, KEYB), f32)],
        compiler_params=_params(("arbitrary",)),
        name="gla_fwd",
    )(proj, proj, proj, proj, lr, w_gate, b_gate, gla_g)


def _gla_bwd(proj, lr, o, states, dob, dproj, w_gate, b_gate, gla_g, riders=(), sends_what=(), tm=256):
    T = proj.shape[0]
    cpb = tm // CB
    nb = T // tm
    n = len(riders)

    def body(q_ref, k_ref, v_ref, zb_ref, lr_ref, o_ref, st_ref, dob_ref, dp_in, wg_ref, bg_ref, gg_ref, *rest):
        srcs, (dp_ref, dlr_ref, dwg_ref, dbg_ref, dgg_ref), dsts = rest[:n], rest[n:n + 5], rest[n + 5:2 * n + 5]
        dstate, la_s, dlogit_s, tail_s, *sems = rest[2 * n + 5:]
        del dp_in
        step = pl.program_id(0)

        @pl.when(step == 0)
        def _():
            dstate[...] = jnp.zeros_like(dstate)
            dwg_ref[...] = jnp.zeros_like(dwg_ref)
            dbg_ref[...] = jnp.zeros_like(dbg_ref)
            dgg_ref[...] = jnp.zeros_like(dgg_ref)
            for cp in _exchange_copies(srcs, dsts, sends_what, sems) if n else []:
                cp.start()

        lr_v = lr_ref[...]
        logit, la = _log_alpha(lr_v, wg_ref[...], bg_ref[...])
        la_s[...] = _dot_exact(_chunk_tri(tm, upper=False), la)
        causal = _tri(CB, upper=False) > 0.5
        gg = gg_ref[...]
        dgg_acc = jnp.zeros((1, DV), f32)
        dstates = [dstate[hd] for hd in range(HB)]
        for c in reversed(range(cpb)):
            rows = slice(c * CB, (c + 1) * CB)
            b = la_s[rows, :]
            bl = b[CB - 1:CB, :]
            bm = b[CB // 2 - 1:CB // 2, :]
            eb_all, eqm_all, ekm_all = jnp.exp(b), jnp.exp(b - bm), jnp.exp(bm - b)
            eks_all, el_all = jnp.exp(bl - b), jnp.exp(bl)
            q_all = q_ref[rows, :] * QSCALE
            k_all = k_ref[rows, :]
            qi_all = (q_all * eqm_all).astype(bf16)
            ki_all = (k_all * ekm_all).astype(bf16)
            qe_all = (q_all * eb_all).astype(bf16)
            ksf_all = k_all * eks_all
            ks_all = ksf_all.astype(bf16)
            for hd in range(HB):
                kc = slice(hd * DK, (hd + 1) * DK)
                vc = slice(hd * DV, (hd + 1) * DV)
                o_h = o_ref[rows, vc]
                ro = lax.rsqrt(jnp.mean(o_h * o_h, axis=-1, keepdims=True) + EPS)
                ohat = o_h * ro
                zb = zb_ref[rows, vc]
                sg = _sigmoid(zb)
                dob_h = dob_ref[rows, vc]
                don = dob_h * (zb * sg)
                dp_ref[rows, 2 * D + hd * DV:2 * D + (hd + 1) * DV] = (
                    dob_h * ohat * gg * (sg * (1.0 + zb * (1.0 - sg)))).astype(bf16)
                dgg_acc = dgg_acc + jnp.sum(don * ohat, axis=0, keepdims=True)
                dohat = don * gg
                do = (ro * (dohat - ohat * jnp.mean(dohat * ohat, axis=-1, keepdims=True))).astype(bf16)
                e_b, e_qm, e_km, e_ks, e_l = eb_all[:, kc], eqm_all[:, kc], ekm_all[:, kc], eks_all[:, kc], el_all[:, kc]
                q, k, ks_f = q_all[:, kc], k_all[:, kc], ksf_all[:, kc]
                qi, ki, qe, ks = qi_all[:, kc], ki_all[:, kc], qe_all[:, kc], ks_all[:, kc]
                v = v_ref[rows, vc].astype(bf16)
                p = jnp.where(causal, _dot_nt(qi, ki), 0.0).astype(bf16)
                s0 = st_ref[c, hd]
                ds = dstates[hd]
                ds_b = ds.astype(bf16)
                dv = _dot_tn(p, do) + _dot_nt(ks, ds_b)
                dpm = jnp.where(causal, _dot_nt(do, v), 0.0).astype(bf16)
                dqi = _dot(dpm, ki)
                dki = _dot_tn(dpm, qi)
                dqe = _dot(do, s0.astype(bf16))
                dks = _dot(v, ds_b)
                dq_s = dqi * e_qm + dqe * e_b
                dk = dki * e_km + dks * e_ks
                tail = (jnp.sum(dks * ks_f, axis=0, keepdims=True)
                        + e_l * jnp.sum(ds * s0, axis=0, keepdims=True))
                dstates[hd] = _dot_tn(do, qe) + ds * e_l
                dp_ref[rows, kc] = (dq_s * QSCALE).astype(bf16)
                dp_ref[rows, KEYB + hd * DK:KEYB + (hd + 1) * DK] = dk.astype(bf16)
                dp_ref[rows, D + hd * DV:D + (hd + 1) * DV] = dv.astype(bf16)
                dlogit_s[rows, kc] = dq_s * q - dk * k
                tail_s[rows, kc] = jnp.broadcast_to(tail, (CB, DK))
        for hd in range(HB):
            dstate[hd] = dstates[hd]
        dgg_ref[...] += dgg_acc
        dg = _dot_exact(_chunk_tri(tm, upper=True), dlogit_s[...]) + tail_s[...]
        dlogit = dg * (1.0 / GATE_NORM) * _sigmoid(-logit)
        dbg_ref[...] += jnp.sum(dlogit, axis=0, keepdims=True)
        dlb = dlogit.astype(bf16)
        dlr_ref[...] = _dot_nt(dlb, wg_ref[...].astype(bf16)).astype(bf16)
        dwg_ref[...] += _dot_tn(lr_v.astype(bf16), dlb)

        if n:
            @pl.when(step == nb - 1)
            def _():
                for cp in _exchange_copies(srcs, dsts, sends_what, sems):
                    cp.wait()

    def rev(cidx):
        return lambda i, c=cidx: (nb - 1 - i, c)

    return pl.pallas_call(
        body,
        grid=(nb,),
        in_specs=[pl.BlockSpec((tm, KEYB), rev(COL_Q)),
                  pl.BlockSpec((tm, KEYB), rev(COL_K)),
                  pl.BlockSpec((tm, D), rev(COL_VB)),
                  pl.BlockSpec((tm, D), rev(COL_ZB)),
                  pl.BlockSpec((tm, LRP), rev(0)),
                  pl.BlockSpec((tm, D), rev(0)),
                  pl.BlockSpec((cpb, HB, DV, DK), lambda i: (nb - 1 - i, 0, 0, 0)),
                  pl.BlockSpec((tm, D), rev(0)),
                  pl.BlockSpec(memory_space=pl.ANY),
                  _resident((LRP, KEYB)), _resident((1, KEYB)), _resident((1, DV))] + [_ANY] * n,
        out_specs=[pl.BlockSpec((tm, 3 * D), rev(1)),
                   pl.BlockSpec((tm, LRP), rev(0)),
                   _resident((LRP, KEYB)), _resident((1, KEYB)), _resident((1, DV))] + [_ANY] * n,
        out_shape=[jax.ShapeDtypeStruct(dproj.shape, dproj.dtype),
                   jax.ShapeDtypeStruct((T, LRP), bf16),
                   jax.ShapeDtypeStruct((LRP, KEYB), f32), jax.ShapeDtypeStruct((1, KEYB), f32),
                   jax.ShapeDtypeStruct((1, DV), f32)] + _exchange_shapes(riders, sends_what),
        scratch_shapes=[pltpu.VMEM((HB, DV, DK), f32)] + [pltpu.VMEM((tm, KEYB), f32)] * 3
        + (_exchange_sems(n) if n else []),
        input_output_aliases={8: 0},
        compiler_params=_params(("arbitrary",)),
        name="gla_bwd",
    )(proj, proj, proj, proj, lr, o, states, dob, dproj, w_gate, b_gate, gla_g, *riders)


def _merge_fwd_bwd(x, tgt, proj, a, ob, w_a, w_b, w_o, g_f, tm=256):
    T = x.shape[0]

    def body(x_ref, t_ref, gt_ref, a_ref, ob_ref, wa_ref, wb_ref, wo_ref, gf_ref,
             dp_ref, dy_ref, da_ref, dob_ref, dwa_ref, dwb_ref, dwo_ref, dgf_ref, loss_ref):
        @pl.when(pl.program_id(0) == 0)
        def _():
            dwa_ref[...] = jnp.zeros_like(dwa_ref)
            dwb_ref[...] = jnp.zeros_like(dwb_ref)
            dwo_ref[...] = jnp.zeros_like(dwo_ref)
            dgf_ref[...] = jnp.zeros_like(dgf_ref)
            loss_ref[...] = jnp.zeros_like(loss_ref)

        ga = _sigmoid(gt_ref[:, :D])
        gb = _sigmoid(gt_ref[:, D:])
        a_v = a_ref[...]
        ob_v = ob_ref[...]
        pa = _dot(a_v, wa_ref[...])
        pb = _dot(ob_v, wb_ref[...])
        mb = (ga * pa + gb * pb).astype(bf16)
        y = x_ref[...] + _dot(mb, wo_ref[...])
        r1 = lax.rsqrt(jnp.mean(y * y, axis=-1, keepdims=True) + EPS)
        yhat = y * r1
        gf = gf_ref[...]
        err = yhat * gf - t_ref[...]
        loss_ref[...] += jnp.sum(err * err, axis=0, keepdims=True) * (0.5 / D)
        dout = err * (1.0 / D)
        dgf_ref[...] += jnp.sum(dout * yhat, axis=0, keepdims=True)
        dyn = dout * gf
        dy = r1 * (dyn - yhat * jnp.mean(dyn * yhat, axis=-1, keepdims=True))
        dy_ref[...] = dy
        dyb = dy.astype(bf16)
        dwo_ref[...] += _dot_tn(mb, dyb)
        dm = _dot_nt(dyb, wo_ref[...])
        dpa = (dm * ga).astype(bf16)
        dpb = (dm * gb).astype(bf16)
        dp_ref[:, :D] = (dm * pa * ga * (1.0 - ga)).astype(bf16)
        dp_ref[:, D:] = (dm * pb * gb * (1.0 - gb)).astype(bf16)
        dwa_ref[...] += _dot_tn(a_v, dpa)
        dwb_ref[...] += _dot_tn(ob_v, dpb)
        da_ref[...] = _dot_nt(dpa, wa_ref[...])
        dob_ref[...] = _dot_nt(dpb, wb_ref[...])

    row = lambda: pl.BlockSpec((tm, D), lambda i: (i, 0))
    return pl.pallas_call(
        body,
        grid=(T // tm,),
        in_specs=[row(), row(), pl.BlockSpec((tm, 2 * D), lambda i: (i, COL_GATES)), row(), row(),
                  _resident((D, D)), _resident((D, D)), _resident((D, D)), _resident((1, D))],
        out_specs=[pl.BlockSpec((tm, 2 * D), lambda i: (i, COL_GATES)), row(), row(), row(),
                   _resident((D, D)), _resident((D, D)), _resident((D, D)), _resident((1, D)), _resident((1, D))],
        out_shape=[jax.ShapeDtypeStruct((T, NMAIN), bf16),
                   jax.ShapeDtypeStruct((T, D), f32), jax.ShapeDtypeStruct((T, D), f32),
                   jax.ShapeDtypeStruct((T, D), f32),
                   jax.ShapeDtypeStruct((D, D), f32), jax.ShapeDtypeStruct((D, D), f32),
                   jax.ShapeDtypeStruct((D, D), f32),
                   jax.ShapeDtypeStruct((1, D), f32), jax.ShapeDtypeStruct((1, D), f32)],
        compiler_params=_params(("arbitrary",)),
        name="merge_fwd_bwd",
    )(x, tgt, proj, a, ob, w_a, w_b, w_o, g_f)


def _dx_bwd(x, dy, dproj, dlr, g0, w_t, riders=(), sends_what=(), tm=256):
    T = x.shape[0]
    nsteps = T // tm
    n = len(riders)

    def body(x_ref, dy_ref, dp_ref, dl_ref, g_ref, w_ref, *rest):
        srcs, (dx_ref, dg_ref), dsts, sems = rest[:n], rest[n:n + 2], rest[n + 2:2 * n + 2], rest[2 * n + 2:]

        @pl.when(pl.program_id(0) == 0)
        def _():
            dg_ref[...] = jnp.zeros_like(dg_ref)
            for cp in _exchange_copies(srcs, dsts, sends_what, sems) if n else []:
                cp.start()

        xv = x_ref[...]
        r = lax.rsqrt(jnp.mean(xv * xv, axis=-1, keepdims=True) + EPS)
        xhat = xv * r
        dh = (_dot(dp_ref[:, :LR_COL], w_ref[:LR_COL, :]) + _dot(dp_ref[:, LR_COL:], w_ref[LR_COL + RANK:, :])
              + _dot(dl_ref[...], w_ref[LR_ROWS, :]))
        dg_ref[...] += jnp.sum(dh * xhat, axis=0, keepdims=True)
        t = dh * g_ref[...]
        dx_ref[...] = dy_ref[...] + r * (t - xhat * jnp.mean(t * xhat, axis=-1, keepdims=True))

        if n:
            @pl.when(pl.program_id(0) == nsteps - 1)
            def _():
                for cp in _exchange_copies(srcs, dsts, sends_what, sems):
                    cp.wait()

    row = lambda: pl.BlockSpec((tm, D), lambda i: (i, 0))
    return pl.pallas_call(
        body,
        grid=(nsteps,),
        in_specs=[row(), row(), pl.BlockSpec((tm, NMAIN), lambda i: (i, 0)),
                  pl.BlockSpec((tm, LRP), lambda i: (i, 0)),
                  _resident((1, D)), _resident((NMAIN + RANK, D))] + [_ANY] * n,
        out_specs=[row(), _resident((1, D))] + [_ANY] * n,
        out_shape=[jax.ShapeDtypeStruct((T, D), f32), jax.ShapeDtypeStruct((1, D), f32)]
        + _exchange_shapes(riders, sends_what),
        scratch_shapes=_exchange_sems(n) if n else [],
        compiler_params=_params(("arbitrary",)),
        name="dx_bwd",
    )(x, dy, dproj, dlr, g0, w_t, *riders)


def _dw_in(h, dproj, dlr, swap=False, riders=(), kinds=(), tm=1024, tn=1024):
    T = h.shape[0]
    tm = min(tm, T)
    nj, nk = NMAIN // tn, T // tm
    lr_tile = LR_COL // tn
    n = len(riders)
    hd = D // 2

    def body(h_ref, dp_ref, dl_ref, *rest):
        srcs, rest = rest[:n], rest[n:]
        out_ref, rest = rest[0], rest[1:]
        if swap:
            got_ref, dsts, rest = rest[0], rest[1:1 + n], rest[1 + n:]
            acc, lr_acc, sems, lr_sem, swap_send, swap_recv = rest[:6]
            rider_sems = rest[6:]
        else:
            acc, lr_acc, sems, lr_sem = rest
        j, k = pl.program_id(0), pl.program_id(1)
        slot = j % 2

        def tile_row(jj):
            return pl.multiple_of(jj * tn + jnp.where(jj >= lr_tile, RANK, 0), 8)

        def tile_out(jj, s):
            return pltpu.make_async_copy(acc.at[s], out_ref.at[pl.ds(tile_row(jj), tn)], sems.at[s])

        def other_half(ref):
            c = lax.axis_index("c")
            return ref.at[:, pl.ds(pl.multiple_of((1 - c) * hd, 128), hd)]

        def tile_swap(jj, s):
            return _remote(other_half(acc.at[s]), got_ref.at[pl.ds(tile_row(jj), tn)], swap_send.at[jj], swap_recv.at[jj],
                           _sibling())

        def lr_swap():
            return _remote(other_half(lr_acc.at[pl.ds(0, RANK)]), got_ref.at[pl.ds(LR_COL, RANK)], swap_send.at[nj],
                           swap_recv.at[nj], _sibling())

        lr_rows = pltpu.make_async_copy(lr_acc.at[pl.ds(0, RANK)], out_ref.at[pl.ds(LR_COL, RANK)], lr_sem)

        @pl.when(j == 0)
        def _():
            @pl.when(k == 0)
            def _():
                lr_acc[...] = jnp.zeros_like(lr_acc)
                for cp in _halves_copies(srcs, dsts, kinds, rider_sems) if n else []:
                    cp.start()

            lr_acc[...] += _dot_tn(dl_ref[...], h_ref[...])

            @pl.when(k == nk - 1)
            def _():
                lr_rows.start()
                if swap:
                    lr_swap().start()

        @pl.when(k == 0)
        def _():
            acc[slot] = jnp.zeros((tn, D), f32)

        acc[slot] += _dot_tn(dp_ref[...], h_ref[...])

        @pl.when(k == nk - 1)
        def _():
            tile_out(j, slot).start()
            if swap:
                tile_swap(j, slot).start()

            @pl.when(j > 0)
            def _():
                tile_out(j - 1, 1 - slot).wait()
                if swap:
                    tile_swap(j - 1, 1 - slot).wait_send()

            @pl.when(j == nj - 1)
            def _():
                tile_out(j, slot).wait()
                lr_rows.wait()
                if swap:
                    tile_swap(j, slot).wait_send()
                    lr_swap().wait()
                    for jj in range(nj):
                        tile_swap(jj, 0).wait_recv()
                    for cp in _halves_copies(srcs, dsts, kinds, rider_sems) if n else []:
                        cp.wait()

    swap_outs = [jax.ShapeDtypeStruct((NMAIN + RANK, hd), f32)] + _halves_shapes(riders, kinds) if swap else []
    swap_sems = [pltpu.SemaphoreType.DMA((nj + 1,))] * 2 + (_halves_sems(riders) if n else []) if swap else []
    return pl.pallas_call(
        body,
        grid=(nj, nk),
        in_specs=[pl.BlockSpec((tm, D), lambda j, k: (k, 0)), pl.BlockSpec((tm, tn), lambda j, k: (k, j)),
                  pl.BlockSpec((tm, LRP), lambda j, k: (k, 0))] + [_ANY] * n,
        out_specs=[_ANY] * (1 + len(swap_outs)),
        out_shape=[jax.ShapeDtypeStruct((NMAIN + RANK, D), f32)] + swap_outs,
        scratch_shapes=[pltpu.VMEM((2, tn, D), f32), pltpu.VMEM((LRP, D), f32),
                        pltpu.SemaphoreType.DMA((2,)), pltpu.SemaphoreType.DMA] + swap_sems,
        compiler_params=_params(("arbitrary", "arbitrary")),
        name="dw_in",
    )(h, dproj, dlr, *riders)


MESH = pl.DeviceIdType.MESH


def _place():
    x, y, c = lax.axis_index("x"), lax.axis_index("y"), lax.axis_index("c")
    others = [(1 - x, y), (x, 1 - y), (1 - x, 1 - y)]
    return x, y, c, 2 * x + y, others


def _sibling():
    return lax.axis_index("x"), lax.axis_index("y"), 1 - lax.axis_index("c")


def _remote(src, dst, send_sem, recv_sem, to):
    return pltpu.make_async_remote_copy(src_ref=src, dst_ref=dst, send_sem=send_sem, recv_sem=recv_sem,
                                        device_id=to, device_id_type=MESH)


def _half(ref, e, by_columns):
    if not by_columns:
        return ref.at[e]
    hw = ref.shape[-1] // 2
    return ref.at[:, pl.ds(pl.multiple_of(e * hw, 128), hw)]


RELAY_ROWS = 1024


def _gather_win(shard):
    h, w = shard.shape
    part_a, part_b = pl.ds(0, RELAY_ROWS), pl.ds(RELAY_ROWS, h - RELAY_ROWS)

    def body(src, dst, send_sems, recv_sems, relay_send, relay_recv, pass_send, pass_recv, own_sems, stage):
        x, y, c, me, others = _place()
        (to_x, to_y, _), sibling = others, (x, y, 1 - c)
        j_x, j_y, j_d = (2 * cx + cy for cx, cy in others)
        cols = pl.ds(pl.multiple_of(c * (w // 2), 128), w // 2)
        theirs = pl.ds(pl.multiple_of((1 - c) * (w // 2), 128), w // 2)

        def mine(j, rows=pl.ds(0, h)):
            return dst.at[j, rows, cols]

        to_stage = pltpu.make_async_copy(src, stage, own_sems.at[0])
        to_slot = pltpu.make_async_copy(stage, dst.at[me], own_sems.at[1])
        sends = [_remote(src.at[:, cols], mine(me), send_sems.at[k], recv_sems.at[k], (*to, c))
                 for k, to in enumerate((to_x, to_y))]
        for cp in sends:
            cp.start()
        to_stage.start()
        relays = [_remote(mine(j_x, part_a), mine(j_x, part_a), relay_send.at[0], relay_recv.at[0], (*to_y, c)),
                  _remote(mine(j_y, part_b), mine(j_y, part_b), relay_send.at[1], relay_recv.at[1], (*to_x, c))]
        landed = [mine(j_x), mine(j_y), mine(j_d, part_a), mine(j_d, part_b)]
        passes = [_remote(place, place, pass_send.at[k], pass_recv.at[k], sibling) for k, place in enumerate(landed)]
        for k in range(2):
            _remote(src.at[:, cols], landed[k], send_sems.at[k], recv_sems.at[k], sibling).wait_recv()
            relays[k].start()
            passes[k].start()
        to_stage.wait()
        to_slot.start()
        for k in range(2):
            _remote(landed[2 + k], landed[2 + k], relay_send.at[k], relay_recv.at[k], sibling).wait_recv()
            passes[2 + k].start()
        for k, place in enumerate([(j_x, pl.ds(0, h)), (j_y, pl.ds(0, h)), (j_d, part_a), (j_d, part_b)]):
            got = dst.at[place[0], place[1], theirs]
            _remote(got, got, pass_send.at[k], pass_recv.at[k], sibling).wait_recv()
        for cp in sends + relays + passes:
            cp.wait_send()
        to_slot.wait()

    return pl.pallas_call(
        body,
        in_specs=[_ANY],
        out_specs=_ANY,
        out_shape=jax.ShapeDtypeStruct((NCHIP, h, w), shard.dtype),
        scratch_shapes=[pltpu.SemaphoreType.DMA((2,))] * 4 + [pltpu.SemaphoreType.DMA((4,))] * 2
        + [pltpu.SemaphoreType.DMA((2,)), pltpu.VMEM((h, w), shard.dtype)],
        compiler_params=pltpu.CompilerParams(vmem_limit_bytes=VMEM_LIMIT),
        name="gather_win",
    )(shard)


def _gathered_shapes(shards):
    return [jax.ShapeDtypeStruct((NCHIP,) + s.shape, s.dtype) for s in shards]


def _gather_scratch(shards):
    n = len(shards)
    return ([pltpu.SemaphoreType.DMA((3, n))] * 4 + [pltpu.SemaphoreType.DMA((2, n))]
            + [pltpu.VMEM(s.shape, s.dtype) for s in shards])


def _own_to_stage(srcs, scratch):
    own_sems, stages = scratch[4], scratch[5:]
    return [pltpu.make_async_copy(srcs[a], stages[a], own_sems.at[0, a]) for a in range(len(srcs))]


def _own_to_slot(dsts, scratch):
    own_sems, stages = scratch[4], scratch[5:]
    me = _place()[3]
    return [pltpu.make_async_copy(stages[a], dsts[a].at[me], own_sems.at[1, a]) for a in range(len(dsts))]


def _gather_copies(srcs, dsts, by_columns, sems, sends_only):
    n = len(srcs)
    send_sems, recv_sems, pass_send, pass_recv = sems[:4]
    x, y, c, me, others = _place()
    sibling = (x, y, 1 - c)

    def src(a, e):
        return _half(srcs[a], e, by_columns[a])

    def dst(a, j, e):
        return _half(dsts[a].at[j], e, by_columns[a])

    sends, arrivals, passes, passed = [], [], [], []
    for k, (cx, cy) in enumerate(others):
        j = 2 * cx + cy
        for a in range(n):
            sends.append(_remote(src(a, c), dst(a, me, c), send_sems.at[k, a], recv_sems.at[k, a], (cx, cy, c)))
            if not sends_only:
                arrivals.append(_remote(src(a, c), dst(a, j, c), send_sems.at[k, a], recv_sems.at[k, a], (cx, cy, c)))
                passes.append(_remote(dst(a, j, c), dst(a, j, c), pass_send.at[k, a], pass_recv.at[k, a], sibling))
                passed.append(_remote(src(a, c), dst(a, j, 1 - c), pass_send.at[k, a], pass_recv.at[k, a], sibling))
    return sends, arrivals, passes, passed


def _gather_start(srcs, dsts, by_columns, scratch):
    for cp in _gather_copies(srcs, dsts, by_columns, scratch, sends_only=True)[0]:
        cp.start()
    for cp in _own_to_stage(srcs, scratch):
        cp.start()


def _gather_finish(srcs, dsts, by_columns, scratch):
    for cp in _own_to_stage(srcs, scratch):
        cp.wait()
    own = _own_to_slot(dsts, scratch)
    for cp in own:
        cp.start()
    sends, arrivals, passes, passed = _gather_copies(srcs, dsts, by_columns, scratch, sends_only=False)
    for arrival, cp in zip(arrivals, passes):
        arrival.wait_recv()
        cp.start()
    for arrival in passed:
        arrival.wait_recv()
    for cp in sends + passes:
        cp.wait_send()
    for cp in own:
        cp.wait()


HALF_FIRST, CHIP_FIRST, BY_COLUMNS = "half_first", "chip_first", "by_columns"


def _sibling_halves(bufs, kinds):
    n = len(bufs)

    def body(*refs):
        cps = _halves_copies(refs[:n], refs[n:2 * n], kinds, refs[2 * n:])
        for cp in cps:
            cp.start()
        for cp in cps:
            cp.wait()

    return pl.pallas_call(
        body,
        in_specs=[_ANY] * n,
        out_specs=[_ANY] * n,
        out_shape=_halves_shapes(bufs, kinds),
        scratch_shapes=_halves_sems(bufs),
        name="sibling_halves",
    )(*bufs)


def _halves_shapes(bufs, kinds):
    def landed(b, kind):
        if kind == HALF_FIRST:
            return b.shape[1:]
        if kind == CHIP_FIRST:
            return (b.shape[0],) + b.shape[2:]
        return b.shape[:2] + (b.shape[2] // 2,)

    return [jax.ShapeDtypeStruct(landed(b, kind), b.dtype) for b, kind in zip(bufs, kinds)]


def _halves_sems(bufs):
    return [pltpu.SemaphoreType.DMA((len(bufs), NCHIP))] * 2


def _halves_copies(srcs, dsts, kinds, sems):
    send_sems, recv_sems = sems
    x, y, c, _, _ = _place()
    cps = []
    for a, kind in enumerate(kinds):
        if kind == HALF_FIRST:
            cps.append(_remote(srcs[a].at[1 - c], dsts[a], send_sems.at[a, 0], recv_sems.at[a, 0], (x, y, 1 - c)))
        else:
            cps += [_remote(_half(srcs[a].at[j], 1 - c, kind == BY_COLUMNS), dsts[a].at[j],
                            send_sems.at[a, j], recv_sems.at[a, j], (x, y, 1 - c)) for j in range(srcs[a].shape[0])]
    return cps


TO_ITS_CHIP, TO_EVERY_CHIP, ROWS_TO_ITS_CHIP = "to_its_chip", "to_every_chip", "rows_to_its_chip"
PIECE_STEP = 2048
PIECE_ROWS = 2064


def _chip_exchange(parts, sends_what):
    n = len(parts)

    def body(*refs):
        cps = _exchange_copies(refs[:n], refs[n:2 * n], sends_what, refs[2 * n:])
        for cp in cps:
            cp.start()
        for cp in cps:
            cp.wait()

    return pl.pallas_call(
        body,
        in_specs=[_ANY] * n,
        out_specs=[_ANY] * n,
        out_shape=_exchange_shapes(parts, sends_what),
        scratch_shapes=_exchange_sems(n),
        name="chip_exchange",
    )(*parts)


def _exchange_shapes(parts, sends_what):
    def landed(p, what):
        return (3, PIECE_ROWS, p.shape[1]) if what == ROWS_TO_ITS_CHIP else (3,) + p.shape[1:]

    return [jax.ShapeDtypeStruct(landed(p, what), p.dtype) for p, what in zip(parts, sends_what)]


def _exchange_sems(n):
    return [pltpu.SemaphoreType.DMA((3, n))] * 2


def _exchange_copies(srcs, dsts, sends_what, sems):
    send_sems, recv_sems = sems
    x, y, c, me, others = _place()

    def part(a, j):
        if sends_what[a] == ROWS_TO_ITS_CHIP:
            return srcs[a].at[pl.ds(pl.multiple_of(j * PIECE_STEP, PIECE_STEP), PIECE_ROWS)]
        return srcs[a].at[j if sends_what[a] == TO_ITS_CHIP else 0]

    return [_remote(part(a, 2 * cx + cy), dsts[a].at[k], send_sems.at[k, a], recv_sems.at[k, a], (cx, cy, c))
            for k, (cx, cy) in enumerate(others) for a in range(len(srcs))]


def _sibling_swap(halves):
    n = len(halves)

    def body(*refs):
        srcs, dsts = refs[:n], refs[n:2 * n]
        send_sems, recv_sems = refs[2 * n:]
        x, y, c, _, _ = _place()
        cps = [_remote(srcs[a], dsts[a], send_sems.at[a], recv_sems.at[a], (x, y, 1 - c)) for a in range(n)]
        for cp in cps:
            cp.start()
        for cp in cps:
            cp.wait()

    return pl.pallas_call(
        body,
        in_specs=[_ANY] * n,
        out_specs=[_ANY] * n,
        out_shape=[jax.ShapeDtypeStruct(s.shape, s.dtype) for s in halves],
        scratch_shapes=[pltpu.SemaphoreType.DMA((n,))] * 2,
        name="sibling_swap",
    )(*halves)


def _tile(h, w, operands=5):
    if h % 128 == 0:
        return 128, w
    budget = VMEM_LIMIT * 3 // 4 // (2 * operands * 4)
    tw = w
    while h * tw > budget and tw % 256 == 0:
        tw //= 2
    return h, tw


def _pair_sum(place, buf, kind, got, out_dtype):
    nj, h, w = got.shape
    th, tw = _tile(h, w)
    nq = w // tw

    def body(p_ref, a_ref, b_ref, o_ref):
        del p_ref
        o_ref[...] = (a_ref[...] + b_ref[...]).astype(out_dtype)

    if kind == HALF_FIRST:
        mine = pl.BlockSpec((None, None, th, tw), lambda j, r, q, p: (p[0], j, r, q))
    elif kind == CHIP_FIRST:
        mine = pl.BlockSpec((None, None, th, tw), lambda j, r, q, p: (j, p[0], r, q))
    else:
        mine = pl.BlockSpec((None, th, tw), lambda j, r, q, p: (j, r, p[0] * nq + q))
    return pl.pallas_call(
        body,
        grid_spec=pltpu.PrefetchScalarGridSpec(
            num_scalar_prefetch=1,
            grid=(nj, h // th, w // tw),
            in_specs=[mine, pl.BlockSpec((None, th, tw), lambda j, r, q, p: (j, r, q))],
            out_specs=pl.BlockSpec((None, th, tw), lambda j, r, q, p: (j, r, q)),
        ),
        out_shape=jax.ShapeDtypeStruct((nj, h, w), out_dtype),
        compiler_params=_params(("parallel", "parallel", "parallel")),
        name="pair_sum",
    )(place, buf, got)


def _chip_sum(place, part, slots):
    nj, h, w = part.shape
    th, tw = _tile(h, w)

    def body(p_ref, own_ref, s_ref, o_ref):
        me = p_ref[1]
        own = own_ref[...].astype(f32)
        by_flip = {2: s_ref[0].astype(f32), 1: s_ref[1].astype(f32), 3: s_ref[2].astype(f32)}
        acc = None
        for j in range(NCHIP):
            flip = me ^ j
            term = jnp.where(flip == 0, own, jnp.where(flip == 2, by_flip[2], jnp.where(flip == 1, by_flip[1], by_flip[3])))
            acc = term if acc is None else acc + term
        o_ref[...] = acc

    return pl.pallas_call(
        body,
        grid_spec=pltpu.PrefetchScalarGridSpec(
            num_scalar_prefetch=1,
            grid=(h // th, w // tw),
            in_specs=[pl.BlockSpec((None, th, tw), lambda r, q, p: (p[1] if nj == NCHIP else 0, r, q)),
                      pl.BlockSpec((3, th, tw), lambda r, q, p: (0, r, q))],
            out_specs=pl.BlockSpec((th, tw), lambda r, q, p: (r, q)),
        ),
        out_shape=jax.ShapeDtypeStruct((h, w), f32),
        compiler_params=_params(("parallel", "parallel")),
        name="chip_sum",
    )(place, part, slots)


def _adamw_math(w, g, m, v):
    nm = ADAM_B1 * m + (1.0 - ADAM_B1) * g
    nv = ADAM_B2 * v + (1.0 - ADAM_B2) * (g * g)
    m_hat = nm / (1.0 - ADAM_B1 ** ADAM_STEP)
    v_hat = nv / (1.0 - ADAM_B2 ** ADAM_STEP)
    return -ADAM_LR * (m_hat / (jnp.sqrt(v_hat) + ADAM_EPS) + ADAM_WD * w), nm, nv


def _adamw(w, g, m, v):
    rows, width = w.shape
    th, tw = _tile(rows, width, operands=7)

    def body(w_ref, g_ref, m_ref, v_ref, d_ref, nm_ref, nv_ref):
        d_ref[...], nm_ref[...], nv_ref[...] = _adamw_math(w_ref[...], g_ref[...], m_ref[...], v_ref[...])

    spec = pl.BlockSpec((th, tw), lambda r, q: (r, q))
    return pl.pallas_call(
        body,
        grid=(rows // th, width // tw),
        in_specs=[spec] * 4,
        out_specs=[spec] * 3,
        out_shape=[jax.ShapeDtypeStruct((rows, width), f32)] * 3,
        compiler_params=_params(("parallel", "parallel")),
        name="adamw",
    )(w, g, m, v)


def _adamw_halves(place, w, mine, got, m, v, axis):
    rows, width = w.shape
    h, hw = mine.shape
    th, tw = _tile(h, hw, operands=10)
    nr, nq = h // th, hw // tw

    def body(p_ref, w_ref, a_ref, b_ref, m_ref, v_ref, g_ref, d_ref, nm_ref, nv_ref):
        g = jnp.where(pl.program_id(0) == p_ref[0], a_ref[...], b_ref[...])
        g_ref[...] = g
        d_ref[...], nm_ref[...], nv_ref[...] = _adamw_math(w_ref[...], g, m_ref[...], v_ref[...])

    if axis == 0:
        full = pl.BlockSpec((th, tw), lambda e, r, q, p: (e * nr + r, q))
    else:
        full = pl.BlockSpec((th, tw), lambda e, r, q, p: (r, e * nq + q))
    half = pl.BlockSpec((th, tw), lambda e, r, q, p: (r, q))
    return pl.pallas_call(
        body,
        grid_spec=pltpu.PrefetchScalarGridSpec(
            num_scalar_prefetch=1,
            grid=(2, nr, nq),
            in_specs=[full, half, half, full, full],
            out_specs=[full] * 4,
        ),
        out_shape=[jax.ShapeDtypeStruct((rows, width), f32)] * 4,
        compiler_params=_params(("parallel", "parallel", "parallel")),
        name="adamw_halves",
    )(place, w, mine, got, m, v)


def _pair_sum_columns(place, buf, got, out_dtype, th=432):
    rows, hw = got.shape

    def body(p_ref, a_ref, b_ref, o_ref):
        del p_ref
        o_ref[...] = (a_ref[...] + b_ref[...]).astype(out_dtype)

    return pl.pallas_call(
        body,
        grid_spec=pltpu.PrefetchScalarGridSpec(
            num_scalar_prefetch=1,
            grid=(rows // th,),
            in_specs=[pl.BlockSpec((th, hw), lambda r, p: (r, p[0])), pl.BlockSpec((th, hw), lambda r, p: (r, 0))],
            out_specs=pl.BlockSpec((th, hw), lambda r, p: (r, 0)),
        ),
        out_shape=jax.ShapeDtypeStruct((rows, hw), out_dtype),
        compiler_params=_params(("parallel",)),
        name="pair_sum_columns",
    )(place, buf, got)


_SMALL = (("norm_g", 8), ("ln_v_g", 8), ("ln_v_b", 8), ("w_spatial", 1024), ("b_spatial", 8), ("b_gate_up", 4),
          ("gla_norm_g", 2), ("final_norm_g", 8), ("w_gate_up", 64), ("loss", 8))
_SMALL_ROWS = 1152


def _pack_rows(arrays, rows):
    flat = jnp.concatenate([a.reshape(-1, 128) for a in arrays], axis=0)
    return jnp.pad(flat, ((0, rows - flat.shape[0]), (0, 0)))


def kernel(x, norm_g, w_in, ln_v_g, ln_v_b, w_spatial, b_spatial, w_gate_up, b_gate_up, gla_norm_g, w_branch_a, w_branch_b, w_out, final_norm_g, loss_target, m_norm_g, m_w_in, m_ln_v_g, m_ln_v_b, m_w_spatial, m_b_spatial, m_w_gate_up, m_b_gate_up, m_gla_norm_g, m_w_branch_a, m_w_branch_b, m_w_out, m_final_norm_g, v_norm_g, v_w_in, v_ln_v_g, v_ln_v_b, v_w_spatial, v_b_spatial, v_w_gate_up, v_b_gate_up, v_gla_norm_g, v_w_branch_a, v_w_branch_b, v_w_out, v_final_norm_g):
    chip = 2 * lax.axis_index("x") + lax.axis_index("y")
    core = lax.axis_index("c")
    place = jnp.stack([core, chip]).astype(jnp.int32)
    mat_names = ("w_branch_a", "w_branch_b", "w_out")

    wt_shard = jnp.transpose(w_in[0]).astype(bf16)
    mats = [w[0].astype(bf16).reshape(2, D // NCHIP // 2, D) for w in (w_branch_a, w_branch_b, w_out)]
    gate_sh = w_gate_up[0].reshape(2, RANK // 2, 128)
    g_win = _gather_win(wt_shard)
    w_t = g_win.reshape(NCHIP * WIN_SHARD, D)
    b_sb = jnp.broadcast_to(b_spatial[0][:, :, None], (HA, CA, GA))
    xs, tgt = x[0], loss_target[0]

    proj, lr, h, g_a, g_b, g_o, g_gate = _proj_fwd(xs, norm_g, w_t, riders=mats + [gate_sh])
    w_a, w_b, w_o = (g.reshape(D, D) for g in (g_a, g_b, g_o))
    w_gate = jnp.transpose(g_gate.reshape(NCHIP, RANK, 128), (1, 0, 2)).reshape(RANK, KEYB)
    w_gate = jnp.pad(w_gate, ((0, LRP - RANK), (0, 0)))
    a = _mixer_a_fwd(proj, ln_v_g, ln_v_b, w_spatial[0], b_sb)
    o, ob, states = _gla_fwd(proj, lr, w_gate, b_gate_up, gla_norm_g)
    dproj, dy, da, dob, dwa, dwb, dwo, dgf, loss_cols = _merge_fwd_bwd(xs, tgt, proj, a, ob, w_a, w_b, w_o,
                                                                       final_norm_g.reshape(1, D))
    b_mats = [t.reshape(NCHIP, 2, D // NCHIP // 2, D) for t in (dwa, dwb, dwo)]
    dproj, dws, dbs, dlg, dlb, *got_mats = _mixer_a_bwd(proj, da, dproj, ln_v_g, ln_v_b, w_spatial[0], b_sb,
                                                        riders=b_mats, kinds=[CHIP_FIRST] * 3)
    part_mats = [_pair_sum(place, b, CHIP_FIRST, g, bf16) for b, g in zip(b_mats, got_mats)]
    dproj, dlr, dwg, dbg, dgg, *slots_mats = _gla_bwd(proj, lr, o, states, dob, dproj, w_gate, b_gate_up, gla_norm_g,
                                                      riders=part_mats, sends_what=[TO_ITS_CHIP] * 3)
    dwt, got_win = _dw_in(h, dproj, dlr, swap=True)
    part_win = _pair_sum_columns(place, dwt, got_win, bf16)
    dx, dg0, slots_win = _dx_bwd(xs, dy, dproj, dlr, norm_g, w_t, riders=[part_win], sends_what=[ROWS_TO_ITS_CHIP])
    slots_big = [slots_win] + slots_mats
    small = _pack_rows([dg0, dlg, dlb, dws, dbs[:, :, 0], dbg, dgg, dgf, dwg[:RANK], loss_cols], _SMALL_ROWS)
    b_small = small.reshape(2, 1, _SMALL_ROWS // 2, 128)
    (got_small,) = _sibling_halves([b_small], [HALF_FIRST])
    part_small = _pair_sum(place, b_small, HALF_FIRST, got_small, f32)
    (slots_small,) = _chip_exchange([part_small], [TO_EVERY_CHIP])
    own_win = lax.dynamic_slice_in_dim(part_win, chip * PIECE_STEP, PIECE_ROWS, axis=0)[None]
    mine = [_chip_sum(place, p, s) for p, s in zip([own_win] + part_mats + [part_small], slots_big + [slots_small])]
    theirs = list(_sibling_swap(mine))
    mine[0], theirs[0] = (lax.dynamic_slice_in_dim(t, (WIN_SHARD - PIECE_STEP) * chip, WIN_SHARD, axis=0)
                          for t in (mine[0], theirs[0]))

    g_small = jnp.where(core == 0, jnp.concatenate([mine[4], theirs[4]], axis=0),
                        jnp.concatenate([theirs[4], mine[4]], axis=0))
    grads = {}
    row = 0
    for name, rows in _SMALL:
        grads[name] = g_small[row:row + rows]
        row += rows
    loss = jnp.sum(grads["loss"])
    dwg_full = grads["w_gate_up"].reshape(RANK, KEYB)
    grads["w_gate_up"] = lax.dynamic_slice_in_dim(dwg_full, chip * 128, 128, axis=1)

    weights = dict(norm_g=norm_g, w_in=w_in, ln_v_g=ln_v_g, ln_v_b=ln_v_b, w_spatial=w_spatial, b_spatial=b_spatial,
                   w_gate_up=w_gate_up, b_gate_up=b_gate_up, gla_norm_g=gla_norm_g, w_branch_a=w_branch_a,
                   w_branch_b=w_branch_b, w_out=w_out, final_norm_g=final_norm_g)
    m_in = dict(norm_g=m_norm_g, w_in=m_w_in, ln_v_g=m_ln_v_g, ln_v_b=m_ln_v_b, w_spatial=m_w_spatial,
                b_spatial=m_b_spatial, w_gate_up=m_w_gate_up, b_gate_up=m_b_gate_up, gla_norm_g=m_gla_norm_g,
                w_branch_a=m_w_branch_a, w_branch_b=m_w_branch_b, w_out=m_w_out, final_norm_g=m_final_norm_g)
    v_in = dict(norm_g=v_norm_g, w_in=v_w_in, ln_v_g=v_ln_v_g, ln_v_b=v_ln_v_b, w_spatial=v_w_spatial,
                b_spatial=v_b_spatial, w_gate_up=v_w_gate_up, b_gate_up=v_b_gate_up, gla_norm_g=v_gla_norm_g,
                w_branch_a=v_w_branch_a, w_branch_b=v_w_branch_b, w_out=v_w_out, final_norm_g=v_final_norm_g)
    names = list(weights)
    small_names = [n for n in names if n != "w_in" and n not in mat_names]
    out_g, out_d, out_m, out_v = {}, {}, {}, {}
    res = _adamw_halves(place, jnp.transpose(w_in[0]), mine[0], theirs[0], jnp.transpose(m_w_in[0]),
                        jnp.transpose(v_w_in[0]), axis=1)
    out_g["w_in"], out_d["w_in"], out_m["w_in"], out_v["w_in"] = (jnp.transpose(t)[None] for t in res)
    for i, n in enumerate(mat_names):
        res = _adamw_halves(place, weights[n][0], mine[1 + i], theirs[1 + i], m_in[n][0], v_in[n][0], axis=0)
        out_g[n], out_d[n], out_m[n], out_v[n] = (t[None] for t in res)
    upd_rows = sum(weights[n].size for n in small_names) // 128
    pad_rows = -(-upd_rows // 8) * 8
    packed = [_pack_rows([t[n] for n in small_names], pad_rows) for t in (weights, grads, m_in, v_in)]
    d_s, m_s, v_s = _adamw(*packed)
    row = 0
    for n in small_names:
        shape = weights[n].shape
        rows = weights[n].size // 128
        out_g[n] = grads[n].reshape(shape)
        out_d[n], out_m[n], out_v[n] = (t[row:row + rows].reshape(shape) for t in (d_s, m_s, v_s))
        row += rows
    return (loss, dx[None], *[out_g[n] for n in names], *[out_d[n] for n in names],
            *[out_m[n] for n in names], *[out_v[n] for n in names])
```

```python
import functools
import math

import jax
import jax.numpy as jnp
from jax import lax
from jax.experimental import pallas as pl
from jax.experimental.pallas import tpu as pltpu

f32 = jnp.float32
bf16 = jnp.bfloat16

D = 1024
NMAIN = 8192
LRP = 128
RANK = 16
HA, GA, CA = 8, 128, 128
HB, DK, DV, CB = 4, 128, 256, 64
KEYB = HB * DK
EPS = 1e-6
LN_EPS = 1e-5
GATE_NORM = 16.0
QSCALE = DK ** -0.5
COL_U, COL_V, COL_ZA = 0, 1, 2
COL_Q, COL_K = 6, 7
COL_VB, COL_ZB = 4, 5
COL_GATES = 3
VMEM_LIMIT = 56 * 1024 * 1024
NCHIP = 4
WIN_SHARD = 2052
LR_COL = 6144
_ANY = pl.BlockSpec(memory_space=pl.ANY)

ADAM_LR, ADAM_B1, ADAM_B2, ADAM_EPS, ADAM_WD, ADAM_STEP = 0.001, 0.9, 0.999, 1e-08, 0.01, 10

_SQRT_HALF = 0.7071067811865476
_INV_SQRT_2PI = 0.3989422804014327


def _dot(a, b):
    return jnp.dot(a, b, preferred_element_type=f32)


def _dot_nt(a, b):
    return lax.dot_general(a, b, (((1,), (1,)), ((), ())), preferred_element_type=f32)


def _dot_tn(a, b):
    return lax.dot_general(a, b, (((0,), (0,)), ((), ())), preferred_element_type=f32)


def _dot_exact(a, b):
    return jnp.dot(a, b, preferred_element_type=f32, precision=lax.Precision.HIGHEST)


def _gelu(x):
    return 0.5 * x * (1.0 + lax.erf(x * _SQRT_HALF))


def _gelu_and_grad(x):
    cdf = 0.5 * (1.0 + lax.erf(x * _SQRT_HALF))
    return x * cdf, cdf + x * (jnp.exp(-0.5 * x * x) * _INV_SQRT_2PI)


def _sigmoid(x):
    return 0.5 * jnp.tanh(0.5 * x) + 0.5


def _params(sem):
    return pltpu.CompilerParams(dimension_semantics=sem, vmem_limit_bytes=VMEM_LIMIT)


def _resident(shape):
    nd = len(shape)
    return pl.BlockSpec(shape, lambda *_: (0,) * nd, pipeline_mode=pl.Buffered(1))


def _w_rows(c, tn):
    start = c * tn + (RANK if c * tn >= LR_COL else 0)
    return slice(start, start + tn)


LR_ROWS = slice(LR_COL, LR_COL + LRP)


def _proj_fwd(x, g0, w_t, riders=(), tm=256, tn=1024):
    T = x.shape[0]
    nsteps = T // tm
    n = len(riders)

    def body(x_ref, g_ref, w_ref, *rest):
        srcs, (proj_ref, lr_ref, h_ref), dsts, sems = rest[:n], rest[n:n + 3], rest[n + 3:2 * n + 3], rest[2 * n + 3:]
        if n:
            @pl.when(pl.program_id(0) == 0)
            def _():
                _gather_start(srcs, dsts, [False] * n, sems)

        xv = x_ref[...]
        r = lax.rsqrt(jnp.mean(xv * xv, axis=-1, keepdims=True) + EPS)
        h = (xv * r * g_ref[...]).astype(bf16)
        h_ref[...] = h
        lr_ref[...] = _dot_nt(h, w_ref[LR_ROWS, :])
        for c in range(NMAIN // tn):
            proj_ref[:, c * tn:(c + 1) * tn] = _dot_nt(h, w_ref[_w_rows(c, tn), :])

        if n:
            @pl.when(pl.program_id(0) == nsteps - 1)
            def _():
                _gather_finish(srcs, dsts, [False] * n, sems)

    return pl.pallas_call(
        body,
        grid=(nsteps,),
        in_specs=[
            pl.BlockSpec((tm, D), lambda i: (i, 0)),
            _resident((1, D)), _resident((NMAIN + RANK, D)),
        ] + [_ANY] * n,
        out_specs=[
            pl.BlockSpec((tm, NMAIN), lambda i: (i, 0)),
            pl.BlockSpec((tm, LRP), lambda i: (i, 0)),
            pl.BlockSpec((tm, D), lambda i: (i, 0)),
        ] + [_ANY] * n,
        out_shape=[
            jax.ShapeDtypeStruct((T, NMAIN), f32),
            jax.ShapeDtypeStruct((T, LRP), f32),
            jax.ShapeDtypeStruct((T, D), bf16),
        ] + _gathered_shapes(riders),
        scratch_shapes=_gather_scratch(riders) if n else [],
        compiler_params=_params(("arbitrary",)),
        name="proj_fwd",
    )(x, g0, w_t, *riders)


def _causal_mask():
    t = lax.broadcasted_iota(jnp.int32, (CA, CA), 0)
    s = lax.broadcasted_iota(jnp.int32, (CA, CA), 1)
    return s <= t


def _layernorm_parts(gv):
    mu = jnp.mean(gv, axis=-1, keepdims=True)
    xc = gv - mu
    rs = lax.rsqrt(jnp.mean(xc * xc, axis=-1, keepdims=True) + LN_EPS)
    return xc * rs, rs


def _mixer_a_fwd(proj, ln_g, ln_b, w_s, b_sb, tm=256):
    T = proj.shape[0]

    def body(u_ref, v_ref, za_ref, lg_ref, lb_ref, ws_ref, bs_ref, a_ref, vln_s):
        vhat, _ = _layernorm_parts(_gelu(v_ref[...]))
        vln_s[...] = (vhat * lg_ref[...] + lb_ref[...]).astype(bf16)
        mask = _causal_mask()
        for g in range(HA):
            wg = jnp.where(mask, ws_ref[g], 0.0).astype(bf16)
            cols = slice(g * GA, (g + 1) * GA)
            for c in range(tm // CA):
                rows = slice(c * CA, (c + 1) * CA)
                mixed = _dot(wg, vln_s[rows, cols]) + bs_ref[g]
                za = za_ref[rows, cols]
                a = _gelu(u_ref[rows, cols]) * mixed * (za * _sigmoid(za))
                a_ref[rows, cols] = a.astype(bf16)

    def col(cidx):
        return pl.BlockSpec((tm, D), lambda i, c=cidx: (i, c))

    return pl.pallas_call(
        body,
        grid=(T // tm,),
        in_specs=[col(COL_U), col(COL_V), col(COL_ZA), _resident((1, D)), _resident((1, D)),
                  _resident((HA, CA, CA)), _resident((HA, CA, GA))],
        out_specs=pl.BlockSpec((tm, D), lambda i: (i, 0)),
        out_shape=jax.ShapeDtypeStruct((T, D), bf16),
        scratch_shapes=[pltpu.VMEM((tm, D), bf16)],
        compiler_params=_params(("parallel",)),
        name="mixer_a_fwd",
    )(proj, proj, proj, ln_g, ln_b, w_s, b_sb)


def _mixer_a_bwd(proj, da, dproj, ln_g, ln_b, w_s, b_sb, riders=(), kinds=(), tm=256):
    T = proj.shape[0]
    nsteps = T // tm
    n = len(riders)

    def body(u_ref, v_ref, za_ref, da_ref, dp_in, lg_ref, lb_ref, ws_ref, bs_ref, *rest):
        srcs, (dp_ref, dws_ref, dbs_ref, dlg_ref, dlb_ref), dsts = rest[:n], rest[n:n + 5], rest[n + 5:2 * n + 5]
        vln_s, dvln_s, *sems = rest[2 * n + 5:]
        del dp_in
        i = pl.program_id(0)

        @pl.when(i == 0)
        def _():
            dws_ref[...] = jnp.zeros_like(dws_ref)
            dbs_ref[...] = jnp.zeros_like(dbs_ref)
            dlg_ref[...] = jnp.zeros_like(dlg_ref)
            dlb_ref[...] = jnp.zeros_like(dlb_ref)
            for cp in _halves_copies(srcs, dsts, kinds, sems) if n else []:
                cp.start()

        gv, gv_grad = _gelu_and_grad(v_ref[...])
        vhat, rs = _layernorm_parts(gv)
        vln_s[...] = (vhat * lg_ref[...] + lb_ref[...]).astype(bf16)
        mask = _causal_mask()
        for g in range(HA):
            wg = jnp.where(mask, ws_ref[g], 0.0).astype(bf16)
            cols = slice(g * GA, (g + 1) * GA)
            dw_acc = jnp.zeros((CA, CA), f32)
            db_acc = jnp.zeros((CA, 1), f32)
            for c in range(tm // CA):
                rows = slice(c * CA, (c + 1) * CA)
                vln = vln_s[rows, cols]
                mixed = _dot(wg, vln) + bs_ref[g]
                u = u_ref[rows, cols]
                za = za_ref[rows, cols]
                da_blk = da_ref[rows, cols]
                sg = _sigmoid(za)
                sz = za * sg
                gu, gu_grad = _gelu_and_grad(u)
                dp_ref[rows, cols] = (da_blk * mixed * sz * gu_grad).astype(bf16)
                dp_ref[rows, 2 * D + g * GA:2 * D + (g + 1) * GA] = (
                    da_blk * gu * mixed * (sg * (1.0 + za * (1.0 - sg)))).astype(bf16)
                dmixed = da_blk * gu * sz
                dmb = dmixed.astype(bf16)
                dvln_s[rows, cols] = _dot_tn(wg, dmb)
                dw_acc = dw_acc + _dot_nt(dmb, vln)
                db_acc = db_acc + jnp.sum(dmixed, axis=-1, keepdims=True)
            dws_ref[g] += dw_acc
            dbs_ref[g] += jnp.broadcast_to(db_acc, (CA, GA))

        dvln = dvln_s[...]
        dlg_ref[...] += jnp.sum(dvln * vhat, axis=0, keepdims=True)
        dlb_ref[...] += jnp.sum(dvln, axis=0, keepdims=True)
        dvhat = dvln * lg_ref[...]
        dgv = rs * (dvhat - jnp.mean(dvhat, axis=-1, keepdims=True)
                    - vhat * jnp.mean(dvhat * vhat, axis=-1, keepdims=True))
        dp_ref[:, D:2 * D] = (dgv * gv_grad).astype(bf16)

        @pl.when(i == nsteps - 1)
        def _():
            for g in range(HA):
                dws_ref[g] = jnp.where(mask, dws_ref[g], 0.0)
            for cp in _halves_copies(srcs, dsts, kinds, sems) if n else []:
                cp.wait()

    def col(cidx):
        return pl.BlockSpec((tm, D), lambda i, c=cidx: (i, c))

    return pl.pallas_call(
        body,
        grid=(nsteps,),
        in_specs=[col(COL_U), col(COL_V), col(COL_ZA), pl.BlockSpec((tm, D), lambda i: (i, 0)),
                  pl.BlockSpec(memory_space=pl.ANY),
                  _resident((1, D)), _resident((1, D)), _resident((HA, CA, CA)), _resident((HA, CA, GA))] + [_ANY] * n,
        out_specs=[pl.BlockSpec((tm, 3 * D), lambda i: (i, 0)),
                   _resident((HA, CA, CA)), _resident((HA, CA, GA)), _resident((1, D)), _resident((1, D))] + [_ANY] * n,
        out_shape=[jax.ShapeDtypeStruct(dproj.shape, dproj.dtype),
                   jax.ShapeDtypeStruct((HA, CA, CA), f32), jax.ShapeDtypeStruct((HA, CA, GA), f32),
                   jax.ShapeDtypeStruct((1, D), f32), jax.ShapeDtypeStruct((1, D), f32)] + _halves_shapes(riders, kinds),
        scratch_shapes=[pltpu.VMEM((tm, D), bf16), pltpu.VMEM((tm, D), f32)] + (_halves_sems(riders) if n else []),
        input_output_aliases={4: 0},
        compiler_params=_params(("arbitrary",)),
        name="mixer_a_bwd",
    )(proj, proj, proj, da, dproj, ln_g, ln_b, w_s, b_sb, *riders)


def _tri(n, upper):
    r = lax.broadcasted_iota(jnp.int32, (n, n), 0)
    c = lax.broadcasted_iota(jnp.int32, (n, n), 1)
    return jnp.where((c >= r) if upper else (c <= r), 1.0, 0.0).astype(f32)


def _chunk_tri(n, upper):
    r = lax.broadcasted_iota(jnp.int32, (n, n), 0)
    c = lax.broadcasted_iota(jnp.int32, (n, n), 1)
    shift = CB.bit_length() - 1
    same_chunk = jnp.right_shift(r, shift) == jnp.right_shift(c, shift)
    return jnp.where(same_chunk & ((c >= r) if upper else (c <= r)), 1.0, 0.0).astype(f32)


def _log_alpha(lr, wg, bg):
    logit = _dot(lr.astype(bf16), wg.astype(bf16)) + bg
    la = (jnp.minimum(logit, 0.0) - jnp.log1p(jnp.exp(-jnp.abs(logit)))) * (1.0 / GATE_NORM)
    return logit, la


def _gla_fwd(proj, lr, w_gate, b_gate, gla_g, tm=256):
    T = proj.shape[0]
    nchunk = T // CB
    cpb = tm // CB

    def body(q_ref, k_ref, v_ref, zb_ref, lr_ref, wg_ref, bg_ref, gg_ref,
             o_ref, ob_ref, st_ref, state, la_s):
        @pl.when(pl.program_id(0) == 0)
        def _():
            state[...] = jnp.zeros_like(state)

        _, la = _log_alpha(lr_ref[...], wg_ref[...], bg_ref[...])
        la_s[...] = _dot_exact(_chunk_tri(tm, upper=False), la)
        causal = _tri(CB, upper=False) > 0.5
        states = [state[hd] for hd in range(HB)]
        for c in range(cpb):
            rows = slice(c * CB, (c + 1) * CB)
            b = la_s[rows, :]
            bl = b[CB - 1:CB, :]
            bm = b[CB // 2 - 1:CB // 2, :]
            q = q_ref[rows, :] * QSCALE
            k = k_ref[rows, :]
            qi_all = (q * jnp.exp(b - bm)).astype(bf16)
            ki_all = (k * jnp.exp(bm - b)).astype(bf16)
            qe_all = (q * jnp.exp(b)).astype(bf16)
            ks_all = (k * jnp.exp(bl - b)).astype(bf16)
            e_l = jnp.exp(bl)
            for hd in range(HB):
                kc = slice(hd * DK, (hd + 1) * DK)
                vc = slice(hd * DV, (hd + 1) * DV)
                v = v_ref[rows, vc].astype(bf16)
                p = jnp.where(causal, _dot_nt(qi_all[:, kc], ki_all[:, kc]), 0.0).astype(bf16)
                s0 = states[hd]
                st_ref[c, hd] = s0
                o = _dot(p, v) + _dot_nt(qe_all[:, kc], s0.astype(bf16))
                states[hd] = s0 * e_l[:, kc] + _dot_tn(v, ks_all[:, kc])
                o_ref[rows, vc] = o
                ro = lax.rsqrt(jnp.mean(o * o, axis=-1, keepdims=True) + EPS)
                zb = zb_ref[rows, vc]
                ob_ref[rows, vc] = (o * ro * gg_ref[...] * (zb * _sigmoid(zb))).astype(bf16)
        for hd in range(HB):
            state[hd] = states[hd]

    return pl.pallas_call(
        body,
        grid=(T // tm,),
        in_specs=[pl.BlockSpec((tm, KEYB), lambda i: (i, COL_Q)),
                  pl.BlockSpec((tm, KEYB), lambda i: (i, COL_K)),
                  pl.BlockSpec((tm, D), lambda i: (i, COL_VB)),
                  pl.BlockSpec((tm, D), lambda i: (i, COL_ZB)),
                  pl.BlockSpec((tm, LRP), lambda i: (i, 0)),
                  _resident((LRP, KEYB)), _resident((1, KEYB)), _resident((1, DV))],
        out_specs=[pl.BlockSpec((tm, D), lambda i: (i, 0)),
                   pl.BlockSpec((tm, D), lambda i: (i, 0)),
                   pl.BlockSpec((cpb, HB, DV, DK), lambda i: (i, 0, 0, 0))],
        out_shape=[jax.ShapeDtypeStruct((T, D), f32), jax.ShapeDtypeStruct((T, D), bf16),
                   jax.ShapeDtypeStruct((nchunk, HB, DV, DK), f32)],
        scratch_shapes=[pltpu.VMEM((HB, DV, DK), f32), pltpu.VMEM((tm, KEYB), f32)],
        compiler_params=_params(("arbitrary",)),
        name="gla_fwd",
    )(proj, proj, proj, proj, lr, w_gate, b_gate, gla_g)


def _gla_bwd(proj, lr, o, states, dob, dproj, w_gate, b_gate, gla_g, riders=(), sends_what=(), tm=256):
    T = proj.shape[0]
    cpb = tm // CB
    nb = T // tm
    n = len(riders)

    def body(q_ref, k_ref, v_ref, zb_ref, lr_ref, o_ref, st_ref, dob_ref, dp_in, wg_ref, bg_ref, gg_ref, *rest):
        srcs, (dp_ref, dlr_ref, dwg_ref, dbg_ref, dgg_ref), dsts = rest[:n], rest[n:n + 5], rest[n + 5:2 * n + 5]
        dstate, la_s, dlogit_s, tail_s, *sems = rest[2 * n + 5:]
        del dp_in
        step = pl.program_id(0)

        @pl.when(step == 0)
        def _():
            dstate[...] = jnp.zeros_like(dstate)
            dwg_ref[...] = jnp.zeros_like(dwg_ref)
            dbg_ref[...] = jnp.zeros_like(dbg_ref)
            dgg_ref[...] = jnp.zeros_like(dgg_ref)
            for cp in _exchange_copies(srcs, dsts, sends_what, sems) if n else []:
                cp.start()

        lr_v = lr_ref[...]
        logit, la = _log_alpha(lr_v, wg_ref[...], bg_ref[...])
        la_s[...] = _dot_exact(_chunk_tri(tm, upper=False), la)
        causal = _tri(CB, upper=False) > 0.5
        gg = gg_ref[...]
        dgg_acc = jnp.zeros((1, DV), f32)
        dstates = [dstate[hd] for hd in range(HB)]
        for c in reversed(range(cpb)):
            rows = slice(c * CB, (c + 1) * CB)
            b = la_s[rows, :]
            bl = b[CB - 1:CB, :]
            bm = b[CB // 2 - 1:CB // 2, :]
            eb_all, eqm_all, ekm_all = jnp.exp(b), jnp.exp(b - bm), jnp.exp(bm - b)
            eks_all, el_all = jnp.exp(bl - b), jnp.exp(bl)
            q_all = q_ref[rows, :] * QSCALE
            k_all = k_ref[rows, :]
            qi_all = (q_all * eqm_all).astype(bf16)
            ki_all = (k_all * ekm_all).astype(bf16)
            qe_all = (q_all * eb_all).astype(bf16)
            ksf_all = k_all * eks_all
            ks_all = ksf_all.astype(bf16)
            for hd in range(HB):
                kc = slice(hd * DK, (hd + 1) * DK)
                vc = slice(hd * DV, (hd + 1) * DV)
                o_h = o_ref[rows, vc]
                ro = lax.rsqrt(jnp.mean(o_h * o_h, axis=-1, keepdims=True) + EPS)
                ohat = o_h * ro
                zb = zb_ref[rows, vc]
                sg = _sigmoid(zb)
                dob_h = dob_ref[rows, vc]
                don = dob_h * (zb * sg)
                dp_ref[rows, 2 * D + hd * DV:2 * D + (hd + 1) * DV] = (
                    dob_h * ohat * gg * (sg * (1.0 + zb * (1.0 - sg)))).astype(bf16)
                dgg_acc = dgg_acc + jnp.sum(don * ohat, axis=0, keepdims=True)
                dohat = don * gg
                do = (ro * (dohat - ohat * jnp.mean(dohat * ohat, axis=-1, keepdims=True))).astype(bf16)
                e_b, e_qm, e_km, e_ks, e_l = eb_all[:, kc], eqm_all[:, kc], ekm_all[:, kc], eks_all[:, kc], el_all[:, kc]
                q, k, ks_f = q_all[:, kc], k_all[:, kc], ksf_all[:, kc]
                qi, ki, qe, ks = qi_all[:, kc], ki_all[:, kc], qe_all[:, kc], ks_all[:, kc]
                v = v_ref[rows, vc].astype(bf16)
                p = jnp.where(causal, _dot_nt(qi, ki), 0.0).astype(bf16)
                s0 = st_ref[c, hd]
                ds = dstates[hd]
                ds_b = ds.astype(bf16)
                dv = _dot_tn(p, do) + _dot_nt(ks, ds_b)
                dpm = jnp.where(causal, _dot_nt(do, v), 0.0).astype(bf16)
                dqi = _dot(dpm, ki)
                dki = _dot_tn(dpm, qi)
                dqe = _dot(do, s0.astype(bf16))
                dks = _dot(v, ds_b)
                dq_s = dqi * e_qm + dqe * e_b
                dk = dki * e_km + dks * e_ks
                tail = (jnp.sum(dks * ks_f, axis=0, keepdims=True)
                        + e_l * jnp.sum(ds * s0, axis=0, keepdims=True))
                dstates[hd] = _dot_tn(do, qe) + ds * e_l
                dp_ref[rows, kc] = (dq_s * QSCALE).astype(bf16)
                dp_ref[rows, KEYB + hd * DK:KEYB + (hd + 1) * DK] = dk.astype(bf16)
                dp_ref[rows, D + hd * DV:D + (hd + 1) * DV] = dv.astype(bf16)
                dlogit_s[rows, kc] = dq_s * q - dk * k
                tail_s[rows, kc] = jnp.broadcast_to(tail, (CB, DK))
        for hd in range(HB):
            dstate[hd] = dstates[hd]
        dgg_ref[...] += dgg_acc
        dg = _dot_exact(_chunk_tri(tm, upper=True), dlogit_s[...]) + tail_s[...]
        dlogit = dg * (1.0 / GATE_NORM) * _sigmoid(-logit)
        dbg_ref[...] += jnp.sum(dlogit, axis=0, keepdims=True)
        dlb = dlogit.astype(bf16)
        dlr_ref[...] = _dot_nt(dlb, wg_ref[...].astype(bf16)).astype(bf16)
        dwg_ref[...] += _dot_tn(lr_v.astype(bf16), dlb)

        if n:
            @pl.when(step == nb - 1)
            def _():
                for cp in _exchange_copies(srcs, dsts, sends_what, sems):
                    cp.wait()

    def rev(cidx):
        return lambda i, c=cidx: (nb - 1 - i, c)

    return pl.pallas_call(
        body,
        grid=(nb,),
        in_specs=[pl.BlockSpec((tm, KEYB), rev(COL_Q)),
                  pl.BlockSpec((tm, KEYB), rev(COL_K)),
                  pl.BlockSpec((tm, D), rev(COL_VB)),
                  pl.BlockSpec((tm, D), rev(COL_ZB)),
                  pl.BlockSpec((tm, LRP), rev(0)),
                  pl.BlockSpec((tm, D), rev(0)),
                  pl.BlockSpec((cpb, HB, DV, DK), lambda i: (nb - 1 - i, 0, 0, 0)),
                  pl.BlockSpec((tm, D), rev(0)),
                  pl.BlockSpec(memory_space=pl.ANY),
                  _resident((LRP, KEYB)), _resident((1, KEYB)), _resident((1, DV))] + [_ANY] * n,
        out_specs=[pl.BlockSpec((tm, 3 * D), rev(1)),
                   pl.BlockSpec((tm, LRP), rev(0)),
                   _resident((LRP, KEYB)), _resident((1, KEYB)), _resident((1, DV))] + [_ANY] * n,
        out_shape=[jax.ShapeDtypeStruct(dproj.shape, dproj.dtype),
                   jax.ShapeDtypeStruct((T, LRP), bf16),
                   jax.ShapeDtypeStruct((LRP, KEYB), f32), jax.ShapeDtypeStruct((1, KEYB), f32),
                   jax.ShapeDtypeStruct((1, DV), f32)] + _exchange_shapes(riders, sends_what),
        scratch_shapes=[pltpu.VMEM((HB, DV, DK), f32)] + [pltpu.VMEM((tm, KEYB), f32)] * 3
        + (_exchange_sems(n) if n else []),
        input_output_aliases={8: 0},
        compiler_params=_params(("arbitrary",)),
        name="gla_bwd",
    )(proj, proj, proj, proj, lr, o, states, dob, dproj, w_gate, b_gate, gla_g, *riders)


def _merge_fwd_bwd(x, tgt, proj, a, ob, w_a, w_b, w_o, g_f, tm=256):
    T = x.shape[0]

    def body(x_ref, t_ref, gt_ref, a_ref, ob_ref, wa_ref, wb_ref, wo_ref, gf_ref,
             dp_ref, dy_ref, da_ref, dob_ref, dwa_ref, dwb_ref, dwo_ref, dgf_ref, loss_ref):
        @pl.when(pl.program_id(0) == 0)
        def _():
            dwa_ref[...] = jnp.zeros_like(dwa_ref)
            dwb_ref[...] = jnp.zeros_like(dwb_ref)
            dwo_ref[...] = jnp.zeros_like(dwo_ref)
            dgf_ref[...] = jnp.zeros_like(dgf_ref)
            loss_ref[...] = jnp.zeros_like(loss_ref)

        ga = _sigmoid(gt_ref[:, :D])
        gb = _sigmoid(gt_ref[:, D:])
        a_v = a_ref[...]
        ob_v = ob_ref[...]
        pa = _dot(a_v, wa_ref[...])
        pb = _dot(ob_v, wb_ref[...])
        mb = (ga * pa + gb * pb).astype(bf16)
        y = x_ref[...] + _dot(mb, wo_ref[...])
        r1 = lax.rsqrt(jnp.mean(y * y, axis=-1, keepdims=True) + EPS)
        yhat = y * r1
        gf = gf_ref[...]
        err = yhat * gf - t_ref[...]
        loss_ref[...] += jnp.sum(err * err, axis=0, keepdims=True) * (0.5 / D)
        dout = err * (1.0 / D)
        dgf_ref[...] += jnp.sum(dout * yhat, axis=0, keepdims=True)
        dyn = dout * gf
        dy = r1 * (dyn - yhat * jnp.mean(dyn * yhat, axis=-1, keepdims=True))
        dy_ref[...] = dy
        dyb = dy.astype(bf16)
        dwo_ref[...] += _dot_tn(mb, dyb)
        dm = _dot_nt(dyb, wo_ref[...])
        dpa = (dm * ga).astype(bf16)
        dpb = (dm * gb).astype(bf16)
        dp_ref[:, :D] = (dm * pa * ga * (1.0 - ga)).astype(bf16)
        dp_ref[:, D:] = (dm * pb * gb * (1.0 - gb)).astype(bf16)
        dwa_ref[...] += _dot_tn(a_v, dpa)
        dwb_ref[...] += _dot_tn(ob_v, dpb)
        da_ref[...] = _dot_nt(dpa, wa_ref[...])
        dob_ref[...] = _dot_nt(dpb, wb_ref[...])

    row = lambda: pl.BlockSpec((tm, D), lambda i: (i, 0))
    return pl.pallas_call(
        body,
        grid=(T // tm,),
        in_specs=[row(), row(), pl.BlockSpec((tm, 2 * D), lambda i: (i, COL_GATES)), row(), row(),
                  _resident((D, D)), _resident((D, D)), _resident((D, D)), _resident((1, D))],
        out_specs=[pl.BlockSpec((tm, 2 * D), lambda i: (i, COL_GATES)), row(), row(), row(),
                   _resident((D, D)), _resident((D, D)), _resident((D, D)), _resident((1, D)), _resident((1, D))],
        out_shape=[jax.ShapeDtypeStruct((T, NMAIN), bf16),
                   jax.ShapeDtypeStruct((T, D), f32), jax.ShapeDtypeStruct((T, D), f32),
                   jax.ShapeDtypeStruct((T, D), f32),
                   jax.ShapeDtypeStruct((D, D), f32), jax.ShapeDtypeStruct((D, D), f32),
                   jax.ShapeDtypeStruct((D, D), f32),
                   jax.ShapeDtypeStruct((1, D), f32), jax.ShapeDtypeStruct((1, D), f32)],
        compiler_params=_params(("arbitrary",)),
        name="merge_fwd_bwd",
    )(x, tgt, proj, a, ob, w_a, w_b, w_o, g_f)


def _dx_bwd(x, dy, dproj, dlr, g0, w_t, riders=(), sends_what=(), tm=256):
    T = x.shape[0]
    nsteps = T // tm
    n = len(riders)

    def body(x_ref, dy_ref, dp_ref, dl_ref, g_ref, w_ref, *rest):
        srcs, (dx_ref, dg_ref), dsts, sems = rest[:n], rest[n:n + 2], rest[n + 2:2 * n + 2], rest[2 * n + 2:]

        @pl.when(pl.program_id(0) == 0)
        def _():
            dg_ref[...] = jnp.zeros_like(dg_ref)
            for cp in _exchange_copies(srcs, dsts, sends_what, sems) if n else []:
                cp.start()

        xv = x_ref[...]
        r = lax.rsqrt(jnp.mean(xv * xv, axis=-1, keepdims=True) + EPS)
        xhat = xv * r
        dh = (_dot(dp_ref[:, :LR_COL], w_ref[:LR_COL, :]) + _dot(dp_ref[:, LR_COL:], w_ref[LR_COL + RANK:, :])
              + _dot(dl_ref[...], w_ref[LR_ROWS, :]))
        dg_ref[...] += jnp.sum(dh * xhat, axis=0, keepdims=True)
        t = dh * g_ref[...]
        dx_ref[...] = dy_ref[...] + r * (t - xhat * jnp.mean(t * xhat, axis=-1, keepdims=True))

        if n:
            @pl.when(pl.program_id(0) == nsteps - 1)
            def _():
                for cp in _exchange_copies(srcs, dsts, sends_what, sems):
                    cp.wait()

    row = lambda: pl.BlockSpec((tm, D), lambda i: (i, 0))
    return pl.pallas_call(
        body,
        grid=(nsteps,),
        in_specs=[row(), row(), pl.BlockSpec((tm, NMAIN), lambda i: (i, 0)),
                  pl.BlockSpec((tm, LRP), lambda i: (i, 0)),
                  _resident((1, D)), _resident((NMAIN + RANK, D))] + [_ANY] * n,
        out_specs=[row(), _resident((1, D))] + [_ANY] * n,
        out_shape=[jax.ShapeDtypeStruct((T, D), f32), jax.ShapeDtypeStruct((1, D), f32)]
        + _exchange_shapes(riders, sends_what),
        scratch_shapes=_exchange_sems(n) if n else [],
        compiler_params=_params(("arbitrary",)),
        name="dx_bwd",
    )(x, dy, dproj, dlr, g0, w_t, *riders)


def _dw_in(h, dproj, dlr, pair_dtype=None, tm=1024, tn=1024):
    T = h.shape[0]
    tm = min(tm, T)
    nj, nk = NMAIN // tn, T // tm
    lr_tile = LR_COL // tn
    pair = pair_dtype is not None
    hd = D // 2

    def body(h_ref, dp_ref, dl_ref, out_ref, acc, lr_acc, sems, lr_sem, *more):
        j, k = pl.program_id(0), pl.program_id(1)
        slot = j % 2

        def tile_row(jj):
            return pl.multiple_of(jj * tn + jnp.where(jj >= lr_tile, RANK, 0), 8)

        if pair:
            land, lr_land, part_buf, lr_part, swap_send, swap_recv = more
            c = lax.axis_index("c")
            mine = pl.ds(pl.multiple_of(c * hd, 128), hd)
            other = pl.ds(pl.multiple_of((1 - c) * hd, 128), hd)

            def tile_swap(jj, s):
                return _remote(acc.at[s, :, other], land.at[jj], swap_send.at[jj], swap_recv.at[jj], _sibling())

            def lr_swap():
                return _remote(lr_acc.at[pl.ds(0, RANK), other], lr_land, swap_send.at[nj], swap_recv.at[nj], _sibling())

            def tile_out(jj, s):
                return pltpu.make_async_copy(part_buf.at[s], out_ref.at[pl.ds(tile_row(jj), tn)], sems.at[s])

            def finish_tile(jj, s):
                tile_swap(jj, s).wait()
                part_buf[s] = (acc[s, :, mine] + land[jj]).astype(pair_dtype)
                tile_out(jj, s).start()

            lr_out = pltpu.make_async_copy(lr_part, out_ref.at[pl.ds(LR_COL, RANK)], lr_sem)
        else:
            def tile_out(jj, s):
                return pltpu.make_async_copy(acc.at[s], out_ref.at[pl.ds(tile_row(jj), tn)], sems.at[s])

            lr_out = pltpu.make_async_copy(lr_acc.at[pl.ds(0, RANK)], out_ref.at[pl.ds(LR_COL, RANK)], lr_sem)

        @pl.when(j == 0)
        def _():
            @pl.when(k == 0)
            def _():
                lr_acc[...] = jnp.zeros_like(lr_acc)

            lr_acc[...] += _dot_tn(dl_ref[...], h_ref[...])

            @pl.when(k == nk - 1)
            def _():
                if pair:
                    lr_swap().start()
                else:
                    lr_out.start()

        @pl.when(k == 0)
        def _():
            acc[slot] = jnp.zeros((tn, D), f32)

        acc[slot] += _dot_tn(dp_ref[...], h_ref[...])

        @pl.when(k == nk - 1)
        def _():
            if pair:
                tile_swap(j, slot).start()

                @pl.when(j >= 3)
                def _():
                    tile_out(j - 3, 1 - slot).wait()

                @pl.when(j >= 1)
                def _():
                    finish_tile(j - 1, 1 - slot)

                @pl.when(j == nj - 1)
                def _():
                    tile_out(j - 2, slot).wait()
                    finish_tile(j, slot)
                    lr_swap().wait()
                    lr_part[...] = (lr_acc[0:RANK, mine] + lr_land[...]).astype(pair_dtype)
                    lr_out.start()
                    tile_out(j - 1, 1 - slot).wait()
                    tile_out(j, slot).wait()
                    lr_out.wait()
            else:
                tile_out(j, slot).start()

                @pl.when(j > 0)
                def _():
                    tile_out(j - 1, 1 - slot).wait()

                @pl.when(j == nj - 1)
                def _():
                    tile_out(j, slot).wait()
                    lr_out.wait()

    pair_scratch = [pltpu.VMEM((nj, tn, hd), f32), pltpu.VMEM((RANK, hd), f32), pltpu.VMEM((2, tn, hd), pair_dtype),
                    pltpu.VMEM((RANK, hd), pair_dtype)] + [pltpu.SemaphoreType.DMA((nj + 1,))] * 2 if pair else []
    return pl.pallas_call(
        body,
        grid=(nj, nk),
        in_specs=[pl.BlockSpec((tm, D), lambda j, k: (k, 0)), pl.BlockSpec((tm, tn), lambda j, k: (k, j)),
                  pl.BlockSpec((tm, LRP), lambda j, k: (k, 0))],
        out_specs=_ANY,
        out_shape=jax.ShapeDtypeStruct((NMAIN + RANK, hd), pair_dtype) if pair
        else jax.ShapeDtypeStruct((NMAIN + RANK, D), f32),
        scratch_shapes=[pltpu.VMEM((2, tn, D), f32), pltpu.VMEM((LRP, D), f32),
                        pltpu.SemaphoreType.DMA((2,)), pltpu.SemaphoreType.DMA] + pair_scratch,
        compiler_params=_params(("arbitrary", "arbitrary")),
        name="dw_in",
    )(h, dproj, dlr)


MESH = pl.DeviceIdType.MESH


def _place():
    x, y, c = lax.axis_index("x"), lax.axis_index("y"), lax.axis_index("c")
    others = [(1 - x, y), (x, 1 - y), (1 - x, 1 - y)]
    return x, y, c, 2 * x + y, others


def _sibling():
    return lax.axis_index("x"), lax.axis_index("y"), 1 - lax.axis_index("c")


def _remote(src, dst, send_sem, recv_sem, to):
    return pltpu.make_async_remote_copy(src_ref=src, dst_ref=dst, send_sem=send_sem, recv_sem=recv_sem,
                                        device_id=to, device_id_type=MESH)


def _half(ref, e, by_columns):
    if not by_columns:
        return ref.at[e]
    hw = ref.shape[-1] // 2
    return ref.at[:, pl.ds(pl.multiple_of(e * hw, 128), hw)]


RELAY_ROWS = 1024


def _gather_win(shard):
    h, w = shard.shape
    part_a, part_b = pl.ds(0, RELAY_ROWS), pl.ds(RELAY_ROWS, h - RELAY_ROWS)

    def body(src, dst, send_sems, recv_sems, relay_send, relay_recv, pass_send, pass_recv, own_sems, stage):
        x, y, c, me, others = _place()
        (to_x, to_y, _), sibling = others, (x, y, 1 - c)
        j_x, j_y, j_d = (2 * cx + cy for cx, cy in others)
        cols = pl.ds(pl.multiple_of(c * (w // 2), 128), w // 2)
        theirs = pl.ds(pl.multiple_of((1 - c) * (w // 2), 128), w // 2)

        def mine(j, rows=pl.ds(0, h)):
            return dst.at[j, rows, cols]

        to_stage = pltpu.make_async_copy(src, stage, own_sems.at[0])
        to_slot = pltpu.make_async_copy(stage, dst.at[me], own_sems.at[1])
        sends = [_remote(src.at[:, cols], mine(me), send_sems.at[k], recv_sems.at[k], (*to, c))
                 for k, to in enumerate((to_x, to_y))]
        for cp in sends:
            cp.start()
        to_stage.start()
        relays = [_remote(mine(j_x, part_a), mine(j_x, part_a), relay_send.at[0], relay_recv.at[0], (*to_y, c)),
                  _remote(mine(j_y, part_b), mine(j_y, part_b), relay_send.at[1], relay_recv.at[1], (*to_x, c))]
        landed = [mine(j_x), mine(j_y), mine(j_d, part_a), mine(j_d, part_b)]
        passes = [_remote(place, place, pass_send.at[k], pass_recv.at[k], sibling) for k, place in enumerate(landed)]
        for k in range(2):
            _remote(src.at[:, cols], landed[k], send_sems.at[k], recv_sems.at[k], sibling).wait_recv()
            relays[k].start()
            passes[k].start()
        to_stage.wait()
        to_slot.start()
        for k in range(2):
            _remote(landed[2 + k], landed[2 + k], relay_send.at[k], relay_recv.at[k], sibling).wait_recv()
            passes[2 + k].start()
        for k, place in enumerate([(j_x, pl.ds(0, h)), (j_y, pl.ds(0, h)), (j_d, part_a), (j_d, part_b)]):
            got = dst.at[place[0], place[1], theirs]
            _remote(got, got, pass_send.at[k], pass_recv.at[k], sibling).wait_recv()
        for cp in sends + relays + passes:
            cp.wait_send()
        to_slot.wait()

    return pl.pallas_call(
        body,
        in_specs=[_ANY],
        out_specs=_ANY,
        out_shape=jax.ShapeDtypeStruct((NCHIP, h, w), shard.dtype),
        scratch_shapes=[pltpu.SemaphoreType.DMA((2,))] * 4 + [pltpu.SemaphoreType.DMA((4,))] * 2
        + [pltpu.SemaphoreType.DMA((2,)), pltpu.VMEM((h, w), shard.dtype)],
        compiler_params=pltpu.CompilerParams(vmem_limit_bytes=VMEM_LIMIT),
        name="gather_win",
    )(shard)


def _gathered_shapes(shards):
    return [jax.ShapeDtypeStruct((NCHIP,) + s.shape, s.dtype) for s in shards]


def _gather_scratch(shards):
    n = len(shards)
    return ([pltpu.SemaphoreType.DMA((3, n))] * 4 + [pltpu.SemaphoreType.DMA((2, n))]
            + [pltpu.VMEM(s.shape, s.dtype) for s in shards])


def _own_to_stage(srcs, scratch):
    own_sems, stages = scratch[4], scratch[5:]
    return [pltpu.make_async_copy(srcs[a], stages[a], own_sems.at[0, a]) for a in range(len(srcs))]


def _own_to_slot(dsts, scratch):
    own_sems, stages = scratch[4], scratch[5:]
    me = _place()[3]
    return [pltpu.make_async_copy(stages[a], dsts[a].at[me], own_sems.at[1, a]) for a in range(len(dsts))]


def _gather_copies(srcs, dsts, by_columns, sems, sends_only):
    n = len(srcs)
    send_sems, recv_sems, pass_send, pass_recv = sems[:4]
    x, y, c, me, others = _place()
    sibling = (x, y, 1 - c)

    def src(a, e):
        return _half(srcs[a], e, by_columns[a])

    def dst(a, j, e):
        return _half(dsts[a].at[j], e, by_columns[a])

    sends, arrivals, passes, passed = [], [], [], []
    for k, (cx, cy) in enumerate(others):
        j = 2 * cx + cy
        for a in range(n):
            sends.append(_remote(src(a, c), dst(a, me, c), send_sems.at[k, a], recv_sems.at[k, a], (cx, cy, c)))
            if not sends_only:
                arrivals.append(_remote(src(a, c), dst(a, j, c), send_sems.at[k, a], recv_sems.at[k, a], (cx, cy, c)))
                passes.append(_remote(dst(a, j, c), dst(a, j, c), pass_send.at[k, a], pass_recv.at[k, a], sibling))
                passed.append(_remote(src(a, c), dst(a, j, 1 - c), pass_send.at[k, a], pass_recv.at[k, a], sibling))
    return sends, arrivals, passes, passed


def _gather_start(srcs, dsts, by_columns, scratch):
    for cp in _gather_copies(srcs, dsts, by_columns, scratch, sends_only=True)[0]:
        cp.start()
    for cp in _own_to_stage(srcs, scratch):
        cp.start()


def _gather_finish(srcs, dsts, by_columns, scratch):
    for cp in _own_to_stage(srcs, scratch):
        cp.wait()
    own = _own_to_slot(dsts, scratch)
    for cp in own:
        cp.start()
    sends, arrivals, passes, passed = _gather_copies(srcs, dsts, by_columns, scratch, sends_only=False)
    for arrival, cp in zip(arrivals, passes):
        arrival.wait_recv()
        cp.start()
    for arrival in passed:
        arrival.wait_recv()
    for cp in sends + passes:
        cp.wait_send()
    for cp in own:
        cp.wait()


HALF_FIRST, CHIP_FIRST, BY_COLUMNS = "half_first", "chip_first", "by_columns"


def _sibling_halves(bufs, kinds):
    n = len(bufs)

    def body(*refs):
        cps = _halves_copies(refs[:n], refs[n:2 * n], kinds, refs[2 * n:])
        for cp in cps:
            cp.start()
        for cp in cps:
            cp.wait()

    return pl.pallas_call(
        body,
        in_specs=[_ANY] * n,
        out_specs=[_ANY] * n,
        out_shape=_halves_shapes(bufs, kinds),
        scratch_shapes=_halves_sems(bufs),
        name="sibling_halves",
    )(*bufs)


def _halves_shapes(bufs, kinds):
    def landed(b, kind):
        if kind == HALF_FIRST:
            return b.shape[1:]
        if kind == CHIP_FIRST:
            return (b.shape[0],) + b.shape[2:]
        return b.shape[:2] + (b.shape[2] // 2,)

    return [jax.ShapeDtypeStruct(landed(b, kind), b.dtype) for b, kind in zip(bufs, kinds)]


def _halves_sems(bufs):
    return [pltpu.SemaphoreType.DMA((len(bufs), NCHIP))] * 2


def _halves_copies(srcs, dsts, kinds, sems):
    send_sems, recv_sems = sems
    x, y, c, _, _ = _place()
    cps = []
    for a, kind in enumerate(kinds):
        if kind == HALF_FIRST:
            cps.append(_remote(srcs[a].at[1 - c], dsts[a], send_sems.at[a, 0], recv_sems.at[a, 0], (x, y, 1 - c)))
        else:
            cps += [_remote(_half(srcs[a].at[j], 1 - c, kind == BY_COLUMNS), dsts[a].at[j],
                            send_sems.at[a, j], recv_sems.at[a, j], (x, y, 1 - c)) for j in range(srcs[a].shape[0])]
    return cps


TO_ITS_CHIP, TO_EVERY_CHIP, ROWS_TO_ITS_CHIP = "to_its_chip", "to_every_chip", "rows_to_its_chip"
PIECE_STEP = 2048
PIECE_ROWS = 2064


def _chip_exchange(parts, sends_what):
    n = len(parts)

    def body(*refs):
        cps = _exchange_copies(refs[:n], refs[n:2 * n], sends_what, refs[2 * n:])
        for cp in cps:
            cp.start()
        for cp in cps:
            cp.wait()

    return pl.pallas_call(
        body,
        in_specs=[_ANY] * n,
        out_specs=[_ANY] * n,
        out_shape=_exchange_shapes(parts, sends_what),
        scratch_shapes=_exchange_sems(n),
        name="chip_exchange",
    )(*parts)


def _exchange_shapes(parts, sends_what):
    def landed(p, what):
        return (3, PIECE_ROWS, p.shape[1]) if what == ROWS_TO_ITS_CHIP else (3,) + p.shape[1:]

    return [jax.ShapeDtypeStruct(landed(p, what), p.dtype) for p, what in zip(parts, sends_what)]


def _exchange_sems(n):
    return [pltpu.SemaphoreType.DMA((3, n))] * 2


def _exchange_copies(srcs, dsts, sends_what, sems):
    send_sems, recv_sems = sems
    x, y, c, me, others = _place()

    def part(a, j):
        if sends_what[a] == ROWS_TO_ITS_CHIP:
            return srcs[a].at[pl.ds(pl.multiple_of(j * PIECE_STEP, PIECE_STEP), PIECE_ROWS)]
        return srcs[a].at[j if sends_what[a] == TO_ITS_CHIP else 0]

    return [_remote(part(a, 2 * cx + cy), dsts[a].at[k], send_sems.at[k, a], recv_sems.at[k, a], (cx, cy, c))
            for k, (cx, cy) in enumerate(others) for a in range(len(srcs))]


def _sibling_swap(halves):
    n = len(halves)

    def body(*refs):
        srcs, dsts = refs[:n], refs[n:2 * n]
        send_sems, recv_sems = refs[2 * n:]
        x, y, c, _, _ = _place()
        cps = [_remote(srcs[a], dsts[a], send_sems.at[a], recv_sems.at[a], (x, y, 1 - c)) for a in range(n)]
        for cp in cps:
            cp.start()
        for cp in cps:
            cp.wait()

    return pl.pallas_call(
        body,
        in_specs=[_ANY] * n,
        out_specs=[_ANY] * n,
        out_shape=[jax.ShapeDtypeStruct(s.shape, s.dtype) for s in halves],
        scratch_shapes=[pltpu.SemaphoreType.DMA((n,))] * 2,
        name="sibling_swap",
    )(*halves)


def _tile(h, w, operands=5):
    if h % 128 == 0:
        return 128, w
    budget = VMEM_LIMIT * 3 // 4 // (2 * operands * 4)
    tw = w
    while h * tw > budget and tw % 256 == 0:
        tw //= 2
    return h, tw


def _pair_sum(place, buf, kind, got, out_dtype):
    nj, h, w = got.shape
    th, tw = _tile(h, w)
    nq = w // tw

    def body(p_ref, a_ref, b_ref, o_ref):
        del p_ref
        o_ref[...] = (a_ref[...] + b_ref[...]).astype(out_dtype)

    if kind == HALF_FIRST:
        mine = pl.BlockSpec((None, None, th, tw), lambda j, r, q, p: (p[0], j, r, q))
    elif kind == CHIP_FIRST:
        mine = pl.BlockSpec((None, None, th, tw), lambda j, r, q, p: (j, p[0], r, q))
    else:
        mine = pl.BlockSpec((None, th, tw), lambda j, r, q, p: (j, r, p[0] * nq + q))
    return pl.pallas_call(
        body,
        grid_spec=pltpu.PrefetchScalarGridSpec(
            num_scalar_prefetch=1,
            grid=(nj, h // th, w // tw),
            in_specs=[mine, pl.BlockSpec((None, th, tw), lambda j, r, q, p: (j, r, q))],
            out_specs=pl.BlockSpec((None, th, tw), lambda j, r, q, p: (j, r, q)),
        ),
        out_shape=jax.ShapeDtypeStruct((nj, h, w), out_dtype),
        compiler_params=_params(("parallel", "parallel", "parallel")),
        name="pair_sum",
    )(place, buf, got)


def _chip_sum(place, part, slots):
    nj, h, w = part.shape
    th, tw = _tile(h, w)

    def body(p_ref, own_ref, s_ref, o_ref):
        me = p_ref[1]
        own = own_ref[...].astype(f32)
        by_flip = {2: s_ref[0].astype(f32), 1: s_ref[1].astype(f32), 3: s_ref[2].astype(f32)}
        acc = None
        for j in range(NCHIP):
            flip = me ^ j
            term = jnp.where(flip == 0, own, jnp.where(flip == 2, by_flip[2], jnp.where(flip == 1, by_flip[1], by_flip[3])))
            acc = term if acc is None else acc + term
        o_ref[...] = acc

    return pl.pallas_call(
        body,
        grid_spec=pltpu.PrefetchScalarGridSpec(
            num_scalar_prefetch=1,
            grid=(h // th, w // tw),
            in_specs=[pl.BlockSpec((None, th, tw), lambda r, q, p: (p[1] if nj == NCHIP else 0, r, q)),
                      pl.BlockSpec((3, th, tw), lambda r, q, p: (0, r, q))],
            out_specs=pl.BlockSpec((th, tw), lambda r, q, p: (r, q)),
        ),
        out_shape=jax.ShapeDtypeStruct((h, w), f32),
        compiler_params=_params(("parallel", "parallel")),
        name="chip_sum",
    )(place, part, slots)


def _adamw_math(w, g, m, v):
    nm = ADAM_B1 * m + (1.0 - ADAM_B1) * g
    nv = ADAM_B2 * v + (1.0 - ADAM_B2) * (g * g)
    m_hat = nm / (1.0 - ADAM_B1 ** ADAM_STEP)
    v_hat = nv / (1.0 - ADAM_B2 ** ADAM_STEP)
    return -ADAM_LR * (m_hat / (jnp.sqrt(v_hat) + ADAM_EPS) + ADAM_WD * w), nm, nv


def _adamw(w, g, m, v):
    rows, width = w.shape
    th, tw = _tile(rows, width, operands=7)

    def body(w_ref, g_ref, m_ref, v_ref, d_ref, nm_ref, nv_ref):
        d_ref[...], nm_ref[...], nv_ref[...] = _adamw_math(w_ref[...], g_ref[...], m_ref[...], v_ref[...])

    spec = pl.BlockSpec((th, tw), lambda r, q: (r, q))
    return pl.pallas_call(
        body,
        grid=(rows // th, width // tw),
        in_specs=[spec] * 4,
        out_specs=[spec] * 3,
        out_shape=[jax.ShapeDtypeStruct((rows, width), f32)] * 3,
        compiler_params=_params(("parallel", "parallel")),
        name="adamw",
    )(w, g, m, v)


def _adamw_halves(place, w, mine, got, m, v, axis):
    rows, width = w.shape
    h, hw = mine.shape
    th, tw = _tile(h, hw, operands=10)
    nr, nq = h // th, hw // tw

    def body(p_ref, w_ref, a_ref, b_ref, m_ref, v_ref, g_ref, d_ref, nm_ref, nv_ref):
        g = jnp.where(pl.program_id(0) == p_ref[0], a_ref[...], b_ref[...])
        g_ref[...] = g
        d_ref[...], nm_ref[...], nv_ref[...] = _adamw_math(w_ref[...], g, m_ref[...], v_ref[...])

    if axis == 0:
        full = pl.BlockSpec((th, tw), lambda e, r, q, p: (e * nr + r, q))
    else:
        full = pl.BlockSpec((th, tw), lambda e, r, q, p: (r, e * nq + q))
    half = pl.BlockSpec((th, tw), lambda e, r, q, p: (r, q))
    return pl.pallas_call(
        body,
        grid_spec=pltpu.PrefetchScalarGridSpec(
            num_scalar_prefetch=1,
            grid=(2, nr, nq),
            in_specs=[full, half, half, full, full],
            out_specs=[full] * 4,
        ),
        out_shape=[jax.ShapeDtypeStruct((rows, width), f32)] * 4,
        compiler_params=_params(("parallel", "parallel", "parallel")),
        name="adamw_halves",
    )(place, w, mine, got, m, v)


_SMALL = (("norm_g", 8), ("ln_v_g", 8), ("ln_v_b", 8), ("w_spatial", 1024), ("b_spatial", 8), ("b_gate_up", 4),
          ("gla_norm_g", 2), ("final_norm_g", 8), ("w_gate_up", 64), ("loss", 8))
_SMALL_ROWS = 1152


def _pack_rows(arrays, rows):
    flat = jnp.concatenate([a.reshape(-1, 128) for a in arrays], axis=0)
    return jnp.pad(flat, ((0, rows - flat.shape[0]), (0, 0)))


def kernel(x, norm_g, w_in, ln_v_g, ln_v_b, w_spatial, b_spatial, w_gate_up, b_gate_up, gla_norm_g, w_branch_a, w_branch_b, w_out, final_norm_g, loss_target, m_norm_g, m_w_in, m_ln_v_g, m_ln_v_b, m_w_spatial, m_b_spatial, m_w_gate_up, m_b_gate_up, m_gla_norm_g, m_w_branch_a, m_w_branch_b, m_w_out, m_final_norm_g, v_norm_g, v_w_in, v_ln_v_g, v_ln_v_b, v_w_spatial, v_b_spatial, v_w_gate_up, v_b_gate_up, v_gla_norm_g, v_w_branch_a, v_w_branch_b, v_w_out, v_final_norm_g):
    chip = 2 * lax.axis_index("x") + lax.axis_index("y")
    core = lax.axis_index("c")
    place = jnp.stack([core, chip]).astype(jnp.int32)
    mat_names = ("w_branch_a", "w_branch_b", "w_out")

    wt_shard = jnp.transpose(w_in[0]).astype(bf16)
    mats = [w[0].astype(bf16).reshape(2, D // NCHIP // 2, D) for w in (w_branch_a, w_branch_b, w_out)]
    gate_sh = w_gate_up[0].reshape(2, RANK // 2, 128)
    g_win = _gather_win(wt_shard)
    w_t = g_win.reshape(NCHIP * WIN_SHARD, D)
    b_sb = jnp.broadcast_to(b_spatial[0][:, :, None], (HA, CA, GA))
    xs, tgt = x[0], loss_target[0]

    proj, lr, h, g_a, g_b, g_o, g_gate = _proj_fwd(xs, norm_g, w_t, riders=mats + [gate_sh])
    w_a, w_b, w_o = (g.reshape(D, D) for g in (g_a, g_b, g_o))
    w_gate = jnp.transpose(g_gate.reshape(NCHIP, RANK, 128), (1, 0, 2)).reshape(RANK, KEYB)
    w_gate = jnp.pad(w_gate, ((0, LRP - RANK), (0, 0)))
    a = _mixer_a_fwd(proj, ln_v_g, ln_v_b, w_spatial[0], b_sb)
    o, ob, states = _gla_fwd(proj, lr, w_gate, b_gate_up, gla_norm_g)
    dproj, dy, da, dob, dwa, dwb, dwo, dgf, loss_cols = _merge_fwd_bwd(xs, tgt, proj, a, ob, w_a, w_b, w_o,
                                                                       final_norm_g.reshape(1, D))
    b_mats = [t.reshape(NCHIP, 2, D // NCHIP // 2, D) for t in (dwa, dwb, dwo)]
    dproj, dws, dbs, dlg, dlb, *got_mats = _mixer_a_bwd(proj, da, dproj, ln_v_g, ln_v_b, w_spatial[0], b_sb,
                                                        riders=b_mats, kinds=[CHIP_FIRST] * 3)
    part_mats = [_pair_sum(place, b, CHIP_FIRST, g, bf16) for b, g in zip(b_mats, got_mats)]
    dproj, dlr, dwg, dbg, dgg, *slots_mats = _gla_bwd(proj, lr, o, states, dob, dproj, w_gate, b_gate_up, gla_norm_g,
                                                      riders=part_mats, sends_what=[TO_ITS_CHIP] * 3)
    part_win = _dw_in(h, dproj, dlr, pair_dtype=bf16)
    dx, dg0, slots_win = _dx_bwd(xs, dy, dproj, dlr, norm_g, w_t, riders=[part_win], sends_what=[ROWS_TO_ITS_CHIP])
    slots_big = [slots_win] + slots_mats
    small = _pack_rows([dg0, dlg, dlb, dws, dbs[:, :, 0], dbg, dgg, dgf, dwg[:RANK], loss_cols], _SMALL_ROWS)
    b_small = small.reshape(2, 1, _SMALL_ROWS // 2, 128)
    (got_small,) = _sibling_halves([b_small], [HALF_FIRST])
    part_small = _pair_sum(place, b_small, HALF_FIRST, got_small, f32)
    (slots_small,) = _chip_exchange([part_small], [TO_EVERY_CHIP])
    own_win = lax.dynamic_slice_in_dim(part_win, chip * PIECE_STEP, PIECE_ROWS, axis=0)[None]
    mine = [_chip_sum(place, p, s) for p, s in zip([own_win] + part_mats + [part_small], slots_big + [slots_small])]
    theirs = list(_sibling_swap(mine))
    mine[0], theirs[0] = (lax.dynamic_slice_in_dim(t, (WIN_SHARD - PIECE_STEP) * chip, WIN_SHARD, axis=0)
                          for t in (mine[0], theirs[0]))

    g_small = jnp.where(core == 0, jnp.concatenate([mine[4], theirs[4]], axis=0),
                        jnp.concatenate([theirs[4], mine[4]], axis=0))
    grads = {}
    row = 0
    for name, rows in _SMALL:
        grads[name] = g_small[row:row + rows]
        row += rows
    loss = jnp.sum(grads["loss"])
    dwg_full = grads["w_gate_up"].reshape(RANK, KEYB)
    grads["w_gate_up"] = lax.dynamic_slice_in_dim(dwg_full, chip * 128, 128, axis=1)

    weights = dict(norm_g=norm_g, w_in=w_in, ln_v_g=ln_v_g, ln_v_b=ln_v_b, w_spatial=w_spatial, b_spatial=b_spatial,
                   w_gate_up=w_gate_up, b_gate_up=b_gate_up, gla_norm_g=gla_norm_g, w_branch_a=w_branch_a,
                   w_branch_b=w_branch_b, w_out=w_out, final_norm_g=final_norm_g)
    m_in = dict(norm_g=m_norm_g, w_in=m_w_in, ln_v_g=m_ln_v_g, ln_v_b=m_ln_v_b, w_spatial=m_w_spatial,
                b_spatial=m_b_spatial, w_gate_up=m_w_gate_up, b_gate_up=m_b_gate_up, gla_norm_g=m_gla_norm_g,
                w_branch_a=m_w_branch_a, w_branch_b=m_w_branch_b, w_out=m_w_out, final_norm_g=m_final_norm_g)
    v_in = dict(norm_g=v_norm_g, w_in=v_w_in, ln_v_g=v_ln_v_g, ln_v_b=v_ln_v_b, w_spatial=v_w_spatial,
                b_spatial=v_b_spatial, w_gate_up=v_w_gate_up, b_gate_up=v_b_gate_up, gla_norm_g=v_gla_norm_g,
                w_branch_a=v_w_branch_a, w_branch_b=v_w_branch_b, w_out=v_w_out, final_norm_g=v_final_norm_g)
    names = list(weights)
    small_names = [n for n in names if n != "w_in" and n not in mat_names]
    out_g, out_d, out_m, out_v = {}, {}, {}, {}
    res = _adamw_halves(place, jnp.transpose(w_in[0]), mine[0], theirs[0], jnp.transpose(m_w_in[0]),
                        jnp.transpose(v_w_in[0]), axis=1)
    out_g["w_in"], out_d["w_in"], out_m["w_in"], out_v["w_in"] = (jnp.transpose(t)[None] for t in res)
    for i, n in enumerate(mat_names):
        res = _adamw_halves(place, weights[n][0], mine[1 + i], theirs[1 + i], m_in[n][0], v_in[n][0], axis=0)
        out_g[n], out_d[n], out_m[n], out_v[n] = (t[None] for t in res)
    upd_rows = sum(weights[n].size for n in small_names) // 128
    pad_rows = -(-upd_rows // 8) * 8
    packed = [_pack_rows([t[n] for n in small_names], pad_rows) for t in (weights, grads, m_in, v_in)]
    d_s, m_s, v_s = _adamw(*packed)
    row = 0
    for n in small_names:
        shape = weights[n].shape
        rows = weights[n].size // 128
        out_g[n] = grads[n].reshape(shape)
        out_d[n], out_m[n], out_v[n] = (t[row:row + rows].reshape(shape) for t in (d_s, m_s, v_s))
        row += rows
    return (loss, dx[None], *[out_g[n] for n in names], *[out_d[n] for n in names],
            *[out_m[n] for n in names], *[out_v[n] for n in names])
```

```python
import functools
import math

import jax
import jax.numpy as jnp
from jax import lax
from jax.experimental import pallas as pl
from jax.experimental.pallas import tpu as pltpu

f32 = jnp.float32
bf16 = jnp.bfloat16

D = 1024
NMAIN = 8192
LRP = 128
RANK = 16
HA, GA, CA = 8, 128, 128
HB, DK, DV, CB = 4, 128, 256, 64
KEYB = HB * DK
EPS = 1e-6
LN_EPS = 1e-5
GATE_NORM = 16.0
QSCALE = DK ** -0.5
COL_U, COL_V, COL_ZA = 0, 1, 2
COL_Q, COL_K = 6, 7
COL_VB, COL_ZB = 4, 5
COL_GATES = 3
VMEM_LIMIT = 56 * 1024 * 1024
NCHIP = 4
WIN_SHARD = 2052
LR_COL = 6144
_ANY = pl.BlockSpec(memory_space=pl.ANY)

ADAM_LR, ADAM_B1, ADAM_B2, ADAM_EPS, ADAM_WD, ADAM_STEP = 0.001, 0.9, 0.999, 1e-08, 0.01, 10

_SQRT_HALF = 0.7071067811865476
_INV_SQRT_2PI = 0.3989422804014327


def _dot(a, b):
    return jnp.dot(a, b, preferred_element_type=f32)


def _dot_nt(a, b):
    return lax.dot_general(a, b, (((1,), (1,)), ((), ())), preferred_element_type=f32)


def _dot_tn(a, b):
    return lax.dot_general(a, b, (((0,), (0,)), ((), ())), preferred_element_type=f32)


def _dot_exact(a, b):
    return jnp.dot(a, b, preferred_element_type=f32, precision=lax.Precision.HIGHEST)


def _gelu(x):
    return 0.5 * x * (1.0 + lax.erf(x * _SQRT_HALF))


def _gelu_and_grad(x):
    cdf = 0.5 * (1.0 + lax.erf(x * _SQRT_HALF))
    return x * cdf, cdf + x * (jnp.exp(-0.5 * x * x) * _INV_SQRT_2PI)


def _sigmoid(x):
    return 0.5 * jnp.tanh(0.5 * x) + 0.5


def _params(sem):
    return pltpu.CompilerParams(dimension_semantics=sem, vmem_limit_bytes=VMEM_LIMIT)


def _resident(shape):
    nd = len(shape)
    return pl.BlockSpec(shape, lambda *_: (0,) * nd, pipeline_mode=pl.Buffered(1))


def _w_rows(c, tn):
    start = c * tn + (RANK if c * tn >= LR_COL else 0)
    return slice(start, start + tn)


LR_ROWS = slice(LR_COL, LR_COL + LRP)


def _proj_fwd(x, g0, w_t, riders=(), tm=256, tn=1024):
    T = x.shape[0]
    nsteps = T // tm
    n = len(riders)

    def body(x_ref, g_ref, w_ref, *rest):
        srcs, (proj_ref, lr_ref, h_ref), dsts, sems = rest[:n], rest[n:n + 3], rest[n + 3:2 * n + 3], rest[2 * n + 3:]
        if n:
            @pl.when(pl.program_id(0) == 0)
            def _():
                _gather_start(srcs, dsts, [False] * n, sems)

        xv = x_ref[...]
        r = lax.rsqrt(jnp.mean(xv * xv, axis=-1, keepdims=True) + EPS)
        h = (xv * r * g_ref[...]).astype(bf16)
        h_ref[...] = h
        lr_ref[...] = _dot_nt(h, w_ref[LR_ROWS, :])
        for c in range(NMAIN // tn):
            proj_ref[:, c * tn:(c + 1) * tn] = _dot_nt(h, w_ref[_w_rows(c, tn), :])

        if n:
            @pl.when(pl.program_id(0) == nsteps - 1)
            def _():
                _gather_finish(srcs, dsts, [False] * n, sems)

    return pl.pallas_call(
        body,
        grid=(nsteps,),
        in_specs=[
            pl.BlockSpec((tm, D), lambda i: (i, 0)),
            _resident((1, D)), _resident((NMAIN + RANK, D)),
        ] + [_ANY] * n,
        out_specs=[
            pl.BlockSpec((tm, NMAIN), lambda i: (i, 0)),
            pl.BlockSpec((tm, LRP), lambda i: (i, 0)),
            pl.BlockSpec((tm, D), lambda i: (i, 0)),
        ] + [_ANY] * n,
        out_shape=[
            jax.ShapeDtypeStruct((T, NMAIN), f32),
            jax.ShapeDtypeStruct((T, LRP), f32),
            jax.ShapeDtypeStruct((T, D), bf16),
        ] + _gathered_shapes(riders),
        scratch_shapes=_gather_scratch(riders) if n else [],
        compiler_params=_params(("arbitrary",)),
        name="proj_fwd",
    )(x, g0, w_t, *riders)


def _causal_mask():
    t = lax.broadcasted_iota(jnp.int32, (CA, CA), 0)
    s = lax.broadcasted_iota(jnp.int32, (CA, CA), 1)
    return s <= t


def _layernorm_parts(gv):
    mu = jnp.mean(gv, axis=-1, keepdims=True)
    xc = gv - mu
    rs = lax.rsqrt(jnp.mean(xc * xc, axis=-1, keepdims=True) + LN_EPS)
    return xc * rs, rs


def _mixer_a_fwd(proj, ln_g, ln_b, w_s, b_sb, tm=256):
    T = proj.shape[0]

    def body(u_ref, v_ref, za_ref, lg_ref, lb_ref, ws_ref, bs_ref, a_ref, vln_s):
        vhat, _ = _layernorm_parts(_gelu(v_ref[...]))
        vln_s[...] = (vhat * lg_ref[...] + lb_ref[...]).astype(bf16)
        mask = _causal_mask()
        for g in range(HA):
            wg = jnp.where(mask, ws_ref[g], 0.0).astype(bf16)
            cols = slice(g * GA, (g + 1) * GA)
            for c in range(tm // CA):
                rows = slice(c * CA, (c + 1) * CA)
                mixed = _dot(wg, vln_s[rows, cols]) + bs_ref[g]
                za = za_ref[rows, cols]
                a = _gelu(u_ref[rows, cols]) * mixed * (za * _sigmoid(za))
                a_ref[rows, cols] = a.astype(bf16)

    def col(cidx):
        return pl.BlockSpec((tm, D), lambda i, c=cidx: (i, c))

    return pl.pallas_call(
        body,
        grid=(T // tm,),
        in_specs=[col(COL_U), col(COL_V), col(COL_ZA), _resident((1, D)), _resident((1, D)),
                  _resident((HA, CA, CA)), _resident((HA, CA, GA))],
        out_specs=pl.BlockSpec((tm, D), lambda i: (i, 0)),
        out_shape=jax.ShapeDtypeStruct((T, D), bf16),
        scratch_shapes=[pltpu.VMEM((tm, D), bf16)],
        compiler_params=_params(("parallel",)),
        name="mixer_a_fwd",
    )(proj, proj, proj, ln_g, ln_b, w_s, b_sb)


def _mixer_a_bwd(proj, da, dproj, ln_g, ln_b, w_s, b_sb, riders=(), kinds=(), tm=256):
    T = proj.shape[0]
    nsteps = T // tm
    n = len(riders)

    def body(u_ref, v_ref, za_ref, da_ref, dp_in, lg_ref, lb_ref, ws_ref, bs_ref, *rest):
        srcs, (dp_ref, dws_ref, dbs_ref, dlg_ref, dlb_ref), dsts = rest[:n], rest[n:n + 5], rest[n + 5:2 * n + 5]
        vln_s, dvln_s, *sems = rest[2 * n + 5:]
        del dp_in
        i = pl.program_id(0)

        @pl.when(i == 0)
        def _():
            dws_ref[...] = jnp.zeros_like(dws_ref)
            dbs_ref[...] = jnp.zeros_like(dbs_ref)
            dlg_ref[...] = jnp.zeros_like(dlg_ref)
            dlb_ref[...] = jnp.zeros_like(dlb_ref)
            for cp in _halves_copies(srcs, dsts, kinds, sems) if n else []:
                cp.start()

        gv, gv_grad = _gelu_and_grad(v_ref[...])
        vhat, rs = _layernorm_parts(gv)
        vln_s[...] = (vhat * lg_ref[...] + lb_ref[...]).astype(bf16)
        mask = _causal_mask()
        for g in range(HA):
            wg = jnp.where(mask, ws_ref[g], 0.0).astype(bf16)
            cols = slice(g * GA, (g + 1) * GA)
            dw_acc = jnp.zeros((CA, CA), f32)
            db_acc = jnp.zeros((CA, 1), f32)
            for c in range(tm // CA):
                rows = slice(c * CA, (c + 1) * CA)
                vln = vln_s[rows, cols]
                mixed = _dot(wg, vln) + bs_ref[g]
                u = u_ref[rows, cols]
                za = za_ref[rows, cols]
                da_blk = da_ref[rows, cols]
                sg = _sigmoid(za)
                sz = za * sg
                gu, gu_grad = _gelu_and_grad(u)
                dp_ref[rows, cols] = (da_blk * mixed * sz * gu_grad).astype(bf16)
                dp_ref[rows, 2 * D + g * GA:2 * D + (g + 1) * GA] = (
                    da_blk * gu * mixed * (sg * (1.0 + za * (1.0 - sg)))).astype(bf16)
                dmixed = da_blk * gu * sz
                dmb = dmixed.astype(bf16)
                dvln_s[rows, cols] = _dot_tn(wg, dmb)
                dw_acc = dw_acc + _dot_nt(dmb, vln)
                db_acc = db_acc + jnp.sum(dmixed, axis=-1, keepdims=True)
            dws_ref[g] += dw_acc
            dbs_ref[g] += jnp.broadcast_to(db_acc, (CA, GA))

        dvln = dvln_s[...]
        dlg_ref[...] += jnp.sum(dvln * vhat, axis=0, keepdims=True)
        dlb_ref[...] += jnp.sum(dvln, axis=0, keepdims=True)
        dvhat = dvln * lg_ref[...]
        dgv = rs * (dvhat - jnp.mean(dvhat, axis=-1, keepdims=True)
                    - vhat * jnp.mean(dvhat * vhat, axis=-1, keepdims=True))
        dp_ref[:, D:2 * D] = (dgv * gv_grad).astype(bf16)

        @pl.when(i == nsteps - 1)
        def _():
            for g in range(HA):
                dws_ref[g] = jnp.where(mask, dws_ref[g], 0.0)
            for cp in _halves_copies(srcs, dsts, kinds, sems) if n else []:
                cp.wait()

    def col(cidx):
        return pl.BlockSpec((tm, D), lambda i, c=cidx: (i, c))

    return pl.pallas_call(
        body,
        grid=(nsteps,),
        in_specs=[col(COL_U), col(COL_V), col(COL_ZA), pl.BlockSpec((tm, D), lambda i: (i, 0)),
                  pl.BlockSpec(memory_space=pl.ANY),
                  _resident((1, D)), _resident((1, D)), _resident((HA, CA, CA)), _resident((HA, CA, GA))] + [_ANY] * n,
        out_specs=[pl.BlockSpec((tm, 3 * D), lambda i: (i, 0)),
                   _resident((HA, CA, CA)), _resident((HA, CA, GA)), _resident((1, D)), _resident((1, D))] + [_ANY] * n,
        out_shape=[jax.ShapeDtypeStruct(dproj.shape, dproj.dtype),
                   jax.ShapeDtypeStruct((HA, CA, CA), f32), jax.ShapeDtypeStruct((HA, CA, GA), f32),
                   jax.ShapeDtypeStruct((1, D), f32), jax.ShapeDtypeStruct((1, D), f32)] + _halves_shapes(riders, kinds),
        scratch_shapes=[pltpu.VMEM((tm, D), bf16), pltpu.VMEM((tm, D), f32)] + (_halves_sems(riders) if n else []),
        input_output_aliases={4: 0},
        compiler_params=_params(("arbitrary",)),
        name="mixer_a_bwd",
    )(proj, proj, proj, da, dproj, ln_g, ln_b, w_s, b_sb, *riders)


def _tri(n, upper):
    r = lax.broadcasted_iota(jnp.int32, (n, n), 0)
    c = lax.broadcasted_iota(jnp.int32, (n, n), 1)
    return jnp.where((c >= r) if upper else (c <= r), 1.0, 0.0).astype(f32)


def _chunk_tri(n, upper):
    r = lax.broadcasted_iota(jnp.int32, (n, n), 0)
    c = lax.broadcasted_iota(jnp.int32, (n, n), 1)
    shift = CB.bit_length() - 1
    same_chunk = jnp.right_shift(r, shift) == jnp.right_shift(c, shift)
    return jnp.where(same_chunk & ((c >= r) if upper else (c <= r)), 1.0, 0.0).astype(f32)


def _log_alpha(lr, wg, bg):
    logit = _dot(lr.astype(bf16), wg.astype(bf16)) + bg
    la = (jnp.minimum(logit, 0.0) - jnp.log1p(jnp.exp(-jnp.abs(logit)))) * (1.0 / GATE_NORM)
    return logit, la


def _gla_fwd(proj, lr, w_gate, b_gate, gla_g, tm=256):
    T = proj.shape[0]
    nchunk = T // CB
    cpb = tm // CB

    def body(q_ref, k_ref, v_ref, zb_ref, lr_ref, wg_ref, bg_ref, gg_ref,
             o_ref, ob_ref, st_ref, state, la_s):
        @pl.when(pl.program_id(0) == 0)
        def _():
            state[...] = jnp.zeros_like(state)

        _, la = _log_alpha(lr_ref[...], wg_ref[...], bg_ref[...])
        la_s[...] = _dot_exact(_chunk_tri(tm, upper=False), la)
        causal = _tri(CB, upper=False) > 0.5
        states = [state[hd] for hd in range(HB)]
        for c in range(cpb):
            rows = slice(c * CB, (c + 1) * CB)
            b = la_s[rows, :]
            bl = b[CB - 1:CB, :]
            bm = b[CB // 2 - 1:CB // 2, :]
            q = q_ref[rows, :] * QSCALE
            k = k_ref[rows, :]
            qi_all = (q * jnp.exp(b - bm)).astype(bf16)
            ki_all = (k * jnp.exp(bm - b)).astype(bf16)
            qe_all = (q * jnp.exp(b)).astype(bf16)
            ks_all = (k * jnp.exp(bl - b)).astype(bf16)
            e_l = jnp.exp(bl)
            for hd in range(HB):
                kc = slice(hd * DK, (hd + 1) * DK)
                vc = slice(hd * DV, (hd + 1) * DV)
                v = v_ref[rows, vc].astype(bf16)
                p = jnp.where(causal, _dot_nt(qi_all[:, kc], ki_all[:, kc]), 0.0).astype(bf16)
                s0 = states[hd]
                st_ref[c, hd] = s0
                o = _dot(p, v) + _dot_nt(qe_all[:, kc], s0.astype(bf16))
                states[hd] = s0 * e_l[:, kc] + _dot_tn(v, ks_all[:, kc])
                o_ref[rows, vc] = o
                ro = lax.rsqrt(jnp.mean(o * o, axis=-1, keepdims=True) + EPS)
                zb = zb_ref[rows, vc]
                ob_ref[rows, vc] = (o * ro * gg_ref[...] * (zb * _sigmoid(zb))).astype(bf16)
        for hd in range(HB):
            state[hd] = states[hd]

    return pl.pallas_call(
        body,
        grid=(T // tm,),
        in_specs=[pl.BlockSpec((tm, KEYB), lambda i: (i, COL_Q)),
                  pl.BlockSpec((tm, KEYB), lambda i: (i, COL_K)),
                  pl.BlockSpec((tm, D), lambda i: (i, COL_VB)),
                  pl.BlockSpec((tm, D), lambda i: (i, COL_ZB)),
                  pl.BlockSpec((tm, LRP), lambda i: (i, 0)),
                  _resident((LRP, KEYB)), _resident((1, KEYB)), _resident((1, DV))],
        out_specs=[pl.BlockSpec((tm, D), lambda i: (i, 0)),
                   pl.BlockSpec((tm, D), lambda i: (i, 0)),
                   pl.BlockSpec((cpb, HB, DV, DK), lambda i: (i, 0, 0, 0))],
        out_shape=[jax.ShapeDtypeStruct((T, D), f32), jax.ShapeDtypeStruct((T, D), bf16),
                   jax.ShapeDtypeStruct((nchunk, HB, DV, DK), f32)],
        scratch_shapes=[pltpu.VMEM((HB, DV, DK), f32), pltpu.VMEM((tm, KEYB), f32)],
        compiler_params=_params(("arbitrary",)),
        name="gla_fwd",
    )(proj, proj, proj, proj, lr, w_gate, b_gate, gla_g)


def _gla_bwd(proj, lr, o, states, dob, dproj, w_gate, b_gate, gla_g, riders=(), sends_what=(), tm=256):
    T = proj.shape[0]
    cpb = tm // CB
    nb = T // tm
    n = len(riders)

    def body(q_ref, k_ref, v_ref, zb_ref, lr_ref, o_ref, st_ref, dob_ref, dp_in, wg_ref, bg_ref, gg_ref, *rest):
        srcs, (dp_ref, dlr_ref, dwg_ref, dbg_ref, dgg_ref), dsts = rest[:n], rest[n:n + 5], rest[n + 5:2 * n + 5]
        dstate, la_s, dlogit_s, tail_s, *sems = rest[2 * n + 5:]
        del dp_in
        step = pl.program_id(0)

        @pl.when(step == 0)
        def _():
            dstate[...] = jnp.zeros_like(dstate)
            dwg_ref[...] = jnp.zeros_like(dwg_ref)
            dbg_ref[...] = jnp.zeros_like(dbg_ref)
            dgg_ref[...] = jnp.zeros_like(dgg_ref)
            for cp in _exchange_copies(srcs, dsts, sends_what, sems) if n else []:
                cp.start()

        lr_v = lr_ref[...]
        logit, la = _log_alpha(lr_v, wg_ref[...], bg_ref[...])
        la_s[...] = _dot_exact(_chunk_tri(tm, upper=False), la)
        causal = _tri(CB, upper=False) > 0.5
        gg = gg_ref[...]
        dgg_acc = jnp.zeros((1, DV), f32)
        dstates = [dstate[hd] for hd in range(HB)]
        for c in reversed(range(cpb)):
            rows = slice(c * CB, (c + 1) * CB)
            b = la_s[rows, :]
            bl = b[CB - 1:CB, :]
            bm = b[CB // 2 - 1:CB // 2, :]
            eb_all, eqm_all, ekm_all = jnp.exp(b), jnp.exp(b - bm), jnp.exp(bm - b)
            eks_all, el_all = jnp.exp(bl - b), jnp.exp(bl)
            q_all = q_ref[rows, :] * QSCALE
            k_all = k_ref[rows, :]
            qi_all = (q_all * eqm_all).astype(bf16)
            ki_all = (k_all * ekm_all).astype(bf16)
            qe_all = (q_all * eb_all).astype(bf16)
            ksf_all = k_all * eks_all
            ks_all = ksf_all.astype(bf16)
            for hd in range(HB):
                kc = slice(hd * DK, (hd + 1) * DK)
                vc = slice(hd * DV, (hd + 1) * DV)
                o_h = o_ref[rows, vc]
                ro = lax.rsqrt(jnp.mean(o_h * o_h, axis=-1, keepdims=True) + EPS)
                ohat = o_h * ro
                zb = zb_ref[rows, vc]
                sg = _sigmoid(zb)
                dob_h = dob_ref[rows, vc]
                don = dob_h * (zb * sg)
                dp_ref[rows, 2 * D + hd * DV:2 * D + (hd + 1) * DV] = (
                    dob_h * ohat * gg * (sg * (1.0 + zb * (1.0 - sg)))).astype(bf16)
                dgg_acc = dgg_acc + jnp.sum(don * ohat, axis=0, keepdims=True)
                dohat = don * gg
                do = (ro * (dohat - ohat * jnp.mean(dohat * ohat, axis=-1, keepdims=True))).astype(bf16)
                e_b, e_qm, e_km, e_ks, e_l = eb_all[:, kc], eqm_all[:, kc], ekm_all[:, kc], eks_all[:, kc], el_all[:, kc]
                q, k, ks_f = q_all[:, kc], k_all[:, kc], ksf_all[:, kc]
                qi, ki, qe, ks = qi_all[:, kc], ki_all[:, kc], qe_all[:, kc], ks_all[:, kc]
                v = v_ref[rows, vc].astype(bf16)
                p = jnp.where(causal, _dot_nt(qi, ki), 0.0).astype(bf16)
                s0 = st_ref[c, hd]
                ds = dstates[hd]
                ds_b = ds.astype(bf16)
                dv = _dot_tn(p, do) + _dot_nt(ks, ds_b)
                dpm = jnp.where(causal, _dot_nt(do, v), 0.0).astype(bf16)
                dqi = _dot(dpm, ki)
                dki = _dot_tn(dpm, qi)
                dqe = _dot(do, s0.astype(bf16))
                dks = _dot(v, ds_b)
                dq_s = dqi * e_qm + dqe * e_b
                dk = dki * e_km + dks * e_ks
                tail = (jnp.sum(dks * ks_f, axis=0, keepdims=True)
                        + e_l * jnp.sum(ds * s0, axis=0, keepdims=True))
                dstates[hd] = _dot_tn(do, qe) + ds * e_l
                dp_ref[rows, kc] = (dq_s * QSCALE).astype(bf16)
                dp_ref[rows, KEYB + hd * DK:KEYB + (hd + 1) * DK] = dk.astype(bf16)
                dp_ref[rows, D + hd * DV:D + (hd + 1) * DV] = dv.astype(bf16)
                dlogit_s[rows, kc] = dq_s * q - dk * k
                tail_s[rows, kc] = jnp.broadcast_to(tail, (CB, DK))
        for hd in range(HB):
            dstate[hd] = dstates[hd]
        dgg_ref[...] += dgg_acc
        dg = _dot_exact(_chunk_tri(tm, upper=True), dlogit_s[...]) + tail_s[...]
        dlogit = dg * (1.0 / GATE_NORM) * _sigmoid(-logit)
        dbg_ref[...] += jnp.sum(dlogit, axis=0, keepdims=True)
        dlb = dlogit.astype(bf16)
        dlr_ref[...] = _dot_nt(dlb, wg_ref[...].astype(bf16)).astype(bf16)
        dwg_ref[...] += _dot_tn(lr_v.astype(bf16), dlb)

        if n:
            @pl.when(step == nb - 1)
            def _():
                for cp in _exchange_copies(srcs, dsts, sends_what, sems):
                    cp.wait()

    def rev(cidx):
        return lambda i, c=cidx: (nb - 1 - i, c)

    return pl.pallas_call(
        body,
        grid=(nb,),
        in_specs=[pl.BlockSpec((tm, KEYB), rev(COL_Q)),
                  pl.BlockSpec((tm, KEYB), rev(COL_K)),
                  pl.BlockSpec((tm, D), rev(COL_VB)),
                  pl.BlockSpec((tm, D), rev(COL_ZB)),
                  pl.BlockSpec((tm, LRP), rev(0)),
                  pl.BlockSpec((tm, D), rev(0)),
                  pl.BlockSpec((cpb, HB, DV, DK), lambda i: (nb - 1 - i, 0, 0, 0)),
                  pl.BlockSpec((tm, D), rev(0)),
                  pl.BlockSpec(memory_space=pl.ANY),
                  _resident((LRP, KEYB)), _resident((1, KEYB)), _resident((1, DV))] + [_ANY] * n,
        out_specs=[pl.BlockSpec((tm, 3 * D), rev(1)),
                   pl.BlockSpec((tm, LRP), rev(0)),
                   _resident((LRP, KEYB)), _resident((1, KEYB)), _resident((1, DV))] + [_ANY] * n,
        out_shape=[jax.ShapeDtypeStruct(dproj.shape, dproj.dtype),
                   jax.ShapeDtypeStruct((T, LRP), bf16),
                   jax.ShapeDtypeStruct((LRP, KEYB), f32), jax.ShapeDtypeStruct((1, KEYB), f32),
                   jax.ShapeDtypeStruct((1, DV), f32)] + _exchange_shapes(riders, sends_what),
        scratch_shapes=[pltpu.VMEM((HB, DV, DK), f32)] + [pltpu.VMEM((tm, KEYB), f32)] * 3
        + (_exchange_sems(n) if n else []),
        input_output_aliases={8: 0},
        compiler_params=_params(("arbitrary",)),
        name="gla_bwd",
    )(proj, proj, proj, proj, lr, o, states, dob, dproj, w_gate, b_gate, gla_g, *riders)


def _merge_fwd_bwd(x, tgt, proj, a, ob, w_a, w_b, w_o, g_f, tm=256):
    T = x.shape[0]

    def body(x_ref, t_ref, gt_ref, a_ref, ob_ref, wa_ref, wb_ref, wo_ref, gf_ref,
             dp_ref, dy_ref, da_ref, dob_ref, dwa_ref, dwb_ref, dwo_ref, dgf_ref, loss_ref):
        @pl.when(pl.program_id(0) == 0)
        def _():
            dwa_ref[...] = jnp.zeros_like(dwa_ref)
            dwb_ref[...] = jnp.zeros_like(dwb_ref)
            dwo_ref[...] = jnp.zeros_like(dwo_ref)
            dgf_ref[...] = jnp.zeros_like(dgf_ref)
            loss_ref[...] = jnp.zeros_like(loss_ref)

        ga = _sigmoid(gt_ref[:, :D])
        gb = _sigmoid(gt_ref[:, D:])
        a_v = a_ref[...]
        ob_v = ob_ref[...]
        pa = _dot(a_v, wa_ref[...])
        pb = _dot(ob_v, wb_ref[...])
        mb = (ga * pa + gb * pb).astype(bf16)
        y = x_ref[...] + _dot(mb, wo_ref[...])
        r1 = lax.rsqrt(jnp.mean(y * y, axis=-1, keepdims=True) + EPS)
        yhat = y * r1
        gf = gf_ref[...]
        err = yhat * gf - t_ref[...]
        loss_ref[...] += jnp.sum(err * err, axis=0, keepdims=True) * (0.5 / D)
        dout = err * (1.0 / D)
        dgf_ref[...] += jnp.sum(dout * yhat, axis=0, keepdims=True)
        dyn = dout * gf
        dy = r1 * (dyn - yhat * jnp.mean(dyn * yhat, axis=-1, keepdims=True))
        dy_ref[...] = dy
        dyb = dy.astype(bf16)
        dwo_ref[...] += _dot_tn(mb, dyb)
        dm = _dot_nt(dyb, wo_ref[...])
        dpa = (dm * ga).astype(bf16)
        dpb = (dm * gb).astype(bf16)
        dp_ref[:, :D] = (dm * pa * ga * (1.0 - ga)).astype(bf16)
        dp_ref[:, D:] = (dm * pb * gb * (1.0 - gb)).astype(bf16)
        dwa_ref[...] += _dot_tn(a_v, dpa)
        dwb_ref[...] += _dot_tn(ob_v, dpb)
        da_ref[...] = _dot_nt(dpa, wa_ref[...])
        dob_ref[...] = _dot_nt(dpb, wb_ref[...])

    row = lambda: pl.BlockSpec((tm, D), lambda i: (i, 0))
    return pl.pallas_call(
        body,
        grid=(T // tm,),
        in_specs=[row(), row(), pl.BlockSpec((tm, 2 * D), lambda i: (i, COL_GATES)), row(), row(),
                  _resident((D, D)), _resident((D, D)), _resident((D, D)), _resident((1, D))],
        out_specs=[pl.BlockSpec((tm, 2 * D), lambda i: (i, COL_GATES)), row(), row(), row(),
                   _resident((D, D)), _resident((D, D)), _resident((D, D)), _resident((1, D)), _resident((1, D))],
        out_shape=[jax.ShapeDtypeStruct((T, NMAIN), bf16),
                   jax.ShapeDtypeStruct((T, D), f32), jax.ShapeDtypeStruct((T, D), f32),
                   jax.ShapeDtypeStruct((T, D), f32),
                   jax.ShapeDtypeStruct((D, D), f32), jax.ShapeDtypeStruct((D, D), f32),
                   jax.ShapeDtypeStruct((D, D), f32),
                   jax.ShapeDtypeStruct((1, D), f32), jax.ShapeDtypeStruct((1, D), f32)],
        compiler_params=_params(("arbitrary",)),
        name="merge_fwd_bwd",
    )(x, tgt, proj, a, ob, w_a, w_b, w_o, g_f)


def _dx_bwd(x, dy, dproj, dlr, g0, w_t, riders=(), sends_what=(), tm=256):
    T = x.shape[0]
    nsteps = T // tm
    n = len(riders)

    def body(x_ref, dy_ref, dp_ref, dl_ref, g_ref, w_ref, *rest):
        srcs, (dx_ref, dg_ref), dsts, sems = rest[:n], rest[n:n + 2], rest[n + 2:2 * n + 2], rest[2 * n + 2:]

        @pl.when(pl.program_id(0) == 0)
        def _():
            dg_ref[...] = jnp.zeros_like(dg_ref)
            for cp in _exchange_copies(srcs, dsts, sends_what, sems) if n else []:
                cp.start()

        xv = x_ref[...]
        r = lax.rsqrt(jnp.mean(xv * xv, axis=-1, keepdims=True) + EPS)
        xhat = xv * r
        dh = (_dot(dp_ref[:, :LR_COL], w_ref[:LR_COL, :]) + _dot(dp_ref[:, LR_COL:], w_ref[LR_COL + RANK:, :])
              + _dot(dl_ref[...], w_ref[LR_ROWS, :]))
        dg_ref[...] += jnp.sum(dh * xhat, axis=0, keepdims=True)
        t = dh * g_ref[...]
        dx_ref[...] = dy_ref[...] + r * (t - xhat * jnp.mean(t * xhat, axis=-1, keepdims=True))

        if n:
            @pl.when(pl.program_id(0) == nsteps - 1)
            def _():
                for cp in _exchange_copies(srcs, dsts, sends_what, sems):
                    cp.wait()

    row = lambda: pl.BlockSpec((tm, D), lambda i: (i, 0))
    return pl.pallas_call(
        body,
        grid=(nsteps,),
        in_specs=[row(), row(), pl.BlockSpec((tm, NMAIN), lambda i: (i, 0)),
                  pl.BlockSpec((tm, LRP), lambda i: (i, 0)),
                  _resident((1, D)), _resident((NMAIN + RANK, D))] + [_ANY] * n,
        out_specs=[row(), _resident((1, D))] + [_ANY] * n,
        out_shape=[jax.ShapeDtypeStruct((T, D), f32), jax.ShapeDtypeStruct((1, D), f32)]
        + _exchange_shapes(riders, sends_what),
        scratch_shapes=_exchange_sems(n) if n else [],
        compiler_params=_params(("arbitrary",)),
        name="dx_bwd",
    )(x, dy, dproj, dlr, g0, w_t, *riders)


def _dw_in(h, dproj, dlr, pair_dtype=None, tm=1024, tn=1024):
    T = h.shape[0]
    tm = min(tm, T)
    nj, nk = NMAIN // tn, T // tm
    lr_tile = LR_COL // tn
    pair = pair_dtype is not None
    hd = D // 2

    def body(h_ref, dp_ref, dl_ref, out_ref, acc, lr_acc, sems, lr_sem, *more):
        j, k = pl.program_id(0), pl.program_id(1)
        slot = j % 2

        def tile_row(jj):
            return pl.multiple_of(jj * tn + jnp.where(jj >= lr_tile, RANK, 0), 8)

        if pair:
            land, lr_land, part_buf, lr_part, swap_send, swap_recv = more
            c = lax.axis_index("c")
            mine = pl.ds(pl.multiple_of(c * hd, 128), hd)
            other = pl.ds(pl.multiple_of((1 - c) * hd, 128), hd)

            def tile_swap(jj, s):
                return _remote(acc.at[s, :, other], land.at[jj], swap_send.at[jj], swap_recv.at[jj], _sibling())

            def lr_swap():
                return _remote(lr_acc.at[pl.ds(0, RANK), other], lr_land, swap_send.at[nj], swap_recv.at[nj], _sibling())

            def tile_out(jj, s):
                return pltpu.make_async_copy(part_buf.at[s], out_ref.at[pl.ds(tile_row(jj), tn)], sems.at[s])

            def finish_tile(jj, s):
                tile_swap(jj, s).wait()
                part_buf[s] = (acc[s, :, mine] + land[jj]).astype(pair_dtype)
                tile_out(jj, s).start()

            lr_out = pltpu.make_async_copy(lr_part, out_ref.at[pl.ds(LR_COL, RANK)], lr_sem)
        else:
            def tile_out(jj, s):
                return pltpu.make_async_copy(acc.at[s], out_ref.at[pl.ds(tile_row(jj), tn)], sems.at[s])

            lr_out = pltpu.make_async_copy(lr_acc.at[pl.ds(0, RANK)], out_ref.at[pl.ds(LR_COL, RANK)], lr_sem)

        @pl.when(j == 0)
        def _():
            @pl.when(k == 0)
            def _():
                lr_acc[...] = jnp.zeros_like(lr_acc)

            lr_acc[...] += _dot_tn(dl_ref[...], h_ref[...])

            @pl.when(k == nk - 1)
            def _():
                if pair:
                    lr_swap().start()
                else:
                    lr_out.start()

        @pl.when(k == 0)
        def _():
            acc[slot] = jnp.zeros((tn, D), f32)

        acc[slot] += _dot_tn(dp_ref[...], h_ref[...])

        @pl.when(k == nk - 1)
        def _():
            if pair:
                tile_swap(j, slot).start()

                @pl.when(j >= 3)
                def _():
                    tile_out(j - 3, 1 - slot).wait()

                @pl.when(j >= 1)
                def _():
                    finish_tile(j - 1, 1 - slot)

                @pl.when(j == nj - 1)
                def _():
                    tile_out(j - 2, slot).wait()
                    finish_tile(j, slot)
                    lr_swap().wait()
                    lr_part[...] = (lr_acc[0:RANK, mine] + lr_land[...]).astype(pair_dtype)
                    lr_out.start()
                    tile_out(j - 1, 1 - slot).wait()
                    tile_out(j, slot).wait()
                    lr_out.wait()
            else:
                tile_out(j, slot).start()

                @pl.when(j > 0)
                def _():
                    tile_out(j - 1, 1 - slot).wait()

                @pl.when(j == nj - 1)
                def _():
                    tile_out(j, slot).wait()
                    lr_out.wait()

    pair_scratch = [pltpu.VMEM((nj, tn, hd), f32), pltpu.VMEM((RANK, hd), f32), pltpu.VMEM((2, tn, hd), pair_dtype),
                    pltpu.VMEM((RANK, hd), pair_dtype)] + [pltpu.SemaphoreType.DMA((nj + 1,))] * 2 if pair else []
    return pl.pallas_call(
        body,
        grid=(nj, nk),
        in_specs=[pl.BlockSpec((tm, D), lambda j, k: (k, 0)), pl.BlockSpec((tm, tn), lambda j, k: (k, j)),
                  pl.BlockSpec((tm, LRP), lambda j, k: (k, 0))],
        out_specs=_ANY,
        out_shape=jax.ShapeDtypeStruct((NMAIN + RANK, hd), pair_dtype) if pair
        else jax.ShapeDtypeStruct((NMAIN + RANK, D), f32),
        scratch_shapes=[pltpu.VMEM((2, tn, D), f32), pltpu.VMEM((LRP, D), f32),
                        pltpu.SemaphoreType.DMA((2,)), pltpu.SemaphoreType.DMA] + pair_scratch,
        compiler_params=_params(("arbitrary", "arbitrary")),
        name="dw_in",
    )(h, dproj, dlr)


MESH = pl.DeviceIdType.MESH


def _place():
    x, y, c = lax.axis_index("x"), lax.axis_index("y"), lax.axis_index("c")
    others = [(1 - x, y), (x, 1 - y), (1 - x, 1 - y)]
    return x, y, c, 2 * x + y, others


def _sibling():
    return lax.axis_index("x"), lax.axis_index("y"), 1 - lax.axis_index("c")


def _remote(src, dst, send_sem, recv_sem, to):
    return pltpu.make_async_remote_copy(src_ref=src, dst_ref=dst, send_sem=send_sem, recv_sem=recv_sem,
                                        device_id=to, device_id_type=MESH)


def _half(ref, e, by_columns):
    if not by_columns:
        return ref.at[e]
    hw = ref.shape[-1] // 2
    return ref.at[:, pl.ds(pl.multiple_of(e * hw, 128), hw)]


RELAY_ROWS = 1024


def _gather_win(shard):
    h, w = shard.shape
    part_a, part_b = pl.ds(0, RELAY_ROWS), pl.ds(RELAY_ROWS, h - RELAY_ROWS)

    def body(src, dst, send_sems, recv_sems, relay_send, relay_recv, pass_send, pass_recv, own_sems, stage):
        x, y, c, me, others = _place()
        (to_x, to_y, _), sibling = others, (x, y, 1 - c)
        j_x, j_y, j_d = (2 * cx + cy for cx, cy in others)
        cols = pl.ds(pl.multiple_of(c * (w // 2), 128), w // 2)
        theirs = pl.ds(pl.multiple_of((1 - c) * (w // 2), 128), w // 2)

        def mine(j, rows=pl.ds(0, h)):
            return dst.at[j, rows, cols]

        to_stage = pltpu.make_async_copy(src, stage, own_sems.at[0])
        to_slot = pltpu.make_async_copy(stage, dst.at[me], own_sems.at[1])
        sends = [_remote(src.at[:, cols], mine(me), send_sems.at[k], recv_sems.at[k], (*to, c))
                 for k, to in enumerate((to_x, to_y))]
        for cp in sends:
            cp.start()
        to_stage.start()
        relays = [_remote(mine(j_x, part_a), mine(j_x, part_a), relay_send.at[0], relay_recv.at[0], (*to_y, c)),
                  _remote(mine(j_y, part_b), mine(j_y, part_b), relay_send.at[1], relay_recv.at[1], (*to_x, c))]
        landed = [mine(j_x), mine(j_y), mine(j_d, part_a), mine(j_d, part_b)]
        passes = [_remote(place, place, pass_send.at[k], pass_recv.at[k], sibling) for k, place in enumerate(landed)]
        for k in range(2):
            _remote(src.at[:, cols], landed[k], send_sems.at[k], recv_sems.at[k], sibling).wait_recv()
            relays[k].start()
            passes[k].start()
        to_stage.wait()
        to_slot.start()
        for k in range(2):
            _remote(landed[2 + k], landed[2 + k], relay_send.at[k], relay_recv.at[k], sibling).wait_recv()
            passes[2 + k].start()
        for k, place in enumerate([(j_x, pl.ds(0, h)), (j_y, pl.ds(0, h)), (j_d, part_a), (j_d, part_b)]):
            got = dst.at[place[0], place[1], theirs]
            _remote(got, got, pass_send.at[k], pass_recv.at[k], sibling).wait_recv()
        for cp in sends + relays + passes:
            cp.wait_send()
        to_slot.wait()

    return pl.pallas_call(
        body,
        in_specs=[_ANY],
        out_specs=_ANY,
        out_shape=jax.ShapeDtypeStruct((NCHIP, h, w), shard.dtype),
        scratch_shapes=[pltpu.SemaphoreType.DMA((2,))] * 4 + [pltpu.SemaphoreType.DMA((4,))] * 2
        + [pltpu.SemaphoreType.DMA((2,)), pltpu.VMEM((h, w), shard.dtype)],
        compiler_params=pltpu.CompilerParams(vmem_limit_bytes=VMEM_LIMIT),
        name="gather_win",
    )(shard)


def _gathered_shapes(shards):
    return [jax.ShapeDtypeStruct((NCHIP,) + s.shape, s.dtype) for s in shards]


def _gather_scratch(shards):
    n = len(shards)
    return ([pltpu.SemaphoreType.DMA((3, n))] * 4 + [pltpu.SemaphoreType.DMA((2, n))]
            + [pltpu.VMEM(s.shape, s.dtype) for s in shards])


def _own_to_stage(srcs, scratch):
    own_sems, stages = scratch[4], scratch[5:]
    return [pltpu.make_async_copy(srcs[a], stages[a], own_sems.at[0, a]) for a in range(len(srcs))]


def _own_to_slot(dsts, scratch):
    own_sems, stages = scratch[4], scratch[5:]
    me = _place()[3]
    return [pltpu.make_async_copy(stages[a], dsts[a].at[me], own_sems.at[1, a]) for a in range(len(dsts))]


def _gather_copies(srcs, dsts, by_columns, sems, sends_only):
    n = len(srcs)
    send_sems, recv_sems, pass_send, pass_recv = sems[:4]
    x, y, c, me, others = _place()
    sibling = (x, y, 1 - c)

    def src(a, e):
        return _half(srcs[a], e, by_columns[a])

    def dst(a, j, e):
        return _half(dsts[a].at[j], e, by_columns[a])

    sends, arrivals, passes, passed = [], [], [], []
    for k, (cx, cy) in enumerate(others):
        j = 2 * cx + cy
        for a in range(n):
            sends.append(_remote(src(a, c), dst(a, me, c), send_sems.at[k, a], recv_sems.at[k, a], (cx, cy, c)))
            if not sends_only:
                arrivals.append(_remote(src(a, c), dst(a, j, c), send_sems.at[k, a], recv_sems.at[k, a], (cx, cy, c)))
                passes.append(_remote(dst(a, j, c), dst(a, j, c), pass_send.at[k, a], pass_recv.at[k, a], sibling))
                passed.append(_remote(src(a, c), dst(a, j, 1 - c), pass_send.at[k, a], pass_recv.at[k, a], sibling))
    return sends, arrivals, passes, passed


def _gather_start(srcs, dsts, by_columns, scratch):
    for cp in _gather_copies(srcs, dsts, by_columns, scratch, sends_only=True)[0]:
        cp.start()
    for cp in _own_to_stage(srcs, scratch):
        cp.start()


def _gather_finish(srcs, dsts, by_columns, scratch):
    for cp in _own_to_stage(srcs, scratch):
        cp.wait()
    own = _own_to_slot(dsts, scratch)
    for cp in own:
        cp.start()
    sends, arrivals, passes, passed = _gather_copies(srcs, dsts, by_columns, scratch, sends_only=False)
    for arrival, cp in zip(arrivals, passes):
        arrival.wait_recv()
        cp.start()
    for arrival in passed:
        arrival.wait_recv()
    for cp in sends + passes:
        cp.wait_send()
    for cp in own:
        cp.wait()


HALF_FIRST, CHIP_FIRST, BY_COLUMNS = "half_first", "chip_first", "by_columns"


def _sibling_halves(bufs, kinds):
    n = len(bufs)

    def body(*refs):
        cps = _halves_copies(refs[:n], refs[n:2 * n], kinds, refs[2 * n:])
        for cp in cps:
            cp.start()
        for cp in cps:
            cp.wait()

    return pl.pallas_call(
        body,
        in_specs=[_ANY] * n,
        out_specs=[_ANY] * n,
        out_shape=_halves_shapes(bufs, kinds),
        scratch_shapes=_halves_sems(bufs),
        name="sibling_halves",
    )(*bufs)


def _halves_shapes(bufs, kinds):
    def landed(b, kind):
        if kind == HALF_FIRST:
            return b.shape[1:]
        if kind == CHIP_FIRST:
            return (b.shape[0],) + b.shape[2:]
        return b.shape[:2] + (b.shape[2] // 2,)

    return [jax.ShapeDtypeStruct(landed(b, kind), b.dtype) for b, kind in zip(bufs, kinds)]


def _halves_sems(bufs):
    return [pltpu.SemaphoreType.DMA((len(bufs), NCHIP))] * 2


def _halves_copies(srcs, dsts, kinds, sems):
    send_sems, recv_sems = sems
    x, y, c, _, _ = _place()
    cps = []
    for a, kind in enumerate(kinds):
        if kind == HALF_FIRST:
            cps.append(_remote(srcs[a].at[1 - c], dsts[a], send_sems.at[a, 0], recv_sems.at[a, 0], (x, y, 1 - c)))
        else:
            cps += [_remote(_half(srcs[a].at[j], 1 - c, kind == BY_COLUMNS), dsts[a].at[j],
                            send_sems.at[a, j], recv_sems.at[a, j], (x, y, 1 - c)) for j in range(srcs[a].shape[0])]
    return cps


TO_ITS_CHIP, TO_EVERY_CHIP, ROWS_TO_ITS_CHIP = "to_its_chip", "to_every_chip", "rows_to_its_chip"
PIECE_STEP = 2048
PIECE_ROWS = 2064


def _chip_exchange(parts, sends_what):
    n = len(parts)

    def body(*refs):
        cps = _exchange_copies(refs[:n], refs[n:2 * n], sends_what, refs[2 * n:])
        for cp in cps:
            cp.start()
        for cp in cps:
            cp.wait()

    return pl.pallas_call(
        body,
        in_specs=[_ANY] * n,
        out_specs=[_ANY] * n,
        out_shape=_exchange_shapes(parts, sends_what),
        scratch_shapes=_exchange_sems(n),
        name="chip_exchange",
    )(*parts)


def _exchange_shapes(parts, sends_what):
    def landed(p, what):
        return (3, PIECE_ROWS, p.shape[1]) if what == ROWS_TO_ITS_CHIP else (3,) + p.shape[1:]

    return [jax.ShapeDtypeStruct(landed(p, what), p.dtype) for p, what in zip(parts, sends_what)]


def _exchange_sems(n):
    return [pltpu.SemaphoreType.DMA((3, n))] * 2


def _exchange_copies(srcs, dsts, sends_what, sems):
    send_sems, recv_sems = sems
    x, y, c, me, others = _place()

    def part(a, j):
        if sends_what[a] == ROWS_TO_ITS_CHIP:
            return srcs[a].at[pl.ds(pl.multiple_of(j * PIECE_STEP, PIECE_STEP), PIECE_ROWS)]
        return srcs[a].at[j if sends_what[a] == TO_ITS_CHIP else 0]

    return [_remote(part(a, 2 * cx + cy), dsts[a].at[k], send_sems.at[k, a], recv_sems.at[k, a], (cx, cy, c))
            for k, (cx, cy) in enumerate(others) for a in range(len(srcs))]


def _sibling_swap(halves):
    n = len(halves)

    def body(*refs):
        srcs, dsts = refs[:n], refs[n:2 * n]
        send_sems, recv_sems = refs[2 * n:]
        x, y, c, _, _ = _place()
        cps = [_remote(srcs[a], dsts[a], send_sems.at[a], recv_sems.at[a], (x, y, 1 - c)) for a in range(n)]
        for cp in cps:
            cp.start()
        for cp in cps:
            cp.wait()

    return pl.pallas_call(
        body,
        in_specs=[_ANY] * n,
        out_specs=[_ANY] * n,
        out_shape=[jax.ShapeDtypeStruct(s.shape, s.dtype) for s in halves],
        scratch_shapes=[pltpu.SemaphoreType.DMA((n,))] * 2,
        name="sibling_swap",
    )(*halves)


def _tile(h, w, operands=5):
    if h % 128 == 0:
        return 128, w
    budget = VMEM_LIMIT * 3 // 4 // (2 * operands * 4)
    tw = w
    while h * tw > budget and tw % 256 == 0:
        tw //= 2
    return h, tw


def _pair_sum(place, bufs, kind, gots, out_dtype):
    m = len(bufs)
    nj, h, w = gots[0].shape
    th, tw = _tile(h, w, operands=3 * m)
    nq = w // tw

    def body(p_ref, *refs):
        del p_ref
        for a_ref, b_ref, o_ref in zip(refs[:m], refs[m:2 * m], refs[2 * m:]):
            o_ref[...] = (a_ref[...] + b_ref[...]).astype(out_dtype)

    if kind == HALF_FIRST:
        mine = pl.BlockSpec((None, None, th, tw), lambda j, r, q, p: (p[0], j, r, q))
    elif kind == CHIP_FIRST:
        mine = pl.BlockSpec((None, None, th, tw), lambda j, r, q, p: (j, p[0], r, q))
    else:
        mine = pl.BlockSpec((None, th, tw), lambda j, r, q, p: (j, r, p[0] * nq + q))
    landed = pl.BlockSpec((None, th, tw), lambda j, r, q, p: (j, r, q))
    return pl.pallas_call(
        body,
        grid_spec=pltpu.PrefetchScalarGridSpec(
            num_scalar_prefetch=1,
            grid=(nj, h // th, w // tw),
            in_specs=[mine] * m + [landed] * m,
            out_specs=[landed] * m,
        ),
        out_shape=[jax.ShapeDtypeStruct((nj, h, w), out_dtype)] * m,
        compiler_params=_params(("parallel", "parallel", "parallel")),
        name="pair_sum",
    )(place, *bufs, *gots)


def _chip_sum(place, parts, slots):
    m = len(parts)
    nj, h, w = parts[0].shape
    th, tw = _tile(h, w, operands=5 * m)

    def body(p_ref, *refs):
        me = p_ref[1]
        for own_ref, s_ref, o_ref in zip(refs[:m], refs[m:2 * m], refs[2 * m:]):
            own = own_ref[...].astype(f32)
            by_flip = {2: s_ref[0].astype(f32), 1: s_ref[1].astype(f32), 3: s_ref[2].astype(f32)}
            acc = None
            for j in range(NCHIP):
                flip = me ^ j
                term = jnp.where(flip == 0, own,
                                 jnp.where(flip == 2, by_flip[2], jnp.where(flip == 1, by_flip[1], by_flip[3])))
                acc = term if acc is None else acc + term
            o_ref[...] = acc

    return pl.pallas_call(
        body,
        grid_spec=pltpu.PrefetchScalarGridSpec(
            num_scalar_prefetch=1,
            grid=(h // th, w // tw),
            in_specs=[pl.BlockSpec((None, th, tw), lambda r, q, p: (p[1] if nj == NCHIP else 0, r, q))] * m
            + [pl.BlockSpec((3, th, tw), lambda r, q, p: (0, r, q))] * m,
            out_specs=[pl.BlockSpec((th, tw), lambda r, q, p: (r, q))] * m,
        ),
        out_shape=[jax.ShapeDtypeStruct((h, w), f32)] * m,
        compiler_params=_params(("parallel", "parallel")),
        name="chip_sum",
    )(place, *parts, *slots)


def _adamw_math(w, g, m, v):
    nm = ADAM_B1 * m + (1.0 - ADAM_B1) * g
    nv = ADAM_B2 * v + (1.0 - ADAM_B2) * (g * g)
    m_hat = nm / (1.0 - ADAM_B1 ** ADAM_STEP)
    v_hat = nv / (1.0 - ADAM_B2 ** ADAM_STEP)
    return -ADAM_LR * (m_hat / (jnp.sqrt(v_hat) + ADAM_EPS) + ADAM_WD * w), nm, nv


def _adamw(w, g, m, v):
    rows, width = w.shape
    th, tw = _tile(rows, width, operands=7)

    def body(w_ref, g_ref, m_ref, v_ref, d_ref, nm_ref, nv_ref):
        d_ref[...], nm_ref[...], nv_ref[...] = _adamw_math(w_ref[...], g_ref[...], m_ref[...], v_ref[...])

    spec = pl.BlockSpec((th, tw), lambda r, q: (r, q))
    return pl.pallas_call(
        body,
        grid=(rows // th, width // tw),
        in_specs=[spec] * 4,
        out_specs=[spec] * 3,
        out_shape=[jax.ShapeDtypeStruct((rows, width), f32)] * 3,
        compiler_params=_params(("parallel", "parallel")),
        name="adamw",
    )(w, g, m, v)


def _adamw_halves(place, ws, mines, gots, ms, vs, axis):
    k = len(ws)
    rows, width = ws[0].shape
    h, hw = mines[0].shape
    th, tw = _tile(h, hw, operands=10 * k)
    nr, nq = h // th, hw // tw

    def body(p_ref, *refs):
        ins, outs = refs[:5 * k], refs[5 * k:]
        for i in range(k):
            w_ref, a_ref, b_ref, m_ref, v_ref = ins[i::k]
            g_ref, d_ref, nm_ref, nv_ref = outs[i::k]
            g = jnp.where(pl.program_id(0) == p_ref[0], a_ref[...], b_ref[...])
            g_ref[...] = g
            d_ref[...], nm_ref[...], nv_ref[...] = _adamw_math(w_ref[...], g, m_ref[...], v_ref[...])

    if axis == 0:
        full = pl.BlockSpec((th, tw), lambda e, r, q, p: (e * nr + r, q))
    else:
        full = pl.BlockSpec((th, tw), lambda e, r, q, p: (r, e * nq + q))
    half = pl.BlockSpec((th, tw), lambda e, r, q, p: (r, q))
    res = pl.pallas_call(
        body,
        grid_spec=pltpu.PrefetchScalarGridSpec(
            num_scalar_prefetch=1,
            grid=(2, nr, nq),
            in_specs=[full] * k + [half] * (2 * k) + [full] * (2 * k),
            out_specs=[full] * (4 * k),
        ),
        out_shape=[jax.ShapeDtypeStruct((rows, width), f32)] * (4 * k),
        compiler_params=_params(("parallel", "parallel", "parallel")),
        name="adamw_halves",
    )(place, *ws, *mines, *gots, *ms, *vs)
    return [res[i::k] for i in range(k)]


_SMALL = (("norm_g", 8), ("ln_v_g", 8), ("ln_v_b", 8), ("w_spatial", 1024), ("b_spatial", 8), ("b_gate_up", 4),
          ("gla_norm_g", 2), ("final_norm_g", 8), ("w_gate_up", 64), ("loss", 8))
_SMALL_ROWS = 1152


def _pack_rows(arrays, rows):
    flat = jnp.concatenate([a.reshape(-1, 128) for a in arrays], axis=0)
    return jnp.pad(flat, ((0, rows - flat.shape[0]), (0, 0)))


def kernel(x, norm_g, w_in, ln_v_g, ln_v_b, w_spatial, b_spatial, w_gate_up, b_gate_up, gla_norm_g, w_branch_a, w_branch_b, w_out, final_norm_g, loss_target, m_norm_g, m_w_in, m_ln_v_g, m_ln_v_b, m_w_spatial, m_b_spatial, m_w_gate_up, m_b_gate_up, m_gla_norm_g, m_w_branch_a, m_w_branch_b, m_w_out, m_final_norm_g, v_norm_g, v_w_in, v_ln_v_g, v_ln_v_b, v_w_spatial, v_b_spatial, v_w_gate_up, v_b_gate_up, v_gla_norm_g, v_w_branch_a, v_w_branch_b, v_w_out, v_final_norm_g):
    chip = 2 * lax.axis_index("x") + lax.axis_index("y")
    core = lax.axis_index("c")
    place = jnp.stack([core, chip]).astype(jnp.int32)
    mat_names = ("w_branch_a", "w_branch_b", "w_out")

    wt_shard = jnp.transpose(w_in[0]).astype(bf16)
    mats = [w[0].astype(bf16).reshape(2, D // NCHIP // 2, D) for w in (w_branch_a, w_branch_b, w_out)]
    gate_sh = w_gate_up[0].reshape(2, RANK // 2, 128)
    g_win = _gather_win(wt_shard)
    w_t = g_win.reshape(NCHIP * WIN_SHARD, D)
    b_sb = jnp.broadcast_to(b_spatial[0][:, :, None], (HA, CA, GA))
    xs, tgt = x[0], loss_target[0]

    proj, lr, h, g_a, g_b, g_o, g_gate = _proj_fwd(xs, norm_g, w_t, riders=mats + [gate_sh])
    w_a, w_b, w_o = (g.reshape(D, D) for g in (g_a, g_b, g_o))
    w_gate = jnp.transpose(g_gate.reshape(NCHIP, RANK, 128), (1, 0, 2)).reshape(RANK, KEYB)
    w_gate = jnp.pad(w_gate, ((0, LRP - RANK), (0, 0)))
    a = _mixer_a_fwd(proj, ln_v_g, ln_v_b, w_spatial[0], b_sb)
    o, ob, states = _gla_fwd(proj, lr, w_gate, b_gate_up, gla_norm_g)
    dproj, dy, da, dob, dwa, dwb, dwo, dgf, loss_cols = _merge_fwd_bwd(xs, tgt, proj, a, ob, w_a, w_b, w_o,
                                                                       final_norm_g.reshape(1, D))
    b_mats = [t.reshape(NCHIP, 2, D // NCHIP // 2, D) for t in (dwa, dwb, dwo)]
    dproj, dws, dbs, dlg, dlb, *got_mats = _mixer_a_bwd(proj, da, dproj, ln_v_g, ln_v_b, w_spatial[0], b_sb,
                                                        riders=b_mats, kinds=[CHIP_FIRST] * 3)
    part_mats = _pair_sum(place, b_mats, CHIP_FIRST, got_mats, bf16)
    dproj, dlr, dwg, dbg, dgg, *slots_mats = _gla_bwd(proj, lr, o, states, dob, dproj, w_gate, b_gate_up, gla_norm_g,
                                                      riders=part_mats, sends_what=[TO_ITS_CHIP] * 3)
    part_win = _dw_in(h, dproj, dlr, pair_dtype=bf16)
    dx, dg0, slots_win = _dx_bwd(xs, dy, dproj, dlr, norm_g, w_t, riders=[part_win], sends_what=[ROWS_TO_ITS_CHIP])
    slots_big = [slots_win] + slots_mats
    small = _pack_rows([dg0, dlg, dlb, dws, dbs[:, :, 0], dbg, dgg, dgf, dwg[:RANK], loss_cols], _SMALL_ROWS)
    b_small = small.reshape(2, 1, _SMALL_ROWS // 2, 128)
    (got_small,) = _sibling_halves([b_small], [HALF_FIRST])
    (part_small,) = _pair_sum(place, [b_small], HALF_FIRST, [got_small], f32)
    (slots_small,) = _chip_exchange([part_small], [TO_EVERY_CHIP])
    own_win = lax.dynamic_slice_in_dim(part_win, chip * PIECE_STEP, PIECE_ROWS, axis=0)[None]
    mine = [*_chip_sum(place, [own_win], slots_big[:1]), *_chip_sum(place, part_mats, slots_big[1:]),
            *_chip_sum(place, [part_small], [slots_small])]
    theirs = list(_sibling_swap(mine))
    mine[0], theirs[0] = (lax.dynamic_slice_in_dim(t, (WIN_SHARD - PIECE_STEP) * chip, WIN_SHARD, axis=0)
                          for t in (mine[0], theirs[0]))

    g_small = jnp.where(core == 0, jnp.concatenate([mine[4], theirs[4]], axis=0),
                        jnp.concatenate([theirs[4], mine[4]], axis=0))
    grads = {}
    row = 0
    for name, rows in _SMALL:
        grads[name] = g_small[row:row + rows]
        row += rows
    loss = jnp.sum(grads["loss"])
    dwg_full = grads["w_gate_up"].reshape(RANK, KEYB)
    grads["w_gate_up"] = lax.dynamic_slice_in_dim(dwg_full, chip * 128, 128, axis=1)

    weights = dict(norm_g=norm_g, w_in=w_in, ln_v_g=ln_v_g, ln_v_b=ln_v_b, w_spatial=w_spatial, b_spatial=b_spatial,
                   w_gate_up=w_gate_up, b_gate_up=b_gate_up, gla_norm_g=gla_norm_g, w_branch_a=w_branch_a,
                   w_branch_b=w_branch_b, w_out=w_out, final_norm_g=final_norm_g)
    m_in = dict(norm_g=m_norm_g, w_in=m_w_in, ln_v_g=m_ln_v_g, ln_v_b=m_ln_v_b, w_spatial=m_w_spatial,
                b_spatial=m_b_spatial, w_gate_up=m_w_gate_up, b_gate_up=m_b_gate_up, gla_norm_g=m_gla_norm_g,
                w_branch_a=m_w_branch_a, w_branch_b=m_w_branch_b, w_out=m_w_out, final_norm_g=m_final_norm_g)
    v_in = dict(norm_g=v_norm_g, w_in=v_w_in, ln_v_g=v_ln_v_g, ln_v_b=v_ln_v_b, w_spatial=v_w_spatial,
                b_spatial=v_b_spatial, w_gate_up=v_w_gate_up, b_gate_up=v_b_gate_up, gla_norm_g=v_gla_norm_g,
                w_branch_a=v_w_branch_a, w_branch_b=v_w_branch_b, w_out=v_w_out, final_norm_g=v_final_norm_g)
    names = list(weights)
    small_names = [n for n in names if n != "w_in" and n not in mat_names]
    out_g, out_d, out_m, out_v = {}, {}, {}, {}
    (res,) = _adamw_halves(place, [jnp.transpose(w_in[0])], mine[:1], theirs[:1], [jnp.transpose(m_w_in[0])],
                           [jnp.transpose(v_w_in[0])], axis=1)
    out_g["w_in"], out_d["w_in"], out_m["w_in"], out_v["w_in"] = (jnp.transpose(t)[None] for t in res)
    res_mats = _adamw_halves(place, [weights[n][0] for n in mat_names], mine[1:4], theirs[1:4],
                             [m_in[n][0] for n in mat_names], [v_in[n][0] for n in mat_names], axis=0)
    for n, res in zip(mat_names, res_mats):
        out_g[n], out_d[n], out_m[n], out_v[n] = (t[None] for t in res)
    upd_rows = sum(weights[n].size for n in small_names) // 128
    pad_rows = -(-upd_rows // 8) * 8
    packed = [_pack_rows([t[n] for n in small_names], pad_rows) for t in (weights, grads, m_in, v_in)]
    d_s, m_s, v_s = _adamw(*packed)
    row = 0
    for n in small_names:
        shape = weights[n].shape
        rows = weights[n].size // 128
        out_g[n] = grads[n].reshape(shape)
        out_d[n], out_m[n], out_v[n] = (t[row:row + rows].reshape(shape) for t in (d_s, m_s, v_s))
        row += rows
    return (loss, dx[None], *[out_g[n] for n in names], *[out_d[n] for n in names],
            *[out_m[n] for n in names], *[out_v[n] for n in names])
```

```python
import functools
import math

import jax
import jax.numpy as jnp
from jax import lax
from jax.experimental import pallas as pl
from jax.experimental.pallas import tpu as pltpu

f32 = jnp.float32
bf16 = jnp.bfloat16

D = 1024
NMAIN = 8192
LRP = 128
RANK = 16
HA, GA, CA = 8, 128, 128
HB, DK, DV, CB = 4, 128, 256, 64
KEYB = HB * DK
EPS = 1e-6
LN_EPS = 1e-5
GATE_NORM = 16.0
QSCALE = DK ** -0.5
COL_U, COL_V, COL_ZA = 0, 1, 2
COL_Q, COL_K = 6, 7
COL_VB, COL_ZB = 4, 5
COL_GATES = 3
VMEM_LIMIT = 56 * 1024 * 1024
NCHIP = 4
WIN_SHARD = 2052
LR_COL = 6144
_ANY = pl.BlockSpec(memory_space=pl.ANY)

ADAM_LR, ADAM_B1, ADAM_B2, ADAM_EPS, ADAM_WD, ADAM_STEP = 0.001, 0.9, 0.999, 1e-08, 0.01, 10

_SQRT_HALF = 0.7071067811865476
_INV_SQRT_2PI = 0.3989422804014327


def _dot(a, b):
    return jnp.dot(a, b, preferred_element_type=f32)


def _dot_nt(a, b):
    return lax.dot_general(a, b, (((1,), (1,)), ((), ())), preferred_element_type=f32)


def _dot_tn(a, b):
    return lax.dot_general(a, b, (((0,), (0,)), ((), ())), preferred_element_type=f32)


def _dot_exact(a, b):
    return jnp.dot(a, b, preferred_element_type=f32, precision=lax.Precision.HIGHEST)


def _gelu(x):
    return 0.5 * x * (1.0 + lax.erf(x * _SQRT_HALF))


def _gelu_and_grad(x):
    cdf = 0.5 * (1.0 + lax.erf(x * _SQRT_HALF))
    return x * cdf, cdf + x * (jnp.exp(-0.5 * x * x) * _INV_SQRT_2PI)


def _sigmoid(x):
    return 0.5 * jnp.tanh(0.5 * x) + 0.5


def _params(sem):
    return pltpu.CompilerParams(dimension_semantics=sem, vmem_limit_bytes=VMEM_LIMIT)


def _resident(shape):
    nd = len(shape)
    return pl.BlockSpec(shape, lambda *_: (0,) * nd, pipeline_mode=pl.Buffered(1))


def _w_rows(c, tn):
    start = c * tn + (RANK if c * tn >= LR_COL else 0)
    return slice(start, start + tn)


LR_ROWS = slice(LR_COL, LR_COL + LRP)


def _forward(x, g0, w_t, ln_g, ln_b, w_s, b_sb, w_gate, b_gate, gla_g, riders=(), tm=256, tn=1024):
    T = x.shape[0]
    nsteps = T // tm
    n = len(riders)
    n_out = 7

    def body(x_ref, g_ref, w_ref, lg_ref, lb_ref, ws_ref, bs_ref, wg_ref, bg_ref, gg_ref, *rest):
        srcs, outs, dsts = rest[:n], rest[n:n + n_out], rest[n + n_out:2 * n + n_out]
        proj_ref, lr_ref, h_ref, a_ref, o_ref, ob_ref, st_ref = outs
        vln_s, state, la_s, *sems = rest[2 * n + n_out:]

        @pl.when(pl.program_id(0) == 0)
        def _():
            state[...] = jnp.zeros_like(state)
            if n:
                _gather_start(srcs, dsts, [False] * n, sems)

        xv = x_ref[...]
        r = lax.rsqrt(jnp.mean(xv * xv, axis=-1, keepdims=True) + EPS)
        h = (xv * r * g_ref[...]).astype(bf16)
        h_ref[...] = h
        lr_ref[...] = _dot_nt(h, w_ref[LR_ROWS, :])
        for c in range(NMAIN // tn):
            proj_ref[:, c * tn:(c + 1) * tn] = _dot_nt(h, w_ref[_w_rows(c, tn), :])

        def cols(block, width):
            return proj_ref.at[:, block * width:(block + 1) * width]

        _mixer_a_fwd_tile(cols(COL_U, D), cols(COL_V, D), cols(COL_ZA, D), lg_ref, lb_ref, ws_ref, bs_ref, a_ref, vln_s)
        _gla_fwd_tile(cols(COL_Q, KEYB), cols(COL_K, KEYB), cols(COL_VB, D), cols(COL_ZB, D), lr_ref, wg_ref, bg_ref,
                      gg_ref, o_ref, ob_ref, st_ref, state, la_s)

        if n:
            @pl.when(pl.program_id(0) == nsteps - 1)
            def _():
                _gather_finish(srcs, dsts, [False] * n, sems)

    row = lambda width: pl.BlockSpec((tm, width), lambda i: (i, 0))
    return pl.pallas_call(
        body,
        grid=(nsteps,),
        in_specs=[row(D), _resident((1, D)), _resident((NMAIN + RANK, D)), _resident((1, D)), _resident((1, D)),
                  _resident((HA, CA, CA)), _resident((HA, CA, GA)),
                  _resident((LRP, KEYB)), _resident((1, KEYB)), _resident((1, DV))] + [_ANY] * n,
        out_specs=[row(NMAIN), row(LRP), row(D), row(D), row(D), row(D),
                   pl.BlockSpec((tm // CB, HB, DV, DK), lambda i: (i, 0, 0, 0))] + [_ANY] * n,
        out_shape=[jax.ShapeDtypeStruct((T, NMAIN), f32), jax.ShapeDtypeStruct((T, LRP), f32),
                   jax.ShapeDtypeStruct((T, D), bf16), jax.ShapeDtypeStruct((T, D), bf16),
                   jax.ShapeDtypeStruct((T, D), f32), jax.ShapeDtypeStruct((T, D), bf16),
                   jax.ShapeDtypeStruct((T // CB, HB, DV, DK), f32)] + _gathered_shapes(riders),
        scratch_shapes=[pltpu.VMEM((tm, D), bf16), pltpu.VMEM((HB, DV, DK), f32), pltpu.VMEM((tm, KEYB), f32)]
        + (_gather_scratch(riders) if n else []),
        compiler_params=_params(("arbitrary",)),
        name="forward",
    )(x, g0, w_t, ln_g, ln_b, w_s, b_sb, w_gate, b_gate, gla_g, *riders)


def _causal_mask():
    t = lax.broadcasted_iota(jnp.int32, (CA, CA), 0)
    s = lax.broadcasted_iota(jnp.int32, (CA, CA), 1)
    return s <= t


def _layernorm_parts(gv):
    mu = jnp.mean(gv, axis=-1, keepdims=True)
    xc = gv - mu
    rs = lax.rsqrt(jnp.mean(xc * xc, axis=-1, keepdims=True) + LN_EPS)
    return xc * rs, rs


def _mixer_a_fwd_tile(u_ref, v_ref, za_ref, lg_ref, lb_ref, ws_ref, bs_ref, a_ref, vln_s):
    tm = u_ref.shape[0]
    vhat, _ = _layernorm_parts(_gelu(v_ref[...]))
    vln_s[...] = (vhat * lg_ref[...] + lb_ref[...]).astype(bf16)
    mask = _causal_mask()
    for g in range(HA):
        wg = jnp.where(mask, ws_ref[g], 0.0).astype(bf16)
        cols = slice(g * GA, (g + 1) * GA)
        for c in range(tm // CA):
            rows = slice(c * CA, (c + 1) * CA)
            mixed = _dot(wg, vln_s[rows, cols]) + bs_ref[g]
            za = za_ref[rows, cols]
            a = _gelu(u_ref[rows, cols]) * mixed * (za * _sigmoid(za))
            a_ref[rows, cols] = a.astype(bf16)


def _mixer_a_bwd(proj, da, dproj, ln_g, ln_b, w_s, b_sb, riders=(), kinds=(), tm=256):
    T = proj.shape[0]
    nsteps = T // tm
    n = len(riders)

    def body(u_ref, v_ref, za_ref, da_ref, dp_in, lg_ref, lb_ref, ws_ref, bs_ref, *rest):
        srcs, (dp_ref, dws_ref, dbs_ref, dlg_ref, dlb_ref), dsts = rest[:n], rest[n:n + 5], rest[n + 5:2 * n + 5]
        vln_s, dvln_s, *sems = rest[2 * n + 5:]
        del dp_in
        i = pl.program_id(0)

        @pl.when(i == 0)
        def _():
            dws_ref[...] = jnp.zeros_like(dws_ref)
            dbs_ref[...] = jnp.zeros_like(dbs_ref)
            dlg_ref[...] = jnp.zeros_like(dlg_ref)
            dlb_ref[...] = jnp.zeros_like(dlb_ref)
            for cp in _halves_copies(srcs, dsts, kinds, sems) if n else []:
                cp.start()

        gv, gv_grad = _gelu_and_grad(v_ref[...])
        vhat, rs = _layernorm_parts(gv)
        vln_s[...] = (vhat * lg_ref[...] + lb_ref[...]).astype(bf16)
        mask = _causal_mask()
        for g in range(HA):
            wg = jnp.where(mask, ws_ref[g], 0.0).astype(bf16)
            cols = slice(g * GA, (g + 1) * GA)
            dw_acc = jnp.zeros((CA, CA), f32)
            db_acc = jnp.zeros((CA, 1), f32)
            for c in range(tm // CA):
                rows = slice(c * CA, (c + 1) * CA)
                vln = vln_s[rows, cols]
                mixed = _dot(wg, vln) + bs_ref[g]
                u = u_ref[rows, cols]
                za = za_ref[rows, cols]
                da_blk = da_ref[rows, cols]
                sg = _sigmoid(za)
                sz = za * sg
                gu, gu_grad = _gelu_and_grad(u)
                dp_ref[rows, cols] = (da_blk * mixed * sz * gu_grad).astype(bf16)
                dp_ref[rows, 2 * D + g * GA:2 * D + (g + 1) * GA] = (
                    da_blk * gu * mixed * (sg * (1.0 + za * (1.0 - sg)))).astype(bf16)
                dmixed = da_blk * gu * sz
                dmb = dmixed.astype(bf16)
                dvln_s[rows, cols] = _dot_tn(wg, dmb)
                dw_acc = dw_acc + _dot_nt(dmb, vln)
                db_acc = db_acc + jnp.sum(dmixed, axis=-1, keepdims=True)
            dws_ref[g] += dw_acc
            dbs_ref[g] += jnp.broadcast_to(db_acc, (CA, GA))

        dvln = dvln_s[...]
        dlg_ref[...] += jnp.sum(dvln * vhat, axis=0, keepdims=True)
        dlb_ref[...] += jnp.sum(dvln, axis=0, keepdims=True)
        dvhat = dvln * lg_ref[...]
        dgv = rs * (dvhat - jnp.mean(dvhat, axis=-1, keepdims=True)
                    - vhat * jnp.mean(dvhat * vhat, axis=-1, keepdims=True))
        dp_ref[:, D:2 * D] = (dgv * gv_grad).astype(bf16)

        @pl.when(i == nsteps - 1)
        def _():
            for g in range(HA):
                dws_ref[g] = jnp.where(mask, dws_ref[g], 0.0)
            for cp in _halves_copies(srcs, dsts, kinds, sems) if n else []:
                cp.wait()

    def col(cidx):
        return pl.BlockSpec((tm, D), lambda i, c=cidx: (i, c))

    return pl.pallas_call(
        body,
        grid=(nsteps,),
        in_specs=[col(COL_U), col(COL_V), col(COL_ZA), pl.BlockSpec((tm, D), lambda i: (i, 0)),
                  pl.BlockSpec(memory_space=pl.ANY),
                  _resident((1, D)), _resident((1, D)), _resident((HA, CA, CA)), _resident((HA, CA, GA))] + [_ANY] * n,
        out_specs=[pl.BlockSpec((tm, 3 * D), lambda i: (i, 0)),
                   _resident((HA, CA, CA)), _resident((HA, CA, GA)), _resident((1, D)), _resident((1, D))] + [_ANY] * n,
        out_shape=[jax.ShapeDtypeStruct(dproj.shape, dproj.dtype),
                   jax.ShapeDtypeStruct((HA, CA, CA), f32), jax.ShapeDtypeStruct((HA, CA, GA), f32),
                   jax.ShapeDtypeStruct((1, D), f32), jax.ShapeDtypeStruct((1, D), f32)] + _halves_shapes(riders, kinds),
        scratch_shapes=[pltpu.VMEM((tm, D), bf16), pltpu.VMEM((tm, D), f32)] + (_halves_sems(riders) if n else []),
        input_output_aliases={4: 0},
        compiler_params=_params(("arbitrary",)),
        name="mixer_a_bwd",
    )(proj, proj, proj, da, dproj, ln_g, ln_b, w_s, b_sb, *riders)


def _tri(n, upper):
    r = lax.broadcasted_iota(jnp.int32, (n, n), 0)
    c = lax.broadcasted_iota(jnp.int32, (n, n), 1)
    return jnp.where((c >= r) if upper else (c <= r), 1.0, 0.0).astype(f32)


def _chunk_tri(n, upper):
    r = lax.broadcasted_iota(jnp.int32, (n, n), 0)
    c = lax.broadcasted_iota(jnp.int32, (n, n), 1)
    shift = CB.bit_length() - 1
    same_chunk = jnp.right_shift(r, shift) == jnp.right_shift(c, shift)
    return jnp.where(same_chunk & ((c >= r) if upper else (c <= r)), 1.0, 0.0).astype(f32)


def _log_alpha(lr, wg, bg):
    logit = _dot(lr.astype(bf16), wg.astype(bf16)) + bg
    la = (jnp.minimum(logit, 0.0) - jnp.log1p(jnp.exp(-jnp.abs(logit)))) * (1.0 / GATE_NORM)
    return logit, la


def _gla_fwd_tile(q_ref, k_ref, v_ref, zb_ref, lr_ref, wg_ref, bg_ref, gg_ref, o_ref, ob_ref, st_ref, state, la_s):
    tm = q_ref.shape[0]
    _, la = _log_alpha(lr_ref[...], wg_ref[...], bg_ref[...])
    la_s[...] = _dot_exact(_chunk_tri(tm, upper=False), la)
    causal = _tri(CB, upper=False) > 0.5
    states = [state[hd] for hd in range(HB)]
    for c in range(tm // CB):
        rows = slice(c * CB, (c + 1) * CB)
        b = la_s[rows, :]
        bl = b[CB - 1:CB, :]
        bm = b[CB // 2 - 1:CB // 2, :]
        q = q_ref[rows, :] * QSCALE
        k = k_ref[rows, :]
        qi_all = (q * jnp.exp(b - bm)).astype(bf16)
        ki_all = (k * jnp.exp(bm - b)).astype(bf16)
        qe_all = (q * jnp.exp(b)).astype(bf16)
        ks_all = (k * jnp.exp(bl - b)).astype(bf16)
        e_l = jnp.exp(bl)
        for hd in range(HB):
            kc = slice(hd * DK, (hd + 1) * DK)
            vc = slice(hd * DV, (hd + 1) * DV)
            v = v_ref[rows, vc].astype(bf16)
            p = jnp.where(causal, _dot_nt(qi_all[:, kc], ki_all[:, kc]), 0.0).astype(bf16)
            s0 = states[hd]
            st_ref[c, hd] = s0
            o = _dot(p, v) + _dot_nt(qe_all[:, kc], s0.astype(bf16))
            states[hd] = s0 * e_l[:, kc] + _dot_tn(v, ks_all[:, kc])
            o_ref[rows, vc] = o
            ro = lax.rsqrt(jnp.mean(o * o, axis=-1, keepdims=True) + EPS)
            zb = zb_ref[rows, vc]
            ob_ref[rows, vc] = (o * ro * gg_ref[...] * (zb * _sigmoid(zb))).astype(bf16)
    for hd in range(HB):
        state[hd] = states[hd]


def _gla_bwd(proj, lr, o, states, dob, dproj, w_gate, b_gate, gla_g, riders=(), sends_what=(), tm=256):
    T = proj.shape[0]
    cpb = tm // CB
    nb = T // tm
    n = len(riders)

    def body(q_ref, k_ref, v_ref, zb_ref, lr_ref, o_ref, st_ref, dob_ref, dp_in, wg_ref, bg_ref, gg_ref, *rest):
        srcs, (dp_ref, dlr_ref, dwg_ref, dbg_ref, dgg_ref), dsts = rest[:n], rest[n:n + 5], rest[n + 5:2 * n + 5]
        dstate, la_s, dlogit_s, tail_s, *sems = rest[2 * n + 5:]
        del dp_in
        step = pl.program_id(0)

        @pl.when(step == 0)
        def _():
            dstate[...] = jnp.zeros_like(dstate)
            dwg_ref[...] = jnp.zeros_like(dwg_ref)
            dbg_ref[...] = jnp.zeros_like(dbg_ref)
            dgg_ref[...] = jnp.zeros_like(dgg_ref)
            for cp in _exchange_copies(srcs, dsts, sends_what, sems) if n else []:
                cp.start()

        lr_v = lr_ref[...]
        logit, la = _log_alpha(lr_v, wg_ref[...], bg_ref[...])
        la_s[...] = _dot_exact(_chunk_tri(tm, upper=False), la)
        causal = _tri(CB, upper=False) > 0.5
        gg = gg_ref[...]
        dgg_acc = jnp.zeros((1, DV), f32)
        dstates = [dstate[hd] for hd in range(HB)]
        for c in reversed(range(cpb)):
            rows = slice(c * CB, (c + 1) * CB)
            b = la_s[rows, :]
            bl = b[CB - 1:CB, :]
            bm = b[CB // 2 - 1:CB // 2, :]
            eb_all, eqm_all, ekm_all = jnp.exp(b), jnp.exp(b - bm), jnp.exp(bm - b)
            eks_all, el_all = jnp.exp(bl - b), jnp.exp(bl)
            q_all = q_ref[rows, :] * QSCALE
            k_all = k_ref[rows, :]
            qi_all = (q_all * eqm_all).astype(bf16)
            ki_all = (k_all * ekm_all).astype(bf16)
            qe_all = (q_all * eb_all).astype(bf16)
            ksf_all = k_all * eks_all
            ks_all = ksf_all.astype(bf16)
            for hd in range(HB):
                kc = slice(hd * DK, (hd + 1) * DK)
                vc = slice(hd * DV, (hd + 1) * DV)
                o_h = o_ref[rows, vc]
                ro = lax.rsqrt(jnp.mean(o_h * o_h, axis=-1, keepdims=True) + EPS)
                ohat = o_h * ro
                zb = zb_ref[rows, vc]
                sg = _sigmoid(zb)
                dob_h = dob_ref[rows, vc]
                don = dob_h * (zb * sg)
                dp_ref[rows, 2 * D + hd * DV:2 * D + (hd + 1) * DV] = (
                    dob_h * ohat * gg * (sg * (1.0 + zb * (1.0 - sg)))).astype(bf16)
                dgg_acc = dgg_acc + jnp.sum(don * ohat, axis=0, keepdims=True)
                dohat = don * gg
                do = (ro * (dohat - ohat * jnp.mean(dohat * ohat, axis=-1, keepdims=True))).astype(bf16)
                e_b, e_qm, e_km, e_ks, e_l = eb_all[:, kc], eqm_all[:, kc], ekm_all[:, kc], eks_all[:, kc], el_all[:, kc]
                q, k, ks_f = q_all[:, kc], k_all[:, kc], ksf_all[:, kc]
                qi, ki, qe, ks = qi_all[:, kc], ki_all[:, kc], qe_all[:, kc], ks_all[:, kc]
                v = v_ref[rows, vc].astype(bf16)
                p = jnp.where(causal, _dot_nt(qi, ki), 0.0).astype(bf16)
                s0 = st_ref[c, hd]
                ds = dstates[hd]
                ds_b = ds.astype(bf16)
                dv = _dot_tn(p, do) + _dot_nt(ks, ds_b)
                dpm = jnp.where(causal, _dot_nt(do, v), 0.0).astype(bf16)
                dqi = _dot(dpm, ki)
                dki = _dot_tn(dpm, qi)
                dqe = _dot(do, s0.astype(bf16))
                dks = _dot(v, ds_b)
                dq_s = dqi * e_qm + dqe * e_b
                dk = dki * e_km + dks * e_ks
                tail = (jnp.sum(dks * ks_f, axis=0, keepdims=True)
                        + e_l * jnp.sum(ds * s0, axis=0, keepdims=True))
                dstates[hd] = _dot_tn(do, qe) + ds * e_l
                dp_ref[rows, kc] = (dq_s * QSCALE).astype(bf16)
                dp_ref[rows, KEYB + hd * DK:KEYB + (hd + 1) * DK] = dk.astype(bf16)
                dp_ref[rows, D + hd * DV:D + (hd + 1) * DV] = dv.astype(bf16)
                dlogit_s[rows, kc] = dq_s * q - dk * k
                tail_s[rows, kc] = jnp.broadcast_to(tail, (CB, DK))
        for hd in range(HB):
            dstate[hd] = dstates[hd]
        dgg_ref[...] += dgg_acc
        dg = _dot_exact(_chunk_tri(tm, upper=True), dlogit_s[...]) + tail_s[...]
        dlogit = dg * (1.0 / GATE_NORM) * _sigmoid(-logit)
        dbg_ref[...] += jnp.sum(dlogit, axis=0, keepdims=True)
        dlb = dlogit.astype(bf16)
        dlr_ref[...] = _dot_nt(dlb, wg_ref[...].astype(bf16)).astype(bf16)
        dwg_ref[...] += _dot_tn(lr_v.astype(bf16), dlb)

        if n:
            @pl.when(step == nb - 1)
            def _():
                for cp in _exchange_copies(srcs, dsts, sends_what, sems):
                    cp.wait()

    def rev(cidx):
        return lambda i, c=cidx: (nb - 1 - i, c)

    return pl.pallas_call(
        body,
        grid=(nb,),
        in_specs=[pl.BlockSpec((tm, KEYB), rev(COL_Q)),
                  pl.BlockSpec((tm, KEYB), rev(COL_K)),
                  pl.BlockSpec((tm, D), rev(COL_VB)),
                  pl.BlockSpec((tm, D), rev(COL_ZB)),
                  pl.BlockSpec((tm, LRP), rev(0)),
                  pl.BlockSpec((tm, D), rev(0)),
                  pl.BlockSpec((cpb, HB, DV, DK), lambda i: (nb - 1 - i, 0, 0, 0)),
                  pl.BlockSpec((tm, D), rev(0)),
                  pl.BlockSpec(memory_space=pl.ANY),
                  _resident((LRP, KEYB)), _resident((1, KEYB)), _resident((1, DV))] + [_ANY] * n,
        out_specs=[pl.BlockSpec((tm, 3 * D), rev(1)),
                   pl.BlockSpec((tm, LRP), rev(0)),
                   _resident((LRP, KEYB)), _resident((1, KEYB)), _resident((1, DV))] + [_ANY] * n,
        out_shape=[jax.ShapeDtypeStruct(dproj.shape, dproj.dtype),
                   jax.ShapeDtypeStruct((T, LRP), bf16),
                   jax.ShapeDtypeStruct((LRP, KEYB), f32), jax.ShapeDtypeStruct((1, KEYB), f32),
                   jax.ShapeDtypeStruct((1, DV), f32)] + _exchange_shapes(riders, sends_what),
        scratch_shapes=[pltpu.VMEM((HB, DV, DK), f32)] + [pltpu.VMEM((tm, KEYB), f32)] * 3
        + (_exchange_sems(n) if n else []),
        input_output_aliases={8: 0},
        compiler_params=_params(("arbitrary",)),
        name="gla_bwd",
    )(proj, proj, proj, proj, lr, o, states, dob, dproj, w_gate, b_gate, gla_g, *riders)


def _merge_fwd_bwd(x, tgt, proj, a, ob, w_a, w_b, w_o, g_f, tm=256):
    T = x.shape[0]

    def body(x_ref, t_ref, gt_ref, a_ref, ob_ref, wa_ref, wb_ref, wo_ref, gf_ref,
             dp_ref, dy_ref, da_ref, dob_ref, dwa_ref, dwb_ref, dwo_ref, dgf_ref, loss_ref):
        @pl.when(pl.program_id(0) == 0)
        def _():
            dwa_ref[...] = jnp.zeros_like(dwa_ref)
            dwb_ref[...] = jnp.zeros_like(dwb_ref)
            dwo_ref[...] = jnp.zeros_like(dwo_ref)
            dgf_ref[...] = jnp.zeros_like(dgf_ref)
            loss_ref[...] = jnp.zeros_like(loss_ref)

        ga = _sigmoid(gt_ref[:, :D])
        gb = _sigmoid(gt_ref[:, D:])
        a_v = a_ref[...]
        ob_v = ob_ref[...]
        pa = _dot(a_v, wa_ref[...])
        pb = _dot(ob_v, wb_ref[...])
        mb = (ga * pa + gb * pb).astype(bf16)
        y = x_ref[...] + _dot(mb, wo_ref[...])
        r1 = lax.rsqrt(jnp.mean(y * y, axis=-1, keepdims=True) + EPS)
        yhat = y * r1
        gf = gf_ref[...]
        err = yhat * gf - t_ref[...]
        loss_ref[...] += jnp.sum(err * err, axis=0, keepdims=True) * (0.5 / D)
        dout = err * (1.0 / D)
        dgf_ref[...] += jnp.sum(dout * yhat, axis=0, keepdims=True)
        dyn = dout * gf
        dy = r1 * (dyn - yhat * jnp.mean(dyn * yhat, axis=-1, keepdims=True))
        dy_ref[...] = dy
        dyb = dy.astype(bf16)
        dwo_ref[...] += _dot_tn(mb, dyb)
        dm = _dot_nt(dyb, wo_ref[...])
        dpa = (dm * ga).astype(bf16)
        dpb = (dm * gb).astype(bf16)
        dp_ref[:, :D] = (dm * pa * ga * (1.0 - ga)).astype(bf16)
        dp_ref[:, D:] = (dm * pb * gb * (1.0 - gb)).astype(bf16)
        dwa_ref[...] += _dot_tn(a_v, dpa)
        dwb_ref[...] += _dot_tn(ob_v, dpb)
        da_ref[...] = _dot_nt(dpa, wa_ref[...])
        dob_ref[...] = _dot_nt(dpb, wb_ref[...])

    row = lambda: pl.BlockSpec((tm, D), lambda i: (i, 0))
    return pl.pallas_call(
        body,
        grid=(T // tm,),
        in_specs=[row(), row(), pl.BlockSpec((tm, 2 * D), lambda i: (i, COL_GATES)), row(), row(),
                  _resident((D, D)), _resident((D, D)), _resident((D, D)), _resident((1, D))],
        out_specs=[pl.BlockSpec((tm, 2 * D), lambda i: (i, COL_GATES)), row(), row(), row(),
                   _resident((D, D)), _resident((D, D)), _resident((D, D)), _resident((1, D)), _resident((1, D))],
        out_shape=[jax.ShapeDtypeStruct((T, NMAIN), bf16),
                   jax.ShapeDtypeStruct((T, D), f32), jax.ShapeDtypeStruct((T, D), f32),
                   jax.ShapeDtypeStruct((T, D), f32),
                   jax.ShapeDtypeStruct((D, D), f32), jax.ShapeDtypeStruct((D, D), f32),
                   jax.ShapeDtypeStruct((D, D), f32),
                   jax.ShapeDtypeStruct((1, D), f32), jax.ShapeDtypeStruct((1, D), f32)],
        compiler_params=_params(("arbitrary",)),
        name="merge_fwd_bwd",
    )(x, tgt, proj, a, ob, w_a, w_b, w_o, g_f)


def _dx_bwd(x, dy, dproj, dlr, g0, w_t, riders=(), sends_what=(), tm=256):
    T = x.shape[0]
    nsteps = T // tm
    n = len(riders)

    def body(x_ref, dy_ref, dp_ref, dl_ref, g_ref, w_ref, *rest):
        srcs, (dx_ref, dg_ref), dsts, sems = rest[:n], rest[n:n + 2], rest[n + 2:2 * n + 2], rest[2 * n + 2:]

        @pl.when(pl.program_id(0) == 0)
        def _():
            dg_ref[...] = jnp.zeros_like(dg_ref)
            for cp in _exchange_copies(srcs, dsts, sends_what, sems) if n else []:
                cp.start()

        xv = x_ref[...]
        r = lax.rsqrt(jnp.mean(xv * xv, axis=-1, keepdims=True) + EPS)
        xhat = xv * r
        dh = (_dot(dp_ref[:, :LR_COL], w_ref[:LR_COL, :]) + _dot(dp_ref[:, LR_COL:], w_ref[LR_COL + RANK:, :])
              + _dot(dl_ref[...], w_ref[LR_ROWS, :]))
        dg_ref[...] += jnp.sum(dh * xhat, axis=0, keepdims=True)
        t = dh * g_ref[...]
        dx_ref[...] = dy_ref[...] + r * (t - xhat * jnp.mean(t * xhat, axis=-1, keepdims=True))

        if n:
            @pl.when(pl.program_id(0) == nsteps - 1)
            def _():
                for cp in _exchange_copies(srcs, dsts, sends_what, sems):
                    cp.wait()

    row = lambda: pl.BlockSpec((tm, D), lambda i: (i, 0))
    return pl.pallas_call(
        body,
        grid=(nsteps,),
        in_specs=[row(), row(), pl.BlockSpec((tm, NMAIN), lambda i: (i, 0)),
                  pl.BlockSpec((tm, LRP), lambda i: (i, 0)),
                  _resident((1, D)), _resident((NMAIN + RANK, D))] + [_ANY] * n,
        out_specs=[row(), _resident((1, D))] + [_ANY] * n,
        out_shape=[jax.ShapeDtypeStruct((T, D), f32), jax.ShapeDtypeStruct((1, D), f32)]
        + _exchange_shapes(riders, sends_what),
        scratch_shapes=_exchange_sems(n) if n else [],
        compiler_params=_params(("arbitrary",)),
        name="dx_bwd",
    )(x, dy, dproj, dlr, g0, w_t, *riders)


def _dw_in(h, dproj, dlr, pair_dtype=None, tm=1024, tn=1024):
    T = h.shape[0]
    tm = min(tm, T)
    nj, nk = NMAIN // tn, T // tm
    lr_tile = LR_COL // tn
    pair = pair_dtype is not None
    hd = D // 2

    def body(h_ref, dp_ref, dl_ref, out_ref, acc, lr_acc, sems, lr_sem, *more):
        j, k = pl.program_id(0), pl.program_id(1)
        slot = j % 2

        def tile_row(jj):
            return pl.multiple_of(jj * tn + jnp.where(jj >= lr_tile, RANK, 0), 8)

        if pair:
            land, lr_land, part_buf, lr_part, swap_send, swap_recv = more
            c = lax.axis_index("c")
            mine = pl.ds(pl.multiple_of(c * hd, 128), hd)
            other = pl.ds(pl.multiple_of((1 - c) * hd, 128), hd)

            def tile_swap(jj, s):
                return _remote(acc.at[s, :, other], land.at[jj], swap_send.at[jj], swap_recv.at[jj], _sibling())

            def lr_swap():
                return _remote(lr_acc.at[pl.ds(0, RANK), other], lr_land, swap_send.at[nj], swap_recv.at[nj], _sibling())

            def tile_out(jj, s):
                return pltpu.make_async_copy(part_buf.at[s], out_ref.at[pl.ds(tile_row(jj), tn)], sems.at[s])

            def finish_tile(jj, s):
                tile_swap(jj, s).wait()
                part_buf[s] = (acc[s, :, mine] + land[jj]).astype(pair_dtype)
                tile_out(jj, s).start()

            lr_out = pltpu.make_async_copy(lr_part, out_ref.at[pl.ds(LR_COL, RANK)], lr_sem)
        else:
            def tile_out(jj, s):
                return pltpu.make_async_copy(acc.at[s], out_ref.at[pl.ds(tile_row(jj), tn)], sems.at[s])

            lr_out = pltpu.make_async_copy(lr_acc.at[pl.ds(0, RANK)], out_ref.at[pl.ds(LR_COL, RANK)], lr_sem)

        @pl.when(j == 0)
        def _():
            @pl.when(k == 0)
            def _():
                lr_acc[...] = jnp.zeros_like(lr_acc)

            lr_acc[...] += _dot_tn(dl_ref[...], h_ref[...])

            @pl.when(k == nk - 1)
            def _():
                if pair:
                    lr_swap().start()
                else:
                    lr_out.start()

        @pl.when(k == 0)
        def _():
            acc[slot] = jnp.zeros((tn, D), f32)

        acc[slot] += _dot_tn(dp_ref[...], h_ref[...])

        @pl.when(k == nk - 1)
        def _():
            if pair:
                tile_swap(j, slot).start()

                @pl.when(j >= 3)
                def _():
                    tile_out(j - 3, 1 - slot).wait()

                @pl.when(j >= 1)
                def _():
                    finish_tile(j - 1, 1 - slot)

                @pl.when(j == nj - 1)
                def _():
                    tile_out(j - 2, slot).wait()
                    finish_tile(j, slot)
                    lr_swap().wait()
                    lr_part[...] = (lr_acc[0:RANK, mine] + lr_land[...]).astype(pair_dtype)
                    lr_out.start()
                    tile_out(j - 1, 1 - slot).wait()
                    tile_out(j, slot).wait()
                    lr_out.wait()
            else:
                tile_out(j, slot).start()

                @pl.when(j > 0)
                def _():
                    tile_out(j - 1, 1 - slot).wait()

                @pl.when(j == nj - 1)
                def _():
                    tile_out(j, slot).wait()
                    lr_out.wait()

    pair_scratch = [pltpu.VMEM((nj, tn, hd), f32), pltpu.VMEM((RANK, hd), f32), pltpu.VMEM((2, tn, hd), pair_dtype),
                    pltpu.VMEM((RANK, hd), pair_dtype)] + [pltpu.SemaphoreType.DMA((nj + 1,))] * 2 if pair else []
    return pl.pallas_call(
        body,
        grid=(nj, nk),
        in_specs=[pl.BlockSpec((tm, D), lambda j, k: (k, 0)), pl.BlockSpec((tm, tn), lambda j, k: (k, j)),
                  pl.BlockSpec((tm, LRP), lambda j, k: (k, 0))],
        out_specs=_ANY,
        out_shape=jax.ShapeDtypeStruct((NMAIN + RANK, hd), pair_dtype) if pair
        else jax.ShapeDtypeStruct((NMAIN + RANK, D), f32),
        scratch_shapes=[pltpu.VMEM((2, tn, D), f32), pltpu.VMEM((LRP, D), f32),
                        pltpu.SemaphoreType.DMA((2,)), pltpu.SemaphoreType.DMA] + pair_scratch,
        compiler_params=_params(("arbitrary", "arbitrary")),
        name="dw_in",
    )(h, dproj, dlr)


MESH = pl.DeviceIdType.MESH


def _place():
    x, y, c = lax.axis_index("x"), lax.axis_index("y"), lax.axis_index("c")
    others = [(1 - x, y), (x, 1 - y), (1 - x, 1 - y)]
    return x, y, c, 2 * x + y, others


def _sibling():
    return lax.axis_index("x"), lax.axis_index("y"), 1 - lax.axis_index("c")


def _remote(src, dst, send_sem, recv_sem, to):
    return pltpu.make_async_remote_copy(src_ref=src, dst_ref=dst, send_sem=send_sem, recv_sem=recv_sem,
                                        device_id=to, device_id_type=MESH)


def _half(ref, e, by_columns):
    if not by_columns:
        return ref.at[e]
    hw = ref.shape[-1] // 2
    return ref.at[:, pl.ds(pl.multiple_of(e * hw, 128), hw)]


RELAY_ROWS = 1024


def _gather_win(shard, small):
    h, w = shard.shape
    part_a, part_b = pl.ds(0, RELAY_ROWS), pl.ds(RELAY_ROWS, h - RELAY_ROWS)

    def body(src, small_src, dst, small_dst, send_sems, recv_sems, relay_send, relay_recv, pass_send, pass_recv, own_sems,
             small_send, small_recv, small_own, stage, small_stage):
        x, y, c, me, others = _place()
        (to_x, to_y, _), sibling = others, (x, y, 1 - c)
        j_x, j_y, j_d = (2 * cx + cy for cx, cy in others)
        cols = pl.ds(pl.multiple_of(c * (w // 2), 128), w // 2)
        theirs = pl.ds(pl.multiple_of((1 - c) * (w // 2), 128), w // 2)

        def mine(j, rows=pl.ds(0, h)):
            return dst.at[j, rows, cols]

        small_sends = [_remote(small_src, small_dst.at[me], small_send.at[k], small_recv.at[k], (*to, c))
                       for k, to in enumerate(others)]
        for cp in small_sends:
            cp.start()
        small_in = pltpu.make_async_copy(small_src, small_stage, small_own.at[0])
        small_out = pltpu.make_async_copy(small_stage, small_dst.at[me], small_own.at[1])
        small_in.start()

        to_stage = pltpu.make_async_copy(src, stage, own_sems.at[0])
        to_slot = pltpu.make_async_copy(stage, dst.at[me], own_sems.at[1])
        sends = [_remote(src.at[:, cols], mine(me), send_sems.at[k], recv_sems.at[k], (*to, c))
                 for k, to in enumerate((to_x, to_y))]
        for cp in sends:
            cp.start()
        to_stage.start()
        relays = [_remote(mine(j_x, part_a), mine(j_x, part_a), relay_send.at[0], relay_recv.at[0], (*to_y, c)),
                  _remote(mine(j_y, part_b), mine(j_y, part_b), relay_send.at[1], relay_recv.at[1], (*to_x, c))]
        landed = [mine(j_x), mine(j_y), mine(j_d, part_a), mine(j_d, part_b)]
        passes = [_remote(place, place, pass_send.at[k], pass_recv.at[k], sibling) for k, place in enumerate(landed)]
        for k in range(2):
            _remote(src.at[:, cols], landed[k], send_sems.at[k], recv_sems.at[k], sibling).wait_recv()
            relays[k].start()
            passes[k].start()
        to_stage.wait()
        to_slot.start()
        for k in range(2):
            _remote(landed[2 + k], landed[2 + k], relay_send.at[k], relay_recv.at[k], sibling).wait_recv()
            passes[2 + k].start()
        for k, place in enumerate([(j_x, pl.ds(0, h)), (j_y, pl.ds(0, h)), (j_d, part_a), (j_d, part_b)]):
            got = dst.at[place[0], place[1], theirs]
            _remote(got, got, pass_send.at[k], pass_recv.at[k], sibling).wait_recv()
        for cp in sends + relays + passes:
            cp.wait_send()
        to_slot.wait()
        small_in.wait()
        small_out.start()
        for k, (cx, cy) in enumerate(others):
            _remote(small_src, small_dst.at[2 * cx + cy], small_send.at[k], small_recv.at[k], sibling).wait_recv()
        for cp in small_sends:
            cp.wait_send()
        small_out.wait()

    return pl.pallas_call(
        body,
        in_specs=[_ANY] * 2,
        out_specs=[_ANY] * 2,
        out_shape=[jax.ShapeDtypeStruct((NCHIP, h, w), shard.dtype),
                   jax.ShapeDtypeStruct((NCHIP,) + small.shape, small.dtype)],
        scratch_shapes=[pltpu.SemaphoreType.DMA((2,))] * 4 + [pltpu.SemaphoreType.DMA((4,))] * 2
        + [pltpu.SemaphoreType.DMA((2,))] + [pltpu.SemaphoreType.DMA((3,))] * 2 + [pltpu.SemaphoreType.DMA((2,))]
        + [pltpu.VMEM((h, w), shard.dtype), pltpu.VMEM(small.shape, small.dtype)],
        compiler_params=pltpu.CompilerParams(vmem_limit_bytes=VMEM_LIMIT),
        name="gather_win",
    )(shard, small)


def _gathered_shapes(shards):
    return [jax.ShapeDtypeStruct((NCHIP,) + s.shape, s.dtype) for s in shards]


def _gather_scratch(shards):
    n = len(shards)
    return ([pltpu.SemaphoreType.DMA((3, n))] * 4 + [pltpu.SemaphoreType.DMA((2, n))]
            + [pltpu.VMEM(s.shape, s.dtype) for s in shards])


def _own_to_stage(srcs, scratch):
    own_sems, stages = scratch[4], scratch[5:]
    return [pltpu.make_async_copy(srcs[a], stages[a], own_sems.at[0, a]) for a in range(len(srcs))]


def _own_to_slot(dsts, scratch):
    own_sems, stages = scratch[4], scratch[5:]
    me = _place()[3]
    return [pltpu.make_async_copy(stages[a], dsts[a].at[me], own_sems.at[1, a]) for a in range(len(dsts))]


def _gather_copies(srcs, dsts, by_columns, sems, sends_only):
    n = len(srcs)
    send_sems, recv_sems, pass_send, pass_recv = sems[:4]
    x, y, c, me, others = _place()
    sibling = (x, y, 1 - c)

    def src(a, e):
        return _half(srcs[a], e, by_columns[a])

    def dst(a, j, e):
        return _half(dsts[a].at[j], e, by_columns[a])

    sends, arrivals, passes, passed = [], [], [], []
    for k, (cx, cy) in enumerate(others):
        j = 2 * cx + cy
        for a in range(n):
            sends.append(_remote(src(a, c), dst(a, me, c), send_sems.at[k, a], recv_sems.at[k, a], (cx, cy, c)))
            if not sends_only:
                arrivals.append(_remote(src(a, c), dst(a, j, c), send_sems.at[k, a], recv_sems.at[k, a], (cx, cy, c)))
                passes.append(_remote(dst(a, j, c), dst(a, j, c), pass_send.at[k, a], pass_recv.at[k, a], sibling))
                passed.append(_remote(src(a, c), dst(a, j, 1 - c), pass_send.at[k, a], pass_recv.at[k, a], sibling))
    return sends, arrivals, passes, passed


def _gather_start(srcs, dsts, by_columns, scratch):
    for cp in _gather_copies(srcs, dsts, by_columns, scratch, sends_only=True)[0]:
        cp.start()
    for cp in _own_to_stage(srcs, scratch):
        cp.start()


def _gather_finish(srcs, dsts, by_columns, scratch):
    for cp in _own_to_stage(srcs, scratch):
        cp.wait()
    own = _own_to_slot(dsts, scratch)
    for cp in own:
        cp.start()
    sends, arrivals, passes, passed = _gather_copies(srcs, dsts, by_columns, scratch, sends_only=False)
    for arrival, cp in zip(arrivals, passes):
        arrival.wait_recv()
        cp.start()
    for arrival in passed:
        arrival.wait_recv()
    for cp in sends + passes:
        cp.wait_send()
    for cp in own:
        cp.wait()


HALF_FIRST, CHIP_FIRST, BY_COLUMNS = "half_first", "chip_first", "by_columns"


def _sibling_halves(bufs, kinds):
    n = len(bufs)

    def body(*refs):
        cps = _halves_copies(refs[:n], refs[n:2 * n], kinds, refs[2 * n:])
        for cp in cps:
            cp.start()
        for cp in cps:
            cp.wait()

    return pl.pallas_call(
        body,
        in_specs=[_ANY] * n,
        out_specs=[_ANY] * n,
        out_shape=_halves_shapes(bufs, kinds),
        scratch_shapes=_halves_sems(bufs),
        name="sibling_halves",
    )(*bufs)


def _halves_shapes(bufs, kinds):
    def landed(b, kind):
        if kind == HALF_FIRST:
            return b.shape[1:]
        if kind == CHIP_FIRST:
            return (b.shape[0],) + b.shape[2:]
        return b.shape[:2] + (b.shape[2] // 2,)

    return [jax.ShapeDtypeStruct(landed(b, kind), b.dtype) for b, kind in zip(bufs, kinds)]


def _halves_sems(bufs):
    return [pltpu.SemaphoreType.DMA((len(bufs), NCHIP))] * 2


def _halves_copies(srcs, dsts, kinds, sems):
    send_sems, recv_sems = sems
    x, y, c, _, _ = _place()
    cps = []
    for a, kind in enumerate(kinds):
        if kind == HALF_FIRST:
            cps.append(_remote(srcs[a].at[1 - c], dsts[a], send_sems.at[a, 0], recv_sems.at[a, 0], (x, y, 1 - c)))
        else:
            cps += [_remote(_half(srcs[a].at[j], 1 - c, kind == BY_COLUMNS), dsts[a].at[j],
                            send_sems.at[a, j], recv_sems.at[a, j], (x, y, 1 - c)) for j in range(srcs[a].shape[0])]
    return cps


TO_ITS_CHIP, TO_EVERY_CHIP, ROWS_TO_ITS_CHIP = "to_its_chip", "to_every_chip", "rows_to_its_chip"
PIECE_STEP = 2048
PIECE_ROWS = 2064


def _chip_exchange(parts, sends_what):
    n = len(parts)

    def body(*refs):
        cps = _exchange_copies(refs[:n], refs[n:2 * n], sends_what, refs[2 * n:])
        for cp in cps:
            cp.start()
        for cp in cps:
            cp.wait()

    return pl.pallas_call(
        body,
        in_specs=[_ANY] * n,
        out_specs=[_ANY] * n,
        out_shape=_exchange_shapes(parts, sends_what),
        scratch_shapes=_exchange_sems(n),
        name="chip_exchange",
    )(*parts)


def _exchange_shapes(parts, sends_what):
    def landed(p, what):
        return (3, PIECE_ROWS, p.shape[1]) if what == ROWS_TO_ITS_CHIP else (3,) + p.shape[1:]

    return [jax.ShapeDtypeStruct(landed(p, what), p.dtype) for p, what in zip(parts, sends_what)]


def _exchange_sems(n):
    return [pltpu.SemaphoreType.DMA((3, n))] * 2


def _exchange_copies(srcs, dsts, sends_what, sems):
    send_sems, recv_sems = sems
    x, y, c, me, others = _place()

    def part(a, j):
        if sends_what[a] == ROWS_TO_ITS_CHIP:
            return srcs[a].at[pl.ds(pl.multiple_of(j * PIECE_STEP, PIECE_STEP), PIECE_ROWS)]
        return srcs[a].at[j if sends_what[a] == TO_ITS_CHIP else 0]

    return [_remote(part(a, 2 * cx + cy), dsts[a].at[k], send_sems.at[k, a], recv_sems.at[k, a], (cx, cy, c))
            for k, (cx, cy) in enumerate(others) for a in range(len(srcs))]


def _sibling_swap(halves):
    n = len(halves)

    def body(*refs):
        srcs, dsts = refs[:n], refs[n:2 * n]
        send_sems, recv_sems = refs[2 * n:]
        x, y, c, _, _ = _place()
        cps = [_remote(srcs[a], dsts[a], send_sems.at[a], recv_sems.at[a], (x, y, 1 - c)) for a in range(n)]
        for cp in cps:
            cp.start()
        for cp in cps:
            cp.wait()

    return pl.pallas_call(
        body,
        in_specs=[_ANY] * n,
        out_specs=[_ANY] * n,
        out_shape=[jax.ShapeDtypeStruct(s.shape, s.dtype) for s in halves],
        scratch_shapes=[pltpu.SemaphoreType.DMA((n,))] * 2,
        name="sibling_swap",
    )(*halves)


def _tile(h, w, operands=5):
    if h % 128 == 0:
        return 128, w
    budget = VMEM_LIMIT * 3 // 4 // (2 * operands * 4)
    tw = w
    while h * tw > budget and tw % 256 == 0:
        tw //= 2
    return h, tw


def _pair_sum(place, bufs, kind, gots, out_dtype):
    m = len(bufs)
    nj, h, w = gots[0].shape
    th, tw = _tile(h, w, operands=3 * m)
    nq = w // tw

    def body(p_ref, *refs):
        del p_ref
        for a_ref, b_ref, o_ref in zip(refs[:m], refs[m:2 * m], refs[2 * m:]):
            o_ref[...] = (a_ref[...] + b_ref[...]).astype(out_dtype)

    if kind == HALF_FIRST:
        mine = pl.BlockSpec((None, None, th, tw), lambda j, r, q, p: (p[0], j, r, q))
    elif kind == CHIP_FIRST:
        mine = pl.BlockSpec((None, None, th, tw), lambda j, r, q, p: (j, p[0], r, q))
    else:
        mine = pl.BlockSpec((None, th, tw), lambda j, r, q, p: (j, r, p[0] * nq + q))
    landed = pl.BlockSpec((None, th, tw), lambda j, r, q, p: (j, r, q))
    return pl.pallas_call(
        body,
        grid_spec=pltpu.PrefetchScalarGridSpec(
            num_scalar_prefetch=1,
            grid=(nj, h // th, w // tw),
            in_specs=[mine] * m + [landed] * m,
            out_specs=[landed] * m,
        ),
        out_shape=[jax.ShapeDtypeStruct((nj, h, w), out_dtype)] * m,
        compiler_params=_params(("parallel", "parallel", "parallel")),
        name="pair_sum",
    )(place, *bufs, *gots)


def _chip_sum(place, parts, slots):
    m = len(parts)
    nj, h, w = parts[0].shape
    th, tw = _tile(h, w, operands=5 * m)

    def body(p_ref, *refs):
        me = p_ref[1]
        for own_ref, s_ref, o_ref in zip(refs[:m], refs[m:2 * m], refs[2 * m:]):
            own = own_ref[...].astype(f32)
            by_flip = {2: s_ref[0].astype(f32), 1: s_ref[1].astype(f32), 3: s_ref[2].astype(f32)}
            acc = None
            for j in range(NCHIP):
                flip = me ^ j
                term = jnp.where(flip == 0, own,
                                 jnp.where(flip == 2, by_flip[2], jnp.where(flip == 1, by_flip[1], by_flip[3])))
                acc = term if acc is None else acc + term
            o_ref[...] = acc

    return pl.pallas_call(
        body,
        grid_spec=pltpu.PrefetchScalarGridSpec(
            num_scalar_prefetch=1,
            grid=(h // th, w // tw),
            in_specs=[pl.BlockSpec((None, th, tw), lambda r, q, p: (p[1] if nj == NCHIP else 0, r, q))] * m
            + [pl.BlockSpec((3, th, tw), lambda r, q, p: (0, r, q))] * m,
            out_specs=[pl.BlockSpec((th, tw), lambda r, q, p: (r, q))] * m,
        ),
        out_shape=[jax.ShapeDtypeStruct((h, w), f32)] * m,
        compiler_params=_params(("parallel", "parallel")),
        name="chip_sum",
    )(place, *parts, *slots)


def _adamw_math(w, g, m, v):
    nm = ADAM_B1 * m + (1.0 - ADAM_B1) * g
    nv = ADAM_B2 * v + (1.0 - ADAM_B2) * (g * g)
    m_hat = nm / (1.0 - ADAM_B1 ** ADAM_STEP)
    v_hat = nv / (1.0 - ADAM_B2 ** ADAM_STEP)
    return -ADAM_LR * (m_hat / (jnp.sqrt(v_hat) + ADAM_EPS) + ADAM_WD * w), nm, nv


def _adamw(w, g, m, v):
    rows, width = w.shape
    th, tw = _tile(rows, width, operands=7)

    def body(w_ref, g_ref, m_ref, v_ref, d_ref, nm_ref, nv_ref):
        d_ref[...], nm_ref[...], nv_ref[...] = _adamw_math(w_ref[...], g_ref[...], m_ref[...], v_ref[...])

    spec = pl.BlockSpec((th, tw), lambda r, q: (r, q))
    return pl.pallas_call(
        body,
        grid=(rows // th, width // tw),
        in_specs=[spec] * 4,
        out_specs=[spec] * 3,
        out_shape=[jax.ShapeDtypeStruct((rows, width), f32)] * 3,
        compiler_params=_params(("parallel", "parallel")),
        name="adamw",
    )(w, g, m, v)


def _adamw_halves(place, ws, mines, gots, ms, vs, axis):
    k = len(ws)
    rows, width = ws[0].shape
    h, hw = mines[0].shape
    th, tw = _tile(h, hw, operands=10 * k)
    nr, nq = h // th, hw // tw

    def body(p_ref, *refs):
        ins, outs = refs[:5 * k], refs[5 * k:]
        for i in range(k):
            w_ref, a_ref, b_ref, m_ref, v_ref = ins[i::k]
            g_ref, d_ref, nm_ref, nv_ref = outs[i::k]
            g = jnp.where(pl.program_id(0) == p_ref[0], a_ref[...], b_ref[...])
            g_ref[...] = g
            d_ref[...], nm_ref[...], nv_ref[...] = _adamw_math(w_ref[...], g, m_ref[...], v_ref[...])

    if axis == 0:
        full = pl.BlockSpec((th, tw), lambda e, r, q, p: (e * nr + r, q))
    else:
        full = pl.BlockSpec((th, tw), lambda e, r, q, p: (r, e * nq + q))
    half = pl.BlockSpec((th, tw), lambda e, r, q, p: (r, q))
    res = pl.pallas_call(
        body,
        grid_spec=pltpu.PrefetchScalarGridSpec(
            num_scalar_prefetch=1,
            grid=(2, nr, nq),
            in_specs=[full] * k + [half] * (2 * k) + [full] * (2 * k),
            out_specs=[full] * (4 * k),
        ),
        out_shape=[jax.ShapeDtypeStruct((rows, width), f32)] * (4 * k),
        compiler_params=_params(("parallel", "parallel", "parallel")),
        name="adamw_halves",
    )(place, *ws, *mines, *gots, *ms, *vs)
    return [res[i::k] for i in range(k)]


_SMALL = (("norm_g", 8), ("ln_v_g", 8), ("ln_v_b", 8), ("w_spatial", 1024), ("b_spatial", 8), ("b_gate_up", 4),
          ("gla_norm_g", 2), ("final_norm_g", 8), ("w_gate_up", 64), ("loss", 8))
_SMALL_ROWS = 1152


def _pack_rows(arrays, rows):
    flat = jnp.concatenate([a.reshape(-1, 128) for a in arrays], axis=0)
    return jnp.pad(flat, ((0, rows - flat.shape[0]), (0, 0)))


def kernel(x, norm_g, w_in, ln_v_g, ln_v_b, w_spatial, b_spatial, w_gate_up, b_gate_up, gla_norm_g, w_branch_a, w_branch_b, w_out, final_norm_g, loss_target, m_norm_g, m_w_in, m_ln_v_g, m_ln_v_b, m_w_spatial, m_b_spatial, m_w_gate_up, m_b_gate_up, m_gla_norm_g, m_w_branch_a, m_w_branch_b, m_w_out, m_final_norm_g, v_norm_g, v_w_in, v_ln_v_g, v_ln_v_b, v_w_spatial, v_b_spatial, v_w_gate_up, v_b_gate_up, v_gla_norm_g, v_w_branch_a, v_w_branch_b, v_w_out, v_final_norm_g):
    chip = 2 * lax.axis_index("x") + lax.axis_index("y")
    core = lax.axis_index("c")
    place = jnp.stack([core, chip]).astype(jnp.int32)
    mat_names = ("w_branch_a", "w_branch_b", "w_out")

    wt_shard = jnp.transpose(w_in[0]).astype(bf16)
    mats = [w[0].astype(bf16).reshape(2, D // NCHIP // 2, D) for w in (w_branch_a, w_branch_b, w_out)]
    gate_sh = w_gate_up[0].reshape(2, RANK // 2, 128)
    g_win, g_gate = _gather_win(wt_shard, gate_sh)
    w_t = g_win.reshape(NCHIP * WIN_SHARD, D)
    w_gate = jnp.transpose(g_gate.reshape(NCHIP, RANK, 128), (1, 0, 2)).reshape(RANK, KEYB)
    w_gate = jnp.pad(w_gate, ((0, LRP - RANK), (0, 0)))
    b_sb = jnp.broadcast_to(b_spatial[0][:, :, None], (HA, CA, GA))
    xs, tgt = x[0], loss_target[0]

    proj, lr, h, a, o, ob, states, g_a, g_b, g_o = _forward(xs, norm_g, w_t, ln_v_g, ln_v_b, w_spatial[0], b_sb, w_gate,
                                                           b_gate_up, gla_norm_g, riders=mats)
    w_a, w_b, w_o = (g.reshape(D, D) for g in (g_a, g_b, g_o))
    dproj, dy, da, dob, dwa, dwb, dwo, dgf, loss_cols = _merge_fwd_bwd(xs, tgt, proj, a, ob, w_a, w_b, w_o,
                                                                       final_norm_g.reshape(1, D))
    b_mats = [t.reshape(NCHIP, 2, D // NCHIP // 2, D) for t in (dwa, dwb, dwo)]
    dproj, dws, dbs, dlg, dlb, *got_mats = _mixer_a_bwd(proj, da, dproj, ln_v_g, ln_v_b, w_spatial[0], b_sb,
                                                        riders=b_mats, kinds=[CHIP_FIRST] * 3)
    part_mats = _pair_sum(place, b_mats, CHIP_FIRST, got_mats, bf16)
    dproj, dlr, dwg, dbg, dgg, *slots_mats = _gla_bwd(proj, lr, o, states, dob, dproj, w_gate, b_gate_up, gla_norm_g,
                                                      riders=part_mats, sends_what=[TO_ITS_CHIP] * 3)
    part_win = _dw_in(h, dproj, dlr, pair_dtype=bf16)
    dx, dg0, slots_win = _dx_bwd(xs, dy, dproj, dlr, norm_g, w_t, riders=[part_win], sends_what=[ROWS_TO_ITS_CHIP])
    slots_big = [slots_win] + slots_mats
    small = _pack_rows([dg0, dlg, dlb, dws, dbs[:, :, 0], dbg, dgg, dgf, dwg[:RANK], loss_cols], _SMALL_ROWS)
    b_small = small.reshape(2, 1, _SMALL_ROWS // 2, 128)
    (got_small,) = _sibling_halves([b_small], [HALF_FIRST])
    (part_small,) = _pair_sum(place, [b_small], HALF_FIRST, [got_small], f32)
    (slots_small,) = _chip_exchange([part_small], [TO_EVERY_CHIP])
    own_win = lax.dynamic_slice_in_dim(part_win, chip * PIECE_STEP, PIECE_ROWS, axis=0)[None]
    mine = [*_chip_sum(place, [own_win], slots_big[:1]), *_chip_sum(place, part_mats, slots_big[1:]),
            *_chip_sum(place, [part_small], [slots_small])]
    theirs = list(_sibling_swap(mine))
    mine[0], theirs[0] = (lax.dynamic_slice_in_dim(t, (WIN_SHARD - PIECE_STEP) * chip, WIN_SHARD, axis=0)
                          for t in (mine[0], theirs[0]))

    g_small = jnp.where(core == 0, jnp.concatenate([mine[4], theirs[4]], axis=0),
                        jnp.concatenate([theirs[4], mine[4]], axis=0))
    grads = {}
    row = 0
    for name, rows in _SMALL:
        grads[name] = g_small[row:row + rows]
        row += rows
    loss = jnp.sum(grads["loss"])
    dwg_full = grads["w_gate_up"].reshape(RANK, KEYB)
    grads["w_gate_up"] = lax.dynamic_slice_in_dim(dwg_full, chip * 128, 128, axis=1)

    weights = dict(norm_g=norm_g, w_in=w_in, ln_v_g=ln_v_g, ln_v_b=ln_v_b, w_spatial=w_spatial, b_spatial=b_spatial,
                   w_gate_up=w_gate_up, b_gate_up=b_gate_up, gla_norm_g=gla_norm_g, w_branch_a=w_branch_a,
                   w_branch_b=w_branch_b, w_out=w_out, final_norm_g=final_norm_g)
    m_in = dict(norm_g=m_norm_g, w_in=m_w_in, ln_v_g=m_ln_v_g, ln_v_b=m_ln_v_b, w_spatial=m_w_spatial,
                b_spatial=m_b_spatial, w_gate_up=m_w_gate_up, b_gate_up=m_b_gate_up, gla_norm_g=m_gla_norm_g,
                w_branch_a=m_w_branch_a, w_branch_b=m_w_branch_b, w_out=m_w_out, final_norm_g=m_final_norm_g)
    v_in = dict(norm_g=v_norm_g, w_in=v_w_in, ln_v_g=v_ln_v_g, ln_v_b=v_ln_v_b, w_spatial=v_w_spatial,
                b_spatial=v_b_spatial, w_gate_up=v_w_gate_up, b_gate_up=v_b_gate_up, gla_norm_g=v_gla_norm_g,
                w_branch_a=v_w_branch_a, w_branch_b=v_w_branch_b, w_out=v_w_out, final_norm_g=v_final_norm_g)
    names = list(weights)
    small_names = [n for n in names if n != "w_in" and n not in mat_names]
    out_g, out_d, out_m, out_v = {}, {}, {}, {}
    (res,) = _adamw_halves(place, [jnp.transpose(w_in[0])], mine[:1], theirs[:1], [jnp.transpose(m_w_in[0])],
                           [jnp.transpose(v_w_in[0])], axis=1)
    out_g["w_in"], out_d["w_in"], out_m["w_in"], out_v["w_in"] = (jnp.transpose(t)[None] for t in res)
    res_mats = _adamw_halves(place, [weights[n][0] for n in mat_names], mine[1:4], theirs[1:4],
                             [m_in[n][0] for n in mat_names], [v_in[n][0] for n in mat_names], axis=0)
    for n, res in zip(mat_names, res_mats):
        out_g[n], out_d[n], out_m[n], out_v[n] = (t[None] for t in res)
    upd_rows = sum(weights[n].size for n in small_names) // 128
    pad_rows = -(-upd_rows // 8) * 8
    packed = [_pack_rows([t[n] for n in small_names], pad_rows) for t in (weights, grads, m_in, v_in)]
    d_s, m_s, v_s = _adamw(*packed)
    row = 0
    for n in small_names:
        shape = weights[n].shape
        rows = weights[n].size // 128
        out_g[n] = grads[n].reshape(shape)
        out_d[n], out_m[n], out_v[n] = (t[row:row + rows].reshape(shape) for t in (d_s, m_s, v_s))
        row += rows
    return (loss, dx[None], *[out_g[n] for n in names], *[out_d[n] for n in names],
            *[out_m[n] for n in names], *[out_v[n] for n in names])
```

```python
import functools
import math

import jax
import jax.numpy as jnp
from jax import lax
from jax.experimental import pallas as pl
from jax.experimental.pallas import tpu as pltpu

f32 = jnp.float32
bf16 = jnp.bfloat16

D = 1024
NMAIN = 8192
LRP = 128
RANK = 16
HA, GA, CA = 8, 128, 128
HB, DK, DV, CB = 4, 128, 256, 64
KEYB = HB * DK
EPS = 1e-6
LN_EPS = 1e-5
GATE_NORM = 16.0
QSCALE = DK ** -0.5
COL_U, COL_V, COL_ZA = 0, 1, 2
COL_Q, COL_K = 6, 7
COL_VB, COL_ZB = 4, 5
COL_GATES = 3
VMEM_LIMIT = 56 * 1024 * 1024
NCHIP = 4
WIN_SHARD = 2052
LR_COL = 6144
_ANY = pl.BlockSpec(memory_space=pl.ANY)

ADAM_LR, ADAM_B1, ADAM_B2, ADAM_EPS, ADAM_WD, ADAM_STEP = 0.001, 0.9, 0.999, 1e-08, 0.01, 10

_SQRT_HALF = 0.7071067811865476
_INV_SQRT_2PI = 0.3989422804014327


def _dot(a, b):
    return jnp.dot(a, b, preferred_element_type=f32)


def _dot_nt(a, b):
    return lax.dot_general(a, b, (((1,), (1,)), ((), ())), preferred_element_type=f32)


def _dot_tn(a, b):
    return lax.dot_general(a, b, (((0,), (0,)), ((), ())), preferred_element_type=f32)


def _dot_exact(a, b):
    return jnp.dot(a, b, preferred_element_type=f32, precision=lax.Precision.HIGHEST)


def _gelu(x):
    return 0.5 * x * (1.0 + lax.erf(x * _SQRT_HALF))


def _gelu_and_grad(x):
    cdf = 0.5 * (1.0 + lax.erf(x * _SQRT_HALF))
    return x * cdf, cdf + x * (jnp.exp(-0.5 * x * x) * _INV_SQRT_2PI)


def _sigmoid(x):
    return 0.5 * jnp.tanh(0.5 * x) + 0.5


def _params(sem):
    return pltpu.CompilerParams(dimension_semantics=sem, vmem_limit_bytes=VMEM_LIMIT)


def _resident(shape):
    nd = len(shape)
    return pl.BlockSpec(shape, lambda *_: (0,) * nd, pipeline_mode=pl.Buffered(1))


def _w_rows(c, tn):
    start = c * tn + (RANK if c * tn >= LR_COL else 0)
    return slice(start, start + tn)


LR_ROWS = slice(LR_COL, LR_COL + LRP)


def _forward(x, g0, w_t, ln_g, ln_b, w_s, b_sb, w_gate, b_gate, gla_g, riders=(), tm=256, tn=1024):
    T = x.shape[0]
    nsteps = T // tm
    n = len(riders)
    n_out = 7

    def body(x_ref, g_ref, w_ref, lg_ref, lb_ref, ws_ref, bs_ref, wg_ref, bg_ref, gg_ref, *rest):
        srcs, outs, dsts = rest[:n], rest[n:n + n_out], rest[n + n_out:2 * n + n_out]
        proj_ref, lr_ref, h_ref, a_ref, o_ref, ob_ref, st_ref = outs
        vln_s, state, la_s, *sems = rest[2 * n + n_out:]

        @pl.when(pl.program_id(0) == 0)
        def _():
            state[...] = jnp.zeros_like(state)
            if n:
                _gather_start(srcs, dsts, [False] * n, sems)

        xv = x_ref[...]
        r = lax.rsqrt(jnp.mean(xv * xv, axis=-1, keepdims=True) + EPS)
        h = (xv * r * g_ref[...]).astype(bf16)
        h_ref[...] = h
        lr_ref[...] = _dot_nt(h, w_ref[LR_ROWS, :])
        for c in range(NMAIN // tn):
            proj_ref[:, c * tn:(c + 1) * tn] = _dot_nt(h, w_ref[_w_rows(c, tn), :])

        def cols(block, width):
            return proj_ref.at[:, block * width:(block + 1) * width]

        _mixer_a_fwd_tile(cols(COL_U, D), cols(COL_V, D), cols(COL_ZA, D), lg_ref, lb_ref, ws_ref, bs_ref, a_ref, vln_s)
        _gla_fwd_tile(cols(COL_Q, KEYB), cols(COL_K, KEYB), cols(COL_VB, D), cols(COL_ZB, D), lr_ref, wg_ref, bg_ref,
                      gg_ref, o_ref, ob_ref, st_ref, state, la_s)

        if n:
            @pl.when(pl.program_id(0) == nsteps - 1)
            def _():
                _gather_finish(srcs, dsts, [False] * n, sems)

    row = lambda width: pl.BlockSpec((tm, width), lambda i: (i, 0))
    return pl.pallas_call(
        body,
        grid=(nsteps,),
        in_specs=[row(D), _resident((1, D)), _resident((NMAIN + RANK, D)), _resident((1, D)), _resident((1, D)),
                  _resident((HA, CA, CA)), _resident((HA, CA, GA)),
                  _resident((LRP, KEYB)), _resident((1, KEYB)), _resident((1, DV))] + [_ANY] * n,
        out_specs=[row(NMAIN), row(LRP), row(D), row(D), row(D), row(D),
                   pl.BlockSpec((tm // CB, HB, DV, DK), lambda i: (i, 0, 0, 0))] + [_ANY] * n,
        out_shape=[jax.ShapeDtypeStruct((T, NMAIN), f32), jax.ShapeDtypeStruct((T, LRP), f32),
                   jax.ShapeDtypeStruct((T, D), bf16), jax.ShapeDtypeStruct((T, D), bf16),
                   jax.ShapeDtypeStruct((T, D), f32), jax.ShapeDtypeStruct((T, D), bf16),
                   jax.ShapeDtypeStruct((T // CB, HB, DV, DK), f32)] + _gathered_shapes(riders),
        scratch_shapes=[pltpu.VMEM((tm, D), bf16), pltpu.VMEM((HB, DV, DK), f32), pltpu.VMEM((tm, KEYB), f32)]
        + (_gather_scratch(riders) if n else []),
        compiler_params=_params(("arbitrary",)),
        name="forward",
    )(x, g0, w_t, ln_g, ln_b, w_s, b_sb, w_gate, b_gate, gla_g, *riders)


def _causal_mask():
    t = lax.broadcasted_iota(jnp.int32, (CA, CA), 0)
    s = lax.broadcasted_iota(jnp.int32, (CA, CA), 1)
    return s <= t


def _layernorm_parts(gv):
    mu = jnp.mean(gv, axis=-1, keepdims=True)
    xc = gv - mu
    rs = lax.rsqrt(jnp.mean(xc * xc, axis=-1, keepdims=True) + LN_EPS)
    return xc * rs, rs


def _mixer_a_fwd_tile(u_ref, v_ref, za_ref, lg_ref, lb_ref, ws_ref, bs_ref, a_ref, vln_s):
    tm = u_ref.shape[0]
    vhat, _ = _layernorm_parts(_gelu(v_ref[...]))
    vln_s[...] = (vhat * lg_ref[...] + lb_ref[...]).astype(bf16)
    mask = _causal_mask()
    for g in range(HA):
        wg = jnp.where(mask, ws_ref[g], 0.0).astype(bf16)
        cols = slice(g * GA, (g + 1) * GA)
        for c in range(tm // CA):
            rows = slice(c * CA, (c + 1) * CA)
            mixed = _dot(wg, vln_s[rows, cols]) + bs_ref[g]
            za = za_ref[rows, cols]
            a = _gelu(u_ref[rows, cols]) * mixed * (za * _sigmoid(za))
            a_ref[rows, cols] = a.astype(bf16)


def _mixer_a_bwd_body(nsteps, tm, n, kinds):
    def body(u_ref, v_ref, za_ref, da_ref, dp_in, lg_ref, lb_ref, ws_ref, bs_ref, *rest):
        srcs, (dp_ref, dws_ref, dbs_ref, dlg_ref, dlb_ref), dsts = rest[:n], rest[n:n + 5], rest[n + 5:2 * n + 5]
        vln_s, dvln_s, *sems = rest[2 * n + 5:]
        del dp_in
        i = pl.program_id(0)

        @pl.when(i == 0)
        def _():
            dws_ref[...] = jnp.zeros_like(dws_ref)
            dbs_ref[...] = jnp.zeros_like(dbs_ref)
            dlg_ref[...] = jnp.zeros_like(dlg_ref)
            dlb_ref[...] = jnp.zeros_like(dlb_ref)
            for cp in _halves_copies(srcs, dsts, kinds, sems) if n else []:
                cp.start()

        gv, gv_grad = _gelu_and_grad(v_ref[...])
        vhat, rs = _layernorm_parts(gv)
        vln_s[...] = (vhat * lg_ref[...] + lb_ref[...]).astype(bf16)
        mask = _causal_mask()
        for g in range(HA):
            wg = jnp.where(mask, ws_ref[g], 0.0).astype(bf16)
            cols = slice(g * GA, (g + 1) * GA)
            dw_acc = jnp.zeros((CA, CA), f32)
            db_acc = jnp.zeros((CA, 1), f32)
            for c in range(tm // CA):
                rows = slice(c * CA, (c + 1) * CA)
                vln = vln_s[rows, cols]
                mixed = _dot(wg, vln) + bs_ref[g]
                u = u_ref[rows, cols]
                za = za_ref[rows, cols]
                da_blk = da_ref[rows, cols]
                sg = _sigmoid(za)
                sz = za * sg
                gu, gu_grad = _gelu_and_grad(u)
                dp_ref[rows, cols] = (da_blk * mixed * sz * gu_grad).astype(bf16)
                dp_ref[rows, 2 * D + g * GA:2 * D + (g + 1) * GA] = (
                    da_blk * gu * mixed * (sg * (1.0 + za * (1.0 - sg)))).astype(bf16)
                dmixed = da_blk * gu * sz
                dmb = dmixed.astype(bf16)
                dvln_s[rows, cols] = _dot_tn(wg, dmb)
                dw_acc = dw_acc + _dot_nt(dmb, vln)
                db_acc = db_acc + jnp.sum(dmixed, axis=-1, keepdims=True)
            dws_ref[g] += dw_acc
            dbs_ref[g] += jnp.broadcast_to(db_acc, (CA, GA))

        dvln = dvln_s[...]
        dlg_ref[...] += jnp.sum(dvln * vhat, axis=0, keepdims=True)
        dlb_ref[...] += jnp.sum(dvln, axis=0, keepdims=True)
        dvhat = dvln * lg_ref[...]
        dgv = rs * (dvhat - jnp.mean(dvhat, axis=-1, keepdims=True)
                    - vhat * jnp.mean(dvhat * vhat, axis=-1, keepdims=True))
        dp_ref[:, D:2 * D] = (dgv * gv_grad).astype(bf16)

        @pl.when(i == nsteps - 1)
        def _():
            for g in range(HA):
                dws_ref[g] = jnp.where(mask, dws_ref[g], 0.0)
            for cp in _halves_copies(srcs, dsts, kinds, sems) if n else []:
                cp.wait()

    return body


def _tri(n, upper):
    r = lax.broadcasted_iota(jnp.int32, (n, n), 0)
    c = lax.broadcasted_iota(jnp.int32, (n, n), 1)
    return jnp.where((c >= r) if upper else (c <= r), 1.0, 0.0).astype(f32)


def _chunk_tri(n, upper):
    r = lax.broadcasted_iota(jnp.int32, (n, n), 0)
    c = lax.broadcasted_iota(jnp.int32, (n, n), 1)
    shift = CB.bit_length() - 1
    same_chunk = jnp.right_shift(r, shift) == jnp.right_shift(c, shift)
    return jnp.where(same_chunk & ((c >= r) if upper else (c <= r)), 1.0, 0.0).astype(f32)


def _log_alpha(lr, wg, bg):
    logit = _dot(lr.astype(bf16), wg.astype(bf16)) + bg
    la = (jnp.minimum(logit, 0.0) - jnp.log1p(jnp.exp(-jnp.abs(logit)))) * (1.0 / GATE_NORM)
    return logit, la


def _gla_fwd_tile(q_ref, k_ref, v_ref, zb_ref, lr_ref, wg_ref, bg_ref, gg_ref, o_ref, ob_ref, st_ref, state, la_s):
    tm = q_ref.shape[0]
    _, la = _log_alpha(lr_ref[...], wg_ref[...], bg_ref[...])
    la_s[...] = _dot_exact(_chunk_tri(tm, upper=False), la)
    causal = _tri(CB, upper=False) > 0.5
    states = [state[hd] for hd in range(HB)]
    for c in range(tm // CB):
        rows = slice(c * CB, (c + 1) * CB)
        b = la_s[rows, :]
        bl = b[CB - 1:CB, :]
        bm = b[CB // 2 - 1:CB // 2, :]
        q = q_ref[rows, :] * QSCALE
        k = k_ref[rows, :]
        qi_all = (q * jnp.exp(b - bm)).astype(bf16)
        ki_all = (k * jnp.exp(bm - b)).astype(bf16)
        qe_all = (q * jnp.exp(b)).astype(bf16)
        ks_all = (k * jnp.exp(bl - b)).astype(bf16)
        e_l = jnp.exp(bl)
        for hd in range(HB):
            kc = slice(hd * DK, (hd + 1) * DK)
            vc = slice(hd * DV, (hd + 1) * DV)
            v = v_ref[rows, vc].astype(bf16)
            p = jnp.where(causal, _dot_nt(qi_all[:, kc], ki_all[:, kc]), 0.0).astype(bf16)
            s0 = states[hd]
            st_ref[c, hd] = s0
            o = _dot(p, v) + _dot_nt(qe_all[:, kc], s0.astype(bf16))
            states[hd] = s0 * e_l[:, kc] + _dot_tn(v, ks_all[:, kc])
            o_ref[rows, vc] = o
            ro = lax.rsqrt(jnp.mean(o * o, axis=-1, keepdims=True) + EPS)
            zb = zb_ref[rows, vc]
            ob_ref[rows, vc] = (o * ro * gg_ref[...] * (zb * _sigmoid(zb))).astype(bf16)
    for hd in range(HB):
        state[hd] = states[hd]


def _gla_bwd_body(nb, tm, n, sends_what):
    cpb = tm // CB

    def body(q_ref, k_ref, v_ref, zb_ref, lr_ref, o_ref, st_ref, dob_ref, dp_in, wg_ref, bg_ref, gg_ref, *rest):
        srcs, (dp_ref, dlr_ref, dwg_ref, dbg_ref, dgg_ref), dsts = rest[:n], rest[n:n + 5], rest[n + 5:2 * n + 5]
        dstate, la_s, dlogit_s, tail_s, *sems = rest[2 * n + 5:]
        del dp_in
        step = pl.program_id(0)

        @pl.when(step == 0)
        def _():
            dstate[...] = jnp.zeros_like(dstate)
            dwg_ref[...] = jnp.zeros_like(dwg_ref)
            dbg_ref[...] = jnp.zeros_like(dbg_ref)
            dgg_ref[...] = jnp.zeros_like(dgg_ref)
            for cp in _exchange_copies(srcs, dsts, sends_what, sems) if n else []:
                cp.start()

        lr_v = lr_ref[...]
        logit, la = _log_alpha(lr_v, wg_ref[...], bg_ref[...])
        la_s[...] = _dot_exact(_chunk_tri(tm, upper=False), la)
        causal = _tri(CB, upper=False) > 0.5
        gg = gg_ref[...]
        dgg_acc = jnp.zeros((1, DV), f32)
        dstates = [dstate[hd] for hd in range(HB)]
        for c in reversed(range(cpb)):
            rows = slice(c * CB, (c + 1) * CB)
            b = la_s[rows, :]
            bl = b[CB - 1:CB, :]
            bm = b[CB // 2 - 1:CB // 2, :]
            eb_all, eqm_all, ekm_all = jnp.exp(b), jnp.exp(b - bm), jnp.exp(bm - b)
            eks_all, el_all = jnp.exp(bl - b), jnp.exp(bl)
            q_all = q_ref[rows, :] * QSCALE
            k_all = k_ref[rows, :]
            qi_all = (q_all * eqm_all).astype(bf16)
            ki_all = (k_all * ekm_all).astype(bf16)
            qe_all = (q_all * eb_all).astype(bf16)
            ksf_all = k_all * eks_all
            ks_all = ksf_all.astype(bf16)
            for hd in range(HB):
                kc = slice(hd * DK, (hd + 1) * DK)
                vc = slice(hd * DV, (hd + 1) * DV)
                o_h = o_ref[rows, vc]
                ro = lax.rsqrt(jnp.mean(o_h * o_h, axis=-1, keepdims=True) + EPS)
                ohat = o_h * ro
                zb = zb_ref[rows, vc]
                sg = _sigmoid(zb)
                dob_h = dob_ref[rows, vc]
                don = dob_h * (zb * sg)
                dp_ref[rows, 2 * D + hd * DV:2 * D + (hd + 1) * DV] = (
                    dob_h * ohat * gg * (sg * (1.0 + zb * (1.0 - sg)))).astype(bf16)
                dgg_acc = dgg_acc + jnp.sum(don * ohat, axis=0, keepdims=True)
                dohat = don * gg
                do = (ro * (dohat - ohat * jnp.mean(dohat * ohat, axis=-1, keepdims=True))).astype(bf16)
                e_b, e_qm, e_km, e_ks, e_l = eb_all[:, kc], eqm_all[:, kc], ekm_all[:, kc], eks_all[:, kc], el_all[:, kc]
                q, k, ks_f = q_all[:, kc], k_all[:, kc], ksf_all[:, kc]
                qi, ki, qe, ks = qi_all[:, kc], ki_all[:, kc], qe_all[:, kc], ks_all[:, kc]
                v = v_ref[rows, vc].astype(bf16)
                p = jnp.where(causal, _dot_nt(qi, ki), 0.0).astype(bf16)
                s0 = st_ref[c, hd]
                ds = dstates[hd]
                ds_b = ds.astype(bf16)
                dv = _dot_tn(p, do) + _dot_nt(ks, ds_b)
                dpm = jnp.where(causal, _dot_nt(do, v), 0.0).astype(bf16)
                dqi = _dot(dpm, ki)
                dki = _dot_tn(dpm, qi)
                dqe = _dot(do, s0.astype(bf16))
                dks = _dot(v, ds_b)
                dq_s = dqi * e_qm + dqe * e_b
                dk = dki * e_km + dks * e_ks
                tail = (jnp.sum(dks * ks_f, axis=0, keepdims=True)
                        + e_l * jnp.sum(ds * s0, axis=0, keepdims=True))
                dstates[hd] = _dot_tn(do, qe) + ds * e_l
                dp_ref[rows, kc] = (dq_s * QSCALE).astype(bf16)
                dp_ref[rows, KEYB + hd * DK:KEYB + (hd + 1) * DK] = dk.astype(bf16)
                dp_ref[rows, D + hd * DV:D + (hd + 1) * DV] = dv.astype(bf16)
                dlogit_s[rows, kc] = dq_s * q - dk * k
                tail_s[rows, kc] = jnp.broadcast_to(tail, (CB, DK))
        for hd in range(HB):
            dstate[hd] = dstates[hd]
        dgg_ref[...] += dgg_acc
        dg = _dot_exact(_chunk_tri(tm, upper=True), dlogit_s[...]) + tail_s[...]
        dlogit = dg * (1.0 / GATE_NORM) * _sigmoid(-logit)
        dbg_ref[...] += jnp.sum(dlogit, axis=0, keepdims=True)
        dlb = dlogit.astype(bf16)
        dlr_ref[...] = _dot_nt(dlb, wg_ref[...].astype(bf16)).astype(bf16)
        dwg_ref[...] += _dot_tn(lr_v.astype(bf16), dlb)

        if n:
            @pl.when(step == nb - 1)
            def _():
                for cp in _exchange_copies(srcs, dsts, sends_what, sems):
                    cp.wait()

    return body


def _mixers_bwd(proj, lr, o, states, da, dob, dproj, ln_g, ln_b, w_s, b_sb, w_gate, b_gate, gla_g, riders=(), kinds=(),
                tm=256):
    T = proj.shape[0]
    nb = T // tm
    n = len(riders)
    body_a = _mixer_a_bwd_body(nb, tm, n, kinds)
    body_b = _gla_bwd_body(nb, tm, 0, ())

    def body(u_ref, v_ref, za_ref, da_ref, q_ref, k_ref, vb_ref, zb_ref, lr_ref, o_ref, st_ref, dob_ref, dp_in,
             lg_ref, lb_ref, ws_ref, bs_ref, wg_ref, bg_ref, gg_ref, *rest):
        srcs, outs, dsts = rest[:n], rest[n:n + 9], rest[n + 9:2 * n + 9]
        dp_ref, dws_ref, dbs_ref, dlg_ref, dlb_ref, dlr_ref, dwg_ref, dbg_ref, dgg_ref = outs
        vln_s, dvln_s, dstate, la_s, dlogit_s, tail_s, *sems = rest[2 * n + 9:]
        body_a(u_ref, v_ref, za_ref, da_ref, dp_in, lg_ref, lb_ref, ws_ref, bs_ref, *srcs,
               dp_ref.at[:, :3 * D], dws_ref, dbs_ref, dlg_ref, dlb_ref, *dsts, vln_s, dvln_s, *sems)
        body_b(q_ref, k_ref, vb_ref, zb_ref, lr_ref, o_ref, st_ref, dob_ref, dp_in, wg_ref, bg_ref, gg_ref,
               dp_ref.at[:, 3 * D:], dlr_ref, dwg_ref, dbg_ref, dgg_ref, dstate, la_s, dlogit_s, tail_s)

    def rev(width, cidx=0):
        return pl.BlockSpec((tm, width), lambda i, c=cidx: (nb - 1 - i, c))

    return pl.pallas_call(
        body,
        grid=(nb,),
        in_specs=[rev(D, COL_U), rev(D, COL_V), rev(D, COL_ZA), rev(D),
                  rev(KEYB, COL_Q), rev(KEYB, COL_K), rev(D, COL_VB), rev(D, COL_ZB), rev(LRP), rev(D),
                  pl.BlockSpec((tm // CB, HB, DV, DK), lambda i: (nb - 1 - i, 0, 0, 0)), rev(D), _ANY,
                  _resident((1, D)), _resident((1, D)), _resident((HA, CA, CA)), _resident((HA, CA, GA)),
                  _resident((LRP, KEYB)), _resident((1, KEYB)), _resident((1, DV))] + [_ANY] * n,
        out_specs=[rev(6 * D), _resident((HA, CA, CA)), _resident((HA, CA, GA)), _resident((1, D)), _resident((1, D)),
                   rev(LRP), _resident((LRP, KEYB)), _resident((1, KEYB)), _resident((1, DV))] + [_ANY] * n,
        out_shape=[jax.ShapeDtypeStruct(dproj.shape, dproj.dtype),
                   jax.ShapeDtypeStruct((HA, CA, CA), f32), jax.ShapeDtypeStruct((HA, CA, GA), f32),
                   jax.ShapeDtypeStruct((1, D), f32), jax.ShapeDtypeStruct((1, D), f32),
                   jax.ShapeDtypeStruct((T, LRP), bf16),
                   jax.ShapeDtypeStruct((LRP, KEYB), f32), jax.ShapeDtypeStruct((1, KEYB), f32),
                   jax.ShapeDtypeStruct((1, DV), f32)] + _halves_shapes(riders, kinds),
        scratch_shapes=[pltpu.VMEM((tm, D), bf16), pltpu.VMEM((tm, D), f32), pltpu.VMEM((HB, DV, DK), f32)]
        + [pltpu.VMEM((tm, KEYB), f32)] * 3 + (_halves_sems(riders) if n else []),
        input_output_aliases={12: 0},
        compiler_params=_params(("arbitrary",)),
        name="mixers_bwd",
    )(proj, proj, proj, da, proj, proj, proj, proj, lr, o, states, dob, dproj,
      ln_g, ln_b, w_s, b_sb, w_gate, b_gate, gla_g, *riders)


def _merge_fwd_bwd(x, tgt, proj, a, ob, w_a, w_b, w_o, g_f, tm=256):
    T = x.shape[0]

    def body(x_ref, t_ref, gt_ref, a_ref, ob_ref, wa_ref, wb_ref, wo_ref, gf_ref,
             dp_ref, dy_ref, da_ref, dob_ref, dwa_ref, dwb_ref, dwo_ref, dgf_ref, loss_ref):
        @pl.when(pl.program_id(0) == 0)
        def _():
            dwa_ref[...] = jnp.zeros_like(dwa_ref)
            dwb_ref[...] = jnp.zeros_like(dwb_ref)
            dwo_ref[...] = jnp.zeros_like(dwo_ref)
            dgf_ref[...] = jnp.zeros_like(dgf_ref)
            loss_ref[...] = jnp.zeros_like(loss_ref)

        ga = _sigmoid(gt_ref[:, :D])
        gb = _sigmoid(gt_ref[:, D:])
        a_v = a_ref[...]
        ob_v = ob_ref[...]
        pa = _dot(a_v, wa_ref[...])
        pb = _dot(ob_v, wb_ref[...])
        mb = (ga * pa + gb * pb).astype(bf16)
        y = x_ref[...] + _dot(mb, wo_ref[...])
        r1 = lax.rsqrt(jnp.mean(y * y, axis=-1, keepdims=True) + EPS)
        yhat = y * r1
        gf = gf_ref[...]
        err = yhat * gf - t_ref[...]
        loss_ref[...] += jnp.sum(err * err, axis=0, keepdims=True) * (0.5 / D)
        dout = err * (1.0 / D)
        dgf_ref[...] += jnp.sum(dout * yhat, axis=0, keepdims=True)
        dyn = dout * gf
        dy = r1 * (dyn - yhat * jnp.mean(dyn * yhat, axis=-1, keepdims=True))
        dy_ref[...] = dy
        dyb = dy.astype(bf16)
        dwo_ref[...] += _dot_tn(mb, dyb)
        dm = _dot_nt(dyb, wo_ref[...])
        dpa = (dm * ga).astype(bf16)
        dpb = (dm * gb).astype(bf16)
        dp_ref[:, :D] = (dm * pa * ga * (1.0 - ga)).astype(bf16)
        dp_ref[:, D:] = (dm * pb * gb * (1.0 - gb)).astype(bf16)
        dwa_ref[...] += _dot_tn(a_v, dpa)
        dwb_ref[...] += _dot_tn(ob_v, dpb)
        da_ref[...] = _dot_nt(dpa, wa_ref[...])
        dob_ref[...] = _dot_nt(dpb, wb_ref[...])

    row = lambda: pl.BlockSpec((tm, D), lambda i: (i, 0))
    return pl.pallas_call(
        body,
        grid=(T // tm,),
        in_specs=[row(), row(), pl.BlockSpec((tm, 2 * D), lambda i: (i, COL_GATES)), row(), row(),
                  _resident((D, D)), _resident((D, D)), _resident((D, D)), _resident((1, D))],
        out_specs=[pl.BlockSpec((tm, 2 * D), lambda i: (i, COL_GATES)), row(), row(), row(),
                   _resident((D, D)), _resident((D, D)), _resident((D, D)), _resident((1, D)), _resident((1, D))],
        out_shape=[jax.ShapeDtypeStruct((T, NMAIN), bf16),
                   jax.ShapeDtypeStruct((T, D), f32), jax.ShapeDtypeStruct((T, D), f32),
                   jax.ShapeDtypeStruct((T, D), f32),
                   jax.ShapeDtypeStruct((D, D), f32), jax.ShapeDtypeStruct((D, D), f32),
                   jax.ShapeDtypeStruct((D, D), f32),
                   jax.ShapeDtypeStruct((1, D), f32), jax.ShapeDtypeStruct((1, D), f32)],
        compiler_params=_params(("arbitrary",)),
        name="merge_fwd_bwd",
    )(x, tgt, proj, a, ob, w_a, w_b, w_o, g_f)


def _dx_bwd(x, dy, dproj, dlr, g0, w_t, riders=(), sends_what=(), tm=256):
    T = x.shape[0]
    nsteps = T // tm
    n = len(riders)

    def body(x_ref, dy_ref, dp_ref, dl_ref, g_ref, w_ref, *rest):
        srcs, (dx_ref, dg_ref), dsts, sems = rest[:n], rest[n:n + 2], rest[n + 2:2 * n + 2], rest[2 * n + 2:]

        @pl.when(pl.program_id(0) == 0)
        def _():
            dg_ref[...] = jnp.zeros_like(dg_ref)
            for cp in _exchange_copies(srcs, dsts, sends_what, sems) if n else []:
                cp.start()

        xv = x_ref[...]
        r = lax.rsqrt(jnp.mean(xv * xv, axis=-1, keepdims=True) + EPS)
        xhat = xv * r
        dh = (_dot(dp_ref[:, :LR_COL], w_ref[:LR_COL, :]) + _dot(dp_ref[:, LR_COL:], w_ref[LR_COL + RANK:, :])
              + _dot(dl_ref[...], w_ref[LR_ROWS, :]))
        dg_ref[...] += jnp.sum(dh * xhat, axis=0, keepdims=True)
        t = dh * g_ref[...]
        dx_ref[...] = dy_ref[...] + r * (t - xhat * jnp.mean(t * xhat, axis=-1, keepdims=True))

        if n:
            @pl.when(pl.program_id(0) == nsteps - 1)
            def _():
                for cp in _exchange_copies(srcs, dsts, sends_what, sems):
                    cp.wait()

    row = lambda: pl.BlockSpec((tm, D), lambda i: (i, 0))
    return pl.pallas_call(
        body,
        grid=(nsteps,),
        in_specs=[row(), row(), pl.BlockSpec((tm, NMAIN), lambda i: (i, 0)),
                  pl.BlockSpec((tm, LRP), lambda i: (i, 0)),
                  _resident((1, D)), _resident((NMAIN + RANK, D))] + [_ANY] * n,
        out_specs=[row(), _resident((1, D))] + [_ANY] * n,
        out_shape=[jax.ShapeDtypeStruct((T, D), f32), jax.ShapeDtypeStruct((1, D), f32)]
        + _exchange_shapes(riders, sends_what),
        scratch_shapes=_exchange_sems(n) if n else [],
        compiler_params=_params(("arbitrary",)),
        name="dx_bwd",
    )(x, dy, dproj, dlr, g0, w_t, *riders)


def _dw_in(h, dproj, dlr, pair_dtype=None, riders=(), sends_what=(), half_riders=(), kinds=(), tm=1024, tn=1024):
    T = h.shape[0]
    tm = min(tm, T)
    nj, nk = NMAIN // tn, T // tm
    lr_tile = LR_COL // tn
    pair = pair_dtype is not None
    hd = D // 2
    n, m = len(riders), len(half_riders)

    def body(h_ref, dp_ref, dl_ref, *rest):
        srcs, half_srcs, out_ref = rest[:n], rest[n:n + m], rest[n + m]
        dsts, half_dsts = rest[n + m + 1:2 * n + m + 1], rest[2 * n + m + 1:2 * (n + m) + 1]
        acc, lr_acc, sems, lr_sem, *more = rest[2 * (n + m) + 1:]
        j, k = pl.program_id(0), pl.program_id(1)
        slot = j % 2

        def tile_row(jj):
            return pl.multiple_of(jj * tn + jnp.where(jj >= lr_tile, RANK, 0), 8)

        def riding():
            cps = _exchange_copies(srcs, dsts, sends_what, rider_sems[:2]) if n else []
            return cps + (_halves_copies(half_srcs, half_dsts, kinds, rider_sems[2 if n else 0:]) if m else [])

        if pair:
            land, lr_land, part_buf, lr_part, swap_send, swap_recv, *rider_sems = more
            c = lax.axis_index("c")
            mine = pl.ds(pl.multiple_of(c * hd, 128), hd)
            other = pl.ds(pl.multiple_of((1 - c) * hd, 128), hd)

            def tile_swap(jj, s):
                return _remote(acc.at[s, :, other], land.at[jj], swap_send.at[jj], swap_recv.at[jj], _sibling())

            def lr_swap():
                return _remote(lr_acc.at[pl.ds(0, RANK), other], lr_land, swap_send.at[nj], swap_recv.at[nj], _sibling())

            def tile_out(jj, s):
                return pltpu.make_async_copy(part_buf.at[s], out_ref.at[pl.ds(tile_row(jj), tn)], sems.at[s])

            def finish_tile(jj, s):
                tile_swap(jj, s).wait()
                part_buf[s] = (acc[s, :, mine] + land[jj]).astype(pair_dtype)
                tile_out(jj, s).start()

            lr_out = pltpu.make_async_copy(lr_part, out_ref.at[pl.ds(LR_COL, RANK)], lr_sem)
        else:
            def tile_out(jj, s):
                return pltpu.make_async_copy(acc.at[s], out_ref.at[pl.ds(tile_row(jj), tn)], sems.at[s])

            lr_out = pltpu.make_async_copy(lr_acc.at[pl.ds(0, RANK)], out_ref.at[pl.ds(LR_COL, RANK)], lr_sem)

        @pl.when(j == 0)
        def _():
            @pl.when(k == 0)
            def _():
                lr_acc[...] = jnp.zeros_like(lr_acc)
                for cp in riding() if pair else []:
                    cp.start()

            lr_acc[...] += _dot_tn(dl_ref[...], h_ref[...])

            @pl.when(k == nk - 1)
            def _():
                if pair:
                    lr_swap().start()
                else:
                    lr_out.start()

        @pl.when(k == 0)
        def _():
            acc[slot] = jnp.zeros((tn, D), f32)

        acc[slot] += _dot_tn(dp_ref[...], h_ref[...])

        @pl.when(k == nk - 1)
        def _():
            if pair:
                tile_swap(j, slot).start()

                @pl.when(j >= 3)
                def _():
                    tile_out(j - 3, 1 - slot).wait()

                @pl.when(j >= 1)
                def _():
                    finish_tile(j - 1, 1 - slot)

                @pl.when(j == nj - 1)
                def _():
                    tile_out(j - 2, slot).wait()
                    finish_tile(j, slot)
                    lr_swap().wait()
                    lr_part[...] = (lr_acc[0:RANK, mine] + lr_land[...]).astype(pair_dtype)
                    lr_out.start()
                    tile_out(j - 1, 1 - slot).wait()
                    tile_out(j, slot).wait()
                    lr_out.wait()
                    for cp in riding():
                        cp.wait()
            else:
                tile_out(j, slot).start()

                @pl.when(j > 0)
                def _():
                    tile_out(j - 1, 1 - slot).wait()

                @pl.when(j == nj - 1)
                def _():
                    tile_out(j, slot).wait()
                    lr_out.wait()

    pair_scratch = [pltpu.VMEM((nj, tn, hd), f32), pltpu.VMEM((RANK, hd), f32), pltpu.VMEM((2, tn, hd), pair_dtype),
                    pltpu.VMEM((RANK, hd), pair_dtype)] + [pltpu.SemaphoreType.DMA((nj + 1,))] * 2 if pair else []
    res = pl.pallas_call(
        body,
        grid=(nj, nk),
        in_specs=[pl.BlockSpec((tm, D), lambda j, k: (k, 0)), pl.BlockSpec((tm, tn), lambda j, k: (k, j)),
                  pl.BlockSpec((tm, LRP), lambda j, k: (k, 0))] + [_ANY] * (n + m),
        out_specs=[_ANY] * (1 + n + m),
        out_shape=[jax.ShapeDtypeStruct((NMAIN + RANK, hd), pair_dtype) if pair
                   else jax.ShapeDtypeStruct((NMAIN + RANK, D), f32)]
        + _exchange_shapes(riders, sends_what) + _halves_shapes(half_riders, kinds),
        scratch_shapes=[pltpu.VMEM((2, tn, D), f32), pltpu.VMEM((LRP, D), f32),
                        pltpu.SemaphoreType.DMA((2,)), pltpu.SemaphoreType.DMA] + pair_scratch
        + (_exchange_sems(n) if n else []) + (_halves_sems(half_riders) if m else []),
        compiler_params=_params(("arbitrary", "arbitrary")),
        name="dw_in",
    )(h, dproj, dlr, *riders, *half_riders)
    return res if n + m else res[0]


MESH = pl.DeviceIdType.MESH


def _place():
    x, y, c = lax.axis_index("x"), lax.axis_index("y"), lax.axis_index("c")
    others = [(1 - x, y), (x, 1 - y), (1 - x, 1 - y)]
    return x, y, c, 2 * x + y, others


def _sibling():
    return lax.axis_index("x"), lax.axis_index("y"), 1 - lax.axis_index("c")


def _remote(src, dst, send_sem, recv_sem, to):
    return pltpu.make_async_remote_copy(src_ref=src, dst_ref=dst, send_sem=send_sem, recv_sem=recv_sem,
                                        device_id=to, device_id_type=MESH)


def _half(ref, e, by_columns):
    if not by_columns:
        return ref.at[e]
    hw = ref.shape[-1] // 2
    return ref.at[:, pl.ds(pl.multiple_of(e * hw, 128), hw)]


RELAY_ROWS = 1024


def _gather_win(shard, small):
    h, w = shard.shape
    part_a, part_b = pl.ds(0, RELAY_ROWS), pl.ds(RELAY_ROWS, h - RELAY_ROWS)

    def body(src, small_src, dst, small_dst, send_sems, recv_sems, relay_send, relay_recv, pass_send, pass_recv, own_sems,
             small_send, small_recv, small_own, stage, small_stage):
        x, y, c, me, others = _place()
        (to_x, to_y, _), sibling = others, (x, y, 1 - c)
        j_x, j_y, j_d = (2 * cx + cy for cx, cy in others)
        cols = pl.ds(pl.multiple_of(c * (w // 2), 128), w // 2)
        theirs = pl.ds(pl.multiple_of((1 - c) * (w // 2), 128), w // 2)

        def mine(j, rows=pl.ds(0, h)):
            return dst.at[j, rows, cols]

        small_sends = [_remote(small_src, small_dst.at[me], small_send.at[k], small_recv.at[k], (*to, c))
                       for k, to in enumerate(others)]
        for cp in small_sends:
            cp.start()
        small_in = pltpu.make_async_copy(small_src, small_stage, small_own.at[0])
        small_out = pltpu.make_async_copy(small_stage, small_dst.at[me], small_own.at[1])
        small_in.start()

        to_stage = pltpu.make_async_copy(src, stage, own_sems.at[0])
        to_slot = pltpu.make_async_copy(stage, dst.at[me], own_sems.at[1])
        sends = [_remote(src.at[:, cols], mine(me), send_sems.at[k], recv_sems.at[k], (*to, c))
                 for k, to in enumerate((to_x, to_y))]
        for cp in sends:
            cp.start()
        to_stage.start()
        relays = [_remote(mine(j_x, part_a), mine(j_x, part_a), relay_send.at[0], relay_recv.at[0], (*to_y, c)),
                  _remote(mine(j_y, part_b), mine(j_y, part_b), relay_send.at[1], relay_recv.at[1], (*to_x, c))]
        landed = [mine(j_x), mine(j_y), mine(j_d, part_a), mine(j_d, part_b)]
        passes = [_remote(place, place, pass_send.at[k], pass_recv.at[k], sibling) for k, place in enumerate(landed)]
        for k in range(2):
            _remote(src.at[:, cols], landed[k], send_sems.at[k], recv_sems.at[k], sibling).wait_recv()
            relays[k].start()
            passes[k].start()
        to_stage.wait()
        to_slot.start()
        for k in range(2):
            _remote(landed[2 + k], landed[2 + k], relay_send.at[k], relay_recv.at[k], sibling).wait_recv()
            passes[2 + k].start()
        for k, place in enumerate([(j_x, pl.ds(0, h)), (j_y, pl.ds(0, h)), (j_d, part_a), (j_d, part_b)]):
            got = dst.at[place[0], place[1], theirs]
            _remote(got, got, pass_send.at[k], pass_recv.at[k], sibling).wait_recv()
        for cp in sends + relays + passes:
            cp.wait_send()
        to_slot.wait()
        small_in.wait()
        small_out.start()
        for k, (cx, cy) in enumerate(others):
            _remote(small_src, small_dst.at[2 * cx + cy], small_send.at[k], small_recv.at[k], sibling).wait_recv()
        for cp in small_sends:
            cp.wait_send()
        small_out.wait()

    return pl.pallas_call(
        body,
        in_specs=[_ANY] * 2,
        out_specs=[_ANY] * 2,
        out_shape=[jax.ShapeDtypeStruct((NCHIP, h, w), shard.dtype),
                   jax.ShapeDtypeStruct((NCHIP,) + small.shape, small.dtype)],
        scratch_shapes=[pltpu.SemaphoreType.DMA((2,))] * 4 + [pltpu.SemaphoreType.DMA((4,))] * 2
        + [pltpu.SemaphoreType.DMA((2,))] + [pltpu.SemaphoreType.DMA((3,))] * 2 + [pltpu.SemaphoreType.DMA((2,))]
        + [pltpu.VMEM((h, w), shard.dtype), pltpu.VMEM(small.shape, small.dtype)],
        compiler_params=pltpu.CompilerParams(vmem_limit_bytes=VMEM_LIMIT),
        name="gather_win",
    )(shard, small)


def _gathered_shapes(shards):
    return [jax.ShapeDtypeStruct((NCHIP,) + s.shape, s.dtype) for s in shards]


def _gather_scratch(shards):
    n = len(shards)
    return ([pltpu.SemaphoreType.DMA((3, n))] * 4 + [pltpu.SemaphoreType.DMA((2, n))]
            + [pltpu.VMEM(s.shape, s.dtype) for s in shards])


def _own_to_stage(srcs, scratch):
    own_sems, stages = scratch[4], scratch[5:]
    return [pltpu.make_async_copy(srcs[a], stages[a], own_sems.at[0, a]) for a in range(len(srcs))]


def _own_to_slot(dsts, scratch):
    own_sems, stages = scratch[4], scratch[5:]
    me = _place()[3]
    return [pltpu.make_async_copy(stages[a], dsts[a].at[me], own_sems.at[1, a]) for a in range(len(dsts))]


def _gather_copies(srcs, dsts, by_columns, sems, sends_only):
    n = len(srcs)
    send_sems, recv_sems, pass_send, pass_recv = sems[:4]
    x, y, c, me, others = _place()
    sibling = (x, y, 1 - c)

    def src(a, e):
        return _half(srcs[a], e, by_columns[a])

    def dst(a, j, e):
        return _half(dsts[a].at[j], e, by_columns[a])

    sends, arrivals, passes, passed = [], [], [], []
    for k, (cx, cy) in enumerate(others):
        j = 2 * cx + cy
        for a in range(n):
            sends.append(_remote(src(a, c), dst(a, me, c), send_sems.at[k, a], recv_sems.at[k, a], (cx, cy, c)))
            if not sends_only:
                arrivals.append(_remote(src(a, c), dst(a, j, c), send_sems.at[k, a], recv_sems.at[k, a], (cx, cy, c)))
                passes.append(_remote(dst(a, j, c), dst(a, j, c), pass_send.at[k, a], pass_recv.at[k, a], sibling))
                passed.append(_remote(src(a, c), dst(a, j, 1 - c), pass_send.at[k, a], pass_recv.at[k, a], sibling))
    return sends, arrivals, passes, passed


def _gather_start(srcs, dsts, by_columns, scratch):
    for cp in _gather_copies(srcs, dsts, by_columns, scratch, sends_only=True)[0]:
        cp.start()
    for cp in _own_to_stage(srcs, scratch):
        cp.start()


def _gather_finish(srcs, dsts, by_columns, scratch):
    for cp in _own_to_stage(srcs, scratch):
        cp.wait()
    own = _own_to_slot(dsts, scratch)
    for cp in own:
        cp.start()
    sends, arrivals, passes, passed = _gather_copies(srcs, dsts, by_columns, scratch, sends_only=False)
    for arrival, cp in zip(arrivals, passes):
        arrival.wait_recv()
        cp.start()
    for arrival in passed:
        arrival.wait_recv()
    for cp in sends + passes:
        cp.wait_send()
    for cp in own:
        cp.wait()


HALF_FIRST, CHIP_FIRST, BY_COLUMNS = "half_first", "chip_first", "by_columns"


def _sibling_halves(bufs, kinds):
    n = len(bufs)

    def body(*refs):
        cps = _halves_copies(refs[:n], refs[n:2 * n], kinds, refs[2 * n:])
        for cp in cps:
            cp.start()
        for cp in cps:
            cp.wait()

    return pl.pallas_call(
        body,
        in_specs=[_ANY] * n,
        out_specs=[_ANY] * n,
        out_shape=_halves_shapes(bufs, kinds),
        scratch_shapes=_halves_sems(bufs),
        name="sibling_halves",
    )(*bufs)


def _halves_shapes(bufs, kinds):
    def landed(b, kind):
        if kind == HALF_FIRST:
            return b.shape[1:]
        if kind == CHIP_FIRST:
            return (b.shape[0],) + b.shape[2:]
        return b.shape[:2] + (b.shape[2] // 2,)

    return [jax.ShapeDtypeStruct(landed(b, kind), b.dtype) for b, kind in zip(bufs, kinds)]


def _halves_sems(bufs):
    return [pltpu.SemaphoreType.DMA((len(bufs), NCHIP))] * 2


def _halves_copies(srcs, dsts, kinds, sems):
    send_sems, recv_sems = sems
    x, y, c, _, _ = _place()
    cps = []
    for a, kind in enumerate(kinds):
        if kind == HALF_FIRST:
            cps.append(_remote(srcs[a].at[1 - c], dsts[a], send_sems.at[a, 0], recv_sems.at[a, 0], (x, y, 1 - c)))
        else:
            cps += [_remote(_half(srcs[a].at[j], 1 - c, kind == BY_COLUMNS), dsts[a].at[j],
                            send_sems.at[a, j], recv_sems.at[a, j], (x, y, 1 - c)) for j in range(srcs[a].shape[0])]
    return cps


TO_ITS_CHIP, TO_EVERY_CHIP, ROWS_TO_ITS_CHIP = "to_its_chip", "to_every_chip", "rows_to_its_chip"
PIECE_STEP = 2048
PIECE_ROWS = 2064


def _chip_exchange(parts, sends_what):
    n = len(parts)

    def body(*refs):
        cps = _exchange_copies(refs[:n], refs[n:2 * n], sends_what, refs[2 * n:])
        for cp in cps:
            cp.start()
        for cp in cps:
            cp.wait()

    return pl.pallas_call(
        body,
        in_specs=[_ANY] * n,
        out_specs=[_ANY] * n,
        out_shape=_exchange_shapes(parts, sends_what),
        scratch_shapes=_exchange_sems(n),
        name="chip_exchange",
    )(*parts)


def _exchange_shapes(parts, sends_what):
    def landed(p, what):
        return (3, PIECE_ROWS, p.shape[1]) if what == ROWS_TO_ITS_CHIP else (3,) + p.shape[1:]

    return [jax.ShapeDtypeStruct(landed(p, what), p.dtype) for p, what in zip(parts, sends_what)]


def _exchange_sems(n):
    return [pltpu.SemaphoreType.DMA((3, n))] * 2


def _exchange_copies(srcs, dsts, sends_what, sems):
    send_sems, recv_sems = sems
    x, y, c, me, others = _place()

    def part(a, j):
        if sends_what[a] == ROWS_TO_ITS_CHIP:
            return srcs[a].at[pl.ds(pl.multiple_of(j * PIECE_STEP, PIECE_STEP), PIECE_ROWS)]
        return srcs[a].at[j if sends_what[a] == TO_ITS_CHIP else 0]

    return [_remote(part(a, 2 * cx + cy), dsts[a].at[k], send_sems.at[k, a], recv_sems.at[k, a], (cx, cy, c))
            for k, (cx, cy) in enumerate(others) for a in range(len(srcs))]


def _sibling_swap(halves):
    n = len(halves)

    def body(*refs):
        srcs, dsts = refs[:n], refs[n:2 * n]
        send_sems, recv_sems = refs[2 * n:]
        x, y, c, _, _ = _place()
        cps = [_remote(srcs[a], dsts[a], send_sems.at[a], recv_sems.at[a], (x, y, 1 - c)) for a in range(n)]
        for cp in cps:
            cp.start()
        for cp in cps:
            cp.wait()

    return pl.pallas_call(
        body,
        in_specs=[_ANY] * n,
        out_specs=[_ANY] * n,
        out_shape=[jax.ShapeDtypeStruct(s.shape, s.dtype) for s in halves],
        scratch_shapes=[pltpu.SemaphoreType.DMA((n,))] * 2,
        name="sibling_swap",
    )(*halves)


def _tile(h, w, operands=5):
    if h % 128 == 0:
        return 128, w
    budget = VMEM_LIMIT * 3 // 4 // (2 * operands * 4)
    tw = w
    while h * tw > budget and tw % 256 == 0:
        tw //= 2
    return h, tw


def _pair_sum(place, bufs, kind, gots, out_dtype):
    m = len(bufs)
    nj, h, w = gots[0].shape
    th, tw = _tile(h, w, operands=3 * m)
    nq = w // tw

    def body(p_ref, *refs):
        del p_ref
        for a_ref, b_ref, o_ref in zip(refs[:m], refs[m:2 * m], refs[2 * m:]):
            o_ref[...] = (a_ref[...] + b_ref[...]).astype(out_dtype)

    if kind == HALF_FIRST:
        mine = pl.BlockSpec((None, None, th, tw), lambda j, r, q, p: (p[0], j, r, q))
    elif kind == CHIP_FIRST:
        mine = pl.BlockSpec((None, None, th, tw), lambda j, r, q, p: (j, p[0], r, q))
    else:
        mine = pl.BlockSpec((None, th, tw), lambda j, r, q, p: (j, r, p[0] * nq + q))
    landed = pl.BlockSpec((None, th, tw), lambda j, r, q, p: (j, r, q))
    return pl.pallas_call(
        body,
        grid_spec=pltpu.PrefetchScalarGridSpec(
            num_scalar_prefetch=1,
            grid=(nj, h // th, w // tw),
            in_specs=[mine] * m + [landed] * m,
            out_specs=[landed] * m,
        ),
        out_shape=[jax.ShapeDtypeStruct((nj, h, w), out_dtype)] * m,
        compiler_params=_params(("parallel", "parallel", "parallel")),
        name="pair_sum",
    )(place, *bufs, *gots)


def _chip_sum(place, parts, slots):
    m = len(parts)
    nj, h, w = parts[0].shape
    th, tw = _tile(h, w, operands=5 * m)

    def body(p_ref, *refs):
        me = p_ref[1]
        for own_ref, s_ref, o_ref in zip(refs[:m], refs[m:2 * m], refs[2 * m:]):
            own = own_ref[...].astype(f32)
            by_flip = {2: s_ref[0].astype(f32), 1: s_ref[1].astype(f32), 3: s_ref[2].astype(f32)}
            acc = None
            for j in range(NCHIP):
                flip = me ^ j
                term = jnp.where(flip == 0, own,
                                 jnp.where(flip == 2, by_flip[2], jnp.where(flip == 1, by_flip[1], by_flip[3])))
                acc = term if acc is None else acc + term
            o_ref[...] = acc

    return pl.pallas_call(
        body,
        grid_spec=pltpu.PrefetchScalarGridSpec(
            num_scalar_prefetch=1,
            grid=(h // th, w // tw),
            in_specs=[pl.BlockSpec((None, th, tw), lambda r, q, p: (p[1] if nj == NCHIP else 0, r, q))] * m
            + [pl.BlockSpec((3, th, tw), lambda r, q, p: (0, r, q))] * m,
            out_specs=[pl.BlockSpec((th, tw), lambda r, q, p: (r, q))] * m,
        ),
        out_shape=[jax.ShapeDtypeStruct((h, w), f32)] * m,
        compiler_params=_params(("parallel", "parallel")),
        name="chip_sum",
    )(place, *parts, *slots)


def _adamw_math(w, g, m, v):
    nm = ADAM_B1 * m + (1.0 - ADAM_B1) * g
    nv = ADAM_B2 * v + (1.0 - ADAM_B2) * (g * g)
    m_hat = nm / (1.0 - ADAM_B1 ** ADAM_STEP)
    v_hat = nv / (1.0 - ADAM_B2 ** ADAM_STEP)
    return -ADAM_LR * (m_hat / (jnp.sqrt(v_hat) + ADAM_EPS) + ADAM_WD * w), nm, nv


def _adamw(w, g, m, v):
    rows, width = w.shape
    th, tw = _tile(rows, width, operands=7)

    def body(w_ref, g_ref, m_ref, v_ref, d_ref, nm_ref, nv_ref):
        d_ref[...], nm_ref[...], nv_ref[...] = _adamw_math(w_ref[...], g_ref[...], m_ref[...], v_ref[...])

    spec = pl.BlockSpec((th, tw), lambda r, q: (r, q))
    return pl.pallas_call(
        body,
        grid=(rows // th, width // tw),
        in_specs=[spec] * 4,
        out_specs=[spec] * 3,
        out_shape=[jax.ShapeDtypeStruct((rows, width), f32)] * 3,
        compiler_params=_params(("parallel", "parallel")),
        name="adamw",
    )(w, g, m, v)


def _adamw_halves(place, ws, mines, gots, ms, vs, axis):
    k = len(ws)
    rows, width = ws[0].shape
    h, hw = mines[0].shape
    th, tw = _tile(h, hw, operands=10 * k)
    nr, nq = h // th, hw // tw

    def body(p_ref, *refs):
        ins, outs = refs[:5 * k], refs[5 * k:]
        for i in range(k):
            w_ref, a_ref, b_ref, m_ref, v_ref = ins[i::k]
            g_ref, d_ref, nm_ref, nv_ref = outs[i::k]
            g = jnp.where(pl.program_id(0) == p_ref[0], a_ref[...], b_ref[...])
            g_ref[...] = g
            d_ref[...], nm_ref[...], nv_ref[...] = _adamw_math(w_ref[...], g, m_ref[...], v_ref[...])

    if axis == 0:
        full = pl.BlockSpec((th, tw), lambda e, r, q, p: (e * nr + r, q))
    else:
        full = pl.BlockSpec((th, tw), lambda e, r, q, p: (r, e * nq + q))
    half = pl.BlockSpec((th, tw), lambda e, r, q, p: (r, q))
    res = pl.pallas_call(
        body,
        grid_spec=pltpu.PrefetchScalarGridSpec(
            num_scalar_prefetch=1,
            grid=(2, nr, nq),
            in_specs=[full] * k + [half] * (2 * k) + [full] * (2 * k),
            out_specs=[full] * (4 * k),
        ),
        out_shape=[jax.ShapeDtypeStruct((rows, width), f32)] * (4 * k),
        compiler_params=_params(("parallel", "parallel", "parallel")),
        name="adamw_halves",
    )(place, *ws, *mines, *gots, *ms, *vs)
    return [res[i::k] for i in range(k)]


_SMALL = (("norm_g", 8), ("ln_v_g", 8), ("ln_v_b", 8), ("b_spatial", 8), ("b_gate_up", 4),
          ("gla_norm_g", 2), ("final_norm_g", 8), ("w_gate_up", 64), ("loss", 8))
_SMALL_ROWS = 128


def _pack_rows(arrays, rows):
    flat = jnp.concatenate([a.reshape(-1, 128) for a in arrays], axis=0)
    return jnp.pad(flat, ((0, rows - flat.shape[0]), (0, 0)))


def kernel(x, norm_g, w_in, ln_v_g, ln_v_b, w_spatial, b_spatial, w_gate_up, b_gate_up, gla_norm_g, w_branch_a, w_branch_b, w_out, final_norm_g, loss_target, m_norm_g, m_w_in, m_ln_v_g, m_ln_v_b, m_w_spatial, m_b_spatial, m_w_gate_up, m_b_gate_up, m_gla_norm_g, m_w_branch_a, m_w_branch_b, m_w_out, m_final_norm_g, v_norm_g, v_w_in, v_ln_v_g, v_ln_v_b, v_w_spatial, v_b_spatial, v_w_gate_up, v_b_gate_up, v_gla_norm_g, v_w_branch_a, v_w_branch_b, v_w_out, v_final_norm_g):
    chip = 2 * lax.axis_index("x") + lax.axis_index("y")
    core = lax.axis_index("c")
    place = jnp.stack([core, chip]).astype(jnp.int32)
    mat_names = ("w_branch_a", "w_branch_b", "w_out")

    wt_shard = jnp.transpose(w_in[0]).astype(bf16)
    mats = [w[0].astype(bf16).reshape(2, D // NCHIP // 2, D) for w in (w_branch_a, w_branch_b, w_out)]
    gate_sh = w_gate_up[0].reshape(2, RANK // 2, 128)
    g_win, g_gate = _gather_win(wt_shard, gate_sh)
    w_t = g_win.reshape(NCHIP * WIN_SHARD, D)
    w_gate = jnp.transpose(g_gate.reshape(NCHIP, RANK, 128), (1, 0, 2)).reshape(RANK, KEYB)
    w_gate = jnp.pad(w_gate, ((0, LRP - RANK), (0, 0)))
    b_sb = jnp.broadcast_to(b_spatial[0][:, :, None], (HA, CA, GA))
    xs, tgt = x[0], loss_target[0]

    proj, lr, h, a, o, ob, states, g_a, g_b, g_o = _forward(xs, norm_g, w_t, ln_v_g, ln_v_b, w_spatial[0], b_sb, w_gate,
                                                           b_gate_up, gla_norm_g, riders=mats)
    w_a, w_b, w_o = (g.reshape(D, D) for g in (g_a, g_b, g_o))
    dproj, dy, da, dob, dwa, dwb, dwo, dgf, loss_cols = _merge_fwd_bwd(xs, tgt, proj, a, ob, w_a, w_b, w_o,
                                                                       final_norm_g.reshape(1, D))
    b_mats = [t.reshape(NCHIP, 2, D // NCHIP // 2, D) for t in (dwa, dwb, dwo)]
    dproj, dws, dbs, dlg, dlb, dlr, dwg, dbg, dgg, *got_mats = _mixers_bwd(
        proj, lr, o, states, da, dob, dproj, ln_v_g, ln_v_b, w_spatial[0], b_sb, w_gate, b_gate_up, gla_norm_g,
        riders=b_mats, kinds=[CHIP_FIRST] * 3)
    part_mats = _pair_sum(place, b_mats, CHIP_FIRST, got_mats, bf16)
    b_ws = dws.reshape(2, 1, HA * CA // 2, GA)
    part_win, *landed = _dw_in(h, dproj, dlr, pair_dtype=bf16, riders=part_mats, sends_what=[TO_ITS_CHIP] * 3,
                               half_riders=[b_ws], kinds=[HALF_FIRST])
    slots_mats, got_ws = landed[:3], landed[3]
    (part_ws,) = _pair_sum(place, [b_ws], HALF_FIRST, [got_ws], bf16)
    dx, dg0, slots_win, slots_ws = _dx_bwd(xs, dy, dproj, dlr, norm_g, w_t, riders=[part_win, part_ws],
                                           sends_what=[ROWS_TO_ITS_CHIP, TO_EVERY_CHIP])
    small = _pack_rows([dg0, dlg, dlb, dbs[:, :, 0], dbg, dgg, dgf, dwg[:RANK], loss_cols], _SMALL_ROWS)
    b_small = small.reshape(2, 1, _SMALL_ROWS // 2, 128)
    (got_small,) = _sibling_halves([b_small], [HALF_FIRST])
    (part_small,) = _pair_sum(place, [b_small], HALF_FIRST, [got_small], f32)
    (slots_small,) = _chip_exchange([part_small], [TO_EVERY_CHIP])
    own_win = lax.dynamic_slice_in_dim(part_win, chip * PIECE_STEP, PIECE_ROWS, axis=0)[None]
    mine = [*_chip_sum(place, [own_win], [slots_win]), *_chip_sum(place, part_mats, slots_mats),
            *_chip_sum(place, [part_small], [slots_small]), *_chip_sum(place, [part_ws], [slots_ws])]
    theirs = list(_sibling_swap(mine))
    mine[0], theirs[0] = (lax.dynamic_slice_in_dim(t, (WIN_SHARD - PIECE_STEP) * chip, WIN_SHARD, axis=0)
                          for t in (mine[0], theirs[0]))

    def whole(i):
        return jnp.where(core == 0, jnp.concatenate([mine[i], theirs[i]], axis=0),
                         jnp.concatenate([theirs[i], mine[i]], axis=0))

    g_small = whole(4)
    grads = {"w_spatial": whole(5)}
    row = 0
    for name, rows in _SMALL:
        grads[name] = g_small[row:row + rows]
        row += rows
    loss = jnp.sum(grads["loss"])
    dwg_full = grads["w_gate_up"].reshape(RANK, KEYB)
    grads["w_gate_up"] = lax.dynamic_slice_in_dim(dwg_full, chip * 128, 128, axis=1)

    weights = dict(norm_g=norm_g, w_in=w_in, ln_v_g=ln_v_g, ln_v_b=ln_v_b, w_spatial=w_spatial, b_spatial=b_spatial,
                   w_gate_up=w_gate_up, b_gate_up=b_gate_up, gla_norm_g=gla_norm_g, w_branch_a=w_branch_a,
                   w_branch_b=w_branch_b, w_out=w_out, final_norm_g=final_norm_g)
    m_in = dict(norm_g=m_norm_g, w_in=m_w_in, ln_v_g=m_ln_v_g, ln_v_b=m_ln_v_b, w_spatial=m_w_spatial,
                b_spatial=m_b_spatial, w_gate_up=m_w_gate_up, b_gate_up=m_b_gate_up, gla_norm_g=m_gla_norm_g,
                w_branch_a=m_w_branch_a, w_branch_b=m_w_branch_b, w_out=m_w_out, final_norm_g=m_final_norm_g)
    v_in = dict(norm_g=v_norm_g, w_in=v_w_in, ln_v_g=v_ln_v_g, ln_v_b=v_ln_v_b, w_spatial=v_w_spatial,
                b_spatial=v_b_spatial, w_gate_up=v_w_gate_up, b_gate_up=v_b_gate_up, gla_norm_g=v_gla_norm_g,
                w_branch_a=v_w_branch_a, w_branch_b=v_w_branch_b, w_out=v_w_out, final_norm_g=v_final_norm_g)
    names = list(weights)
    small_names = [n for n in names if n != "w_in" and n not in mat_names]
    out_g, out_d, out_m, out_v = {}, {}, {}, {}
    (res,) = _adamw_halves(place, [jnp.transpose(w_in[0])], mine[:1], theirs[:1], [jnp.transpose(m_w_in[0])],
                           [jnp.transpose(v_w_in[0])], axis=1)
    out_g["w_in"], out_d["w_in"], out_m["w_in"], out_v["w_in"] = (jnp.transpose(t)[None] for t in res)
    res_mats = _adamw_halves(place, [weights[n][0] for n in mat_names], mine[1:4], theirs[1:4],
                             [m_in[n][0] for n in mat_names], [v_in[n][0] for n in mat_names], axis=0)
    for n, res in zip(mat_names, res_mats):
        out_g[n], out_d[n], out_m[n], out_v[n] = (t[None] for t in res)
    upd_rows = sum(weights[n].size for n in small_names) // 128
    pad_rows = -(-upd_rows // 8) * 8
    packed = [_pack_rows([t[n] for n in small_names], pad_rows) for t in (weights, grads, m_in, v_in)]
    d_s, m_s, v_s = _adamw(*packed)
    row = 0
    for n in small_names:
        shape = weights[n].shape
        rows = weights[n].size // 128
        out_g[n] = grads[n].reshape(shape)
        out_d[n], out_m[n], out_v[n] = (t[row:row + rows].reshape(shape) for t in (d_s, m_s, v_s))
        row += rows
    return (loss, dx[None], *[out_g[n] for n in names], *[out_d[n] for n in names],
            *[out_m[n] for n in names], *[out_v[n] for n in names])
```

```python
import functools
import math

import jax
import jax.numpy as jnp
from jax import lax
from jax.experimental import pallas as pl
from jax.experimental.pallas import tpu as pltpu

f32 = jnp.float32
bf16 = jnp.bfloat16

D = 1024
NMAIN = 8192
LRP = 128
RANK = 16
HA, GA, CA = 8, 128, 128
HB, DK, DV, CB = 4, 128, 256, 64
KEYB = HB * DK
EPS = 1e-6
LN_EPS = 1e-5
GATE_NORM = 16.0
QSCALE = DK ** -0.5
COL_U, COL_V, COL_ZA = 0, 1, 2
COL_Q, COL_K = 6, 7
COL_VB, COL_ZB = 4, 5
COL_GATES = 3
VMEM_LIMIT = 56 * 1024 * 1024
NCHIP = 4
WIN_SHARD = 2052
LR_COL = 6144
_ANY = pl.BlockSpec(memory_space=pl.ANY)

ADAM_LR, ADAM_B1, ADAM_B2, ADAM_EPS, ADAM_WD, ADAM_STEP = 0.001, 0.9, 0.999, 1e-08, 0.01, 10

_SQRT_HALF = 0.7071067811865476
_INV_SQRT_2PI = 0.3989422804014327


def _dot(a, b):
    return jnp.dot(a, b, preferred_element_type=f32)


def _dot_nt(a, b):
    return lax.dot_general(a, b, (((1,), (1,)), ((), ())), preferred_element_type=f32)


def _dot_tn(a, b):
    return lax.dot_general(a, b, (((0,), (0,)), ((), ())), preferred_element_type=f32)


def _dot_exact(a, b):
    return jnp.dot(a, b, preferred_element_type=f32, precision=lax.Precision.HIGHEST)


def _gelu(x):
    return 0.5 * x * (1.0 + lax.erf(x * _SQRT_HALF))


def _gelu_and_grad(x):
    cdf = 0.5 * (1.0 + lax.erf(x * _SQRT_HALF))
    return x * cdf, cdf + x * (jnp.exp(-0.5 * x * x) * _INV_SQRT_2PI)


def _sigmoid(x):
    return 0.5 * jnp.tanh(0.5 * x) + 0.5


def _params(sem):
    return pltpu.CompilerParams(dimension_semantics=sem, vmem_limit_bytes=VMEM_LIMIT)


def _resident(shape):
    nd = len(shape)
    return pl.BlockSpec(shape, lambda *_: (0,) * nd, pipeline_mode=pl.Buffered(1))


def _w_rows(c, tn):
    start = c * tn + (RANK if c * tn >= LR_COL else 0)
    return slice(start, start + tn)


LR_ROWS = slice(LR_COL, LR_COL + LRP)


def _forward(x, g0, w_t, ln_g, ln_b, w_s, b_sb, w_gate, b_gate, gla_g, riders=(), tm=256, tn=1024):
    T = x.shape[0]
    nsteps = T // tm
    n = len(riders)
    n_out = 7

    def body(x_ref, g_ref, w_ref, lg_ref, lb_ref, ws_ref, bs_ref, wg_ref, bg_ref, gg_ref, *rest):
        srcs, outs, dsts = rest[:n], rest[n:n + n_out], rest[n + n_out:2 * n + n_out]
        proj_ref, lr_ref, h_ref, a_ref, o_ref, ob_ref, st_ref = outs
        vln_s, state, la_s, *sems = rest[2 * n + n_out:]

        @pl.when(pl.program_id(0) == 0)
        def _():
            state[...] = jnp.zeros_like(state)
            if n:
                _gather_start(srcs, dsts, [False] * n, sems)

        xv = x_ref[...]
        r = lax.rsqrt(jnp.mean(xv * xv, axis=-1, keepdims=True) + EPS)
        h = (xv * r * g_ref[...]).astype(bf16)
        h_ref[...] = h
        def project(chunks):
            for c in chunks:
                proj_ref[:, c * tn:(c + 1) * tn] = _dot_nt(h, w_ref[_w_rows(c, tn), :])

        def cols(block, width):
            return proj_ref.at[:, block * width:(block + 1) * width]

        project(range(0, 3 * D // tn))
        _mixer_a_fwd_tile(cols(COL_U, D), cols(COL_V, D), cols(COL_ZA, D), lg_ref, lb_ref, ws_ref, bs_ref, a_ref, vln_s)
        lr_ref[...] = _dot_nt(h, w_ref[LR_ROWS, :])
        project(range(3 * D // tn, 6 * D // tn))
        _gla_fwd_tile(cols(COL_Q, KEYB), cols(COL_K, KEYB), cols(COL_VB, D), cols(COL_ZB, D), lr_ref, wg_ref, bg_ref,
                      gg_ref, o_ref, ob_ref, st_ref, state, la_s)
        project(range(6 * D // tn, NMAIN // tn))

        if n:
            @pl.when(pl.program_id(0) == nsteps - 1)
            def _():
                _gather_finish(srcs, dsts, [False] * n, sems)

    row = lambda width: pl.BlockSpec((tm, width), lambda i: (i, 0))
    return pl.pallas_call(
        body,
        grid=(nsteps,),
        in_specs=[row(D), _resident((1, D)), _resident((NMAIN + RANK, D)), _resident((1, D)), _resident((1, D)),
                  _resident((HA, CA, CA)), _resident((HA, CA, GA)),
                  _resident((LRP, KEYB)), _resident((1, KEYB)), _resident((1, DV))] + [_ANY] * n,
        out_specs=[row(NMAIN), row(LRP), row(D), row(D), row(D), row(D),
                   pl.BlockSpec((tm // CB, HB, DV, DK), lambda i: (i, 0, 0, 0))] + [_ANY] * n,
        out_shape=[jax.ShapeDtypeStruct((T, NMAIN), f32), jax.ShapeDtypeStruct((T, LRP), f32),
                   jax.ShapeDtypeStruct((T, D), bf16), jax.ShapeDtypeStruct((T, D), bf16),
                   jax.ShapeDtypeStruct((T, D), f32), jax.ShapeDtypeStruct((T, D), bf16),
                   jax.ShapeDtypeStruct((T // CB, HB, DV, DK), f32)] + _gathered_shapes(riders),
        scratch_shapes=[pltpu.VMEM((tm, D), bf16), pltpu.VMEM((HB, DV, DK), f32), pltpu.VMEM((tm, KEYB), f32)]
        + (_gather_scratch(riders) if n else []),
        compiler_params=_params(("arbitrary",)),
        name="forward",
    )(x, g0, w_t, ln_g, ln_b, w_s, b_sb, w_gate, b_gate, gla_g, *riders)


def _causal_mask():
    t = lax.broadcasted_iota(jnp.int32, (CA, CA), 0)
    s = lax.broadcasted_iota(jnp.int32, (CA, CA), 1)
    return s <= t


def _layernorm_parts(gv):
    mu = jnp.mean(gv, axis=-1, keepdims=True)
    xc = gv - mu
    rs = lax.rsqrt(jnp.mean(xc * xc, axis=-1, keepdims=True) + LN_EPS)
    return xc * rs, rs


def _mixer_a_fwd_tile(u_ref, v_ref, za_ref, lg_ref, lb_ref, ws_ref, bs_ref, a_ref, vln_s):
    tm = u_ref.shape[0]
    vhat, _ = _layernorm_parts(_gelu(v_ref[...]))
    vln_s[...] = (vhat * lg_ref[...] + lb_ref[...]).astype(bf16)
    mask = _causal_mask()
    for g in range(HA):
        wg = jnp.where(mask, ws_ref[g], 0.0).astype(bf16)
        cols = slice(g * GA, (g + 1) * GA)
        for c in range(tm // CA):
            rows = slice(c * CA, (c + 1) * CA)
            mixed = _dot(wg, vln_s[rows, cols]) + bs_ref[g]
            za = za_ref[rows, cols]
            a = _gelu(u_ref[rows, cols]) * mixed * (za * _sigmoid(za))
            a_ref[rows, cols] = a.astype(bf16)


def _mixer_a_bwd(proj, da, dproj, ln_g, ln_b, w_s, b_sb, riders=(), kinds=(), tm=256):
    T = proj.shape[0]
    nsteps = T // tm
    n = len(riders)

    def body(u_ref, v_ref, za_ref, da_ref, dp_in, lg_ref, lb_ref, ws_ref, bs_ref, *rest):
        srcs, (dp_ref, dws_ref, dbs_ref, dlg_ref, dlb_ref), dsts = rest[:n], rest[n:n + 5], rest[n + 5:2 * n + 5]
        vln_s, dvln_s, *sems = rest[2 * n + 5:]
        del dp_in
        i = pl.program_id(0)

        @pl.when(i == 0)
        def _():
            dws_ref[...] = jnp.zeros_like(dws_ref)
            dbs_ref[...] = jnp.zeros_like(dbs_ref)
            dlg_ref[...] = jnp.zeros_like(dlg_ref)
            dlb_ref[...] = jnp.zeros_like(dlb_ref)
            for cp in _halves_copies(srcs, dsts, kinds, sems) if n else []:
                cp.start()

        gv, gv_grad = _gelu_and_grad(v_ref[...])
        vhat, rs = _layernorm_parts(gv)
        vln_s[...] = (vhat * lg_ref[...] + lb_ref[...]).astype(bf16)
        mask = _causal_mask()
        for g in range(HA):
            wg = jnp.where(mask, ws_ref[g], 0.0).astype(bf16)
            cols = slice(g * GA, (g + 1) * GA)
            dw_acc = jnp.zeros((CA, CA), f32)
            db_acc = jnp.zeros((CA, 1), f32)
            for c in range(tm // CA):
                rows = slice(c * CA, (c + 1) * CA)
                vln = vln_s[rows, cols]
                mixed = _dot(wg, vln) + bs_ref[g]
                u = u_ref[rows, cols]
                za = za_ref[rows, cols]
                da_blk = da_ref[rows, cols]
                sg = _sigmoid(za)
                sz = za * sg
                gu, gu_grad = _gelu_and_grad(u)
                dp_ref[rows, cols] = (da_blk * mixed * sz * gu_grad).astype(bf16)
                dp_ref[rows, 2 * D + g * GA:2 * D + (g + 1) * GA] = (
                    da_blk * gu * mixed * (sg * (1.0 + za * (1.0 - sg)))).astype(bf16)
                dmixed = da_blk * gu * sz
                dmb = dmixed.astype(bf16)
                dvln_s[rows, cols] = _dot_tn(wg, dmb)
                dw_acc = dw_acc + _dot_nt(dmb, vln)
                db_acc = db_acc + jnp.sum(dmixed, axis=-1, keepdims=True)
            dws_ref[g] += dw_acc
            dbs_ref[g] += jnp.broadcast_to(db_acc, (CA, GA))

        dvln = dvln_s[...]
        dlg_ref[...] += jnp.sum(dvln * vhat, axis=0, keepdims=True)
        dlb_ref[...] += jnp.sum(dvln, axis=0, keepdims=True)
        dvhat = dvln * lg_ref[...]
        dgv = rs * (dvhat - jnp.mean(dvhat, axis=-1, keepdims=True)
                    - vhat * jnp.mean(dvhat * vhat, axis=-1, keepdims=True))
        dp_ref[:, D:2 * D] = (dgv * gv_grad).astype(bf16)

        @pl.when(i == nsteps - 1)
        def _():
            for g in range(HA):
                dws_ref[g] = jnp.where(mask, dws_ref[g], 0.0)
            for cp in _halves_copies(srcs, dsts, kinds, sems) if n else []:
                cp.wait()

    def col(cidx):
        return pl.BlockSpec((tm, D), lambda i, c=cidx: (i, c))

    return pl.pallas_call(
        body,
        grid=(nsteps,),
        in_specs=[col(COL_U), col(COL_V), col(COL_ZA), pl.BlockSpec((tm, D), lambda i: (i, 0)),
                  pl.BlockSpec(memory_space=pl.ANY),
                  _resident((1, D)), _resident((1, D)), _resident((HA, CA, CA)), _resident((HA, CA, GA))] + [_ANY] * n,
        out_specs=[pl.BlockSpec((tm, 3 * D), lambda i: (i, 0)),
                   _resident((HA, CA, CA)), _resident((HA, CA, GA)), _resident((1, D)), _resident((1, D))] + [_ANY] * n,
        out_shape=[jax.ShapeDtypeStruct(dproj.shape, dproj.dtype),
                   jax.ShapeDtypeStruct((HA, CA, CA), f32), jax.ShapeDtypeStruct((HA, CA, GA), f32),
                   jax.ShapeDtypeStruct((1, D), f32), jax.ShapeDtypeStruct((1, D), f32)] + _halves_shapes(riders, kinds),
        scratch_shapes=[pltpu.VMEM((tm, D), bf16), pltpu.VMEM((tm, D), f32)] + (_halves_sems(riders) if n else []),
        input_output_aliases={4: 0},
        compiler_params=_params(("arbitrary",)),
        name="mixer_a_bwd",
    )(proj, proj, proj, da, dproj, ln_g, ln_b, w_s, b_sb, *riders)


def _tri(n, upper):
    r = lax.broadcasted_iota(jnp.int32, (n, n), 0)
    c = lax.broadcasted_iota(jnp.int32, (n, n), 1)
    return jnp.where((c >= r) if upper else (c <= r), 1.0, 0.0).astype(f32)


def _chunk_tri(n, upper):
    r = lax.broadcasted_iota(jnp.int32, (n, n), 0)
    c = lax.broadcasted_iota(jnp.int32, (n, n), 1)
    shift = CB.bit_length() - 1
    same_chunk = jnp.right_shift(r, shift) == jnp.right_shift(c, shift)
    return jnp.where(same_chunk & ((c >= r) if upper else (c <= r)), 1.0, 0.0).astype(f32)


def _log_alpha(lr, wg, bg):
    logit = _dot(lr.astype(bf16), wg.astype(bf16)) + bg
    la = (jnp.minimum(logit, 0.0) - jnp.log1p(jnp.exp(-jnp.abs(logit)))) * (1.0 / GATE_NORM)
    return logit, la


def _gla_fwd_tile(q_ref, k_ref, v_ref, zb_ref, lr_ref, wg_ref, bg_ref, gg_ref, o_ref, ob_ref, st_ref, state, la_s):
    tm = q_ref.shape[0]
    _, la = _log_alpha(lr_ref[...], wg_ref[...], bg_ref[...])
    la_s[...] = _dot_exact(_chunk_tri(tm, upper=False), la)
    causal = _tri(CB, upper=False) > 0.5
    states = [state[hd] for hd in range(HB)]
    for c in range(tm // CB):
        rows = slice(c * CB, (c + 1) * CB)
        b = la_s[rows, :]
        bl = b[CB - 1:CB, :]
        bm = b[CB // 2 - 1:CB // 2, :]
        q = q_ref[rows, :] * QSCALE
        k = k_ref[rows, :]
        qi_all = (q * jnp.exp(b - bm)).astype(bf16)
        ki_all = (k * jnp.exp(bm - b)).astype(bf16)
        qe_all = (q * jnp.exp(b)).astype(bf16)
        ks_all = (k * jnp.exp(bl - b)).astype(bf16)
        e_l = jnp.exp(bl)
        for hd in range(HB):
            kc = slice(hd * DK, (hd + 1) * DK)
            vc = slice(hd * DV, (hd + 1) * DV)
            v = v_ref[rows, vc].astype(bf16)
            p = jnp.where(causal, _dot_nt(qi_all[:, kc], ki_all[:, kc]), 0.0).astype(bf16)
            s0 = states[hd]
            st_ref[c, hd] = s0
            o = _dot(p, v) + _dot_nt(qe_all[:, kc], s0.astype(bf16))
            states[hd] = s0 * e_l[:, kc] + _dot_tn(v, ks_all[:, kc])
            o_ref[rows, vc] = o
            ro = lax.rsqrt(jnp.mean(o * o, axis=-1, keepdims=True) + EPS)
            zb = zb_ref[rows, vc]
            ob_ref[rows, vc] = (o * ro * gg_ref[...] * (zb * _sigmoid(zb))).astype(bf16)
    for hd in range(HB):
        state[hd] = states[hd]


def _gla_bwd(proj, lr, o, states, dob, dproj, w_gate, b_gate, gla_g, riders=(), sends_what=(), tm=256):
    T = proj.shape[0]
    cpb = tm // CB
    nb = T // tm
    n = len(riders)

    def body(q_ref, k_ref, v_ref, zb_ref, lr_ref, o_ref, st_ref, dob_ref, dp_in, wg_ref, bg_ref, gg_ref, *rest):
        srcs, (dp_ref, dlr_ref, dwg_ref, dbg_ref, dgg_ref), dsts = rest[:n], rest[n:n + 5], rest[n + 5:2 * n + 5]
        dstate, la_s, dlogit_s, tail_s, *sems = rest[2 * n + 5:]
        del dp_in
        step = pl.program_id(0)

        @pl.when(step == 0)
        def _():
            dstate[...] = jnp.zeros_like(dstate)
            dwg_ref[...] = jnp.zeros_like(dwg_ref)
            dbg_ref[...] = jnp.zeros_like(dbg_ref)
            dgg_ref[...] = jnp.zeros_like(dgg_ref)
            for cp in _exchange_copies(srcs, dsts, sends_what, sems) if n else []:
                cp.start()

        lr_v = lr_ref[...]
        logit, la = _log_alpha(lr_v, wg_ref[...], bg_ref[...])
        la_s[...] = _dot_exact(_chunk_tri(tm, upper=False), la)
        causal = _tri(CB, upper=False) > 0.5
        gg = gg_ref[...]
        dgg_acc = jnp.zeros((1, DV), f32)
        dstates = [dstate[hd] for hd in range(HB)]
        for c in reversed(range(cpb)):
            rows = slice(c * CB, (c + 1) * CB)
            b = la_s[rows, :]
            bl = b[CB - 1:CB, :]
            bm = b[CB // 2 - 1:CB // 2, :]
            eb_all, eqm_all, ekm_all = jnp.exp(b), jnp.exp(b - bm), jnp.exp(bm - b)
            eks_all, el_all = jnp.exp(bl - b), jnp.exp(bl)
            q_all = q_ref[rows, :] * QSCALE
            k_all = k_ref[rows, :]
            qi_all = (q_all * eqm_all).astype(bf16)
            ki_all = (k_all * ekm_all).astype(bf16)
            qe_all = (q_all * eb_all).astype(bf16)
            ksf_all = k_all * eks_all
            ks_all = ksf_all.astype(bf16)
            for hd in range(HB):
                kc = slice(hd * DK, (hd + 1) * DK)
                vc = slice(hd * DV, (hd + 1) * DV)
                o_h = o_ref[rows, vc]
                ro = lax.rsqrt(jnp.mean(o_h * o_h, axis=-1, keepdims=True) + EPS)
                ohat = o_h * ro
                zb = zb_ref[rows, vc]
                sg = _sigmoid(zb)
                dob_h = dob_ref[rows, vc]
                don = dob_h * (zb * sg)
                dp_ref[rows, 2 * D + hd * DV:2 * D + (hd + 1) * DV] = (
                    dob_h * ohat * gg * (sg * (1.0 + zb * (1.0 - sg)))).astype(bf16)
                dgg_acc = dgg_acc + jnp.sum(don * ohat, axis=0, keepdims=True)
                dohat = don * gg
                do = (ro * (dohat - ohat * jnp.mean(dohat * ohat, axis=-1, keepdims=True))).astype(bf16)
                e_b, e_qm, e_km, e_ks, e_l = eb_all[:, kc], eqm_all[:, kc], ekm_all[:, kc], eks_all[:, kc], el_all[:, kc]
                q, k, ks_f = q_all[:, kc], k_all[:, kc], ksf_all[:, kc]
                qi, ki, qe, ks = qi_all[:, kc], ki_all[:, kc], qe_all[:, kc], ks_all[:, kc]
                v = v_ref[rows, vc].astype(bf16)
                p = jnp.where(causal, _dot_nt(qi, ki), 0.0).astype(bf16)
                s0 = st_ref[c, hd]
                ds = dstates[hd]
                ds_b = ds.astype(bf16)
                dv = _dot_tn(p, do) + _dot_nt(ks, ds_b)
                dpm = jnp.where(causal, _dot_nt(do, v), 0.0).astype(bf16)
                dqi = _dot(dpm, ki)
                dki = _dot_tn(dpm, qi)
                dqe = _dot(do, s0.astype(bf16))
                dks = _dot(v, ds_b)
                dq_s = dqi * e_qm + dqe * e_b
                dk = dki * e_km + dks * e_ks
                tail = (jnp.sum(dks * ks_f, axis=0, keepdims=True)
                        + e_l * jnp.sum(ds * s0, axis=0, keepdims=True))
                dstates[hd] = _dot_tn(do, qe) + ds * e_l
                dp_ref[rows, kc] = (dq_s * QSCALE).astype(bf16)
                dp_ref[rows, KEYB + hd * DK:KEYB + (hd + 1) * DK] = dk.astype(bf16)
                dp_ref[rows, D + hd * DV:D + (hd + 1) * DV] = dv.astype(bf16)
                dlogit_s[rows, kc] = dq_s * q - dk * k
                tail_s[rows, kc] = jnp.broadcast_to(tail, (CB, DK))
        for hd in range(HB):
            dstate[hd] = dstates[hd]
        dgg_ref[...] += dgg_acc
        dg = _dot_exact(_chunk_tri(tm, upper=True), dlogit_s[...]) + tail_s[...]
        dlogit = dg * (1.0 / GATE_NORM) * _sigmoid(-logit)
        dbg_ref[...] += jnp.sum(dlogit, axis=0, keepdims=True)
        dlb = dlogit.astype(bf16)
        dlr_ref[...] = _dot_nt(dlb, wg_ref[...].astype(bf16)).astype(bf16)
        dwg_ref[...] += _dot_tn(lr_v.astype(bf16), dlb)

        if n:
            @pl.when(step == nb - 1)
            def _():
                for cp in _exchange_copies(srcs, dsts, sends_what, sems):
                    cp.wait()

    def rev(cidx):
        return lambda i, c=cidx: (nb - 1 - i, c)

    return pl.pallas_call(
        body,
        grid=(nb,),
        in_specs=[pl.BlockSpec((tm, KEYB), rev(COL_Q)),
                  pl.BlockSpec((tm, KEYB), rev(COL_K)),
                  pl.BlockSpec((tm, D), rev(COL_VB)),
                  pl.BlockSpec((tm, D), rev(COL_ZB)),
                  pl.BlockSpec((tm, LRP), rev(0)),
                  pl.BlockSpec((tm, D), rev(0)),
                  pl.BlockSpec((cpb, HB, DV, DK), lambda i: (nb - 1 - i, 0, 0, 0)),
                  pl.BlockSpec((tm, D), rev(0)),
                  pl.BlockSpec(memory_space=pl.ANY),
                  _resident((LRP, KEYB)), _resident((1, KEYB)), _resident((1, DV))] + [_ANY] * n,
        out_specs=[pl.BlockSpec((tm, 3 * D), rev(1)),
                   pl.BlockSpec((tm, LRP), rev(0)),
                   _resident((LRP, KEYB)), _resident((1, KEYB)), _resident((1, DV))] + [_ANY] * n,
        out_shape=[jax.ShapeDtypeStruct(dproj.shape, dproj.dtype),
                   jax.ShapeDtypeStruct((T, LRP), bf16),
                   jax.ShapeDtypeStruct((LRP, KEYB), f32), jax.ShapeDtypeStruct((1, KEYB), f32),
                   jax.ShapeDtypeStruct((1, DV), f32)] + _exchange_shapes(riders, sends_what),
        scratch_shapes=[pltpu.VMEM((HB, DV, DK), f32)] + [pltpu.VMEM((tm, KEYB), f32)] * 3
        + (_exchange_sems(n) if n else []),
        input_output_aliases={8: 0},
        compiler_params=_params(("arbitrary",)),
        name="gla_bwd",
    )(proj, proj, proj, proj, lr, o, states, dob, dproj, w_gate, b_gate, gla_g, *riders)


def _merge_fwd_bwd(x, tgt, proj, a, ob, w_a, w_b, w_o, g_f, tm=256):
    T = x.shape[0]

    def body(x_ref, t_ref, gt_ref, a_ref, ob_ref, wa_ref, wb_ref, wo_ref, gf_ref,
             dp_ref, dy_ref, da_ref, dob_ref, dwa_ref, dwb_ref, dwo_ref, dgf_ref, loss_ref):
        @pl.when(pl.program_id(0) == 0)
        def _():
            dwa_ref[...] = jnp.zeros_like(dwa_ref)
            dwb_ref[...] = jnp.zeros_like(dwb_ref)
            dwo_ref[...] = jnp.zeros_like(dwo_ref)
            dgf_ref[...] = jnp.zeros_like(dgf_ref)
            loss_ref[...] = jnp.zeros_like(loss_ref)

        ga = _sigmoid(gt_ref[:, :D])
        gb = _sigmoid(gt_ref[:, D:])
        a_v = a_ref[...]
        ob_v = ob_ref[...]
        pa = _dot(a_v, wa_ref[...])
        pb = _dot(ob_v, wb_ref[...])
        mb = (ga * pa + gb * pb).astype(bf16)
        y = x_ref[...] + _dot(mb, wo_ref[...])
        r1 = lax.rsqrt(jnp.mean(y * y, axis=-1, keepdims=True) + EPS)
        yhat = y * r1
        gf = gf_ref[...]
        err = yhat * gf - t_ref[...]
        loss_ref[...] += jnp.sum(err * err, axis=0, keepdims=True) * (0.5 / D)
        dout = err * (1.0 / D)
        dgf_ref[...] += jnp.sum(dout * yhat, axis=0, keepdims=True)
        dyn = dout * gf
        dy = r1 * (dyn - yhat * jnp.mean(dyn * yhat, axis=-1, keepdims=True))
        dy_ref[...] = dy
        dyb = dy.astype(bf16)
        dwo_ref[...] += _dot_tn(mb, dyb)
        dm = _dot_nt(dyb, wo_ref[...])
        dpa = (dm * ga).astype(bf16)
        dpb = (dm * gb).astype(bf16)
        dp_ref[:, :D] = (dm * pa * ga * (1.0 - ga)).astype(bf16)
        dp_ref[:, D:] = (dm * pb * gb * (1.0 - gb)).astype(bf16)
        dwa_ref[...] += _dot_tn(a_v, dpa)
        dwb_ref[...] += _dot_tn(ob_v, dpb)
        da_ref[...] = _dot_nt(dpa, wa_ref[...])
        dob_ref[...] = _dot_nt(dpb, wb_ref[...])

    row = lambda: pl.BlockSpec((tm, D), lambda i: (i, 0))
    return pl.pallas_call(
        body,
        grid=(T // tm,),
        in_specs=[row(), row(), pl.BlockSpec((tm, 2 * D), lambda i: (i, COL_GATES)), row(), row(),
                  _resident((D, D)), _resident((D, D)), _resident((D, D)), _resident((1, D))],
        out_specs=[pl.BlockSpec((tm, 2 * D), lambda i: (i, COL_GATES)), row(), row(), row(),
                   _resident((D, D)), _resident((D, D)), _resident((D, D)), _resident((1, D)), _resident((1, D))],
        out_shape=[jax.ShapeDtypeStruct((T, NMAIN), bf16),
                   jax.ShapeDtypeStruct((T, D), f32), jax.ShapeDtypeStruct((T, D), f32),
                   jax.ShapeDtypeStruct((T, D), f32),
                   jax.ShapeDtypeStruct((D, D), f32), jax.ShapeDtypeStruct((D, D), f32),
                   jax.ShapeDtypeStruct((D, D), f32),
                   jax.ShapeDtypeStruct((1, D), f32), jax.ShapeDtypeStruct((1, D), f32)],
        compiler_params=_params(("arbitrary",)),
        name="merge_fwd_bwd",
    )(x, tgt, proj, a, ob, w_a, w_b, w_o, g_f)


def _dx_bwd(x, dy, dproj, dlr, g0, w_t, riders=(), sends_what=(), tm=256):
    T = x.shape[0]
    nsteps = T // tm
    n = len(riders)

    def body(x_ref, dy_ref, dp_ref, dl_ref, g_ref, w_ref, *rest):
        srcs, (dx_ref, dg_ref), dsts, sems = rest[:n], rest[n:n + 2], rest[n + 2:2 * n + 2], rest[2 * n + 2:]

        @pl.when(pl.program_id(0) == 0)
        def _():
            dg_ref[...] = jnp.zeros_like(dg_ref)
            for cp in _exchange_copies(srcs, dsts, sends_what, sems) if n else []:
                cp.start()

        xv = x_ref[...]
        r = lax.rsqrt(jnp.mean(xv * xv, axis=-1, keepdims=True) + EPS)
        xhat = xv * r
        dh = (_dot(dp_ref[:, :LR_COL], w_ref[:LR_COL, :]) + _dot(dp_ref[:, LR_COL:], w_ref[LR_COL + RANK:, :])
              + _dot(dl_ref[...], w_ref[LR_ROWS, :]))
        dg_ref[...] += jnp.sum(dh * xhat, axis=0, keepdims=True)
        t = dh * g_ref[...]
        dx_ref[...] = dy_ref[...] + r * (t - xhat * jnp.mean(t * xhat, axis=-1, keepdims=True))

        if n:
            @pl.when(pl.program_id(0) == nsteps - 1)
            def _():
                for cp in _exchange_copies(srcs, dsts, sends_what, sems):
                    cp.wait()

    row = lambda: pl.BlockSpec((tm, D), lambda i: (i, 0))
    return pl.pallas_call(
        body,
        grid=(nsteps,),
        in_specs=[row(), row(), pl.BlockSpec((tm, NMAIN), lambda i: (i, 0)),
                  pl.BlockSpec((tm, LRP), lambda i: (i, 0)),
                  _resident((1, D)), _resident((NMAIN + RANK, D))] + [_ANY] * n,
        out_specs=[row(), _resident((1, D))] + [_ANY] * n,
        out_shape=[jax.ShapeDtypeStruct((T, D), f32), jax.ShapeDtypeStruct((1, D), f32)]
        + _exchange_shapes(riders, sends_what),
        scratch_shapes=_exchange_sems(n) if n else [],
        compiler_params=_params(("arbitrary",)),
        name="dx_bwd",
    )(x, dy, dproj, dlr, g0, w_t, *riders)


def _dw_in(h, dproj, dlr, pair_dtype=None, tm=1024, tn=1024):
    T = h.shape[0]
    tm = min(tm, T)
    nj, nk = NMAIN // tn, T // tm
    lr_tile = LR_COL // tn
    pair = pair_dtype is not None
    hd = D // 2

    def body(h_ref, dp_ref, dl_ref, out_ref, acc, lr_acc, sems, lr_sem, *more):
        j, k = pl.program_id(0), pl.program_id(1)
        slot = j % 2

        def tile_row(jj):
            return pl.multiple_of(jj * tn + jnp.where(jj >= lr_tile, RANK, 0), 8)

        if pair:
            land, lr_land, part_buf, lr_part, swap_send, swap_recv = more
            c = lax.axis_index("c")
            mine = pl.ds(pl.multiple_of(c * hd, 128), hd)
            other = pl.ds(pl.multiple_of((1 - c) * hd, 128), hd)

            def tile_swap(jj, s):
                return _remote(acc.at[s, :, other], land.at[jj], swap_send.at[jj], swap_recv.at[jj], _sibling())

            def lr_swap():
                return _remote(lr_acc.at[pl.ds(0, RANK), other], lr_land, swap_send.at[nj], swap_recv.at[nj], _sibling())

            def tile_out(jj, s):
                return pltpu.make_async_copy(part_buf.at[s], out_ref.at[pl.ds(tile_row(jj), tn)], sems.at[s])

            def finish_tile(jj, s):
                tile_swap(jj, s).wait()
                part_buf[s] = (acc[s, :, mine] + land[jj]).astype(pair_dtype)
                tile_out(jj, s).start()

            lr_out = pltpu.make_async_copy(lr_part, out_ref.at[pl.ds(LR_COL, RANK)], lr_sem)
        else:
            def tile_out(jj, s):
                return pltpu.make_async_copy(acc.at[s], out_ref.at[pl.ds(tile_row(jj), tn)], sems.at[s])

            lr_out = pltpu.make_async_copy(lr_acc.at[pl.ds(0, RANK)], out_ref.at[pl.ds(LR_COL, RANK)], lr_sem)

        @pl.when(j == 0)
        def _():
            @pl.when(k == 0)
            def _():
                lr_acc[...] = jnp.zeros_like(lr_acc)

            lr_acc[...] += _dot_tn(dl_ref[...], h_ref[...])

            @pl.when(k == nk - 1)
            def _():
                if pair:
                    lr_swap().start()
                else:
                    lr_out.start()

        @pl.when(k == 0)
        def _():
            acc[slot] = jnp.zeros((tn, D), f32)

        acc[slot] += _dot_tn(dp_ref[...], h_ref[...])

        @pl.when(k == nk - 1)
        def _():
            if pair:
                tile_swap(j, slot).start()

                @pl.when(j >= 3)
                def _():
                    tile_out(j - 3, 1 - slot).wait()

                @pl.when(j >= 1)
                def _():
                    finish_tile(j - 1, 1 - slot)

                @pl.when(j == nj - 1)
                def _():
                    tile_out(j - 2, slot).wait()
                    finish_tile(j, slot)
                    lr_swap().wait()
                    lr_part[...] = (lr_acc[0:RANK, mine] + lr_land[...]).astype(pair_dtype)
                    lr_out.start()
                    tile_out(j - 1, 1 - slot).wait()
                    tile_out(j, slot).wait()
                    lr_out.wait()
            else:
                tile_out(j, slot).start()

                @pl.when(j > 0)
                def _():
                    tile_out(j - 1, 1 - slot).wait()

                @pl.when(j == nj - 1)
                def _():
                    tile_out(j, slot).wait()
                    lr_out.wait()

    pair_scratch = [pltpu.VMEM((nj, tn, hd), f32), pltpu.VMEM((RANK, hd), f32), pltpu.VMEM((2, tn, hd), pair_dtype),
                    pltpu.VMEM((RANK, hd), pair_dtype)] + [pltpu.SemaphoreType.DMA((nj + 1,))] * 2 if pair else []
    return pl.pallas_call(
        body,
        grid=(nj, nk),
        in_specs=[pl.BlockSpec((tm, D), lambda j, k: (k, 0)), pl.BlockSpec((tm, tn), lambda j, k: (k, j)),
                  pl.BlockSpec((tm, LRP), lambda j, k: (k, 0))],
        out_specs=_ANY,
        out_shape=jax.ShapeDtypeStruct((NMAIN + RANK, hd), pair_dtype) if pair
        else jax.ShapeDtypeStruct((NMAIN + RANK, D), f32),
        scratch_shapes=[pltpu.VMEM((2, tn, D), f32), pltpu.VMEM((LRP, D), f32),
                        pltpu.SemaphoreType.DMA((2,)), pltpu.SemaphoreType.DMA] + pair_scratch,
        compiler_params=_params(("arbitrary", "arbitrary")),
        name="dw_in",
    )(h, dproj, dlr)


MESH = pl.DeviceIdType.MESH


def _place():
    x, y, c = lax.axis_index("x"), lax.axis_index("y"), lax.axis_index("c")
    others = [(1 - x, y), (x, 1 - y), (1 - x, 1 - y)]
    return x, y, c, 2 * x + y, others


def _sibling():
    return lax.axis_index("x"), lax.axis_index("y"), 1 - lax.axis_index("c")


def _remote(src, dst, send_sem, recv_sem, to):
    return pltpu.make_async_remote_copy(src_ref=src, dst_ref=dst, send_sem=send_sem, recv_sem=recv_sem,
                                        device_id=to, device_id_type=MESH)


def _half(ref, e, by_columns):
    if not by_columns:
        return ref.at[e]
    hw = ref.shape[-1] // 2
    return ref.at[:, pl.ds(pl.multiple_of(e * hw, 128), hw)]


RELAY_ROWS = 1024


def _gather_win(shard, small):
    h, w = shard.shape
    part_a, part_b = pl.ds(0, RELAY_ROWS), pl.ds(RELAY_ROWS, h - RELAY_ROWS)

    def body(src, small_src, dst, small_dst, send_sems, recv_sems, relay_send, relay_recv, pass_send, pass_recv, own_sems,
             small_send, small_recv, small_own, stage, small_stage):
        x, y, c, me, others = _place()
        (to_x, to_y, _), sibling = others, (x, y, 1 - c)
        j_x, j_y, j_d = (2 * cx + cy for cx, cy in others)
        cols = pl.ds(pl.multiple_of(c * (w // 2), 128), w // 2)
        theirs = pl.ds(pl.multiple_of((1 - c) * (w // 2), 128), w // 2)

        def mine(j, rows=pl.ds(0, h)):
            return dst.at[j, rows, cols]

        small_sends = [_remote(small_src, small_dst.at[me], small_send.at[k], small_recv.at[k], (*to, c))
                       for k, to in enumerate(others)]
        for cp in small_sends:
            cp.start()
        small_in = pltpu.make_async_copy(small_src, small_stage, small_own.at[0])
        small_out = pltpu.make_async_copy(small_stage, small_dst.at[me], small_own.at[1])
        small_in.start()

        to_stage = pltpu.make_async_copy(src, stage, own_sems.at[0])
        to_slot = pltpu.make_async_copy(stage, dst.at[me], own_sems.at[1])
        sends = [_remote(src.at[:, cols], mine(me), send_sems.at[k], recv_sems.at[k], (*to, c))
                 for k, to in enumerate((to_x, to_y))]
        for cp in sends:
            cp.start()
        to_stage.start()
        relays = [_remote(mine(j_x, part_a), mine(j_x, part_a), relay_send.at[0], relay_recv.at[0], (*to_y, c)),
                  _remote(mine(j_y, part_b), mine(j_y, part_b), relay_send.at[1], relay_recv.at[1], (*to_x, c))]
        landed = [mine(j_x), mine(j_y), mine(j_d, part_a), mine(j_d, part_b)]
        passes = [_remote(place, place, pass_send.at[k], pass_recv.at[k], sibling) for k, place in enumerate(landed)]
        for k in range(2):
            _remote(src.at[:, cols], landed[k], send_sems.at[k], recv_sems.at[k], sibling).wait_recv()
            relays[k].start()
            passes[k].start()
        to_stage.wait()
        to_slot.start()
        for k in range(2):
            _remote(landed[2 + k], landed[2 + k], relay_send.at[k], relay_recv.at[k], sibling).wait_recv()
            passes[2 + k].start()
        for k, place in enumerate([(j_x, pl.ds(0, h)), (j_y, pl.ds(0, h)), (j_d, part_a), (j_d, part_b)]):
            got = dst.at[place[0], place[1], theirs]
            _remote(got, got, pass_send.at[k], pass_recv.at[k], sibling).wait_recv()
        for cp in sends + relays + passes:
            cp.wait_send()
        to_slot.wait()
        small_in.wait()
        small_out.start()
        for k, (cx, cy) in enumerate(others):
            _remote(small_src, small_dst.at[2 * cx + cy], small_send.at[k], small_recv.at[k], sibling).wait_recv()
        for cp in small_sends:
            cp.wait_send()
        small_out.wait()

    return pl.pallas_call(
        body,
        in_specs=[_ANY] * 2,
        out_specs=[_ANY] * 2,
        out_shape=[jax.ShapeDtypeStruct((NCHIP, h, w), shard.dtype),
                   jax.ShapeDtypeStruct((NCHIP,) + small.shape, small.dtype)],
        scratch_shapes=[pltpu.SemaphoreType.DMA((2,))] * 4 + [pltpu.SemaphoreType.DMA((4,))] * 2
        + [pltpu.SemaphoreType.DMA((2,))] + [pltpu.SemaphoreType.DMA((3,))] * 2 + [pltpu.SemaphoreType.DMA((2,))]
        + [pltpu.VMEM((h, w), shard.dtype), pltpu.VMEM(small.shape, small.dtype)],
        compiler_params=pltpu.CompilerParams(vmem_limit_bytes=VMEM_LIMIT),
        name="gather_win",
    )(shard, small)


def _gathered_shapes(shards):
    return [jax.ShapeDtypeStruct((NCHIP,) + s.shape, s.dtype) for s in shards]


def _gather_scratch(shards):
    n = len(shards)
    return ([pltpu.SemaphoreType.DMA((3, n))] * 4 + [pltpu.SemaphoreType.DMA((2, n))]
            + [pltpu.VMEM(s.shape, s.dtype) for s in shards])


def _own_to_stage(srcs, scratch):
    own_sems, stages = scratch[4], scratch[5:]
    return [pltpu.make_async_copy(srcs[a], stages[a], own_sems.at[0, a]) for a in range(len(srcs))]


def _own_to_slot(dsts, scratch):
    own_sems, stages = scratch[4], scratch[5:]
    me = _place()[3]
    return [pltpu.make_async_copy(stages[a], dsts[a].at[me], own_sems.at[1, a]) for a in range(len(dsts))]


def _gather_copies(srcs, dsts, by_columns, sems, sends_only):
    n = len(srcs)
    send_sems, recv_sems, pass_send, pass_recv = sems[:4]
    x, y, c, me, others = _place()
    sibling = (x, y, 1 - c)

    def src(a, e):
        return _half(srcs[a], e, by_columns[a])

    def dst(a, j, e):
        return _half(dsts[a].at[j], e, by_columns[a])

    sends, arrivals, passes, passed = [], [], [], []
    for k, (cx, cy) in enumerate(others):
        j = 2 * cx + cy
        for a in range(n):
            sends.append(_remote(src(a, c), dst(a, me, c), send_sems.at[k, a], recv_sems.at[k, a], (cx, cy, c)))
            if not sends_only:
                arrivals.append(_remote(src(a, c), dst(a, j, c), send_sems.at[k, a], recv_sems.at[k, a], (cx, cy, c)))
                passes.append(_remote(dst(a, j, c), dst(a, j, c), pass_send.at[k, a], pass_recv.at[k, a], sibling))
                passed.append(_remote(src(a, c), dst(a, j, 1 - c), pass_send.at[k, a], pass_recv.at[k, a], sibling))
    return sends, arrivals, passes, passed


def _gather_start(srcs, dsts, by_columns, scratch):
    for cp in _gather_copies(srcs, dsts, by_columns, scratch, sends_only=True)[0]:
        cp.start()
    for cp in _own_to_stage(srcs, scratch):
        cp.start()


def _gather_finish(srcs, dsts, by_columns, scratch):
    for cp in _own_to_stage(srcs, scratch):
        cp.wait()
    own = _own_to_slot(dsts, scratch)
    for cp in own:
        cp.start()
    sends, arrivals, passes, passed = _gather_copies(srcs, dsts, by_columns, scratch, sends_only=False)
    for arrival, cp in zip(arrivals, passes):
        arrival.wait_recv()
        cp.start()
    for arrival in passed:
        arrival.wait_recv()
    for cp in sends + passes:
        cp.wait_send()
    for cp in own:
        cp.wait()


HALF_FIRST, CHIP_FIRST, BY_COLUMNS = "half_first", "chip_first", "by_columns"


def _sibling_halves(bufs, kinds):
    n = len(bufs)

    def body(*refs):
        cps = _halves_copies(refs[:n], refs[n:2 * n], kinds, refs[2 * n:])
        for cp in cps:
            cp.start()
        for cp in cps:
            cp.wait()

    return pl.pallas_call(
        body,
        in_specs=[_ANY] * n,
        out_specs=[_ANY] * n,
        out_shape=_halves_shapes(bufs, kinds),
        scratch_shapes=_halves_sems(bufs),
        name="sibling_halves",
    )(*bufs)


def _halves_shapes(bufs, kinds):
    def landed(b, kind):
        if kind == HALF_FIRST:
            return b.shape[1:]
        if kind == CHIP_FIRST:
            return (b.shape[0],) + b.shape[2:]
        return b.shape[:2] + (b.shape[2] // 2,)

    return [jax.ShapeDtypeStruct(landed(b, kind), b.dtype) for b, kind in zip(bufs, kinds)]


def _halves_sems(bufs):
    return [pltpu.SemaphoreType.DMA((len(bufs), NCHIP))] * 2


def _halves_copies(srcs, dsts, kinds, sems):
    send_sems, recv_sems = sems
    x, y, c, _, _ = _place()
    cps = []
    for a, kind in enumerate(kinds):
        if kind == HALF_FIRST:
            cps.append(_remote(srcs[a].at[1 - c], dsts[a], send_sems.at[a, 0], recv_sems.at[a, 0], (x, y, 1 - c)))
        else:
            cps += [_remote(_half(srcs[a].at[j], 1 - c, kind == BY_COLUMNS), dsts[a].at[j],
                            send_sems.at[a, j], recv_sems.at[a, j], (x, y, 1 - c)) for j in range(srcs[a].shape[0])]
    return cps


TO_ITS_CHIP, TO_EVERY_CHIP, ROWS_TO_ITS_CHIP = "to_its_chip", "to_every_chip", "rows_to_its_chip"
PIECE_STEP = 2048
PIECE_ROWS = 2064


def _chip_exchange(parts, sends_what):
    n = len(parts)

    def body(*refs):
        cps = _exchange_copies(refs[:n], refs[n:2 * n], sends_what, refs[2 * n:])
        for cp in cps:
            cp.start()
        for cp in cps:
            cp.wait()

    return pl.pallas_call(
        body,
        in_specs=[_ANY] * n,
        out_specs=[_ANY] * n,
        out_shape=_exchange_shapes(parts, sends_what),
        scratch_shapes=_exchange_sems(n),
        name="chip_exchange",
    )(*parts)


def _exchange_shapes(parts, sends_what):
    def landed(p, what):
        return (3, PIECE_ROWS, p.shape[1]) if what == ROWS_TO_ITS_CHIP else (3,) + p.shape[1:]

    return [jax.ShapeDtypeStruct(landed(p, what), p.dtype) for p, what in zip(parts, sends_what)]


def _exchange_sems(n):
    return [pltpu.SemaphoreType.DMA((3, n))] * 2


def _exchange_copies(srcs, dsts, sends_what, sems):
    send_sems, recv_sems = sems
    x, y, c, me, others = _place()

    def part(a, j):
        if sends_what[a] == ROWS_TO_ITS_CHIP:
            return srcs[a].at[pl.ds(pl.multiple_of(j * PIECE_STEP, PIECE_STEP), PIECE_ROWS)]
        return srcs[a].at[j if sends_what[a] == TO_ITS_CHIP else 0]

    return [_remote(part(a, 2 * cx + cy), dsts[a].at[k], send_sems.at[k, a], recv_sems.at[k, a], (cx, cy, c))
            for k, (cx, cy) in enumerate(others) for a in range(len(srcs))]


def _sibling_swap(halves):
    n = len(halves)

    def body(*refs):
        srcs, dsts = refs[:n], refs[n:2 * n]
        send_sems, recv_sems = refs[2 * n:]
        x, y, c, _, _ = _place()
        cps = [_remote(srcs[a], dsts[a], send_sems.at[a], recv_sems.at[a], (x, y, 1 - c)) for a in range(n)]
        for cp in cps:
            cp.start()
        for cp in cps:
            cp.wait()

    return pl.pallas_call(
        body,
        in_specs=[_ANY] * n,
        out_specs=[_ANY] * n,
        out_shape=[jax.ShapeDtypeStruct(s.shape, s.dtype) for s in halves],
        scratch_shapes=[pltpu.SemaphoreType.DMA((n,))] * 2,
        name="sibling_swap",
    )(*halves)


def _tile(h, w, operands=5):
    if h % 128 == 0:
        return 128, w
    budget = VMEM_LIMIT * 3 // 4 // (2 * operands * 4)
    tw = w
    while h * tw > budget and tw % 256 == 0:
        tw //= 2
    return h, tw


def _pair_sum(place, bufs, kind, gots, out_dtype):
    m = len(bufs)
    nj, h, w = gots[0].shape
    th, tw = _tile(h, w, operands=3 * m)
    nq = w // tw

    def body(p_ref, *refs):
        del p_ref
        for a_ref, b_ref, o_ref in zip(refs[:m], refs[m:2 * m], refs[2 * m:]):
            o_ref[...] = (a_ref[...] + b_ref[...]).astype(out_dtype)

    if kind == HALF_FIRST:
        mine = pl.BlockSpec((None, None, th, tw), lambda j, r, q, p: (p[0], j, r, q))
    elif kind == CHIP_FIRST:
        mine = pl.BlockSpec((None, None, th, tw), lambda j, r, q, p: (j, p[0], r, q))
    else:
        mine = pl.BlockSpec((None, th, tw), lambda j, r, q, p: (j, r, p[0] * nq + q))
    landed = pl.BlockSpec((None, th, tw), lambda j, r, q, p: (j, r, q))
    return pl.pallas_call(
        body,
        grid_spec=pltpu.PrefetchScalarGridSpec(
            num_scalar_prefetch=1,
            grid=(nj, h // th, w // tw),
            in_specs=[mine] * m + [landed] * m,
            out_specs=[landed] * m,
        ),
        out_shape=[jax.ShapeDtypeStruct((nj, h, w), out_dtype)] * m,
        compiler_params=_params(("parallel", "parallel", "parallel")),
        name="pair_sum",
    )(place, *bufs, *gots)


def _chip_sum(place, parts, slots):
    m = len(parts)
    nj, h, w = parts[0].shape
    th, tw = _tile(h, w, operands=5 * m)

    def body(p_ref, *refs):
        me = p_ref[1]
        for own_ref, s_ref, o_ref in zip(refs[:m], refs[m:2 * m], refs[2 * m:]):
            own = own_ref[...].astype(f32)
            by_flip = {2: s_ref[0].astype(f32), 1: s_ref[1].astype(f32), 3: s_ref[2].astype(f32)}
            acc = None
            for j in range(NCHIP):
                flip = me ^ j
                term = jnp.where(flip == 0, own,
                                 jnp.where(flip == 2, by_flip[2], jnp.where(flip == 1, by_flip[1], by_flip[3])))
                acc = term if acc is None else acc + term
            o_ref[...] = acc

    return pl.pallas_call(
        body,
        grid_spec=pltpu.PrefetchScalarGridSpec(
            num_scalar_prefetch=1,
            grid=(h // th, w // tw),
            in_specs=[pl.BlockSpec((None, th, tw), lambda r, q, p: (p[1] if nj == NCHIP else 0, r, q))] * m
            + [pl.BlockSpec((3, th, tw), lambda r, q, p: (0, r, q))] * m,
            out_specs=[pl.BlockSpec((th, tw), lambda r, q, p: (r, q))] * m,
        ),
        out_shape=[jax.ShapeDtypeStruct((h, w), f32)] * m,
        compiler_params=_params(("parallel", "parallel")),
        name="chip_sum",
    )(place, *parts, *slots)


def _adamw_math(w, g, m, v):
    nm = ADAM_B1 * m + (1.0 - ADAM_B1) * g
    nv = ADAM_B2 * v + (1.0 - ADAM_B2) * (g * g)
    m_hat = nm / (1.0 - ADAM_B1 ** ADAM_STEP)
    v_hat = nv / (1.0 - ADAM_B2 ** ADAM_STEP)
    return -ADAM_LR * (m_hat / (jnp.sqrt(v_hat) + ADAM_EPS) + ADAM_WD * w), nm, nv


def _adamw(w, g, m, v):
    rows, width = w.shape
    th, tw = _tile(rows, width, operands=7)

    def body(w_ref, g_ref, m_ref, v_ref, d_ref, nm_ref, nv_ref):
        d_ref[...], nm_ref[...], nv_ref[...] = _adamw_math(w_ref[...], g_ref[...], m_ref[...], v_ref[...])

    spec = pl.BlockSpec((th, tw), lambda r, q: (r, q))
    return pl.pallas_call(
        body,
        grid=(rows // th, width // tw),
        in_specs=[spec] * 4,
        out_specs=[spec] * 3,
        out_shape=[jax.ShapeDtypeStruct((rows, width), f32)] * 3,
        compiler_params=_params(("parallel", "parallel")),
        name="adamw",
    )(w, g, m, v)


def _adamw_halves(place, ws, mines, gots, ms, vs, axis):
    k = len(ws)
    rows, width = ws[0].shape
    h, hw = mines[0].shape
    th, tw = _tile(h, hw, operands=10 * k)
    nr, nq = h // th, hw // tw

    def body(p_ref, *refs):
        ins, outs = refs[:5 * k], refs[5 * k:]
        for i in range(k):
            w_ref, a_ref, b_ref, m_ref, v_ref = ins[i::k]
            g_ref, d_ref, nm_ref, nv_ref = outs[i::k]
            g = jnp.where(pl.program_id(0) == p_ref[0], a_ref[...], b_ref[...])
            g_ref[...] = g
            d_ref[...], nm_ref[...], nv_ref[...] = _adamw_math(w_ref[...], g, m_ref[...], v_ref[...])

    if axis == 0:
        full = pl.BlockSpec((th, tw), lambda e, r, q, p: (e * nr + r, q))
    else:
        full = pl.BlockSpec((th, tw), lambda e, r, q, p: (r, e * nq + q))
    half = pl.BlockSpec((th, tw), lambda e, r, q, p: (r, q))
    res = pl.pallas_call(
        body,
        grid_spec=pltpu.PrefetchScalarGridSpec(
            num_scalar_prefetch=1,
            grid=(2, nr, nq),
            in_specs=[full] * k + [half] * (2 * k) + [full] * (2 * k),
            out_specs=[full] * (4 * k),
        ),
        out_shape=[jax.ShapeDtypeStruct((rows, width), f32)] * (4 * k),
        compiler_params=_params(("parallel", "parallel", "parallel")),
        name="adamw_halves",
    )(place, *ws, *mines, *gots, *ms, *vs)
    return [res[i::k] for i in range(k)]


_SMALL = (("norm_g", 8), ("ln_v_g", 8), ("ln_v_b", 8), ("w_spatial", 1024), ("b_spatial", 8), ("b_gate_up", 4),
          ("gla_norm_g", 2), ("final_norm_g", 8), ("w_gate_up", 64), ("loss", 8))
_SMALL_ROWS = 1152


def _pack_rows(arrays, rows):
    flat = jnp.concatenate([a.reshape(-1, 128) for a in arrays], axis=0)
    return jnp.pad(flat, ((0, rows - flat.shape[0]), (0, 0)))


def kernel(x, norm_g, w_in, ln_v_g, ln_v_b, w_spatial, b_spatial, w_gate_up, b_gate_up, gla_norm_g, w_branch_a, w_branch_b, w_out, final_norm_g, loss_target, m_norm_g, m_w_in, m_ln_v_g, m_ln_v_b, m_w_spatial, m_b_spatial, m_w_gate_up, m_b_gate_up, m_gla_norm_g, m_w_branch_a, m_w_branch_b, m_w_out, m_final_norm_g, v_norm_g, v_w_in, v_ln_v_g, v_ln_v_b, v_w_spatial, v_b_spatial, v_w_gate_up, v_b_gate_up, v_gla_norm_g, v_w_branch_a, v_w_branch_b, v_w_out, v_final_norm_g):
    chip = 2 * lax.axis_index("x") + lax.axis_index("y")
    core = lax.axis_index("c")
    place = jnp.stack([core, chip]).astype(jnp.int32)
    mat_names = ("w_branch_a", "w_branch_b", "w_out")

    wt_shard = jnp.transpose(w_in[0]).astype(bf16)
    mats = [w[0].astype(bf16).reshape(2, D // NCHIP // 2, D) for w in (w_branch_a, w_branch_b, w_out)]
    gate_sh = w_gate_up[0].reshape(2, RANK // 2, 128)
    g_win, g_gate = _gather_win(wt_shard, gate_sh)
    w_t = g_win.reshape(NCHIP * WIN_SHARD, D)
    w_gate = jnp.transpose(g_gate.reshape(NCHIP, RANK, 128), (1, 0, 2)).reshape(RANK, KEYB)
    w_gate = jnp.pad(w_gate, ((0, LRP - RANK), (0, 0)))
    b_sb = jnp.broadcast_to(b_spatial[0][:, :, None], (HA, CA, GA))
    xs, tgt = x[0], loss_target[0]

    proj, lr, h, a, o, ob, states, g_a, g_b, g_o = _forward(xs, norm_g, w_t, ln_v_g, ln_v_b, w_spatial[0], b_sb, w_gate,
                                                           b_gate_up, gla_norm_g, riders=mats)
    w_a, w_b, w_o = (g.reshape(D, D) for g in (g_a, g_b, g_o))
    dproj, dy, da, dob, dwa, dwb, dwo, dgf, loss_cols = _merge_fwd_bwd(xs, tgt, proj, a, ob, w_a, w_b, w_o,
                                                                       final_norm_g.reshape(1, D))
    b_mats = [t.reshape(NCHIP, 2, D // NCHIP // 2, D) for t in (dwa, dwb, dwo)]
    dproj, dws, dbs, dlg, dlb, *got_mats = _mixer_a_bwd(proj, da, dproj, ln_v_g, ln_v_b, w_spatial[0], b_sb,
                                                        riders=b_mats, kinds=[CHIP_FIRST] * 3)
    part_mats = _pair_sum(place, b_mats, CHIP_FIRST, got_mats, bf16)
    dproj, dlr, dwg, dbg, dgg, *slots_mats = _gla_bwd(proj, lr, o, states, dob, dproj, w_gate, b_gate_up, gla_norm_g,
                                                      riders=part_mats, sends_what=[TO_ITS_CHIP] * 3)
    part_win = _dw_in(h, dproj, dlr, pair_dtype=bf16)
    dx, dg0, slots_win = _dx_bwd(xs, dy, dproj, dlr, norm_g, w_t, riders=[part_win], sends_what=[ROWS_TO_ITS_CHIP])
    slots_big = [slots_win] + slots_mats
    small = _pack_rows([dg0, dlg, dlb, dws, dbs[:, :, 0], dbg, dgg, dgf, dwg[:RANK], loss_cols], _SMALL_ROWS)
    b_small = small.reshape(2, 1, _SMALL_ROWS // 2, 128)
    (got_small,) = _sibling_halves([b_small], [HALF_FIRST])
    (part_small,) = _pair_sum(place, [b_small], HALF_FIRST, [got_small], f32)
    (slots_small,) = _chip_exchange([part_small], [TO_EVERY_CHIP])
    own_win = lax.dynamic_slice_in_dim(part_win, chip * PIECE_STEP, PIECE_ROWS, axis=0)[None]
    mine = [*_chip_sum(place, [own_win], slots_big[:1]), *_chip_sum(place, part_mats, slots_big[1:]),
            *_chip_sum(place, [part_small], [slots_small])]
    theirs = list(_sibling_swap(mine))
    mine[0], theirs[0] = (lax.dynamic_slice_in_dim(t, (WIN_SHARD - PIECE_STEP) * chip, WIN_SHARD, axis=0)
                          for t in (mine[0], theirs[0]))

    g_small = jnp.where(core == 0, jnp.concatenate([mine[4], theirs[4]], axis=0),
                        jnp.concatenate([theirs[4], mine[4]], axis=0))
    grads = {}
    row = 0
    for name, rows in _SMALL:
        grads[name] = g_small[row:row + rows]
        row += rows
    loss = jnp.sum(grads["loss"])
    dwg_full = grads["w_gate_up"].reshape(RANK, KEYB)
    grads["w_gate_up"] = lax.dynamic_slice_in_dim(dwg_full, chip * 128, 128, axis=1)

    weights = dict(norm_g=norm_g, w_in=w_in, ln_v_g=ln_v_g, ln_v_b=ln_v_b, w_spatial=w_spatial, b_spatial=b_spatial,
                   w_gate_up=w_gate_up, b_gate_up=b_gate_up, gla_norm_g=gla_norm_g, w_branch_a=w_branch_a,
                   w_branch_b=w_branch_b, w_out=w_out, final_norm_g=final_norm_g)
    m_in = dict(norm_g=m_norm_g, w_in=m_w_in, ln_v_g=m_ln_v_g, ln_v_b=m_ln_v_b, w_spatial=m_w_spatial,
                b_spatial=m_b_spatial, w_gate_up=m_w_gate_up, b_gate_up=m_b_gate_up, gla_norm_g=m_gla_norm_g,
                w_branch_a=m_w_branch_a, w_branch_b=m_w_branch_b, w_out=m_w_out, final_norm_g=m_final_norm_g)
    v_in = dict(norm_g=v_norm_g, w_in=v_w_in, ln_v_g=v_ln_v_g, ln_v_b=v_ln_v_b, w_spatial=v_w_spatial,
                b_spatial=v_b_spatial, w_gate_up=v_w_gate_up, b_gate_up=v_b_gate_up, gla_norm_g=v_gla_norm_g,
                w_branch_a=v_w_branch_a, w_branch_b=v_w_branch_b, w_out=v_w_out, final_norm_g=v_final_norm_g)
    names = list(weights)
    small_names = [n for n in names if n != "w_in" and n not in mat_names]
    out_g, out_d, out_m, out_v = {}, {}, {}, {}
    (res,) = _adamw_halves(place, [jnp.transpose(w_in[0])], mine[:1], theirs[:1], [jnp.transpose(m_w_in[0])],
                           [jnp.transpose(v_w_in[0])], axis=1)
    out_g["w_in"], out_d["w_in"], out_m["w_in"], out_v["w_in"] = (jnp.transpose(t)[None] for t in res)
    res_mats = _adamw_halves(place, [weights[n][0] for n in mat_names], mine[1:4], theirs[1:4],
                             [m_in[n][0] for n in mat_names], [v_in[n][0] for n in mat_names], axis=0)
    for n, res in zip(mat_names, res_mats):
        out_g[n], out_d[n], out_m[n], out_v[n] = (t[None] for t in res)
    upd_rows = sum(weights[n].size for n in small_names) // 128
    pad_rows = -(-upd_rows // 8) * 8
    packed = [_pack_rows([t[n] for n in small_names], pad_rows) for t in (weights, grads, m_in, v_in)]
    d_s, m_s, v_s = _adamw(*packed)
    row = 0
    for n in small_names:
        shape = weights[n].shape
        rows = weights[n].size // 128
        out_g[n] = grads[n].reshape(shape)
        out_d[n], out_m[n], out_v[n] = (t[row:row + rows].reshape(shape) for t in (d_s, m_s, v_s))
        row += rows
    return (loss, dx[None], *[out_g[n] for n in names], *[out_d[n] for n in names],
            *[out_m[n] for n in names], *[out_v[n] for n in names])
```

```python
import functools
import math

import jax
import jax.numpy as jnp
from jax import lax
from jax.experimental import pallas as pl
from jax.experimental.pallas import tpu as pltpu

f32 = jnp.float32
bf16 = jnp.bfloat16

D = 1024
NMAIN = 8192
LRP = 128
RANK = 16
HA, GA, CA = 8, 128, 128
HB, DK, DV, CB = 4, 128, 256, 64
KEYB = HB * DK
EPS = 1e-6
LN_EPS = 1e-5
GATE_NORM = 16.0
QSCALE = DK ** -0.5
COL_U, COL_V, COL_ZA = 0, 1, 2
COL_Q, COL_K = 6, 7
COL_VB, COL_ZB = 4, 5
COL_GATES = 3
VMEM_LIMIT = 56 * 1024 * 1024
NCHIP = 4
WIN_SHARD = 2052
LR_COL = 6144
_ANY = pl.BlockSpec(memory_space=pl.ANY)

ADAM_LR, ADAM_B1, ADAM_B2, ADAM_EPS, ADAM_WD, ADAM_STEP = 0.001, 0.9, 0.999, 1e-08, 0.01, 10

_SQRT_HALF = 0.7071067811865476
_INV_SQRT_2PI = 0.3989422804014327


def _dot(a, b):
    return jnp.dot(a, b, preferred_element_type=f32)


def _dot_nt(a, b):
    return lax.dot_general(a, b, (((1,), (1,)), ((), ())), preferred_element_type=f32)


def _dot_tn(a, b):
    return lax.dot_general(a, b, (((0,), (0,)), ((), ())), preferred_element_type=f32)


def _dot_exact(a, b):
    return jnp.dot(a, b, preferred_element_type=f32, precision=lax.Precision.HIGHEST)


def _gelu(x):
    return 0.5 * x * (1.0 + lax.erf(x * _SQRT_HALF))


def _gelu_and_grad(x):
    cdf = 0.5 * (1.0 + lax.erf(x * _SQRT_HALF))
    return x * cdf, cdf + x * (jnp.exp(-0.5 * x * x) * _INV_SQRT_2PI)


def _sigmoid(x):
    return 0.5 * jnp.tanh(0.5 * x) + 0.5


def _params(sem):
    return pltpu.CompilerParams(dimension_semantics=sem, vmem_limit_bytes=VMEM_LIMIT)


def _resident(shape):
    nd = len(shape)
    return pl.BlockSpec(shape, lambda *_: (0,) * nd, pipeline_mode=pl.Buffered(1))


def _w_rows(c, tn):
    start = c * tn + (RANK if c * tn >= LR_COL else 0)
    return slice(start, start + tn)


LR_ROWS = slice(LR_COL, LR_COL + LRP)


def _forward(x, g0, w_t, ln_g, ln_b, w_s, b_sb, w_gate, b_gate, gla_g, riders=(), tm=256, tn=1024):
    T = x.shape[0]
    nsteps = T // tm
    n = len(riders)
    n_out = 7

    def body(x_ref, g_ref, w_ref, lg_ref, lb_ref, ws_ref, bs_ref, wg_ref, bg_ref, gg_ref, *rest):
        srcs, outs, dsts = rest[:n], rest[n:n + n_out], rest[n + n_out:2 * n + n_out]
        proj_ref, lr_ref, h_ref, a_ref, o_ref, ob_ref, st_ref = outs
        vln_s, state, la_s, *sems = rest[2 * n + n_out:]

        @pl.when(pl.program_id(0) == 0)
        def _():
            state[...] = jnp.zeros_like(state)
            if n:
                _gather_start(srcs, dsts, [False] * n, sems)

        xv = x_ref[...]
        r = lax.rsqrt(jnp.mean(xv * xv, axis=-1, keepdims=True) + EPS)
        h = (xv * r * g_ref[...]).astype(bf16)
        h_ref[...] = h
        def project(chunks):
            for c in chunks:
                proj_ref[:, c * tn:(c + 1) * tn] = _dot_nt(h, w_ref[_w_rows(c, tn), :])

        def cols(block, width):
            return proj_ref.at[:, block * width:(block + 1) * width]

        project(range(0, 3 * D // tn))
        _mixer_a_fwd_tile(cols(COL_U, D), cols(COL_V, D), cols(COL_ZA, D), lg_ref, lb_ref, ws_ref, bs_ref, a_ref, vln_s)
        lr_ref[...] = _dot_nt(h, w_ref[LR_ROWS, :])
        project(range(3 * D // tn, 6 * D // tn))
        _gla_fwd_tile(cols(COL_Q, KEYB), cols(COL_K, KEYB), cols(COL_VB, D), cols(COL_ZB, D), lr_ref, wg_ref, bg_ref,
                      gg_ref, o_ref, ob_ref, st_ref, state, la_s)
        project(range(6 * D // tn, NMAIN // tn))

        if n:
            @pl.when(pl.program_id(0) == nsteps - 1)
            def _():
                _gather_finish(srcs, dsts, [False] * n, sems)

    row = lambda width: pl.BlockSpec((tm, width), lambda i: (i, 0))
    return pl.pallas_call(
        body,
        grid=(nsteps,),
        in_specs=[row(D), _resident((1, D)), _resident((NMAIN + RANK, D)), _resident((1, D)), _resident((1, D)),
                  _resident((HA, CA, CA)), _resident((HA, CA, GA)),
                  _resident((LRP, KEYB)), _resident((1, KEYB)), _resident((1, DV))] + [_ANY] * n,
        out_specs=[row(NMAIN), row(LRP), row(D), row(D), row(D), row(D),
                   pl.BlockSpec((tm // CB, HB, DV, DK), lambda i: (i, 0, 0, 0))] + [_ANY] * n,
        out_shape=[jax.ShapeDtypeStruct((T, NMAIN), f32), jax.ShapeDtypeStruct((T, LRP), f32),
                   jax.ShapeDtypeStruct((T, D), bf16), jax.ShapeDtypeStruct((T, D), bf16),
                   jax.ShapeDtypeStruct((T, D), f32), jax.ShapeDtypeStruct((T, D), bf16),
                   jax.ShapeDtypeStruct((T // CB, HB, DV, DK), f32)] + _gathered_shapes(riders),
        scratch_shapes=[pltpu.VMEM((tm, D), bf16), pltpu.VMEM((HB, DV, DK), f32), pltpu.VMEM((tm, KEYB), f32)]
        + (_gather_scratch(riders) if n else []),
        compiler_params=_params(("arbitrary",)),
        name="forward",
    )(x, g0, w_t, ln_g, ln_b, w_s, b_sb, w_gate, b_gate, gla_g, *riders)


def _causal_mask():
    t = lax.broadcasted_iota(jnp.int32, (CA, CA), 0)
    s = lax.broadcasted_iota(jnp.int32, (CA, CA), 1)
    return s <= t


def _layernorm_parts(gv):
    mu = jnp.mean(gv, axis=-1, keepdims=True)
    xc = gv - mu
    rs = lax.rsqrt(jnp.mean(xc * xc, axis=-1, keepdims=True) + LN_EPS)
    return xc * rs, rs


def _mixer_a_fwd_tile(u_ref, v_ref, za_ref, lg_ref, lb_ref, ws_ref, bs_ref, a_ref, vln_s):
    tm = u_ref.shape[0]
    vhat, _ = _layernorm_parts(_gelu(v_ref[...]))
    vln_s[...] = (vhat * lg_ref[...] + lb_ref[...]).astype(bf16)
    mask = _causal_mask()
    for g in range(HA):
        wg = jnp.where(mask, ws_ref[g], 0.0).astype(bf16)
        cols = slice(g * GA, (g + 1) * GA)
        for c in range(tm // CA):
            rows = slice(c * CA, (c + 1) * CA)
            mixed = _dot(wg, vln_s[rows, cols]) + bs_ref[g]
            za = za_ref[rows, cols]
            a = _gelu(u_ref[rows, cols]) * mixed * (za * _sigmoid(za))
            a_ref[rows, cols] = a.astype(bf16)


def _mixer_a_bwd(proj, da, dproj, ln_g, ln_b, w_s, b_sb, riders=(), kinds=(), tm=256):
    T = proj.shape[0]
    nsteps = T // tm
    n = len(riders)

    def body(u_ref, v_ref, za_ref, da_ref, dp_in, lg_ref, lb_ref, ws_ref, bs_ref, *rest):
        srcs, (dp_ref, dws_ref, dbs_ref, dlg_ref, dlb_ref), dsts = rest[:n], rest[n:n + 5], rest[n + 5:2 * n + 5]
        vln_s, dvln_s, *sems = rest[2 * n + 5:]
        del dp_in
        i = pl.program_id(0)

        @pl.when(i == 0)
        def _():
            dws_ref[...] = jnp.zeros_like(dws_ref)
            dbs_ref[...] = jnp.zeros_like(dbs_ref)
            dlg_ref[...] = jnp.zeros_like(dlg_ref)
            dlb_ref[...] = jnp.zeros_like(dlb_ref)
            for cp in _halves_copies(srcs, dsts, kinds, sems) if n else []:
                cp.start()

        gv, gv_grad = _gelu_and_grad(v_ref[...])
        vhat, rs = _layernorm_parts(gv)
        vln_s[...] = (vhat * lg_ref[...] + lb_ref[...]).astype(bf16)
        mask = _causal_mask()
        for g in range(HA):
            wg = jnp.where(mask, ws_ref[g], 0.0).astype(bf16)
            cols = slice(g * GA, (g + 1) * GA)
            dw_acc = jnp.zeros((CA, CA), f32)
            db_acc = jnp.zeros((CA, 1), f32)
            for c in range(tm // CA):
                rows = slice(c * CA, (c + 1) * CA)
                vln = vln_s[rows, cols]
                mixed = _dot(wg, vln) + bs_ref[g]
                u = u_ref[rows, cols]
                za = za_ref[rows, cols]
                da_blk = da_ref[rows, cols]
                sg = _sigmoid(za)
                sz = za * sg
                gu, gu_grad = _gelu_and_grad(u)
                dp_ref[rows, cols] = (da_blk * mixed * sz * gu_grad).astype(bf16)
                dp_ref[rows, 2 * D + g * GA:2 * D + (g + 1) * GA] = (
                    da_blk * gu * mixed * (sg * (1.0 + za * (1.0 - sg)))).astype(bf16)
                dmixed = da_blk * gu * sz
                dmb = dmixed.astype(bf16)
                dvln_s[rows, cols] = _dot_tn(wg, dmb)
                dw_acc = dw_acc + _dot_nt(dmb, vln)
                db_acc = db_acc + jnp.sum(dmixed, axis=-1, keepdims=True)
            dws_ref[g] += dw_acc
            dbs_ref[g] += jnp.broadcast_to(db_acc, (CA, GA))

        dvln = dvln_s[...]
        dlg_ref[...] += jnp.sum(dvln * vhat, axis=0, keepdims=True)
        dlb_ref[...] += jnp.sum(dvln, axis=0, keepdims=True)
        dvhat = dvln * lg_ref[...]
        dgv = rs * (dvhat - jnp.mean(dvhat, axis=-1, keepdims=True)
                    - vhat * jnp.mean(dvhat * vhat, axis=-1, keepdims=True))
        dp_ref[:, D:2 * D] = (dgv * gv_grad).astype(bf16)

        @pl.when(i == nsteps - 1)
        def _():
            for g in range(HA):
                dws_ref[g] = jnp.where(mask, dws_ref[g], 0.0)
            for cp in _halves_copies(srcs, dsts, kinds, sems) if n else []:
                cp.wait()

    def col(cidx):
        return pl.BlockSpec((tm, D), lambda i, c=cidx: (i, c))

    return pl.pallas_call(
        body,
        grid=(nsteps,),
        in_specs=[col(COL_U), col(COL_V), col(COL_ZA), pl.BlockSpec((tm, D), lambda i: (i, 0)),
                  pl.BlockSpec(memory_space=pl.ANY),
                  _resident((1, D)), _resident((1, D)), _resident((HA, CA, CA)), _resident((HA, CA, GA))] + [_ANY] * n,
        out_specs=[pl.BlockSpec((tm, 3 * D), lambda i: (i, 0)),
                   _resident((HA, CA, CA)), _resident((HA, CA, GA)), _resident((1, D)), _resident((1, D))] + [_ANY] * n,
        out_shape=[jax.ShapeDtypeStruct(dproj.shape, dproj.dtype),
                   jax.ShapeDtypeStruct((HA, CA, CA), f32), jax.ShapeDtypeStruct((HA, CA, GA), f32),
                   jax.ShapeDtypeStruct((1, D), f32), jax.ShapeDtypeStruct((1, D), f32)] + _halves_shapes(riders, kinds),
        scratch_shapes=[pltpu.VMEM((tm, D), bf16), pltpu.VMEM((tm, D), f32)] + (_halves_sems(riders) if n else []),
        input_output_aliases={4: 0},
        compiler_params=_params(("arbitrary",)),
        name="mixer_a_bwd",
    )(proj, proj, proj, da, dproj, ln_g, ln_b, w_s, b_sb, *riders)


def _tri(n, upper):
    r = lax.broadcasted_iota(jnp.int32, (n, n), 0)
    c = lax.broadcasted_iota(jnp.int32, (n, n), 1)
    return jnp.where((c >= r) if upper else (c <= r), 1.0, 0.0).astype(f32)


def _chunk_tri(n, upper):
    r = lax.broadcasted_iota(jnp.int32, (n, n), 0)
    c = lax.broadcasted_iota(jnp.int32, (n, n), 1)
    shift = CB.bit_length() - 1
    same_chunk = jnp.right_shift(r, shift) == jnp.right_shift(c, shift)
    return jnp.where(same_chunk & ((c >= r) if upper else (c <= r)), 1.0, 0.0).astype(f32)


def _log_alpha(lr, wg, bg):
    logit = _dot(lr.astype(bf16), wg.astype(bf16)) + bg
    la = (jnp.minimum(logit, 0.0) - jnp.log1p(jnp.exp(-jnp.abs(logit)))) * (1.0 / GATE_NORM)
    return logit, la


def _gla_fwd_tile(q_ref, k_ref, v_ref, zb_ref, lr_ref, wg_ref, bg_ref, gg_ref, o_ref, ob_ref, st_ref, state, la_s):
    tm = q_ref.shape[0]
    _, la = _log_alpha(lr_ref[...], wg_ref[...], bg_ref[...])
    la_s[...] = _dot_exact(_chunk_tri(tm, upper=False), la)
    causal = _tri(CB, upper=False) > 0.5
    states = [state[hd] for hd in range(HB)]
    for c in range(tm // CB):
        rows = slice(c * CB, (c + 1) * CB)
        b = la_s[rows, :]
        bl = b[CB - 1:CB, :]
        bm = b[CB // 2 - 1:CB // 2, :]
        q = q_ref[rows, :] * QSCALE
        k = k_ref[rows, :]
        qi_all = (q * jnp.exp(b - bm)).astype(bf16)
        ki_all = (k * jnp.exp(bm - b)).astype(bf16)
        qe_all = (q * jnp.exp(b)).astype(bf16)
        ks_all = (k * jnp.exp(bl - b)).astype(bf16)
        e_l = jnp.exp(bl)
        for hd in range(HB):
            kc = slice(hd * DK, (hd + 1) * DK)
            vc = slice(hd * DV, (hd + 1) * DV)
            v = v_ref[rows, vc].astype(bf16)
            p = jnp.where(causal, _dot_nt(qi_all[:, kc], ki_all[:, kc]), 0.0).astype(bf16)
            s0 = states[hd]
            st_ref[c, hd] = s0
            o = _dot(p, v) + _dot_nt(qe_all[:, kc], s0.astype(bf16))
            states[hd] = s0 * e_l[:, kc] + _dot_tn(v, ks_all[:, kc])
            o_ref[rows, vc] = o
            ro = lax.rsqrt(jnp.mean(o * o, axis=-1, keepdims=True) + EPS)
            zb = zb_ref[rows, vc]
            ob_ref[rows, vc] = (o * ro * gg_ref[...] * (zb * _sigmoid(zb))).astype(bf16)
    for hd in range(HB):
        state[hd] = states[hd]


def _gla_bwd(proj, lr, o, states, dob, dproj, w_gate, b_gate, gla_g, riders=(), sends_what=(), tm=256):
    T = proj.shape[0]
    cpb = tm // CB
    nb = T // tm
    n = len(riders)

    def body(q_ref, k_ref, v_ref, zb_ref, lr_ref, o_ref, st_ref, dob_ref, dp_in, wg_ref, bg_ref, gg_ref, *rest):
        srcs, (dp_ref, dlr_ref, dwg_ref, dbg_ref, dgg_ref), dsts = rest[:n], rest[n:n + 5], rest[n + 5:2 * n + 5]
        dstate, la_s, dlogit_s, tail_s, *sems = rest[2 * n + 5:]
        del dp_in
        step = pl.program_id(0)

        @pl.when(step == 0)
        def _():
            dstate[...] = jnp.zeros_like(dstate)
            dwg_ref[...] = jnp.zeros_like(dwg_ref)
            dbg_ref[...] = jnp.zeros_like(dbg_ref)
            dgg_ref[...] = jnp.zeros_like(dgg_ref)
            for cp in _exchange_copies(srcs, dsts, sends_what, sems) if n else []:
                cp.start()

        lr_v = lr_ref[...]
        logit, la = _log_alpha(lr_v, wg_ref[...], bg_ref[...])
        la_s[...] = _dot_exact(_chunk_tri(tm, upper=False), la)
        causal = _tri(CB, upper=False) > 0.5
        gg = gg_ref[...]
        dgg_acc = jnp.zeros((1, DV), f32)
        dstates = [dstate[hd] for hd in range(HB)]
        for c in reversed(range(cpb)):
            rows = slice(c * CB, (c + 1) * CB)
            b = la_s[rows, :]
            bl = b[CB - 1:CB, :]
            bm = b[CB // 2 - 1:CB // 2, :]
            eb_all, eqm_all, ekm_all = jnp.exp(b), jnp.exp(b - bm), jnp.exp(bm - b)
            eks_all, el_all = jnp.exp(bl - b), jnp.exp(bl)
            q_all = q_ref[rows, :] * QSCALE
            k_all = k_ref[rows, :]
            qi_all = (q_all * eqm_all).astype(bf16)
            ki_all = (k_all * ekm_all).astype(bf16)
            qe_all = (q_all * eb_all).astype(bf16)
            ksf_all = k_all * eks_all
            ks_all = ksf_all.astype(bf16)
            for hd in range(HB):
                kc = slice(hd * DK, (hd + 1) * DK)
                vc = slice(hd * DV, (hd + 1) * DV)
                o_h = o_ref[rows, vc]
                ro = lax.rsqrt(jnp.mean(o_h * o_h, axis=-1, keepdims=True) + EPS)
                ohat = o_h * ro
                zb = zb_ref[rows, vc]
                sg = _sigmoid(zb)
                dob_h = dob_ref[rows, vc]
                don = dob_h * (zb * sg)
                dp_ref[rows, 2 * D + hd * DV:2 * D + (hd + 1) * DV] = (
                    dob_h * ohat * gg * (sg * (1.0 + zb * (1.0 - sg)))).astype(bf16)
                dgg_acc = dgg_acc + jnp.sum(don * ohat, axis=0, keepdims=True)
                dohat = don * gg
                do = (ro * (dohat - ohat * jnp.mean(dohat * ohat, axis=-1, keepdims=True))).astype(bf16)
                e_b, e_qm, e_km, e_ks, e_l = eb_all[:, kc], eqm_all[:, kc], ekm_all[:, kc], eks_all[:, kc], el_all[:, kc]
                q, k, ks_f = q_all[:, kc], k_all[:, kc], ksf_all[:, kc]
                qi, ki, qe, ks = qi_all[:, kc], ki_all[:, kc], qe_all[:, kc], ks_all[:, kc]
                v = v_ref[rows, vc].astype(bf16)
                p = jnp.where(causal, _dot_nt(qi, ki), 0.0).astype(bf16)
                s0 = st_ref[c, hd]
                ds = dstates[hd]
                ds_b = ds.astype(bf16)
                dv = _dot_tn(p, do) + _dot_nt(ks, ds_b)
                dpm = jnp.where(causal, _dot_nt(do, v), 0.0).astype(bf16)
                dqi = _dot(dpm, ki)
                dki = _dot_tn(dpm, qi)
                dqe = _dot(do, s0.astype(bf16))
                dks = _dot(v, ds_b)
                dq_s = dqi * e_qm + dqe * e_b
                dk = dki * e_km + dks * e_ks
                tail = (jnp.sum(dks * ks_f, axis=0, keepdims=True)
                        + e_l * jnp.sum(ds * s0, axis=0, keepdims=True))
                dstates[hd] = _dot_tn(do, qe) + ds * e_l
                dp_ref[rows, kc] = (dq_s * QSCALE).astype(bf16)
                dp_ref[rows, KEYB + hd * DK:KEYB + (hd + 1) * DK] = dk.astype(bf16)
                dp_ref[rows, D + hd * DV:D + (hd + 1) * DV] = dv.astype(bf16)
                dlogit_s[rows, kc] = dq_s * q - dk * k
                tail_s[rows, kc] = jnp.broadcast_to(tail, (CB, DK))
        for hd in range(HB):
            dstate[hd] = dstates[hd]
        dgg_ref[...] += dgg_acc
        dg = _dot_exact(_chunk_tri(tm, upper=True), dlogit_s[...]) + tail_s[...]
        dlogit = dg * (1.0 / GATE_NORM) * _sigmoid(-logit)
        dbg_ref[...] += jnp.sum(dlogit, axis=0, keepdims=True)
        dlb = dlogit.astype(bf16)
        dlr_ref[...] = _dot_nt(dlb, wg_ref[...].astype(bf16)).astype(bf16)
        dwg_ref[...] += _dot_tn(lr_v.astype(bf16), dlb)

        if n:
            @pl.when(step == nb - 1)
            def _():
                for cp in _exchange_copies(srcs, dsts, sends_what, sems):
                    cp.wait()

    def rev(cidx):
        return lambda i, c=cidx: (nb - 1 - i, c)

    return pl.pallas_call(
        body,
        grid=(nb,),
        in_specs=[pl.BlockSpec((tm, KEYB), rev(COL_Q)),
                  pl.BlockSpec((tm, KEYB), rev(COL_K)),
                  pl.BlockSpec((tm, D), rev(COL_VB)),
                  pl.BlockSpec((tm, D), rev(COL_ZB)),
                  pl.BlockSpec((tm, LRP), rev(0)),
                  pl.BlockSpec((tm, D), rev(0)),
                  pl.BlockSpec((cpb, HB, DV, DK), lambda i: (nb - 1 - i, 0, 0, 0)),
                  pl.BlockSpec((tm, D), rev(0)),
                  pl.BlockSpec(memory_space=pl.ANY),
                  _resident((LRP, KEYB)), _resident((1, KEYB)), _resident((1, DV))] + [_ANY] * n,
        out_specs=[pl.BlockSpec((tm, 3 * D), rev(1)),
                   pl.BlockSpec((tm, LRP), rev(0)),
                   _resident((LRP, KEYB)), _resident((1, KEYB)), _resident((1, DV))] + [_ANY] * n,
        out_shape=[jax.ShapeDtypeStruct(dproj.shape, dproj.dtype),
                   jax.ShapeDtypeStruct((T, LRP), bf16),
                   jax.ShapeDtypeStruct((LRP, KEYB), f32), jax.ShapeDtypeStruct((1, KEYB), f32),
                   jax.ShapeDtypeStruct((1, DV), f32)] + _exchange_shapes(riders, sends_what),
        scratch_shapes=[pltpu.VMEM((HB, DV, DK), f32)] + [pltpu.VMEM((tm, KEYB), f32)] * 3
        + (_exchange_sems(n) if n else []),
        input_output_aliases={8: 0},
        compiler_params=_params(("arbitrary",)),
        name="gla_bwd",
    )(proj, proj, proj, proj, lr, o, states, dob, dproj, w_gate, b_gate, gla_g, *riders)


def _merge_fwd_bwd(x, tgt, proj, a, ob, w_a, w_b, w_o, g_f, tm=256):
    T = x.shape[0]

    def body(x_ref, t_ref, gt_ref, a_ref, ob_ref, wa_ref, wb_ref, wo_ref, gf_ref,
             dp_ref, dy_ref, da_ref, dob_ref, dwa_ref, dwb_ref, dwo_ref, dgf_ref, loss_ref):
        @pl.when(pl.program_id(0) == 0)
        def _():
            dwa_ref[...] = jnp.zeros_like(dwa_ref)
            dwb_ref[...] = jnp.zeros_like(dwb_ref)
            dwo_ref[...] = jnp.zeros_like(dwo_ref)
            dgf_ref[...] = jnp.zeros_like(dgf_ref)
            loss_ref[...] = jnp.zeros_like(loss_ref)

        ga = _sigmoid(gt_ref[:, :D])
        gb = _sigmoid(gt_ref[:, D:])
        a_v = a_ref[...]
        ob_v = ob_ref[...]
        pa = _dot(a_v, wa_ref[...])
        pb = _dot(ob_v, wb_ref[...])
        mb = (ga * pa + gb * pb).astype(bf16)
        y = x_ref[...] + _dot(mb, wo_ref[...])
        r1 = lax.rsqrt(jnp.mean(y * y, axis=-1, keepdims=True) + EPS)
        yhat = y * r1
        gf = gf_ref[...]
        err = yhat * gf - t_ref[...]
        loss_ref[...] += jnp.sum(err * err, axis=0, keepdims=True) * (0.5 / D)
        dout = err * (1.0 / D)
        dgf_ref[...] += jnp.sum(dout * yhat, axis=0, keepdims=True)
        dyn = dout * gf
        dy = r1 * (dyn - yhat * jnp.mean(dyn * yhat, axis=-1, keepdims=True))
        dy_ref[...] = dy
        dyb = dy.astype(bf16)
        dwo_ref[...] += _dot_tn(mb, dyb)
        dm = _dot_nt(dyb, wo_ref[...])
        dpa = (dm * ga).astype(bf16)
        dpb = (dm * gb).astype(bf16)
        dp_ref[:, :D] = (dm * pa * ga * (1.0 - ga)).astype(bf16)
        dp_ref[:, D:] = (dm * pb * gb * (1.0 - gb)).astype(bf16)
        dwa_ref[...] += _dot_tn(a_v, dpa)
        dwb_ref[...] += _dot_tn(ob_v, dpb)
        da_ref[...] = _dot_nt(dpa, wa_ref[...])
        dob_ref[...] = _dot_nt(dpb, wb_ref[...])

    row = lambda: pl.BlockSpec((tm, D), lambda i: (i, 0))
    return pl.pallas_call(
        body,
        grid=(T // tm,),
        in_specs=[row(), row(), pl.BlockSpec((tm, 2 * D), lambda i: (i, COL_GATES)), row(), row(),
                  _resident((D, D)), _resident((D, D)), _resident((D, D)), _resident((1, D))],
        out_specs=[pl.BlockSpec((tm, 2 * D), lambda i: (i, COL_GATES)), row(), row(), row(),
                   _resident((D, D)), _resident((D, D)), _resident((D, D)), _resident((1, D)), _resident((1, D))],
        out_shape=[jax.ShapeDtypeStruct((T, NMAIN), bf16),
                   jax.ShapeDtypeStruct((T, D), f32), jax.ShapeDtypeStruct((T, D), f32),
                   jax.ShapeDtypeStruct((T, D), f32),
                   jax.ShapeDtypeStruct((D, D), f32), jax.ShapeDtypeStruct((D, D), f32),
                   jax.ShapeDtypeStruct((D, D), f32),
                   jax.ShapeDtypeStruct((1, D), f32), jax.ShapeDtypeStruct((1, D), f32)],
        compiler_params=_params(("arbitrary",)),
        name="merge_fwd_bwd",
    )(x, tgt, proj, a, ob, w_a, w_b, w_o, g_f)


def _dx_bwd(x, dy, dproj, dlr, g0, w_t, riders=(), sends_what=(), tm=256):
    T = x.shape[0]
    nsteps = T // tm
    n = len(riders)

    def body(x_ref, dy_ref, dp_ref, dl_ref, g_ref, w_ref, *rest):
        srcs, (dx_ref, dg_ref), dsts, sems = rest[:n], rest[n:n + 2], rest[n + 2:2 * n + 2], rest[2 * n + 2:]

        @pl.when(pl.program_id(0) == 0)
        def _():
            dg_ref[...] = jnp.zeros_like(dg_ref)
            for cp in _exchange_copies(srcs, dsts, sends_what, sems) if n else []:
                cp.start()

        xv = x_ref[...]
        r = lax.rsqrt(jnp.mean(xv * xv, axis=-1, keepdims=True) + EPS)
        xhat = xv * r
        dh = (_dot(dp_ref[:, :LR_COL], w_ref[:LR_COL, :]) + _dot(dp_ref[:, LR_COL:], w_ref[LR_COL + RANK:, :])
              + _dot(dl_ref[...], w_ref[LR_ROWS, :]))
        dg_ref[...] += jnp.sum(dh * xhat, axis=0, keepdims=True)
        t = dh * g_ref[...]
        dx_ref[...] = dy_ref[...] + r * (t - xhat * jnp.mean(t * xhat, axis=-1, keepdims=True))

        if n:
            @pl.when(pl.program_id(0) == nsteps - 1)
            def _():
                for cp in _exchange_copies(srcs, dsts, sends_what, sems):
                    cp.wait()

    row = lambda: pl.BlockSpec((tm, D), lambda i: (i, 0))
    return pl.pallas_call(
        body,
        grid=(nsteps,),
        in_specs=[row(), row(), pl.BlockSpec((tm, NMAIN), lambda i: (i, 0)),
                  pl.BlockSpec((tm, LRP), lambda i: (i, 0)),
                  _resident((1, D)), _resident((NMAIN + RANK, D))] + [_ANY] * n,
        out_specs=[row(), _resident((1, D))] + [_ANY] * n,
        out_shape=[jax.ShapeDtypeStruct((T, D), f32), jax.ShapeDtypeStruct((1, D), f32)]
        + _exchange_shapes(riders, sends_what),
        scratch_shapes=_exchange_sems(n) if n else [],
        compiler_params=_params(("arbitrary",)),
        name="dx_bwd",
    )(x, dy, dproj, dlr, g0, w_t, *riders)


def _dw_in(h, dproj, dlr, pair_dtype=None, tm=2048, tn=1024):
    T = h.shape[0]
    tm = min(tm, T)
    nj, nk = NMAIN // tn, T // tm
    lr_tile = LR_COL // tn
    pair = pair_dtype is not None
    hd = D // 2

    def body(h_ref, dp_ref, dl_ref, out_ref, acc, lr_acc, sems, lr_sem, *more):
        j, k = pl.program_id(0), pl.program_id(1)
        slot = j % 2

        def tile_row(jj):
            return pl.multiple_of(jj * tn + jnp.where(jj >= lr_tile, RANK, 0), 8)

        if pair:
            land, lr_land, part_buf, lr_part, swap_send, swap_recv = more
            c = lax.axis_index("c")
            mine = pl.ds(pl.multiple_of(c * hd, 128), hd)
            other = pl.ds(pl.multiple_of((1 - c) * hd, 128), hd)

            def tile_swap(jj, s):
                return _remote(acc.at[s, :, other], land.at[jj], swap_send.at[jj], swap_recv.at[jj], _sibling())

            def lr_swap():
                return _remote(lr_acc.at[pl.ds(0, RANK), other], lr_land, swap_send.at[nj], swap_recv.at[nj], _sibling())

            def tile_out(jj, s):
                return pltpu.make_async_copy(part_buf.at[s], out_ref.at[pl.ds(tile_row(jj), tn)], sems.at[s])

            def finish_tile(jj, s):
                tile_swap(jj, s).wait()
                part_buf[s] = (acc[s, :, mine] + land[jj]).astype(pair_dtype)
                tile_out(jj, s).start()

            lr_out = pltpu.make_async_copy(lr_part, out_ref.at[pl.ds(LR_COL, RANK)], lr_sem)
        else:
            def tile_out(jj, s):
                return pltpu.make_async_copy(acc.at[s], out_ref.at[pl.ds(tile_row(jj), tn)], sems.at[s])

            lr_out = pltpu.make_async_copy(lr_acc.at[pl.ds(0, RANK)], out_ref.at[pl.ds(LR_COL, RANK)], lr_sem)

        @pl.when(j == 0)
        def _():
            @pl.when(k == 0)
            def _():
                lr_acc[...] = jnp.zeros_like(lr_acc)

            lr_acc[...] += _dot_tn(dl_ref[...], h_ref[...])

            @pl.when(k == nk - 1)
            def _():
                if pair:
                    lr_swap().start()
                else:
                    lr_out.start()

        @pl.when(k == 0)
        def _():
            acc[slot] = jnp.zeros((tn, D), f32)

        acc[slot] += _dot_tn(dp_ref[...], h_ref[...])

        @pl.when(k == nk - 1)
        def _():
            if pair:
                tile_swap(j, slot).start()

                @pl.when(j >= 3)
                def _():
                    tile_out(j - 3, 1 - slot).wait()

                @pl.when(j >= 1)
                def _():
                    finish_tile(j - 1, 1 - slot)

                @pl.when(j == nj - 1)
                def _():
                    tile_out(j - 2, slot).wait()
                    finish_tile(j, slot)
                    lr_swap().wait()
                    lr_part[...] = (lr_acc[0:RANK, mine] + lr_land[...]).astype(pair_dtype)
                    lr_out.start()
                    tile_out(j - 1, 1 - slot).wait()
                    tile_out(j, slot).wait()
                    lr_out.wait()
            else:
                tile_out(j, slot).start()

                @pl.when(j > 0)
                def _():
                    tile_out(j - 1, 1 - slot).wait()

                @pl.when(j == nj - 1)
                def _():
                    tile_out(j, slot).wait()
                    lr_out.wait()

    pair_scratch = [pltpu.VMEM((nj, tn, hd), f32), pltpu.VMEM((RANK, hd), f32), pltpu.VMEM((2, tn, hd), pair_dtype),
                    pltpu.VMEM((RANK, hd), pair_dtype)] + [pltpu.SemaphoreType.DMA((nj + 1,))] * 2 if pair else []
    return pl.pallas_call(
        body,
        grid=(nj, nk),
        in_specs=[pl.BlockSpec((tm, D), lambda j, k: (k, 0)), pl.BlockSpec((tm, tn), lambda j, k: (k, j)),
                  pl.BlockSpec((tm, LRP), lambda j, k: (k, 0))],
        out_specs=_ANY,
        out_shape=jax.ShapeDtypeStruct((NMAIN + RANK, hd), pair_dtype) if pair
        else jax.ShapeDtypeStruct((NMAIN + RANK, D), f32),
        scratch_shapes=[pltpu.VMEM((2, tn, D), f32), pltpu.VMEM((LRP, D), f32),
                        pltpu.SemaphoreType.DMA((2,)), pltpu.SemaphoreType.DMA] + pair_scratch,
        compiler_params=_params(("arbitrary", "arbitrary")),
        name="dw_in",
    )(h, dproj, dlr)


MESH = pl.DeviceIdType.MESH


def _place():
    x, y, c = lax.axis_index("x"), lax.axis_index("y"), lax.axis_index("c")
    others = [(1 - x, y), (x, 1 - y), (1 - x, 1 - y)]
    return x, y, c, 2 * x + y, others


def _sibling():
    return lax.axis_index("x"), lax.axis_index("y"), 1 - lax.axis_index("c")


def _remote(src, dst, send_sem, recv_sem, to):
    return pltpu.make_async_remote_copy(src_ref=src, dst_ref=dst, send_sem=send_sem, recv_sem=recv_sem,
                                        device_id=to, device_id_type=MESH)


def _half(ref, e, by_columns):
    if not by_columns:
        return ref.at[e]
    hw = ref.shape[-1] // 2
    return ref.at[:, pl.ds(pl.multiple_of(e * hw, 128), hw)]


RELAY_ROWS = 1024


def _gather_win(shard, small):
    h, w = shard.shape
    part_a, part_b = pl.ds(0, RELAY_ROWS), pl.ds(RELAY_ROWS, h - RELAY_ROWS)

    def body(src, small_src, dst, small_dst, send_sems, recv_sems, relay_send, relay_recv, pass_send, pass_recv, own_sems,
             small_send, small_recv, small_own, stage, small_stage):
        x, y, c, me, others = _place()
        (to_x, to_y, _), sibling = others, (x, y, 1 - c)
        j_x, j_y, j_d = (2 * cx + cy for cx, cy in others)
        cols = pl.ds(pl.multiple_of(c * (w // 2), 128), w // 2)
        theirs = pl.ds(pl.multiple_of((1 - c) * (w // 2), 128), w // 2)

        def mine(j, rows=pl.ds(0, h)):
            return dst.at[j, rows, cols]

        small_sends = [_remote(small_src, small_dst.at[me], small_send.at[k], small_recv.at[k], (*to, c))
                       for k, to in enumerate(others)]
        for cp in small_sends:
            cp.start()
        small_in = pltpu.make_async_copy(small_src, small_stage, small_own.at[0])
        small_out = pltpu.make_async_copy(small_stage, small_dst.at[me], small_own.at[1])
        small_in.start()

        to_stage = pltpu.make_async_copy(src, stage, own_sems.at[0])
        to_slot = pltpu.make_async_copy(stage, dst.at[me], own_sems.at[1])
        sends = [_remote(src.at[:, cols], mine(me), send_sems.at[k], recv_sems.at[k], (*to, c))
                 for k, to in enumerate((to_x, to_y))]
        for cp in sends:
            cp.start()
        to_stage.start()
        relays = [_remote(mine(j_x, part_a), mine(j_x, part_a), relay_send.at[0], relay_recv.at[0], (*to_y, c)),
                  _remote(mine(j_y, part_b), mine(j_y, part_b), relay_send.at[1], relay_recv.at[1], (*to_x, c))]
        landed = [mine(j_x), mine(j_y), mine(j_d, part_a), mine(j_d, part_b)]
        passes = [_remote(place, place, pass_send.at[k], pass_recv.at[k], sibling) for k, place in enumerate(landed)]
        for k in range(2):
            _remote(src.at[:, cols], landed[k], send_sems.at[k], recv_sems.at[k], sibling).wait_recv()
            relays[k].start()
            passes[k].start()
        to_stage.wait()
        to_slot.start()
        for k in range(2):
            _remote(landed[2 + k], landed[2 + k], relay_send.at[k], relay_recv.at[k], sibling).wait_recv()
            passes[2 + k].start()
        for k, place in enumerate([(j_x, pl.ds(0, h)), (j_y, pl.ds(0, h)), (j_d, part_a), (j_d, part_b)]):
            got = dst.at[place[0], place[1], theirs]
            _remote(got, got, pass_send.at[k], pass_recv.at[k], sibling).wait_recv()
        for cp in sends + relays + passes:
            cp.wait_send()
        to_slot.wait()
        small_in.wait()
        small_out.start()
        for k, (cx, cy) in enumerate(others):
            _remote(small_src, small_dst.at[2 * cx + cy], small_send.at[k], small_recv.at[k], sibling).wait_recv()
        for cp in small_sends:
            cp.wait_send()
        small_out.wait()

    return pl.pallas_call(
        body,
        in_specs=[_ANY] * 2,
        out_specs=[_ANY] * 2,
        out_shape=[jax.ShapeDtypeStruct((NCHIP, h, w), shard.dtype),
                   jax.ShapeDtypeStruct((NCHIP,) + small.shape, small.dtype)],
        scratch_shapes=[pltpu.SemaphoreType.DMA((2,))] * 4 + [pltpu.SemaphoreType.DMA((4,))] * 2
        + [pltpu.SemaphoreType.DMA((2,))] + [pltpu.SemaphoreType.DMA((3,))] * 2 + [pltpu.SemaphoreType.DMA((2,))]
        + [pltpu.VMEM((h, w), shard.dtype), pltpu.VMEM(small.shape, small.dtype)],
        compiler_params=pltpu.CompilerParams(vmem_limit_bytes=VMEM_LIMIT),
        name="gather_win",
    )(shard, small)


def _gathered_shapes(shards):
    return [jax.ShapeDtypeStruct((NCHIP,) + s.shape, s.dtype) for s in shards]


def _gather_scratch(shards):
    n = len(shards)
    return ([pltpu.SemaphoreType.DMA((3, n))] * 4 + [pltpu.SemaphoreType.DMA((2, n))]
            + [pltpu.VMEM(s.shape, s.dtype) for s in shards])


def _own_to_stage(srcs, scratch):
    own_sems, stages = scratch[4], scratch[5:]
    return [pltpu.make_async_copy(srcs[a], stages[a], own_sems.at[0, a]) for a in range(len(srcs))]


def _own_to_slot(dsts, scratch):
    own_sems, stages = scratch[4], scratch[5:]
    me = _place()[3]
    return [pltpu.make_async_copy(stages[a], dsts[a].at[me], own_sems.at[1, a]) for a in range(len(dsts))]


def _gather_copies(srcs, dsts, by_columns, sems, sends_only):
    n = len(srcs)
    send_sems, recv_sems, pass_send, pass_recv = sems[:4]
    x, y, c, me, others = _place()
    sibling = (x, y, 1 - c)

    def src(a, e):
        return _half(srcs[a], e, by_columns[a])

    def dst(a, j, e):
        return _half(dsts[a].at[j], e, by_columns[a])

    sends, arrivals, passes, passed = [], [], [], []
    for k, (cx, cy) in enumerate(others):
        j = 2 * cx + cy
        for a in range(n):
            sends.append(_remote(src(a, c), dst(a, me, c), send_sems.at[k, a], recv_sems.at[k, a], (cx, cy, c)))
            if not sends_only:
                arrivals.append(_remote(src(a, c), dst(a, j, c), send_sems.at[k, a], recv_sems.at[k, a], (cx, cy, c)))
                passes.append(_remote(dst(a, j, c), dst(a, j, c), pass_send.at[k, a], pass_recv.at[k, a], sibling))
                passed.append(_remote(src(a, c), dst(a, j, 1 - c), pass_send.at[k, a], pass_recv.at[k, a], sibling))
    return sends, arrivals, passes, passed


def _gather_start(srcs, dsts, by_columns, scratch):
    for cp in _gather_copies(srcs, dsts, by_columns, scratch, sends_only=True)[0]:
        cp.start()
    for cp in _own_to_stage(srcs, scratch):
        cp.start()


def _gather_finish(srcs, dsts, by_columns, scratch):
    for cp in _own_to_stage(srcs, scratch):
        cp.wait()
    own = _own_to_slot(dsts, scratch)
    for cp in own:
        cp.start()
    sends, arrivals, passes, passed = _gather_copies(srcs, dsts, by_columns, scratch, sends_only=False)
    for arrival, cp in zip(arrivals, passes):
        arrival.wait_recv()
        cp.start()
    for arrival in passed:
        arrival.wait_recv()
    for cp in sends + passes:
        cp.wait_send()
    for cp in own:
        cp.wait()


HALF_FIRST, CHIP_FIRST, BY_COLUMNS = "half_first", "chip_first", "by_columns"


def _sibling_halves(bufs, kinds):
    n = len(bufs)

    def body(*refs):
        cps = _halves_copies(refs[:n], refs[n:2 * n], kinds, refs[2 * n:])
        for cp in cps:
            cp.start()
        for cp in cps:
            cp.wait()

    return pl.pallas_call(
        body,
        in_specs=[_ANY] * n,
        out_specs=[_ANY] * n,
        out_shape=_halves_shapes(bufs, kinds),
        scratch_shapes=_halves_sems(bufs),
        name="sibling_halves",
    )(*bufs)


def _halves_shapes(bufs, kinds):
    def landed(b, kind):
        if kind == HALF_FIRST:
            return b.shape[1:]
        if kind == CHIP_FIRST:
            return (b.shape[0],) + b.shape[2:]
        return b.shape[:2] + (b.shape[2] // 2,)

    return [jax.ShapeDtypeStruct(landed(b, kind), b.dtype) for b, kind in zip(bufs, kinds)]


def _halves_sems(bufs):
    return [pltpu.SemaphoreType.DMA((len(bufs), NCHIP))] * 2


def _halves_copies(srcs, dsts, kinds, sems):
    send_sems, recv_sems = sems
    x, y, c, _, _ = _place()
    cps = []
    for a, kind in enumerate(kinds):
        if kind == HALF_FIRST:
            cps.append(_remote(srcs[a].at[1 - c], dsts[a], send_sems.at[a, 0], recv_sems.at[a, 0], (x, y, 1 - c)))
        else:
            cps += [_remote(_half(srcs[a].at[j], 1 - c, kind == BY_COLUMNS), dsts[a].at[j],
                            send_sems.at[a, j], recv_sems.at[a, j], (x, y, 1 - c)) for j in range(srcs[a].shape[0])]
    return cps


TO_ITS_CHIP, TO_EVERY_CHIP, ROWS_TO_ITS_CHIP = "to_its_chip", "to_every_chip", "rows_to_its_chip"
PIECE_STEP = 2048
PIECE_ROWS = 2064


def _chip_exchange(parts, sends_what):
    n = len(parts)

    def body(*refs):
        cps = _exchange_copies(refs[:n], refs[n:2 * n], sends_what, refs[2 * n:])
        for cp in cps:
            cp.start()
        for cp in cps:
            cp.wait()

    return pl.pallas_call(
        body,
        in_specs=[_ANY] * n,
        out_specs=[_ANY] * n,
        out_shape=_exchange_shapes(parts, sends_what),
        scratch_shapes=_exchange_sems(n),
        name="chip_exchange",
    )(*parts)


def _exchange_shapes(parts, sends_what):
    def landed(p, what):
        return (3, PIECE_ROWS, p.shape[1]) if what == ROWS_TO_ITS_CHIP else (3,) + p.shape[1:]

    return [jax.ShapeDtypeStruct(landed(p, what), p.dtype) for p, what in zip(parts, sends_what)]


def _exchange_sems(n):
    return [pltpu.SemaphoreType.DMA((3, n))] * 2


def _exchange_copies(srcs, dsts, sends_what, sems):
    send_sems, recv_sems = sems
    x, y, c, me, others = _place()

    def part(a, j):
        if sends_what[a] == ROWS_TO_ITS_CHIP:
            return srcs[a].at[pl.ds(pl.multiple_of(j * PIECE_STEP, PIECE_STEP), PIECE_ROWS)]
        return srcs[a].at[j if sends_what[a] == TO_ITS_CHIP else 0]

    return [_remote(part(a, 2 * cx + cy), dsts[a].at[k], send_sems.at[k, a], recv_sems.at[k, a], (cx, cy, c))
            for k, (cx, cy) in enumerate(others) for a in range(len(srcs))]


def _sibling_swap(halves):
    n = len(halves)

    def body(*refs):
        srcs, dsts = refs[:n], refs[n:2 * n]
        send_sems, recv_sems = refs[2 * n:]
        x, y, c, _, _ = _place()
        cps = [_remote(srcs[a], dsts[a], send_sems.at[a], recv_sems.at[a], (x, y, 1 - c)) for a in range(n)]
        for cp in cps:
            cp.start()
        for cp in cps:
            cp.wait()

    return pl.pallas_call(
        body,
        in_specs=[_ANY] * n,
        out_specs=[_ANY] * n,
        out_shape=[jax.ShapeDtypeStruct(s.shape, s.dtype) for s in halves],
        scratch_shapes=[pltpu.SemaphoreType.DMA((n,))] * 2,
        name="sibling_swap",
    )(*halves)


def _tile(h, w, operands=5):
    if h % 128 == 0:
        return 128, w
    budget = VMEM_LIMIT * 3 // 4 // (2 * operands * 4)
    tw = w
    while h * tw > budget and tw % 256 == 0:
        tw //= 2
    return h, tw


def _pair_sum(place, bufs, kind, gots, out_dtype):
    m = len(bufs)
    nj, h, w = gots[0].shape
    th, tw = _tile(h, w, operands=3 * m)
    nq = w // tw

    def body(p_ref, *refs):
        del p_ref
        for a_ref, b_ref, o_ref in zip(refs[:m], refs[m:2 * m], refs[2 * m:]):
            o_ref[...] = (a_ref[...] + b_ref[...]).astype(out_dtype)

    if kind == HALF_FIRST:
        mine = pl.BlockSpec((None, None, th, tw), lambda j, r, q, p: (p[0], j, r, q))
    elif kind == CHIP_FIRST:
        mine = pl.BlockSpec((None, None, th, tw), lambda j, r, q, p: (j, p[0], r, q))
    else:
        mine = pl.BlockSpec((None, th, tw), lambda j, r, q, p: (j, r, p[0] * nq + q))
    landed = pl.BlockSpec((None, th, tw), lambda j, r, q, p: (j, r, q))
    return pl.pallas_call(
        body,
        grid_spec=pltpu.PrefetchScalarGridSpec(
            num_scalar_prefetch=1,
            grid=(nj, h // th, w // tw),
            in_specs=[mine] * m + [landed] * m,
            out_specs=[landed] * m,
        ),
        out_shape=[jax.ShapeDtypeStruct((nj, h, w), out_dtype)] * m,
        compiler_params=_params(("parallel", "parallel", "parallel")),
        name="pair_sum",
    )(place, *bufs, *gots)


def _chip_sum(place, parts, slots):
    m = len(parts)
    nj, h, w = parts[0].shape
    th, tw = _tile(h, w, operands=5 * m)

    def body(p_ref, *refs):
        me = p_ref[1]
        for own_ref, s_ref, o_ref in zip(refs[:m], refs[m:2 * m], refs[2 * m:]):
            own = own_ref[...].astype(f32)
            by_flip = {2: s_ref[0].astype(f32), 1: s_ref[1].astype(f32), 3: s_ref[2].astype(f32)}
            acc = None
            for j in range(NCHIP):
                flip = me ^ j
                term = jnp.where(flip == 0, own,
                                 jnp.where(flip == 2, by_flip[2], jnp.where(flip == 1, by_flip[1], by_flip[3])))
                acc = term if acc is None else acc + term
            o_ref[...] = acc

    return pl.pallas_call(
        body,
        grid_spec=pltpu.PrefetchScalarGridSpec(
            num_scalar_prefetch=1,
            grid=(h // th, w // tw),
            in_specs=[pl.BlockSpec((None, th, tw), lambda r, q, p: (p[1] if nj == NCHIP else 0, r, q))] * m
            + [pl.BlockSpec((3, th, tw), lambda r, q, p: (0, r, q))] * m,
            out_specs=[pl.BlockSpec((th, tw), lambda r, q, p: (r, q))] * m,
        ),
        out_shape=[jax.ShapeDtypeStruct((h, w), f32)] * m,
        compiler_params=_params(("parallel", "parallel")),
        name="chip_sum",
    )(place, *parts, *slots)


def _adamw_math(w, g, m, v):
    nm = ADAM_B1 * m + (1.0 - ADAM_B1) * g
    nv = ADAM_B2 * v + (1.0 - ADAM_B2) * (g * g)
    m_hat = nm / (1.0 - ADAM_B1 ** ADAM_STEP)
    v_hat = nv / (1.0 - ADAM_B2 ** ADAM_STEP)
    return -ADAM_LR * (m_hat / (jnp.sqrt(v_hat) + ADAM_EPS) + ADAM_WD * w), nm, nv


def _adamw(w, g, m, v):
    rows, width = w.shape
    th, tw = _tile(rows, width, operands=7)

    def body(w_ref, g_ref, m_ref, v_ref, d_ref, nm_ref, nv_ref):
        d_ref[...], nm_ref[...], nv_ref[...] = _adamw_math(w_ref[...], g_ref[...], m_ref[...], v_ref[...])

    spec = pl.BlockSpec((th, tw), lambda r, q: (r, q))
    return pl.pallas_call(
        body,
        grid=(rows // th, width // tw),
        in_specs=[spec] * 4,
        out_specs=[spec] * 3,
        out_shape=[jax.ShapeDtypeStruct((rows, width), f32)] * 3,
        compiler_params=_params(("parallel", "parallel")),
        name="adamw",
    )(w, g, m, v)


def _adamw_halves(place, ws, mines, gots, ms, vs, axis):
    k = len(ws)
    rows, width = ws[0].shape
    h, hw = mines[0].shape
    th, tw = _tile(h, hw, operands=10 * k)
    nr, nq = h // th, hw // tw

    def body(p_ref, *refs):
        ins, outs = refs[:5 * k], refs[5 * k:]
        for i in range(k):
            w_ref, a_ref, b_ref, m_ref, v_ref = ins[i::k]
            g_ref, d_ref, nm_ref, nv_ref = outs[i::k]
            g = jnp.where(pl.program_id(0) == p_ref[0], a_ref[...], b_ref[...])
            g_ref[...] = g
            d_ref[...], nm_ref[...], nv_ref[...] = _adamw_math(w_ref[...], g, m_ref[...], v_ref[...])

    if axis == 0:
        full = pl.BlockSpec((th, tw), lambda e, r, q, p: (e * nr + r, q))
    else:
        full = pl.BlockSpec((th, tw), lambda e, r, q, p: (r, e * nq + q))
    half = pl.BlockSpec((th, tw), lambda e, r, q, p: (r, q))
    res = pl.pallas_call(
        body,
        grid_spec=pltpu.PrefetchScalarGridSpec(
            num_scalar_prefetch=1,
            grid=(2, nr, nq),
            in_specs=[full] * k + [half] * (2 * k) + [full] * (2 * k),
            out_specs=[full] * (4 * k),
        ),
        out_shape=[jax.ShapeDtypeStruct((rows, width), f32)] * (4 * k),
        compiler_params=_params(("parallel", "parallel", "parallel")),
        name="adamw_halves",
    )(place, *ws, *mines, *gots, *ms, *vs)
    return [res[i::k] for i in range(k)]


_SMALL = (("norm_g", 8), ("ln_v_g", 8), ("ln_v_b", 8), ("w_spatial", 1024), ("b_spatial", 8), ("b_gate_up", 4),
          ("gla_norm_g", 2), ("final_norm_g", 8), ("w_gate_up", 64), ("loss", 8))
_SMALL_ROWS = 1152


def _pack_rows(arrays, rows):
    flat = jnp.concatenate([a.reshape(-1, 128) for a in arrays], axis=0)
    return jnp.pad(flat, ((0, rows - flat.shape[0]), (0, 0)))


def kernel(x, norm_g, w_in, ln_v_g, ln_v_b, w_spatial, b_spatial, w_gate_up, b_gate_up, gla_norm_g, w_branch_a, w_branch_b, w_out, final_norm_g, loss_target, m_norm_g, m_w_in, m_ln_v_g, m_ln_v_b, m_w_spatial, m_b_spatial, m_w_gate_up, m_b_gate_up, m_gla_norm_g, m_w_branch_a, m_w_branch_b, m_w_out, m_final_norm_g, v_norm_g, v_w_in, v_ln_v_g, v_ln_v_b, v_w_spatial, v_b_spatial, v_w_gate_up, v_b_gate_up, v_gla_norm_g, v_w_branch_a, v_w_branch_b, v_w_out, v_final_norm_g):
    chip = 2 * lax.axis_index("x") + lax.axis_index("y")
    core = lax.axis_index("c")
    place = jnp.stack([core, chip]).astype(jnp.int32)
    mat_names = ("w_branch_a", "w_branch_b", "w_out")

    wt_shard = jnp.transpose(w_in[0]).astype(bf16)
    mats = [w[0].astype(bf16).reshape(2, D // NCHIP // 2, D) for w in (w_branch_a, w_branch_b, w_out)]
    gate_sh = w_gate_up[0].reshape(2, RANK // 2, 128)
    g_win, g_gate = _gather_win(wt_shard, gate_sh)
    w_t = g_win.reshape(NCHIP * WIN_SHARD, D)
    w_gate = jnp.transpose(g_gate.reshape(NCHIP, RANK, 128), (1, 0, 2)).reshape(RANK, KEYB)
    w_gate = jnp.pad(w_gate, ((0, LRP - RANK), (0, 0)))
    b_sb = jnp.broadcast_to(b_spatial[0][:, :, None], (HA, CA, GA))
    xs, tgt = x[0], loss_target[0]

    proj, lr, h, a, o, ob, states, g_a, g_b, g_o = _forward(xs, norm_g, w_t, ln_v_g, ln_v_b, w_spatial[0], b_sb, w_gate,
                                                           b_gate_up, gla_norm_g, riders=mats)
    w_a, w_b, w_o = (g.reshape(D, D) for g in (g_a, g_b, g_o))
    dproj, dy, da, dob, dwa, dwb, dwo, dgf, loss_cols = _merge_fwd_bwd(xs, tgt, proj, a, ob, w_a, w_b, w_o,
                                                                       final_norm_g.reshape(1, D))
    b_mats = [t.reshape(NCHIP, 2, D // NCHIP // 2, D) for t in (dwa, dwb, dwo)]
    dproj, dws, dbs, dlg, dlb, *got_mats = _mixer_a_bwd(proj, da, dproj, ln_v_g, ln_v_b, w_spatial[0], b_sb,
                                                        riders=b_mats, kinds=[CHIP_FIRST] * 3)
    part_mats = _pair_sum(place, b_mats, CHIP_FIRST, got_mats, bf16)
    dproj, dlr, dwg, dbg, dgg, *slots_mats = _gla_bwd(proj, lr, o, states, dob, dproj, w_gate, b_gate_up, gla_norm_g,
                                                      riders=part_mats, sends_what=[TO_ITS_CHIP] * 3)
    part_win = _dw_in(h, dproj, dlr, pair_dtype=bf16)
    dx, dg0, slots_win = _dx_bwd(xs, dy, dproj, dlr, norm_g, w_t, riders=[part_win], sends_what=[ROWS_TO_ITS_CHIP])
    slots_big = [slots_win] + slots_mats
    small = _pack_rows([dg0, dlg, dlb, dws, dbs[:, :, 0], dbg, dgg, dgf, dwg[:RANK], loss_cols], _SMALL_ROWS)
    b_small = small.reshape(2, 1, _SMALL_ROWS // 2, 128)
    (got_small,) = _sibling_halves([b_small], [HALF_FIRST])
    (part_small,) = _pair_sum(place, [b_small], HALF_FIRST, [got_small], f32)
    (slots_small,) = _chip_exchange([part_small], [TO_EVERY_CHIP])
    own_win = lax.dynamic_slice_in_dim(part_win, chip * PIECE_STEP, PIECE_ROWS, axis=0)[None]
    mine = [*_chip_sum(place, [own_win], slots_big[:1]), *_chip_sum(place, part_mats, slots_big[1:]),
            *_chip_sum(place, [part_small], [slots_small])]
    theirs = list(_sibling_swap(mine))
    mine[0], theirs[0] = (lax.dynamic_slice_in_dim(t, (WIN_SHARD - PIECE_STEP) * chip, WIN_SHARD, axis=0)
                          for t in (mine[0], theirs[0]))

    g_small = jnp.where(core == 0, jnp.concatenate([mine[4], theirs[4]], axis=0),
                        jnp.concatenate([theirs[4], mine[4]], axis=0))
    grads = {}
    row = 0
    for name, rows in _SMALL:
        grads[name] = g_small[row:row + rows]
        row += rows
    loss = jnp.sum(grads["loss"])
    dwg_full = grads["w_gate_up"].reshape(RANK, KEYB)
    grads["w_gate_up"] = lax.dynamic_slice_in_dim(dwg_full, chip * 128, 128, axis=1)

    weights = dict(norm_g=norm_g, w_in=w_in, ln_v_g=ln_v_g, ln_v_b=ln_v_b, w_spatial=w_spatial, b_spatial=b_spatial,
                   w_gate_up=w_gate_up, b_gate_up=b_gate_up, gla_norm_g=gla_norm_g, w_branch_a=w_branch_a,
                   w_branch_b=w_branch_b, w_out=w_out, final_norm_g=final_norm_g)
    m_in = dict(norm_g=m_norm_g, w_in=m_w_in, ln_v_g=m_ln_v_g, ln_v_b=m_ln_v_b, w_spatial=m_w_spatial,
                b_spatial=m_b_spatial, w_gate_up=m_w_gate_up, b_gate_up=m_b_gate_up, gla_norm_g=m_gla_norm_g,
                w_branch_a=m_w_branch_a, w_branch_b=m_w_branch_b, w_out=m_w_out, final_norm_g=m_final_norm_g)
    v_in = dict(norm_g=v_norm_g, w_in=v_w_in, ln_v_g=v_ln_v_g, ln_v_b=v_ln_v_b, w_spatial=v_w_spatial,
                b_spatial=v_b_spatial, w_gate_up=v_w_gate_up, b_gate_up=v_b_gate_up, gla_norm_g=v_gla_norm_g,
                w_branch_a=v_w_branch_a, w_branch_b=v_w_branch_b, w_out=v_w_out, final_norm_g=v_final_norm_g)
    names = list(weights)
    small_names = [n for n in names if n != "w_in" and n not in mat_names]
    out_g, out_d, out_m, out_v = {}, {}, {}, {}
    (res,) = _adamw_halves(place, [jnp.transpose(w_in[0])], mine[:1], theirs[:1], [jnp.transpose(m_w_in[0])],
                           [jnp.transpose(v_w_in[0])], axis=1)
    out_g["w_in"], out_d["w_in"], out_m["w_in"], out_v["w_in"] = (jnp.transpose(t)[None] for t in res)
    res_mats = _adamw_halves(place, [weights[n][0] for n in mat_names], mine[1:4], theirs[1:4],
                             [m_in[n][0] for n in mat_names], [v_in[n][0] for n in mat_names], axis=0)
    for n, res in zip(mat_names, res_mats):
        out_g[n], out_d[n], out_m[n], out_v[n] = (t[None] for t in res)
    upd_rows = sum(weights[n].size for n in small_names) // 128
    pad_rows = -(-upd_rows // 8) * 8
    packed = [_pack_rows([t[n] for n in small_names], pad_rows) for t in (weights, grads, m_in, v_in)]
    d_s, m_s, v_s = _adamw(*packed)
    row = 0
    for n in small_names:
        shape = weights[n].shape
        rows = weights[n].size // 128
        out_g[n] = grads[n].reshape(shape)
        out_d[n], out_m[n], out_v[n] = (t[row:row + rows].reshape(shape) for t in (d_s, m_s, v_s))
        row += rows
    return (loss, dx[None], *[out_g[n] for n in names], *[out_d[n] for n in names],
            *[out_m[n] for n in names], *[out_v[n] for n in names])
```

```python
import functools
import math

import jax
import jax.numpy as jnp
from jax import lax
from jax.experimental import pallas as pl
from jax.experimental.pallas import tpu as pltpu

f32 = jnp.float32
bf16 = jnp.bfloat16

D = 1024
NMAIN = 8192
LRP = 128
RANK = 16
HA, GA, CA = 8, 128, 128
HB, DK, DV, CB = 4, 128, 256, 64
KEYB = HB * DK
EPS = 1e-6
LN_EPS = 1e-5
GATE_NORM = 16.0
QSCALE = DK ** -0.5
COL_U, COL_V, COL_ZA = 0, 1, 2
COL_Q, COL_K = 6, 7
COL_VB, COL_ZB = 4, 5
COL_GATES = 3
VMEM_LIMIT = 56 * 1024 * 1024
NCHIP = 4
WIN_SHARD = 2052
LR_COL = 6144
_ANY = pl.BlockSpec(memory_space=pl.ANY)

ADAM_LR, ADAM_B1, ADAM_B2, ADAM_EPS, ADAM_WD, ADAM_STEP = 0.001, 0.9, 0.999, 1e-08, 0.01, 10

_SQRT_HALF = 0.7071067811865476
_INV_SQRT_2PI = 0.3989422804014327


def _dot(a, b):
    return jnp.dot(a, b, preferred_element_type=f32)


def _dot_nt(a, b):
    return lax.dot_general(a, b, (((1,), (1,)), ((), ())), preferred_element_type=f32)


def _dot_tn(a, b):
    return lax.dot_general(a, b, (((0,), (0,)), ((), ())), preferred_element_type=f32)


def _dot_exact(a, b):
    return jnp.dot(a, b, preferred_element_type=f32, precision=lax.Precision.HIGHEST)


def _gelu(x):
    return 0.5 * x * (1.0 + lax.erf(x * _SQRT_HALF))


def _gelu_and_grad(x):
    cdf = 0.5 * (1.0 + lax.erf(x * _SQRT_HALF))
    return x * cdf, cdf + x * (jnp.exp(-0.5 * x * x) * _INV_SQRT_2PI)


def _sigmoid(x):
    return 0.5 * jnp.tanh(0.5 * x) + 0.5


def _params(sem):
    return pltpu.CompilerParams(dimension_semantics=sem, vmem_limit_bytes=VMEM_LIMIT)


def _resident(shape):
    nd = len(shape)
    return pl.BlockSpec(shape, lambda *_: (0,) * nd, pipeline_mode=pl.Buffered(1))


def _w_rows(c, tn):
    start = c * tn + (RANK if c * tn >= LR_COL else 0)
    return slice(start, start + tn)


LR_ROWS = slice(LR_COL, LR_COL + LRP)


def _forward(x, g0, w_t, ln_g, ln_b, w_s, b_sb, w_gate, b_gate, gla_g, riders=(), tm=256, tn=1024):
    T = x.shape[0]
    nsteps = T // tm
    n = len(riders)
    n_out = 7

    def body(x_ref, g_ref, w_ref, lg_ref, lb_ref, ws_ref, bs_ref, wg_ref, bg_ref, gg_ref, *rest):
        srcs, outs, dsts = rest[:n], rest[n:n + n_out], rest[n + n_out:2 * n + n_out]
        proj_ref, lr_ref, h_ref, a_ref, o_ref, ob_ref, st_ref = outs
        vln_s, state, la_s, *sems = rest[2 * n + n_out:]

        @pl.when(pl.program_id(0) == 0)
        def _():
            state[...] = jnp.zeros_like(state)
            if n:
                _gather_start(srcs, dsts, [False] * n, sems)

        xv = x_ref[...]
        r = lax.rsqrt(jnp.mean(xv * xv, axis=-1, keepdims=True) + EPS)
        h = (xv * r * g_ref[...]).astype(bf16)
        h_ref[...] = h
        def project(chunks):
            for c in chunks:
                proj_ref[:, c * tn:(c + 1) * tn] = _dot_nt(h, w_ref[_w_rows(c, tn), :])

        def cols(block, width):
            return proj_ref.at[:, block * width:(block + 1) * width]

        project(range(0, 3 * D // tn))
        _mixer_a_fwd_tile(cols(COL_U, D), cols(COL_V, D), cols(COL_ZA, D), lg_ref, lb_ref, ws_ref, bs_ref, a_ref, vln_s)
        lr_ref[...] = _dot_nt(h, w_ref[LR_ROWS, :])
        project(range(3 * D // tn, 6 * D // tn))
        _gla_fwd_tile(cols(COL_Q, KEYB), cols(COL_K, KEYB), cols(COL_VB, D), cols(COL_ZB, D), lr_ref, wg_ref, bg_ref,
                      gg_ref, o_ref, ob_ref, st_ref, state, la_s)
        project(range(6 * D // tn, NMAIN // tn))

        if n:
            @pl.when(pl.program_id(0) == nsteps - 1)
            def _():
                _gather_finish(srcs, dsts, [False] * n, sems)

    row = lambda width: pl.BlockSpec((tm, width), lambda i: (i, 0))
    return pl.pallas_call(
        body,
        grid=(nsteps,),
        in_specs=[row(D), _resident((1, D)), _resident((NMAIN + RANK, D)), _resident((1, D)), _resident((1, D)),
                  _resident((HA, CA, CA)), _resident((HA, CA, GA)),
                  _resident((LRP, KEYB)), _resident((1, KEYB)), _resident((1, DV))] + [_ANY] * n,
        out_specs=[row(NMAIN), row(LRP), row(D), row(D), row(D), row(D),
                   pl.BlockSpec((tm // CB, HB, DV, DK), lambda i: (i, 0, 0, 0))] + [_ANY] * n,
        out_shape=[jax.ShapeDtypeStruct((T, NMAIN), f32), jax.ShapeDtypeStruct((T, LRP), f32),
                   jax.ShapeDtypeStruct((T, D), bf16), jax.ShapeDtypeStruct((T, D), bf16),
                   jax.ShapeDtypeStruct((T, D), f32), jax.ShapeDtypeStruct((T, D), bf16),
                   jax.ShapeDtypeStruct((T // CB, HB, DV, DK), f32)] + _gathered_shapes(riders),
        scratch_shapes=[pltpu.VMEM((tm, D), bf16), pltpu.VMEM((HB, DV, DK), f32), pltpu.VMEM((tm, KEYB), f32)]
        + (_gather_scratch(riders) if n else []),
        compiler_params=_params(("arbitrary",)),
        name="forward",
    )(x, g0, w_t, ln_g, ln_b, w_s, b_sb, w_gate, b_gate, gla_g, *riders)


def _causal_mask():
    t = lax.broadcasted_iota(jnp.int32, (CA, CA), 0)
    s = lax.broadcasted_iota(jnp.int32, (CA, CA), 1)
    return s <= t


def _layernorm_parts(gv):
    mu = jnp.mean(gv, axis=-1, keepdims=True)
    xc = gv - mu
    rs = lax.rsqrt(jnp.mean(xc * xc, axis=-1, keepdims=True) + LN_EPS)
    return xc * rs, rs


def _mixer_a_fwd_tile(u_ref, v_ref, za_ref, lg_ref, lb_ref, ws_ref, bs_ref, a_ref, vln_s):
    tm = u_ref.shape[0]
    vhat, _ = _layernorm_parts(_gelu(v_ref[...]))
    vln_s[...] = (vhat * lg_ref[...] + lb_ref[...]).astype(bf16)
    mask = _causal_mask()
    for g in range(HA):
        wg = jnp.where(mask, ws_ref[g], 0.0).astype(bf16)
        cols = slice(g * GA, (g + 1) * GA)
        for c in range(tm // CA):
            rows = slice(c * CA, (c + 1) * CA)
            mixed = _dot(wg, vln_s[rows, cols]) + bs_ref[g]
            za = za_ref[rows, cols]
            a = _gelu(u_ref[rows, cols]) * mixed * (za * _sigmoid(za))
            a_ref[rows, cols] = a.astype(bf16)


def _mixer_a_bwd(proj, da, dproj, ln_g, ln_b, w_s, b_sb, riders=(), kinds=(), tm=256):
    T = proj.shape[0]
    nsteps = T // tm
    n = len(riders)

    def body(u_ref, v_ref, za_ref, da_ref, dp_in, lg_ref, lb_ref, ws_ref, bs_ref, *rest):
        srcs, (dp_ref, dws_ref, dbs_ref, dlg_ref, dlb_ref), dsts = rest[:n], rest[n:n + 5], rest[n + 5:2 * n + 5]
        vln_s, dvln_s, *sems = rest[2 * n + 5:]
        del dp_in
        i = pl.program_id(0)

        @pl.when(i == 0)
        def _():
            dws_ref[...] = jnp.zeros_like(dws_ref)
            dbs_ref[...] = jnp.zeros_like(dbs_ref)
            dlg_ref[...] = jnp.zeros_like(dlg_ref)
            dlb_ref[...] = jnp.zeros_like(dlb_ref)
            for cp in _halves_copies(srcs, dsts, kinds, sems) if n else []:
                cp.start()

        gv, gv_grad = _gelu_and_grad(v_ref[...])
        vhat, rs = _layernorm_parts(gv)
        vln_s[...] = (vhat * lg_ref[...] + lb_ref[...]).astype(bf16)
        mask = _causal_mask()
        for g in range(HA):
            wg = jnp.where(mask, ws_ref[g], 0.0).astype(bf16)
            cols = slice(g * GA, (g + 1) * GA)
            dw_acc = jnp.zeros((CA, CA), f32)
            db_acc = jnp.zeros((CA, 1), f32)
            for c in range(tm // CA):
                rows = slice(c * CA, (c + 1) * CA)
                vln = vln_s[rows, cols]
                mixed = _dot(wg, vln) + bs_ref[g]
                u = u_ref[rows, cols]
                za = za_ref[rows, cols]
                da_blk = da_ref[rows, cols]
                sg = _sigmoid(za)
                sz = za * sg
                gu, gu_grad = _gelu_and_grad(u)
                dp_ref[rows, cols] = (da_blk * mixed * sz * gu_grad).astype(bf16)
                dp_ref[rows, 2 * D + g * GA:2 * D + (g + 1) * GA] = (
                    da_blk * gu * mixed * (sg * (1.0 + za * (1.0 - sg)))).astype(bf16)
                dmixed = da_blk * gu * sz
                dmb = dmixed.astype(bf16)
                dvln_s[rows, cols] = _dot_tn(wg, dmb)
                dw_acc = dw_acc + _dot_nt(dmb, vln)
                db_acc = db_acc + jnp.sum(dmixed, axis=-1, keepdims=True)
            dws_ref[g] += dw_acc
            dbs_ref[g] += jnp.broadcast_to(db_acc, (CA, GA))

        dvln = dvln_s[...]
        dlg_ref[...] += jnp.sum(dvln * vhat, axis=0, keepdims=True)
        dlb_ref[...] += jnp.sum(dvln, axis=0, keepdims=True)
        dvhat = dvln * lg_ref[...]
        dgv = rs * (dvhat - jnp.mean(dvhat, axis=-1, keepdims=True)
                    - vhat * jnp.mean(dvhat * vhat, axis=-1, keepdims=True))
        dp_ref[:, D:2 * D] = (dgv * gv_grad).astype(bf16)

        @pl.when(i == nsteps - 1)
        def _():
            for g in range(HA):
                dws_ref[g] = jnp.where(mask, dws_ref[g], 0.0)
            for cp in _halves_copies(srcs, dsts, kinds, sems) if n else []:
                cp.wait()

    def col(cidx):
        return pl.BlockSpec((tm, D), lambda i, c=cidx: (i, c))

    return pl.pallas_call(
        body,
        grid=(nsteps,),
        in_specs=[col(COL_U), col(COL_V), col(COL_ZA), pl.BlockSpec((tm, D), lambda i: (i, 0)),
                  pl.BlockSpec(memory_space=pl.ANY),
                  _resident((1, D)), _resident((1, D)), _resident((HA, CA, CA)), _resident((HA, CA, GA))] + [_ANY] * n,
        out_specs=[pl.BlockSpec((tm, 3 * D), lambda i: (i, 0)),
                   _resident((HA, CA, CA)), _resident((HA, CA, GA)), _resident((1, D)), _resident((1, D))] + [_ANY] * n,
        out_shape=[jax.ShapeDtypeStruct(dproj.shape, dproj.dtype),
                   jax.ShapeDtypeStruct((HA, CA, CA), f32), jax.ShapeDtypeStruct((HA, CA, GA), f32),
                   jax.ShapeDtypeStruct((1, D), f32), jax.ShapeDtypeStruct((1, D), f32)] + _halves_shapes(riders, kinds),
        scratch_shapes=[pltpu.VMEM((tm, D), bf16), pltpu.VMEM((tm, D), f32)] + (_halves_sems(riders) if n else []),
        input_output_aliases={4: 0},
        compiler_params=_params(("arbitrary",)),
        name="mixer_a_bwd",
    )(proj, proj, proj, da, dproj, ln_g, ln_b, w_s, b_sb, *riders)


def _tri(n, upper):
    r = lax.broadcasted_iota(jnp.int32, (n, n), 0)
    c = lax.broadcasted_iota(jnp.int32, (n, n), 1)
    return jnp.where((c >= r) if upper else (c <= r), 1.0, 0.0).astype(f32)


def _chunk_tri(n, upper):
    r = lax.broadcasted_iota(jnp.int32, (n, n), 0)
    c = lax.broadcasted_iota(jnp.int32, (n, n), 1)
    shift = CB.bit_length() - 1
    same_chunk = jnp.right_shift(r, shift) == jnp.right_shift(c, shift)
    return jnp.where(same_chunk & ((c >= r) if upper else (c <= r)), 1.0, 0.0).astype(f32)


def _log_alpha(lr, wg, bg):
    logit = _dot(lr.astype(bf16), wg.astype(bf16)) + bg
    la = (jnp.minimum(logit, 0.0) - jnp.log1p(jnp.exp(-jnp.abs(logit)))) * (1.0 / GATE_NORM)
    return logit, la


def _gla_fwd_tile(q_ref, k_ref, v_ref, zb_ref, lr_ref, wg_ref, bg_ref, gg_ref, o_ref, ob_ref, st_ref, state, la_s):
    tm = q_ref.shape[0]
    _, la = _log_alpha(lr_ref[...], wg_ref[...], bg_ref[...])
    la_s[...] = _dot_exact(_chunk_tri(tm, upper=False), la)
    causal = _tri(CB, upper=False) > 0.5
    states = [state[hd] for hd in range(HB)]
    for c in range(tm // CB):
        rows = slice(c * CB, (c + 1) * CB)
        b = la_s[rows, :]
        bl = b[CB - 1:CB, :]
        bm = b[CB // 2 - 1:CB // 2, :]
        q = q_ref[rows, :] * QSCALE
        k = k_ref[rows, :]
        qi_all = (q * jnp.exp(b - bm)).astype(bf16)
        ki_all = (k * jnp.exp(bm - b)).astype(bf16)
        qe_all = (q * jnp.exp(b)).astype(bf16)
        ks_all = (k * jnp.exp(bl - b)).astype(bf16)
        e_l = jnp.exp(bl)
        for hd in range(HB):
            kc = slice(hd * DK, (hd + 1) * DK)
            vc = slice(hd * DV, (hd + 1) * DV)
            v = v_ref[rows, vc].astype(bf16)
            p = jnp.where(causal, _dot_nt(qi_all[:, kc], ki_all[:, kc]), 0.0).astype(bf16)
            s0 = states[hd]
            st_ref[c, hd] = s0
            o = _dot(p, v) + _dot_nt(qe_all[:, kc], s0.astype(bf16))
            states[hd] = s0 * e_l[:, kc] + _dot_tn(v, ks_all[:, kc])
            o_ref[rows, vc] = o
            ro = lax.rsqrt(jnp.mean(o * o, axis=-1, keepdims=True) + EPS)
            zb = zb_ref[rows, vc]
            ob_ref[rows, vc] = (o * ro * gg_ref[...] * (zb * _sigmoid(zb))).astype(bf16)
    for hd in range(HB):
        state[hd] = states[hd]


def _gla_bwd(proj, lr, o, states, dob, dproj, w_gate, b_gate, gla_g, riders=(), sends_what=(), tm=256):
    T = proj.shape[0]
    cpb = tm // CB
    nb = T // tm
    n = len(riders)

    def body(q_ref, k_ref, v_ref, zb_ref, lr_ref, o_ref, st_ref, dob_ref, dp_in, wg_ref, bg_ref, gg_ref, *rest):
        srcs, (dp_ref, dlr_ref, dwg_ref, dbg_ref, dgg_ref), dsts = rest[:n], rest[n:n + 5], rest[n + 5:2 * n + 5]
        dstate, la_s, dlogit_s, tail_s, *sems = rest[2 * n + 5:]
        del dp_in
        step = pl.program_id(0)

        @pl.when(step == 0)
        def _():
            dstate[...] = jnp.zeros_like(dstate)
            dwg_ref[...] = jnp.zeros_like(dwg_ref)
            dbg_ref[...] = jnp.zeros_like(dbg_ref)
            dgg_ref[...] = jnp.zeros_like(dgg_ref)
            for cp in _exchange_copies(srcs, dsts, sends_what, sems) if n else []:
                cp.start()

        lr_v = lr_ref[...]
        logit, la = _log_alpha(lr_v, wg_ref[...], bg_ref[...])
        la_s[...] = _dot_exact(_chunk_tri(tm, upper=False), la)
        causal = _tri(CB, upper=False) > 0.5
        gg = gg_ref[...]
        dgg_acc = jnp.zeros((1, DV), f32)
        dstates = [dstate[hd] for hd in range(HB)]
        for c in reversed(range(cpb)):
            rows = slice(c * CB, (c + 1) * CB)
            b = la_s[rows, :]
            bl = b[CB - 1:CB, :]
            bm = b[CB // 2 - 1:CB // 2, :]
            eb_all, eqm_all, ekm_all = jnp.exp(b), jnp.exp(b - bm), jnp.exp(bm - b)
            eks_all, el_all = jnp.exp(bl - b), jnp.exp(bl)
            q_all = q_ref[rows, :] * QSCALE
            k_all = k_ref[rows, :]
            qi_all = (q_all * eqm_all).astype(bf16)
            ki_all = (k_all * ekm_all).astype(bf16)
            qe_all = (q_all * eb_all).astype(bf16)
            ksf_all = k_all * eks_all
            ks_all = ksf_all.astype(bf16)
            for hd in range(HB):
                kc = slice(hd * DK, (hd + 1) * DK)
                vc = slice(hd * DV, (hd + 1) * DV)
                o_h = o_ref[rows, vc]
                ro = lax.rsqrt(jnp.mean(o_h * o_h, axis=-1, keepdims=True) + EPS)
                ohat = o_h * ro
                zb = zb_ref[rows, vc]
                sg = _sigmoid(zb)
                dob_h = dob_ref[rows, vc]
                don = dob_h * (zb * sg)
                dp_ref[rows, 2 * D + hd * DV:2 * D + (hd + 1) * DV] = (
                    dob_h * ohat * gg * (sg * (1.0 + zb * (1.0 - sg)))).astype(bf16)
                dgg_acc = dgg_acc + jnp.sum(don * ohat, axis=0, keepdims=True)
                dohat = don * gg
                do = (ro * (dohat - ohat * jnp.mean(dohat * ohat, axis=-1, keepdims=True))).astype(bf16)
                e_b, e_qm, e_km, e_ks, e_l = eb_all[:, kc], eqm_all[:, kc], ekm_all[:, kc], eks_all[:, kc], el_all[:, kc]
                q, k, ks_f = q_all[:, kc], k_all[:, kc], ksf_all[:, kc]
                qi, ki, qe, ks = qi_all[:, kc], ki_all[:, kc], qe_all[:, kc], ks_all[:, kc]
                v = v_ref[rows, vc].astype(bf16)
                p = jnp.where(causal, _dot_nt(qi, ki), 0.0).astype(bf16)
                s0 = st_ref[c, hd]
                ds = dstates[hd]
                ds_b = ds.astype(bf16)
                dv = _dot_tn(p, do) + _dot_nt(ks, ds_b)
                dpm = jnp.where(causal, _dot_nt(do, v), 0.0).astype(bf16)
                dqi = _dot(dpm, ki)
                dki = _dot_tn(dpm, qi)
                dqe = _dot(do, s0.astype(bf16))
                dks = _dot(v, ds_b)
                dq_s = dqi * e_qm + dqe * e_b
                dk = dki * e_km + dks * e_ks
                tail = (jnp.sum(dks * ks_f, axis=0, keepdims=True)
                        + e_l * jnp.sum(ds * s0, axis=0, keepdims=True))
                dstates[hd] = _dot_tn(do, qe) + ds * e_l
                dp_ref[rows, kc] = (dq_s * QSCALE).astype(bf16)
                dp_ref[rows, KEYB + hd * DK:KEYB + (hd + 1) * DK] = dk.astype(bf16)
                dp_ref[rows, D + hd * DV:D + (hd + 1) * DV] = dv.astype(bf16)
                dlogit_s[rows, kc] = dq_s * q - dk * k
                tail_s[rows, kc] = jnp.broadcast_to(tail, (CB, DK))
        for hd in range(HB):
            dstate[hd] = dstates[hd]
        dgg_ref[...] += dgg_acc
        dg = _dot_exact(_chunk_tri(tm, upper=True), dlogit_s[...]) + tail_s[...]
        dlogit = dg * (1.0 / GATE_NORM) * _sigmoid(-logit)
        dbg_ref[...] += jnp.sum(dlogit, axis=0, keepdims=True)
        dlb = dlogit.astype(bf16)
        dlr_ref[...] = _dot_nt(dlb, wg_ref[...].astype(bf16)).astype(bf16)
        dwg_ref[...] += _dot_tn(lr_v.astype(bf16), dlb)

        if n:
            @pl.when(step == nb - 1)
            def _():
                for cp in _exchange_copies(srcs, dsts, sends_what, sems):
                    cp.wait()

    def rev(cidx):
        return lambda i, c=cidx: (nb - 1 - i, c)

    return pl.pallas_call(
        body,
        grid=(nb,),
        in_specs=[pl.BlockSpec((tm, KEYB), rev(COL_Q)),
                  pl.BlockSpec((tm, KEYB), rev(COL_K)),
                  pl.BlockSpec((tm, D), rev(COL_VB)),
                  pl.BlockSpec((tm, D), rev(COL_ZB)),
                  pl.BlockSpec((tm, LRP), rev(0)),
                  pl.BlockSpec((tm, D), rev(0)),
                  pl.BlockSpec((cpb, HB, DV, DK), lambda i: (nb - 1 - i, 0, 0, 0)),
                  pl.BlockSpec((tm, D), rev(0)),
                  pl.BlockSpec(memory_space=pl.ANY),
                  _resident((LRP, KEYB)), _resident((1, KEYB)), _resident((1, DV))] + [_ANY] * n,
        out_specs=[pl.BlockSpec((tm, 3 * D), rev(1)),
                   pl.BlockSpec((tm, LRP), rev(0)),
                   _resident((LRP, KEYB)), _resident((1, KEYB)), _resident((1, DV))] + [_ANY] * n,
        out_shape=[jax.ShapeDtypeStruct(dproj.shape, dproj.dtype),
                   jax.ShapeDtypeStruct((T, LRP), bf16),
                   jax.ShapeDtypeStruct((LRP, KEYB), f32), jax.ShapeDtypeStruct((1, KEYB), f32),
                   jax.ShapeDtypeStruct((1, DV), f32)] + _exchange_shapes(riders, sends_what),
        scratch_shapes=[pltpu.VMEM((HB, DV, DK), f32)] + [pltpu.VMEM((tm, KEYB), f32)] * 3
        + (_exchange_sems(n) if n else []),
        input_output_aliases={8: 0},
        compiler_params=_params(("arbitrary",)),
        name="gla_bwd",
    )(proj, proj, proj, proj, lr, o, states, dob, dproj, w_gate, b_gate, gla_g, *riders)


def _merge_fwd_bwd(x, tgt, proj, a, ob, w_a, w_b, w_o, g_f, tm=256):
    T = x.shape[0]

    def body(x_ref, t_ref, gt_ref, a_ref, ob_ref, wa_ref, wb_ref, wo_ref, gf_ref,
             dp_ref, dy_ref, da_ref, dob_ref, dwa_ref, dwb_ref, dwo_ref, dgf_ref, loss_ref):
        @pl.when(pl.program_id(0) == 0)
        def _():
            dwa_ref[...] = jnp.zeros_like(dwa_ref)
            dwb_ref[...] = jnp.zeros_like(dwb_ref)
            dwo_ref[...] = jnp.zeros_like(dwo_ref)
            dgf_ref[...] = jnp.zeros_like(dgf_ref)
            loss_ref[...] = jnp.zeros_like(loss_ref)

        ga = _sigmoid(gt_ref[:, :D])
        gb = _sigmoid(gt_ref[:, D:])
        a_v = a_ref[...]
        ob_v = ob_ref[...]
        pa = _dot(a_v, wa_ref[...])
        pb = _dot(ob_v, wb_ref[...])
        mb = (ga * pa + gb * pb).astype(bf16)
        y = x_ref[...] + _dot(mb, wo_ref[...])
        r1 = lax.rsqrt(jnp.mean(y * y, axis=-1, keepdims=True) + EPS)
        yhat = y * r1
        gf = gf_ref[...]
        err = yhat * gf - t_ref[...]
        loss_ref[...] += jnp.sum(err * err, axis=0, keepdims=True) * (0.5 / D)
        dout = err * (1.0 / D)
        dgf_ref[...] += jnp.sum(dout * yhat, axis=0, keepdims=True)
        dyn = dout * gf
        dy = r1 * (dyn - yhat * jnp.mean(dyn * yhat, axis=-1, keepdims=True))
        dy_ref[...] = dy
        dyb = dy.astype(bf16)
        dwo_ref[...] += _dot_tn(mb, dyb)
        dm = _dot_nt(dyb, wo_ref[...])
        dpa = (dm * ga).astype(bf16)
        dpb = (dm * gb).astype(bf16)
        dp_ref[:, :D] = (dm * pa * ga * (1.0 - ga)).astype(bf16)
        dp_ref[:, D:] = (dm * pb * gb * (1.0 - gb)).astype(bf16)
        dwa_ref[...] += _dot_tn(a_v, dpa)
        dwb_ref[...] += _dot_tn(ob_v, dpb)
        da_ref[...] = _dot_nt(dpa, wa_ref[...])
        dob_ref[...] = _dot_nt(dpb, wb_ref[...])

    row = lambda: pl.BlockSpec((tm, D), lambda i: (i, 0))
    return pl.pallas_call(
        body,
        grid=(T // tm,),
        in_specs=[row(), row(), pl.BlockSpec((tm, 2 * D), lambda i: (i, COL_GATES)), row(), row(),
                  _resident((D, D)), _resident((D, D)), _resident((D, D)), _resident((1, D))],
        out_specs=[pl.BlockSpec((tm, 2 * D), lambda i: (i, COL_GATES)), row(), row(), row(),
                   _resident((D, D)), _resident((D, D)), _resident((D, D)), _resident((1, D)), _resident((1, D))],
        out_shape=[jax.ShapeDtypeStruct((T, NMAIN), bf16),
                   jax.ShapeDtypeStruct((T, D), f32), jax.ShapeDtypeStruct((T, D), f32),
                   jax.ShapeDtypeStruct((T, D), f32),
                   jax.ShapeDtypeStruct((D, D), f32), jax.ShapeDtypeStruct((D, D), f32),
                   jax.ShapeDtypeStruct((D, D), f32),
                   jax.ShapeDtypeStruct((1, D), f32), jax.ShapeDtypeStruct((1, D), f32)],
        compiler_params=_params(("arbitrary",)),
        name="merge_fwd_bwd",
    )(x, tgt, proj, a, ob, w_a, w_b, w_o, g_f)


def _dx_bwd(x, dy, dproj, dlr, g0, w_t, riders=(), sends_what=(), tm=256):
    T = x.shape[0]
    nsteps = T // tm
    n = len(riders)

    def body(x_ref, dy_ref, dp_ref, dl_ref, g_ref, w_ref, *rest):
        srcs, (dx_ref, dg_ref), dsts, sems = rest[:n], rest[n:n + 2], rest[n + 2:2 * n + 2], rest[2 * n + 2:]

        @pl.when(pl.program_id(0) == 0)
        def _():
            dg_ref[...] = jnp.zeros_like(dg_ref)
            for cp in _exchange_copies(srcs, dsts, sends_what, sems) if n else []:
                cp.start()

        xv = x_ref[...]
        r = lax.rsqrt(jnp.mean(xv * xv, axis=-1, keepdims=True) + EPS)
        xhat = xv * r
        dh = _dot(dl_ref[...], w_ref[LR_ROWS, :])
        for c in range(NMAIN // D):
            dh = dh + _dot(dp_ref[:, c * D:(c + 1) * D], w_ref[_w_rows(c, D), :])
        dg_ref[...] += jnp.sum(dh * xhat, axis=0, keepdims=True)
        t = dh * g_ref[...]
        dx_ref[...] = dy_ref[...] + r * (t - xhat * jnp.mean(t * xhat, axis=-1, keepdims=True))

        if n:
            @pl.when(pl.program_id(0) == nsteps - 1)
            def _():
                for cp in _exchange_copies(srcs, dsts, sends_what, sems):
                    cp.wait()

    row = lambda: pl.BlockSpec((tm, D), lambda i: (i, 0))
    return pl.pallas_call(
        body,
        grid=(nsteps,),
        in_specs=[row(), row(), pl.BlockSpec((tm, NMAIN), lambda i: (i, 0)),
                  pl.BlockSpec((tm, LRP), lambda i: (i, 0)),
                  _resident((1, D)), _resident((NMAIN + RANK, D))] + [_ANY] * n,
        out_specs=[row(), _resident((1, D))] + [_ANY] * n,
        out_shape=[jax.ShapeDtypeStruct((T, D), f32), jax.ShapeDtypeStruct((1, D), f32)]
        + _exchange_shapes(riders, sends_what),
        scratch_shapes=_exchange_sems(n) if n else [],
        compiler_params=_params(("arbitrary",)),
        name="dx_bwd",
    )(x, dy, dproj, dlr, g0, w_t, *riders)


def _dw_in(h, dproj, dlr, pair_dtype=None, tm=2048, tn=1024):
    T = h.shape[0]
    tm = min(tm, T)
    nj, nk = NMAIN // tn, T // tm
    lr_tile = LR_COL // tn
    pair = pair_dtype is not None
    hd = D // 2

    def body(h_ref, dp_ref, dl_ref, out_ref, acc, lr_acc, sems, lr_sem, *more):
        j, k = pl.program_id(0), pl.program_id(1)
        slot = j % 2

        def tile_row(jj):
            return pl.multiple_of(jj * tn + jnp.where(jj >= lr_tile, RANK, 0), 8)

        if pair:
            land, lr_land, part_buf, lr_part, swap_send, swap_recv = more
            c = lax.axis_index("c")
            mine = pl.ds(pl.multiple_of(c * hd, 128), hd)
            other = pl.ds(pl.multiple_of((1 - c) * hd, 128), hd)

            def tile_swap(jj, s):
                return _remote(acc.at[s, :, other], land.at[jj], swap_send.at[jj], swap_recv.at[jj], _sibling())

            def lr_swap():
                return _remote(lr_acc.at[pl.ds(0, RANK), other], lr_land, swap_send.at[nj], swap_recv.at[nj], _sibling())

            def tile_out(jj, s):
                return pltpu.make_async_copy(part_buf.at[s], out_ref.at[pl.ds(tile_row(jj), tn)], sems.at[s])

            def finish_tile(jj, s):
                tile_swap(jj, s).wait()
                part_buf[s] = (acc[s, :, mine] + land[jj]).astype(pair_dtype)
                tile_out(jj, s).start()

            lr_out = pltpu.make_async_copy(lr_part, out_ref.at[pl.ds(LR_COL, RANK)], lr_sem)
        else:
            def tile_out(jj, s):
                return pltpu.make_async_copy(acc.at[s], out_ref.at[pl.ds(tile_row(jj), tn)], sems.at[s])

            lr_out = pltpu.make_async_copy(lr_acc.at[pl.ds(0, RANK)], out_ref.at[pl.ds(LR_COL, RANK)], lr_sem)

        @pl.when(j == 0)
        def _():
            @pl.when(k == 0)
            def _():
                lr_acc[...] = jnp.zeros_like(lr_acc)

            lr_acc[...] += _dot_tn(dl_ref[...], h_ref[...])

            @pl.when(k == nk - 1)
            def _():
                if pair:
                    lr_swap().start()
                else:
                    lr_out.start()

        @pl.when(k == 0)
        def _():
            acc[slot] = jnp.zeros((tn, D), f32)

        acc[slot] += _dot_tn(dp_ref[...], h_ref[...])

        @pl.when(k == nk - 1)
        def _():
            if pair:
                tile_swap(j, slot).start()

                @pl.when(j >= 3)
                def _():
                    tile_out(j - 3, 1 - slot).wait()

                @pl.when(j >= 1)
                def _():
                    finish_tile(j - 1, 1 - slot)

                @pl.when(j == nj - 1)
                def _():
                    tile_out(j - 2, slot).wait()
                    finish_tile(j, slot)
                    lr_swap().wait()
                    lr_part[...] = (lr_acc[0:RANK, mine] + lr_land[...]).astype(pair_dtype)
                    lr_out.start()
                    tile_out(j - 1, 1 - slot).wait()
                    tile_out(j, slot).wait()
                    lr_out.wait()
            else:
                tile_out(j, slot).start()

                @pl.when(j > 0)
                def _():
                    tile_out(j - 1, 1 - slot).wait()

                @pl.when(j == nj - 1)
                def _():
                    tile_out(j, slot).wait()
                    lr_out.wait()

    pair_scratch = [pltpu.VMEM((nj, tn, hd), f32), pltpu.VMEM((RANK, hd), f32), pltpu.VMEM((2, tn, hd), pair_dtype),
                    pltpu.VMEM((RANK, hd), pair_dtype)] + [pltpu.SemaphoreType.DMA((nj + 1,))] * 2 if pair else []
    return pl.pallas_call(
        body,
        grid=(nj, nk),
        in_specs=[pl.BlockSpec((tm, D), lambda j, k: (k, 0)), pl.BlockSpec((tm, tn), lambda j, k: (k, j)),
                  pl.BlockSpec((tm, LRP), lambda j, k: (k, 0))],
        out_specs=_ANY,
        out_shape=jax.ShapeDtypeStruct((NMAIN + RANK, hd), pair_dtype) if pair
        else jax.ShapeDtypeStruct((NMAIN + RANK, D), f32),
        scratch_shapes=[pltpu.VMEM((2, tn, D), f32), pltpu.VMEM((LRP, D), f32),
                        pltpu.SemaphoreType.DMA((2,)), pltpu.SemaphoreType.DMA] + pair_scratch,
        compiler_params=_params(("arbitrary", "arbitrary")),
        name="dw_in",
    )(h, dproj, dlr)


MESH = pl.DeviceIdType.MESH


def _place():
    x, y, c = lax.axis_index("x"), lax.axis_index("y"), lax.axis_index("c")
    others = [(1 - x, y), (x, 1 - y), (1 - x, 1 - y)]
    return x, y, c, 2 * x + y, others


def _sibling():
    return lax.axis_index("x"), lax.axis_index("y"), 1 - lax.axis_index("c")


def _remote(src, dst, send_sem, recv_sem, to):
    return pltpu.make_async_remote_copy(src_ref=src, dst_ref=dst, send_sem=send_sem, recv_sem=recv_sem,
                                        device_id=to, device_id_type=MESH)


def _half(ref, e, by_columns):
    if not by_columns:
        return ref.at[e]
    hw = ref.shape[-1] // 2
    return ref.at[:, pl.ds(pl.multiple_of(e * hw, 128), hw)]


RELAY_ROWS = 1024


def _gather_win(shard, small):
    h, w = shard.shape
    part_a, part_b = pl.ds(0, RELAY_ROWS), pl.ds(RELAY_ROWS, h - RELAY_ROWS)

    def body(src, small_src, dst, small_dst, send_sems, recv_sems, relay_send, relay_recv, pass_send, pass_recv, own_sems,
             small_send, small_recv, small_own, stage, small_stage):
        x, y, c, me, others = _place()
        (to_x, to_y, _), sibling = others, (x, y, 1 - c)
        j_x, j_y, j_d = (2 * cx + cy for cx, cy in others)
        cols = pl.ds(pl.multiple_of(c * (w // 2), 128), w // 2)
        theirs = pl.ds(pl.multiple_of((1 - c) * (w // 2), 128), w // 2)

        def mine(j, rows=pl.ds(0, h)):
            return dst.at[j, rows, cols]

        small_sends = [_remote(small_src, small_dst.at[me], small_send.at[k], small_recv.at[k], (*to, c))
                       for k, to in enumerate(others)]
        for cp in small_sends:
            cp.start()
        small_in = pltpu.make_async_copy(small_src, small_stage, small_own.at[0])
        small_out = pltpu.make_async_copy(small_stage, small_dst.at[me], small_own.at[1])
        small_in.start()

        to_stage = pltpu.make_async_copy(src, stage, own_sems.at[0])
        to_slot = pltpu.make_async_copy(stage, dst.at[me], own_sems.at[1])
        sends = [_remote(src.at[:, cols], mine(me), send_sems.at[k], recv_sems.at[k], (*to, c))
                 for k, to in enumerate((to_x, to_y))]
        for cp in sends:
            cp.start()
        to_stage.start()
        relays = [_remote(mine(j_x, part_a), mine(j_x, part_a), relay_send.at[0], relay_recv.at[0], (*to_y, c)),
                  _remote(mine(j_y, part_b), mine(j_y, part_b), relay_send.at[1], relay_recv.at[1], (*to_x, c))]
        landed = [mine(j_x), mine(j_y), mine(j_d, part_a), mine(j_d, part_b)]
        passes = [_remote(place, place, pass_send.at[k], pass_recv.at[k], sibling) for k, place in enumerate(landed)]
        for k in range(2):
            _remote(src.at[:, cols], landed[k], send_sems.at[k], recv_sems.at[k], sibling).wait_recv()
            relays[k].start()
            passes[k].start()
        to_stage.wait()
        to_slot.start()
        for k in range(2):
            _remote(landed[2 + k], landed[2 + k], relay_send.at[k], relay_recv.at[k], sibling).wait_recv()
            passes[2 + k].start()
        for k, place in enumerate([(j_x, pl.ds(0, h)), (j_y, pl.ds(0, h)), (j_d, part_a), (j_d, part_b)]):
            got = dst.at[place[0], place[1], theirs]
            _remote(got, got, pass_send.at[k], pass_recv.at[k], sibling).wait_recv()
        for cp in sends + relays + passes:
            cp.wait_send()
        to_slot.wait()
        small_in.wait()
        small_out.start()
        for k, (cx, cy) in enumerate(others):
            _remote(small_src, small_dst.at[2 * cx + cy], small_send.at[k], small_recv.at[k], sibling).wait_recv()
        for cp in small_sends:
            cp.wait_send()
        small_out.wait()

    return pl.pallas_call(
        body,
        in_specs=[_ANY] * 2,
        out_specs=[_ANY] * 2,
        out_shape=[jax.ShapeDtypeStruct((NCHIP, h, w), shard.dtype),
                   jax.ShapeDtypeStruct((NCHIP,) + small.shape, small.dtype)],
        scratch_shapes=[pltpu.SemaphoreType.DMA((2,))] * 4 + [pltpu.SemaphoreType.DMA((4,))] * 2
        + [pltpu.SemaphoreType.DMA((2,))] + [pltpu.SemaphoreType.DMA((3,))] * 2 + [pltpu.SemaphoreType.DMA((2,))]
        + [pltpu.VMEM((h, w), shard.dtype), pltpu.VMEM(small.shape, small.dtype)],
        compiler_params=pltpu.CompilerParams(vmem_limit_bytes=VMEM_LIMIT),
        name="gather_win",
    )(shard, small)


def _gathered_shapes(shards):
    return [jax.ShapeDtypeStruct((NCHIP,) + s.shape, s.dtype) for s in shards]


def _gather_scratch(shards):
    n = len(shards)
    return ([pltpu.SemaphoreType.DMA((3, n))] * 4 + [pltpu.SemaphoreType.DMA((2, n))]
            + [pltpu.VMEM(s.shape, s.dtype) for s in shards])


def _own_to_stage(srcs, scratch):
    own_sems, stages = scratch[4], scratch[5:]
    return [pltpu.make_async_copy(srcs[a], stages[a], own_sems.at[0, a]) for a in range(len(srcs))]


def _own_to_slot(dsts, scratch):
    own_sems, stages = scratch[4], scratch[5:]
    me = _place()[3]
    return [pltpu.make_async_copy(stages[a], dsts[a].at[me], own_sems.at[1, a]) for a in range(len(dsts))]


def _gather_copies(srcs, dsts, by_columns, sems, sends_only):
    n = len(srcs)
    send_sems, recv_sems, pass_send, pass_recv = sems[:4]
    x, y, c, me, others = _place()
    sibling = (x, y, 1 - c)

    def src(a, e):
        return _half(srcs[a], e, by_columns[a])

    def dst(a, j, e):
        return _half(dsts[a].at[j], e, by_columns[a])

    sends, arrivals, passes, passed = [], [], [], []
    for k, (cx, cy) in enumerate(others):
        j = 2 * cx + cy
        for a in range(n):
            sends.append(_remote(src(a, c), dst(a, me, c), send_sems.at[k, a], recv_sems.at[k, a], (cx, cy, c)))
            if not sends_only:
                arrivals.append(_remote(src(a, c), dst(a, j, c), send_sems.at[k, a], recv_sems.at[k, a], (cx, cy, c)))
                passes.append(_remote(dst(a, j, c), dst(a, j, c), pass_send.at[k, a], pass_recv.at[k, a], sibling))
                passed.append(_remote(src(a, c), dst(a, j, 1 - c), pass_send.at[k, a], pass_recv.at[k, a], sibling))
    return sends, arrivals, passes, passed


def _gather_start(srcs, dsts, by_columns, scratch):
    for cp in _gather_copies(srcs, dsts, by_columns, scratch, sends_only=True)[0]:
        cp.start()
    for cp in _own_to_stage(srcs, scratch):
        cp.start()


def _gather_finish(srcs, dsts, by_columns, scratch):
    for cp in _own_to_stage(srcs, scratch):
        cp.wait()
    own = _own_to_slot(dsts, scratch)
    for cp in own:
        cp.start()
    sends, arrivals, passes, passed = _gather_copies(srcs, dsts, by_columns, scratch, sends_only=False)
    for arrival, cp in zip(arrivals, passes):
        arrival.wait_recv()
        cp.start()
    for arrival in passed:
        arrival.wait_recv()
    for cp in sends + passes:
        cp.wait_send()
    for cp in own:
        cp.wait()


HALF_FIRST, CHIP_FIRST, BY_COLUMNS = "half_first", "chip_first", "by_columns"


def _sibling_halves(bufs, kinds):
    n = len(bufs)

    def body(*refs):
        cps = _halves_copies(refs[:n], refs[n:2 * n], kinds, refs[2 * n:])
        for cp in cps:
            cp.start()
        for cp in cps:
            cp.wait()

    return pl.pallas_call(
        body,
        in_specs=[_ANY] * n,
        out_specs=[_ANY] * n,
        out_shape=_halves_shapes(bufs, kinds),
        scratch_shapes=_halves_sems(bufs),
        name="sibling_halves",
    )(*bufs)


def _halves_shapes(bufs, kinds):
    def landed(b, kind):
        if kind == HALF_FIRST:
            return b.shape[1:]
        if kind == CHIP_FIRST:
            return (b.shape[0],) + b.shape[2:]
        return b.shape[:2] + (b.shape[2] // 2,)

    return [jax.ShapeDtypeStruct(landed(b, kind), b.dtype) for b, kind in zip(bufs, kinds)]


def _halves_sems(bufs):
    return [pltpu.SemaphoreType.DMA((len(bufs), NCHIP))] * 2


def _halves_copies(srcs, dsts, kinds, sems):
    send_sems, recv_sems = sems
    x, y, c, _, _ = _place()
    cps = []
    for a, kind in enumerate(kinds):
        if kind == HALF_FIRST:
            cps.append(_remote(srcs[a].at[1 - c], dsts[a], send_sems.at[a, 0], recv_sems.at[a, 0], (x, y, 1 - c)))
        else:
            cps += [_remote(_half(srcs[a].at[j], 1 - c, kind == BY_COLUMNS), dsts[a].at[j],
                            send_sems.at[a, j], recv_sems.at[a, j], (x, y, 1 - c)) for j in range(srcs[a].shape[0])]
    return cps


TO_ITS_CHIP, TO_EVERY_CHIP, ROWS_TO_ITS_CHIP = "to_its_chip", "to_every_chip", "rows_to_its_chip"
PIECE_STEP = 2048
PIECE_ROWS = 2064


def _chip_exchange(parts, sends_what):
    n = len(parts)

    def body(*refs):
        cps = _exchange_copies(refs[:n], refs[n:2 * n], sends_what, refs[2 * n:])
        for cp in cps:
            cp.start()
        for cp in cps:
            cp.wait()

    return pl.pallas_call(
        body,
        in_specs=[_ANY] * n,
        out_specs=[_ANY] * n,
        out_shape=_exchange_shapes(parts, sends_what),
        scratch_shapes=_exchange_sems(n),
        name="chip_exchange",
    )(*parts)


def _exchange_shapes(parts, sends_what):
    def landed(p, what):
        return (3, PIECE_ROWS, p.shape[1]) if what == ROWS_TO_ITS_CHIP else (3,) + p.shape[1:]

    return [jax.ShapeDtypeStruct(landed(p, what), p.dtype) for p, what in zip(parts, sends_what)]


def _exchange_sems(n):
    return [pltpu.SemaphoreType.DMA((3, n))] * 2


def _exchange_copies(srcs, dsts, sends_what, sems):
    send_sems, recv_sems = sems
    x, y, c, me, others = _place()

    def part(a, j):
        if sends_what[a] == ROWS_TO_ITS_CHIP:
            return srcs[a].at[pl.ds(pl.multiple_of(j * PIECE_STEP, PIECE_STEP), PIECE_ROWS)]
        return srcs[a].at[j if sends_what[a] == TO_ITS_CHIP else 0]

    return [_remote(part(a, 2 * cx + cy), dsts[a].at[k], send_sems.at[k, a], recv_sems.at[k, a], (cx, cy, c))
            for k, (cx, cy) in enumerate(others) for a in range(len(srcs))]


def _sibling_swap(halves):
    n = len(halves)

    def body(*refs):
        srcs, dsts = refs[:n], refs[n:2 * n]
        send_sems, recv_sems = refs[2 * n:]
        x, y, c, _, _ = _place()
        cps = [_remote(srcs[a], dsts[a], send_sems.at[a], recv_sems.at[a], (x, y, 1 - c)) for a in range(n)]
        for cp in cps:
            cp.start()
        for cp in cps:
            cp.wait()

    return pl.pallas_call(
        body,
        in_specs=[_ANY] * n,
        out_specs=[_ANY] * n,
        out_shape=[jax.ShapeDtypeStruct(s.shape, s.dtype) for s in halves],
        scratch_shapes=[pltpu.SemaphoreType.DMA((n,))] * 2,
        name="sibling_swap",
    )(*halves)


def _tile(h, w, operands=5):
    if h % 128 == 0:
        return 128, w
    budget = VMEM_LIMIT * 3 // 4 // (2 * operands * 4)
    tw = w
    while h * tw > budget and tw % 256 == 0:
        tw //= 2
    return h, tw


def _pair_sum(place, bufs, kind, gots, out_dtype):
    m = len(bufs)
    nj, h, w = gots[0].shape
    th, tw = _tile(h, w, operands=3 * m)
    nq = w // tw

    def body(p_ref, *refs):
        del p_ref
        for a_ref, b_ref, o_ref in zip(refs[:m], refs[m:2 * m], refs[2 * m:]):
            o_ref[...] = (a_ref[...] + b_ref[...]).astype(out_dtype)

    if kind == HALF_FIRST:
        mine = pl.BlockSpec((None, None, th, tw), lambda j, r, q, p: (p[0], j, r, q))
    elif kind == CHIP_FIRST:
        mine = pl.BlockSpec((None, None, th, tw), lambda j, r, q, p: (j, p[0], r, q))
    else:
        mine = pl.BlockSpec((None, th, tw), lambda j, r, q, p: (j, r, p[0] * nq + q))
    landed = pl.BlockSpec((None, th, tw), lambda j, r, q, p: (j, r, q))
    return pl.pallas_call(
        body,
        grid_spec=pltpu.PrefetchScalarGridSpec(
            num_scalar_prefetch=1,
            grid=(nj, h // th, w // tw),
            in_specs=[mine] * m + [landed] * m,
            out_specs=[landed] * m,
        ),
        out_shape=[jax.ShapeDtypeStruct((nj, h, w), out_dtype)] * m,
        compiler_params=_params(("parallel", "parallel", "parallel")),
        name="pair_sum",
    )(place, *bufs, *gots)


def _chip_sum(place, parts, slots):
    m = len(parts)
    nj, h, w = parts[0].shape
    th, tw = _tile(h, w, operands=5 * m)

    def body(p_ref, *refs):
        me = p_ref[1]
        for own_ref, s_ref, o_ref in zip(refs[:m], refs[m:2 * m], refs[2 * m:]):
            own = own_ref[...].astype(f32)
            by_flip = {2: s_ref[0].astype(f32), 1: s_ref[1].astype(f32), 3: s_ref[2].astype(f32)}
            acc = None
            for j in range(NCHIP):
                flip = me ^ j
                term = jnp.where(flip == 0, own,
                                 jnp.where(flip == 2, by_flip[2], jnp.where(flip == 1, by_flip[1], by_flip[3])))
                acc = term if acc is None else acc + term
            o_ref[...] = acc

    return pl.pallas_call(
        body,
        grid_spec=pltpu.PrefetchScalarGridSpec(
            num_scalar_prefetch=1,
            grid=(h // th, w // tw),
            in_specs=[pl.BlockSpec((None, th, tw), lambda r, q, p: (p[1] if nj == NCHIP else 0, r, q))] * m
            + [pl.BlockSpec((3, th, tw), lambda r, q, p: (0, r, q))] * m,
            out_specs=[pl.BlockSpec((th, tw), lambda r, q, p: (r, q))] * m,
        ),
        out_shape=[jax.ShapeDtypeStruct((h, w), f32)] * m,
        compiler_params=_params(("parallel", "parallel")),
        name="chip_sum",
    )(place, *parts, *slots)


def _adamw_math(w, g, m, v):
    nm = ADAM_B1 * m + (1.0 - ADAM_B1) * g
    nv = ADAM_B2 * v + (1.0 - ADAM_B2) * (g * g)
    m_hat = nm / (1.0 - ADAM_B1 ** ADAM_STEP)
    v_hat = nv / (1.0 - ADAM_B2 ** ADAM_STEP)
    return -ADAM_LR * (m_hat / (jnp.sqrt(v_hat) + ADAM_EPS) + ADAM_WD * w), nm, nv


def _adamw(w, g, m, v):
    rows, width = w.shape
    th, tw = _tile(rows, width, operands=7)

    def body(w_ref, g_ref, m_ref, v_ref, d_ref, nm_ref, nv_ref):
        d_ref[...], nm_ref[...], nv_ref[...] = _adamw_math(w_ref[...], g_ref[...], m_ref[...], v_ref[...])

    spec = pl.BlockSpec((th, tw), lambda r, q: (r, q))
    return pl.pallas_call(
        body,
        grid=(rows // th, width // tw),
        in_specs=[spec] * 4,
        out_specs=[spec] * 3,
        out_shape=[jax.ShapeDtypeStruct((rows, width), f32)] * 3,
        compiler_params=_params(("parallel", "parallel")),
        name="adamw",
    )(w, g, m, v)


def _adamw_halves(place, ws, mines, gots, ms, vs, axis):
    k = len(ws)
    rows, width = ws[0].shape
    h, hw = mines[0].shape
    th, tw = _tile(h, hw, operands=10 * k)
    nr, nq = h // th, hw // tw

    def body(p_ref, *refs):
        ins, outs = refs[:5 * k], refs[5 * k:]
        for i in range(k):
            w_ref, a_ref, b_ref, m_ref, v_ref = ins[i::k]
            g_ref, d_ref, nm_ref, nv_ref = outs[i::k]
            g = jnp.where(pl.program_id(0) == p_ref[0], a_ref[...], b_ref[...])
            g_ref[...] = g
            d_ref[...], nm_ref[...], nv_ref[...] = _adamw_math(w_ref[...], g, m_ref[...], v_ref[...])

    if axis == 0:
        full = pl.BlockSpec((th, tw), lambda e, r, q, p: (e * nr + r, q))
    else:
        full = pl.BlockSpec((th, tw), lambda e, r, q, p: (r, e * nq + q))
    half = pl.BlockSpec((th, tw), lambda e, r, q, p: (r, q))
    res = pl.pallas_call(
        body,
        grid_spec=pltpu.PrefetchScalarGridSpec(
            num_scalar_prefetch=1,
            grid=(2, nr, nq),
            in_specs=[full] * k + [half] * (2 * k) + [full] * (2 * k),
            out_specs=[full] * (4 * k),
        ),
        out_shape=[jax.ShapeDtypeStruct((rows, width), f32)] * (4 * k),
        compiler_params=_params(("parallel", "parallel", "parallel")),
        name="adamw_halves",
    )(place, *ws, *mines, *gots, *ms, *vs)
    return [res[i::k] for i in range(k)]


_SMALL = (("norm_g", 8), ("ln_v_g", 8), ("ln_v_b", 8), ("w_spatial", 1024), ("b_spatial", 8), ("b_gate_up", 4),
          ("gla_norm_g", 2), ("final_norm_g", 8), ("w_gate_up", 64), ("loss", 8))
_SMALL_ROWS = 1152


def _pack_rows(arrays, rows):
    flat = jnp.concatenate([a.reshape(-1, 128) for a in arrays], axis=0)
    return jnp.pad(flat, ((0, rows - flat.shape[0]), (0, 0)))


def kernel(x, norm_g, w_in, ln_v_g, ln_v_b, w_spatial, b_spatial, w_gate_up, b_gate_up, gla_norm_g, w_branch_a, w_branch_b, w_out, final_norm_g, loss_target, m_norm_g, m_w_in, m_ln_v_g, m_ln_v_b, m_w_spatial, m_b_spatial, m_w_gate_up, m_b_gate_up, m_gla_norm_g, m_w_branch_a, m_w_branch_b, m_w_out, m_final_norm_g, v_norm_g, v_w_in, v_ln_v_g, v_ln_v_b, v_w_spatial, v_b_spatial, v_w_gate_up, v_b_gate_up, v_gla_norm_g, v_w_branch_a, v_w_branch_b, v_w_out, v_final_norm_g):
    chip = 2 * lax.axis_index("x") + lax.axis_index("y")
    core = lax.axis_index("c")
    place = jnp.stack([core, chip]).astype(jnp.int32)
    mat_names = ("w_branch_a", "w_branch_b", "w_out")

    wt_shard = jnp.transpose(w_in[0]).astype(bf16)
    mats = [w[0].astype(bf16).reshape(2, D // NCHIP // 2, D) for w in (w_branch_a, w_branch_b, w_out)]
    gate_sh = w_gate_up[0].reshape(2, RANK // 2, 128)
    g_win, g_gate = _gather_win(wt_shard, gate_sh)
    w_t = g_win.reshape(NCHIP * WIN_SHARD, D)
    w_gate = jnp.transpose(g_gate.reshape(NCHIP, RANK, 128), (1, 0, 2)).reshape(RANK, KEYB)
    w_gate = jnp.pad(w_gate, ((0, LRP - RANK), (0, 0)))
    b_sb = jnp.broadcast_to(b_spatial[0][:, :, None], (HA, CA, GA))
    xs, tgt = x[0], loss_target[0]

    proj, lr, h, a, o, ob, states, g_a, g_b, g_o = _forward(xs, norm_g, w_t, ln_v_g, ln_v_b, w_spatial[0], b_sb, w_gate,
                                                           b_gate_up, gla_norm_g, riders=mats)
    w_a, w_b, w_o = (g.reshape(D, D) for g in (g_a, g_b, g_o))
    dproj, dy, da, dob, dwa, dwb, dwo, dgf, loss_cols = _merge_fwd_bwd(xs, tgt, proj, a, ob, w_a, w_b, w_o,
                                                                       final_norm_g.reshape(1, D))
    b_mats = [t.reshape(NCHIP, 2, D // NCHIP // 2, D) for t in (dwa, dwb, dwo)]
    dproj, dws, dbs, dlg, dlb, *got_mats = _mixer_a_bwd(proj, da, dproj, ln_v_g, ln_v_b, w_spatial[0], b_sb,
                                                        riders=b_mats, kinds=[CHIP_FIRST] * 3)
    part_mats = _pair_sum(place, b_mats, CHIP_FIRST, got_mats, bf16)
    dproj, dlr, dwg, dbg, dgg, *slots_mats = _gla_bwd(proj, lr, o, states, dob, dproj, w_gate, b_gate_up, gla_norm_g,
                                                      riders=part_mats, sends_what=[TO_ITS_CHIP] * 3)
    part_win = _dw_in(h, dproj, dlr, pair_dtype=bf16)
    dx, dg0, slots_win = _dx_bwd(xs, dy, dproj, dlr, norm_g, w_t, riders=[part_win], sends_what=[ROWS_TO_ITS_CHIP])
    slots_big = [slots_win] + slots_mats
    small = _pack_rows([dg0, dlg, dlb, dws, dbs[:, :, 0], dbg, dgg, dgf, dwg[:RANK], loss_cols], _SMALL_ROWS)
    b_small = small.reshape(2, 1, _SMALL_ROWS // 2, 128)
    (got_small,) = _sibling_halves([b_small], [HALF_FIRST])
    (part_small,) = _pair_sum(place, [b_small], HALF_FIRST, [got_small], f32)
    (slots_small,) = _chip_exchange([part_small], [TO_EVERY_CHIP])
    own_win = lax.dynamic_slice_in_dim(part_win, chip * PIECE_STEP, PIECE_ROWS, axis=0)[None]
    mine = [*_chip_sum(place, [own_win], slots_big[:1]), *_chip_sum(place, part_mats, slots_big[1:]),
            *_chip_sum(place, [part_small], [slots_small])]
    theirs = list(_sibling_swap(mine))
    mine[0], theirs[0] = (lax.dynamic_slice_in_dim(t, (WIN_SHARD - PIECE_STEP) * chip, WIN_SHARD, axis=0)
                          for t in (mine[0], theirs[0]))

    g_small = jnp.where(core == 0, jnp.concatenate([mine[4], theirs[4]], axis=0),
                        jnp.concatenate([theirs[4], mine[4]], axis=0))
    grads = {}
    row = 0
    for name, rows in _SMALL:
        grads[name] = g_small[row:row + rows]
        row += rows
    loss = jnp.sum(grads["loss"])
    dwg_full = grads["w_gate_up"].reshape(RANK, KEYB)
    grads["w_gate_up"] = lax.dynamic_slice_in_dim(dwg_full, chip * 128, 128, axis=1)

    weights = dict(norm_g=norm_g, w_in=w_in, ln_v_g=ln_v_g, ln_v_b=ln_v_b, w_spatial=w_spatial, b_spatial=b_spatial,
                   w_gate_up=w_gate_up, b_gate_up=b_gate_up, gla_norm_g=gla_norm_g, w_branch_a=w_branch_a,
                   w_branch_b=w_branch_b, w_out=w_out, final_norm_g=final_norm_g)
    m_in = dict(norm_g=m_norm_g, w_in=m_w_in, ln_v_g=m_ln_v_g, ln_v_b=m_ln_v_b, w_spatial=m_w_spatial,
                b_spatial=m_b_spatial, w_gate_up=m_w_gate_up, b_gate_up=m_b_gate_up, gla_norm_g=m_gla_norm_g,
                w_branch_a=m_w_branch_a, w_branch_b=m_w_branch_b, w_out=m_w_out, final_norm_g=m_final_norm_g)
    v_in = dict(norm_g=v_norm_g, w_in=v_w_in, ln_v_g=v_ln_v_g, ln_v_b=v_ln_v_b, w_spatial=v_w_spatial,
                b_spatial=v_b_spatial, w_gate_up=v_w_gate_up, b_gate_up=v_b_gate_up, gla_norm_g=v_gla_norm_g,
                w_branch_a=v_w_branch_a, w_branch_b=v_w_branch_b, w_out=v_w_out, final_norm_g=v_final_norm_g)
    names = list(weights)
    small_names = [n for n in names if n != "w_in" and n not in mat_names]
    out_g, out_d, out_m, out_v = {}, {}, {}, {}
    (res,) = _adamw_halves(place, [jnp.transpose(w_in[0])], mine[:1], theirs[:1], [jnp.transpose(m_w_in[0])],
                           [jnp.transpose(v_w_in[0])], axis=1)
    out_g["w_in"], out_d["w_in"], out_m["w_in"], out_v["w_in"] = (jnp.transpose(t)[None] for t in res)
    res_mats = _adamw_halves(place, [weights[n][0] for n in mat_names], mine[1:4], theirs[1:4],
                             [m_in[n][0] for n in mat_names], [v_in[n][0] for n in mat_names], axis=0)
    for n, res in zip(mat_names, res_mats):
        out_g[n], out_d[n], out_m[n], out_v[n] = (t[None] for t in res)
    upd_rows = sum(weights[n].size for n in small_names) // 128
    pad_rows = -(-upd_rows // 8) * 8
    packed = [_pack_rows([t[n] for n in small_names], pad_rows) for t in (weights, grads, m_in, v_in)]
    d_s, m_s, v_s = _adamw(*packed)
    row = 0
    for n in small_names:
        shape = weights[n].shape
        rows = weights[n].size // 128
        out_g[n] = grads[n].reshape(shape)
        out_d[n], out_m[n], out_v[n] = (t[row:row + rows].reshape(shape) for t in (d_s, m_s, v_s))
        row += rows
    return (loss, dx[None], *[out_g[n] for n in names], *[out_d[n] for n in names],
            *[out_m[n] for n in names], *[out_v[n] for n in names])
```
